```python
import math
import jax, jax.numpy as jnp
from jax import lax
import numpy as np

D_MODEL = 2048
BATCH = 4
SEQ = 2048
DEPTH = 1
DEC_BATCH = 128
DEC_SEQ = 1
PAST_LEN = 16384
PAGE_SIZE = 128

SSD_HEADS = 16
SSD_HEAD_DIM = 64
SSD_WIDTH = SSD_HEADS * SSD_HEAD_DIM
SSD_GROUPS = 2
SSD_STATE = 128
SSD_CONV_DIM = SSD_WIDTH + 2 * SSD_GROUPS * SSD_STATE
GDN_HEADS = 8
GDN_HEAD_DIM = 128
GDN_WIDTH = GDN_HEADS * GDN_HEAD_DIM
GDN_CONV_DIM = 3 * GDN_WIDTH
MEM_TOKENS = 256
MEM_HEADS = 4
MEM_HEAD_DIM = 128
MEM_WIDTH = MEM_HEADS * MEM_HEAD_DIM

MIX_WIDTH = SSD_WIDTH + GDN_WIDTH + MEM_WIDTH
CONV_WIDTH = 4
CHUNK = 64
EPS = 1e-6
IN_SIZES = (SSD_CONV_DIM, SSD_HEADS, GDN_CONV_DIM, GDN_HEADS, GDN_HEADS, MEM_WIDTH, MIX_WIDTH)
IN_COLS = SSD_CONV_DIM + SSD_HEADS + GDN_CONV_DIM + 2 * GDN_HEADS + MEM_WIDTH + MIX_WIDTH

kernel_name = "hymba_ssd_gdn_memory_decoder_step"


def rmsnorm(x, w):
    xf = x.astype(jnp.float32)
    y = xf * lax.rsqrt(jnp.mean(xf * xf, axis=-1, keepdims=True) + EPS)
    return (y * w.astype(jnp.float32)).astype(x.dtype)


def headnorm(y):
    y = y.astype(jnp.float32)
    return y * lax.rsqrt(jnp.mean(y * y, axis=-1, keepdims=True) + EPS)


def l2norm(x):
    x = x.astype(jnp.float32)
    return x * lax.rsqrt(jnp.sum(x * x, axis=-1, keepdims=True) + EPS)


def pad_time(a, lp):
    pad = lp - a.shape[1]
    if pad == 0:
        return a
    widths = [(0, 0)] * a.ndim
    widths[1] = (0, pad)
    return jnp.pad(a, widths)


def causal_conv(prefix, u, w, b):
    L = u.shape[1]
    up = jnp.concatenate([prefix.astype(u.dtype), u], axis=1)
    out = up[:, 0:L] * w[0]
    for j in range(1, CONV_WIDTH):
        out = out + up[:, j:j + L] * w[j]
    if b is not None:
        out = out + b
    return jax.nn.silu(out), up[:, -(CONV_WIDTH - 1):]


def ssd_chunked(x, dt, A, Bm, Cm, D, h0):
    f32 = jnp.float32
    Bsz, L, H, P = x.shape
    G, N, R = SSD_GROUPS, SSD_STATE, H // SSD_GROUPS
    cs = min(CHUNK, L)
    nC = -(-L // cs)
    lp = nC * cs
    xc = pad_time(x.astype(f32), lp).reshape(Bsz, nC, cs, G, R, P)
    dtc = pad_time(dt.astype(f32), lp).reshape(Bsz, nC, cs, G, R)
    Bc = pad_time(Bm.astype(f32), lp).reshape(Bsz, nC, cs, G, N)
    Cc = pad_time(Cm.astype(f32), lp).reshape(Bsz, nC, cs, G, N)
    cum = jnp.cumsum(dtc * A.astype(f32).reshape(G, R), axis=2)
    cumT = jnp.moveaxis(cum, 2, -1)
    tril = jnp.tril(jnp.ones((cs, cs), bool))
    Lmat = jnp.exp(jnp.where(tril, cumT[..., :, None] - cumT[..., None, :], -jnp.inf))
    CB = jnp.einsum('bclgn,bcsgn->bcgls', Cc, Bc)
    scores = CB[:, :, :, None] * Lmat * jnp.moveaxis(dtc, 2, -1)[..., None, :]
    y = jnp.einsum('bcgrls,bcsgrp->bclgrp', scores, xc)
    cum_last = cum[:, :, -1]
    w_end = jnp.exp(cum_last[:, :, None] - cum) * dtc
    S = jnp.einsum('bcsgn,bcsgr,bcsgrp->bcgrpn', Bc, w_end, xc)

    def step(h, inp):
        dec, s = inp
        return h * jnp.exp(dec)[..., None, None] + s, h

    h0g = h0.astype(f32).reshape(Bsz, G, R, P, N)
    hT, h_prev = lax.scan(step, h0g, (jnp.moveaxis(cum_last, 1, 0), jnp.moveaxis(S, 1, 0)))
    h_prev = jnp.moveaxis(h_prev, 0, 1)
    y = y + jnp.einsum('bclgn,bcgrpn->bclgrp', Cc, h_prev) * jnp.exp(cum)[..., None]
    y = y + D.astype(f32).reshape(G, R)[:, :, None] * xc
    y = y.reshape(Bsz, lp, H, P)[:, :L]
    return y, hT.reshape(Bsz, H, P, N)


def gdn_chunked(q, k, v, g, beta, S0):
    f32 = jnp.float32
    Bsz, L, H, K = q.shape
    cs = min(CHUNK, L)
    nC = -(-L // cs)
    lp = nC * cs

    def blk(a):
        a = pad_time(a.astype(f32), lp)
        return jnp.moveaxis(a.reshape((Bsz, nC, cs, H) + a.shape[3:]), 3, 2)

    qc, kc, vc, gc, bc = blk(q), blk(k), blk(v), blk(g), blk(beta)
    Gc = jnp.cumsum(gc, axis=-1)
    tril = jnp.tril(jnp.ones((cs, cs), bool))
    eye = jnp.eye(cs, dtype=f32)
    decay = jnp.exp(jnp.where(tril, Gc[..., :, None] - Gc[..., None, :], -jnp.inf))
    kb = kc * bc[..., None]
    strict = tril & (eye == 0)
    Amat = jnp.where(strict, jnp.einsum('bnhik,bnhjk->bnhij', kb, kc) * decay, 0.0)
    T = lax.linalg.triangular_solve(eye + Amat, jnp.broadcast_to(eye, Amat.shape),
                                    left_side=True, lower=True, unit_diagonal=True)
    u = jnp.einsum('bnhij,bnhjv->bnhiv', T, vc * bc[..., None])
    wk = jnp.einsum('bnhij,bnhjk->bnhik', T, kb * jnp.exp(Gc)[..., None])
    attn = jnp.einsum('bnhik,bnhjk->bnhij', qc, kc) * decay
    qg = qc * jnp.exp(Gc)[..., None]
    g_last = Gc[..., -1]
    k_end = kc * jnp.exp(g_last[..., None] - Gc)[..., None]

    def step(S, inp):
        u_c, w_c, qg_c, a_c, ke_c, gl_c = inp
        v_new = u_c - jnp.einsum('bhik,bhkv->bhiv', w_c, S)
        o = jnp.einsum('bhik,bhkv->bhiv', qg_c, S) + jnp.einsum('bhij,bhjv->bhiv', a_c, v_new)
        S = S * jnp.exp(gl_c)[..., None, None] + jnp.einsum('bhjk,bhjv->bhkv', ke_c, v_new)
        return S, o

    mv = lambda a: jnp.moveaxis(a, 1, 0)
    ST, o = lax.scan(step, S0.astype(f32), (mv(u), mv(wk), mv(qg), mv(attn), mv(k_end), mv(g_last)))
    o = jnp.moveaxis(jnp.moveaxis(o, 0, 1), 2, 3)
    o = o.reshape(Bsz, lp, H, vc.shape[-1])[:, :L]
    return o, ST


def mixer(x, mem_k, mem_v, ssd_conv_prev, ssd_h, gdn_conv_prev, gdn_S,
          norm_w, w_in, ssd_conv_w, ssd_conv_b, ssd_dt_bias, ssd_A_log, ssd_D,
          gdn_conv_w, gdn_dt_bias, gdn_A_log, mix_norm_w, w_out):
    f32 = jnp.float32
    Bsz, L, _ = x.shape
    h = rmsnorm(x, norm_w)
    proj = jnp.einsum('bld,de->ble', h, w_in)
    splits = np.cumsum(IN_SIZES)[:-1].tolist()
    xbc_in, dt_raw, qkv_in, b_raw, a_raw, q_mem, z = jnp.split(proj, splits, axis=-1)

    xbc, ssd_conv_new = causal_conv(ssd_conv_prev, xbc_in, ssd_conv_w, ssd_conv_b)
    xs, Bm, Cm = jnp.split(xbc, [SSD_WIDTH, SSD_WIDTH + SSD_GROUPS * SSD_STATE], axis=-1)
    dt = jax.nn.softplus(dt_raw.astype(f32) + ssd_dt_bias.astype(f32))
    A = -jnp.exp(ssd_A_log.astype(f32))
    y_ssd, ssd_h_new = ssd_chunked(xs.reshape(Bsz, L, SSD_HEADS, SSD_HEAD_DIM), dt, A,
                                   Bm.reshape(Bsz, L, SSD_GROUPS, SSD_STATE),
                                   Cm.reshape(Bsz, L, SSD_GROUPS, SSD_STATE), ssd_D, ssd_h)

    qkv, gdn_conv_new = causal_conv(gdn_conv_prev, qkv_in, gdn_conv_w, None)
    q, k, v = jnp.split(qkv, [GDN_WIDTH, 2 * GDN_WIDTH], axis=-1)
    q = l2norm(q.reshape(Bsz, L, GDN_HEADS, GDN_HEAD_DIM)) * (GDN_HEAD_DIM ** -0.5)
    k = l2norm(k.reshape(Bsz, L, GDN_HEADS, GDN_HEAD_DIM))
    v = v.reshape(Bsz, L, GDN_HEADS, GDN_HEAD_DIM)
    beta = jax.nn.sigmoid(b_raw.astype(f32))
    g = -jnp.exp(gdn_A_log.astype(f32)) * jax.nn.softplus(a_raw.astype(f32) + gdn_dt_bias.astype(f32))
    y_gdn, gdn_S_new = gdn_chunked(q, k, v, g, beta, gdn_S)

    qm = q_mem.reshape(Bsz, L, MEM_HEADS, MEM_HEAD_DIM)
    s = jnp.einsum('blhd,bmhd->bhlm', qm, mem_k).astype(f32) * (MEM_HEAD_DIM ** -0.5)
    p = jax.nn.softmax(s, axis=-1)
    y_mem = jnp.einsum('bhlm,bmhd->blhd', p, mem_v.astype(f32))

    y = jnp.concatenate([headnorm(y_ssd).reshape(Bsz, L, SSD_WIDTH),
                         headnorm(y_gdn).reshape(Bsz, L, GDN_WIDTH),
                         headnorm(y_mem).reshape(Bsz, L, MEM_WIDTH)], axis=-1)
    y = y * mix_norm_w.astype(f32) * jax.nn.silu(z.astype(f32))
    out = jnp.einsum('ble,ed->bld', y.astype(x.dtype), w_out)
    return (x + out).astype(x.dtype), ssd_conv_new, ssd_h_new, gdn_conv_new, gdn_S_new


def setup_inputs(seed: int = 0) -> dict:
    key = jax.random.key(seed)
    ks = jax.random.split(key, 24)
    f32 = jnp.float32

    def nrm(k, shape, s=1.0):
        return s * jax.random.normal(k, shape, f32)

    def unif(k, shape, lo, hi):
        return jax.random.uniform(k, shape, f32, lo, hi)

    def dt_bias(k, shape):
        dt = jnp.exp(unif(k, shape, math.log(1e-3), math.log(1e-1)))
        return dt + jnp.log(-jnp.expm1(-dt))

    return {
        "x_prompt": nrm(ks[0], (BATCH, SEQ, D_MODEL)),
        "x_sample": nrm(ks[1], (DEC_BATCH, DEC_SEQ, D_MODEL)),
        "mem_prompt": nrm(ks[2], (BATCH, MEM_TOKENS, D_MODEL)),
        "state_ssd_conv": nrm(ks[3], (DEPTH, DEC_BATCH, CONV_WIDTH - 1, SSD_CONV_DIM)),
        "state_ssd": nrm(ks[4], (DEPTH, DEC_BATCH, SSD_HEADS, SSD_HEAD_DIM, SSD_STATE), 0.1),
        "state_gdn_conv": nrm(ks[5], (DEPTH, DEC_BATCH, CONV_WIDTH - 1, GDN_CONV_DIM)),
        "state_gdn": nrm(ks[6], (DEPTH, DEC_BATCH, GDN_HEADS, GDN_HEAD_DIM, GDN_HEAD_DIM), 0.1),
        "cache_mem_k": nrm(ks[7], (DEPTH, DEC_BATCH, MEM_TOKENS, MEM_HEADS, MEM_HEAD_DIM)),
        "cache_mem_v": nrm(ks[8], (DEPTH, DEC_BATCH, MEM_TOKENS, MEM_HEADS, MEM_HEAD_DIM)),
        "norm_w": 1.0 + nrm(ks[9], (DEPTH, D_MODEL), 0.02),
        "w_in": nrm(ks[10], (DEPTH, D_MODEL, IN_COLS), D_MODEL ** -0.5),
        "ssd_conv_w": nrm(ks[11], (DEPTH, CONV_WIDTH, SSD_CONV_DIM), CONV_WIDTH ** -0.5),
        "ssd_conv_b": nrm(ks[12], (DEPTH, SSD_CONV_DIM), 0.02),
        "ssd_dt_bias": dt_bias(ks[13], (DEPTH, SSD_HEADS)),
        "ssd_A_log": jnp.log(unif(ks[14], (DEPTH, SSD_HEADS), 1.0, 16.0)),
        "ssd_D": 1.0 + nrm(ks[15], (DEPTH, SSD_HEADS), 0.1),
        "gdn_conv_w": nrm(ks[16], (DEPTH, CONV_WIDTH, GDN_CONV_DIM), CONV_WIDTH ** -0.5),
        "gdn_dt_bias": dt_bias(ks[17], (DEPTH, GDN_HEADS)),
        "gdn_A_log": jnp.log(unif(ks[18], (DEPTH, GDN_HEADS), 1.0, 16.0)),
        "mem_norm_w": 1.0 + nrm(ks[19], (DEPTH, D_MODEL), 0.02),
        "w_mem_kv": nrm(ks[20], (DEPTH, D_MODEL, 2 * MEM_WIDTH), D_MODEL ** -0.5),
        "mix_norm_w": 1.0 + nrm(ks[21], (DEPTH, MIX_WIDTH), 0.02),
        "w_out": nrm(ks[22], (DEPTH, MIX_WIDTH, D_MODEL), MIX_WIDTH ** -0.5),
        "final_norm_w": 1.0 + nrm(ks[23], (D_MODEL,), 0.02),
    }


def reference(x_prompt, x_sample, mem_prompt, state_ssd_conv, state_ssd, state_gdn_conv, state_gdn,
              cache_mem_k, cache_mem_v, norm_w, w_in, ssd_conv_w, ssd_conv_b, ssd_dt_bias, ssd_A_log,
              ssd_D, gdn_conv_w, gdn_dt_bias, gdn_A_log, mem_norm_w, w_mem_kv, mix_norm_w, w_out,
              final_norm_w):
    f32 = jnp.float32
    Bp = x_prompt.shape[0]
    xp, xs = x_prompt, x_sample
    p_sc, p_sh, p_gc, p_gs, p_mk, p_mv = [], [], [], [], [], []
    s_sc, s_sh, s_gc, s_gs = [], [], [], []
    for l in range(DEPTH):
        lp = (norm_w[l], w_in[l], ssd_conv_w[l], ssd_conv_b[l], ssd_dt_bias[l], ssd_A_log[l], ssd_D[l],
              gdn_conv_w[l], gdn_dt_bias[l], gdn_A_log[l], mix_norm_w[l], w_out[l])
        kv = jnp.einsum('bmd,de->bme', rmsnorm(mem_prompt, mem_norm_w[l]), w_mem_kv[l])
        mk, mv = jnp.split(kv, 2, axis=-1)
        mk = mk.reshape(Bp, MEM_TOKENS, MEM_HEADS, MEM_HEAD_DIM)
        mv = mv.reshape(Bp, MEM_TOKENS, MEM_HEADS, MEM_HEAD_DIM)
        xp, sc, sh, gc, gs = mixer(
            xp, mk, mv,
            jnp.zeros((Bp, CONV_WIDTH - 1, SSD_CONV_DIM), xp.dtype),
            jnp.zeros((Bp, SSD_HEADS, SSD_HEAD_DIM, SSD_STATE), f32),
            jnp.zeros((Bp, CONV_WIDTH - 1, GDN_CONV_DIM), xp.dtype),
            jnp.zeros((Bp, GDN_HEADS, GDN_HEAD_DIM, GDN_HEAD_DIM), f32), *lp)
        p_sc.append(sc); p_sh.append(sh); p_gc.append(gc); p_gs.append(gs); p_mk.append(mk); p_mv.append(mv)
        xs, sc, sh, gc, gs = mixer(xs, cache_mem_k[l], cache_mem_v[l], state_ssd_conv[l], state_ssd[l],
                                   state_gdn_conv[l], state_gdn[l], *lp)
        s_sc.append(sc); s_sh.append(sh); s_gc.append(gc); s_gs.append(gs)
    y_prompt = rmsnorm(xp, final_norm_w)
    y_sample = rmsnorm(xs, final_norm_w)
    p_ssd_conv = jnp.stack(p_sc); p_ssd = jnp.stack(p_sh)
    p_gdn_conv = jnp.stack(p_gc); p_gdn = jnp.stack(p_gs)
    p_mem_k = jnp.stack(p_mk); p_mem_v = jnp.stack(p_mv)
    s_ssd_conv = jnp.stack(s_sc); s_ssd = jnp.stack(s_sh)
    s_gdn_conv = jnp.stack(s_gc); s_gdn = jnp.stack(s_gs)
    return (y_prompt, y_sample, p_ssd_conv, p_ssd, p_gdn_conv, p_gdn, p_mem_k, p_mem_v,
            s_ssd_conv, s_ssd, s_gdn_conv, s_gdn)
```

```python
import functools

import numpy as np
import jax
import jax.numpy as jnp
from jax import lax
from jax.experimental import pallas as pl
from jax.experimental.pallas import tpu as pltpu

F32, BF16 = jnp.float32, jnp.bfloat16

D_MODEL = 2048
SSD_HEADS, SSD_P, SSD_GROUPS, SSD_N = 16, 64, 2, 128
SSD_W = SSD_HEADS * SSD_P
SSD_GW = SSD_W // SSD_GROUPS
SSD_CONV = SSD_W + 2 * SSD_GROUPS * SSD_N
GDN_HEADS, GDN_D = 8, 128
GDN_W = GDN_HEADS * GDN_D
GDN_CONV = 3 * GDN_W
MEM_TOKENS, MEM_HEADS, MEM_D = 256, 4, 128
MEM_W = MEM_HEADS * MEM_D
MIX_W = SSD_W + GDN_W + MEM_W
CONV_K = 4
CHUNK = 64
EPS = 1e-6

LANES = 128
SUBLANES = 8
VMEM_LIMIT = 56 * 1024 * 1024
PROJ_ROWS = 512
MEM_Q_ROWS = 256

COL_QKV = 0
COL_XBC = COL_QKV + GDN_CONV
COL_QMEM = COL_XBC + SSD_CONV
COL_Z = COL_QMEM + MEM_W
N_MAIN = COL_Z + MIX_W
SM_DT, SM_B, SM_A = 0, SSD_HEADS, SSD_HEADS + GDN_HEADS


def _dot(a, b):
    return jnp.dot(a, b, preferred_element_type=F32)


def _dot_nt(a, b):
    return lax.dot_general(a, b, (((1,), (1,)), ((), ())), preferred_element_type=F32)


def _dot_tn(a, b):
    return lax.dot_general(a, b, (((0,), (0,)), ((), ())), preferred_element_type=F32)


def _split(x, n):
    parts, r = [], x
    for i in range(n):
        p = r.astype(BF16)
        parts.append(p)
        if i + 1 < n:
            r = r - p.astype(F32)
    return parts


def _sel_left(sel, x, n=3):
    return functools.reduce(lambda a, b: a + b, [_dot(sel, p) for p in _split(x, n)])


def _sel_right(x, sel, n=3):
    return functools.reduce(lambda a, b: a + b, [_dot(p, sel) for p in _split(x, n)])


def _sel_right_nt(x, sel, n=3):
    return functools.reduce(lambda a, b: a + b, [_dot_nt(p, sel) for p in _split(x, n)])


def _transpose_sel(x, n=3):
    eye = _eye(LANES).astype(BF16)
    return functools.reduce(lambda a, b: a + b, [_dot_nt(eye, p) for p in _split(x, n)])


def _eye(n):
    return (lax.broadcasted_iota(jnp.int32, (n, n), 0) == lax.broadcasted_iota(jnp.int32, (n, n), 1)).astype(F32)


def _sigmoid(x):
    return 1.0 / (1.0 + jnp.exp(-x))


def _silu(x):
    return x * _sigmoid(x)


def _softplus(x):
    return jnp.maximum(x, 0.0) + jnp.log1p(jnp.exp(-jnp.abs(x)))


def _params(*sem):
    return pltpu.CompilerParams(dimension_semantics=sem, vmem_limit_bytes=VMEM_LIMIT)


def _row_tile(rows, preferred):
    return preferred if rows % preferred == 0 else rows


def _norm_matmul_kernel(x_ref, nw_ref, w_ref, ws_ref, o_ref, os_ref, h_ref):
    @pl.when(pl.program_id(1) == 0)
    def _():
        x = x_ref[...]
        ms = jnp.mean(x * x, axis=-1, keepdims=True)
        h = (x * lax.rsqrt(ms + EPS) * nw_ref[...]).astype(BF16)
        h_ref[...] = h
        os_ref[...] = _dot(h, ws_ref[...])

    o_ref[...] = _dot(h_ref[...], w_ref[...])


def _norm_matmul(x, nw, w, ws, tm, tn):
    m, k = x.shape
    n = w.shape[1]
    ns = ws.shape[1]
    return pl.pallas_call(
        _norm_matmul_kernel,
        grid=(m // tm, n // tn),
        in_specs=[
            pl.BlockSpec((tm, k), lambda i, j: (i, 0)),
            pl.BlockSpec((1, k), lambda i, j: (0, 0)),
            pl.BlockSpec((k, tn), lambda i, j: (0, j)),
            pl.BlockSpec((k, ns), lambda i, j: (0, 0)),
        ],
        out_specs=[
            pl.BlockSpec((tm, tn), lambda i, j: (i, j)),
            pl.BlockSpec((tm, ns), lambda i, j: (i, 0)),
        ],
        out_shape=[jax.ShapeDtypeStruct((m, n), F32), jax.ShapeDtypeStruct((m, ns), F32)],
        scratch_shapes=[pltpu.VMEM((tm, k), BF16)],
        compiler_params=_params("parallel", "arbitrary"),
        name="norm_matmul",
    )(x, nw, w, ws)


def _out_proj_kernel(y1_ref, y2_ref, y3_ref, w1_ref, w2_ref, w3_ref, x_ref, fw_ref, o_ref):
    acc = (_dot(y1_ref[...].astype(BF16), w1_ref[...]) + _dot(y2_ref[...].astype(BF16), w2_ref[...])
           + _dot(y3_ref[...].astype(BF16), w3_ref[...]))
    r = x_ref[...] + acc
    ms = jnp.mean(r * r, axis=-1, keepdims=True)
    o_ref[...] = r * lax.rsqrt(ms + EPS) * fw_ref[...]


def _out_proj(y1, y2, y3, w1, w2, w3, x, fw, tm):
    m, d = x.shape
    row = lambda i: (i, 0)
    whole = lambda i: (0, 0)
    return pl.pallas_call(
        _out_proj_kernel,
        grid=(m // tm,),
        in_specs=[
            pl.BlockSpec((tm, y1.shape[1]), row), pl.BlockSpec((tm, y2.shape[1]), row), pl.BlockSpec((tm, y3.shape[1]), row),
            pl.BlockSpec(w1.shape, whole), pl.BlockSpec(w2.shape, whole), pl.BlockSpec(w3.shape, whole),
            pl.BlockSpec((tm, d), row), pl.BlockSpec((1, d), whole),
        ],
        out_specs=pl.BlockSpec((tm, d), row),
        out_shape=jax.ShapeDtypeStruct((m, d), F32),
        compiler_params=_params("parallel"),
        name="out_proj",
    )(y1, y2, y3, w1, w2, w3, x, fw)


def _causal_conv_tile(u, ubuf_ref, cw_ref, bias):
    t = u.shape[0]
    ubuf_ref[SUBLANES:SUBLANES + t, :] = u
    acc = cw_ref[CONV_K - 1:CONV_K, :] * u
    if bias is not None:
        acc = acc + bias
    for j in range(CONV_K - 1):
        lo = SUBLANES - (CONV_K - 1) + j
        acc = acc + cw_ref[j:j + 1, :] * ubuf_ref[lo:lo + t, :]
    tail = ubuf_ref[t:t + SUBLANES, :]
    ubuf_ref[0:SUBLANES, :] = tail
    return _silu(acc), tail


def _head_norm_gate(y, msq, width, z, mixw):
    return y * lax.rsqrt(msq * (1.0 / width) + EPS) * mixw * _silu(z)


def _ssd_prompt_kernel(xbc_ref, sm_ref, z_ref, cw_ref, cb_ref, dtb_ref, alog_ref, dexp_ref, mixw_ref, e_ref,
                       y_ref, tail_ref, state_ref, ubuf_ref, h_ref):
    c = pl.program_id(1)
    t = xbc_ref.shape[0]

    @pl.when(c == 0)
    def _():
        ubuf_ref[0:SUBLANES, :] = jnp.zeros((SUBLANES, SSD_CONV), F32)
        h_ref[...] = jnp.zeros_like(h_ref)

    xbc, tail = _causal_conv_tile(xbc_ref[...], ubuf_ref, cw_ref, cb_ref[...])
    tail_ref[0] = tail
    xs = xbc[:, :SSD_W]
    e = e_ref[...]

    dt = _softplus(sm_ref[...] + dtb_ref[...])
    a = dt * (-jnp.exp(alog_ref[...]))
    ri = lax.broadcasted_iota(jnp.int32, (t, t), 0)
    ci = lax.broadcasted_iota(jnp.int32, (t, t), 1)
    causal = ri >= ci
    tril = causal.astype(BF16)
    cum = _sel_left(tril, a)
    cum_t = _transpose_sel(cum)
    ecum = jnp.exp(cum)
    wend = jnp.exp(cum[t - 1:t, :] - cum)
    dt_x = _sel_right(dt, e, 2)
    ecum_x = _sel_right(ecum, e, 2)
    wend_x = _sel_right(wend, e, 2)

    xdt = xs * dt_x
    xdt_b = xdt.astype(BF16)
    xw_b = (xdt * wend_x).astype(BF16)
    lane = lax.broadcasted_iota(jnp.int32, (t, LANES), 1)
    left = lane < SSD_P

    y_blocks = []
    for g in range(SSD_GROUPS):
        bg = xbc[:, SSD_W + g * SSD_N:SSD_W + (g + 1) * SSD_N].astype(BF16)
        cg = xbc[:, SSD_W + (SSD_GROUPS + g) * SSD_N:SSD_W + (SSD_GROUPS + g + 1) * SSD_N].astype(BF16)
        gs = slice(g * SSD_GW, (g + 1) * SSD_GW)
        hg = h_ref[:, gs]
        y_inter = _dot(cg, hg.astype(BF16)) * ecum_x[:, gs]
        cb = _dot_nt(cg, bg)
        for pr in range(SSD_GW // LANES):
            blk = g * (SSD_GW // LANES) + pr
            xb = xdt_b[:, blk * LANES:(blk + 1) * LANES]
            acc = None
            for half in range(2):
                h = 2 * blk + half
                diff = cum[:, h:h + 1] - cum_t[h:h + 1, :]
                lmat = jnp.where(causal, jnp.exp(jnp.minimum(diff, 0.0)), 0.0)
                s = (cb * lmat).astype(BF16)
                xh = jnp.where(left if half == 0 else jnp.logical_not(left), xb, jnp.zeros_like(xb))
                part = _dot(s, xh)
                acc = part if acc is None else acc + part
            y_blocks.append(acc + y_inter[:, pr * LANES:(pr + 1) * LANES])
        inc = _dot_tn(bg, xw_b[:, gs])
        h_ref[:, gs] = hg * ecum_x[t - 1:t, gs] + inc
    y = jnp.concatenate(y_blocks, axis=1) + dexp_ref[...] * xs

    msq = _sel_right(_sel_right_nt(y * y, e, 2), e, 2)
    y_ref[...] = _head_norm_gate(y, msq, SSD_P, z_ref[...], mixw_ref[...]).astype(BF16)

    @pl.when(c == pl.num_programs(1) - 1)
    def _():
        state_ref[0] = h_ref[...].T.reshape(SSD_HEADS, SSD_P, SSD_N)


def _ssd_prompt(proj, small, batch, cw, cb, dtb, alog, dexp, mixw, e):
    rows = proj.shape[0]
    nc = rows // batch // CHUNK
    row = lambda b, c: (b * nc + c, 0)
    whole = lambda b, c: (0, 0)
    return pl.pallas_call(
        _ssd_prompt_kernel,
        grid=(batch, nc),
        in_specs=[
            pl.BlockSpec((CHUNK, SSD_CONV), lambda b, c: (b * nc + c, COL_XBC // SSD_CONV)),
            pl.BlockSpec((CHUNK, LANES), row),
            pl.BlockSpec((CHUNK, SSD_W), lambda b, c: (b * nc + c, COL_Z // SSD_W)),
            pl.BlockSpec(cw.shape, whole), pl.BlockSpec(cb.shape, whole), pl.BlockSpec(dtb.shape, whole),
            pl.BlockSpec(alog.shape, whole), pl.BlockSpec(dexp.shape, whole), pl.BlockSpec(mixw.shape, whole),
            pl.BlockSpec(e.shape, whole),
        ],
        out_specs=[
            pl.BlockSpec((CHUNK, SSD_W), row),
            pl.BlockSpec((1, SUBLANES, SSD_CONV), lambda b, c: (b, 0, 0)),
            pl.BlockSpec((1, SSD_HEADS, SSD_P, SSD_N), lambda b, c: (b, 0, 0, 0)),
        ],
        out_shape=[
            jax.ShapeDtypeStruct((rows, SSD_W), BF16),
            jax.ShapeDtypeStruct((batch, SUBLANES, SSD_CONV), F32),
            jax.ShapeDtypeStruct((batch, SSD_HEADS, SSD_P, SSD_N), F32),
        ],
        scratch_shapes=[pltpu.VMEM((CHUNK + SUBLANES, SSD_CONV), F32), pltpu.VMEM((SSD_N, SSD_W), F32)],
        compiler_params=_params("parallel", "arbitrary"),
        name="ssd_prompt",
    )(proj, small, proj, cw, cb, dtb, alog, dexp, mixw, e)


def _unit_lower_inverse(a_strict, ri, ci):
    t = a_strict.shape[0]
    inv = _eye(t) - jnp.where((ri == ci + 1) & (ci % 2 == 0), a_strict, 0.0)
    s = 2
    while s < t:
        sel = (ri // (2 * s) == ci // (2 * s)) & ((ri // s) % 2 == 1) & ((ci // s) % 2 == 0)
        a_s = jnp.where(sel, a_strict, 0.0).astype(BF16)
        inv_b = inv.astype(BF16)
        inv = inv - _dot(_dot(inv_b, a_s).astype(BF16), inv_b)
        s *= 2
    return inv


def _gdn_prompt_kernel(qkv_ref, sm_ref, z_ref, cw_ref, gb_ref, galog_ref, mixw_ref,
                       y_ref, tail_ref, state_ref, ubuf_ref, s_ref):
    c = pl.program_id(1)
    t = qkv_ref.shape[0]

    @pl.when(c == 0)
    def _():
        ubuf_ref[0:SUBLANES, :] = jnp.zeros((SUBLANES, GDN_CONV), F32)
        s_ref[...] = jnp.zeros_like(s_ref)

    qkv, tail = _causal_conv_tile(qkv_ref[...], ubuf_ref, cw_ref, None)
    tail_ref[0] = tail

    sm = sm_ref[...]
    beta = _sigmoid(sm)
    g = -jnp.exp(galog_ref[...]) * _softplus(sm + gb_ref[...])
    ri = lax.broadcasted_iota(jnp.int32, (t, t), 0)
    ci = lax.broadcasted_iota(jnp.int32, (t, t), 1)
    causal = ri >= ci
    strict = ri > ci
    gc = _sel_left(causal.astype(BF16), g)
    gc_t = _transpose_sel(gc)
    eg = jnp.exp(gc)
    g_last = gc[t - 1:t, :]
    eend = jnp.exp(g_last - gc)
    elast = jnp.exp(g_last)

    for h in range(GDN_HEADS):
        hs = slice(h * GDN_D, (h + 1) * GDN_D)
        q = qkv[:, h * GDN_D:(h + 1) * GDN_D]
        k = qkv[:, GDN_W + h * GDN_D:GDN_W + (h + 1) * GDN_D]
        v = qkv[:, 2 * GDN_W + h * GDN_D:2 * GDN_W + (h + 1) * GDN_D]
        q = q * lax.rsqrt(jnp.sum(q * q, axis=-1, keepdims=True) + EPS) * (GDN_D ** -0.5)
        k = k * lax.rsqrt(jnp.sum(k * k, axis=-1, keepdims=True) + EPS)
        b_col = beta[:, SM_B + h:SM_B + h + 1]
        la = SM_A + h
        decay = jnp.where(causal, jnp.exp(jnp.minimum(gc[:, la:la + 1] - gc_t[la:la + 1, :], 0.0)), 0.0)
        eg_col = eg[:, la:la + 1]
        kb = k * b_col
        k_b, q_b, kb_b = k.astype(BF16), q.astype(BF16), kb.astype(BF16)
        a_strict = jnp.where(strict, _dot_nt(kb_b, k_b) * decay, 0.0)
        attn = (_dot_nt(q_b, k_b) * decay).astype(BF16)
        t_inv = _unit_lower_inverse(a_strict, ri, ci).astype(BF16)
        u = _dot(t_inv, (v * b_col).astype(BF16))
        wk = _dot(t_inv, (kb * eg_col).astype(BF16))

        s_old = s_ref[h]
        s_b = s_old.astype(BF16)
        v_new = u - _dot(wk.astype(BF16), s_b)
        v_new_b = v_new.astype(BF16)
        o = _dot((q * eg_col).astype(BF16), s_b) + _dot(attn, v_new_b)
        k_end = (k * eend[:, la:la + 1]).astype(BF16)
        s_ref[h] = s_old * elast[:, la:la + 1] + _dot_tn(k_end, v_new_b)

        msq = jnp.sum(o * o, axis=-1, keepdims=True)
        y_ref[:, hs] = _head_norm_gate(o, msq, GDN_D, z_ref[:, hs], mixw_ref[:, hs]).astype(BF16)

    @pl.when(c == pl.num_programs(1) - 1)
    def _():
        state_ref[0] = s_ref[...]


def _gdn_prompt(proj, small, batch, cw, gb, galog, mixw):
    rows = proj.shape[0]
    nc = rows // batch // CHUNK
    row = lambda b, c: (b * nc + c, 0)
    whole = lambda b, c: (0, 0)
    return pl.pallas_call(
        _gdn_prompt_kernel,
        grid=(batch, nc),
        in_specs=[
            pl.BlockSpec((CHUNK, GDN_CONV), lambda b, c: (b * nc + c, COL_QKV // GDN_CONV)),
            pl.BlockSpec((CHUNK, LANES), row),
            pl.BlockSpec((CHUNK, GDN_W), lambda b, c: (b * nc + c, (COL_Z + SSD_W) // GDN_W)),
            pl.BlockSpec(cw.shape, whole), pl.BlockSpec(gb.shape, whole), pl.BlockSpec(galog.shape, whole),
            pl.BlockSpec(mixw.shape, whole),
        ],
        out_specs=[
            pl.BlockSpec((CHUNK, GDN_W), row),
            pl.BlockSpec((1, SUBLANES, GDN_CONV), lambda b, c: (b, 0, 0)),
            pl.BlockSpec((1, GDN_HEADS, GDN_D, GDN_D), lambda b, c: (b, 0, 0, 0)),
        ],
        out_shape=[
            jax.ShapeDtypeStruct((rows, GDN_W), BF16),
            jax.ShapeDtypeStruct((batch, SUBLANES, GDN_CONV), F32),
            jax.ShapeDtypeStruct((batch, GDN_HEADS, GDN_D, GDN_D), F32),
        ],
        scratch_shapes=[pltpu.VMEM((CHUNK + SUBLANES, GDN_CONV), F32), pltpu.VMEM((GDN_HEADS, GDN_D, GDN_D), F32)],
        compiler_params=_params("parallel", "arbitrary"),
        name="gdn_prompt",
    )(proj, small, proj, cw, gb, galog, mixw)


def _mem_attention_head(q, k, v):
    s = _dot_nt(q.astype(BF16), k.astype(BF16)) * (MEM_D ** -0.5)
    e = jnp.exp(s - jnp.max(s, axis=-1, keepdims=True))
    p = e / jnp.sum(e, axis=-1, keepdims=True)
    return _dot(p.astype(BF16), v.astype(BF16))


def _mem_prompt_kernel(q_ref, k_ref, v_ref, z_ref, mixw_ref, y_ref):
    for h in range(MEM_HEADS):
        hs = slice(h * MEM_D, (h + 1) * MEM_D)
        o = _mem_attention_head(q_ref[:, hs], k_ref[:, hs], v_ref[:, hs])
        msq = jnp.sum(o * o, axis=-1, keepdims=True)
        y_ref[:, hs] = _head_norm_gate(o, msq, MEM_D, z_ref[:, hs], mixw_ref[:, hs]).astype(BF16)


def _mem_prompt(proj, kv, batch, mixw, tq):
    rows = proj.shape[0]
    nq = rows // batch // tq
    return pl.pallas_call(
        _mem_prompt_kernel,
        grid=(batch, nq),
        in_specs=[
            pl.BlockSpec((tq, MEM_W), lambda b, i: (b * nq + i, COL_QMEM // MEM_W)),
            pl.BlockSpec((MEM_TOKENS, MEM_W), lambda b, i: (b, 0)),
            pl.BlockSpec((MEM_TOKENS, MEM_W), lambda b, i: (b, 1)),
            pl.BlockSpec((tq, MEM_W), lambda b, i: (b * nq + i, (COL_Z + SSD_W + GDN_W) // MEM_W)),
            pl.BlockSpec(mixw.shape, lambda b, i: (0, 0)),
        ],
        out_specs=pl.BlockSpec((tq, MEM_W), lambda b, i: (b * nq + i, 0)),
        out_shape=jax.ShapeDtypeStruct((rows, MEM_W), BF16),
        compiler_params=_params("parallel", "parallel"),
        name="mem_prompt",
    )(proj, kv, kv, proj, mixw)


DEC_ROWS = SUBLANES


def _conv_step(u, st, cw_ref, bias, width):
    acc = cw_ref[CONV_K - 1:CONV_K, :] * u
    if bias is not None:
        acc = acc + bias
    for j in range(CONV_K - 1):
        acc = acc + cw_ref[j:j + 1, :] * st[:, j * width:(j + 1) * width]
    return _silu(acc), jnp.concatenate([st[:, width:], u], axis=1)


def _rows_to_columns(x):
    pad = jnp.zeros((LANES - x.shape[0], x.shape[1]), F32)
    return jnp.concatenate([x, pad], axis=0).T


def _pick_rows(parts):
    rid = lax.broadcasted_iota(jnp.int32, parts[0].shape, 0)
    out = parts[0]
    for i in range(1, len(parts)):
        out = jnp.where(rid == i, parts[i], out)
    return out


def _ssd_decode_kernel(xbc_ref, sm_ref, z_ref, cst_ref, st_ref, cw_ref, cb_ref, dtb_ref, alog_ref, dexp_ref,
                       mixw_ref, e_ref, en_ref, y_ref, cst_out_ref, st_out_ref):
    xbc, cst_new = _conv_step(xbc_ref[...], cst_ref[...], cw_ref, cb_ref[...], SSD_CONV)
    cst_out_ref[...] = cst_new
    xs = xbc[:, :SSD_W]
    e = e_ref[...]
    dt = _softplus(sm_ref[...] + dtb_ref[...])
    dec = jnp.exp(dt * (-jnp.exp(alog_ref[...])))
    xd_t = _rows_to_columns(xs * _sel_right(dt, e))
    dec_n = _sel_right(dec, en_ref[...])

    y_groups = []
    for g in range(SSD_GROUPS):
        b_g = xbc[:, SSD_W + g * SSD_N:SSD_W + (g + 1) * SSD_N]
        c_g = xbc[:, SSD_W + (SSD_GROUPS + g) * SSD_N:SSD_W + (SSD_GROUPS + g + 1) * SSD_N].astype(BF16)
        per_row = []
        for i in range(DEC_ROWS):
            new = []
            for r in range(SSD_HEADS // SSD_GROUPS):
                h = g * (SSD_HEADS // SSD_GROUPS) + r
                col = xd_t[h * SSD_P:(h + 1) * SSD_P, i:i + 1]
                hn = st_ref[i, h] * dec_n[i:i + 1, h * SSD_N:(h + 1) * SSD_N] + col * b_g[i:i + 1, :]
                st_out_ref[i, h] = hn
                new.append(hn)
            hg = jnp.concatenate(new, axis=0).astype(BF16)
            per_row.append(_dot_nt(c_g, hg))
        y_groups.append(_pick_rows(per_row))
    y = jnp.concatenate(y_groups, axis=1) + dexp_ref[...] * xs
    msq = _sel_right(_sel_right_nt(y * y, e, 2), e, 2)
    y_ref[...] = _head_norm_gate(y, msq, SSD_P, z_ref[...], mixw_ref[...])


def _ssd_decode(proj, small, cst, st, cw, cb, dtb, alog, dexp, mixw, e, en):
    rows = proj.shape[0]
    row = lambda i: (i, 0)
    whole = lambda i: (0, 0)
    return pl.pallas_call(
        _ssd_decode_kernel,
        grid=(rows // DEC_ROWS,),
        in_specs=[
            pl.BlockSpec((DEC_ROWS, SSD_CONV), lambda i: (i, COL_XBC // SSD_CONV)),
            pl.BlockSpec((DEC_ROWS, LANES), row),
            pl.BlockSpec((DEC_ROWS, SSD_W), lambda i: (i, COL_Z // SSD_W)),
            pl.BlockSpec((DEC_ROWS, cst.shape[1]), row),
            pl.BlockSpec((DEC_ROWS, SSD_HEADS, SSD_P, SSD_N), lambda i: (i, 0, 0, 0)),
            pl.BlockSpec(cw.shape, whole), pl.BlockSpec(cb.shape, whole), pl.BlockSpec(dtb.shape, whole),
            pl.BlockSpec(alog.shape, whole), pl.BlockSpec(dexp.shape, whole), pl.BlockSpec(mixw.shape, whole),
            pl.BlockSpec(e.shape, whole), pl.BlockSpec(en.shape, whole),
        ],
        out_specs=[
            pl.BlockSpec((DEC_ROWS, SSD_W), row),
            pl.BlockSpec((DEC_ROWS, cst.shape[1]), row),
            pl.BlockSpec((DEC_ROWS, SSD_HEADS, SSD_P, SSD_N), lambda i: (i, 0, 0, 0)),
        ],
        out_shape=[
            jax.ShapeDtypeStruct((rows, SSD_W), F32),
            jax.ShapeDtypeStruct(cst.shape, F32),
            jax.ShapeDtypeStruct(st.shape, F32),
        ],
        compiler_params=_params("parallel"),
        name="ssd_decode",
    )(proj, small, proj, cst, st, cw, cb, dtb, alog, dexp, mixw, e, en)


def _gdn_decode_kernel(qkv_ref, sm_ref, z_ref, cst_ref, st_ref, cw_ref, gb_ref, galog_ref, mixw_ref, en_ref,
                       y_ref, cst_out_ref, st_out_ref):
    qkv, cst_new = _conv_step(qkv_ref[...], cst_ref[...], cw_ref, None, GDN_CONV)
    cst_out_ref[...] = cst_new
    sm = sm_ref[...]
    beta = _sigmoid(sm)
    eg = jnp.exp(-jnp.exp(galog_ref[...]) * _softplus(sm + gb_ref[...]))
    eg_n = _sel_right(eg, en_ref[...])

    qs, ks = [], []
    for h in range(GDN_HEADS):
        q = qkv[:, h * GDN_D:(h + 1) * GDN_D]
        k = qkv[:, GDN_W + h * GDN_D:GDN_W + (h + 1) * GDN_D]
        qs.append(q * lax.rsqrt(jnp.sum(q * q, axis=-1, keepdims=True) + EPS) * (GDN_D ** -0.5))
        ks.append(k * lax.rsqrt(jnp.sum(k * k, axis=-1, keepdims=True) + EPS))
    k_t = _rows_to_columns(jnp.concatenate(ks, axis=1))

    for h in range(GDN_HEADS):
        hs = slice(h * GDN_D, (h + 1) * GDN_D)
        q, k = qs[h], ks[h]
        v = qkv[:, 2 * GDN_W + h * GDN_D:2 * GDN_W + (h + 1) * GDN_D]
        q_b, k_b = q.astype(BF16), k.astype(BF16)
        ks_rows, qs_rows = [], []
        for i in range(DEC_ROWS):
            s_b = st_ref[i, h].astype(BF16)
            ks_rows.append(_dot(k_b, s_b))
            qs_rows.append(_dot(q_b, s_b))
        k_s, q_s = _pick_rows(ks_rows), _pick_rows(qs_rows)
        eg_h = eg_n[:, hs]
        v_new = beta[:, SM_B + h:SM_B + h + 1] * (v - eg_h * k_s)
        o = eg_h * q_s + jnp.sum(q * k, axis=-1, keepdims=True) * v_new
        for i in range(DEC_ROWS):
            col = k_t[hs, i:i + 1]
            st_out_ref[i, h] = st_ref[i, h] * eg_h[i:i + 1, :] + col * v_new[i:i + 1, :]
        msq = jnp.sum(o * o, axis=-1, keepdims=True)
        y_ref[:, hs] = _head_norm_gate(o, msq, GDN_D, z_ref[:, hs], mixw_ref[:, hs])


def _gdn_decode(proj, small, cst, st, cw, gb, galog, mixw, en):
    rows = proj.shape[0]
    row = lambda i: (i, 0)
    whole = lambda i: (0, 0)
    return pl.pallas_call(
        _gdn_decode_kernel,
        grid=(rows // DEC_ROWS,),
        in_specs=[
            pl.BlockSpec((DEC_ROWS, GDN_CONV), lambda i: (i, COL_QKV // GDN_CONV)),
            pl.BlockSpec((DEC_ROWS, LANES), row),
            pl.BlockSpec((DEC_ROWS, GDN_W), lambda i: (i, (COL_Z + SSD_W) // GDN_W)),
            pl.BlockSpec((DEC_ROWS, cst.shape[1]), row),
            pl.BlockSpec((DEC_ROWS, GDN_HEADS, GDN_D, GDN_D), lambda i: (i, 0, 0, 0)),
            pl.BlockSpec(cw.shape, whole), pl.BlockSpec(gb.shape, whole), pl.BlockSpec(galog.shape, whole),
            pl.BlockSpec(mixw.shape, whole), pl.BlockSpec(en.shape, whole),
        ],
        out_specs=[
            pl.BlockSpec((DEC_ROWS, GDN_W), row),
            pl.BlockSpec((DEC_ROWS, cst.shape[1]), row),
            pl.BlockSpec((DEC_ROWS, GDN_HEADS, GDN_D, GDN_D), lambda i: (i, 0, 0, 0)),
        ],
        out_shape=[
            jax.ShapeDtypeStruct((rows, GDN_W), F32),
            jax.ShapeDtypeStruct(cst.shape, F32),
            jax.ShapeDtypeStruct(st.shape, F32),
        ],
        compiler_params=_params("parallel"),
        name="gdn_decode",
    )(proj, small, proj, cst, st, cw, gb, galog, mixw, en)


def _mem_decode_kernel(q_ref, k_ref, v_ref, z_ref, mixw_ref, y_ref):
    for h in range(MEM_HEADS):
        hs = slice(h * MEM_D, (h + 1) * MEM_D)
        q = q_ref[:, hs]
        o = _pick_rows([_mem_attention_head(q, k_ref[i, :, hs], v_ref[i, :, hs]) for i in range(DEC_ROWS)])
        msq = jnp.sum(o * o, axis=-1, keepdims=True)
        y_ref[:, hs] = _head_norm_gate(o, msq, MEM_D, z_ref[:, hs], mixw_ref[:, hs])


def _mem_decode(proj, mem_k, mem_v, mixw):
    rows = proj.shape[0]
    return pl.pallas_call(
        _mem_decode_kernel,
        grid=(rows // DEC_ROWS,),
        in_specs=[
            pl.BlockSpec((DEC_ROWS, MEM_W), lambda i: (i, COL_QMEM // MEM_W)),
            pl.BlockSpec((DEC_ROWS, MEM_TOKENS, MEM_W), lambda i: (i, 0, 0)),
            pl.BlockSpec((DEC_ROWS, MEM_TOKENS, MEM_W), lambda i: (i, 0, 0)),
            pl.BlockSpec((DEC_ROWS, MEM_W), lambda i: (i, (COL_Z + SSD_W + GDN_W) // MEM_W)),
            pl.BlockSpec(mixw.shape, lambda i: (0, 0)),
        ],
        out_specs=pl.BlockSpec((DEC_ROWS, MEM_W), lambda i: (i, 0)),
        out_shape=jax.ShapeDtypeStruct((rows, MEM_W), F32),
        compiler_params=_params("parallel"),
        name="mem_decode",
    )(proj, mem_k, mem_v, proj, mixw)


def _head_expander(heads, first_lane, width):
    m = np.zeros((LANES, heads * width), np.float32)
    for h in range(heads):
        m[first_lane + h, h * width:(h + 1) * width] = 1.0
    return jnp.asarray(m, BF16)


def _lane_row(vec, first_lane):
    return jnp.zeros((1, LANES), F32).at[0, first_lane:first_lane + vec.shape[0]].set(vec.astype(F32))


def kernel(x_prompt, x_sample, mem_prompt, state_ssd_conv, state_ssd, state_gdn_conv, state_gdn, cache_mem_k, cache_mem_v, norm_w, w_in, ssd_conv_w, ssd_conv_b, ssd_dt_bias, ssd_A_log, ssd_D, gdn_conv_w, gdn_dt_bias, gdn_A_log, mem_norm_w, w_mem_kv, mix_norm_w, w_out, final_norm_w):
    bp, seq, d = x_prompt.shape
    bs = x_sample.shape[0]
    assert (d, seq % CHUNK, bs % DEC_ROWS, norm_w.shape[0]) == (D_MODEL, 0, 0, 1)

    w = w_in[0]
    o_dt = SSD_CONV
    o_qkv = o_dt + SSD_HEADS
    o_b = o_qkv + GDN_CONV
    o_a = o_b + GDN_HEADS
    o_qm = o_a + GDN_HEADS
    o_z = o_qm + MEM_W
    w_main = jnp.concatenate([w[:, o_qkv:o_b], w[:, :o_dt], w[:, o_qm:o_z], w[:, o_z:]], axis=1).astype(BF16)
    w_small = jnp.concatenate(
        [w[:, o_dt:o_qkv], w[:, o_b:o_a], w[:, o_a:o_qm], jnp.zeros((d, LANES - SSD_HEADS - 2 * GDN_HEADS), F32)], axis=1
    ).astype(BF16)
    wo = w_out[0].astype(BF16)
    wo1, wo2, wo3 = wo[:SSD_W], wo[SSD_W:SSD_W + GDN_W], wo[SSD_W + GDN_W:]
    nw = norm_w[0][None, :]
    mixw = mix_norm_w[0][None, :]
    mixw1, mixw2, mixw3 = mixw[:, :SSD_W], mixw[:, SSD_W:SSD_W + GDN_W], mixw[:, SSD_W + GDN_W:]
    fw = final_norm_w[None, :]
    ssd_dtb = _lane_row(ssd_dt_bias[0], SM_DT)
    ssd_alog = _lane_row(ssd_A_log[0], SM_DT)
    ssd_dexp = jnp.repeat(ssd_D[0].astype(F32), SSD_P)[None, :]
    gdn_b = _lane_row(gdn_dt_bias[0], SM_A)
    gdn_alog = _lane_row(gdn_A_log[0], SM_A)
    e_ssd = _head_expander(SSD_HEADS, SM_DT, SSD_P)
    e_ssd_n = _head_expander(SSD_HEADS, SM_DT, SSD_N)
    e_gdn_n = _head_expander(GDN_HEADS, SM_A, GDN_D)
    ssd_cw, ssd_cb, gdn_cw = ssd_conv_w[0], ssd_conv_b[0][None, :], gdn_conv_w[0]

    xp = x_prompt.reshape(bp * seq, d)
    proj_p, small_p = _norm_matmul(xp, nw, w_main, w_small, _row_tile(bp * seq, PROJ_ROWS), SSD_CONV)
    kv, _ = _norm_matmul(mem_prompt.reshape(bp * MEM_TOKENS, d), mem_norm_w[0][None, :], w_mem_kv[0].astype(BF16),
                         jnp.zeros((d, LANES), BF16), _row_tile(bp * MEM_TOKENS, PROJ_ROWS), 2 * MEM_W)
    y_ssd, tail_ssd, p_ssd = _ssd_prompt(proj_p, small_p, bp, ssd_cw, ssd_cb, ssd_dtb, ssd_alog, ssd_dexp, mixw1, e_ssd)
    y_gdn, tail_gdn, p_gdn = _gdn_prompt(proj_p, small_p, bp, gdn_cw, gdn_b, gdn_alog, mixw2)
    y_mem = _mem_prompt(proj_p, kv, bp, mixw3, _row_tile(seq, MEM_Q_ROWS))
    y_prompt = _out_proj(y_ssd, y_gdn, y_mem, wo1, wo2, wo3, xp, fw, _row_tile(bp * seq, PROJ_ROWS)).reshape(bp, seq, d)

    xs = x_sample.reshape(bs, d)
    proj_s, small_s = _norm_matmul(xs, nw, w_main, w_small, bs, SSD_CONV)
    ys_ssd, s_ssd_conv, s_ssd = _ssd_decode(proj_s, small_s, state_ssd_conv[0].reshape(bs, -1), state_ssd[0],
                                            ssd_cw, ssd_cb, ssd_dtb, ssd_alog, ssd_dexp, mixw1, e_ssd, e_ssd_n)
    ys_gdn, s_gdn_conv, s_gdn = _gdn_decode(proj_s, small_s, state_gdn_conv[0].reshape(bs, -1), state_gdn[0],
                                            gdn_cw, gdn_b, gdn_alog, mixw2, e_gdn_n)
    ys_mem = _mem_decode(proj_s, cache_mem_k[0].reshape(bs, MEM_TOKENS, MEM_W), cache_mem_v[0].reshape(bs, MEM_TOKENS, MEM_W), mixw3)
    y_sample = _out_proj(ys_ssd, ys_gdn, ys_mem, wo1, wo2, wo3, xs, fw, bs).reshape(bs, 1, d)

    keep = CONV_K - 1
    mem_shape = (1, bp, MEM_TOKENS, MEM_HEADS, MEM_D)
    return (
        y_prompt, y_sample,
        tail_ssd[None, :, SUBLANES - keep:, :], p_ssd[None],
        tail_gdn[None, :, SUBLANES - keep:, :], p_gdn[None],
        kv[:, :MEM_W].reshape(mem_shape), kv[:, MEM_W:].reshape(mem_shape),
        s_ssd_conv.reshape(1, bs, keep, SSD_CONV), s_ssd[None],
        s_gdn_conv.reshape(1, bs, keep, GDN_CONV), s_gdn[None],
    )
```

```python
import functools

import numpy as np
import jax
import jax.numpy as jnp
from jax import lax
from jax.experimental import pallas as pl
from jax.experimental.pallas import tpu as pltpu

F32, BF16 = jnp.float32, jnp.bfloat16

D_MODEL = 2048
SSD_HEADS, SSD_P, SSD_GROUPS, SSD_N = 16, 64, 2, 128
SSD_W = SSD_HEADS * SSD_P
SSD_GW = SSD_W // SSD_GROUPS
SSD_CONV = SSD_W + 2 * SSD_GROUPS * SSD_N
GDN_HEADS, GDN_D = 8, 128
GDN_W = GDN_HEADS * GDN_D
GDN_CONV = 3 * GDN_W
MEM_TOKENS, MEM_HEADS, MEM_D = 256, 4, 128
MEM_W = MEM_HEADS * MEM_D
MIX_W = SSD_W + GDN_W + MEM_W
CONV_K = 4
CHUNK = 64
EPS = 1e-6

LANES = 128
SUBLANES = 8
VMEM_LIMIT = 56 * 1024 * 1024
PROJ_ROWS = 512
MEM_Q_ROWS = 256

COL_QKV = 0
COL_XBC = COL_QKV + GDN_CONV
COL_QMEM = COL_XBC + SSD_CONV
COL_Z = COL_QMEM + MEM_W
N_MAIN = COL_Z + MIX_W
SM_DT, SM_B, SM_A = 0, SSD_HEADS, SSD_HEADS + GDN_HEADS


def _dot(a, b):
    return jnp.dot(a, b, preferred_element_type=F32)


def _dot_nt(a, b):
    return lax.dot_general(a, b, (((1,), (1,)), ((), ())), preferred_element_type=F32)


def _dot_tn(a, b):
    return lax.dot_general(a, b, (((0,), (0,)), ((), ())), preferred_element_type=F32)


def _split(x, n):
    parts, r = [], x
    for i in range(n):
        p = r.astype(BF16)
        parts.append(p)
        if i + 1 < n:
            r = r - p.astype(F32)
    return parts


def _sel_left(sel, x, n=3):
    return functools.reduce(lambda a, b: a + b, [_dot(sel, p) for p in _split(x, n)])


def _sel_right(x, sel, n=3):
    return functools.reduce(lambda a, b: a + b, [_dot(p, sel) for p in _split(x, n)])


def _sel_right_nt(x, sel, n=3):
    return functools.reduce(lambda a, b: a + b, [_dot_nt(p, sel) for p in _split(x, n)])


def _transpose_sel(x, n=3):
    eye = _eye(LANES).astype(BF16)
    return functools.reduce(lambda a, b: a + b, [_dot_nt(eye, p) for p in _split(x, n)])


def _eye(n):
    return (lax.broadcasted_iota(jnp.int32, (n, n), 0) == lax.broadcasted_iota(jnp.int32, (n, n), 1)).astype(F32)


def _sigmoid(x):
    return 1.0 / (1.0 + jnp.exp(-x))


def _silu(x):
    return x * _sigmoid(x)


def _softplus(x):
    return jnp.maximum(x, 0.0) + jnp.log1p(jnp.exp(-jnp.abs(x)))


def _params(*sem):
    return pltpu.CompilerParams(dimension_semantics=sem, vmem_limit_bytes=VMEM_LIMIT)


def _row_tile(rows, preferred):
    return preferred if rows % preferred == 0 else rows


def _norm_matmul_kernel(x_ref, nw_ref, w_ref, ws_ref, o_ref, os_ref, h_ref):
    @pl.when(pl.program_id(1) == 0)
    def _():
        x = x_ref[...]
        ms = jnp.mean(x * x, axis=-1, keepdims=True)
        h = (x * lax.rsqrt(ms + EPS) * nw_ref[...]).astype(BF16)
        h_ref[...] = h
        os_ref[...] = _dot(h, ws_ref[...])

    o_ref[...] = _dot(h_ref[...], w_ref[...])


def _norm_matmul(x, nw, w, ws, tm, tn):
    m, k = x.shape
    n = w.shape[1]
    ns = ws.shape[1]
    return pl.pallas_call(
        _norm_matmul_kernel,
        grid=(m // tm, n // tn),
        in_specs=[
            pl.BlockSpec((tm, k), lambda i, j: (i, 0)),
            pl.BlockSpec((1, k), lambda i, j: (0, 0)),
            pl.BlockSpec((k, tn), lambda i, j: (0, j)),
            pl.BlockSpec((k, ns), lambda i, j: (0, 0)),
        ],
        out_specs=[
            pl.BlockSpec((tm, tn), lambda i, j: (i, j)),
            pl.BlockSpec((tm, ns), lambda i, j: (i, 0)),
        ],
        out_shape=[jax.ShapeDtypeStruct((m, n), F32), jax.ShapeDtypeStruct((m, ns), F32)],
        scratch_shapes=[pltpu.VMEM((tm, k), BF16)],
        compiler_params=_params("parallel", "arbitrary"),
        name="norm_matmul",
    )(x, nw, w, ws)


def _out_proj_kernel(y1_ref, y2_ref, y3_ref, w1_ref, w2_ref, w3_ref, x_ref, fw_ref, o_ref):
    acc = (_dot(y1_ref[...].astype(BF16), w1_ref[...]) + _dot(y2_ref[...].astype(BF16), w2_ref[...])
           + _dot(y3_ref[...].astype(BF16), w3_ref[...]))
    r = x_ref[...] + acc
    ms = jnp.mean(r * r, axis=-1, keepdims=True)
    o_ref[...] = r * lax.rsqrt(ms + EPS) * fw_ref[...]


def _out_proj(y1, y2, y3, w1, w2, w3, x, fw, tm):
    m, d = x.shape
    row = lambda i: (i, 0)
    whole = lambda i: (0, 0)
    return pl.pallas_call(
        _out_proj_kernel,
        grid=(m // tm,),
        in_specs=[
            pl.BlockSpec((tm, y1.shape[1]), row), pl.BlockSpec((tm, y2.shape[1]), row), pl.BlockSpec((tm, y3.shape[1]), row),
            pl.BlockSpec(w1.shape, whole), pl.BlockSpec(w2.shape, whole), pl.BlockSpec(w3.shape, whole),
            pl.BlockSpec((tm, d), row), pl.BlockSpec((1, d), whole),
        ],
        out_specs=pl.BlockSpec((tm, d), row),
        out_shape=jax.ShapeDtypeStruct((m, d), F32),
        compiler_params=_params("parallel"),
        name="out_proj",
    )(y1, y2, y3, w1, w2, w3, x, fw)


def _causal_conv_tile(u, ubuf_ref, cw_ref, bias):
    t = u.shape[0]
    ubuf_ref[SUBLANES:SUBLANES + t, :] = u
    acc = cw_ref[CONV_K - 1:CONV_K, :] * u
    if bias is not None:
        acc = acc + bias
    for j in range(CONV_K - 1):
        lo = SUBLANES - (CONV_K - 1) + j
        acc = acc + cw_ref[j:j + 1, :] * ubuf_ref[lo:lo + t, :]
    tail = ubuf_ref[t:t + SUBLANES, :]
    ubuf_ref[0:SUBLANES, :] = tail
    return _silu(acc), tail


def _head_norm_gate(y, msq, width, z, mixw):
    return y * lax.rsqrt(msq * (1.0 / width) + EPS) * mixw * _silu(z)


def _ssd_prompt_kernel(xbc_ref, sm_ref, z_ref, cw_ref, cb_ref, dtb_ref, alog_ref, dexp_ref, mixw_ref, e_ref,
                       y_ref, tail_ref, state_ref, ubuf_ref, h_ref):
    c = pl.program_id(1)
    t = xbc_ref.shape[0]

    @pl.when(c == 0)
    def _():
        ubuf_ref[0:SUBLANES, :] = jnp.zeros((SUBLANES, SSD_CONV), F32)
        h_ref[...] = jnp.zeros_like(h_ref)

    xbc, tail = _causal_conv_tile(xbc_ref[...], ubuf_ref, cw_ref, cb_ref[...])
    tail_ref[0] = tail
    xs = xbc[:, :SSD_W]
    e = e_ref[...]

    dt = _softplus(sm_ref[...] + dtb_ref[...])
    a = dt * (-jnp.exp(alog_ref[...]))
    ri = lax.broadcasted_iota(jnp.int32, (t, t), 0)
    ci = lax.broadcasted_iota(jnp.int32, (t, t), 1)
    causal = ri >= ci
    tril = causal.astype(BF16)
    cum = _sel_left(tril, a)
    cum_t = _transpose_sel(cum)
    ecum = jnp.exp(cum)
    wend = jnp.exp(cum[t - 1:t, :] - cum)
    dt_x = _sel_right(dt, e, 2)
    ecum_x = _sel_right(ecum, e, 2)
    wend_x = _sel_right(wend, e, 2)

    xdt = xs * dt_x
    xdt_b = xdt.astype(BF16)
    xw_b = (xdt * wend_x).astype(BF16)
    lane = lax.broadcasted_iota(jnp.int32, (t, LANES), 1)
    left = lane < SSD_P

    y_blocks = []
    for g in range(SSD_GROUPS):
        bg = xbc[:, SSD_W + g * SSD_N:SSD_W + (g + 1) * SSD_N].astype(BF16)
        cg = xbc[:, SSD_W + (SSD_GROUPS + g) * SSD_N:SSD_W + (SSD_GROUPS + g + 1) * SSD_N].astype(BF16)
        gs = slice(g * SSD_GW, (g + 1) * SSD_GW)
        hg = h_ref[:, gs]
        y_inter = _dot(cg, hg.astype(BF16)) * ecum_x[:, gs]
        cb = _dot_nt(cg, bg)
        for pr in range(SSD_GW // LANES):
            blk = g * (SSD_GW // LANES) + pr
            xb = xdt_b[:, blk * LANES:(blk + 1) * LANES]
            acc = None
            for half in range(2):
                h = 2 * blk + half
                diff = cum[:, h:h + 1] - cum_t[h:h + 1, :]
                lmat = jnp.where(causal, jnp.exp(jnp.minimum(diff, 0.0)), 0.0)
                s = (cb * lmat).astype(BF16)
                xh = jnp.where(left if half == 0 else jnp.logical_not(left), xb, jnp.zeros_like(xb))
                part = _dot(s, xh)
                acc = part if acc is None else acc + part
            y_blocks.append(acc + y_inter[:, pr * LANES:(pr + 1) * LANES])
        inc = _dot_tn(bg, xw_b[:, gs])
        h_ref[:, gs] = hg * ecum_x[t - 1:t, gs] + inc
    y = jnp.concatenate(y_blocks, axis=1) + dexp_ref[...] * xs

    msq = _sel_right(_sel_right_nt(y * y, e, 2), e, 2)
    y_ref[...] = _head_norm_gate(y, msq, SSD_P, z_ref[...], mixw_ref[...]).astype(BF16)

    @pl.when(c == pl.num_programs(1) - 1)
    def _():
        state_ref[0] = h_ref[...].T.reshape(SSD_HEADS, SSD_P, SSD_N)


def _ssd_prompt(proj, small, batch, cw, cb, dtb, alog, dexp, mixw, e):
    rows = proj.shape[0]
    nc = rows // batch // CHUNK
    row = lambda b, c: (b * nc + c, 0)
    whole = lambda b, c: (0, 0)
    return pl.pallas_call(
        _ssd_prompt_kernel,
        grid=(batch, nc),
        in_specs=[
            pl.BlockSpec((CHUNK, SSD_CONV), lambda b, c: (b * nc + c, COL_XBC // SSD_CONV)),
            pl.BlockSpec((CHUNK, LANES), row),
            pl.BlockSpec((CHUNK, SSD_W), lambda b, c: (b * nc + c, COL_Z // SSD_W)),
            pl.BlockSpec(cw.shape, whole), pl.BlockSpec(cb.shape, whole), pl.BlockSpec(dtb.shape, whole),
            pl.BlockSpec(alog.shape, whole), pl.BlockSpec(dexp.shape, whole), pl.BlockSpec(mixw.shape, whole),
            pl.BlockSpec(e.shape, whole),
        ],
        out_specs=[
            pl.BlockSpec((CHUNK, SSD_W), row),
            pl.BlockSpec((1, SUBLANES, SSD_CONV), lambda b, c: (b, 0, 0)),
            pl.BlockSpec((1, SSD_HEADS, SSD_P, SSD_N), lambda b, c: (b, 0, 0, 0)),
        ],
        out_shape=[
            jax.ShapeDtypeStruct((rows, SSD_W), BF16),
            jax.ShapeDtypeStruct((batch, SUBLANES, SSD_CONV), F32),
            jax.ShapeDtypeStruct((batch, SSD_HEADS, SSD_P, SSD_N), F32),
        ],
        scratch_shapes=[pltpu.VMEM((CHUNK + SUBLANES, SSD_CONV), F32), pltpu.VMEM((SSD_N, SSD_W), F32)],
        compiler_params=_params("parallel", "arbitrary"),
        name="ssd_prompt",
    )(proj, small, proj, cw, cb, dtb, alog, dexp, mixw, e)


def _unit_lower_inverses(a_stricts, ri, ci):
    t = a_stricts[0].shape[0]
    eye = _eye(t)
    first = (ri == ci + 1) & (ci % 2 == 0)
    invs = [eye - jnp.where(first, a, 0.0) for a in a_stricts]
    s = 2
    while s < t:
        sel = (ri // (2 * s) == ci // (2 * s)) & ((ri // s) % 2 == 1) & ((ci // s) % 2 == 0)
        inv_bs = [inv.astype(BF16) for inv in invs]
        lefts = [_dot(inv_b, jnp.where(sel, a, 0.0).astype(BF16)).astype(BF16) for inv_b, a in zip(inv_bs, a_stricts)]
        invs = [inv - _dot(left, inv_b) for inv, left, inv_b in zip(invs, lefts, inv_bs)]
        s *= 2
    return invs


def _gdn_prompt_kernel(qkv_ref, sm_ref, z_ref, cw_ref, gb_ref, galog_ref, mixw_ref,
                       y_ref, tail_ref, state_ref, ubuf_ref, s_ref):
    c = pl.program_id(1)
    t = qkv_ref.shape[0]

    @pl.when(c == 0)
    def _():
        ubuf_ref[0:SUBLANES, :] = jnp.zeros((SUBLANES, GDN_CONV), F32)
        s_ref[...] = jnp.zeros_like(s_ref)

    qkv, tail = _causal_conv_tile(qkv_ref[...], ubuf_ref, cw_ref, None)
    tail_ref[0] = tail

    sm = sm_ref[...]
    beta = _sigmoid(sm)
    g = -jnp.exp(galog_ref[...]) * _softplus(sm + gb_ref[...])
    ri = lax.broadcasted_iota(jnp.int32, (t, t), 0)
    ci = lax.broadcasted_iota(jnp.int32, (t, t), 1)
    causal = ri >= ci
    strict = ri > ci
    gc = _sel_left(causal.astype(BF16), g)
    gc_t = _transpose_sel(gc)
    eg = jnp.exp(gc)
    g_last = gc[t - 1:t, :]
    eend = jnp.exp(g_last - gc)
    elast = jnp.exp(g_last)

    heads = range(GDN_HEADS)
    hs = [slice(h * GDN_D, (h + 1) * GDN_D) for h in heads]
    la = [SM_A + h for h in heads]
    q = [qkv[:, h * GDN_D:(h + 1) * GDN_D] for h in heads]
    k = [qkv[:, GDN_W + h * GDN_D:GDN_W + (h + 1) * GDN_D] for h in heads]
    v = [qkv[:, 2 * GDN_W + h * GDN_D:2 * GDN_W + (h + 1) * GDN_D] for h in heads]
    q = [x * lax.rsqrt(jnp.sum(x * x, axis=-1, keepdims=True) + EPS) * (GDN_D ** -0.5) for x in q]
    k = [x * lax.rsqrt(jnp.sum(x * x, axis=-1, keepdims=True) + EPS) for x in k]
    b_col = [beta[:, SM_B + h:SM_B + h + 1] for h in heads]
    eg_col = [eg[:, l:l + 1] for l in la]
    decay = [jnp.where(causal, jnp.exp(jnp.minimum(gc[:, l:l + 1] - gc_t[l:l + 1, :], 0.0)), 0.0) for l in la]
    kb = [k[h] * b_col[h] for h in heads]
    k_b = [x.astype(BF16) for x in k]
    kk = [_dot_nt(kb[h].astype(BF16), k_b[h]) for h in heads]
    qk = [_dot_nt(q[h].astype(BF16), k_b[h]) for h in heads]
    a_strict = [jnp.where(strict, kk[h] * decay[h], 0.0) for h in heads]
    attn = [(qk[h] * decay[h]).astype(BF16) for h in heads]
    t_inv = [x.astype(BF16) for x in _unit_lower_inverses(a_strict, ri, ci)]
    u = [_dot(t_inv[h], (v[h] * b_col[h]).astype(BF16)) for h in heads]
    wk = [_dot(t_inv[h], (kb[h] * eg_col[h]).astype(BF16)).astype(BF16) for h in heads]

    s_old = [s_ref[h] for h in heads]
    s_b = [x.astype(BF16) for x in s_old]
    v_new = [(u[h] - _dot(wk[h], s_b[h])).astype(BF16) for h in heads]
    k_end = [(k[h] * eend[:, l:l + 1]).astype(BF16) for h, l in zip(heads, la)]
    s_inc = [_dot_tn(k_end[h], v_new[h]) for h in heads]
    for h in heads:
        s_ref[h] = s_old[h] * elast[:, la[h]:la[h] + 1] + s_inc[h]
    o = [_dot((q[h] * eg_col[h]).astype(BF16), s_b[h]) + _dot(attn[h], v_new[h]) for h in heads]
    msq = [jnp.sum(x * x, axis=-1, keepdims=True) for x in o]
    for h in heads:
        y_ref[:, hs[h]] = _head_norm_gate(o[h], msq[h], GDN_D, z_ref[:, hs[h]], mixw_ref[:, hs[h]]).astype(BF16)

    @pl.when(c == pl.num_programs(1) - 1)
    def _():
        state_ref[0] = s_ref[...]


def _gdn_prompt(proj, small, batch, cw, gb, galog, mixw):
    rows = proj.shape[0]
    nc = rows // batch // CHUNK
    row = lambda b, c: (b * nc + c, 0)
    whole = lambda b, c: (0, 0)
    return pl.pallas_call(
        _gdn_prompt_kernel,
        grid=(batch, nc),
        in_specs=[
            pl.BlockSpec((CHUNK, GDN_CONV), lambda b, c: (b * nc + c, COL_QKV // GDN_CONV)),
            pl.BlockSpec((CHUNK, LANES), row),
            pl.BlockSpec((CHUNK, GDN_W), lambda b, c: (b * nc + c, (COL_Z + SSD_W) // GDN_W)),
            pl.BlockSpec(cw.shape, whole), pl.BlockSpec(gb.shape, whole), pl.BlockSpec(galog.shape, whole),
            pl.BlockSpec(mixw.shape, whole),
        ],
        out_specs=[
            pl.BlockSpec((CHUNK, GDN_W), row),
            pl.BlockSpec((1, SUBLANES, GDN_CONV), lambda b, c: (b, 0, 0)),
            pl.BlockSpec((1, GDN_HEADS, GDN_D, GDN_D), lambda b, c: (b, 0, 0, 0)),
        ],
        out_shape=[
            jax.ShapeDtypeStruct((rows, GDN_W), BF16),
            jax.ShapeDtypeStruct((batch, SUBLANES, GDN_CONV), F32),
            jax.ShapeDtypeStruct((batch, GDN_HEADS, GDN_D, GDN_D), F32),
        ],
        scratch_shapes=[pltpu.VMEM((CHUNK + SUBLANES, GDN_CONV), F32), pltpu.VMEM((GDN_HEADS, GDN_D, GDN_D), F32)],
        compiler_params=_params("parallel", "arbitrary"),
        name="gdn_prompt",
    )(proj, small, proj, cw, gb, galog, mixw)


def _mem_attention_head(q, k, v):
    s = _dot_nt(q.astype(BF16), k.astype(BF16)) * (MEM_D ** -0.5)
    e = jnp.exp(s - jnp.max(s, axis=-1, keepdims=True))
    p = e / jnp.sum(e, axis=-1, keepdims=True)
    return _dot(p.astype(BF16), v.astype(BF16))


def _mem_prompt_kernel(q_ref, k_ref, v_ref, z_ref, mixw_ref, y_ref):
    for h in range(MEM_HEADS):
        hs = slice(h * MEM_D, (h + 1) * MEM_D)
        o = _mem_attention_head(q_ref[:, hs], k_ref[:, hs], v_ref[:, hs])
        msq = jnp.sum(o * o, axis=-1, keepdims=True)
        y_ref[:, hs] = _head_norm_gate(o, msq, MEM_D, z_ref[:, hs], mixw_ref[:, hs]).astype(BF16)


def _mem_prompt(proj, kv, batch, mixw, tq):
    rows = proj.shape[0]
    nq = rows // batch // tq
    return pl.pallas_call(
        _mem_prompt_kernel,
        grid=(batch, nq),
        in_specs=[
            pl.BlockSpec((tq, MEM_W), lambda b, i: (b * nq + i, COL_QMEM // MEM_W)),
            pl.BlockSpec((MEM_TOKENS, MEM_W), lambda b, i: (b, 0)),
            pl.BlockSpec((MEM_TOKENS, MEM_W), lambda b, i: (b, 1)),
            pl.BlockSpec((tq, MEM_W), lambda b, i: (b * nq + i, (COL_Z + SSD_W + GDN_W) // MEM_W)),
            pl.BlockSpec(mixw.shape, lambda b, i: (0, 0)),
        ],
        out_specs=pl.BlockSpec((tq, MEM_W), lambda b, i: (b * nq + i, 0)),
        out_shape=jax.ShapeDtypeStruct((rows, MEM_W), BF16),
        compiler_params=_params("parallel", "parallel"),
        name="mem_prompt",
    )(proj, kv, kv, proj, mixw)


DEC_ROWS = SUBLANES


def _conv_step(u, st, cw_ref, bias, width):
    acc = cw_ref[CONV_K - 1:CONV_K, :] * u
    if bias is not None:
        acc = acc + bias
    for j in range(CONV_K - 1):
        acc = acc + cw_ref[j:j + 1, :] * st[:, j * width:(j + 1) * width]
    return _silu(acc), jnp.concatenate([st[:, width:], u], axis=1)


def _rows_to_columns(x):
    pad = jnp.zeros((LANES - x.shape[0], x.shape[1]), F32)
    return jnp.concatenate([x, pad], axis=0).T


def _pick_rows(parts):
    rid = lax.broadcasted_iota(jnp.int32, parts[0].shape, 0)
    out = parts[0]
    for i in range(1, len(parts)):
        out = jnp.where(rid == i, parts[i], out)
    return out


def _ssd_decode_kernel(xbc_ref, sm_ref, z_ref, cst_ref, st_ref, cw_ref, cb_ref, dtb_ref, alog_ref, dexp_ref,
                       mixw_ref, e_ref, en_ref, y_ref, cst_out_ref, st_out_ref):
    xbc, cst_new = _conv_step(xbc_ref[...], cst_ref[...], cw_ref, cb_ref[...], SSD_CONV)
    cst_out_ref[...] = cst_new
    xs = xbc[:, :SSD_W]
    e = e_ref[...]
    dt = _softplus(sm_ref[...] + dtb_ref[...])
    dec = jnp.exp(dt * (-jnp.exp(alog_ref[...])))
    xd_t = _rows_to_columns(xs * _sel_right(dt, e))
    dec_n = _sel_right(dec, en_ref[...])

    y_groups = []
    for g in range(SSD_GROUPS):
        b_g = xbc[:, SSD_W + g * SSD_N:SSD_W + (g + 1) * SSD_N]
        c_g = xbc[:, SSD_W + (SSD_GROUPS + g) * SSD_N:SSD_W + (SSD_GROUPS + g + 1) * SSD_N].astype(BF16)
        per_row = []
        for i in range(DEC_ROWS):
            new = []
            for r in range(SSD_HEADS // SSD_GROUPS):
                h = g * (SSD_HEADS // SSD_GROUPS) + r
                col = xd_t[h * SSD_P:(h + 1) * SSD_P, i:i + 1]
                hn = st_ref[i, h] * dec_n[i:i + 1, h * SSD_N:(h + 1) * SSD_N] + col * b_g[i:i + 1, :]
                st_out_ref[i, h] = hn
                new.append(hn)
            hg = jnp.concatenate(new, axis=0).astype(BF16)
            per_row.append(_dot_nt(c_g, hg))
        y_groups.append(_pick_rows(per_row))
    y = jnp.concatenate(y_groups, axis=1) + dexp_ref[...] * xs
    msq = _sel_right(_sel_right_nt(y * y, e, 2), e, 2)
    y_ref[...] = _head_norm_gate(y, msq, SSD_P, z_ref[...], mixw_ref[...])


def _ssd_decode(proj, small, cst, st, cw, cb, dtb, alog, dexp, mixw, e, en):
    rows = proj.shape[0]
    row = lambda i: (i, 0)
    whole = lambda i: (0, 0)
    return pl.pallas_call(
        _ssd_decode_kernel,
        grid=(rows // DEC_ROWS,),
        in_specs=[
            pl.BlockSpec((DEC_ROWS, SSD_CONV), lambda i: (i, COL_XBC // SSD_CONV)),
            pl.BlockSpec((DEC_ROWS, LANES), row),
            pl.BlockSpec((DEC_ROWS, SSD_W), lambda i: (i, COL_Z // SSD_W)),
            pl.BlockSpec((DEC_ROWS, cst.shape[1]), row),
            pl.BlockSpec((DEC_ROWS, SSD_HEADS, SSD_P, SSD_N), lambda i: (i, 0, 0, 0)),
            pl.BlockSpec(cw.shape, whole), pl.BlockSpec(cb.shape, whole), pl.BlockSpec(dtb.shape, whole),
            pl.BlockSpec(alog.shape, whole), pl.BlockSpec(dexp.shape, whole), pl.BlockSpec(mixw.shape, whole),
            pl.BlockSpec(e.shape, whole), pl.BlockSpec(en.shape, whole),
        ],
        out_specs=[
            pl.BlockSpec((DEC_ROWS, SSD_W), row),
            pl.BlockSpec((DEC_ROWS, cst.shape[1]), row),
            pl.BlockSpec((DEC_ROWS, SSD_HEADS, SSD_P, SSD_N), lambda i: (i, 0, 0, 0)),
        ],
        out_shape=[
            jax.ShapeDtypeStruct((rows, SSD_W), F32),
            jax.ShapeDtypeStruct(cst.shape, F32),
            jax.ShapeDtypeStruct(st.shape, F32),
        ],
        compiler_params=_params("parallel"),
        name="ssd_decode",
    )(proj, small, proj, cst, st, cw, cb, dtb, alog, dexp, mixw, e, en)


def _gdn_decode_kernel(qkv_ref, sm_ref, z_ref, cst_ref, st_ref, cw_ref, gb_ref, galog_ref, mixw_ref, en_ref,
                       y_ref, cst_out_ref, st_out_ref):
    qkv, cst_new = _conv_step(qkv_ref[...], cst_ref[...], cw_ref, None, GDN_CONV)
    cst_out_ref[...] = cst_new
    sm = sm_ref[...]
    beta = _sigmoid(sm)
    eg = jnp.exp(-jnp.exp(galog_ref[...]) * _softplus(sm + gb_ref[...]))
    eg_n = _sel_right(eg, en_ref[...])

    qs, ks = [], []
    for h in range(GDN_HEADS):
        q = qkv[:, h * GDN_D:(h + 1) * GDN_D]
        k = qkv[:, GDN_W + h * GDN_D:GDN_W + (h + 1) * GDN_D]
        qs.append(q * lax.rsqrt(jnp.sum(q * q, axis=-1, keepdims=True) + EPS) * (GDN_D ** -0.5))
        ks.append(k * lax.rsqrt(jnp.sum(k * k, axis=-1, keepdims=True) + EPS))
    k_t = _rows_to_columns(jnp.concatenate(ks, axis=1))

    for h in range(GDN_HEADS):
        hs = slice(h * GDN_D, (h + 1) * GDN_D)
        q, k = qs[h], ks[h]
        v = qkv[:, 2 * GDN_W + h * GDN_D:2 * GDN_W + (h + 1) * GDN_D]
        q_b, k_b = q.astype(BF16), k.astype(BF16)
        ks_rows, qs_rows = [], []
        for i in range(DEC_ROWS):
            s_b = st_ref[i, h].astype(BF16)
            ks_rows.append(_dot(k_b, s_b))
            qs_rows.append(_dot(q_b, s_b))
        k_s, q_s = _pick_rows(ks_rows), _pick_rows(qs_rows)
        eg_h = eg_n[:, hs]
        v_new = beta[:, SM_B + h:SM_B + h + 1] * (v - eg_h * k_s)
        o = eg_h * q_s + jnp.sum(q * k, axis=-1, keepdims=True) * v_new
        for i in range(DEC_ROWS):
            col = k_t[hs, i:i + 1]
            st_out_ref[i, h] = st_ref[i, h] * eg_h[i:i + 1, :] + col * v_new[i:i + 1, :]
        msq = jnp.sum(o * o, axis=-1, keepdims=True)
        y_ref[:, hs] = _head_norm_gate(o, msq, GDN_D, z_ref[:, hs], mixw_ref[:, hs])


def _gdn_decode(proj, small, cst, st, cw, gb, galog, mixw, en):
    rows = proj.shape[0]
    row = lambda i: (i, 0)
    whole = lambda i: (0, 0)
    return pl.pallas_call(
        _gdn_decode_kernel,
        grid=(rows // DEC_ROWS,),
        in_specs=[
            pl.BlockSpec((DEC_ROWS, GDN_CONV), lambda i: (i, COL_QKV // GDN_CONV)),
            pl.BlockSpec((DEC_ROWS, LANES), row),
            pl.BlockSpec((DEC_ROWS, GDN_W), lambda i: (i, (COL_Z + SSD_W) // GDN_W)),
            pl.BlockSpec((DEC_ROWS, cst.shape[1]), row),
            pl.BlockSpec((DEC_ROWS, GDN_HEADS, GDN_D, GDN_D), lambda i: (i, 0, 0, 0)),
            pl.BlockSpec(cw.shape, whole), pl.BlockSpec(gb.shape, whole), pl.BlockSpec(galog.shape, whole),
            pl.BlockSpec(mixw.shape, whole), pl.BlockSpec(en.shape, whole),
        ],
        out_specs=[
            pl.BlockSpec((DEC_ROWS, GDN_W), row),
            pl.BlockSpec((DEC_ROWS, cst.shape[1]), row),
            pl.BlockSpec((DEC_ROWS, GDN_HEADS, GDN_D, GDN_D), lambda i: (i, 0, 0, 0)),
        ],
        out_shape=[
            jax.ShapeDtypeStruct((rows, GDN_W), F32),
            jax.ShapeDtypeStruct(cst.shape, F32),
            jax.ShapeDtypeStruct(st.shape, F32),
        ],
        compiler_params=_params("parallel"),
        name="gdn_decode",
    )(proj, small, proj, cst, st, cw, gb, galog, mixw, en)


def _mem_decode_kernel(q_ref, k_ref, v_ref, z_ref, mixw_ref, y_ref):
    heads, rows = range(MEM_HEADS), range(DEC_ROWS)
    hs = [slice(h * MEM_D, (h + 1) * MEM_D) for h in heads]
    win = [pl.ds(h, MEM_TOKENS, stride=MEM_HEADS) for h in heads]
    q = [q_ref[:, hs[h]].astype(BF16) for h in heads]
    s = [_pick_rows([_dot_nt(q[h], k_ref[i, win[h], :].astype(BF16)) for i in rows]) * (MEM_D ** -0.5) for h in heads]
    e = [jnp.exp(x - jnp.max(x, axis=-1, keepdims=True)) for x in s]
    p = [(x / jnp.sum(x, axis=-1, keepdims=True)).astype(BF16) for x in e]
    o = [_pick_rows([_dot(p[h], v_ref[i, win[h], :].astype(BF16)) for i in rows]) for h in heads]
    msq = [jnp.sum(x * x, axis=-1, keepdims=True) for x in o]
    for h in heads:
        y_ref[:, hs[h]] = _head_norm_gate(o[h], msq[h], MEM_D, z_ref[:, hs[h]], mixw_ref[:, hs[h]])


def _mem_decode(proj, mem_k, mem_v, mixw):
    rows = proj.shape[0]
    return pl.pallas_call(
        _mem_decode_kernel,
        grid=(rows // DEC_ROWS,),
        in_specs=[
            pl.BlockSpec((DEC_ROWS, MEM_W), lambda i: (i, COL_QMEM // MEM_W)),
            pl.BlockSpec((DEC_ROWS, MEM_TOKENS * MEM_HEADS, MEM_D), lambda i: (i, 0, 0)),
            pl.BlockSpec((DEC_ROWS, MEM_TOKENS * MEM_HEADS, MEM_D), lambda i: (i, 0, 0)),
            pl.BlockSpec((DEC_ROWS, MEM_W), lambda i: (i, (COL_Z + SSD_W + GDN_W) // MEM_W)),
            pl.BlockSpec(mixw.shape, lambda i: (0, 0)),
        ],
        out_specs=pl.BlockSpec((DEC_ROWS, MEM_W), lambda i: (i, 0)),
        out_shape=jax.ShapeDtypeStruct((rows, MEM_W), F32),
        compiler_params=_params("parallel"),
        name="mem_decode",
    )(proj, mem_k, mem_v, proj, mixw)


def _head_expander(heads, first_lane, width):
    m = np.zeros((LANES, heads * width), np.float32)
    for h in range(heads):
        m[first_lane + h, h * width:(h + 1) * width] = 1.0
    return jnp.asarray(m, BF16)


def _lane_row(vec, first_lane):
    return jnp.zeros((1, LANES), F32).at[0, first_lane:first_lane + vec.shape[0]].set(vec.astype(F32))


def kernel(x_prompt, x_sample, mem_prompt, state_ssd_conv, state_ssd, state_gdn_conv, state_gdn, cache_mem_k, cache_mem_v, norm_w, w_in, ssd_conv_w, ssd_conv_b, ssd_dt_bias, ssd_A_log, ssd_D, gdn_conv_w, gdn_dt_bias, gdn_A_log, mem_norm_w, w_mem_kv, mix_norm_w, w_out, final_norm_w):
    bp, seq, d = x_prompt.shape
    bs = x_sample.shape[0]
    assert (d, seq % CHUNK, bs % DEC_ROWS, norm_w.shape[0]) == (D_MODEL, 0, 0, 1)

    w = w_in[0]
    o_dt = SSD_CONV
    o_qkv = o_dt + SSD_HEADS
    o_b = o_qkv + GDN_CONV
    o_a = o_b + GDN_HEADS
    o_qm = o_a + GDN_HEADS
    o_z = o_qm + MEM_W
    w_main = jnp.concatenate([w[:, o_qkv:o_b], w[:, :o_dt], w[:, o_qm:o_z], w[:, o_z:]], axis=1).astype(BF16)
    w_small = jnp.concatenate(
        [w[:, o_dt:o_qkv], w[:, o_b:o_a], w[:, o_a:o_qm], jnp.zeros((d, LANES - SSD_HEADS - 2 * GDN_HEADS), F32)], axis=1
    ).astype(BF16)
    wo = w_out[0].astype(BF16)
    wo1, wo2, wo3 = wo[:SSD_W], wo[SSD_W:SSD_W + GDN_W], wo[SSD_W + GDN_W:]
    nw = norm_w[0][None, :]
    mixw = mix_norm_w[0][None, :]
    mixw1, mixw2, mixw3 = mixw[:, :SSD_W], mixw[:, SSD_W:SSD_W + GDN_W], mixw[:, SSD_W + GDN_W:]
    fw = final_norm_w[None, :]
    ssd_dtb = _lane_row(ssd_dt_bias[0], SM_DT)
    ssd_alog = _lane_row(ssd_A_log[0], SM_DT)
    ssd_dexp = jnp.repeat(ssd_D[0].astype(F32), SSD_P)[None, :]
    gdn_b = _lane_row(gdn_dt_bias[0], SM_A)
    gdn_alog = _lane_row(gdn_A_log[0], SM_A)
    e_ssd = _head_expander(SSD_HEADS, SM_DT, SSD_P)
    e_ssd_n = _head_expander(SSD_HEADS, SM_DT, SSD_N)
    e_gdn_n = _head_expander(GDN_HEADS, SM_A, GDN_D)
    ssd_cw, ssd_cb, gdn_cw = ssd_conv_w[0], ssd_conv_b[0][None, :], gdn_conv_w[0]

    xp = x_prompt.reshape(bp * seq, d)
    proj_p, small_p = _norm_matmul(xp, nw, w_main, w_small, _row_tile(bp * seq, PROJ_ROWS), SSD_CONV)
    kv, _ = _norm_matmul(mem_prompt.reshape(bp * MEM_TOKENS, d), mem_norm_w[0][None, :], w_mem_kv[0].astype(BF16),
                         jnp.zeros((d, LANES), BF16), _row_tile(bp * MEM_TOKENS, PROJ_ROWS), 2 * MEM_W)
    y_ssd, tail_ssd, p_ssd = _ssd_prompt(proj_p, small_p, bp, ssd_cw, ssd_cb, ssd_dtb, ssd_alog, ssd_dexp, mixw1, e_ssd)
    y_gdn, tail_gdn, p_gdn = _gdn_prompt(proj_p, small_p, bp, gdn_cw, gdn_b, gdn_alog, mixw2)
    y_mem = _mem_prompt(proj_p, kv, bp, mixw3, _row_tile(seq, MEM_Q_ROWS))
    y_prompt = _out_proj(y_ssd, y_gdn, y_mem, wo1, wo2, wo3, xp, fw, _row_tile(bp * seq, PROJ_ROWS)).reshape(bp, seq, d)

    xs = x_sample.reshape(bs, d)
    proj_s, small_s = _norm_matmul(xs, nw, w_main, w_small, bs, SSD_CONV)
    ys_ssd, s_ssd_conv, s_ssd = _ssd_decode(proj_s, small_s, state_ssd_conv[0].reshape(bs, -1), state_ssd[0],
                                            ssd_cw, ssd_cb, ssd_dtb, ssd_alog, ssd_dexp, mixw1, e_ssd, e_ssd_n)
    ys_gdn, s_gdn_conv, s_gdn = _gdn_decode(proj_s, small_s, state_gdn_conv[0].reshape(bs, -1), state_gdn[0],
                                            gdn_cw, gdn_b, gdn_alog, mixw2, e_gdn_n)
    ys_mem = _mem_decode(proj_s, cache_mem_k.reshape(bs, MEM_TOKENS * MEM_HEADS, MEM_D),
                         cache_mem_v.reshape(bs, MEM_TOKENS * MEM_HEADS, MEM_D), mixw3)
    y_sample = _out_proj(ys_ssd, ys_gdn, ys_mem, wo1, wo2, wo3, xs, fw, bs).reshape(bs, 1, d)

    keep = CONV_K - 1
    mem_shape = (1, bp, MEM_TOKENS, MEM_HEADS, MEM_D)
    return (
        y_prompt, y_sample,
        tail_ssd[None, :, SUBLANES - keep:, :], p_ssd[None],
        tail_gdn[None, :, SUBLANES - keep:, :], p_gdn[None],
        kv[:, :MEM_W].reshape(mem_shape), kv[:, MEM_W:].reshape(mem_shape),
        s_ssd_conv.reshape(1, bs, keep, SSD_CONV), s_ssd[None],
        s_gdn_conv.reshape(1, bs, keep, GDN_CONV), s_gdn[None],
    )
```

```python
import functools

import numpy as np
import jax
import jax.numpy as jnp
from jax import lax
from jax.experimental import pallas as pl
from jax.experimental.pallas import tpu as pltpu

F32, BF16 = jnp.float32, jnp.bfloat16

D_MODEL = 2048
SSD_HEADS, SSD_P, SSD_GROUPS, SSD_N = 16, 64, 2, 128
SSD_W = SSD_HEADS * SSD_P
SSD_GW = SSD_W // SSD_GROUPS
SSD_CONV = SSD_W + 2 * SSD_GROUPS * SSD_N
GDN_HEADS, GDN_D = 8, 128
GDN_W = GDN_HEADS * GDN_D
GDN_CONV = 3 * GDN_W
MEM_TOKENS, MEM_HEADS, MEM_D = 256, 4, 128
MEM_W = MEM_HEADS * MEM_D
MIX_W = SSD_W + GDN_W + MEM_W
CONV_K = 4
CHUNK = 64
EPS = 1e-6

LANES = 128
SUBLANES = 8
VMEM_LIMIT = 56 * 1024 * 1024
PROJ_ROWS = 1024
OUT_ROWS = 512
MEM_Q_ROWS = 256
SCAN_ROWS = 256
CONV_COLS = 512

COL_QKV = 0
COL_XBC = COL_QKV + GDN_CONV
COL_QMEM = COL_XBC + SSD_CONV
COL_Z = COL_QMEM + MEM_W
N_MAIN = COL_Z + MIX_W
SM_DT, SM_B, SM_A = 0, SSD_HEADS, SSD_HEADS + GDN_HEADS


def _dot(a, b):
    return jnp.dot(a, b, preferred_element_type=F32)


def _dot_nt(a, b):
    return lax.dot_general(a, b, (((1,), (1,)), ((), ())), preferred_element_type=F32)


def _dot_tn(a, b):
    return lax.dot_general(a, b, (((0,), (0,)), ((), ())), preferred_element_type=F32)


def _split(x, n):
    parts, r = [], x
    for i in range(n):
        p = r.astype(BF16)
        parts.append(p)
        if i + 1 < n:
            r = r - p.astype(F32)
    return parts


def _sel_left(sel, x, n=3):
    return functools.reduce(lambda a, b: a + b, [_dot(sel, p) for p in _split(x, n)])


def _sel_right(x, sel, n=3):
    return functools.reduce(lambda a, b: a + b, [_dot(p, sel) for p in _split(x, n)])


def _sel_right_nt(x, sel, n=3):
    return functools.reduce(lambda a, b: a + b, [_dot_nt(p, sel) for p in _split(x, n)])


def _transpose_sel(x, n=3):
    eye = _eye(LANES).astype(BF16)
    return functools.reduce(lambda a, b: a + b, [_dot_nt(eye, p) for p in _split(x, n)])


def _eye(n):
    return (lax.broadcasted_iota(jnp.int32, (n, n), 0) == lax.broadcasted_iota(jnp.int32, (n, n), 1)).astype(F32)


def _sigmoid(x):
    return 1.0 / (1.0 + jnp.exp(-x))


def _silu(x):
    return x * _sigmoid(x)


def _softplus(x):
    return jnp.maximum(x, 0.0) + jnp.log1p(jnp.exp(-jnp.abs(x)))


def _params(*sem):
    return pltpu.CompilerParams(dimension_semantics=sem, vmem_limit_bytes=VMEM_LIMIT)


def _row_tile(rows, preferred):
    return preferred if rows % preferred == 0 else rows


def _norm_matmul_kernel(x_ref, nw_ref, w_ref, ws_ref, o_ref, os_ref, h_ref):
    @pl.when(pl.program_id(1) == 0)
    def _():
        x = x_ref[...]
        ms = jnp.mean(x * x, axis=-1, keepdims=True)
        h = (x * lax.rsqrt(ms + EPS) * nw_ref[...]).astype(BF16)
        h_ref[...] = h
        os_ref[...] = _dot(h, ws_ref[...])

    o_ref[...] = _dot(h_ref[...], w_ref[...])


def _norm_matmul(x, nw, w, ws, tm, tn):
    m, k = x.shape
    n = w.shape[1]
    ns = ws.shape[1]
    return pl.pallas_call(
        _norm_matmul_kernel,
        grid=(m // tm, n // tn),
        in_specs=[
            pl.BlockSpec((tm, k), lambda i, j: (i, 0)),
            pl.BlockSpec((1, k), lambda i, j: (0, 0)),
            pl.BlockSpec((k, tn), lambda i, j: (0, j)),
            pl.BlockSpec((k, ns), lambda i, j: (0, 0)),
        ],
        out_specs=[
            pl.BlockSpec((tm, tn), lambda i, j: (i, j)),
            pl.BlockSpec((tm, ns), lambda i, j: (i, 0)),
        ],
        out_shape=[jax.ShapeDtypeStruct((m, n), F32), jax.ShapeDtypeStruct((m, ns), F32)],
        scratch_shapes=[pltpu.VMEM((tm, k), BF16)],
        compiler_params=_params("parallel", "arbitrary"),
        name="norm_matmul",
    )(x, nw, w, ws)


def _out_proj_kernel(y1_ref, y2_ref, y3_ref, w1_ref, w2_ref, w3_ref, x_ref, fw_ref, o_ref):
    acc = (_dot(y1_ref[...].astype(BF16), w1_ref[...]) + _dot(y2_ref[...].astype(BF16), w2_ref[...])
           + _dot(y3_ref[...].astype(BF16), w3_ref[...]))
    r = x_ref[...] + acc
    ms = jnp.mean(r * r, axis=-1, keepdims=True)
    o_ref[...] = r * lax.rsqrt(ms + EPS) * fw_ref[...]


def _out_proj(y1, y2, y3, w1, w2, w3, x, fw, tm):
    m, d = x.shape
    row = lambda i: (i, 0)
    whole = lambda i: (0, 0)
    return pl.pallas_call(
        _out_proj_kernel,
        grid=(m // tm,),
        in_specs=[
            pl.BlockSpec((tm, y1.shape[1]), row), pl.BlockSpec((tm, y2.shape[1]), row), pl.BlockSpec((tm, y3.shape[1]), row),
            pl.BlockSpec(w1.shape, whole), pl.BlockSpec(w2.shape, whole), pl.BlockSpec(w3.shape, whole),
            pl.BlockSpec((tm, d), row), pl.BlockSpec((1, d), whole),
        ],
        out_specs=pl.BlockSpec((tm, d), row),
        out_shape=jax.ShapeDtypeStruct((m, d), F32),
        compiler_params=_params("parallel"),
        name="out_proj",
    )(y1, y2, y3, w1, w2, w3, x, fw)


def _causal_conv_tile(u_ref, ubuf_ref, cw_ref, cb_ref, out_ref, tail_ref):
    t, width = u_ref.shape
    for c0 in range(0, width, CONV_COLS):
        cs = slice(c0, c0 + CONV_COLS)
        u = u_ref[:, cs]
        ubuf_ref[SUBLANES:SUBLANES + t, cs] = u
        acc = cw_ref[CONV_K - 1:CONV_K, cs] * u
        if cb_ref is not None:
            acc = acc + cb_ref[:, cs]
        for j in range(CONV_K - 1):
            lo = SUBLANES - (CONV_K - 1) + j
            acc = acc + cw_ref[j:j + 1, cs] * ubuf_ref[lo:lo + t, cs]
        tail = ubuf_ref[t:t + SUBLANES, cs]
        ubuf_ref[0:SUBLANES, cs] = tail
        tail_ref[0, :, cs] = tail
        out_ref[:, cs] = _silu(acc)


def _head_norm_gate(y, msq, width, z, mixw):
    return y * lax.rsqrt(msq * (1.0 / width) + EPS) * mixw * _silu(z)


def _ssd_prompt_kernel(xbc_ref, sm_ref, z_ref, cw_ref, cb_ref, dtb_ref, alog_ref, dexp_ref, mixw_ref, e_ref,
                       y_ref, tail_ref, state_ref, ubuf_ref, conv_ref, h_ref):
    c = pl.program_id(1)
    t = xbc_ref.shape[0]
    subs = range(t // CHUNK)
    groups = range(SSD_GROUPS)
    blocks = range(SSD_GW // LANES)

    @pl.when(c == 0)
    def _():
        ubuf_ref[0:SUBLANES, :] = jnp.zeros((SUBLANES, SSD_CONV), F32)
        h_ref[...] = jnp.zeros_like(h_ref)

    _causal_conv_tile(xbc_ref, ubuf_ref, cw_ref, cb_ref, conv_ref, tail_ref)
    xs = conv_ref[:, :SSD_W]
    e = e_ref[...]
    rows = [slice(j * CHUNK, (j + 1) * CHUNK) for j in subs]
    gs = [slice(g * SSD_GW, (g + 1) * SSD_GW) for g in groups]

    dt = _softplus(sm_ref[...] + dtb_ref[...])
    a = dt * (-jnp.exp(alog_ref[...]))
    rt = lax.broadcasted_iota(jnp.int32, (t, t), 0)
    ct = lax.broadcasted_iota(jnp.int32, (t, t), 1)
    chunk_causal = (rt >= ct) & (rt // CHUNK == ct // CHUNK)
    cum = _sel_left(chunk_causal.astype(BF16), a)
    cum_t = _transpose_sel(cum)
    ecum = jnp.exp(cum)
    wend = jnp.concatenate([jnp.exp(cum[(j + 1) * CHUNK - 1:(j + 1) * CHUNK, :] - cum[rows[j]]) for j in subs], axis=0)
    dt_x = _sel_right(dt, e, 2)
    ecum_x = _sel_right(ecum, e, 2)
    wend_x = _sel_right(wend, e, 2)

    xdt = xs * dt_x
    xdt_b = xdt.astype(BF16)
    xw_b = (xdt * wend_x).astype(BF16)
    causal = lax.broadcasted_iota(jnp.int32, (CHUNK, CHUNK), 0) >= lax.broadcasted_iota(jnp.int32, (CHUNK, CHUNK), 1)
    left = lax.broadcasted_iota(jnp.int32, (CHUNK, LANES), 1) < SSD_P

    jg = [(j, g) for j in subs for g in groups]
    bmat = {(j, g): conv_ref[rows[j], SSD_W + g * SSD_N:SSD_W + (g + 1) * SSD_N].astype(BF16) for j, g in jg}
    cmat = {(j, g): conv_ref[rows[j], SSD_W + (SSD_GROUPS + g) * SSD_N:SSD_W + (SSD_GROUPS + g + 1) * SSD_N].astype(BF16)
            for j, g in jg}
    cb = {p: _dot_nt(cmat[p], bmat[p]) for p in jg}
    inc = {(j, g): _dot_tn(bmat[j, g], xw_b[rows[j], gs[g]]) for j, g in jg}
    scores = {}
    for j, g in jg:
        for blk in blocks:
            for half in range(2):
                h = (g * len(blocks) + blk) * 2 + half
                diff = cum[rows[j], h:h + 1] - cum_t[h:h + 1, rows[j]]
                lmat = jnp.where(causal, jnp.exp(jnp.minimum(diff, 0.0)), 0.0)
                scores[j, g, blk, half] = (cb[j, g] * lmat).astype(BF16)
    intra = {}
    for j, g in jg:
        for blk in blocks:
            lanes = slice(g * SSD_GW + blk * LANES, g * SSD_GW + (blk + 1) * LANES)
            xb = xdt_b[rows[j], lanes]
            zero = jnp.zeros_like(xb)
            intra[j, g, blk] = (_dot(scores[j, g, blk, 0], jnp.where(left, xb, zero))
                                + _dot(scores[j, g, blk, 1], jnp.where(left, zero, xb)))

    state = [h_ref[:, gs[g]] for g in groups]
    for j in subs:
        inter = [_dot(cmat[j, g], state[g].astype(BF16)) * ecum_x[rows[j], gs[g]] for g in groups]
        last = (j + 1) * CHUNK - 1
        state = [state[g] * ecum_x[last:last + 1, gs[g]] + inc[j, g] for g in groups]
        y = jnp.concatenate([intra[j, g, blk] + inter[g][:, blk * LANES:(blk + 1) * LANES]
                             for g in groups for blk in blocks], axis=1) + dexp_ref[...] * xs[rows[j]]
        msq = _sel_right(_sel_right_nt(y * y, e, 2), e, 2)
        y_ref[rows[j], :] = _head_norm_gate(y, msq, SSD_P, z_ref[rows[j], :], mixw_ref[...]).astype(BF16)
    for g in groups:
        h_ref[:, gs[g]] = state[g]

    @pl.when(c == pl.num_programs(1) - 1)
    def _():
        state_ref[0] = h_ref[...].T.reshape(SSD_HEADS, SSD_P, SSD_N)


def _ssd_prompt(proj, small, batch, cw, cb, dtb, alog, dexp, mixw, e, tile):
    rows = proj.shape[0]
    nc = rows // batch // tile
    row = lambda b, c: (b * nc + c, 0)
    whole = lambda b, c: (0, 0)
    return pl.pallas_call(
        _ssd_prompt_kernel,
        grid=(batch, nc),
        in_specs=[
            pl.BlockSpec((tile, SSD_CONV), lambda b, c: (b * nc + c, COL_XBC // SSD_CONV)),
            pl.BlockSpec((tile, LANES), row),
            pl.BlockSpec((tile, SSD_W), lambda b, c: (b * nc + c, COL_Z // SSD_W)),
            pl.BlockSpec(cw.shape, whole), pl.BlockSpec(cb.shape, whole), pl.BlockSpec(dtb.shape, whole),
            pl.BlockSpec(alog.shape, whole), pl.BlockSpec(dexp.shape, whole), pl.BlockSpec(mixw.shape, whole),
            pl.BlockSpec(e.shape, whole),
        ],
        out_specs=[
            pl.BlockSpec((tile, SSD_W), row),
            pl.BlockSpec((1, SUBLANES, SSD_CONV), lambda b, c: (b, 0, 0)),
            pl.BlockSpec((1, SSD_HEADS, SSD_P, SSD_N), lambda b, c: (b, 0, 0, 0)),
        ],
        out_shape=[
            jax.ShapeDtypeStruct((rows, SSD_W), BF16),
            jax.ShapeDtypeStruct((batch, SUBLANES, SSD_CONV), F32),
            jax.ShapeDtypeStruct((batch, SSD_HEADS, SSD_P, SSD_N), F32),
        ],
        scratch_shapes=[pltpu.VMEM((tile + SUBLANES, SSD_CONV), F32), pltpu.VMEM((tile, SSD_CONV), F32),
                        pltpu.VMEM((SSD_N, SSD_W), F32)],
        compiler_params=_params("parallel", "arbitrary"),
        name="ssd_prompt",
    )(proj, small, proj, cw, cb, dtb, alog, dexp, mixw, e)


def _unit_lower_inverses(a_stricts, ri, ci):
    t = a_stricts[0].shape[0]
    eye = _eye(t)
    first = (ri == ci + 1) & (ci % 2 == 0)
    invs = [eye - jnp.where(first, a, 0.0) for a in a_stricts]
    s = 2
    while s < t:
        sel = (ri // (2 * s) == ci // (2 * s)) & ((ri // s) % 2 == 1) & ((ci // s) % 2 == 0)
        inv_bs = [inv.astype(BF16) for inv in invs]
        lefts = [_dot(inv_b, jnp.where(sel, a, 0.0).astype(BF16)).astype(BF16) for inv_b, a in zip(inv_bs, a_stricts)]
        invs = [inv - _dot(left, inv_b) for inv, left, inv_b in zip(invs, lefts, inv_bs)]
        s *= 2
    return invs


def _gdn_prompt_kernel(qkv_ref, sm_ref, z_ref, cw_ref, gb_ref, galog_ref, mixw_ref,
                       y_ref, tail_ref, state_ref, ubuf_ref, conv_ref, s_ref):
    c = pl.program_id(1)
    t = qkv_ref.shape[0]
    subs = range(t // CHUNK)
    heads = range(GDN_HEADS)

    @pl.when(c == 0)
    def _():
        ubuf_ref[0:SUBLANES, :] = jnp.zeros((SUBLANES, GDN_CONV), F32)
        s_ref[...] = jnp.zeros_like(s_ref)

    _causal_conv_tile(qkv_ref, ubuf_ref, cw_ref, None, conv_ref, tail_ref)

    sm = sm_ref[...]
    beta = _sigmoid(sm)
    g = -jnp.exp(galog_ref[...]) * _softplus(sm + gb_ref[...])
    rt = lax.broadcasted_iota(jnp.int32, (t, t), 0)
    ct = lax.broadcasted_iota(jnp.int32, (t, t), 1)
    chunk_causal = (rt >= ct) & (rt // CHUNK == ct // CHUNK)
    gc = _sel_left(chunk_causal.astype(BF16), g)
    gc_t = _transpose_sel(gc)
    eg = jnp.exp(gc)
    ri = lax.broadcasted_iota(jnp.int32, (CHUNK, CHUNK), 0)
    ci = lax.broadcasted_iota(jnp.int32, (CHUNK, CHUNK), 1)
    causal = ri >= ci
    strict = ri > ci

    rows = [slice(j * CHUNK, (j + 1) * CHUNK) for j in subs]
    hs = [slice(h * GDN_D, (h + 1) * GDN_D) for h in heads]
    la = [SM_A + h for h in heads]
    pairs = [(j, h) for j in subs for h in heads]
    q, k, kb, vb, kbg, qg, decay = {}, {}, {}, {}, {}, {}, {}
    for h in heads:
        qf = conv_ref[:, h * GDN_D:(h + 1) * GDN_D]
        kf = conv_ref[:, GDN_W + h * GDN_D:GDN_W + (h + 1) * GDN_D]
        vf = conv_ref[:, 2 * GDN_W + h * GDN_D:2 * GDN_W + (h + 1) * GDN_D]
        qf = qf * lax.rsqrt(jnp.sum(qf * qf, axis=-1, keepdims=True) + EPS) * (GDN_D ** -0.5)
        kf = kf * lax.rsqrt(jnp.sum(kf * kf, axis=-1, keepdims=True) + EPS)
        b_col = beta[:, SM_B + h:SM_B + h + 1]
        eg_col = eg[:, la[h]:la[h] + 1]
        kbf = kf * b_col
        vbf, kbgf, qgf = (vf * b_col).astype(BF16), (kbf * eg_col).astype(BF16), (qf * eg_col).astype(BF16)
        for j in subs:
            q[j, h], k[j, h], kb[j, h] = qf[rows[j]].astype(BF16), kf[rows[j]], kbf[rows[j]].astype(BF16)
            vb[j, h], kbg[j, h], qg[j, h] = vbf[rows[j]], kbgf[rows[j]], qgf[rows[j]]
            diff = gc[rows[j], la[h]:la[h] + 1] - gc_t[la[h]:la[h] + 1, rows[j]]
            decay[j, h] = jnp.where(causal, jnp.exp(jnp.minimum(diff, 0.0)), 0.0)
    k_b = {p: k[p].astype(BF16) for p in pairs}
    kk = {p: _dot_nt(kb[p], k_b[p]) for p in pairs}
    qk = {p: _dot_nt(q[p], k_b[p]) for p in pairs}
    a_strict = [jnp.where(strict, kk[p] * decay[p], 0.0) for p in pairs]
    attn = {p: (qk[p] * decay[p]).astype(BF16) for p in pairs}
    t_inv = dict(zip(pairs, [x.astype(BF16) for x in _unit_lower_inverses(a_strict, ri, ci)]))
    u = {p: _dot(t_inv[p], vb[p]) for p in pairs}
    wk = {p: _dot(t_inv[p], kbg[p]).astype(BF16) for p in pairs}

    state = [s_ref[h] for h in heads]
    for j in subs:
        g_last = gc[(j + 1) * CHUNK - 1:(j + 1) * CHUNK, :]
        eend = jnp.exp(g_last - gc[rows[j]])
        elast = jnp.exp(g_last)
        s_b = [x.astype(BF16) for x in state]
        v_new = [(u[j, h] - _dot(wk[j, h], s_b[h])).astype(BF16) for h in heads]
        k_end = [(k[j, h] * eend[:, la[h]:la[h] + 1]).astype(BF16) for h in heads]
        s_inc = [_dot_tn(k_end[h], v_new[h]) for h in heads]
        state = [state[h] * elast[:, la[h]:la[h] + 1] + s_inc[h] for h in heads]
        o = [_dot(qg[j, h], s_b[h]) + _dot(attn[j, h], v_new[h]) for h in heads]
        msq = [jnp.sum(x * x, axis=-1, keepdims=True) for x in o]
        for h in heads:
            y_ref[rows[j], hs[h]] = _head_norm_gate(o[h], msq[h], GDN_D, z_ref[rows[j], hs[h]], mixw_ref[:, hs[h]]).astype(BF16)
    for h in heads:
        s_ref[h] = state[h]

    @pl.when(c == pl.num_programs(1) - 1)
    def _():
        state_ref[0] = s_ref[...]


def _gdn_prompt(proj, small, batch, cw, gb, galog, mixw, tile):
    rows = proj.shape[0]
    nc = rows // batch // tile
    row = lambda b, c: (b * nc + c, 0)
    whole = lambda b, c: (0, 0)
    return pl.pallas_call(
        _gdn_prompt_kernel,
        grid=(batch, nc),
        in_specs=[
            pl.BlockSpec((tile, GDN_CONV), lambda b, c: (b * nc + c, COL_QKV // GDN_CONV)),
            pl.BlockSpec((tile, LANES), row),
            pl.BlockSpec((tile, GDN_W), lambda b, c: (b * nc + c, (COL_Z + SSD_W) // GDN_W)),
            pl.BlockSpec(cw.shape, whole), pl.BlockSpec(gb.shape, whole), pl.BlockSpec(galog.shape, whole),
            pl.BlockSpec(mixw.shape, whole),
        ],
        out_specs=[
            pl.BlockSpec((tile, GDN_W), row),
            pl.BlockSpec((1, SUBLANES, GDN_CONV), lambda b, c: (b, 0, 0)),
            pl.BlockSpec((1, GDN_HEADS, GDN_D, GDN_D), lambda b, c: (b, 0, 0, 0)),
        ],
        out_shape=[
            jax.ShapeDtypeStruct((rows, GDN_W), BF16),
            jax.ShapeDtypeStruct((batch, SUBLANES, GDN_CONV), F32),
            jax.ShapeDtypeStruct((batch, GDN_HEADS, GDN_D, GDN_D), F32),
        ],
        scratch_shapes=[pltpu.VMEM((tile + SUBLANES, GDN_CONV), F32), pltpu.VMEM((tile, GDN_CONV), F32),
                        pltpu.VMEM((GDN_HEADS, GDN_D, GDN_D), F32)],
        compiler_params=_params("parallel", "arbitrary"),
        name="gdn_prompt",
    )(proj, small, proj, cw, gb, galog, mixw)


def _mem_attention_head(q, k, v):
    s = _dot_nt(q.astype(BF16), k.astype(BF16)) * (MEM_D ** -0.5)
    e = jnp.exp(s - jnp.max(s, axis=-1, keepdims=True))
    p = e / jnp.sum(e, axis=-1, keepdims=True)
    return _dot(p.astype(BF16), v.astype(BF16))


def _mem_prompt_kernel(q_ref, k_ref, v_ref, z_ref, mixw_ref, y_ref):
    for h in range(MEM_HEADS):
        hs = slice(h * MEM_D, (h + 1) * MEM_D)
        o = _mem_attention_head(q_ref[:, hs], k_ref[:, hs], v_ref[:, hs])
        msq = jnp.sum(o * o, axis=-1, keepdims=True)
        y_ref[:, hs] = _head_norm_gate(o, msq, MEM_D, z_ref[:, hs], mixw_ref[:, hs]).astype(BF16)


def _mem_prompt(proj, kv, batch, mixw, tq):
    rows = proj.shape[0]
    nq = rows // batch // tq
    return pl.pallas_call(
        _mem_prompt_kernel,
        grid=(batch, nq),
        in_specs=[
            pl.BlockSpec((tq, MEM_W), lambda b, i: (b * nq + i, COL_QMEM // MEM_W)),
            pl.BlockSpec((MEM_TOKENS, MEM_W), lambda b, i: (b, 0)),
            pl.BlockSpec((MEM_TOKENS, MEM_W), lambda b, i: (b, 1)),
            pl.BlockSpec((tq, MEM_W), lambda b, i: (b * nq + i, (COL_Z + SSD_W + GDN_W) // MEM_W)),
            pl.BlockSpec(mixw.shape, lambda b, i: (0, 0)),
        ],
        out_specs=pl.BlockSpec((tq, MEM_W), lambda b, i: (b * nq + i, 0)),
        out_shape=jax.ShapeDtypeStruct((rows, MEM_W), BF16),
        compiler_params=_params("parallel", "parallel"),
        name="mem_prompt",
    )(proj, kv, kv, proj, mixw)


DEC_ROWS = SUBLANES


def _conv_step(u, st, cw_ref, bias, width):
    acc = cw_ref[CONV_K - 1:CONV_K, :] * u
    if bias is not None:
        acc = acc + bias
    for j in range(CONV_K - 1):
        acc = acc + cw_ref[j:j + 1, :] * st[:, j * width:(j + 1) * width]
    return _silu(acc), jnp.concatenate([st[:, width:], u], axis=1)


def _rows_to_columns(x):
    pad = jnp.zeros((LANES - x.shape[0], x.shape[1]), F32)
    return jnp.concatenate([x, pad], axis=0).T


def _pick_rows(parts):
    rid = lax.broadcasted_iota(jnp.int32, parts[0].shape, 0)
    out = parts[0]
    for i in range(1, len(parts)):
        out = jnp.where(rid == i, parts[i], out)
    return out


def _ssd_decode_kernel(xbc_ref, sm_ref, z_ref, cst_ref, st_ref, cw_ref, cb_ref, dtb_ref, alog_ref, dexp_ref,
                       mixw_ref, e_ref, en_ref, y_ref, cst_out_ref, st_out_ref):
    xbc, cst_new = _conv_step(xbc_ref[...], cst_ref[...], cw_ref, cb_ref[...], SSD_CONV)
    cst_out_ref[...] = cst_new
    xs = xbc[:, :SSD_W]
    e = e_ref[...]
    dt = _softplus(sm_ref[...] + dtb_ref[...])
    dec = jnp.exp(dt * (-jnp.exp(alog_ref[...])))
    xd_t = _rows_to_columns(xs * _sel_right(dt, e))
    dec_n = _sel_right(dec, en_ref[...])

    y_groups = []
    for g in range(SSD_GROUPS):
        b_g = xbc[:, SSD_W + g * SSD_N:SSD_W + (g + 1) * SSD_N]
        c_g = xbc[:, SSD_W + (SSD_GROUPS + g) * SSD_N:SSD_W + (SSD_GROUPS + g + 1) * SSD_N].astype(BF16)
        per_row = []
        for i in range(DEC_ROWS):
            new = []
            for r in range(SSD_HEADS // SSD_GROUPS):
                h = g * (SSD_HEADS // SSD_GROUPS) + r
                col = xd_t[h * SSD_P:(h + 1) * SSD_P, i:i + 1]
                hn = st_ref[i, h] * dec_n[i:i + 1, h * SSD_N:(h + 1) * SSD_N] + col * b_g[i:i + 1, :]
                st_out_ref[i, h] = hn
                new.append(hn)
            hg = jnp.concatenate(new, axis=0).astype(BF16)
            per_row.append(_dot_nt(c_g, hg))
        y_groups.append(_pick_rows(per_row))
    y = jnp.concatenate(y_groups, axis=1) + dexp_ref[...] * xs
    msq = _sel_right(_sel_right_nt(y * y, e, 2), e, 2)
    y_ref[...] = _head_norm_gate(y, msq, SSD_P, z_ref[...], mixw_ref[...])


def _ssd_decode(proj, small, cst, st, cw, cb, dtb, alog, dexp, mixw, e, en):
    rows = proj.shape[0]
    row = lambda i: (i, 0)
    whole = lambda i: (0, 0)
    return pl.pallas_call(
        _ssd_decode_kernel,
        grid=(rows // DEC_ROWS,),
        in_specs=[
            pl.BlockSpec((DEC_ROWS, SSD_CONV), lambda i: (i, COL_XBC // SSD_CONV)),
            pl.BlockSpec((DEC_ROWS, LANES), row),
            pl.BlockSpec((DEC_ROWS, SSD_W), lambda i: (i, COL_Z // SSD_W)),
            pl.BlockSpec((DEC_ROWS, cst.shape[1]), row),
            pl.BlockSpec((DEC_ROWS, SSD_HEADS, SSD_P, SSD_N), lambda i: (i, 0, 0, 0)),
            pl.BlockSpec(cw.shape, whole), pl.BlockSpec(cb.shape, whole), pl.BlockSpec(dtb.shape, whole),
            pl.BlockSpec(alog.shape, whole), pl.BlockSpec(dexp.shape, whole), pl.BlockSpec(mixw.shape, whole),
            pl.BlockSpec(e.shape, whole), pl.BlockSpec(en.shape, whole),
        ],
        out_specs=[
            pl.BlockSpec((DEC_ROWS, SSD_W), row),
            pl.BlockSpec((DEC_ROWS, cst.shape[1]), row),
            pl.BlockSpec((DEC_ROWS, SSD_HEADS, SSD_P, SSD_N), lambda i: (i, 0, 0, 0)),
        ],
        out_shape=[
            jax.ShapeDtypeStruct((rows, SSD_W), F32),
            jax.ShapeDtypeStruct(cst.shape, F32),
            jax.ShapeDtypeStruct(st.shape, F32),
        ],
        compiler_params=_params("parallel"),
        name="ssd_decode",
    )(proj, small, proj, cst, st, cw, cb, dtb, alog, dexp, mixw, e, en)


def _gdn_decode_kernel(qkv_ref, sm_ref, z_ref, cst_ref, st_ref, cw_ref, gb_ref, galog_ref, mixw_ref, en_ref,
                       y_ref, cst_out_ref, st_out_ref):
    qkv, cst_new = _conv_step(qkv_ref[...], cst_ref[...], cw_ref, None, GDN_CONV)
    cst_out_ref[...] = cst_new
    sm = sm_ref[...]
    beta = _sigmoid(sm)
    eg = jnp.exp(-jnp.exp(galog_ref[...]) * _softplus(sm + gb_ref[...]))
    eg_n = _sel_right(eg, en_ref[...])

    qs, ks = [], []
    for h in range(GDN_HEADS):
        q = qkv[:, h * GDN_D:(h + 1) * GDN_D]
        k = qkv[:, GDN_W + h * GDN_D:GDN_W + (h + 1) * GDN_D]
        qs.append(q * lax.rsqrt(jnp.sum(q * q, axis=-1, keepdims=True) + EPS) * (GDN_D ** -0.5))
        ks.append(k * lax.rsqrt(jnp.sum(k * k, axis=-1, keepdims=True) + EPS))
    k_t = _rows_to_columns(jnp.concatenate(ks, axis=1))

    for h in range(GDN_HEADS):
        hs = slice(h * GDN_D, (h + 1) * GDN_D)
        q, k = qs[h], ks[h]
        v = qkv[:, 2 * GDN_W + h * GDN_D:2 * GDN_W + (h + 1) * GDN_D]
        q_b, k_b = q.astype(BF16), k.astype(BF16)
        ks_rows, qs_rows = [], []
        for i in range(DEC_ROWS):
            s_b = st_ref[i, h].astype(BF16)
            ks_rows.append(_dot(k_b, s_b))
            qs_rows.append(_dot(q_b, s_b))
        k_s, q_s = _pick_rows(ks_rows), _pick_rows(qs_rows)
        eg_h = eg_n[:, hs]
        v_new = beta[:, SM_B + h:SM_B + h + 1] * (v - eg_h * k_s)
        o = eg_h * q_s + jnp.sum(q * k, axis=-1, keepdims=True) * v_new
        for i in range(DEC_ROWS):
            col = k_t[hs, i:i + 1]
            st_out_ref[i, h] = st_ref[i, h] * eg_h[i:i + 1, :] + col * v_new[i:i + 1, :]
        msq = jnp.sum(o * o, axis=-1, keepdims=True)
        y_ref[:, hs] = _head_norm_gate(o, msq, GDN_D, z_ref[:, hs], mixw_ref[:, hs])


def _gdn_decode(proj, small, cst, st, cw, gb, galog, mixw, en):
    rows = proj.shape[0]
    row = lambda i: (i, 0)
    whole = lambda i: (0, 0)
    return pl.pallas_call(
        _gdn_decode_kernel,
        grid=(rows // DEC_ROWS,),
        in_specs=[
            pl.BlockSpec((DEC_ROWS, GDN_CONV), lambda i: (i, COL_QKV // GDN_CONV)),
            pl.BlockSpec((DEC_ROWS, LANES), row),
            pl.BlockSpec((DEC_ROWS, GDN_W), lambda i: (i, (COL_Z + SSD_W) // GDN_W)),
            pl.BlockSpec((DEC_ROWS, cst.shape[1]), row),
            pl.BlockSpec((DEC_ROWS, GDN_HEADS, GDN_D, GDN_D), lambda i: (i, 0, 0, 0)),
            pl.BlockSpec(cw.shape, whole), pl.BlockSpec(gb.shape, whole), pl.BlockSpec(galog.shape, whole),
            pl.BlockSpec(mixw.shape, whole), pl.BlockSpec(en.shape, whole),
        ],
        out_specs=[
            pl.BlockSpec((DEC_ROWS, GDN_W), row),
            pl.BlockSpec((DEC_ROWS, cst.shape[1]), row),
            pl.BlockSpec((DEC_ROWS, GDN_HEADS, GDN_D, GDN_D), lambda i: (i, 0, 0, 0)),
        ],
        out_shape=[
            jax.ShapeDtypeStruct((rows, GDN_W), F32),
            jax.ShapeDtypeStruct(cst.shape, F32),
            jax.ShapeDtypeStruct(st.shape, F32),
        ],
        compiler_params=_params("parallel"),
        name="gdn_decode",
    )(proj, small, proj, cst, st, cw, gb, galog, mixw, en)


def _mem_decode_kernel(q_ref, k_ref, v_ref, z_ref, mixw_ref, y_ref):
    heads, rows = range(MEM_HEADS), range(DEC_ROWS)
    hs = [slice(h * MEM_D, (h + 1) * MEM_D) for h in heads]
    win = [pl.ds(h, MEM_TOKENS, stride=MEM_HEADS) for h in heads]
    q = [q_ref[:, hs[h]].astype(BF16) for h in heads]
    s = [_pick_rows([_dot_nt(q[h], k_ref[i, win[h], :].astype(BF16)) for i in rows]) * (MEM_D ** -0.5) for h in heads]
    e = [jnp.exp(x - jnp.max(x, axis=-1, keepdims=True)) for x in s]
    p = [(x / jnp.sum(x, axis=-1, keepdims=True)).astype(BF16) for x in e]
    o = [_pick_rows([_dot(p[h], v_ref[i, win[h], :].astype(BF16)) for i in rows]) for h in heads]
    msq = [jnp.sum(x * x, axis=-1, keepdims=True) for x in o]
    for h in heads:
        y_ref[:, hs[h]] = _head_norm_gate(o[h], msq[h], MEM_D, z_ref[:, hs[h]], mixw_ref[:, hs[h]])


def _mem_decode(proj, mem_k, mem_v, mixw):
    rows = proj.shape[0]
    return pl.pallas_call(
        _mem_decode_kernel,
        grid=(rows // DEC_ROWS,),
        in_specs=[
            pl.BlockSpec((DEC_ROWS, MEM_W), lambda i: (i, COL_QMEM // MEM_W)),
            pl.BlockSpec((DEC_ROWS, MEM_TOKENS * MEM_HEADS, MEM_D), lambda i: (i, 0, 0)),
            pl.BlockSpec((DEC_ROWS, MEM_TOKENS * MEM_HEADS, MEM_D), lambda i: (i, 0, 0)),
            pl.BlockSpec((DEC_ROWS, MEM_W), lambda i: (i, (COL_Z + SSD_W + GDN_W) // MEM_W)),
            pl.BlockSpec(mixw.shape, lambda i: (0, 0)),
        ],
        out_specs=pl.BlockSpec((DEC_ROWS, MEM_W), lambda i: (i, 0)),
        out_shape=jax.ShapeDtypeStruct((rows, MEM_W), F32),
        compiler_params=_params("parallel"),
        name="mem_decode",
    )(proj, mem_k, mem_v, proj, mixw)


def _head_expander(heads, first_lane, width):
    m = np.zeros((LANES, heads * width), np.float32)
    for h in range(heads):
        m[first_lane + h, h * width:(h + 1) * width] = 1.0
    return jnp.asarray(m, BF16)


def _lane_row(vec, first_lane):
    return jnp.zeros((1, LANES), F32).at[0, first_lane:first_lane + vec.shape[0]].set(vec.astype(F32))


def kernel(x_prompt, x_sample, mem_prompt, state_ssd_conv, state_ssd, state_gdn_conv, state_gdn, cache_mem_k, cache_mem_v, norm_w, w_in, ssd_conv_w, ssd_conv_b, ssd_dt_bias, ssd_A_log, ssd_D, gdn_conv_w, gdn_dt_bias, gdn_A_log, mem_norm_w, w_mem_kv, mix_norm_w, w_out, final_norm_w):
    bp, seq, d = x_prompt.shape
    bs = x_sample.shape[0]
    assert (d, seq % CHUNK, bs % DEC_ROWS, norm_w.shape[0]) == (D_MODEL, 0, 0, 1)

    w = w_in[0]
    o_dt = SSD_CONV
    o_qkv = o_dt + SSD_HEADS
    o_b = o_qkv + GDN_CONV
    o_a = o_b + GDN_HEADS
    o_qm = o_a + GDN_HEADS
    o_z = o_qm + MEM_W
    w_main = jnp.concatenate([w[:, o_qkv:o_b], w[:, :o_dt], w[:, o_qm:o_z], w[:, o_z:]], axis=1).astype(BF16)
    w_small = jnp.concatenate(
        [w[:, o_dt:o_qkv], w[:, o_b:o_a], w[:, o_a:o_qm], jnp.zeros((d, LANES - SSD_HEADS - 2 * GDN_HEADS), F32)], axis=1
    ).astype(BF16)
    wo = w_out[0].astype(BF16)
    wo1, wo2, wo3 = wo[:SSD_W], wo[SSD_W:SSD_W + GDN_W], wo[SSD_W + GDN_W:]
    nw = norm_w[0][None, :]
    mixw = mix_norm_w[0][None, :]
    mixw1, mixw2, mixw3 = mixw[:, :SSD_W], mixw[:, SSD_W:SSD_W + GDN_W], mixw[:, SSD_W + GDN_W:]
    fw = final_norm_w[None, :]
    ssd_dtb = _lane_row(ssd_dt_bias[0], SM_DT)
    ssd_alog = _lane_row(ssd_A_log[0], SM_DT)
    ssd_dexp = jnp.repeat(ssd_D[0].astype(F32), SSD_P)[None, :]
    gdn_b = _lane_row(gdn_dt_bias[0], SM_A)
    gdn_alog = _lane_row(gdn_A_log[0], SM_A)
    e_ssd = _head_expander(SSD_HEADS, SM_DT, SSD_P)
    e_ssd_n = _head_expander(SSD_HEADS, SM_DT, SSD_N)
    e_gdn_n = _head_expander(GDN_HEADS, SM_A, GDN_D)
    ssd_cw, ssd_cb, gdn_cw = ssd_conv_w[0], ssd_conv_b[0][None, :], gdn_conv_w[0]

    xp = x_prompt.reshape(bp * seq, d)
    proj_p, small_p = _norm_matmul(xp, nw, w_main, w_small, _row_tile(bp * seq, PROJ_ROWS), SSD_CONV)
    kv, _ = _norm_matmul(mem_prompt.reshape(bp * MEM_TOKENS, d), mem_norm_w[0][None, :], w_mem_kv[0].astype(BF16),
                         jnp.zeros((d, LANES), BF16), _row_tile(bp * MEM_TOKENS, PROJ_ROWS), 2 * MEM_W)
    scan_rows = SCAN_ROWS if seq % SCAN_ROWS == 0 else CHUNK
    y_ssd, tail_ssd, p_ssd = _ssd_prompt(proj_p, small_p, bp, ssd_cw, ssd_cb, ssd_dtb, ssd_alog, ssd_dexp, mixw1, e_ssd,
                                         scan_rows)
    y_gdn, tail_gdn, p_gdn = _gdn_prompt(proj_p, small_p, bp, gdn_cw, gdn_b, gdn_alog, mixw2, scan_rows)
    y_mem = _mem_prompt(proj_p, kv, bp, mixw3, _row_tile(seq, MEM_Q_ROWS))
    y_prompt = _out_proj(y_ssd, y_gdn, y_mem, wo1, wo2, wo3, xp, fw, _row_tile(bp * seq, OUT_ROWS)).reshape(bp, seq, d)

    xs = x_sample.reshape(bs, d)
    proj_s, small_s = _norm_matmul(xs, nw, w_main, w_small, bs, SSD_CONV)
    ys_ssd, s_ssd_conv, s_ssd = _ssd_decode(proj_s, small_s, state_ssd_conv[0].reshape(bs, -1), state_ssd[0],
                                            ssd_cw, ssd_cb, ssd_dtb, ssd_alog, ssd_dexp, mixw1, e_ssd, e_ssd_n)
    ys_gdn, s_gdn_conv, s_gdn = _gdn_decode(proj_s, small_s, state_gdn_conv[0].reshape(bs, -1), state_gdn[0],
                                            gdn_cw, gdn_b, gdn_alog, mixw2, e_gdn_n)
    ys_mem = _mem_decode(proj_s, cache_mem_k.reshape(bs, MEM_TOKENS * MEM_HEADS, MEM_D),
                         cache_mem_v.reshape(bs, MEM_TOKENS * MEM_HEADS, MEM_D), mixw3)
    y_sample = _out_proj(ys_ssd, ys_gdn, ys_mem, wo1, wo2, wo3, xs, fw, bs).reshape(bs, 1, d)

    keep = CONV_K - 1
    mem_shape = (1, bp, MEM_TOKENS, MEM_HEADS, MEM_D)
    return (
        y_prompt, y_sample,
        tail_ssd[None, :, SUBLANES - keep:, :], p_ssd[None],
        tail_gdn[None, :, SUBLANES - keep:, :], p_gdn[None],
        kv[:, :MEM_W].reshape(mem_shape), kv[:, MEM_W:].reshape(mem_shape),
        s_ssd_conv.reshape(1, bs, keep, SSD_CONV), s_ssd[None],
        s_gdn_conv.reshape(1, bs, keep, GDN_CONV), s_gdn[None],
    )
```

```python
import functools

import numpy as np
import jax
import jax.numpy as jnp
from jax import lax
from jax.experimental import pallas as pl
from jax.experimental.pallas import tpu as pltpu

F32, BF16 = jnp.float32, jnp.bfloat16

D_MODEL = 2048
SSD_HEADS, SSD_P, SSD_GROUPS, SSD_N = 16, 64, 2, 128
SSD_W = SSD_HEADS * SSD_P
SSD_GW = SSD_W // SSD_GROUPS
SSD_CONV = SSD_W + 2 * SSD_GROUPS * SSD_N
GDN_HEADS, GDN_D = 8, 128
GDN_W = GDN_HEADS * GDN_D
GDN_CONV = 3 * GDN_W
MEM_TOKENS, MEM_HEADS, MEM_D = 256, 4, 128
MEM_W = MEM_HEADS * MEM_D
MIX_W = SSD_W + GDN_W + MEM_W
CONV_K = 4
CHUNK = 64
EPS = 1e-6

LANES = 128
SUBLANES = 8
VMEM_LIMIT = 56 * 1024 * 1024
PROJ_ROWS = 1024
OUT_ROWS = 512
MEM_Q_ROWS = 256
SCAN_ROWS = 256
CONV_PHASES = 4

COL_QKV = 0
COL_XBC = COL_QKV + GDN_CONV
COL_QMEM = COL_XBC + SSD_CONV
COL_Z = COL_QMEM + MEM_W
N_MAIN = COL_Z + MIX_W
SM_DT, SM_B, SM_A = 0, SSD_HEADS, SSD_HEADS + GDN_HEADS


def _dot(a, b):
    return jnp.dot(a, b, preferred_element_type=F32)


def _dot_nt(a, b):
    return lax.dot_general(a, b, (((1,), (1,)), ((), ())), preferred_element_type=F32)


def _dot_tn(a, b):
    return lax.dot_general(a, b, (((0,), (0,)), ((), ())), preferred_element_type=F32)


def _split(x, n):
    parts, r = [], x
    for i in range(n):
        p = r.astype(BF16)
        parts.append(p)
        if i + 1 < n:
            r = r - p.astype(F32)
    return parts


def _sel_left(sel, x, n=3):
    return functools.reduce(lambda a, b: a + b, [_dot(sel, p) for p in _split(x, n)])


def _sel_right(x, sel, n=3):
    return functools.reduce(lambda a, b: a + b, [_dot(p, sel) for p in _split(x, n)])


def _sel_right_nt(x, sel, n=3):
    return functools.reduce(lambda a, b: a + b, [_dot_nt(p, sel) for p in _split(x, n)])


def _transpose_sel(x, n=3):
    eye = _eye(LANES).astype(BF16)
    return functools.reduce(lambda a, b: a + b, [_dot_nt(eye, p) for p in _split(x, n)])


def _eye(n):
    return (lax.broadcasted_iota(jnp.int32, (n, n), 0) == lax.broadcasted_iota(jnp.int32, (n, n), 1)).astype(F32)


def _sigmoid(x):
    return 1.0 / (1.0 + jnp.exp(-x))


def _silu(x):
    return x * _sigmoid(x)


def _softplus(x):
    return jnp.maximum(x, 0.0) + jnp.log1p(jnp.exp(-jnp.abs(x)))


def _params(*sem):
    return pltpu.CompilerParams(dimension_semantics=sem, vmem_limit_bytes=VMEM_LIMIT)


def _row_tile(rows, preferred):
    return preferred if rows % preferred == 0 else rows


def _norm_matmul_kernel(x_ref, nw_ref, w_ref, ws_ref, o_ref, os_ref, h_ref):
    @pl.when(pl.program_id(1) == 0)
    def _():
        x = x_ref[...]
        ms = jnp.mean(x * x, axis=-1, keepdims=True)
        h = (x * lax.rsqrt(ms + EPS) * nw_ref[...]).astype(BF16)
        h_ref[...] = h
        os_ref[...] = _dot(h, ws_ref[...])

    o_ref[...] = _dot(h_ref[...], w_ref[...])


def _norm_matmul(x, nw, w, ws, tm, tn):
    m, k = x.shape
    n = w.shape[1]
    ns = ws.shape[1]
    return pl.pallas_call(
        _norm_matmul_kernel,
        grid=(m // tm, n // tn),
        in_specs=[
            pl.BlockSpec((tm, k), lambda i, j: (i, 0)),
            pl.BlockSpec((1, k), lambda i, j: (0, 0)),
            pl.BlockSpec((k, tn), lambda i, j: (0, j)),
            pl.BlockSpec((k, ns), lambda i, j: (0, 0)),
        ],
        out_specs=[
            pl.BlockSpec((tm, tn), lambda i, j: (i, j)),
            pl.BlockSpec((tm, ns), lambda i, j: (i, 0)),
        ],
        out_shape=[jax.ShapeDtypeStruct((m, n), F32), jax.ShapeDtypeStruct((m, ns), F32)],
        scratch_shapes=[pltpu.VMEM((tm, k), BF16)],
        compiler_params=_params("parallel", "arbitrary"),
        name="norm_matmul",
    )(x, nw, w, ws)


def _out_proj_kernel(y1_ref, y2_ref, y3_ref, w1_ref, w2_ref, w3_ref, x_ref, fw_ref, o_ref):
    acc = (_dot(y1_ref[...].astype(BF16), w1_ref[...]) + _dot(y2_ref[...].astype(BF16), w2_ref[...])
           + _dot(y3_ref[...].astype(BF16), w3_ref[...]))
    r = x_ref[...] + acc
    ms = jnp.mean(r * r, axis=-1, keepdims=True)
    o_ref[...] = r * lax.rsqrt(ms + EPS) * fw_ref[...]


def _out_proj(y1, y2, y3, w1, w2, w3, x, fw, tm):
    m, d = x.shape
    row = lambda i: (i, 0)
    whole = lambda i: (0, 0)
    return pl.pallas_call(
        _out_proj_kernel,
        grid=(m // tm,),
        in_specs=[
            pl.BlockSpec((tm, y1.shape[1]), row), pl.BlockSpec((tm, y2.shape[1]), row), pl.BlockSpec((tm, y3.shape[1]), row),
            pl.BlockSpec(w1.shape, whole), pl.BlockSpec(w2.shape, whole), pl.BlockSpec(w3.shape, whole),
            pl.BlockSpec((tm, d), row), pl.BlockSpec((1, d), whole),
        ],
        out_specs=pl.BlockSpec((tm, d), row),
        out_shape=jax.ShapeDtypeStruct((m, d), F32),
        compiler_params=_params("parallel"),
        name="out_proj",
    )(y1, y2, y3, w1, w2, w3, x, fw)


def _causal_conv_tile(u_ref, ubuf_ref, cw_ref, cb_ref, out_ref, tail_ref):
    t, width = u_ref.shape
    n = t // CONV_PHASES
    for s in range(width // LANES):
        cs = slice(s * LANES, (s + 1) * LANES)
        ubuf_ref[s, SUBLANES:SUBLANES + t, :] = u_ref[:, cs]
        taps = {d: ubuf_ref[s, pl.ds(SUBLANES + d, n, stride=CONV_PHASES), :] for d in range(1 - CONV_K, CONV_PHASES)}
        w = [cw_ref[j:j + 1, cs] for j in range(CONV_K)]
        for r in range(CONV_PHASES):
            acc = w[CONV_K - 1] * taps[r]
            if cb_ref is not None:
                acc = acc + cb_ref[:, cs]
            for j in range(CONV_K - 1):
                acc = acc + w[j] * taps[r - (CONV_K - 1) + j]
            out_ref[s, pl.ds(r, n, stride=CONV_PHASES), :] = _silu(acc)
        tail = ubuf_ref[s, t:t + SUBLANES, :]
        ubuf_ref[s, 0:SUBLANES, :] = tail
        tail_ref[0, :, cs] = tail


def _head_norm_gate(y, msq, width, z, mixw):
    return y * lax.rsqrt(msq * (1.0 / width) + EPS) * mixw * _silu(z)


def _ssd_prompt_kernel(xbc_ref, sm_ref, z_ref, cw_ref, cb_ref, dtb_ref, alog_ref, dexp_ref, mixw_ref, e_ref,
                       y_ref, tail_ref, state_ref, ubuf_ref, conv_ref, h_ref):
    c = pl.program_id(1)
    t = xbc_ref.shape[0]
    subs = range(t // CHUNK)
    groups = range(SSD_GROUPS)
    blocks = range(SSD_GW // LANES)

    @pl.when(c == 0)
    def _():
        ubuf_ref[:, 0:SUBLANES, :] = jnp.zeros((SSD_CONV // LANES, SUBLANES, LANES), F32)
        h_ref[...] = jnp.zeros_like(h_ref)

    _causal_conv_tile(xbc_ref, ubuf_ref, cw_ref, cb_ref, conv_ref, tail_ref)
    xs = jnp.concatenate([conv_ref[s] for s in range(SSD_W // LANES)], axis=1)
    e = e_ref[...]
    rows = [slice(j * CHUNK, (j + 1) * CHUNK) for j in subs]
    gs = [slice(g * SSD_GW, (g + 1) * SSD_GW) for g in groups]

    dt = _softplus(sm_ref[...] + dtb_ref[...])
    a = dt * (-jnp.exp(alog_ref[...]))
    rt = lax.broadcasted_iota(jnp.int32, (t, t), 0)
    ct = lax.broadcasted_iota(jnp.int32, (t, t), 1)
    chunk_causal = (rt >= ct) & (rt // CHUNK == ct // CHUNK)
    cum = _sel_left(chunk_causal.astype(BF16), a)
    cum_t = _transpose_sel(cum)
    ecum = jnp.exp(cum)
    wend = jnp.concatenate([jnp.exp(cum[(j + 1) * CHUNK - 1:(j + 1) * CHUNK, :] - cum[rows[j]]) for j in subs], axis=0)
    dt_x = _sel_right(dt, e, 2)
    ecum_x = _sel_right(ecum, e, 2)
    wend_x = _sel_right(wend, e, 2)

    xdt = xs * dt_x
    xdt_b = xdt.astype(BF16)
    xw_b = (xdt * wend_x).astype(BF16)
    causal = lax.broadcasted_iota(jnp.int32, (CHUNK, CHUNK), 0) >= lax.broadcasted_iota(jnp.int32, (CHUNK, CHUNK), 1)
    left = lax.broadcasted_iota(jnp.int32, (CHUNK, LANES), 1) < SSD_P

    jg = [(j, g) for j in subs for g in groups]
    b_slab, c_slab = SSD_W // LANES, SSD_W // LANES + SSD_GROUPS
    bmat = {(j, g): conv_ref[b_slab + g, rows[j], :].astype(BF16) for j, g in jg}
    cmat = {(j, g): conv_ref[c_slab + g, rows[j], :].astype(BF16) for j, g in jg}
    cb = {p: _dot_nt(cmat[p], bmat[p]) for p in jg}
    inc = {(j, g): _dot_tn(bmat[j, g], xw_b[rows[j], gs[g]]) for j, g in jg}
    scores = {}
    for j, g in jg:
        for blk in blocks:
            for half in range(2):
                h = (g * len(blocks) + blk) * 2 + half
                diff = cum[rows[j], h:h + 1] - cum_t[h:h + 1, rows[j]]
                lmat = jnp.where(causal, jnp.exp(jnp.minimum(diff, 0.0)), 0.0)
                scores[j, g, blk, half] = (cb[j, g] * lmat).astype(BF16)
    intra = {}
    for j, g in jg:
        for blk in blocks:
            lanes = slice(g * SSD_GW + blk * LANES, g * SSD_GW + (blk + 1) * LANES)
            xb = xdt_b[rows[j], lanes]
            zero = jnp.zeros_like(xb)
            intra[j, g, blk] = (_dot(scores[j, g, blk, 0], jnp.where(left, xb, zero))
                                + _dot(scores[j, g, blk, 1], jnp.where(left, zero, xb)))

    state = {(0, g): h_ref[:, gs[g]] for g in groups}
    for j in subs:
        last = (j + 1) * CHUNK - 1
        for g in groups:
            state[j + 1, g] = state[j, g] * ecum_x[last:last + 1, gs[g]] + inc[j, g]
    for g in groups:
        h_ref[:, gs[g]] = state[len(subs), g]
    inter = {p: _dot(cmat[p], state[p].astype(BF16)) for p in jg}
    inter_x = jnp.concatenate([jnp.concatenate([inter[j, g] for g in groups], axis=1) for j in subs], axis=0) * ecum_x
    intra_x = jnp.concatenate([jnp.concatenate([intra[j, g, blk] for g in groups for blk in blocks], axis=1)
                               for j in subs], axis=0)
    y = intra_x + inter_x + dexp_ref[...] * xs
    msq = _sel_right(_sel_right_nt(y * y, e, 2), e, 2)
    y_ref[...] = _head_norm_gate(y, msq, SSD_P, z_ref[...], mixw_ref[...]).astype(BF16)

    @pl.when(c == pl.num_programs(1) - 1)
    def _():
        state_ref[0] = h_ref[...].T.reshape(SSD_HEADS, SSD_P, SSD_N)


def _ssd_prompt(proj, small, batch, cw, cb, dtb, alog, dexp, mixw, e, tile):
    rows = proj.shape[0]
    nc = rows // batch // tile
    row = lambda b, c: (b * nc + c, 0)
    whole = lambda b, c: (0, 0)
    return pl.pallas_call(
        _ssd_prompt_kernel,
        grid=(batch, nc),
        in_specs=[
            pl.BlockSpec((tile, SSD_CONV), lambda b, c: (b * nc + c, COL_XBC // SSD_CONV)),
            pl.BlockSpec((tile, LANES), row),
            pl.BlockSpec((tile, SSD_W), lambda b, c: (b * nc + c, COL_Z // SSD_W)),
            pl.BlockSpec(cw.shape, whole), pl.BlockSpec(cb.shape, whole), pl.BlockSpec(dtb.shape, whole),
            pl.BlockSpec(alog.shape, whole), pl.BlockSpec(dexp.shape, whole), pl.BlockSpec(mixw.shape, whole),
            pl.BlockSpec(e.shape, whole),
        ],
        out_specs=[
            pl.BlockSpec((tile, SSD_W), row),
            pl.BlockSpec((1, SUBLANES, SSD_CONV), lambda b, c: (b, 0, 0)),
            pl.BlockSpec((1, SSD_HEADS, SSD_P, SSD_N), lambda b, c: (b, 0, 0, 0)),
        ],
        out_shape=[
            jax.ShapeDtypeStruct((rows, SSD_W), BF16),
            jax.ShapeDtypeStruct((batch, SUBLANES, SSD_CONV), F32),
            jax.ShapeDtypeStruct((batch, SSD_HEADS, SSD_P, SSD_N), F32),
        ],
        scratch_shapes=[pltpu.VMEM((SSD_CONV // LANES, tile + SUBLANES, LANES), F32),
                        pltpu.VMEM((SSD_CONV // LANES, tile, LANES), F32), pltpu.VMEM((SSD_N, SSD_W), F32)],
        compiler_params=_params("parallel", "arbitrary"),
        name="ssd_prompt",
    )(proj, small, proj, cw, cb, dtb, alog, dexp, mixw, e)


def _unit_lower_inverses(a_stricts, ri, ci):
    t = a_stricts[0].shape[0]
    eye = _eye(t)
    first = (ri == ci + 1) & (ci % 2 == 0)
    invs = [eye - jnp.where(first, a, 0.0) for a in a_stricts]
    s = 2
    while s < t:
        sel = (ri // (2 * s) == ci // (2 * s)) & ((ri // s) % 2 == 1) & ((ci // s) % 2 == 0)
        inv_bs = [inv.astype(BF16) for inv in invs]
        lefts = [_dot(inv_b, jnp.where(sel, a, 0.0).astype(BF16)).astype(BF16) for inv_b, a in zip(inv_bs, a_stricts)]
        invs = [inv - _dot(left, inv_b) for inv, left, inv_b in zip(invs, lefts, inv_bs)]
        s *= 2
    return invs


def _gdn_prompt_kernel(qkv_ref, sm_ref, z_ref, cw_ref, gb_ref, galog_ref, mixw_ref,
                       y_ref, tail_ref, state_ref, ubuf_ref, conv_ref, s_ref):
    c = pl.program_id(1)
    t = qkv_ref.shape[0]
    subs = range(t // CHUNK)
    heads = range(GDN_HEADS)

    @pl.when(c == 0)
    def _():
        ubuf_ref[:, 0:SUBLANES, :] = jnp.zeros((GDN_CONV // LANES, SUBLANES, LANES), F32)
        s_ref[...] = jnp.zeros_like(s_ref)

    _causal_conv_tile(qkv_ref, ubuf_ref, cw_ref, None, conv_ref, tail_ref)

    sm = sm_ref[...]
    beta = _sigmoid(sm)
    g = -jnp.exp(galog_ref[...]) * _softplus(sm + gb_ref[...])
    rt = lax.broadcasted_iota(jnp.int32, (t, t), 0)
    ct = lax.broadcasted_iota(jnp.int32, (t, t), 1)
    chunk_causal = (rt >= ct) & (rt // CHUNK == ct // CHUNK)
    gc = _sel_left(chunk_causal.astype(BF16), g)
    gc_t = _transpose_sel(gc)
    eg = jnp.exp(gc)
    ri = lax.broadcasted_iota(jnp.int32, (CHUNK, CHUNK), 0)
    ci = lax.broadcasted_iota(jnp.int32, (CHUNK, CHUNK), 1)
    causal = ri >= ci
    strict = ri > ci

    rows = [slice(j * CHUNK, (j + 1) * CHUNK) for j in subs]
    hs = [slice(h * GDN_D, (h + 1) * GDN_D) for h in heads]
    la = [SM_A + h for h in heads]
    pairs = [(j, h) for j in subs for h in heads]
    q, k, kb, vb, kbg, qg, decay = {}, {}, {}, {}, {}, {}, {}
    for h in heads:
        qf, kf, vf = conv_ref[h], conv_ref[GDN_HEADS + h], conv_ref[2 * GDN_HEADS + h]
        qf = qf * lax.rsqrt(jnp.sum(qf * qf, axis=-1, keepdims=True) + EPS) * (GDN_D ** -0.5)
        kf = kf * lax.rsqrt(jnp.sum(kf * kf, axis=-1, keepdims=True) + EPS)
        b_col = beta[:, SM_B + h:SM_B + h + 1]
        eg_col = eg[:, la[h]:la[h] + 1]
        kbf = kf * b_col
        vbf, kbgf, qgf = (vf * b_col).astype(BF16), (kbf * eg_col).astype(BF16), (qf * eg_col).astype(BF16)
        for j in subs:
            q[j, h], k[j, h], kb[j, h] = qf[rows[j]].astype(BF16), kf[rows[j]], kbf[rows[j]].astype(BF16)
            vb[j, h], kbg[j, h], qg[j, h] = vbf[rows[j]], kbgf[rows[j]], qgf[rows[j]]
            diff = gc[rows[j], la[h]:la[h] + 1] - gc_t[la[h]:la[h] + 1, rows[j]]
            decay[j, h] = jnp.where(causal, jnp.exp(jnp.minimum(diff, 0.0)), 0.0)
    k_b = {p: k[p].astype(BF16) for p in pairs}
    kk = {p: _dot_nt(kb[p], k_b[p]) for p in pairs}
    qk = {p: _dot_nt(q[p], k_b[p]) for p in pairs}
    a_strict = [jnp.where(strict, kk[p] * decay[p], 0.0) for p in pairs]
    attn = {p: (qk[p] * decay[p]).astype(BF16) for p in pairs}
    t_inv = dict(zip(pairs, [x.astype(BF16) for x in _unit_lower_inverses(a_strict, ri, ci)]))
    u = {p: _dot(t_inv[p], vb[p]) for p in pairs}
    wk = {p: _dot(t_inv[p], kbg[p]).astype(BF16) for p in pairs}

    state = [s_ref[h] for h in heads]
    for j in subs:
        g_last = gc[(j + 1) * CHUNK - 1:(j + 1) * CHUNK, :]
        eend = jnp.exp(g_last - gc[rows[j]])
        elast = jnp.exp(g_last)
        s_b = [x.astype(BF16) for x in state]
        v_new = [(u[j, h] - _dot(wk[j, h], s_b[h])).astype(BF16) for h in heads]
        k_end = [(k[j, h] * eend[:, la[h]:la[h] + 1]).astype(BF16) for h in heads]
        s_inc = [_dot_tn(k_end[h], v_new[h]) for h in heads]
        state = [state[h] * elast[:, la[h]:la[h] + 1] + s_inc[h] for h in heads]
        o = [_dot(qg[j, h], s_b[h]) + _dot(attn[j, h], v_new[h]) for h in heads]
        msq = [jnp.sum(x * x, axis=-1, keepdims=True) for x in o]
        for h in heads:
            y_ref[rows[j], hs[h]] = _head_norm_gate(o[h], msq[h], GDN_D, z_ref[rows[j], hs[h]], mixw_ref[:, hs[h]]).astype(BF16)
    for h in heads:
        s_ref[h] = state[h]

    @pl.when(c == pl.num_programs(1) - 1)
    def _():
        state_ref[0] = s_ref[...]


def _gdn_prompt(proj, small, batch, cw, gb, galog, mixw, tile):
    rows = proj.shape[0]
    nc = rows // batch // tile
    row = lambda b, c: (b * nc + c, 0)
    whole = lambda b, c: (0, 0)
    return pl.pallas_call(
        _gdn_prompt_kernel,
        grid=(batch, nc),
        in_specs=[
            pl.BlockSpec((tile, GDN_CONV), lambda b, c: (b * nc + c, COL_QKV // GDN_CONV)),
            pl.BlockSpec((tile, LANES), row),
            pl.BlockSpec((tile, GDN_W), lambda b, c: (b * nc + c, (COL_Z + SSD_W) // GDN_W)),
            pl.BlockSpec(cw.shape, whole), pl.BlockSpec(gb.shape, whole), pl.BlockSpec(galog.shape, whole),
            pl.BlockSpec(mixw.shape, whole),
        ],
        out_specs=[
            pl.BlockSpec((tile, GDN_W), row),
            pl.BlockSpec((1, SUBLANES, GDN_CONV), lambda b, c: (b, 0, 0)),
            pl.BlockSpec((1, GDN_HEADS, GDN_D, GDN_D), lambda b, c: (b, 0, 0, 0)),
        ],
        out_shape=[
            jax.ShapeDtypeStruct((rows, GDN_W), BF16),
            jax.ShapeDtypeStruct((batch, SUBLANES, GDN_CONV), F32),
            jax.ShapeDtypeStruct((batch, GDN_HEADS, GDN_D, GDN_D), F32),
        ],
        scratch_shapes=[pltpu.VMEM((GDN_CONV // LANES, tile + SUBLANES, LANES), F32),
                        pltpu.VMEM((GDN_CONV // LANES, tile, LANES), F32), pltpu.VMEM((GDN_HEADS, GDN_D, GDN_D), F32)],
        compiler_params=_params("parallel", "arbitrary"),
        name="gdn_prompt",
    )(proj, small, proj, cw, gb, galog, mixw)


def _mem_attention_head(q, k, v):
    s = _dot_nt(q.astype(BF16), k.astype(BF16)) * (MEM_D ** -0.5)
    e = jnp.exp(s - jnp.max(s, axis=-1, keepdims=True))
    p = e / jnp.sum(e, axis=-1, keepdims=True)
    return _dot(p.astype(BF16), v.astype(BF16))


def _mem_prompt_kernel(q_ref, k_ref, v_ref, z_ref, mixw_ref, y_ref):
    for h in range(MEM_HEADS):
        hs = slice(h * MEM_D, (h + 1) * MEM_D)
        o = _mem_attention_head(q_ref[:, hs], k_ref[:, hs], v_ref[:, hs])
        msq = jnp.sum(o * o, axis=-1, keepdims=True)
        y_ref[:, hs] = _head_norm_gate(o, msq, MEM_D, z_ref[:, hs], mixw_ref[:, hs]).astype(BF16)


def _mem_prompt(proj, kv, batch, mixw, tq):
    rows = proj.shape[0]
    nq = rows // batch // tq
    return pl.pallas_call(
        _mem_prompt_kernel,
        grid=(batch, nq),
        in_specs=[
            pl.BlockSpec((tq, MEM_W), lambda b, i: (b * nq + i, COL_QMEM // MEM_W)),
            pl.BlockSpec((MEM_TOKENS, MEM_W), lambda b, i: (b, 0)),
            pl.BlockSpec((MEM_TOKENS, MEM_W), lambda b, i: (b, 1)),
            pl.BlockSpec((tq, MEM_W), lambda b, i: (b * nq + i, (COL_Z + SSD_W + GDN_W) // MEM_W)),
            pl.BlockSpec(mixw.shape, lambda b, i: (0, 0)),
        ],
        out_specs=pl.BlockSpec((tq, MEM_W), lambda b, i: (b * nq + i, 0)),
        out_shape=jax.ShapeDtypeStruct((rows, MEM_W), BF16),
        compiler_params=_params("parallel", "parallel"),
        name="mem_prompt",
    )(proj, kv, kv, proj, mixw)


DEC_ROWS = SUBLANES


def _conv_step(u, st, cw_ref, bias, width):
    acc = cw_ref[CONV_K - 1:CONV_K, :] * u
    if bias is not None:
        acc = acc + bias
    for j in range(CONV_K - 1):
        acc = acc + cw_ref[j:j + 1, :] * st[:, j * width:(j + 1) * width]
    return _silu(acc), jnp.concatenate([st[:, width:], u], axis=1)


def _rows_to_columns(x):
    pad = jnp.zeros((LANES - x.shape[0], x.shape[1]), F32)
    return jnp.concatenate([x, pad], axis=0).T


def _pick_rows(parts):
    rid = lax.broadcasted_iota(jnp.int32, parts[0].shape, 0)
    out = parts[0]
    for i in range(1, len(parts)):
        out = jnp.where(rid == i, parts[i], out)
    return out


def _ssd_decode_kernel(xbc_ref, sm_ref, z_ref, cst_ref, st_ref, cw_ref, cb_ref, dtb_ref, alog_ref, dexp_ref,
                       mixw_ref, e_ref, en_ref, y_ref, cst_out_ref, st_out_ref):
    xbc, cst_new = _conv_step(xbc_ref[...], cst_ref[...], cw_ref, cb_ref[...], SSD_CONV)
    cst_out_ref[...] = cst_new
    xs = xbc[:, :SSD_W]
    e = e_ref[...]
    dt = _softplus(sm_ref[...] + dtb_ref[...])
    dec = jnp.exp(dt * (-jnp.exp(alog_ref[...])))
    xd_t = _rows_to_columns(xs * _sel_right(dt, e))
    dec_n = _sel_right(dec, en_ref[...])

    y_groups = []
    for g in range(SSD_GROUPS):
        b_g = xbc[:, SSD_W + g * SSD_N:SSD_W + (g + 1) * SSD_N]
        c_g = xbc[:, SSD_W + (SSD_GROUPS + g) * SSD_N:SSD_W + (SSD_GROUPS + g + 1) * SSD_N].astype(BF16)
        per_row = []
        for i in range(DEC_ROWS):
            new = []
            for r in range(SSD_HEADS // SSD_GROUPS):
                h = g * (SSD_HEADS // SSD_GROUPS) + r
                col = xd_t[h * SSD_P:(h + 1) * SSD_P, i:i + 1]
                hn = st_ref[i, h] * dec_n[i:i + 1, h * SSD_N:(h + 1) * SSD_N] + col * b_g[i:i + 1, :]
                st_out_ref[i, h] = hn
                new.append(hn)
            hg = jnp.concatenate(new, axis=0).astype(BF16)
            per_row.append(_dot_nt(c_g, hg))
        y_groups.append(_pick_rows(per_row))
    y = jnp.concatenate(y_groups, axis=1) + dexp_ref[...] * xs
    msq = _sel_right(_sel_right_nt(y * y, e, 2), e, 2)
    y_ref[...] = _head_norm_gate(y, msq, SSD_P, z_ref[...], mixw_ref[...])


def _ssd_decode(proj, small, cst, st, cw, cb, dtb, alog, dexp, mixw, e, en):
    rows = proj.shape[0]
    row = lambda i: (i, 0)
    whole = lambda i: (0, 0)
    return pl.pallas_call(
        _ssd_decode_kernel,
        grid=(rows // DEC_ROWS,),
        in_specs=[
            pl.BlockSpec((DEC_ROWS, SSD_CONV), lambda i: (i, COL_XBC // SSD_CONV)),
            pl.BlockSpec((DEC_ROWS, LANES), row),
            pl.BlockSpec((DEC_ROWS, SSD_W), lambda i: (i, COL_Z // SSD_W)),
            pl.BlockSpec((DEC_ROWS, cst.shape[1]), row),
            pl.BlockSpec((DEC_ROWS, SSD_HEADS, SSD_P, SSD_N), lambda i: (i, 0, 0, 0)),
            pl.BlockSpec(cw.shape, whole), pl.BlockSpec(cb.shape, whole), pl.BlockSpec(dtb.shape, whole),
            pl.BlockSpec(alog.shape, whole), pl.BlockSpec(dexp.shape, whole), pl.BlockSpec(mixw.shape, whole),
            pl.BlockSpec(e.shape, whole), pl.BlockSpec(en.shape, whole),
        ],
        out_specs=[
            pl.BlockSpec((DEC_ROWS, SSD_W), row),
            pl.BlockSpec((DEC_ROWS, cst.shape[1]), row),
            pl.BlockSpec((DEC_ROWS, SSD_HEADS, SSD_P, SSD_N), lambda i: (i, 0, 0, 0)),
        ],
        out_shape=[
            jax.ShapeDtypeStruct((rows, SSD_W), F32),
            jax.ShapeDtypeStruct(cst.shape, F32),
            jax.ShapeDtypeStruct(st.shape, F32),
        ],
        compiler_params=_params("parallel"),
        name="ssd_decode",
    )(proj, small, proj, cst, st, cw, cb, dtb, alog, dexp, mixw, e, en)


def _gdn_decode_kernel(qkv_ref, sm_ref, z_ref, cst_ref, st_ref, cw_ref, gb_ref, galog_ref, mixw_ref, en_ref,
                       y_ref, cst_out_ref, st_out_ref):
    qkv, cst_new = _conv_step(qkv_ref[...], cst_ref[...], cw_ref, None, GDN_CONV)
    cst_out_ref[...] = cst_new
    sm = sm_ref[...]
    beta = _sigmoid(sm)
    eg = jnp.exp(-jnp.exp(galog_ref[...]) * _softplus(sm + gb_ref[...]))
    eg_n = _sel_right(eg, en_ref[...])

    qs, ks = [], []
    for h in range(GDN_HEADS):
        q = qkv[:, h * GDN_D:(h + 1) * GDN_D]
        k = qkv[:, GDN_W + h * GDN_D:GDN_W + (h + 1) * GDN_D]
        qs.append(q * lax.rsqrt(jnp.sum(q * q, axis=-1, keepdims=True) + EPS) * (GDN_D ** -0.5))
        ks.append(k * lax.rsqrt(jnp.sum(k * k, axis=-1, keepdims=True) + EPS))
    k_t = _rows_to_columns(jnp.concatenate(ks, axis=1))

    for h in range(GDN_HEADS):
        hs = slice(h * GDN_D, (h + 1) * GDN_D)
        q, k = qs[h], ks[h]
        v = qkv[:, 2 * GDN_W + h * GDN_D:2 * GDN_W + (h + 1) * GDN_D]
        q_b, k_b = q.astype(BF16), k.astype(BF16)
        ks_rows, qs_rows = [], []
        for i in range(DEC_ROWS):
            s_b = st_ref[i, h].astype(BF16)
            ks_rows.append(_dot(k_b, s_b))
            qs_rows.append(_dot(q_b, s_b))
        k_s, q_s = _pick_rows(ks_rows), _pick_rows(qs_rows)
        eg_h = eg_n[:, hs]
        v_new = beta[:, SM_B + h:SM_B + h + 1] * (v - eg_h * k_s)
        o = eg_h * q_s + jnp.sum(q * k, axis=-1, keepdims=True) * v_new
        for i in range(DEC_ROWS):
            col = k_t[hs, i:i + 1]
            st_out_ref[i, h] = st_ref[i, h] * eg_h[i:i + 1, :] + col * v_new[i:i + 1, :]
        msq = jnp.sum(o * o, axis=-1, keepdims=True)
        y_ref[:, hs] = _head_norm_gate(o, msq, GDN_D, z_ref[:, hs], mixw_ref[:, hs])


def _gdn_decode(proj, small, cst, st, cw, gb, galog, mixw, en):
    rows = proj.shape[0]
    row = lambda i: (i, 0)
    whole = lambda i: (0, 0)
    return pl.pallas_call(
        _gdn_decode_kernel,
        grid=(rows // DEC_ROWS,),
        in_specs=[
            pl.BlockSpec((DEC_ROWS, GDN_CONV), lambda i: (i, COL_QKV // GDN_CONV)),
            pl.BlockSpec((DEC_ROWS, LANES), row),
            pl.BlockSpec((DEC_ROWS, GDN_W), lambda i: (i, (COL_Z + SSD_W) // GDN_W)),
            pl.BlockSpec((DEC_ROWS, cst.shape[1]), row),
            pl.BlockSpec((DEC_ROWS, GDN_HEADS, GDN_D, GDN_D), lambda i: (i, 0, 0, 0)),
            pl.BlockSpec(cw.shape, whole), pl.BlockSpec(gb.shape, whole), pl.BlockSpec(galog.shape, whole),
            pl.BlockSpec(mixw.shape, whole), pl.BlockSpec(en.shape, whole),
        ],
        out_specs=[
            pl.BlockSpec((DEC_ROWS, GDN_W), row),
            pl.BlockSpec((DEC_ROWS, cst.shape[1]), row),
            pl.BlockSpec((DEC_ROWS, GDN_HEADS, GDN_D, GDN_D), lambda i: (i, 0, 0, 0)),
        ],
        out_shape=[
            jax.ShapeDtypeStruct((rows, GDN_W), F32),
            jax.ShapeDtypeStruct(cst.shape, F32),
            jax.ShapeDtypeStruct(st.shape, F32),
        ],
        compiler_params=_params("parallel"),
        name="gdn_decode",
    )(proj, small, proj, cst, st, cw, gb, galog, mixw, en)


def _mem_decode_kernel(q_ref, k_ref, v_ref, z_ref, mixw_ref, y_ref):
    heads, rows = range(MEM_HEADS), range(DEC_ROWS)
    hs = [slice(h * MEM_D, (h + 1) * MEM_D) for h in heads]
    win = [pl.ds(h, MEM_TOKENS, stride=MEM_HEADS) for h in heads]
    q = [q_ref[:, hs[h]].astype(BF16) for h in heads]
    s = [_pick_rows([_dot_nt(q[h], k_ref[i, win[h], :].astype(BF16)) for i in rows]) * (MEM_D ** -0.5) for h in heads]
    e = [jnp.exp(x - jnp.max(x, axis=-1, keepdims=True)) for x in s]
    p = [(x / jnp.sum(x, axis=-1, keepdims=True)).astype(BF16) for x in e]
    o = [_pick_rows([_dot(p[h], v_ref[i, win[h], :].astype(BF16)) for i in rows]) for h in heads]
    msq = [jnp.sum(x * x, axis=-1, keepdims=True) for x in o]
    for h in heads:
        y_ref[:, hs[h]] = _head_norm_gate(o[h], msq[h], MEM_D, z_ref[:, hs[h]], mixw_ref[:, hs[h]])


def _mem_decode(proj, mem_k, mem_v, mixw):
    rows = proj.shape[0]
    return pl.pallas_call(
        _mem_decode_kernel,
        grid=(rows // DEC_ROWS,),
        in_specs=[
            pl.BlockSpec((DEC_ROWS, MEM_W), lambda i: (i, COL_QMEM // MEM_W)),
            pl.BlockSpec((DEC_ROWS, MEM_TOKENS * MEM_HEADS, MEM_D), lambda i: (i, 0, 0)),
            pl.BlockSpec((DEC_ROWS, MEM_TOKENS * MEM_HEADS, MEM_D), lambda i: (i, 0, 0)),
            pl.BlockSpec((DEC_ROWS, MEM_W), lambda i: (i, (COL_Z + SSD_W + GDN_W) // MEM_W)),
            pl.BlockSpec(mixw.shape, lambda i: (0, 0)),
        ],
        out_specs=pl.BlockSpec((DEC_ROWS, MEM_W), lambda i: (i, 0)),
        out_shape=jax.ShapeDtypeStruct((rows, MEM_W), F32),
        compiler_params=_params("parallel"),
        name="mem_decode",
    )(proj, mem_k, mem_v, proj, mixw)


IN_DT = SSD_CONV
IN_QKV = IN_DT + SSD_HEADS
IN_B = IN_QKV + GDN_CONV
IN_QMEM = IN_B + 2 * GDN_HEADS
IN_COLS = IN_QMEM + MEM_W + MIX_W
PREP_ROWS = 256


def _prep_w_in_kernel(w_ref, main_ref, small_ref):
    main_ref[:, COL_QKV:COL_QKV + GDN_CONV] = w_ref[:, IN_QKV:IN_B].astype(BF16)
    main_ref[:, COL_XBC:COL_XBC + SSD_CONV] = w_ref[:, :SSD_CONV].astype(BF16)
    main_ref[:, COL_QMEM:N_MAIN] = w_ref[:, IN_QMEM:IN_COLS].astype(BF16)
    lane = lax.broadcasted_iota(jnp.int32, (w_ref.shape[0], LANES), 1)
    dt_blk = w_ref[:, IN_DT:IN_DT + LANES]
    ba_blk = w_ref[:, IN_B - SM_B:IN_B - SM_B + LANES]
    small = jnp.where(lane < SM_B, dt_blk, jnp.where(lane < SM_A + GDN_HEADS, ba_blk, 0.0))
    small_ref[...] = small.astype(BF16)


def _prep_w_in(w):
    k = w.shape[0]
    assert w.shape[1] == IN_COLS and IN_DT % LANES == 0 and (IN_B - SM_B) % LANES == 0
    return pl.pallas_call(
        _prep_w_in_kernel,
        grid=(k // PREP_ROWS,),
        in_specs=[pl.BlockSpec((PREP_ROWS, IN_COLS), lambda i: (i, 0))],
        out_specs=[pl.BlockSpec((PREP_ROWS, N_MAIN), lambda i: (i, 0)), pl.BlockSpec((PREP_ROWS, LANES), lambda i: (i, 0))],
        out_shape=[jax.ShapeDtypeStruct((k, N_MAIN), BF16), jax.ShapeDtypeStruct((k, LANES), BF16)],
        compiler_params=_params("parallel"),
        name="prep_w_in",
    )(w)


def _head_expander(heads, first_lane, width):
    m = np.zeros((LANES, heads * width), np.float32)
    for h in range(heads):
        m[first_lane + h, h * width:(h + 1) * width] = 1.0
    return jnp.asarray(m, BF16)


def _lane_row(vec, first_lane):
    return jnp.zeros((1, LANES), F32).at[0, first_lane:first_lane + vec.shape[0]].set(vec.astype(F32))


def kernel(x_prompt, x_sample, mem_prompt, state_ssd_conv, state_ssd, state_gdn_conv, state_gdn, cache_mem_k, cache_mem_v, norm_w, w_in, ssd_conv_w, ssd_conv_b, ssd_dt_bias, ssd_A_log, ssd_D, gdn_conv_w, gdn_dt_bias, gdn_A_log, mem_norm_w, w_mem_kv, mix_norm_w, w_out, final_norm_w):
    bp, seq, d = x_prompt.shape
    bs = x_sample.shape[0]
    assert (d, seq % CHUNK, bs % DEC_ROWS, norm_w.shape[0]) == (D_MODEL, 0, 0, 1)

    w_main, w_small = _prep_w_in(w_in[0])
    wo = w_out[0].astype(BF16)
    wo1, wo2, wo3 = wo[:SSD_W], wo[SSD_W:SSD_W + GDN_W], wo[SSD_W + GDN_W:]
    nw = norm_w[0][None, :]
    mixw = mix_norm_w[0][None, :]
    mixw1, mixw2, mixw3 = mixw[:, :SSD_W], mixw[:, SSD_W:SSD_W + GDN_W], mixw[:, SSD_W + GDN_W:]
    fw = final_norm_w[None, :]
    ssd_dtb = _lane_row(ssd_dt_bias[0], SM_DT)
    ssd_alog = _lane_row(ssd_A_log[0], SM_DT)
    ssd_dexp = jnp.repeat(ssd_D[0].astype(F32), SSD_P)[None, :]
    gdn_b = _lane_row(gdn_dt_bias[0], SM_A)
    gdn_alog = _lane_row(gdn_A_log[0], SM_A)
    e_ssd = _head_expander(SSD_HEADS, SM_DT, SSD_P)
    e_ssd_n = _head_expander(SSD_HEADS, SM_DT, SSD_N)
    e_gdn_n = _head_expander(GDN_HEADS, SM_A, GDN_D)
    ssd_cw, ssd_cb, gdn_cw = ssd_conv_w[0], ssd_conv_b[0][None, :], gdn_conv_w[0]

    xp = x_prompt.reshape(bp * seq, d)
    proj_p, small_p = _norm_matmul(xp, nw, w_main, w_small, _row_tile(bp * seq, PROJ_ROWS), SSD_CONV)
    kv, _ = _norm_matmul(mem_prompt.reshape(bp * MEM_TOKENS, d), mem_norm_w[0][None, :], w_mem_kv[0].astype(BF16),
                         jnp.zeros((d, LANES), BF16), _row_tile(bp * MEM_TOKENS, PROJ_ROWS), 2 * MEM_W)
    scan_rows = SCAN_ROWS if seq % SCAN_ROWS == 0 else CHUNK
    y_ssd, tail_ssd, p_ssd = _ssd_prompt(proj_p, small_p, bp, ssd_cw, ssd_cb, ssd_dtb, ssd_alog, ssd_dexp, mixw1, e_ssd,
                                         scan_rows)
    y_gdn, tail_gdn, p_gdn = _gdn_prompt(proj_p, small_p, bp, gdn_cw, gdn_b, gdn_alog, mixw2, scan_rows)
    y_mem = _mem_prompt(proj_p, kv, bp, mixw3, _row_tile(seq, MEM_Q_ROWS))
    y_prompt = _out_proj(y_ssd, y_gdn, y_mem, wo1, wo2, wo3, xp, fw, _row_tile(bp * seq, OUT_ROWS)).reshape(bp, seq, d)

    xs = x_sample.reshape(bs, d)
    proj_s, small_s = _norm_matmul(xs, nw, w_main, w_small, bs, SSD_CONV)
    ys_ssd, s_ssd_conv, s_ssd = _ssd_decode(proj_s, small_s, state_ssd_conv[0].reshape(bs, -1), state_ssd[0],
                                            ssd_cw, ssd_cb, ssd_dtb, ssd_alog, ssd_dexp, mixw1, e_ssd, e_ssd_n)
    ys_gdn, s_gdn_conv, s_gdn = _gdn_decode(proj_s, small_s, state_gdn_conv[0].reshape(bs, -1), state_gdn[0],
                                            gdn_cw, gdn_b, gdn_alog, mixw2, e_gdn_n)
    ys_mem = _mem_decode(proj_s, cache_mem_k.reshape(bs, MEM_TOKENS * MEM_HEADS, MEM_D),
                         cache_mem_v.reshape(bs, MEM_TOKENS * MEM_HEADS, MEM_D), mixw3)
    y_sample = _out_proj(ys_ssd, ys_gdn, ys_mem, wo1, wo2, wo3, xs, fw, bs).reshape(bs, 1, d)

    keep = CONV_K - 1
    mem_shape = (1, bp, MEM_TOKENS, MEM_HEADS, MEM_D)
    return (
        y_prompt, y_sample,
        tail_ssd[None, :, SUBLANES - keep:, :], p_ssd[None],
        tail_gdn[None, :, SUBLANES - keep:, :], p_gdn[None],
        kv[:, :MEM_W].reshape(mem_shape), kv[:, MEM_W:].reshape(mem_shape),
        s_ssd_conv.reshape(1, bs, keep, SSD_CONV), s_ssd[None],
        s_gdn_conv.reshape(1, bs, keep, GDN_CONV), s_gdn[None],
    )
```

```python
import functools

import numpy as np
import jax
import jax.numpy as jnp
from jax import lax
from jax.experimental import pallas as pl
from jax.experimental.pallas import tpu as pltpu

F32, BF16 = jnp.float32, jnp.bfloat16

D_MODEL = 2048
SSD_HEADS, SSD_P, SSD_GROUPS, SSD_N = 16, 64, 2, 128
SSD_W = SSD_HEADS * SSD_P
SSD_GW = SSD_W // SSD_GROUPS
SSD_CONV = SSD_W + 2 * SSD_GROUPS * SSD_N
GDN_HEADS, GDN_D = 8, 128
GDN_W = GDN_HEADS * GDN_D
GDN_CONV = 3 * GDN_W
MEM_TOKENS, MEM_HEADS, MEM_D = 256, 4, 128
MEM_W = MEM_HEADS * MEM_D
MIX_W = SSD_W + GDN_W + MEM_W
CONV_K = 4
CHUNK = 64
EPS = 1e-6

LANES = 128
SUBLANES = 8
VMEM_LIMIT = 56 * 1024 * 1024
PROJ_ROWS = 1024
OUT_ROWS = 512
MEM_Q_ROWS = 256
SCAN_ROWS = 256
CONV_PHASES = 4

COL_QKV = 0
COL_XBC = COL_QKV + GDN_CONV
COL_QMEM = COL_XBC + SSD_CONV
COL_Z = COL_QMEM + MEM_W
N_MAIN = COL_Z + MIX_W
SM_DT, SM_B, SM_A = 0, SSD_HEADS, SSD_HEADS + GDN_HEADS


def _dot(a, b):
    return jnp.dot(a, b, preferred_element_type=F32)


def _dot_nt(a, b):
    return lax.dot_general(a, b, (((1,), (1,)), ((), ())), preferred_element_type=F32)


def _dot_tn(a, b):
    return lax.dot_general(a, b, (((0,), (0,)), ((), ())), preferred_element_type=F32)


def _split(x, n):
    parts, r = [], x
    for i in range(n):
        p = r.astype(BF16)
        parts.append(p)
        if i + 1 < n:
            r = r - p.astype(F32)
    return parts


def _sel_left(sel, x, n=3):
    return functools.reduce(lambda a, b: a + b, [_dot(sel, p) for p in _split(x, n)])


def _sel_right(x, sel, n=3):
    return functools.reduce(lambda a, b: a + b, [_dot(p, sel) for p in _split(x, n)])


def _sel_right_nt(x, sel, n=3):
    return functools.reduce(lambda a, b: a + b, [_dot_nt(p, sel) for p in _split(x, n)])


def _transpose_sel(x, n=3):
    eye = _eye(LANES).astype(BF16)
    return functools.reduce(lambda a, b: a + b, [_dot_nt(eye, p) for p in _split(x, n)])


def _eye(n):
    return (lax.broadcasted_iota(jnp.int32, (n, n), 0) == lax.broadcasted_iota(jnp.int32, (n, n), 1)).astype(F32)


def _sigmoid(x):
    return 1.0 / (1.0 + jnp.exp(-x))


def _silu(x):
    return x * _sigmoid(x)


def _softplus(x):
    return jnp.maximum(x, 0.0) + jnp.log1p(jnp.exp(-jnp.abs(x)))


def _params(*sem):
    return pltpu.CompilerParams(dimension_semantics=sem, vmem_limit_bytes=VMEM_LIMIT)


def _row_tile(rows, preferred):
    return preferred if rows % preferred == 0 else rows


def _norm_matmul_kernel(x_ref, nw_ref, w_ref, ws_ref, o_ref, os_ref, h_ref):
    @pl.when(pl.program_id(1) == 0)
    def _():
        x = x_ref[...]
        ms = jnp.mean(x * x, axis=-1, keepdims=True)
        h = (x * lax.rsqrt(ms + EPS) * nw_ref[...]).astype(BF16)
        h_ref[...] = h
        os_ref[...] = _dot_nt(h, ws_ref[...])

    o_ref[...] = _dot_nt(h_ref[...], w_ref[...])


def _norm_matmul(x, nw, w_t, ws_t, tm, tn):
    m, k = x.shape
    n = w_t.shape[0]
    ns = ws_t.shape[0]
    return pl.pallas_call(
        _norm_matmul_kernel,
        grid=(m // tm, n // tn),
        in_specs=[
            pl.BlockSpec((tm, k), lambda i, j: (i, 0)),
            pl.BlockSpec((1, k), lambda i, j: (0, 0)),
            pl.BlockSpec((tn, k), lambda i, j: (j, 0)),
            pl.BlockSpec((ns, k), lambda i, j: (0, 0)),
        ],
        out_specs=[
            pl.BlockSpec((tm, tn), lambda i, j: (i, j)),
            pl.BlockSpec((tm, ns), lambda i, j: (i, 0)),
        ],
        out_shape=[jax.ShapeDtypeStruct((m, n), F32), jax.ShapeDtypeStruct((m, ns), F32)],
        scratch_shapes=[pltpu.VMEM((tm, k), BF16)],
        compiler_params=_params("parallel", "arbitrary"),
        name="norm_matmul",
    )(x, nw, w_t, ws_t)


def _out_proj_kernel(y1_ref, y2_ref, y3_ref, w1_ref, w2_ref, w3_ref, x_ref, fw_ref, o_ref):
    acc = (_dot(y1_ref[...].astype(BF16), w1_ref[...]) + _dot(y2_ref[...].astype(BF16), w2_ref[...])
           + _dot(y3_ref[...].astype(BF16), w3_ref[...]))
    r = x_ref[...] + acc
    ms = jnp.mean(r * r, axis=-1, keepdims=True)
    o_ref[...] = r * lax.rsqrt(ms + EPS) * fw_ref[...]


def _out_proj(y1, y2, y3, w1, w2, w3, x, fw, tm):
    m, d = x.shape
    row = lambda i: (i, 0)
    whole = lambda i: (0, 0)
    return pl.pallas_call(
        _out_proj_kernel,
        grid=(m // tm,),
        in_specs=[
            pl.BlockSpec((tm, y1.shape[1]), row), pl.BlockSpec((tm, y2.shape[1]), row), pl.BlockSpec((tm, y3.shape[1]), row),
            pl.BlockSpec(w1.shape, whole), pl.BlockSpec(w2.shape, whole), pl.BlockSpec(w3.shape, whole),
            pl.BlockSpec((tm, d), row), pl.BlockSpec((1, d), whole),
        ],
        out_specs=pl.BlockSpec((tm, d), row),
        out_shape=jax.ShapeDtypeStruct((m, d), F32),
        compiler_params=_params("parallel"),
        name="out_proj",
    )(y1, y2, y3, w1, w2, w3, x, fw)


def _causal_conv_tile(u_ref, ubuf_ref, cw_ref, cb_ref, out_ref, tail_ref):
    t, width = u_ref.shape
    n = t // CONV_PHASES
    for s in range(width // LANES):
        cs = slice(s * LANES, (s + 1) * LANES)
        ubuf_ref[s, SUBLANES:SUBLANES + t, :] = u_ref[:, cs]
        taps = {d: ubuf_ref[s, pl.ds(SUBLANES + d, n, stride=CONV_PHASES), :] for d in range(1 - CONV_K, CONV_PHASES)}
        w = [cw_ref[j:j + 1, cs] for j in range(CONV_K)]
        for r in range(CONV_PHASES):
            acc = w[CONV_K - 1] * taps[r]
            if cb_ref is not None:
                acc = acc + cb_ref[:, cs]
            for j in range(CONV_K - 1):
                acc = acc + w[j] * taps[r - (CONV_K - 1) + j]
            out_ref[s, pl.ds(r, n, stride=CONV_PHASES), :] = _silu(acc)
        tail = ubuf_ref[s, t:t + SUBLANES, :]
        ubuf_ref[s, 0:SUBLANES, :] = tail
        tail_ref[0, :, cs] = tail


def _head_norm_gate(y, msq, width, z, mixw):
    return y * lax.rsqrt(msq * (1.0 / width) + EPS) * mixw * _silu(z)


def _ssd_prompt_kernel(xbc_ref, sm_ref, z_ref, cw_ref, cb_ref, dtb_ref, alog_ref, dexp_ref, mixw_ref, e_ref,
                       y_ref, tail_ref, state_ref, ubuf_ref, conv_ref, h_ref):
    c = pl.program_id(1)
    t = xbc_ref.shape[0]
    subs = range(t // CHUNK)
    groups = range(SSD_GROUPS)
    blocks = range(SSD_GW // LANES)

    @pl.when(c == 0)
    def _():
        ubuf_ref[:, 0:SUBLANES, :] = jnp.zeros((SSD_CONV // LANES, SUBLANES, LANES), F32)
        h_ref[...] = jnp.zeros_like(h_ref)

    _causal_conv_tile(xbc_ref, ubuf_ref, cw_ref, cb_ref, conv_ref, tail_ref)
    xs = jnp.concatenate([conv_ref[s] for s in range(SSD_W // LANES)], axis=1)
    e = e_ref[...]
    rows = [slice(j * CHUNK, (j + 1) * CHUNK) for j in subs]
    gs = [slice(g * SSD_GW, (g + 1) * SSD_GW) for g in groups]

    dt = _softplus(sm_ref[...] + dtb_ref[...])
    a = dt * (-jnp.exp(alog_ref[...]))
    rt = lax.broadcasted_iota(jnp.int32, (t, t), 0)
    ct = lax.broadcasted_iota(jnp.int32, (t, t), 1)
    chunk_causal = (rt >= ct) & (rt // CHUNK == ct // CHUNK)
    cum = _sel_left(chunk_causal.astype(BF16), a)
    cum_t = _transpose_sel(cum)
    ecum = jnp.exp(cum)
    wend = jnp.concatenate([jnp.exp(cum[(j + 1) * CHUNK - 1:(j + 1) * CHUNK, :] - cum[rows[j]]) for j in subs], axis=0)
    dt_x = _sel_right(dt, e, 2)
    ecum_x = _sel_right(ecum, e, 2)
    wend_x = _sel_right(wend, e, 2)

    xdt = xs * dt_x
    xdt_b = xdt.astype(BF16)
    xw_b = (xdt * wend_x).astype(BF16)
    causal = lax.broadcasted_iota(jnp.int32, (CHUNK, CHUNK), 0) >= lax.broadcasted_iota(jnp.int32, (CHUNK, CHUNK), 1)
    left = lax.broadcasted_iota(jnp.int32, (CHUNK, LANES), 1) < SSD_P

    jg = [(j, g) for j in subs for g in groups]
    b_slab, c_slab = SSD_W // LANES, SSD_W // LANES + SSD_GROUPS
    bmat = {(j, g): conv_ref[b_slab + g, rows[j], :].astype(BF16) for j, g in jg}
    cmat = {(j, g): conv_ref[c_slab + g, rows[j], :].astype(BF16) for j, g in jg}
    cb = {p: _dot_nt(cmat[p], bmat[p]) for p in jg}
    inc = {(j, g): _dot_tn(bmat[j, g], xw_b[rows[j], gs[g]]) for j, g in jg}
    scores = {}
    for j, g in jg:
        for blk in blocks:
            for half in range(2):
                h = (g * len(blocks) + blk) * 2 + half
                diff = cum[rows[j], h:h + 1] - cum_t[h:h + 1, rows[j]]
                lmat = jnp.where(causal, jnp.exp(jnp.minimum(diff, 0.0)), 0.0)
                scores[j, g, blk, half] = (cb[j, g] * lmat).astype(BF16)
    intra = {}
    for j, g in jg:
        for blk in blocks:
            lanes = slice(g * SSD_GW + blk * LANES, g * SSD_GW + (blk + 1) * LANES)
            xb = xdt_b[rows[j], lanes]
            zero = jnp.zeros_like(xb)
            intra[j, g, blk] = (_dot(scores[j, g, blk, 0], jnp.where(left, xb, zero))
                                + _dot(scores[j, g, blk, 1], jnp.where(left, zero, xb)))

    state = {(0, g): h_ref[:, gs[g]] for g in groups}
    for j in subs:
        last = (j + 1) * CHUNK - 1
        for g in groups:
            state[j + 1, g] = state[j, g] * ecum_x[last:last + 1, gs[g]] + inc[j, g]
    for g in groups:
        h_ref[:, gs[g]] = state[len(subs), g]
    inter = {p: _dot(cmat[p], state[p].astype(BF16)) for p in jg}
    inter_x = jnp.concatenate([jnp.concatenate([inter[j, g] for g in groups], axis=1) for j in subs], axis=0) * ecum_x
    intra_x = jnp.concatenate([jnp.concatenate([intra[j, g, blk] for g in groups for blk in blocks], axis=1)
                               for j in subs], axis=0)
    y = intra_x + inter_x + dexp_ref[...] * xs
    msq = _sel_right(_sel_right_nt(y * y, e, 2), e, 2)
    y_ref[...] = _head_norm_gate(y, msq, SSD_P, z_ref[...], mixw_ref[...]).astype(BF16)

    @pl.when(c == pl.num_programs(1) - 1)
    def _():
        state_ref[0] = h_ref[...].T.reshape(SSD_HEADS, SSD_P, SSD_N)


def _ssd_prompt(proj, small, batch, cw, cb, dtb, alog, dexp, mixw, e, tile):
    rows = proj.shape[0]
    nc = rows // batch // tile
    row = lambda b, c: (b * nc + c, 0)
    whole = lambda b, c: (0, 0)
    return pl.pallas_call(
        _ssd_prompt_kernel,
        grid=(batch, nc),
        in_specs=[
            pl.BlockSpec((tile, SSD_CONV), lambda b, c: (b * nc + c, COL_XBC // SSD_CONV)),
            pl.BlockSpec((tile, LANES), row),
            pl.BlockSpec((tile, SSD_W), lambda b, c: (b * nc + c, COL_Z // SSD_W)),
            pl.BlockSpec(cw.shape, whole), pl.BlockSpec(cb.shape, whole), pl.BlockSpec(dtb.shape, whole),
            pl.BlockSpec(alog.shape, whole), pl.BlockSpec(dexp.shape, whole), pl.BlockSpec(mixw.shape, whole),
            pl.BlockSpec(e.shape, whole),
        ],
        out_specs=[
            pl.BlockSpec((tile, SSD_W), row),
            pl.BlockSpec((1, SUBLANES, SSD_CONV), lambda b, c: (b, 0, 0)),
            pl.BlockSpec((1, SSD_HEADS, SSD_P, SSD_N), lambda b, c: (b, 0, 0, 0)),
        ],
        out_shape=[
            jax.ShapeDtypeStruct((rows, SSD_W), BF16),
            jax.ShapeDtypeStruct((batch, SUBLANES, SSD_CONV), F32),
            jax.ShapeDtypeStruct((batch, SSD_HEADS, SSD_P, SSD_N), F32),
        ],
        scratch_shapes=[pltpu.VMEM((SSD_CONV // LANES, tile + SUBLANES, LANES), F32),
                        pltpu.VMEM((SSD_CONV // LANES, tile, LANES), F32), pltpu.VMEM((SSD_N, SSD_W), F32)],
        compiler_params=_params("parallel", "arbitrary"),
        name="ssd_prompt",
    )(proj, small, proj, cw, cb, dtb, alog, dexp, mixw, e)


def _unit_lower_inverses(a_stricts, ri, ci):
    t = a_stricts[0].shape[0]
    eye = _eye(t)
    first = (ri == ci + 1) & (ci % 2 == 0)
    invs = [eye - jnp.where(first, a, 0.0) for a in a_stricts]
    s = 2
    while s < t:
        sel = (ri // (2 * s) == ci // (2 * s)) & ((ri // s) % 2 == 1) & ((ci // s) % 2 == 0)
        inv_bs = [inv.astype(BF16) for inv in invs]
        lefts = [_dot(inv_b, jnp.where(sel, a, 0.0).astype(BF16)).astype(BF16) for inv_b, a in zip(inv_bs, a_stricts)]
        invs = [inv - _dot(left, inv_b) for inv, left, inv_b in zip(invs, lefts, inv_bs)]
        s *= 2
    return invs


def _gdn_prompt_kernel(qkv_ref, sm_ref, z_ref, cw_ref, gb_ref, galog_ref, mixw_ref,
                       y_ref, tail_ref, state_ref, ubuf_ref, conv_ref, s_ref):
    c = pl.program_id(1)
    t = qkv_ref.shape[0]
    subs = range(t // CHUNK)
    heads = range(GDN_HEADS)

    @pl.when(c == 0)
    def _():
        ubuf_ref[:, 0:SUBLANES, :] = jnp.zeros((GDN_CONV // LANES, SUBLANES, LANES), F32)
        s_ref[...] = jnp.zeros_like(s_ref)

    _causal_conv_tile(qkv_ref, ubuf_ref, cw_ref, None, conv_ref, tail_ref)

    sm = sm_ref[...]
    beta = _sigmoid(sm)
    g = -jnp.exp(galog_ref[...]) * _softplus(sm + gb_ref[...])
    rt = lax.broadcasted_iota(jnp.int32, (t, t), 0)
    ct = lax.broadcasted_iota(jnp.int32, (t, t), 1)
    chunk_causal = (rt >= ct) & (rt // CHUNK == ct // CHUNK)
    gc = _sel_left(chunk_causal.astype(BF16), g)
    gc_t = _transpose_sel(gc)
    eg = jnp.exp(gc)
    ri = lax.broadcasted_iota(jnp.int32, (CHUNK, CHUNK), 0)
    ci = lax.broadcasted_iota(jnp.int32, (CHUNK, CHUNK), 1)
    causal = ri >= ci
    strict = ri > ci

    rows = [slice(j * CHUNK, (j + 1) * CHUNK) for j in subs]
    hs = [slice(h * GDN_D, (h + 1) * GDN_D) for h in heads]
    la = [SM_A + h for h in heads]
    pairs = [(j, h) for j in subs for h in heads]
    q, k, kb, vb, kbg, qg, decay = {}, {}, {}, {}, {}, {}, {}
    for h in heads:
        qf, kf, vf = conv_ref[h], conv_ref[GDN_HEADS + h], conv_ref[2 * GDN_HEADS + h]
        qf = qf * lax.rsqrt(jnp.sum(qf * qf, axis=-1, keepdims=True) + EPS) * (GDN_D ** -0.5)
        kf = kf * lax.rsqrt(jnp.sum(kf * kf, axis=-1, keepdims=True) + EPS)
        b_col = beta[:, SM_B + h:SM_B + h + 1]
        eg_col = eg[:, la[h]:la[h] + 1]
        kbf = kf * b_col
        vbf, kbgf, qgf = (vf * b_col).astype(BF16), (kbf * eg_col).astype(BF16), (qf * eg_col).astype(BF16)
        for j in subs:
            q[j, h], k[j, h], kb[j, h] = qf[rows[j]].astype(BF16), kf[rows[j]], kbf[rows[j]].astype(BF16)
            vb[j, h], kbg[j, h], qg[j, h] = vbf[rows[j]], kbgf[rows[j]], qgf[rows[j]]
            diff = gc[rows[j], la[h]:la[h] + 1] - gc_t[la[h]:la[h] + 1, rows[j]]
            decay[j, h] = jnp.where(causal, jnp.exp(jnp.minimum(diff, 0.0)), 0.0)
    k_b = {p: k[p].astype(BF16) for p in pairs}
    kk = {p: _dot_nt(kb[p], k_b[p]) for p in pairs}
    qk = {p: _dot_nt(q[p], k_b[p]) for p in pairs}
    a_strict = [jnp.where(strict, kk[p] * decay[p], 0.0) for p in pairs]
    attn = {p: (qk[p] * decay[p]).astype(BF16) for p in pairs}
    t_inv = dict(zip(pairs, [x.astype(BF16) for x in _unit_lower_inverses(a_strict, ri, ci)]))
    u = {p: _dot(t_inv[p], vb[p]) for p in pairs}
    wk = {p: _dot(t_inv[p], kbg[p]).astype(BF16) for p in pairs}

    state = [s_ref[h] for h in heads]
    for j in subs:
        g_last = gc[(j + 1) * CHUNK - 1:(j + 1) * CHUNK, :]
        eend = jnp.exp(g_last - gc[rows[j]])
        elast = jnp.exp(g_last)
        s_b = [x.astype(BF16) for x in state]
        v_new = [(u[j, h] - _dot(wk[j, h], s_b[h])).astype(BF16) for h in heads]
        k_end = [(k[j, h] * eend[:, la[h]:la[h] + 1]).astype(BF16) for h in heads]
        s_inc = [_dot_tn(k_end[h], v_new[h]) for h in heads]
        state = [state[h] * elast[:, la[h]:la[h] + 1] + s_inc[h] for h in heads]
        o = [_dot(qg[j, h], s_b[h]) + _dot(attn[j, h], v_new[h]) for h in heads]
        msq = [jnp.sum(x * x, axis=-1, keepdims=True) for x in o]
        for h in heads:
            y_ref[rows[j], hs[h]] = _head_norm_gate(o[h], msq[h], GDN_D, z_ref[rows[j], hs[h]], mixw_ref[:, hs[h]]).astype(BF16)
    for h in heads:
        s_ref[h] = state[h]

    @pl.when(c == pl.num_programs(1) - 1)
    def _():
        state_ref[0] = s_ref[...]


def _gdn_prompt(proj, small, batch, cw, gb, galog, mixw, tile):
    rows = proj.shape[0]
    nc = rows // batch // tile
    row = lambda b, c: (b * nc + c, 0)
    whole = lambda b, c: (0, 0)
    return pl.pallas_call(
        _gdn_prompt_kernel,
        grid=(batch, nc),
        in_specs=[
            pl.BlockSpec((tile, GDN_CONV), lambda b, c: (b * nc + c, COL_QKV // GDN_CONV)),
            pl.BlockSpec((tile, LANES), row),
            pl.BlockSpec((tile, GDN_W), lambda b, c: (b * nc + c, (COL_Z + SSD_W) // GDN_W)),
            pl.BlockSpec(cw.shape, whole), pl.BlockSpec(gb.shape, whole), pl.BlockSpec(galog.shape, whole),
            pl.BlockSpec(mixw.shape, whole),
        ],
        out_specs=[
            pl.BlockSpec((tile, GDN_W), row),
            pl.BlockSpec((1, SUBLANES, GDN_CONV), lambda b, c: (b, 0, 0)),
            pl.BlockSpec((1, GDN_HEADS, GDN_D, GDN_D), lambda b, c: (b, 0, 0, 0)),
        ],
        out_shape=[
            jax.ShapeDtypeStruct((rows, GDN_W), BF16),
            jax.ShapeDtypeStruct((batch, SUBLANES, GDN_CONV), F32),
            jax.ShapeDtypeStruct((batch, GDN_HEADS, GDN_D, GDN_D), F32),
        ],
        scratch_shapes=[pltpu.VMEM((GDN_CONV // LANES, tile + SUBLANES, LANES), F32),
                        pltpu.VMEM((GDN_CONV // LANES, tile, LANES), F32), pltpu.VMEM((GDN_HEADS, GDN_D, GDN_D), F32)],
        compiler_params=_params("parallel", "arbitrary"),
        name="gdn_prompt",
    )(proj, small, proj, cw, gb, galog, mixw)


def _mem_attention_head(q, k, v):
    s = _dot_nt(q.astype(BF16), k.astype(BF16)) * (MEM_D ** -0.5)
    e = jnp.exp(s - jnp.max(s, axis=-1, keepdims=True))
    p = e / jnp.sum(e, axis=-1, keepdims=True)
    return _dot(p.astype(BF16), v.astype(BF16))


def _mem_prompt_kernel(q_ref, k_ref, v_ref, z_ref, mixw_ref, y_ref):
    for h in range(MEM_HEADS):
        hs = slice(h * MEM_D, (h + 1) * MEM_D)
        o = _mem_attention_head(q_ref[:, hs], k_ref[:, hs], v_ref[:, hs])
        msq = jnp.sum(o * o, axis=-1, keepdims=True)
        y_ref[:, hs] = _head_norm_gate(o, msq, MEM_D, z_ref[:, hs], mixw_ref[:, hs]).astype(BF16)


def _mem_prompt(proj, kv, batch, mixw, tq):
    rows = proj.shape[0]
    nq = rows // batch // tq
    return pl.pallas_call(
        _mem_prompt_kernel,
        grid=(batch, nq),
        in_specs=[
            pl.BlockSpec((tq, MEM_W), lambda b, i: (b * nq + i, COL_QMEM // MEM_W)),
            pl.BlockSpec((MEM_TOKENS, MEM_W), lambda b, i: (b, 0)),
            pl.BlockSpec((MEM_TOKENS, MEM_W), lambda b, i: (b, 1)),
            pl.BlockSpec((tq, MEM_W), lambda b, i: (b * nq + i, (COL_Z + SSD_W + GDN_W) // MEM_W)),
            pl.BlockSpec(mixw.shape, lambda b, i: (0, 0)),
        ],
        out_specs=pl.BlockSpec((tq, MEM_W), lambda b, i: (b * nq + i, 0)),
        out_shape=jax.ShapeDtypeStruct((rows, MEM_W), BF16),
        compiler_params=_params("parallel", "parallel"),
        name="mem_prompt",
    )(proj, kv, kv, proj, mixw)


DEC_ROWS = SUBLANES


def _conv_step(u, st, cw_ref, bias, width):
    acc = cw_ref[CONV_K - 1:CONV_K, :] * u
    if bias is not None:
        acc = acc + bias
    for j in range(CONV_K - 1):
        acc = acc + cw_ref[j:j + 1, :] * st[:, j * width:(j + 1) * width]
    return _silu(acc), jnp.concatenate([st[:, width:], u], axis=1)


def _rows_to_columns(x):
    pad = jnp.zeros((LANES - x.shape[0], x.shape[1]), F32)
    return jnp.concatenate([x, pad], axis=0).T


def _pick_rows(parts):
    rid = lax.broadcasted_iota(jnp.int32, parts[0].shape, 0)
    out = parts[0]
    for i in range(1, len(parts)):
        out = jnp.where(rid == i, parts[i], out)
    return out


def _ssd_decode_kernel(xbc_ref, sm_ref, z_ref, cst_ref, st_ref, cw_ref, cb_ref, dtb_ref, alog_ref, dexp_ref,
                       mixw_ref, e_ref, en_ref, y_ref, cst_out_ref, st_out_ref):
    xbc, cst_new = _conv_step(xbc_ref[...], cst_ref[...], cw_ref, cb_ref[...], SSD_CONV)
    cst_out_ref[...] = cst_new
    xs = xbc[:, :SSD_W]
    e = e_ref[...]
    dt = _softplus(sm_ref[...] + dtb_ref[...])
    dec = jnp.exp(dt * (-jnp.exp(alog_ref[...])))
    xd_t = _rows_to_columns(xs * _sel_right(dt, e))
    dec_n = _sel_right(dec, en_ref[...])

    y_groups = []
    for g in range(SSD_GROUPS):
        b_g = xbc[:, SSD_W + g * SSD_N:SSD_W + (g + 1) * SSD_N]
        c_g = xbc[:, SSD_W + (SSD_GROUPS + g) * SSD_N:SSD_W + (SSD_GROUPS + g + 1) * SSD_N].astype(BF16)
        per_row = []
        for i in range(DEC_ROWS):
            new = []
            for r in range(SSD_HEADS // SSD_GROUPS):
                h = g * (SSD_HEADS // SSD_GROUPS) + r
                col = xd_t[h * SSD_P:(h + 1) * SSD_P, i:i + 1]
                hn = st_ref[i, h] * dec_n[i:i + 1, h * SSD_N:(h + 1) * SSD_N] + col * b_g[i:i + 1, :]
                st_out_ref[i, h] = hn
                new.append(hn)
            hg = jnp.concatenate(new, axis=0).astype(BF16)
            per_row.append(_dot_nt(c_g, hg))
        y_groups.append(_pick_rows(per_row))
    y = jnp.concatenate(y_groups, axis=1) + dexp_ref[...] * xs
    msq = _sel_right(_sel_right_nt(y * y, e, 2), e, 2)
    y_ref[...] = _head_norm_gate(y, msq, SSD_P, z_ref[...], mixw_ref[...])


def _ssd_decode(proj, small, cst, st, cw, cb, dtb, alog, dexp, mixw, e, en):
    rows = proj.shape[0]
    row = lambda i: (i, 0)
    whole = lambda i: (0, 0)
    return pl.pallas_call(
        _ssd_decode_kernel,
        grid=(rows // DEC_ROWS,),
        in_specs=[
            pl.BlockSpec((DEC_ROWS, SSD_CONV), lambda i: (i, COL_XBC // SSD_CONV)),
            pl.BlockSpec((DEC_ROWS, LANES), row),
            pl.BlockSpec((DEC_ROWS, SSD_W), lambda i: (i, COL_Z // SSD_W)),
            pl.BlockSpec((DEC_ROWS, cst.shape[1]), row),
            pl.BlockSpec((DEC_ROWS, SSD_HEADS, SSD_P, SSD_N), lambda i: (i, 0, 0, 0)),
            pl.BlockSpec(cw.shape, whole), pl.BlockSpec(cb.shape, whole), pl.BlockSpec(dtb.shape, whole),
            pl.BlockSpec(alog.shape, whole), pl.BlockSpec(dexp.shape, whole), pl.BlockSpec(mixw.shape, whole),
            pl.BlockSpec(e.shape, whole), pl.BlockSpec(en.shape, whole),
        ],
        out_specs=[
            pl.BlockSpec((DEC_ROWS, SSD_W), row),
            pl.BlockSpec((DEC_ROWS, cst.shape[1]), row),
            pl.BlockSpec((DEC_ROWS, SSD_HEADS, SSD_P, SSD_N), lambda i: (i, 0, 0, 0)),
        ],
        out_shape=[
            jax.ShapeDtypeStruct((rows, SSD_W), F32),
            jax.ShapeDtypeStruct(cst.shape, F32),
            jax.ShapeDtypeStruct(st.shape, F32),
        ],
        compiler_params=_params("parallel"),
        name="ssd_decode",
    )(proj, small, proj, cst, st, cw, cb, dtb, alog, dexp, mixw, e, en)


def _gdn_decode_kernel(qkv_ref, sm_ref, z_ref, cst_ref, st_ref, cw_ref, gb_ref, galog_ref, mixw_ref, en_ref,
                       y_ref, cst_out_ref, st_out_ref):
    qkv, cst_new = _conv_step(qkv_ref[...], cst_ref[...], cw_ref, None, GDN_CONV)
    cst_out_ref[...] = cst_new
    sm = sm_ref[...]
    beta = _sigmoid(sm)
    eg = jnp.exp(-jnp.exp(galog_ref[...]) * _softplus(sm + gb_ref[...]))
    eg_n = _sel_right(eg, en_ref[...])

    qs, ks = [], []
    for h in range(GDN_HEADS):
        q = qkv[:, h * GDN_D:(h + 1) * GDN_D]
        k = qkv[:, GDN_W + h * GDN_D:GDN_W + (h + 1) * GDN_D]
        qs.append(q * lax.rsqrt(jnp.sum(q * q, axis=-1, keepdims=True) + EPS) * (GDN_D ** -0.5))
        ks.append(k * lax.rsqrt(jnp.sum(k * k, axis=-1, keepdims=True) + EPS))
    k_t = _rows_to_columns(jnp.concatenate(ks, axis=1))

    for h in range(GDN_HEADS):
        hs = slice(h * GDN_D, (h + 1) * GDN_D)
        q, k = qs[h], ks[h]
        v = qkv[:, 2 * GDN_W + h * GDN_D:2 * GDN_W + (h + 1) * GDN_D]
        q_b, k_b = q.astype(BF16), k.astype(BF16)
        ks_rows, qs_rows = [], []
        for i in range(DEC_ROWS):
            s_b = st_ref[i, h].astype(BF16)
            ks_rows.append(_dot(k_b, s_b))
            qs_rows.append(_dot(q_b, s_b))
        k_s, q_s = _pick_rows(ks_rows), _pick_rows(qs_rows)
        eg_h = eg_n[:, hs]
        v_new = beta[:, SM_B + h:SM_B + h + 1] * (v - eg_h * k_s)
        o = eg_h * q_s + jnp.sum(q * k, axis=-1, keepdims=True) * v_new
        for i in range(DEC_ROWS):
            col = k_t[hs, i:i + 1]
            st_out_ref[i, h] = st_ref[i, h] * eg_h[i:i + 1, :] + col * v_new[i:i + 1, :]
        msq = jnp.sum(o * o, axis=-1, keepdims=True)
        y_ref[:, hs] = _head_norm_gate(o, msq, GDN_D, z_ref[:, hs], mixw_ref[:, hs])


def _gdn_decode(proj, small, cst, st, cw, gb, galog, mixw, en):
    rows = proj.shape[0]
    row = lambda i: (i, 0)
    whole = lambda i: (0, 0)
    return pl.pallas_call(
        _gdn_decode_kernel,
        grid=(rows // DEC_ROWS,),
        in_specs=[
            pl.BlockSpec((DEC_ROWS, GDN_CONV), lambda i: (i, COL_QKV // GDN_CONV)),
            pl.BlockSpec((DEC_ROWS, LANES), row),
            pl.BlockSpec((DEC_ROWS, GDN_W), lambda i: (i, (COL_Z + SSD_W) // GDN_W)),
            pl.BlockSpec((DEC_ROWS, cst.shape[1]), row),
            pl.BlockSpec((DEC_ROWS, GDN_HEADS, GDN_D, GDN_D), lambda i: (i, 0, 0, 0)),
            pl.BlockSpec(cw.shape, whole), pl.BlockSpec(gb.shape, whole), pl.BlockSpec(galog.shape, whole),
            pl.BlockSpec(mixw.shape, whole), pl.BlockSpec(en.shape, whole),
        ],
        out_specs=[
            pl.BlockSpec((DEC_ROWS, GDN_W), row),
            pl.BlockSpec((DEC_ROWS, cst.shape[1]), row),
            pl.BlockSpec((DEC_ROWS, GDN_HEADS, GDN_D, GDN_D), lambda i: (i, 0, 0, 0)),
        ],
        out_shape=[
            jax.ShapeDtypeStruct((rows, GDN_W), F32),
            jax.ShapeDtypeStruct(cst.shape, F32),
            jax.ShapeDtypeStruct(st.shape, F32),
        ],
        compiler_params=_params("parallel"),
        name="gdn_decode",
    )(proj, small, proj, cst, st, cw, gb, galog, mixw, en)


def _mem_decode_kernel(q_ref, k_ref, v_ref, z_ref, mixw_ref, y_ref):
    heads, rows = range(MEM_HEADS), range(DEC_ROWS)
    hs = [slice(h * MEM_D, (h + 1) * MEM_D) for h in heads]
    win = [pl.ds(h, MEM_TOKENS, stride=MEM_HEADS) for h in heads]
    q = [q_ref[:, hs[h]].astype(BF16) for h in heads]
    s = [_pick_rows([_dot_nt(q[h], k_ref[i, win[h], :].astype(BF16)) for i in rows]) * (MEM_D ** -0.5) for h in heads]
    e = [jnp.exp(x - jnp.max(x, axis=-1, keepdims=True)) for x in s]
    p = [(x / jnp.sum(x, axis=-1, keepdims=True)).astype(BF16) for x in e]
    o = [_pick_rows([_dot(p[h], v_ref[i, win[h], :].astype(BF16)) for i in rows]) for h in heads]
    msq = [jnp.sum(x * x, axis=-1, keepdims=True) for x in o]
    for h in heads:
        y_ref[:, hs[h]] = _head_norm_gate(o[h], msq[h], MEM_D, z_ref[:, hs[h]], mixw_ref[:, hs[h]])


def _mem_decode(proj, mem_k, mem_v, mixw):
    rows = proj.shape[0]
    return pl.pallas_call(
        _mem_decode_kernel,
        grid=(rows // DEC_ROWS,),
        in_specs=[
            pl.BlockSpec((DEC_ROWS, MEM_W), lambda i: (i, COL_QMEM // MEM_W)),
            pl.BlockSpec((DEC_ROWS, MEM_TOKENS * MEM_HEADS, MEM_D), lambda i: (i, 0, 0)),
            pl.BlockSpec((DEC_ROWS, MEM_TOKENS * MEM_HEADS, MEM_D), lambda i: (i, 0, 0)),
            pl.BlockSpec((DEC_ROWS, MEM_W), lambda i: (i, (COL_Z + SSD_W + GDN_W) // MEM_W)),
            pl.BlockSpec(mixw.shape, lambda i: (0, 0)),
        ],
        out_specs=pl.BlockSpec((DEC_ROWS, MEM_W), lambda i: (i, 0)),
        out_shape=jax.ShapeDtypeStruct((rows, MEM_W), F32),
        compiler_params=_params("parallel"),
        name="mem_decode",
    )(proj, mem_k, mem_v, proj, mixw)


IN_DT = SSD_CONV
IN_QKV = IN_DT + SSD_HEADS
IN_B = IN_QKV + GDN_CONV
IN_QMEM = IN_B + 2 * GDN_HEADS
IN_COLS = IN_QMEM + MEM_W + MIX_W
PREP_COLS = 256


def _prep_w_in_kernel(w_ref, main_ref, small_ref):
    main_ref[COL_QKV:COL_QKV + GDN_CONV, :] = w_ref[IN_QKV:IN_B, :].astype(BF16)
    main_ref[COL_XBC:COL_XBC + SSD_CONV, :] = w_ref[:SSD_CONV, :].astype(BF16)
    main_ref[COL_QMEM:N_MAIN, :] = w_ref[IN_QMEM:IN_COLS, :].astype(BF16)
    small_ref[SM_DT:SM_B, :] = w_ref[IN_DT:IN_QKV, :].astype(BF16)
    small_ref[SM_B:SM_A + GDN_HEADS, :] = w_ref[IN_B:IN_QMEM, :].astype(BF16)
    small_ref[SM_A + GDN_HEADS:, :] = jnp.zeros((LANES - SM_A - GDN_HEADS, w_ref.shape[1]), BF16)


def _prep_w_in(w_t):
    n, k = w_t.shape
    assert n == IN_COLS
    return pl.pallas_call(
        _prep_w_in_kernel,
        grid=(k // PREP_COLS,),
        in_specs=[pl.BlockSpec((IN_COLS, PREP_COLS), lambda i: (0, i))],
        out_specs=[pl.BlockSpec((N_MAIN, PREP_COLS), lambda i: (0, i)), pl.BlockSpec((LANES, PREP_COLS), lambda i: (0, i))],
        out_shape=[jax.ShapeDtypeStruct((N_MAIN, k), BF16), jax.ShapeDtypeStruct((LANES, k), BF16)],
        compiler_params=_params("parallel"),
        name="prep_w_in",
    )(w_t)


def _head_expander(heads, first_lane, width):
    m = np.zeros((LANES, heads * width), np.float32)
    for h in range(heads):
        m[first_lane + h, h * width:(h + 1) * width] = 1.0
    return jnp.asarray(m, BF16)


def _lane_row(vec, first_lane):
    return jnp.zeros((1, LANES), F32).at[0, first_lane:first_lane + vec.shape[0]].set(vec.astype(F32))


def kernel(x_prompt, x_sample, mem_prompt, state_ssd_conv, state_ssd, state_gdn_conv, state_gdn, cache_mem_k, cache_mem_v, norm_w, w_in, ssd_conv_w, ssd_conv_b, ssd_dt_bias, ssd_A_log, ssd_D, gdn_conv_w, gdn_dt_bias, gdn_A_log, mem_norm_w, w_mem_kv, mix_norm_w, w_out, final_norm_w):
    bp, seq, d = x_prompt.shape
    bs = x_sample.shape[0]
    assert (d, seq % CHUNK, bs % DEC_ROWS, norm_w.shape[0]) == (D_MODEL, 0, 0, 1)

    w_main, w_small = _prep_w_in(w_in[0].T)
    wo = w_out[0].astype(BF16)
    wo1, wo2, wo3 = wo[:SSD_W], wo[SSD_W:SSD_W + GDN_W], wo[SSD_W + GDN_W:]
    nw = norm_w[0][None, :]
    mixw = mix_norm_w[0][None, :]
    mixw1, mixw2, mixw3 = mixw[:, :SSD_W], mixw[:, SSD_W:SSD_W + GDN_W], mixw[:, SSD_W + GDN_W:]
    fw = final_norm_w[None, :]
    ssd_dtb = _lane_row(ssd_dt_bias[0], SM_DT)
    ssd_alog = _lane_row(ssd_A_log[0], SM_DT)
    ssd_dexp = jnp.repeat(ssd_D[0].astype(F32), SSD_P)[None, :]
    gdn_b = _lane_row(gdn_dt_bias[0], SM_A)
    gdn_alog = _lane_row(gdn_A_log[0], SM_A)
    e_ssd = _head_expander(SSD_HEADS, SM_DT, SSD_P)
    e_ssd_n = _head_expander(SSD_HEADS, SM_DT, SSD_N)
    e_gdn_n = _head_expander(GDN_HEADS, SM_A, GDN_D)
    ssd_cw, ssd_cb, gdn_cw = ssd_conv_w[0], ssd_conv_b[0][None, :], gdn_conv_w[0]

    xp = x_prompt.reshape(bp * seq, d)
    proj_p, small_p = _norm_matmul(xp, nw, w_main, w_small, _row_tile(bp * seq, PROJ_ROWS), SSD_CONV)
    kv, _ = _norm_matmul(mem_prompt.reshape(bp * MEM_TOKENS, d), mem_norm_w[0][None, :], w_mem_kv[0].T.astype(BF16),
                         jnp.zeros((LANES, d), BF16), _row_tile(bp * MEM_TOKENS, PROJ_ROWS), 2 * MEM_W)
    scan_rows = SCAN_ROWS if seq % SCAN_ROWS == 0 else CHUNK
    y_ssd, tail_ssd, p_ssd = _ssd_prompt(proj_p, small_p, bp, ssd_cw, ssd_cb, ssd_dtb, ssd_alog, ssd_dexp, mixw1, e_ssd,
                                         scan_rows)
    y_gdn, tail_gdn, p_gdn = _gdn_prompt(proj_p, small_p, bp, gdn_cw, gdn_b, gdn_alog, mixw2, scan_rows)
    y_mem = _mem_prompt(proj_p, kv, bp, mixw3, _row_tile(seq, MEM_Q_ROWS))
    y_prompt = _out_proj(y_ssd, y_gdn, y_mem, wo1, wo2, wo3, xp, fw, _row_tile(bp * seq, OUT_ROWS)).reshape(bp, seq, d)

    xs = x_sample.reshape(bs, d)
    proj_s, small_s = _norm_matmul(xs, nw, w_main, w_small, bs, SSD_CONV)
    ys_ssd, s_ssd_conv, s_ssd = _ssd_decode(proj_s, small_s, state_ssd_conv[0].reshape(bs, -1), state_ssd[0],
                                            ssd_cw, ssd_cb, ssd_dtb, ssd_alog, ssd_dexp, mixw1, e_ssd, e_ssd_n)
    ys_gdn, s_gdn_conv, s_gdn = _gdn_decode(proj_s, small_s, state_gdn_conv[0].reshape(bs, -1), state_gdn[0],
                                            gdn_cw, gdn_b, gdn_alog, mixw2, e_gdn_n)
    ys_mem = _mem_decode(proj_s, cache_mem_k.reshape(bs, MEM_TOKENS * MEM_HEADS, MEM_D),
                         cache_mem_v.reshape(bs, MEM_TOKENS * MEM_HEADS, MEM_D), mixw3)
    y_sample = _out_proj(ys_ssd, ys_gdn, ys_mem, wo1, wo2, wo3, xs, fw, bs).reshape(bs, 1, d)

    keep = CONV_K - 1
    mem_shape = (1, bp, MEM_TOKENS, MEM_HEADS, MEM_D)
    return (
        y_prompt, y_sample,
        tail_ssd[None, :, SUBLANES - keep:, :], p_ssd[None],
        tail_gdn[None, :, SUBLANES - keep:, :], p_gdn[None],
        kv[:, :MEM_W].reshape(mem_shape), kv[:, MEM_W:].reshape(mem_shape),
        s_ssd_conv.reshape(1, bs, keep, SSD_CONV), s_ssd[None],
        s_gdn_conv.reshape(1, bs, keep, GDN_CONV), s_gdn[None],
    )
```

```python
import functools

import numpy as np
import jax
import jax.numpy as jnp
from jax import lax
from jax.experimental import pallas as pl
from jax.experimental.pallas import tpu as pltpu

F32, BF16 = jnp.float32, jnp.bfloat16

D_MODEL = 2048
SSD_HEADS, SSD_P, SSD_GROUPS, SSD_N = 16, 64, 2, 128
SSD_W = SSD_HEADS * SSD_P
SSD_GW = SSD_W // SSD_GROUPS
SSD_CONV = SSD_W + 2 * SSD_GROUPS * SSD_N
GDN_HEADS, GDN_D = 8, 128
GDN_W = GDN_HEADS * GDN_D
GDN_CONV = 3 * GDN_W
MEM_TOKENS, MEM_HEADS, MEM_D = 256, 4, 128
MEM_W = MEM_HEADS * MEM_D
MIX_W = SSD_W + GDN_W + MEM_W
CONV_K = 4
CHUNK = 64
EPS = 1e-6

LANES = 128
SUBLANES = 8
VMEM_LIMIT = 56 * 1024 * 1024
PROJ_ROWS = 1024
OUT_ROWS = 512
MEM_Q_ROWS = 256
SCAN_ROWS = 256
CONV_PHASES = 4

COL_QKV = 0
COL_XBC = COL_QKV + GDN_CONV
COL_QMEM = COL_XBC + SSD_CONV
COL_Z = COL_QMEM + MEM_W
N_MAIN = COL_Z + MIX_W
SM_DT, SM_B, SM_A = 0, SSD_HEADS, SSD_HEADS + GDN_HEADS


def _dot(a, b):
    return jnp.dot(a, b, preferred_element_type=F32)


def _dot_nt(a, b):
    return lax.dot_general(a, b, (((1,), (1,)), ((), ())), preferred_element_type=F32)


def _dot_tn(a, b):
    return lax.dot_general(a, b, (((0,), (0,)), ((), ())), preferred_element_type=F32)


def _split(x, n):
    parts, r = [], x
    for i in range(n):
        p = r.astype(BF16)
        parts.append(p)
        if i + 1 < n:
            r = r - p.astype(F32)
    return parts


def _sel_left(sel, x, n=3):
    return functools.reduce(lambda a, b: a + b, [_dot(sel, p) for p in _split(x, n)])


def _sel_right(x, sel, n=3):
    return functools.reduce(lambda a, b: a + b, [_dot(p, sel) for p in _split(x, n)])


def _sel_right_nt(x, sel, n=3):
    return functools.reduce(lambda a, b: a + b, [_dot_nt(p, sel) for p in _split(x, n)])


def _transpose_sel(x, n=3):
    eye = _eye(LANES).astype(BF16)
    return functools.reduce(lambda a, b: a + b, [_dot_nt(eye, p) for p in _split(x, n)])


def _eye(n):
    return (lax.broadcasted_iota(jnp.int32, (n, n), 0) == lax.broadcasted_iota(jnp.int32, (n, n), 1)).astype(F32)


def _sigmoid(x):
    return 1.0 / (1.0 + jnp.exp(-x))


def _silu(x):
    return x * _sigmoid(x)


def _softplus(x):
    return jnp.maximum(x, 0.0) + jnp.log1p(jnp.exp(-jnp.abs(x)))


def _params(*sem):
    return pltpu.CompilerParams(dimension_semantics=sem, vmem_limit_bytes=VMEM_LIMIT)


def _row_tile(rows, preferred):
    return preferred if rows % preferred == 0 else rows


def _norm_matmul_kernel(x_ref, nw_ref, w_ref, ws_ref, o_ref, os_ref, h_ref):
    @pl.when(pl.program_id(1) == 0)
    def _():
        x = x_ref[...]
        ms = jnp.mean(x * x, axis=-1, keepdims=True)
        h = (x * lax.rsqrt(ms + EPS) * nw_ref[...]).astype(BF16)
        h_ref[...] = h
        os_ref[...] = _dot_nt(h, ws_ref[...])

    o_ref[...] = _dot_nt(h_ref[...], w_ref[...])


def _norm_matmul(x, nw, w_t, ws_t, tm, tn):
    m, k = x.shape
    n = w_t.shape[0]
    ns = ws_t.shape[0]
    return pl.pallas_call(
        _norm_matmul_kernel,
        grid=(m // tm, n // tn),
        in_specs=[
            pl.BlockSpec((tm, k), lambda i, j: (i, 0)),
            pl.BlockSpec((1, k), lambda i, j: (0, 0)),
            pl.BlockSpec((tn, k), lambda i, j: (j, 0)),
            pl.BlockSpec((ns, k), lambda i, j: (0, 0)),
        ],
        out_specs=[
            pl.BlockSpec((tm, tn), lambda i, j: (i, j)),
            pl.BlockSpec((tm, ns), lambda i, j: (i, 0)),
        ],
        out_shape=[jax.ShapeDtypeStruct((m, n), F32), jax.ShapeDtypeStruct((m, ns), F32)],
        scratch_shapes=[pltpu.VMEM((tm, k), BF16)],
        compiler_params=_params("parallel", "arbitrary"),
        name="norm_matmul",
    )(x, nw, w_t, ws_t)


def _out_proj_kernel(y1_ref, y2_ref, y3_ref, w1_ref, w2_ref, w3_ref, x_ref, fw_ref, o_ref):
    acc = (_dot(y1_ref[...].astype(BF16), w1_ref[...]) + _dot(y2_ref[...].astype(BF16), w2_ref[...])
           + _dot(y3_ref[...].astype(BF16), w3_ref[...]))
    r = x_ref[...] + acc
    ms = jnp.mean(r * r, axis=-1, keepdims=True)
    o_ref[...] = r * lax.rsqrt(ms + EPS) * fw_ref[...]


def _out_proj(y1, y2, y3, w1, w2, w3, x, fw, tm):
    m, d = x.shape
    row = lambda i: (i, 0)
    whole = lambda i: (0, 0)
    return pl.pallas_call(
        _out_proj_kernel,
        grid=(m // tm,),
        in_specs=[
            pl.BlockSpec((tm, y1.shape[1]), row), pl.BlockSpec((tm, y2.shape[1]), row), pl.BlockSpec((tm, y3.shape[1]), row),
            pl.BlockSpec(w1.shape, whole), pl.BlockSpec(w2.shape, whole), pl.BlockSpec(w3.shape, whole),
            pl.BlockSpec((tm, d), row), pl.BlockSpec((1, d), whole),
        ],
        out_specs=pl.BlockSpec((tm, d), row),
        out_shape=jax.ShapeDtypeStruct((m, d), F32),
        compiler_params=_params("parallel"),
        name="out_proj",
    )(y1, y2, y3, w1, w2, w3, x, fw)


def _causal_conv_tile(u_ref, ubuf_ref, cw_ref, cb_ref, out_ref, tail_ref):
    t, width = u_ref.shape
    n = t // CONV_PHASES
    for s in range(width // LANES):
        cs = slice(s * LANES, (s + 1) * LANES)
        ubuf_ref[s, SUBLANES:SUBLANES + t, :] = u_ref[:, cs]
        taps = {d: ubuf_ref[s, pl.ds(SUBLANES + d, n, stride=CONV_PHASES), :] for d in range(1 - CONV_K, CONV_PHASES)}
        w = [cw_ref[j:j + 1, cs] for j in range(CONV_K)]
        for r in range(CONV_PHASES):
            acc = w[CONV_K - 1] * taps[r]
            if cb_ref is not None:
                acc = acc + cb_ref[:, cs]
            for j in range(CONV_K - 1):
                acc = acc + w[j] * taps[r - (CONV_K - 1) + j]
            out_ref[s, pl.ds(r, n, stride=CONV_PHASES), :] = _silu(acc)
        tail = ubuf_ref[s, t:t + SUBLANES, :]
        ubuf_ref[s, 0:SUBLANES, :] = tail
        tail_ref[0, :, cs] = tail


def _head_norm_gate(y, msq, width, z, mixw):
    return y * lax.rsqrt(msq * (1.0 / width) + EPS) * mixw * _silu(z)


def _ssd_prompt_kernel(xbc_ref, sm_ref, z_ref, cw_ref, cb_ref, dtb_ref, alog_ref, dexp_ref, mixw_ref, e_ref,
                       y_ref, tail_ref, state_ref, ubuf_ref, conv_ref, h_ref):
    c = pl.program_id(1)
    t = xbc_ref.shape[0]
    subs = range(t // CHUNK)
    groups = range(SSD_GROUPS)
    blocks = range(SSD_GW // LANES)

    @pl.when(c == 0)
    def _():
        ubuf_ref[:, 0:SUBLANES, :] = jnp.zeros((SSD_CONV // LANES, SUBLANES, LANES), F32)
        h_ref[...] = jnp.zeros_like(h_ref)

    _causal_conv_tile(xbc_ref, ubuf_ref, cw_ref, cb_ref, conv_ref, tail_ref)
    xs = jnp.concatenate([conv_ref[s] for s in range(SSD_W // LANES)], axis=1)
    e = e_ref[...]
    rows = [slice(j * CHUNK, (j + 1) * CHUNK) for j in subs]
    gs = [slice(g * SSD_GW, (g + 1) * SSD_GW) for g in groups]

    dt = _softplus(sm_ref[...] + dtb_ref[...])
    a = dt * (-jnp.exp(alog_ref[...]))
    rt = lax.broadcasted_iota(jnp.int32, (t, t), 0)
    ct = lax.broadcasted_iota(jnp.int32, (t, t), 1)
    chunk_causal = (rt >= ct) & (rt // CHUNK == ct // CHUNK)
    cum = _sel_left(chunk_causal.astype(BF16), a)
    cum_t = _transpose_sel(cum)
    ecum = jnp.exp(cum)
    wend = jnp.concatenate([jnp.exp(cum[(j + 1) * CHUNK - 1:(j + 1) * CHUNK, :] - cum[rows[j]]) for j in subs], axis=0)
    dt_x = _sel_right(dt, e, 2)
    ecum_x = _sel_right(ecum, e, 2)
    wend_x = _sel_right(wend, e, 2)

    xdt = xs * dt_x
    xdt_b = xdt.astype(BF16)
    xw_b = (xdt * wend_x).astype(BF16)
    causal = lax.broadcasted_iota(jnp.int32, (CHUNK, CHUNK), 0) >= lax.broadcasted_iota(jnp.int32, (CHUNK, CHUNK), 1)
    left = lax.broadcasted_iota(jnp.int32, (CHUNK, LANES), 1) < SSD_P

    jg = [(j, g) for j in subs for g in groups]
    b_slab, c_slab = SSD_W // LANES, SSD_W // LANES + SSD_GROUPS
    bmat = {(j, g): conv_ref[b_slab + g, rows[j], :].astype(BF16) for j, g in jg}
    cmat = {(j, g): conv_ref[c_slab + g, rows[j], :].astype(BF16) for j, g in jg}
    cb = {p: _dot_nt(cmat[p], bmat[p]) for p in jg}
    inc = {(j, g): _dot_tn(bmat[j, g], xw_b[rows[j], gs[g]]) for j, g in jg}
    scores = {}
    for j, g in jg:
        for blk in blocks:
            for half in range(2):
                h = (g * len(blocks) + blk) * 2 + half
                diff = cum[rows[j], h:h + 1] - cum_t[h:h + 1, rows[j]]
                lmat = jnp.where(causal, jnp.exp(jnp.minimum(diff, 0.0)), 0.0)
                scores[j, g, blk, half] = (cb[j, g] * lmat).astype(BF16)
    intra = {}
    for j, g in jg:
        for blk in blocks:
            lanes = slice(g * SSD_GW + blk * LANES, g * SSD_GW + (blk + 1) * LANES)
            xb = xdt_b[rows[j], lanes]
            zero = jnp.zeros_like(xb)
            intra[j, g, blk] = (_dot(scores[j, g, blk, 0], jnp.where(left, xb, zero))
                                + _dot(scores[j, g, blk, 1], jnp.where(left, zero, xb)))

    state = {(0, g): h_ref[:, gs[g]] for g in groups}
    for j in subs:
        last = (j + 1) * CHUNK - 1
        for g in groups:
            state[j + 1, g] = state[j, g] * ecum_x[last:last + 1, gs[g]] + inc[j, g]
    for g in groups:
        h_ref[:, gs[g]] = state[len(subs), g]
    inter = {p: _dot(cmat[p], state[p].astype(BF16)) for p in jg}
    inter_x = jnp.concatenate([jnp.concatenate([inter[j, g] for g in groups], axis=1) for j in subs], axis=0) * ecum_x
    intra_x = jnp.concatenate([jnp.concatenate([intra[j, g, blk] for g in groups for blk in blocks], axis=1)
                               for j in subs], axis=0)
    y = intra_x + inter_x + dexp_ref[...] * xs
    msq = _sel_right(_sel_right_nt(y * y, e, 2), e, 2)
    y_ref[...] = _head_norm_gate(y, msq, SSD_P, z_ref[...], mixw_ref[...]).astype(BF16)

    @pl.when(c == pl.num_programs(1) - 1)
    def _():
        state_ref[0] = h_ref[...].T.reshape(SSD_HEADS, SSD_P, SSD_N)


def _ssd_prompt(proj, small, batch, cw, cb, dtb, alog, dexp, mixw, e, tile):
    rows = proj.shape[0]
    nc = rows // batch // tile
    row = lambda b, c: (b * nc + c, 0)
    whole = lambda b, c: (0, 0)
    return pl.pallas_call(
        _ssd_prompt_kernel,
        grid=(batch, nc),
        in_specs=[
            pl.BlockSpec((tile, SSD_CONV), lambda b, c: (b * nc + c, COL_XBC // SSD_CONV)),
            pl.BlockSpec((tile, LANES), row),
            pl.BlockSpec((tile, SSD_W), lambda b, c: (b * nc + c, COL_Z // SSD_W)),
            pl.BlockSpec(cw.shape, whole), pl.BlockSpec(cb.shape, whole), pl.BlockSpec(dtb.shape, whole),
            pl.BlockSpec(alog.shape, whole), pl.BlockSpec(dexp.shape, whole), pl.BlockSpec(mixw.shape, whole),
            pl.BlockSpec(e.shape, whole),
        ],
        out_specs=[
            pl.BlockSpec((tile, SSD_W), row),
            pl.BlockSpec((1, SUBLANES, SSD_CONV), lambda b, c: (b, 0, 0)),
            pl.BlockSpec((1, SSD_HEADS, SSD_P, SSD_N), lambda b, c: (b, 0, 0, 0)),
        ],
        out_shape=[
            jax.ShapeDtypeStruct((rows, SSD_W), BF16),
            jax.ShapeDtypeStruct((batch, SUBLANES, SSD_CONV), F32),
            jax.ShapeDtypeStruct((batch, SSD_HEADS, SSD_P, SSD_N), F32),
        ],
        scratch_shapes=[pltpu.VMEM((SSD_CONV // LANES, tile + SUBLANES, LANES), F32),
                        pltpu.VMEM((SSD_CONV // LANES, tile, LANES), F32), pltpu.VMEM((SSD_N, SSD_W), F32)],
        compiler_params=_params("parallel", "arbitrary"),
        name="ssd_prompt",
    )(proj, small, proj, cw, cb, dtb, alog, dexp, mixw, e)


def _unit_lower_inverses(a_stricts, ri, ci):
    t = a_stricts[0].shape[0]
    eye = _eye(t)
    first = (ri == ci + 1) & (ci % 2 == 0)
    invs = [eye - jnp.where(first, a, 0.0) for a in a_stricts]
    s = 2
    while s < t:
        sel = (ri // (2 * s) == ci // (2 * s)) & ((ri // s) % 2 == 1) & ((ci // s) % 2 == 0)
        inv_bs = [inv.astype(BF16) for inv in invs]
        lefts = [_dot(inv_b, jnp.where(sel, a, 0.0).astype(BF16)).astype(BF16) for inv_b, a in zip(inv_bs, a_stricts)]
        invs = [inv - _dot(left, inv_b) for inv, left, inv_b in zip(invs, lefts, inv_bs)]
        s *= 2
    return invs


def _gdn_prompt_kernel(qkv_ref, sm_ref, z_ref, cw_ref, gb_ref, galog_ref, mixw_ref,
                       y_ref, tail_ref, state_ref, ubuf_ref, conv_ref, s_ref):
    c = pl.program_id(1)
    t = qkv_ref.shape[0]
    subs = range(t // CHUNK)
    heads = range(GDN_HEADS)

    @pl.when(c == 0)
    def _():
        ubuf_ref[:, 0:SUBLANES, :] = jnp.zeros((GDN_CONV // LANES, SUBLANES, LANES), F32)
        s_ref[...] = jnp.zeros_like(s_ref)

    _causal_conv_tile(qkv_ref, ubuf_ref, cw_ref, None, conv_ref, tail_ref)

    sm = sm_ref[...]
    beta = _sigmoid(sm)
    g = -jnp.exp(galog_ref[...]) * _softplus(sm + gb_ref[...])
    rt = lax.broadcasted_iota(jnp.int32, (t, t), 0)
    ct = lax.broadcasted_iota(jnp.int32, (t, t), 1)
    chunk_causal = (rt >= ct) & (rt // CHUNK == ct // CHUNK)
    gc = _sel_left(chunk_causal.astype(BF16), g)
    gc_t = _transpose_sel(gc)
    eg = jnp.exp(gc)
    ri = lax.broadcasted_iota(jnp.int32, (CHUNK, CHUNK), 0)
    ci = lax.broadcasted_iota(jnp.int32, (CHUNK, CHUNK), 1)
    causal = ri >= ci
    strict = ri > ci

    rows = [slice(j * CHUNK, (j + 1) * CHUNK) for j in subs]
    hs = [slice(h * GDN_D, (h + 1) * GDN_D) for h in heads]
    la = [SM_A + h for h in heads]
    pairs = [(j, h) for j in subs for h in heads]
    q, k, kb, vb, kbg, qg, decay = {}, {}, {}, {}, {}, {}, {}
    for h in heads:
        qf, kf, vf = conv_ref[h], conv_ref[GDN_HEADS + h], conv_ref[2 * GDN_HEADS + h]
        qf = qf * lax.rsqrt(jnp.sum(qf * qf, axis=-1, keepdims=True) + EPS) * (GDN_D ** -0.5)
        kf = kf * lax.rsqrt(jnp.sum(kf * kf, axis=-1, keepdims=True) + EPS)
        b_col = beta[:, SM_B + h:SM_B + h + 1]
        eg_col = eg[:, la[h]:la[h] + 1]
        kbf = kf * b_col
        vbf, kbgf, qgf = (vf * b_col).astype(BF16), (kbf * eg_col).astype(BF16), (qf * eg_col).astype(BF16)
        for j in subs:
            q[j, h], k[j, h], kb[j, h] = qf[rows[j]].astype(BF16), kf[rows[j]], kbf[rows[j]].astype(BF16)
            vb[j, h], kbg[j, h], qg[j, h] = vbf[rows[j]], kbgf[rows[j]], qgf[rows[j]]
            diff = gc[rows[j], la[h]:la[h] + 1] - gc_t[la[h]:la[h] + 1, rows[j]]
            decay[j, h] = jnp.where(causal, jnp.exp(jnp.minimum(diff, 0.0)), 0.0)
    k_b = {p: k[p].astype(BF16) for p in pairs}
    kk = {p: _dot_nt(kb[p], k_b[p]) for p in pairs}
    qk = {p: _dot_nt(q[p], k_b[p]) for p in pairs}
    a_strict = [jnp.where(strict, kk[p] * decay[p], 0.0) for p in pairs]
    attn = {p: (qk[p] * decay[p]).astype(BF16) for p in pairs}
    t_inv = dict(zip(pairs, [x.astype(BF16) for x in _unit_lower_inverses(a_strict, ri, ci)]))
    u = {p: _dot(t_inv[p], vb[p]) for p in pairs}
    wk = {p: _dot(t_inv[p], kbg[p]).astype(BF16) for p in pairs}

    state = [s_ref[h] for h in heads]
    for j in subs:
        g_last = gc[(j + 1) * CHUNK - 1:(j + 1) * CHUNK, :]
        eend = jnp.exp(g_last - gc[rows[j]])
        elast = jnp.exp(g_last)
        s_b = [x.astype(BF16) for x in state]
        v_new = [(u[j, h] - _dot(wk[j, h], s_b[h])).astype(BF16) for h in heads]
        k_end = [(k[j, h] * eend[:, la[h]:la[h] + 1]).astype(BF16) for h in heads]
        s_inc = [_dot_tn(k_end[h], v_new[h]) for h in heads]
        state = [state[h] * elast[:, la[h]:la[h] + 1] + s_inc[h] for h in heads]
        o = [_dot(qg[j, h], s_b[h]) + _dot(attn[j, h], v_new[h]) for h in heads]
        msq = [jnp.sum(x * x, axis=-1, keepdims=True) for x in o]
        for h in heads:
            y_ref[rows[j], hs[h]] = _head_norm_gate(o[h], msq[h], GDN_D, z_ref[rows[j], hs[h]], mixw_ref[:, hs[h]]).astype(BF16)
    for h in heads:
        s_ref[h] = state[h]

    @pl.when(c == pl.num_programs(1) - 1)
    def _():
        state_ref[0] = s_ref[...]


def _gdn_prompt(proj, small, batch, cw, gb, galog, mixw, tile):
    rows = proj.shape[0]
    nc = rows // batch // tile
    row = lambda b, c: (b * nc + c, 0)
    whole = lambda b, c: (0, 0)
    return pl.pallas_call(
        _gdn_prompt_kernel,
        grid=(batch, nc),
        in_specs=[
            pl.BlockSpec((tile, GDN_CONV), lambda b, c: (b * nc + c, COL_QKV // GDN_CONV)),
            pl.BlockSpec((tile, LANES), row),
            pl.BlockSpec((tile, GDN_W), lambda b, c: (b * nc + c, (COL_Z + SSD_W) // GDN_W)),
            pl.BlockSpec(cw.shape, whole), pl.BlockSpec(gb.shape, whole), pl.BlockSpec(galog.shape, whole),
            pl.BlockSpec(mixw.shape, whole),
        ],
        out_specs=[
            pl.BlockSpec((tile, GDN_W), row),
            pl.BlockSpec((1, SUBLANES, GDN_CONV), lambda b, c: (b, 0, 0)),
            pl.BlockSpec((1, GDN_HEADS, GDN_D, GDN_D), lambda b, c: (b, 0, 0, 0)),
        ],
        out_shape=[
            jax.ShapeDtypeStruct((rows, GDN_W), BF16),
            jax.ShapeDtypeStruct((batch, SUBLANES, GDN_CONV), F32),
            jax.ShapeDtypeStruct((batch, GDN_HEADS, GDN_D, GDN_D), F32),
        ],
        scratch_shapes=[pltpu.VMEM((GDN_CONV // LANES, tile + SUBLANES, LANES), F32),
                        pltpu.VMEM((GDN_CONV // LANES, tile, LANES), F32), pltpu.VMEM((GDN_HEADS, GDN_D, GDN_D), F32)],
        compiler_params=_params("parallel", "arbitrary"),
        name="gdn_prompt",
    )(proj, small, proj, cw, gb, galog, mixw)


def _mem_prompt_kernel(q_ref, k_ref, v_ref, z_ref, mixw_ref, y_ref):
    heads = range(MEM_HEADS)
    hs = [slice(h * MEM_D, (h + 1) * MEM_D) for h in heads]
    s = [_dot_nt(q_ref[:, hs[h]].astype(BF16), k_ref[:, hs[h]].astype(BF16)) * (MEM_D ** -0.5) for h in heads]
    e = [jnp.exp(x - jnp.max(x, axis=-1, keepdims=True)) for x in s]
    p = [(x / jnp.sum(x, axis=-1, keepdims=True)).astype(BF16) for x in e]
    o = [_dot(p[h], v_ref[:, hs[h]].astype(BF16)) for h in heads]
    msq = [jnp.sum(x * x, axis=-1, keepdims=True) for x in o]
    for h in heads:
        y_ref[:, hs[h]] = _head_norm_gate(o[h], msq[h], MEM_D, z_ref[:, hs[h]], mixw_ref[:, hs[h]]).astype(BF16)


def _mem_prompt(proj, kv, batch, mixw, tq):
    rows = proj.shape[0]
    nq = rows // batch // tq
    return pl.pallas_call(
        _mem_prompt_kernel,
        grid=(batch, nq),
        in_specs=[
            pl.BlockSpec((tq, MEM_W), lambda b, i: (b * nq + i, COL_QMEM // MEM_W)),
            pl.BlockSpec((MEM_TOKENS, MEM_W), lambda b, i: (b, 0)),
            pl.BlockSpec((MEM_TOKENS, MEM_W), lambda b, i: (b, 1)),
            pl.BlockSpec((tq, MEM_W), lambda b, i: (b * nq + i, (COL_Z + SSD_W + GDN_W) // MEM_W)),
            pl.BlockSpec(mixw.shape, lambda b, i: (0, 0)),
        ],
        out_specs=pl.BlockSpec((tq, MEM_W), lambda b, i: (b * nq + i, 0)),
        out_shape=jax.ShapeDtypeStruct((rows, MEM_W), BF16),
        compiler_params=_params("parallel", "parallel"),
        name="mem_prompt",
    )(proj, kv, kv, proj, mixw)


DEC_ROWS = SUBLANES


def _conv_step(u, st, cw_ref, bias, width):
    acc = cw_ref[CONV_K - 1:CONV_K, :] * u
    if bias is not None:
        acc = acc + bias
    for j in range(CONV_K - 1):
        acc = acc + cw_ref[j:j + 1, :] * st[:, j * width:(j + 1) * width]
    return _silu(acc), jnp.concatenate([st[:, width:], u], axis=1)


def _rows_to_columns(x):
    pad = jnp.zeros((LANES - x.shape[0], x.shape[1]), F32)
    return jnp.concatenate([x, pad], axis=0).T


def _pick_rows(parts):
    rid = lax.broadcasted_iota(jnp.int32, parts[0].shape, 0)
    out = parts[0]
    for i in range(1, len(parts)):
        out = jnp.where(rid == i, parts[i], out)
    return out


def _ssd_decode_step(xbc_ref, sm_ref, z_ref, cst_ref, st_ref, cw_ref, cb_ref, dtb_ref, alog_ref, dexp_ref,
                     mixw_ref, e_ref, en_ref, y_ref, cst_out_ref, st_out_ref):
    xbc, cst_new = _conv_step(xbc_ref[...], cst_ref[...], cw_ref, cb_ref[...], SSD_CONV)
    cst_out_ref[...] = cst_new
    xs = xbc[:, :SSD_W]
    e = e_ref[...]
    dt = _softplus(sm_ref[...] + dtb_ref[...])
    dec = jnp.exp(dt * (-jnp.exp(alog_ref[...])))
    xd_t = _rows_to_columns(xs * _sel_right(dt, e))
    dec_n = _sel_right(dec, en_ref[...])

    y_groups = []
    for g in range(SSD_GROUPS):
        b_g = xbc[:, SSD_W + g * SSD_N:SSD_W + (g + 1) * SSD_N]
        c_g = xbc[:, SSD_W + (SSD_GROUPS + g) * SSD_N:SSD_W + (SSD_GROUPS + g + 1) * SSD_N].astype(BF16)
        per_row = []
        for i in range(DEC_ROWS):
            new = []
            for r in range(SSD_HEADS // SSD_GROUPS):
                h = g * (SSD_HEADS // SSD_GROUPS) + r
                col = xd_t[h * SSD_P:(h + 1) * SSD_P, i:i + 1]
                hn = st_ref[i, h] * dec_n[i:i + 1, h * SSD_N:(h + 1) * SSD_N] + col * b_g[i:i + 1, :]
                st_out_ref[i, h] = hn
                new.append(hn)
            hg = jnp.concatenate(new, axis=0).astype(BF16)
            per_row.append(_dot_nt(c_g, hg))
        y_groups.append(_pick_rows(per_row))
    y = jnp.concatenate(y_groups, axis=1) + dexp_ref[...] * xs
    msq = _sel_right(_sel_right_nt(y * y, e, 2), e, 2)
    y_ref[...] = _head_norm_gate(y, msq, SSD_P, z_ref[...], mixw_ref[...])


def _gdn_decode_step(qkv_ref, sm_ref, z_ref, cst_ref, st_ref, cw_ref, gb_ref, galog_ref, mixw_ref, en_ref,
                     y_ref, cst_out_ref, st_out_ref):
    qkv, cst_new = _conv_step(qkv_ref[...], cst_ref[...], cw_ref, None, GDN_CONV)
    cst_out_ref[...] = cst_new
    sm = sm_ref[...]
    beta = _sigmoid(sm)
    eg = jnp.exp(-jnp.exp(galog_ref[...]) * _softplus(sm + gb_ref[...]))
    eg_n = _sel_right(eg, en_ref[...])

    qs, ks = [], []
    for h in range(GDN_HEADS):
        q = qkv[:, h * GDN_D:(h + 1) * GDN_D]
        k = qkv[:, GDN_W + h * GDN_D:GDN_W + (h + 1) * GDN_D]
        qs.append(q * lax.rsqrt(jnp.sum(q * q, axis=-1, keepdims=True) + EPS) * (GDN_D ** -0.5))
        ks.append(k * lax.rsqrt(jnp.sum(k * k, axis=-1, keepdims=True) + EPS))
    k_t = _rows_to_columns(jnp.concatenate(ks, axis=1))

    for h in range(GDN_HEADS):
        hs = slice(h * GDN_D, (h + 1) * GDN_D)
        q, k = qs[h], ks[h]
        v = qkv[:, 2 * GDN_W + h * GDN_D:2 * GDN_W + (h + 1) * GDN_D]
        q_b, k_b = q.astype(BF16), k.astype(BF16)
        ks_rows, qs_rows = [], []
        for i in range(DEC_ROWS):
            s_b = st_ref[i, h].astype(BF16)
            ks_rows.append(_dot(k_b, s_b))
            qs_rows.append(_dot(q_b, s_b))
        k_s, q_s = _pick_rows(ks_rows), _pick_rows(qs_rows)
        eg_h = eg_n[:, hs]
        v_new = beta[:, SM_B + h:SM_B + h + 1] * (v - eg_h * k_s)
        o = eg_h * q_s + jnp.sum(q * k, axis=-1, keepdims=True) * v_new
        for i in range(DEC_ROWS):
            col = k_t[hs, i:i + 1]
            st_out_ref[i, h] = st_ref[i, h] * eg_h[i:i + 1, :] + col * v_new[i:i + 1, :]
        msq = jnp.sum(o * o, axis=-1, keepdims=True)
        y_ref[:, hs] = _head_norm_gate(o, msq, GDN_D, z_ref[:, hs], mixw_ref[:, hs])


def _mem_decode_step(q_ref, k_ref, v_ref, z_ref, mixw_ref, y_ref):
    heads, rows = range(MEM_HEADS), range(DEC_ROWS)
    hs = [slice(h * MEM_D, (h + 1) * MEM_D) for h in heads]
    win = [pl.ds(h, MEM_TOKENS, stride=MEM_HEADS) for h in heads]
    q = [q_ref[:, hs[h]].astype(BF16) for h in heads]
    s = [_pick_rows([_dot_nt(q[h], k_ref[i, win[h], :].astype(BF16)) for i in rows]) * (MEM_D ** -0.5) for h in heads]
    e = [jnp.exp(x - jnp.max(x, axis=-1, keepdims=True)) for x in s]
    p = [(x / jnp.sum(x, axis=-1, keepdims=True)).astype(BF16) for x in e]
    o = [_pick_rows([_dot(p[h], v_ref[i, win[h], :].astype(BF16)) for i in rows]) for h in heads]
    msq = [jnp.sum(x * x, axis=-1, keepdims=True) for x in o]
    for h in heads:
        y_ref[:, hs[h]] = _head_norm_gate(o[h], msq[h], MEM_D, z_ref[:, hs[h]], mixw_ref[:, hs[h]])


def _decode_kernel(proj_ref, sm_ref, ssd_cst_ref, ssd_st_ref, gdn_cst_ref, gdn_st_ref, k_ref, v_ref,
                   ssd_cw_ref, ssd_cb_ref, dtb_ref, alog_ref, dexp_ref, gdn_cw_ref, gb_ref, galog_ref, mixw_ref,
                   e_ref, en_ssd_ref, en_gdn_ref,
                   y_ssd_ref, y_gdn_ref, y_mem_ref, ssd_cst_out_ref, ssd_st_out_ref, gdn_cst_out_ref, gdn_st_out_ref):
    cols = lambda start, width: proj_ref.at[:, start:start + width]
    mixw = lambda start, width: mixw_ref.at[:, start:start + width]
    _ssd_decode_step(cols(COL_XBC, SSD_CONV), sm_ref, cols(COL_Z, SSD_W), ssd_cst_ref, ssd_st_ref, ssd_cw_ref, ssd_cb_ref,
                     dtb_ref, alog_ref, dexp_ref, mixw(0, SSD_W), e_ref, en_ssd_ref, y_ssd_ref, ssd_cst_out_ref, ssd_st_out_ref)
    _gdn_decode_step(cols(COL_QKV, GDN_CONV), sm_ref, cols(COL_Z + SSD_W, GDN_W), gdn_cst_ref, gdn_st_ref, gdn_cw_ref,
                     gb_ref, galog_ref, mixw(SSD_W, GDN_W), en_gdn_ref, y_gdn_ref, gdn_cst_out_ref, gdn_st_out_ref)
    _mem_decode_step(cols(COL_QMEM, MEM_W), k_ref, v_ref, cols(COL_Z + SSD_W + GDN_W, MEM_W), mixw(SSD_W + GDN_W, MEM_W),
                     y_mem_ref)


def _decode(proj, small, ssd_cst, ssd_st, gdn_cst, gdn_st, mem_k, mem_v, consts):
    rows = proj.shape[0]
    per_row = lambda a: pl.BlockSpec((DEC_ROWS,) + a.shape[1:], lambda i, nd=a.ndim: (i,) + (0,) * (nd - 1))
    whole = lambda a: pl.BlockSpec(a.shape, lambda i, nd=a.ndim: (0,) * nd)
    streamed = (proj, small, ssd_cst, ssd_st, gdn_cst, gdn_st, mem_k, mem_v)
    out_shape = [jax.ShapeDtypeStruct((rows, w), F32) for w in (SSD_W, GDN_W, MEM_W)]
    out_shape += [jax.ShapeDtypeStruct(a.shape, F32) for a in (ssd_cst, ssd_st, gdn_cst, gdn_st)]
    return pl.pallas_call(
        _decode_kernel,
        grid=(rows // DEC_ROWS,),
        in_specs=[per_row(a) for a in streamed] + [whole(a) for a in consts],
        out_specs=[per_row(a) for a in out_shape],
        out_shape=out_shape,
        compiler_params=_params("parallel"),
        name="decode",
    )(*streamed, *consts)


IN_DT = SSD_CONV
IN_QKV = IN_DT + SSD_HEADS
IN_B = IN_QKV + GDN_CONV
IN_QMEM = IN_B + 2 * GDN_HEADS
IN_COLS = IN_QMEM + MEM_W + MIX_W
PREP_COLS = 256


def _prep_w_in_kernel(w_ref, main_ref, small_ref):
    main_ref[COL_QKV:COL_QKV + GDN_CONV, :] = w_ref[IN_QKV:IN_B, :].astype(BF16)
    main_ref[COL_XBC:COL_XBC + SSD_CONV, :] = w_ref[:SSD_CONV, :].astype(BF16)
    main_ref[COL_QMEM:N_MAIN, :] = w_ref[IN_QMEM:IN_COLS, :].astype(BF16)
    small_ref[SM_DT:SM_B, :] = w_ref[IN_DT:IN_QKV, :].astype(BF16)
    small_ref[SM_B:SM_A + GDN_HEADS, :] = w_ref[IN_B:IN_QMEM, :].astype(BF16)
    small_ref[SM_A + GDN_HEADS:, :] = jnp.zeros((LANES - SM_A - GDN_HEADS, w_ref.shape[1]), BF16)


def _prep_w_in(w_t):
    n, k = w_t.shape
    assert n == IN_COLS
    return pl.pallas_call(
        _prep_w_in_kernel,
        grid=(k // PREP_COLS,),
        in_specs=[pl.BlockSpec((IN_COLS, PREP_COLS), lambda i: (0, i))],
        out_specs=[pl.BlockSpec((N_MAIN, PREP_COLS), lambda i: (0, i)), pl.BlockSpec((LANES, PREP_COLS), lambda i: (0, i))],
        out_shape=[jax.ShapeDtypeStruct((N_MAIN, k), BF16), jax.ShapeDtypeStruct((LANES, k), BF16)],
        compiler_params=_params("parallel"),
        name="prep_w_in",
    )(w_t)


def _head_expander(heads, first_lane, width):
    m = np.zeros((LANES, heads * width), np.float32)
    for h in range(heads):
        m[first_lane + h, h * width:(h + 1) * width] = 1.0
    return jnp.asarray(m, BF16)


def _lane_row(vec, first_lane):
    return jnp.zeros((1, LANES), F32).at[0, first_lane:first_lane + vec.shape[0]].set(vec.astype(F32))


def kernel(x_prompt, x_sample, mem_prompt, state_ssd_conv, state_ssd, state_gdn_conv, state_gdn, cache_mem_k, cache_mem_v, norm_w, w_in, ssd_conv_w, ssd_conv_b, ssd_dt_bias, ssd_A_log, ssd_D, gdn_conv_w, gdn_dt_bias, gdn_A_log, mem_norm_w, w_mem_kv, mix_norm_w, w_out, final_norm_w):
    bp, seq, d = x_prompt.shape
    bs = x_sample.shape[0]
    assert (d, seq % CHUNK, bs % DEC_ROWS, norm_w.shape[0]) == (D_MODEL, 0, 0, 1)

    w_main, w_small = _prep_w_in(w_in[0].T)
    wo = w_out[0].astype(BF16)
    wo1, wo2, wo3 = wo[:SSD_W], wo[SSD_W:SSD_W + GDN_W], wo[SSD_W + GDN_W:]
    nw = norm_w[0][None, :]
    mixw = mix_norm_w[0][None, :]
    mixw1, mixw2, mixw3 = mixw[:, :SSD_W], mixw[:, SSD_W:SSD_W + GDN_W], mixw[:, SSD_W + GDN_W:]
    fw = final_norm_w[None, :]
    ssd_dtb = _lane_row(ssd_dt_bias[0], SM_DT)
    ssd_alog = _lane_row(ssd_A_log[0], SM_DT)
    ssd_dexp = jnp.repeat(ssd_D[0].astype(F32), SSD_P)[None, :]
    gdn_b = _lane_row(gdn_dt_bias[0], SM_A)
    gdn_alog = _lane_row(gdn_A_log[0], SM_A)
    e_ssd = _head_expander(SSD_HEADS, SM_DT, SSD_P)
    e_ssd_n = _head_expander(SSD_HEADS, SM_DT, SSD_N)
    e_gdn_n = _head_expander(GDN_HEADS, SM_A, GDN_D)
    ssd_cw, ssd_cb, gdn_cw = ssd_conv_w[0], ssd_conv_b[0][None, :], gdn_conv_w[0]

    xp = x_prompt.reshape(bp * seq, d)
    proj_p, small_p = _norm_matmul(xp, nw, w_main, w_small, _row_tile(bp * seq, PROJ_ROWS), SSD_CONV)
    kv, _ = _norm_matmul(mem_prompt.reshape(bp * MEM_TOKENS, d), mem_norm_w[0][None, :], w_mem_kv[0].T.astype(BF16),
                         jnp.zeros((LANES, d), BF16), _row_tile(bp * MEM_TOKENS, PROJ_ROWS), 2 * MEM_W)
    scan_rows = SCAN_ROWS if seq % SCAN_ROWS == 0 else CHUNK
    y_ssd, tail_ssd, p_ssd = _ssd_prompt(proj_p, small_p, bp, ssd_cw, ssd_cb, ssd_dtb, ssd_alog, ssd_dexp, mixw1, e_ssd,
                                         scan_rows)
    y_gdn, tail_gdn, p_gdn = _gdn_prompt(proj_p, small_p, bp, gdn_cw, gdn_b, gdn_alog, mixw2, scan_rows)
    y_mem = _mem_prompt(proj_p, kv, bp, mixw3, _row_tile(seq, MEM_Q_ROWS))
    y_prompt = _out_proj(y_ssd, y_gdn, y_mem, wo1, wo2, wo3, xp, fw, _row_tile(bp * seq, OUT_ROWS)).reshape(bp, seq, d)

    xs = x_sample.reshape(bs, d)
    proj_s, small_s = _norm_matmul(xs, nw, w_main, w_small, bs, SSD_CONV)
    consts = (ssd_cw, ssd_cb, ssd_dtb, ssd_alog, ssd_dexp, gdn_cw, gdn_b, gdn_alog, mixw, e_ssd, e_ssd_n, e_gdn_n)
    ys_ssd, ys_gdn, ys_mem, s_ssd_conv, s_ssd, s_gdn_conv, s_gdn = _decode(
        proj_s, small_s, state_ssd_conv[0].reshape(bs, -1), state_ssd[0], state_gdn_conv[0].reshape(bs, -1), state_gdn[0],
        cache_mem_k.reshape(bs, MEM_TOKENS * MEM_HEADS, MEM_D), cache_mem_v.reshape(bs, MEM_TOKENS * MEM_HEADS, MEM_D), consts)
    y_sample = _out_proj(ys_ssd, ys_gdn, ys_mem, wo1, wo2, wo3, xs, fw, bs).reshape(bs, 1, d)

    keep = CONV_K - 1
    mem_shape = (1, bp, MEM_TOKENS, MEM_HEADS, MEM_D)
    return (
        y_prompt, y_sample,
        tail_ssd[None, :, SUBLANES - keep:, :], p_ssd[None],
        tail_gdn[None, :, SUBLANES - keep:, :], p_gdn[None],
        kv[:, :MEM_W].reshape(mem_shape), kv[:, MEM_W:].reshape(mem_shape),
        s_ssd_conv.reshape(1, bs, keep, SSD_CONV), s_ssd[None],
        s_gdn_conv.reshape(1, bs, keep, GDN_CONV), s_gdn[None],
    )
```

```python
import functools

import numpy as np
import jax
import jax.numpy as jnp
from jax import lax
from jax.experimental import pallas as pl
from jax.experimental.pallas import tpu as pltpu

F32, BF16 = jnp.float32, jnp.bfloat16

D_MODEL = 2048
SSD_HEADS, SSD_P, SSD_GROUPS, SSD_N = 16, 64, 2, 128
SSD_W = SSD_HEADS * SSD_P
SSD_GW = SSD_W // SSD_GROUPS
SSD_CONV = SSD_W + 2 * SSD_GROUPS * SSD_N
GDN_HEADS, GDN_D = 8, 128
GDN_W = GDN_HEADS * GDN_D
GDN_CONV = 3 * GDN_W
MEM_TOKENS, MEM_HEADS, MEM_D = 256, 4, 128
MEM_W = MEM_HEADS * MEM_D
MIX_W = SSD_W + GDN_W + MEM_W
CONV_K = 4
CHUNK = 64
EPS = 1e-6

LANES = 128
SUBLANES = 8
VMEM_LIMIT = 56 * 1024 * 1024
PROJ_ROWS = 1024
OUT_ROWS = 512
MEM_Q_ROWS = 256
SCAN_ROWS = 256
CONV_PHASES = 4

COL_QKV = 0
COL_XBC = COL_QKV + GDN_CONV
COL_QMEM = COL_XBC + SSD_CONV
COL_Z = COL_QMEM + MEM_W
N_MAIN = COL_Z + MIX_W
SM_DT, SM_B, SM_A = 0, SSD_HEADS, SSD_HEADS + GDN_HEADS


def _dot(a, b):
    return jnp.dot(a, b, preferred_element_type=F32)


def _dot_nt(a, b):
    return lax.dot_general(a, b, (((1,), (1,)), ((), ())), preferred_element_type=F32)


def _dot_tn(a, b):
    return lax.dot_general(a, b, (((0,), (0,)), ((), ())), preferred_element_type=F32)


def _split(x, n):
    parts, r = [], x
    for i in range(n):
        p = r.astype(BF16)
        parts.append(p)
        if i + 1 < n:
            r = r - p.astype(F32)
    return parts


def _sel_left(sel, x, n=3):
    return functools.reduce(lambda a, b: a + b, [_dot(sel, p) for p in _split(x, n)])


def _sel_right(x, sel, n=3):
    return functools.reduce(lambda a, b: a + b, [_dot(p, sel) for p in _split(x, n)])


def _sel_right_nt(x, sel, n=3):
    return functools.reduce(lambda a, b: a + b, [_dot_nt(p, sel) for p in _split(x, n)])


def _transpose_sel(x, n=3):
    eye = _eye(LANES).astype(BF16)
    return functools.reduce(lambda a, b: a + b, [_dot_nt(eye, p) for p in _split(x, n)])


def _eye(n):
    return (lax.broadcasted_iota(jnp.int32, (n, n), 0) == lax.broadcasted_iota(jnp.int32, (n, n), 1)).astype(F32)


def _sigmoid(x):
    return 1.0 / (1.0 + jnp.exp(-x))


def _silu(x):
    return x * _sigmoid(x)


def _softplus(x):
    return jnp.maximum(x, 0.0) + jnp.log1p(jnp.exp(-jnp.abs(x)))


def _params(*sem):
    return pltpu.CompilerParams(dimension_semantics=sem, vmem_limit_bytes=VMEM_LIMIT)


def _row_tile(rows, preferred):
    return preferred if rows % preferred == 0 else rows


def _norm_matmul_kernel(x_ref, nw_ref, w_ref, ws_ref, o_ref, os_ref, h_ref):
    @pl.when(pl.program_id(1) == 0)
    def _():
        x = x_ref[...]
        ms = jnp.mean(x * x, axis=-1, keepdims=True)
        h = (x * lax.rsqrt(ms + EPS) * nw_ref[...]).astype(BF16)
        h_ref[...] = h
        os_ref[...] = _dot_nt(h, ws_ref[...])

    o_ref[...] = _dot_nt(h_ref[...], w_ref[...])


def _norm_matmul(x, nw, w_t, ws_t, tm, tn):
    m, k = x.shape
    n = w_t.shape[0]
    ns = ws_t.shape[0]
    return pl.pallas_call(
        _norm_matmul_kernel,
        grid=(m // tm, n // tn),
        in_specs=[
            pl.BlockSpec((tm, k), lambda i, j: (i, 0)),
            pl.BlockSpec((1, k), lambda i, j: (0, 0)),
            pl.BlockSpec((tn, k), lambda i, j: (j, 0)),
            pl.BlockSpec((ns, k), lambda i, j: (0, 0)),
        ],
        out_specs=[
            pl.BlockSpec((tm, tn), lambda i, j: (i, j)),
            pl.BlockSpec((tm, ns), lambda i, j: (i, 0)),
        ],
        out_shape=[jax.ShapeDtypeStruct((m, n), F32), jax.ShapeDtypeStruct((m, ns), F32)],
        scratch_shapes=[pltpu.VMEM((tm, k), BF16)],
        compiler_params=_params("parallel", "arbitrary"),
        name="norm_matmul",
    )(x, nw, w_t, ws_t)


def _out_proj_kernel(y1_ref, y2_ref, y3_ref, w1_ref, w2_ref, w3_ref, x_ref, fw_ref, o_ref):
    acc = (_dot(y1_ref[...].astype(BF16), w1_ref[...]) + _dot(y2_ref[...].astype(BF16), w2_ref[...])
           + _dot(y3_ref[...].astype(BF16), w3_ref[...]))
    r = x_ref[...] + acc
    ms = jnp.mean(r * r, axis=-1, keepdims=True)
    o_ref[...] = r * lax.rsqrt(ms + EPS) * fw_ref[...]


def _out_proj(y1, y2, y3, w1, w2, w3, x, fw, tm):
    m, d = x.shape
    row = lambda i: (i, 0)
    whole = lambda i: (0, 0)
    return pl.pallas_call(
        _out_proj_kernel,
        grid=(m // tm,),
        in_specs=[
            pl.BlockSpec((tm, y1.shape[1]), row), pl.BlockSpec((tm, y2.shape[1]), row), pl.BlockSpec((tm, y3.shape[1]), row),
            pl.BlockSpec(w1.shape, whole), pl.BlockSpec(w2.shape, whole), pl.BlockSpec(w3.shape, whole),
            pl.BlockSpec((tm, d), row), pl.BlockSpec((1, d), whole),
        ],
        out_specs=pl.BlockSpec((tm, d), row),
        out_shape=jax.ShapeDtypeStruct((m, d), F32),
        compiler_params=_params("parallel"),
        name="out_proj",
    )(y1, y2, y3, w1, w2, w3, x, fw)


def _causal_conv_tile(u_ref, ubuf_ref, cw_ref, cb_ref, out_ref, tail_ref):
    t, width = u_ref.shape
    n = t // CONV_PHASES
    for s in range(width // LANES):
        cs = slice(s * LANES, (s + 1) * LANES)
        ubuf_ref[s, SUBLANES:SUBLANES + t, :] = u_ref[:, cs]
        taps = {d: ubuf_ref[s, pl.ds(SUBLANES + d, n, stride=CONV_PHASES), :] for d in range(1 - CONV_K, CONV_PHASES)}
        w = [cw_ref[j:j + 1, cs] for j in range(CONV_K)]
        for r in range(CONV_PHASES):
            acc = w[CONV_K - 1] * taps[r]
            if cb_ref is not None:
                acc = acc + cb_ref[:, cs]
            for j in range(CONV_K - 1):
                acc = acc + w[j] * taps[r - (CONV_K - 1) + j]
            out_ref[s, pl.ds(r, n, stride=CONV_PHASES), :] = _silu(acc)
        tail = ubuf_ref[s, t:t + SUBLANES, :]
        ubuf_ref[s, 0:SUBLANES, :] = tail
        tail_ref[0, :, cs] = tail


def _head_norm_gate(y, msq, width, z, mixw):
    return y * lax.rsqrt(msq * (1.0 / width) + EPS) * mixw * _silu(z)


def _ssd_prompt_kernel(xbc_ref, sm_ref, z_ref, cw_ref, cb_ref, dtb_ref, alog_ref, dexp_ref, mixw_ref, e_ref,
                       y_ref, tail_ref, state_ref, ubuf_ref, conv_ref, h_ref):
    c = pl.program_id(1)
    t = xbc_ref.shape[0]
    subs = range(t // CHUNK)
    groups = range(SSD_GROUPS)
    blocks = range(SSD_GW // LANES)

    @pl.when(c == 0)
    def _():
        ubuf_ref[:, 0:SUBLANES, :] = jnp.zeros((SSD_CONV // LANES, SUBLANES, LANES), F32)
        h_ref[...] = jnp.zeros_like(h_ref)

    _causal_conv_tile(xbc_ref, ubuf_ref, cw_ref, cb_ref, conv_ref, tail_ref)
    xs = jnp.concatenate([conv_ref[s] for s in range(SSD_W // LANES)], axis=1)
    e = e_ref[...]
    rows = [slice(j * CHUNK, (j + 1) * CHUNK) for j in subs]
    gs = [slice(g * SSD_GW, (g + 1) * SSD_GW) for g in groups]

    dt = _softplus(sm_ref[...] + dtb_ref[...])
    a = dt * (-jnp.exp(alog_ref[...]))
    rt = lax.broadcasted_iota(jnp.int32, (t, t), 0)
    ct = lax.broadcasted_iota(jnp.int32, (t, t), 1)
    chunk_causal = (rt >= ct) & (rt // CHUNK == ct // CHUNK)
    cum = _sel_left(chunk_causal.astype(BF16), a)
    cum_t = _transpose_sel(cum)
    ecum = jnp.exp(cum)
    wend = jnp.concatenate([jnp.exp(cum[(j + 1) * CHUNK - 1:(j + 1) * CHUNK, :] - cum[rows[j]]) for j in subs], axis=0)
    dt_x = _sel_right(dt, e, 2)
    ecum_x = _sel_right(ecum, e, 2)
    wend_x = _sel_right(wend, e, 2)

    xdt = xs * dt_x
    xdt_b = xdt.astype(BF16)
    xw_b = (xdt * wend_x).astype(BF16)
    causal = lax.broadcasted_iota(jnp.int32, (CHUNK, CHUNK), 0) >= lax.broadcasted_iota(jnp.int32, (CHUNK, CHUNK), 1)
    left = lax.broadcasted_iota(jnp.int32, (CHUNK, LANES), 1) < SSD_P

    jg = [(j, g) for j in subs for g in groups]
    b_slab, c_slab = SSD_W // LANES, SSD_W // LANES + SSD_GROUPS
    bmat = {(j, g): conv_ref[b_slab + g, rows[j], :].astype(BF16) for j, g in jg}
    cmat = {(j, g): conv_ref[c_slab + g, rows[j], :].astype(BF16) for j, g in jg}
    cb = {p: _dot_nt(cmat[p], bmat[p]) for p in jg}
    inc = {(j, g): _dot_tn(bmat[j, g], xw_b[rows[j], gs[g]]) for j, g in jg}
    scores = {}
    for j, g in jg:
        for blk in blocks:
            for half in range(2):
                h = (g * len(blocks) + blk) * 2 + half
                diff = cum[rows[j], h:h + 1] - cum_t[h:h + 1, rows[j]]
                lmat = jnp.where(causal, jnp.exp(jnp.minimum(diff, 0.0)), 0.0)
                scores[j, g, blk, half] = (cb[j, g] * lmat).astype(BF16)
    intra = {}
    for j, g in jg:
        for blk in blocks:
            lanes = slice(g * SSD_GW + blk * LANES, g * SSD_GW + (blk + 1) * LANES)
            xb = xdt_b[rows[j], lanes]
            zero = jnp.zeros_like(xb)
            intra[j, g, blk] = (_dot(scores[j, g, blk, 0], jnp.where(left, xb, zero))
                                + _dot(scores[j, g, blk, 1], jnp.where(left, zero, xb)))

    state = {(0, g): h_ref[:, gs[g]] for g in groups}
    for j in subs:
        last = (j + 1) * CHUNK - 1
        for g in groups:
            state[j + 1, g] = state[j, g] * ecum_x[last:last + 1, gs[g]] + inc[j, g]
    for g in groups:
        h_ref[:, gs[g]] = state[len(subs), g]
    inter = {p: _dot(cmat[p], state[p].astype(BF16)) for p in jg}
    inter_x = jnp.concatenate([jnp.concatenate([inter[j, g] for g in groups], axis=1) for j in subs], axis=0) * ecum_x
    intra_x = jnp.concatenate([jnp.concatenate([intra[j, g, blk] for g in groups for blk in blocks], axis=1)
                               for j in subs], axis=0)
    y = intra_x + inter_x + dexp_ref[...] * xs
    msq = _sel_right(_sel_right_nt(y * y, e, 2), e, 2)
    y_ref[...] = _head_norm_gate(y, msq, SSD_P, z_ref[...], mixw_ref[...]).astype(BF16)

    @pl.when(c == pl.num_programs(1) - 1)
    def _():
        state_ref[0] = h_ref[...].T.reshape(SSD_HEADS, SSD_P, SSD_N)


N_SSD_IN, N_SSD_OUT = 10, 3


def _ssd_prompt_decode_kernel(*refs):
    ssd_in, refs = refs[:N_SSD_IN], refs[N_SSD_IN:]
    dec_in, refs = refs[:N_DEC_IN], refs[N_DEC_IN:]
    ssd_out, refs = refs[:N_SSD_OUT], refs[N_SSD_OUT:]
    dec_out, ssd_scratch = refs[:N_DEC_OUT], refs[N_DEC_OUT:]
    _ssd_prompt_kernel(*ssd_in, *ssd_out, *ssd_scratch)
    _decode_rows_kernel(*dec_in, *dec_out)


def _ssd_prompt_decode(proj, small, batch, cw, cb, dtb, alog, dexp, mixw, e, tile, dec_row_mats, dec_states, dec_consts):
    rows = proj.shape[0]
    nc = rows // batch // tile
    row = lambda b, c: (b * nc + c, 0)
    whole = lambda b, c: (0, 0)
    dec_arrays, dec_in_specs, dec_out_shape, dec_out_specs = _decode_operands(
        dec_row_mats, dec_states, dec_consts, batch * nc, lambda b, c: b * nc + c)
    return pl.pallas_call(
        _ssd_prompt_decode_kernel,
        grid=(batch, nc),
        in_specs=[
            pl.BlockSpec((tile, SSD_CONV), lambda b, c: (b * nc + c, COL_XBC // SSD_CONV)),
            pl.BlockSpec((tile, LANES), row),
            pl.BlockSpec((tile, SSD_W), lambda b, c: (b * nc + c, COL_Z // SSD_W)),
            pl.BlockSpec(cw.shape, whole), pl.BlockSpec(cb.shape, whole), pl.BlockSpec(dtb.shape, whole),
            pl.BlockSpec(alog.shape, whole), pl.BlockSpec(dexp.shape, whole), pl.BlockSpec(mixw.shape, whole),
            pl.BlockSpec(e.shape, whole),
        ] + dec_in_specs,
        out_specs=[
            pl.BlockSpec((tile, SSD_W), row),
            pl.BlockSpec((1, SUBLANES, SSD_CONV), lambda b, c: (b, 0, 0)),
            pl.BlockSpec((1, SSD_HEADS, SSD_P, SSD_N), lambda b, c: (b, 0, 0, 0)),
        ] + dec_out_specs,
        out_shape=[
            jax.ShapeDtypeStruct((rows, SSD_W), BF16),
            jax.ShapeDtypeStruct((batch, SUBLANES, SSD_CONV), F32),
            jax.ShapeDtypeStruct((batch, SSD_HEADS, SSD_P, SSD_N), F32),
        ] + dec_out_shape,
        scratch_shapes=[pltpu.VMEM((SSD_CONV // LANES, tile + SUBLANES, LANES), F32),
                        pltpu.VMEM((SSD_CONV // LANES, tile, LANES), F32), pltpu.VMEM((SSD_N, SSD_W), F32)],
        compiler_params=_params("arbitrary", "arbitrary"),
        name="ssd_prompt_decode",
    )(proj, small, proj, cw, cb, dtb, alog, dexp, mixw, e, *dec_arrays)


def _unit_lower_inverses(a_stricts, ri, ci):
    t = a_stricts[0].shape[0]
    eye = _eye(t)
    first = (ri == ci + 1) & (ci % 2 == 0)
    invs = [eye - jnp.where(first, a, 0.0) for a in a_stricts]
    s = 2
    while s < t:
        sel = (ri // (2 * s) == ci // (2 * s)) & ((ri // s) % 2 == 1) & ((ci // s) % 2 == 0)
        inv_bs = [inv.astype(BF16) for inv in invs]
        lefts = [_dot(inv_b, jnp.where(sel, a, 0.0).astype(BF16)).astype(BF16) for inv_b, a in zip(inv_bs, a_stricts)]
        invs = [inv - _dot(left, inv_b) for inv, left, inv_b in zip(invs, lefts, inv_bs)]
        s *= 2
    return invs


def _gdn_prompt_kernel(qkv_ref, sm_ref, z_ref, cw_ref, gb_ref, galog_ref, mixw_ref,
                       y_ref, tail_ref, state_ref, ubuf_ref, conv_ref, s_ref):
    c = pl.program_id(1)
    t = qkv_ref.shape[0]
    subs = range(t // CHUNK)
    heads = range(GDN_HEADS)

    @pl.when(c == 0)
    def _():
        ubuf_ref[:, 0:SUBLANES, :] = jnp.zeros((GDN_CONV // LANES, SUBLANES, LANES), F32)
        s_ref[...] = jnp.zeros_like(s_ref)

    _causal_conv_tile(qkv_ref, ubuf_ref, cw_ref, None, conv_ref, tail_ref)

    sm = sm_ref[...]
    beta = _sigmoid(sm)
    g = -jnp.exp(galog_ref[...]) * _softplus(sm + gb_ref[...])
    rt = lax.broadcasted_iota(jnp.int32, (t, t), 0)
    ct = lax.broadcasted_iota(jnp.int32, (t, t), 1)
    chunk_causal = (rt >= ct) & (rt // CHUNK == ct // CHUNK)
    gc = _sel_left(chunk_causal.astype(BF16), g)
    gc_t = _transpose_sel(gc)
    eg = jnp.exp(gc)
    ri = lax.broadcasted_iota(jnp.int32, (CHUNK, CHUNK), 0)
    ci = lax.broadcasted_iota(jnp.int32, (CHUNK, CHUNK), 1)
    causal = ri >= ci
    strict = ri > ci

    rows = [slice(j * CHUNK, (j + 1) * CHUNK) for j in subs]
    hs = [slice(h * GDN_D, (h + 1) * GDN_D) for h in heads]
    la = [SM_A + h for h in heads]
    pairs = [(j, h) for j in subs for h in heads]
    q, k, kb, vb, kbg, qg, decay = {}, {}, {}, {}, {}, {}, {}
    for h in heads:
        qf, kf, vf = conv_ref[h], conv_ref[GDN_HEADS + h], conv_ref[2 * GDN_HEADS + h]
        qf = qf * lax.rsqrt(jnp.sum(qf * qf, axis=-1, keepdims=True) + EPS) * (GDN_D ** -0.5)
        kf = kf * lax.rsqrt(jnp.sum(kf * kf, axis=-1, keepdims=True) + EPS)
        b_col = beta[:, SM_B + h:SM_B + h + 1]
        eg_col = eg[:, la[h]:la[h] + 1]
        kbf = kf * b_col
        vbf, kbgf, qgf = (vf * b_col).astype(BF16), (kbf * eg_col).astype(BF16), (qf * eg_col).astype(BF16)
        for j in subs:
            q[j, h], k[j, h], kb[j, h] = qf[rows[j]].astype(BF16), kf[rows[j]], kbf[rows[j]].astype(BF16)
            vb[j, h], kbg[j, h], qg[j, h] = vbf[rows[j]], kbgf[rows[j]], qgf[rows[j]]
            diff = gc[rows[j], la[h]:la[h] + 1] - gc_t[la[h]:la[h] + 1, rows[j]]
            decay[j, h] = jnp.where(causal, jnp.exp(jnp.minimum(diff, 0.0)), 0.0)
    k_b = {p: k[p].astype(BF16) for p in pairs}
    kk = {p: _dot_nt(kb[p], k_b[p]) for p in pairs}
    qk = {p: _dot_nt(q[p], k_b[p]) for p in pairs}
    a_strict = [jnp.where(strict, kk[p] * decay[p], 0.0) for p in pairs]
    attn = {p: (qk[p] * decay[p]).astype(BF16) for p in pairs}
    t_inv = dict(zip(pairs, [x.astype(BF16) for x in _unit_lower_inverses(a_strict, ri, ci)]))
    u = {p: _dot(t_inv[p], vb[p]) for p in pairs}
    wk = {p: _dot(t_inv[p], kbg[p]).astype(BF16) for p in pairs}

    state = [s_ref[h] for h in heads]
    for j in subs:
        g_last = gc[(j + 1) * CHUNK - 1:(j + 1) * CHUNK, :]
        eend = jnp.exp(g_last - gc[rows[j]])
        elast = jnp.exp(g_last)
        s_b = [x.astype(BF16) for x in state]
        v_new = [(u[j, h] - _dot(wk[j, h], s_b[h])).astype(BF16) for h in heads]
        k_end = [(k[j, h] * eend[:, la[h]:la[h] + 1]).astype(BF16) for h in heads]
        s_inc = [_dot_tn(k_end[h], v_new[h]) for h in heads]
        state = [state[h] * elast[:, la[h]:la[h] + 1] + s_inc[h] for h in heads]
        o = [_dot(qg[j, h], s_b[h]) + _dot(attn[j, h], v_new[h]) for h in heads]
        msq = [jnp.sum(x * x, axis=-1, keepdims=True) for x in o]
        for h in heads:
            y_ref[rows[j], hs[h]] = _head_norm_gate(o[h], msq[h], GDN_D, z_ref[rows[j], hs[h]], mixw_ref[:, hs[h]]).astype(BF16)
    for h in heads:
        s_ref[h] = state[h]

    @pl.when(c == pl.num_programs(1) - 1)
    def _():
        state_ref[0] = s_ref[...]


def _gdn_prompt(proj, small, batch, cw, gb, galog, mixw, tile):
    rows = proj.shape[0]
    nc = rows // batch // tile
    row = lambda b, c: (b * nc + c, 0)
    whole = lambda b, c: (0, 0)
    return pl.pallas_call(
        _gdn_prompt_kernel,
        grid=(batch, nc),
        in_specs=[
            pl.BlockSpec((tile, GDN_CONV), lambda b, c: (b * nc + c, COL_QKV // GDN_CONV)),
            pl.BlockSpec((tile, LANES), row),
            pl.BlockSpec((tile, GDN_W), lambda b, c: (b * nc + c, (COL_Z + SSD_W) // GDN_W)),
            pl.BlockSpec(cw.shape, whole), pl.BlockSpec(gb.shape, whole), pl.BlockSpec(galog.shape, whole),
            pl.BlockSpec(mixw.shape, whole),
        ],
        out_specs=[
            pl.BlockSpec((tile, GDN_W), row),
            pl.BlockSpec((1, SUBLANES, GDN_CONV), lambda b, c: (b, 0, 0)),
            pl.BlockSpec((1, GDN_HEADS, GDN_D, GDN_D), lambda b, c: (b, 0, 0, 0)),
        ],
        out_shape=[
            jax.ShapeDtypeStruct((rows, GDN_W), BF16),
            jax.ShapeDtypeStruct((batch, SUBLANES, GDN_CONV), F32),
            jax.ShapeDtypeStruct((batch, GDN_HEADS, GDN_D, GDN_D), F32),
        ],
        scratch_shapes=[pltpu.VMEM((GDN_CONV // LANES, tile + SUBLANES, LANES), F32),
                        pltpu.VMEM((GDN_CONV // LANES, tile, LANES), F32), pltpu.VMEM((GDN_HEADS, GDN_D, GDN_D), F32)],
        compiler_params=_params("parallel", "arbitrary"),
        name="gdn_prompt",
    )(proj, small, proj, cw, gb, galog, mixw)


def _mem_prompt_kernel(q_ref, k_ref, v_ref, z_ref, mixw_ref, y_ref):
    heads = range(MEM_HEADS)
    hs = [slice(h * MEM_D, (h + 1) * MEM_D) for h in heads]
    s = [_dot_nt(q_ref[:, hs[h]].astype(BF16), k_ref[:, hs[h]].astype(BF16)) * (MEM_D ** -0.5) for h in heads]
    e = [jnp.exp(x - jnp.max(x, axis=-1, keepdims=True)) for x in s]
    p = [(x / jnp.sum(x, axis=-1, keepdims=True)).astype(BF16) for x in e]
    o = [_dot(p[h], v_ref[:, hs[h]].astype(BF16)) for h in heads]
    msq = [jnp.sum(x * x, axis=-1, keepdims=True) for x in o]
    for h in heads:
        y_ref[:, hs[h]] = _head_norm_gate(o[h], msq[h], MEM_D, z_ref[:, hs[h]], mixw_ref[:, hs[h]]).astype(BF16)


def _mem_prompt(proj, kv, batch, mixw, tq):
    rows = proj.shape[0]
    nq = rows // batch // tq
    return pl.pallas_call(
        _mem_prompt_kernel,
        grid=(batch, nq),
        in_specs=[
            pl.BlockSpec((tq, MEM_W), lambda b, i: (b * nq + i, COL_QMEM // MEM_W)),
            pl.BlockSpec((MEM_TOKENS, MEM_W), lambda b, i: (b, 0)),
            pl.BlockSpec((MEM_TOKENS, MEM_W), lambda b, i: (b, 1)),
            pl.BlockSpec((tq, MEM_W), lambda b, i: (b * nq + i, (COL_Z + SSD_W + GDN_W) // MEM_W)),
            pl.BlockSpec(mixw.shape, lambda b, i: (0, 0)),
        ],
        out_specs=pl.BlockSpec((tq, MEM_W), lambda b, i: (b * nq + i, 0)),
        out_shape=jax.ShapeDtypeStruct((rows, MEM_W), BF16),
        compiler_params=_params("parallel", "parallel"),
        name="mem_prompt",
    )(proj, kv, kv, proj, mixw)


def _conv_step(u, st, cw_ref, bias, width):
    acc = cw_ref[CONV_K - 1:CONV_K, :] * u
    if bias is not None:
        acc = acc + bias
    for j in range(CONV_K - 1):
        acc = acc + cw_ref[j:j + 1, :] * st[:, j * width:(j + 1) * width]
    return _silu(acc), jnp.concatenate([st[:, width:], u], axis=1)


def _rows_to_columns(x):
    pad = jnp.zeros((LANES - x.shape[0], x.shape[1]), F32)
    return jnp.concatenate([x, pad], axis=0).T


def _pick_rows(parts):
    rid = lax.broadcasted_iota(jnp.int32, parts[0].shape, 0)
    out = parts[0]
    for i in range(1, len(parts)):
        out = jnp.where(rid == i, parts[i], out)
    return out


def _ssd_decode_step(xbc_ref, sm_ref, z_ref, cst_ref, st_ref, cw_ref, cb_ref, dtb_ref, alog_ref, dexp_ref,
                     mixw_ref, e_ref, en_ref, y_ref, cst_out_ref, st_out_ref):
    xbc, cst_new = _conv_step(xbc_ref[...], cst_ref[...], cw_ref, cb_ref[...], SSD_CONV)
    cst_out_ref[...] = cst_new
    xs = xbc[:, :SSD_W]
    e = e_ref[...]
    dt = _softplus(sm_ref[...] + dtb_ref[...])
    dec = jnp.exp(dt * (-jnp.exp(alog_ref[...])))
    xd_t = _rows_to_columns(xs * _sel_right(dt, e))
    dec_n = _sel_right(dec, en_ref[...])

    y_groups = []
    for g in range(SSD_GROUPS):
        b_g = xbc[:, SSD_W + g * SSD_N:SSD_W + (g + 1) * SSD_N]
        c_g = xbc[:, SSD_W + (SSD_GROUPS + g) * SSD_N:SSD_W + (SSD_GROUPS + g + 1) * SSD_N].astype(BF16)
        per_row = []
        for i in range(xs.shape[0]):
            new = []
            for r in range(SSD_HEADS // SSD_GROUPS):
                h = g * (SSD_HEADS // SSD_GROUPS) + r
                col = xd_t[h * SSD_P:(h + 1) * SSD_P, i:i + 1]
                hn = st_ref[i, h] * dec_n[i:i + 1, h * SSD_N:(h + 1) * SSD_N] + col * b_g[i:i + 1, :]
                st_out_ref[i, h] = hn
                new.append(hn)
            hg = jnp.concatenate(new, axis=0).astype(BF16)
            per_row.append(_dot_nt(c_g, hg))
        y_groups.append(_pick_rows(per_row))
    y = jnp.concatenate(y_groups, axis=1) + dexp_ref[...] * xs
    msq = _sel_right(_sel_right_nt(y * y, e, 2), e, 2)
    y_ref[...] = _head_norm_gate(y, msq, SSD_P, z_ref[...], mixw_ref[...])


def _gdn_decode_step(qkv_ref, sm_ref, z_ref, cst_ref, st_ref, cw_ref, gb_ref, galog_ref, mixw_ref, en_ref,
                     y_ref, cst_out_ref, st_out_ref):
    qkv, cst_new = _conv_step(qkv_ref[...], cst_ref[...], cw_ref, None, GDN_CONV)
    cst_out_ref[...] = cst_new
    sm = sm_ref[...]
    beta = _sigmoid(sm)
    eg = jnp.exp(-jnp.exp(galog_ref[...]) * _softplus(sm + gb_ref[...]))
    eg_n = _sel_right(eg, en_ref[...])

    qs, ks = [], []
    for h in range(GDN_HEADS):
        q = qkv[:, h * GDN_D:(h + 1) * GDN_D]
        k = qkv[:, GDN_W + h * GDN_D:GDN_W + (h + 1) * GDN_D]
        qs.append(q * lax.rsqrt(jnp.sum(q * q, axis=-1, keepdims=True) + EPS) * (GDN_D ** -0.5))
        ks.append(k * lax.rsqrt(jnp.sum(k * k, axis=-1, keepdims=True) + EPS))
    k_t = _rows_to_columns(jnp.concatenate(ks, axis=1))

    for h in range(GDN_HEADS):
        hs = slice(h * GDN_D, (h + 1) * GDN_D)
        q, k = qs[h], ks[h]
        v = qkv[:, 2 * GDN_W + h * GDN_D:2 * GDN_W + (h + 1) * GDN_D]
        q_b, k_b = q.astype(BF16), k.astype(BF16)
        ks_rows, qs_rows = [], []
        for i in range(q.shape[0]):
            s_b = st_ref[i, h].astype(BF16)
            ks_rows.append(_dot(k_b, s_b))
            qs_rows.append(_dot(q_b, s_b))
        k_s, q_s = _pick_rows(ks_rows), _pick_rows(qs_rows)
        eg_h = eg_n[:, hs]
        v_new = beta[:, SM_B + h:SM_B + h + 1] * (v - eg_h * k_s)
        o = eg_h * q_s + jnp.sum(q * k, axis=-1, keepdims=True) * v_new
        for i in range(q.shape[0]):
            col = k_t[hs, i:i + 1]
            st_out_ref[i, h] = st_ref[i, h] * eg_h[i:i + 1, :] + col * v_new[i:i + 1, :]
        msq = jnp.sum(o * o, axis=-1, keepdims=True)
        y_ref[:, hs] = _head_norm_gate(o, msq, GDN_D, z_ref[:, hs], mixw_ref[:, hs])


def _mem_decode_step(q_ref, k_ref, v_ref, z_ref, mixw_ref, y_ref):
    heads, rows = range(MEM_HEADS), range(q_ref.shape[0])
    hs = [slice(h * MEM_D, (h + 1) * MEM_D) for h in heads]
    win = [pl.ds(h, MEM_TOKENS, stride=MEM_HEADS) for h in heads]
    q = [q_ref[:, hs[h]].astype(BF16) for h in heads]
    s = [_pick_rows([_dot_nt(q[h], k_ref[i, win[h], :].astype(BF16)) for i in rows]) * (MEM_D ** -0.5) for h in heads]
    e = [jnp.exp(x - jnp.max(x, axis=-1, keepdims=True)) for x in s]
    p = [(x / jnp.sum(x, axis=-1, keepdims=True)).astype(BF16) for x in e]
    o = [_pick_rows([_dot(p[h], v_ref[i, win[h], :].astype(BF16)) for i in rows]) for h in heads]
    msq = [jnp.sum(x * x, axis=-1, keepdims=True) for x in o]
    for h in heads:
        y_ref[:, hs[h]] = _head_norm_gate(o[h], msq[h], MEM_D, z_ref[:, hs[h]], mixw_ref[:, hs[h]])


N_DEC_IN, N_DEC_OUT = 20, 7


def _decode_rows_kernel(proj_ref, sm_ref, ssd_cst_ref, gdn_cst_ref, ssd_st_ref, gdn_st_ref, k_ref, v_ref,
                        ssd_cw_ref, ssd_cb_ref, dtb_ref, alog_ref, dexp_ref, gdn_cw_ref, gb_ref, galog_ref, mixw_ref,
                        e_ref, en_ssd_ref, en_gdn_ref,
                        y_ssd_ref, y_gdn_ref, y_mem_ref, ssd_cst_out_ref, gdn_cst_out_ref, ssd_st_out_ref, gdn_st_out_ref):
    proj, sm, ssd_cst, gdn_cst = proj_ref.at[0], sm_ref.at[0], ssd_cst_ref.at[0], gdn_cst_ref.at[0]
    y_ssd, y_gdn, y_mem = y_ssd_ref.at[0], y_gdn_ref.at[0], y_mem_ref.at[0]
    cols = lambda start, width: proj.at[:, start:start + width]
    mixw = lambda start, width: mixw_ref.at[:, start:start + width]
    _ssd_decode_step(cols(COL_XBC, SSD_CONV), sm, cols(COL_Z, SSD_W), ssd_cst, ssd_st_ref, ssd_cw_ref, ssd_cb_ref,
                     dtb_ref, alog_ref, dexp_ref, mixw(0, SSD_W), e_ref, en_ssd_ref, y_ssd, ssd_cst_out_ref.at[0], ssd_st_out_ref)
    _gdn_decode_step(cols(COL_QKV, GDN_CONV), sm, cols(COL_Z + SSD_W, GDN_W), gdn_cst, gdn_st_ref, gdn_cw_ref,
                     gb_ref, galog_ref, mixw(SSD_W, GDN_W), en_gdn_ref, y_gdn, gdn_cst_out_ref.at[0], gdn_st_out_ref)
    _mem_decode_step(cols(COL_QMEM, MEM_W), k_ref, v_ref, cols(COL_Z + SSD_W + GDN_W, MEM_W), mixw(SSD_W + GDN_W, MEM_W),
                     y_mem)


def _decode_operands(row_mats, states, consts, steps, step_index):
    rows = row_mats[0].shape[0] // steps
    grouped = [a.reshape(steps, rows, a.shape[1]) for a in row_mats]
    mat_spec = lambda a: pl.BlockSpec((1,) + a.shape[1:], lambda *g: (step_index(*g), 0, 0))
    state_spec = lambda a: pl.BlockSpec((rows,) + a.shape[1:], lambda *g, nd=a.ndim: (step_index(*g),) + (0,) * (nd - 1))
    whole = lambda a: pl.BlockSpec(a.shape, lambda *g, nd=a.ndim: (0,) * nd)
    in_specs = [mat_spec(a) for a in grouped] + [state_spec(a) for a in states] + [whole(a) for a in consts]
    y_shapes = [jax.ShapeDtypeStruct((steps, rows, w), F32) for w in (SSD_W, GDN_W, MEM_W)]
    cst_shapes = [jax.ShapeDtypeStruct(a.shape, F32) for a in grouped[2:]]
    st_shapes = [jax.ShapeDtypeStruct(a.shape, F32) for a in states[:2]]
    out_shape = y_shapes + cst_shapes + st_shapes
    out_specs = [mat_spec(a) for a in y_shapes + cst_shapes] + [state_spec(a) for a in st_shapes]
    return tuple(grouped) + tuple(states) + tuple(consts), in_specs, out_shape, out_specs


IN_DT = SSD_CONV
IN_QKV = IN_DT + SSD_HEADS
IN_B = IN_QKV + GDN_CONV
IN_QMEM = IN_B + 2 * GDN_HEADS
IN_COLS = IN_QMEM + MEM_W + MIX_W
PREP_COLS = 256


def _prep_w_in_kernel(w_ref, main_ref, small_ref):
    main_ref[COL_QKV:COL_QKV + GDN_CONV, :] = w_ref[IN_QKV:IN_B, :].astype(BF16)
    main_ref[COL_XBC:COL_XBC + SSD_CONV, :] = w_ref[:SSD_CONV, :].astype(BF16)
    main_ref[COL_QMEM:N_MAIN, :] = w_ref[IN_QMEM:IN_COLS, :].astype(BF16)
    small_ref[SM_DT:SM_B, :] = w_ref[IN_DT:IN_QKV, :].astype(BF16)
    small_ref[SM_B:SM_A + GDN_HEADS, :] = w_ref[IN_B:IN_QMEM, :].astype(BF16)
    small_ref[SM_A + GDN_HEADS:, :] = jnp.zeros((LANES - SM_A - GDN_HEADS, w_ref.shape[1]), BF16)


def _prep_w_in(w_t):
    n, k = w_t.shape
    assert n == IN_COLS
    return pl.pallas_call(
        _prep_w_in_kernel,
        grid=(k // PREP_COLS,),
        in_specs=[pl.BlockSpec((IN_COLS, PREP_COLS), lambda i: (0, i))],
        out_specs=[pl.BlockSpec((N_MAIN, PREP_COLS), lambda i: (0, i)), pl.BlockSpec((LANES, PREP_COLS), lambda i: (0, i))],
        out_shape=[jax.ShapeDtypeStruct((N_MAIN, k), BF16), jax.ShapeDtypeStruct((LANES, k), BF16)],
        compiler_params=_params("parallel"),
        name="prep_w_in",
    )(w_t)


def _head_expander(heads, first_lane, width):
    m = np.zeros((LANES, heads * width), np.float32)
    for h in range(heads):
        m[first_lane + h, h * width:(h + 1) * width] = 1.0
    return jnp.asarray(m, BF16)


def _lane_row(vec, first_lane):
    return jnp.zeros((1, LANES), F32).at[0, first_lane:first_lane + vec.shape[0]].set(vec.astype(F32))


def kernel(x_prompt, x_sample, mem_prompt, state_ssd_conv, state_ssd, state_gdn_conv, state_gdn, cache_mem_k, cache_mem_v, norm_w, w_in, ssd_conv_w, ssd_conv_b, ssd_dt_bias, ssd_A_log, ssd_D, gdn_conv_w, gdn_dt_bias, gdn_A_log, mem_norm_w, w_mem_kv, mix_norm_w, w_out, final_norm_w):
    bp, seq, d = x_prompt.shape
    bs = x_sample.shape[0]
    assert (d, seq % CHUNK, norm_w.shape[0]) == (D_MODEL, 0, 1)

    w_main, w_small = _prep_w_in(w_in[0].T)
    wo = w_out[0].astype(BF16)
    wo1, wo2, wo3 = wo[:SSD_W], wo[SSD_W:SSD_W + GDN_W], wo[SSD_W + GDN_W:]
    nw = norm_w[0][None, :]
    mixw = mix_norm_w[0][None, :]
    mixw1, mixw2, mixw3 = mixw[:, :SSD_W], mixw[:, SSD_W:SSD_W + GDN_W], mixw[:, SSD_W + GDN_W:]
    fw = final_norm_w[None, :]
    ssd_dtb = _lane_row(ssd_dt_bias[0], SM_DT)
    ssd_alog = _lane_row(ssd_A_log[0], SM_DT)
    ssd_dexp = jnp.repeat(ssd_D[0].astype(F32), SSD_P)[None, :]
    gdn_b = _lane_row(gdn_dt_bias[0], SM_A)
    gdn_alog = _lane_row(gdn_A_log[0], SM_A)
    e_ssd = _head_expander(SSD_HEADS, SM_DT, SSD_P)
    e_ssd_n = _head_expander(SSD_HEADS, SM_DT, SSD_N)
    e_gdn_n = _head_expander(GDN_HEADS, SM_A, GDN_D)
    ssd_cw, ssd_cb, gdn_cw = ssd_conv_w[0], ssd_conv_b[0][None, :], gdn_conv_w[0]

    xp = x_prompt.reshape(bp * seq, d)
    proj_p, small_p = _norm_matmul(xp, nw, w_main, w_small, _row_tile(bp * seq, PROJ_ROWS), SSD_CONV)
    kv, _ = _norm_matmul(mem_prompt.reshape(bp * MEM_TOKENS, d), mem_norm_w[0][None, :], w_mem_kv[0].T.astype(BF16),
                         jnp.zeros((LANES, d), BF16), _row_tile(bp * MEM_TOKENS, PROJ_ROWS), 2 * MEM_W)
    scan_rows = SCAN_ROWS if seq % SCAN_ROWS == 0 else CHUNK
    assert bs % (bp * seq // scan_rows) == 0

    xs = x_sample.reshape(bs, d)
    proj_s, small_s = _norm_matmul(xs, nw, w_main, w_small, bs, SSD_CONV)
    dec_row_mats = (proj_s, small_s, state_ssd_conv[0].reshape(bs, -1), state_gdn_conv[0].reshape(bs, -1))
    dec_states = (state_ssd[0], state_gdn[0], cache_mem_k.reshape(bs, MEM_TOKENS * MEM_HEADS, MEM_D),
                  cache_mem_v.reshape(bs, MEM_TOKENS * MEM_HEADS, MEM_D))
    dec_consts = (ssd_cw, ssd_cb, ssd_dtb, ssd_alog, ssd_dexp, gdn_cw, gdn_b, gdn_alog, mixw, e_ssd, e_ssd_n, e_gdn_n)
    (y_ssd, tail_ssd, p_ssd, ys_ssd, ys_gdn, ys_mem, s_ssd_conv, s_gdn_conv, s_ssd, s_gdn) = _ssd_prompt_decode(
        proj_p, small_p, bp, ssd_cw, ssd_cb, ssd_dtb, ssd_alog, ssd_dexp, mixw1, e_ssd, scan_rows,
        dec_row_mats, dec_states, dec_consts)
    y_gdn, tail_gdn, p_gdn = _gdn_prompt(proj_p, small_p, bp, gdn_cw, gdn_b, gdn_alog, mixw2, scan_rows)
    y_mem = _mem_prompt(proj_p, kv, bp, mixw3, _row_tile(seq, MEM_Q_ROWS))
    y_prompt = _out_proj(y_ssd, y_gdn, y_mem, wo1, wo2, wo3, xp, fw, _row_tile(bp * seq, OUT_ROWS)).reshape(bp, seq, d)
    y_sample = _out_proj(ys_ssd.reshape(bs, SSD_W), ys_gdn.reshape(bs, GDN_W), ys_mem.reshape(bs, MEM_W),
                         wo1, wo2, wo3, xs, fw, bs).reshape(bs, 1, d)

    keep = CONV_K - 1
    mem_shape = (1, bp, MEM_TOKENS, MEM_HEADS, MEM_D)
    return (
        y_prompt, y_sample,
        tail_ssd[None, :, SUBLANES - keep:, :], p_ssd[None],
        tail_gdn[None, :, SUBLANES - keep:, :], p_gdn[None],
        kv[:, :MEM_W].reshape(mem_shape), kv[:, MEM_W:].reshape(mem_shape),
        s_ssd_conv.reshape(1, bs, keep, SSD_CONV), s_ssd[None],
        s_gdn_conv.reshape(1, bs, keep, GDN_CONV), s_gdn[None],
    )
```

```python
import functools

import numpy as np
import jax
import jax.numpy as jnp
from jax import lax
from jax.experimental import pallas as pl
from jax.experimental.pallas import tpu as pltpu

F32, BF16 = jnp.float32, jnp.bfloat16

D_MODEL = 2048
SSD_HEADS, SSD_P, SSD_GROUPS, SSD_N = 16, 64, 2, 128
SSD_W = SSD_HEADS * SSD_P
SSD_GW = SSD_W // SSD_GROUPS
SSD_CONV = SSD_W + 2 * SSD_GROUPS * SSD_N
GDN_HEADS, GDN_D = 8, 128
GDN_W = GDN_HEADS * GDN_D
GDN_CONV = 3 * GDN_W
MEM_TOKENS, MEM_HEADS, MEM_D = 256, 4, 128
MEM_W = MEM_HEADS * MEM_D
MIX_W = SSD_W + GDN_W + MEM_W
CONV_K = 4
CHUNK = 64
EPS = 1e-6

LANES = 128
SUBLANES = 8
VMEM_LIMIT = 56 * 1024 * 1024
PROJ_ROWS = 1024
OUT_ROWS = 512
MEM_Q_ROWS = 256
SCAN_ROWS = 256
CONV_PHASES = 4

COL_QKV = 0
COL_XBC = COL_QKV + GDN_CONV
COL_QMEM = COL_XBC + SSD_CONV
COL_Z = COL_QMEM + MEM_W
N_MAIN = COL_Z + MIX_W
SM_DT, SM_B, SM_A = 0, SSD_HEADS, SSD_HEADS + GDN_HEADS


def _dot(a, b):
    return jnp.dot(a, b, preferred_element_type=F32)


def _dot_nt(a, b):
    return lax.dot_general(a, b, (((1,), (1,)), ((), ())), preferred_element_type=F32)


def _dot_tn(a, b):
    return lax.dot_general(a, b, (((0,), (0,)), ((), ())), preferred_element_type=F32)


def _split(x, n):
    parts, r = [], x
    for i in range(n):
        p = r.astype(BF16)
        parts.append(p)
        if i + 1 < n:
            r = r - p.astype(F32)
    return parts


def _sel_left(sel, x, n=3):
    return functools.reduce(lambda a, b: a + b, [_dot(sel, p) for p in _split(x, n)])


def _sel_right(x, sel, n=3):
    return functools.reduce(lambda a, b: a + b, [_dot(p, sel) for p in _split(x, n)])


def _sel_right_nt(x, sel, n=3):
    return functools.reduce(lambda a, b: a + b, [_dot_nt(p, sel) for p in _split(x, n)])


def _transpose_sel(x, n=3):
    eye = _eye(LANES).astype(BF16)
    return functools.reduce(lambda a, b: a + b, [_dot_nt(eye, p) for p in _split(x, n)])


def _eye(n):
    return (lax.broadcasted_iota(jnp.int32, (n, n), 0) == lax.broadcasted_iota(jnp.int32, (n, n), 1)).astype(F32)


def _sigmoid(x):
    return 1.0 / (1.0 + jnp.exp(-x))


def _silu(x):
    return x * _sigmoid(x)


def _softplus(x):
    return jnp.maximum(x, 0.0) + jnp.log1p(jnp.exp(-jnp.abs(x)))


def _params(*sem):
    return pltpu.CompilerParams(dimension_semantics=sem, vmem_limit_bytes=VMEM_LIMIT)


def _row_tile(rows, preferred):
    return preferred if rows % preferred == 0 else rows


def _norm_matmul_kernel(x_ref, nw_ref, w_ref, ws_ref, o_ref, os_ref, h_ref):
    @pl.when(pl.program_id(1) == 0)
    def _():
        x = x_ref[...]
        ms = jnp.mean(x * x, axis=-1, keepdims=True)
        h = (x * lax.rsqrt(ms + EPS) * nw_ref[...]).astype(BF16)
        h_ref[...] = h
        os_ref[...] = _dot_nt(h, ws_ref[...])

    o_ref[...] = _dot_nt(h_ref[...], w_ref[...])


def _norm_matmul(x, nw, w_t, ws_t, tm, tn):
    m, k = x.shape
    n = w_t.shape[0]
    ns = ws_t.shape[0]
    return pl.pallas_call(
        _norm_matmul_kernel,
        grid=(m // tm, n // tn),
        in_specs=[
            pl.BlockSpec((tm, k), lambda i, j: (i, 0)),
            pl.BlockSpec((1, k), lambda i, j: (0, 0)),
            pl.BlockSpec((tn, k), lambda i, j: (j, 0)),
            pl.BlockSpec((ns, k), lambda i, j: (0, 0)),
        ],
        out_specs=[
            pl.BlockSpec((tm, tn), lambda i, j: (i, j)),
            pl.BlockSpec((tm, ns), lambda i, j: (i, 0)),
        ],
        out_shape=[jax.ShapeDtypeStruct((m, n), F32), jax.ShapeDtypeStruct((m, ns), F32)],
        scratch_shapes=[pltpu.VMEM((tm, k), BF16)],
        compiler_params=_params("parallel", "arbitrary"),
        name="norm_matmul",
    )(x, nw, w_t, ws_t)


def _out_proj_kernel(y1_ref, y2_ref, y3_ref, w1_ref, w2_ref, w3_ref, x_ref, fw_ref, o_ref):
    acc = (_dot(y1_ref[...].astype(BF16), w1_ref[...]) + _dot(y2_ref[...].astype(BF16), w2_ref[...])
           + _dot(y3_ref[...].astype(BF16), w3_ref[...]))
    r = x_ref[...] + acc
    ms = jnp.mean(r * r, axis=-1, keepdims=True)
    o_ref[...] = r * lax.rsqrt(ms + EPS) * fw_ref[...]


def _out_proj(y1, y2, y3, w1, w2, w3, x, fw, tm):
    m, d = x.shape
    row = lambda i: (i, 0)
    whole = lambda i: (0, 0)
    return pl.pallas_call(
        _out_proj_kernel,
        grid=(m // tm,),
        in_specs=[
            pl.BlockSpec((tm, y1.shape[1]), row), pl.BlockSpec((tm, y2.shape[1]), row), pl.BlockSpec((tm, y3.shape[1]), row),
            pl.BlockSpec(w1.shape, whole), pl.BlockSpec(w2.shape, whole), pl.BlockSpec(w3.shape, whole),
            pl.BlockSpec((tm, d), row), pl.BlockSpec((1, d), whole),
        ],
        out_specs=pl.BlockSpec((tm, d), row),
        out_shape=jax.ShapeDtypeStruct((m, d), F32),
        compiler_params=_params("parallel"),
        name="out_proj",
    )(y1, y2, y3, w1, w2, w3, x, fw)


def _causal_conv_tile(u_ref, ubuf_ref, cw_ref, cb_ref, out_ref, tail_ref):
    t, width = u_ref.shape
    n = t // CONV_PHASES
    for s in range(width // LANES):
        cs = slice(s * LANES, (s + 1) * LANES)
        ubuf_ref[s, SUBLANES:SUBLANES + t, :] = u_ref[:, cs]
        taps = {d: ubuf_ref[s, pl.ds(SUBLANES + d, n, stride=CONV_PHASES), :] for d in range(1 - CONV_K, CONV_PHASES)}
        w = [cw_ref[j:j + 1, cs] for j in range(CONV_K)]
        for r in range(CONV_PHASES):
            acc = w[CONV_K - 1] * taps[r]
            if cb_ref is not None:
                acc = acc + cb_ref[:, cs]
            for j in range(CONV_K - 1):
                acc = acc + w[j] * taps[r - (CONV_K - 1) + j]
            out_ref[s, pl.ds(r, n, stride=CONV_PHASES), :] = _silu(acc)
        tail = ubuf_ref[s, t:t + SUBLANES, :]
        ubuf_ref[s, 0:SUBLANES, :] = tail
        tail_ref[0, :, cs] = tail


def _head_norm_gate(y, msq, width, z, mixw):
    return y * lax.rsqrt(msq * (1.0 / width) + EPS) * mixw * _silu(z)


def _ssd_prompt_kernel(xbc_ref, sm_ref, z_ref, cw_ref, cb_ref, dtb_ref, alog_ref, dexp_ref, mixw_ref, e_ref,
                       y_ref, tail_ref, state_ref, ubuf_ref, conv_ref, h_ref):
    c = pl.program_id(1)
    t = xbc_ref.shape[0]
    subs = range(t // CHUNK)
    groups = range(SSD_GROUPS)
    blocks = range(SSD_GW // LANES)

    @pl.when(c == 0)
    def _():
        ubuf_ref[:, 0:SUBLANES, :] = jnp.zeros((SSD_CONV // LANES, SUBLANES, LANES), F32)
        h_ref[...] = jnp.zeros_like(h_ref)

    _causal_conv_tile(xbc_ref, ubuf_ref, cw_ref, cb_ref, conv_ref, tail_ref)
    xs = jnp.concatenate([conv_ref[s] for s in range(SSD_W // LANES)], axis=1)
    e = e_ref[...]
    rows = [slice(j * CHUNK, (j + 1) * CHUNK) for j in subs]
    gs = [slice(g * SSD_GW, (g + 1) * SSD_GW) for g in groups]

    dt = _softplus(sm_ref[...] + dtb_ref[...])
    a = dt * (-jnp.exp(alog_ref[...]))
    rt = lax.broadcasted_iota(jnp.int32, (t, t), 0)
    ct = lax.broadcasted_iota(jnp.int32, (t, t), 1)
    chunk_causal = (rt >= ct) & (rt // CHUNK == ct // CHUNK)
    cum = _sel_left(chunk_causal.astype(BF16), a)
    cum_t = _transpose_sel(cum)
    ecum = jnp.exp(cum)
    wend = jnp.concatenate([jnp.exp(cum[(j + 1) * CHUNK - 1:(j + 1) * CHUNK, :] - cum[rows[j]]) for j in subs], axis=0)
    dt_x = _sel_right(dt, e, 2)
    ecum_x = _sel_right(ecum, e, 2)
    wend_x = _sel_right(wend, e, 2)

    xdt = xs * dt_x
    xdt_b = xdt.astype(BF16)
    xw_b = (xdt * wend_x).astype(BF16)
    causal = lax.broadcasted_iota(jnp.int32, (CHUNK, CHUNK), 0) >= lax.broadcasted_iota(jnp.int32, (CHUNK, CHUNK), 1)
    left = lax.broadcasted_iota(jnp.int32, (CHUNK, LANES), 1) < SSD_P

    jg = [(j, g) for j in subs for g in groups]
    b_slab, c_slab = SSD_W // LANES, SSD_W // LANES + SSD_GROUPS
    bmat = {(j, g): conv_ref[b_slab + g, rows[j], :].astype(BF16) for j, g in jg}
    cmat = {(j, g): conv_ref[c_slab + g, rows[j], :].astype(BF16) for j, g in jg}
    cb = {p: _dot_nt(cmat[p], bmat[p]) for p in jg}
    inc = {(j, g): _dot_tn(bmat[j, g], xw_b[rows[j], gs[g]]) for j, g in jg}
    scores = {}
    for j, g in jg:
        for blk in blocks:
            for half in range(2):
                h = (g * len(blocks) + blk) * 2 + half
                diff = cum[rows[j], h:h + 1] - cum_t[h:h + 1, rows[j]]
                lmat = jnp.where(causal, jnp.exp(jnp.minimum(diff, 0.0)), 0.0)
                scores[j, g, blk, half] = (cb[j, g] * lmat).astype(BF16)
    intra = {}
    for j, g in jg:
        for blk in blocks:
            lanes = slice(g * SSD_GW + blk * LANES, g * SSD_GW + (blk + 1) * LANES)
            xb = xdt_b[rows[j], lanes]
            zero = jnp.zeros_like(xb)
            intra[j, g, blk] = (_dot(scores[j, g, blk, 0], jnp.where(left, xb, zero))
                                + _dot(scores[j, g, blk, 1], jnp.where(left, zero, xb)))

    state = {(0, g): h_ref[:, gs[g]] for g in groups}
    for j in subs:
        last = (j + 1) * CHUNK - 1
        for g in groups:
            state[j + 1, g] = state[j, g] * ecum_x[last:last + 1, gs[g]] + inc[j, g]
    for g in groups:
        h_ref[:, gs[g]] = state[len(subs), g]
    inter = {p: _dot(cmat[p], state[p].astype(BF16)) for p in jg}
    inter_x = jnp.concatenate([jnp.concatenate([inter[j, g] for g in groups], axis=1) for j in subs], axis=0) * ecum_x
    intra_x = jnp.concatenate([jnp.concatenate([intra[j, g, blk] for g in groups for blk in blocks], axis=1)
                               for j in subs], axis=0)
    y = intra_x + inter_x + dexp_ref[...] * xs
    msq = _sel_right(_sel_right_nt(y * y, e, 2), e, 2)
    y_ref[...] = _head_norm_gate(y, msq, SSD_P, z_ref[...], mixw_ref[...]).astype(BF16)

    @pl.when(c == pl.num_programs(1) - 1)
    def _():
        state_ref[0] = h_ref[...].T.reshape(SSD_HEADS, SSD_P, SSD_N)


def _ssd_prompt(proj, small, batch, cw, cb, dtb, alog, dexp, mixw, e, tile, rider):
    rows = proj.shape[0]
    nc = rows // batch // tile
    row = lambda b, c: (b * nc + c, 0)
    whole = lambda b, c: (0, 0)
    return _scan_with_rider(
        "ssd_prompt", (batch, nc), _ssd_prompt_kernel,
        arrays=(proj, small, proj, cw, cb, dtb, alog, dexp, mixw, e),
        in_specs=[
            pl.BlockSpec((tile, SSD_CONV), lambda b, c: (b * nc + c, COL_XBC // SSD_CONV)),
            pl.BlockSpec((tile, LANES), row),
            pl.BlockSpec((tile, SSD_W), lambda b, c: (b * nc + c, COL_Z // SSD_W)),
            pl.BlockSpec(cw.shape, whole), pl.BlockSpec(cb.shape, whole), pl.BlockSpec(dtb.shape, whole),
            pl.BlockSpec(alog.shape, whole), pl.BlockSpec(dexp.shape, whole), pl.BlockSpec(mixw.shape, whole),
            pl.BlockSpec(e.shape, whole),
        ],
        out_shape=[
            jax.ShapeDtypeStruct((rows, SSD_W), BF16),
            jax.ShapeDtypeStruct((batch, SUBLANES, SSD_CONV), F32),
            jax.ShapeDtypeStruct((batch, SSD_HEADS, SSD_P, SSD_N), F32),
        ],
        out_specs=[
            pl.BlockSpec((tile, SSD_W), row),
            pl.BlockSpec((1, SUBLANES, SSD_CONV), lambda b, c: (b, 0, 0)),
            pl.BlockSpec((1, SSD_HEADS, SSD_P, SSD_N), lambda b, c: (b, 0, 0, 0)),
        ],
        scratch_shapes=[pltpu.VMEM((SSD_CONV // LANES, tile + SUBLANES, LANES), F32),
                        pltpu.VMEM((SSD_CONV // LANES, tile, LANES), F32), pltpu.VMEM((SSD_N, SSD_W), F32)],
        rider=rider)


def _unit_lower_inverses(a_stricts, ri, ci):
    t = a_stricts[0].shape[0]
    eye = _eye(t)
    first = (ri == ci + 1) & (ci % 2 == 0)
    invs = [eye - jnp.where(first, a, 0.0) for a in a_stricts]
    s = 2
    while s < t:
        sel = (ri // (2 * s) == ci // (2 * s)) & ((ri // s) % 2 == 1) & ((ci // s) % 2 == 0)
        inv_bs = [inv.astype(BF16) for inv in invs]
        lefts = [_dot(inv_b, jnp.where(sel, a, 0.0).astype(BF16)).astype(BF16) for inv_b, a in zip(inv_bs, a_stricts)]
        invs = [inv - _dot(left, inv_b) for inv, left, inv_b in zip(invs, lefts, inv_bs)]
        s *= 2
    return invs


def _gdn_prompt_kernel(qkv_ref, sm_ref, z_ref, cw_ref, gb_ref, galog_ref, mixw_ref,
                       y_ref, tail_ref, state_ref, ubuf_ref, conv_ref, s_ref):
    c = pl.program_id(1)
    t = qkv_ref.shape[0]
    subs = range(t // CHUNK)
    heads = range(GDN_HEADS)

    @pl.when(c == 0)
    def _():
        ubuf_ref[:, 0:SUBLANES, :] = jnp.zeros((GDN_CONV // LANES, SUBLANES, LANES), F32)
        s_ref[...] = jnp.zeros_like(s_ref)

    _causal_conv_tile(qkv_ref, ubuf_ref, cw_ref, None, conv_ref, tail_ref)

    sm = sm_ref[...]
    beta = _sigmoid(sm)
    g = -jnp.exp(galog_ref[...]) * _softplus(sm + gb_ref[...])
    rt = lax.broadcasted_iota(jnp.int32, (t, t), 0)
    ct = lax.broadcasted_iota(jnp.int32, (t, t), 1)
    chunk_causal = (rt >= ct) & (rt // CHUNK == ct // CHUNK)
    gc = _sel_left(chunk_causal.astype(BF16), g)
    gc_t = _transpose_sel(gc)
    eg = jnp.exp(gc)
    ri = lax.broadcasted_iota(jnp.int32, (CHUNK, CHUNK), 0)
    ci = lax.broadcasted_iota(jnp.int32, (CHUNK, CHUNK), 1)
    causal = ri >= ci
    strict = ri > ci

    rows = [slice(j * CHUNK, (j + 1) * CHUNK) for j in subs]
    hs = [slice(h * GDN_D, (h + 1) * GDN_D) for h in heads]
    la = [SM_A + h for h in heads]
    pairs = [(j, h) for j in subs for h in heads]
    q, k, kb, vb, kbg, qg, decay = {}, {}, {}, {}, {}, {}, {}
    for h in heads:
        qf, kf, vf = conv_ref[h], conv_ref[GDN_HEADS + h], conv_ref[2 * GDN_HEADS + h]
        qf = qf * lax.rsqrt(jnp.sum(qf * qf, axis=-1, keepdims=True) + EPS) * (GDN_D ** -0.5)
        kf = kf * lax.rsqrt(jnp.sum(kf * kf, axis=-1, keepdims=True) + EPS)
        b_col = beta[:, SM_B + h:SM_B + h + 1]
        eg_col = eg[:, la[h]:la[h] + 1]
        kbf = kf * b_col
        vbf, kbgf, qgf = (vf * b_col).astype(BF16), (kbf * eg_col).astype(BF16), (qf * eg_col).astype(BF16)
        for j in subs:
            q[j, h], k[j, h], kb[j, h] = qf[rows[j]].astype(BF16), kf[rows[j]], kbf[rows[j]].astype(BF16)
            vb[j, h], kbg[j, h], qg[j, h] = vbf[rows[j]], kbgf[rows[j]], qgf[rows[j]]
            diff = gc[rows[j], la[h]:la[h] + 1] - gc_t[la[h]:la[h] + 1, rows[j]]
            decay[j, h] = jnp.where(causal, jnp.exp(jnp.minimum(diff, 0.0)), 0.0)
    k_b = {p: k[p].astype(BF16) for p in pairs}
    kk = {p: _dot_nt(kb[p], k_b[p]) for p in pairs}
    qk = {p: _dot_nt(q[p], k_b[p]) for p in pairs}
    a_strict = [jnp.where(strict, kk[p] * decay[p], 0.0) for p in pairs]
    attn = {p: (qk[p] * decay[p]).astype(BF16) for p in pairs}
    t_inv = dict(zip(pairs, [x.astype(BF16) for x in _unit_lower_inverses(a_strict, ri, ci)]))
    u = {p: _dot(t_inv[p], vb[p]) for p in pairs}
    wk = {p: _dot(t_inv[p], kbg[p]).astype(BF16) for p in pairs}

    state = [s_ref[h] for h in heads]
    for j in subs:
        g_last = gc[(j + 1) * CHUNK - 1:(j + 1) * CHUNK, :]
        eend = jnp.exp(g_last - gc[rows[j]])
        elast = jnp.exp(g_last)
        s_b = [x.astype(BF16) for x in state]
        v_new = [(u[j, h] - _dot(wk[j, h], s_b[h])).astype(BF16) for h in heads]
        k_end = [(k[j, h] * eend[:, la[h]:la[h] + 1]).astype(BF16) for h in heads]
        s_inc = [_dot_tn(k_end[h], v_new[h]) for h in heads]
        state = [state[h] * elast[:, la[h]:la[h] + 1] + s_inc[h] for h in heads]
        o = [_dot(qg[j, h], s_b[h]) + _dot(attn[j, h], v_new[h]) for h in heads]
        msq = [jnp.sum(x * x, axis=-1, keepdims=True) for x in o]
        for h in heads:
            y_ref[rows[j], hs[h]] = _head_norm_gate(o[h], msq[h], GDN_D, z_ref[rows[j], hs[h]], mixw_ref[:, hs[h]]).astype(BF16)
    for h in heads:
        s_ref[h] = state[h]

    @pl.when(c == pl.num_programs(1) - 1)
    def _():
        state_ref[0] = s_ref[...]


def _gdn_prompt(proj, small, batch, cw, gb, galog, mixw, tile, rider):
    rows = proj.shape[0]
    nc = rows // batch // tile
    row = lambda b, c: (b * nc + c, 0)
    whole = lambda b, c: (0, 0)
    return _scan_with_rider(
        "gdn_prompt", (batch, nc), _gdn_prompt_kernel,
        arrays=(proj, small, proj, cw, gb, galog, mixw),
        in_specs=[
            pl.BlockSpec((tile, GDN_CONV), lambda b, c: (b * nc + c, COL_QKV // GDN_CONV)),
            pl.BlockSpec((tile, LANES), row),
            pl.BlockSpec((tile, GDN_W), lambda b, c: (b * nc + c, (COL_Z + SSD_W) // GDN_W)),
            pl.BlockSpec(cw.shape, whole), pl.BlockSpec(gb.shape, whole), pl.BlockSpec(galog.shape, whole),
            pl.BlockSpec(mixw.shape, whole),
        ],
        out_shape=[
            jax.ShapeDtypeStruct((rows, GDN_W), BF16),
            jax.ShapeDtypeStruct((batch, SUBLANES, GDN_CONV), F32),
            jax.ShapeDtypeStruct((batch, GDN_HEADS, GDN_D, GDN_D), F32),
        ],
        out_specs=[
            pl.BlockSpec((tile, GDN_W), row),
            pl.BlockSpec((1, SUBLANES, GDN_CONV), lambda b, c: (b, 0, 0)),
            pl.BlockSpec((1, GDN_HEADS, GDN_D, GDN_D), lambda b, c: (b, 0, 0, 0)),
        ],
        scratch_shapes=[pltpu.VMEM((GDN_CONV // LANES, tile + SUBLANES, LANES), F32),
                        pltpu.VMEM((GDN_CONV // LANES, tile, LANES), F32), pltpu.VMEM((GDN_HEADS, GDN_D, GDN_D), F32)],
        rider=rider)


def _mem_prompt_kernel(q_ref, k_ref, v_ref, z_ref, mixw_ref, y_ref):
    heads = range(MEM_HEADS)
    hs = [slice(h * MEM_D, (h + 1) * MEM_D) for h in heads]
    s = [_dot_nt(q_ref[:, hs[h]].astype(BF16), k_ref[:, hs[h]].astype(BF16)) * (MEM_D ** -0.5) for h in heads]
    e = [jnp.exp(x - jnp.max(x, axis=-1, keepdims=True)) for x in s]
    p = [(x / jnp.sum(x, axis=-1, keepdims=True)).astype(BF16) for x in e]
    o = [_dot(p[h], v_ref[:, hs[h]].astype(BF16)) for h in heads]
    msq = [jnp.sum(x * x, axis=-1, keepdims=True) for x in o]
    for h in heads:
        y_ref[:, hs[h]] = _head_norm_gate(o[h], msq[h], MEM_D, z_ref[:, hs[h]], mixw_ref[:, hs[h]]).astype(BF16)


def _mem_prompt(proj, kv, batch, mixw, tq):
    rows = proj.shape[0]
    nq = rows // batch // tq
    return pl.pallas_call(
        _mem_prompt_kernel,
        grid=(batch, nq),
        in_specs=[
            pl.BlockSpec((tq, MEM_W), lambda b, i: (b * nq + i, COL_QMEM // MEM_W)),
            pl.BlockSpec((MEM_TOKENS, MEM_W), lambda b, i: (b, 0)),
            pl.BlockSpec((MEM_TOKENS, MEM_W), lambda b, i: (b, 1)),
            pl.BlockSpec((tq, MEM_W), lambda b, i: (b * nq + i, (COL_Z + SSD_W + GDN_W) // MEM_W)),
            pl.BlockSpec(mixw.shape, lambda b, i: (0, 0)),
        ],
        out_specs=pl.BlockSpec((tq, MEM_W), lambda b, i: (b * nq + i, 0)),
        out_shape=jax.ShapeDtypeStruct((rows, MEM_W), BF16),
        compiler_params=_params("parallel", "parallel"),
        name="mem_prompt",
    )(proj, kv, kv, proj, mixw)


def _conv_step(u, st, cw_ref, bias, width):
    acc = cw_ref[CONV_K - 1:CONV_K, :] * u
    if bias is not None:
        acc = acc + bias
    for j in range(CONV_K - 1):
        acc = acc + cw_ref[j:j + 1, :] * st[:, j * width:(j + 1) * width]
    return _silu(acc), jnp.concatenate([st[:, width:], u], axis=1)


def _rows_to_columns(x):
    pad = jnp.zeros((LANES - x.shape[0], x.shape[1]), F32)
    return jnp.concatenate([x, pad], axis=0).T


def _pick_rows(parts):
    rid = lax.broadcasted_iota(jnp.int32, parts[0].shape, 0)
    out = parts[0]
    for i in range(1, len(parts)):
        out = jnp.where(rid == i, parts[i], out)
    return out


def _ssd_decode_step(xbc_ref, sm_ref, z_ref, cst_ref, st_ref, cw_ref, cb_ref, dtb_ref, alog_ref, dexp_ref,
                     mixw_ref, e_ref, en_ref, y_ref, cst_out_ref, st_out_ref):
    xbc, cst_new = _conv_step(xbc_ref[...], cst_ref[...], cw_ref, cb_ref[...], SSD_CONV)
    cst_out_ref[...] = cst_new
    xs = xbc[:, :SSD_W]
    e = e_ref[...]
    dt = _softplus(sm_ref[...] + dtb_ref[...])
    dec = jnp.exp(dt * (-jnp.exp(alog_ref[...])))
    xd_t = _rows_to_columns(xs * _sel_right(dt, e))
    dec_n = _sel_right(dec, en_ref[...])

    y_groups = []
    for g in range(SSD_GROUPS):
        b_g = xbc[:, SSD_W + g * SSD_N:SSD_W + (g + 1) * SSD_N]
        c_g = xbc[:, SSD_W + (SSD_GROUPS + g) * SSD_N:SSD_W + (SSD_GROUPS + g + 1) * SSD_N].astype(BF16)
        per_row = []
        for i in range(xs.shape[0]):
            new = []
            for r in range(SSD_HEADS // SSD_GROUPS):
                h = g * (SSD_HEADS // SSD_GROUPS) + r
                col = xd_t[h * SSD_P:(h + 1) * SSD_P, i:i + 1]
                hn = st_ref[i, h] * dec_n[i:i + 1, h * SSD_N:(h + 1) * SSD_N] + col * b_g[i:i + 1, :]
                st_out_ref[i, h] = hn
                new.append(hn)
            hg = jnp.concatenate(new, axis=0).astype(BF16)
            per_row.append(_dot_nt(c_g, hg))
        y_groups.append(_pick_rows(per_row))
    y = jnp.concatenate(y_groups, axis=1) + dexp_ref[...] * xs
    msq = _sel_right(_sel_right_nt(y * y, e, 2), e, 2)
    y_ref[...] = _head_norm_gate(y, msq, SSD_P, z_ref[...], mixw_ref[...])


def _gdn_decode_step(qkv_ref, sm_ref, z_ref, cst_ref, st_ref, cw_ref, gb_ref, galog_ref, mixw_ref, en_ref,
                     y_ref, cst_out_ref, st_out_ref):
    qkv, cst_new = _conv_step(qkv_ref[...], cst_ref[...], cw_ref, None, GDN_CONV)
    cst_out_ref[...] = cst_new
    sm = sm_ref[...]
    beta = _sigmoid(sm)
    eg = jnp.exp(-jnp.exp(galog_ref[...]) * _softplus(sm + gb_ref[...]))
    eg_n = _sel_right(eg, en_ref[...])

    qs, ks = [], []
    for h in range(GDN_HEADS):
        q = qkv[:, h * GDN_D:(h + 1) * GDN_D]
        k = qkv[:, GDN_W + h * GDN_D:GDN_W + (h + 1) * GDN_D]
        qs.append(q * lax.rsqrt(jnp.sum(q * q, axis=-1, keepdims=True) + EPS) * (GDN_D ** -0.5))
        ks.append(k * lax.rsqrt(jnp.sum(k * k, axis=-1, keepdims=True) + EPS))
    k_t = _rows_to_columns(jnp.concatenate(ks, axis=1))

    for h in range(GDN_HEADS):
        hs = slice(h * GDN_D, (h + 1) * GDN_D)
        q, k = qs[h], ks[h]
        v = qkv[:, 2 * GDN_W + h * GDN_D:2 * GDN_W + (h + 1) * GDN_D]
        q_b, k_b = q.astype(BF16), k.astype(BF16)
        ks_rows, qs_rows = [], []
        for i in range(q.shape[0]):
            s_b = st_ref[i, h].astype(BF16)
            ks_rows.append(_dot(k_b, s_b))
            qs_rows.append(_dot(q_b, s_b))
        k_s, q_s = _pick_rows(ks_rows), _pick_rows(qs_rows)
        eg_h = eg_n[:, hs]
        v_new = beta[:, SM_B + h:SM_B + h + 1] * (v - eg_h * k_s)
        o = eg_h * q_s + jnp.sum(q * k, axis=-1, keepdims=True) * v_new
        for i in range(q.shape[0]):
            col = k_t[hs, i:i + 1]
            st_out_ref[i, h] = st_ref[i, h] * eg_h[i:i + 1, :] + col * v_new[i:i + 1, :]
        msq = jnp.sum(o * o, axis=-1, keepdims=True)
        y_ref[:, hs] = _head_norm_gate(o, msq, GDN_D, z_ref[:, hs], mixw_ref[:, hs])


def _mem_decode_step(q_ref, k_ref, v_ref, z_ref, mixw_ref, y_ref):
    heads, rows = range(MEM_HEADS), range(q_ref.shape[0])
    hs = [slice(h * MEM_D, (h + 1) * MEM_D) for h in heads]
    win = [pl.ds(h, MEM_TOKENS, stride=MEM_HEADS) for h in heads]
    q = [q_ref[:, hs[h]].astype(BF16) for h in heads]
    s = [_pick_rows([_dot_nt(q[h], k_ref[i, win[h], :].astype(BF16)) for i in rows]) * (MEM_D ** -0.5) for h in heads]
    e = [jnp.exp(x - jnp.max(x, axis=-1, keepdims=True)) for x in s]
    p = [(x / jnp.sum(x, axis=-1, keepdims=True)).astype(BF16) for x in e]
    o = [_pick_rows([_dot(p[h], v_ref[i, win[h], :].astype(BF16)) for i in rows]) for h in heads]
    msq = [jnp.sum(x * x, axis=-1, keepdims=True) for x in o]
    for h in heads:
        y_ref[:, hs[h]] = _head_norm_gate(o[h], msq[h], MEM_D, z_ref[:, hs[h]], mixw_ref[:, hs[h]])


DEC_ROWS = SUBLANES


def _ssd_mem_decode_kernel(proj_ref, sm_ref, cst_ref, st_ref, k_ref, v_ref, cw_ref, cb_ref, dtb_ref, alog_ref, dexp_ref,
                           mixw_ref, e_ref, en_ref, y_ssd_ref, y_mem_ref, cst_out_ref, st_out_ref):
    cols = lambda start, width: proj_ref.at[:, start:start + width]
    mixw = lambda start, width: mixw_ref.at[:, start:start + width]
    _ssd_decode_step(cols(COL_XBC, SSD_CONV), sm_ref, cols(COL_Z, SSD_W), cst_ref, st_ref, cw_ref, cb_ref, dtb_ref,
                     alog_ref, dexp_ref, mixw(0, SSD_W), e_ref, en_ref, y_ssd_ref, cst_out_ref, st_out_ref)
    _mem_decode_step(cols(COL_QMEM, MEM_W), k_ref, v_ref, cols(COL_Z + SSD_W + GDN_W, MEM_W), mixw(SSD_W + GDN_W, MEM_W),
                     y_mem_ref)


def _gdn_decode_kernel(proj_ref, sm_ref, cst_ref, st_ref, cw_ref, gb_ref, galog_ref, mixw_ref, en_ref,
                       y_ref, cst_out_ref, st_out_ref):
    _gdn_decode_step(proj_ref.at[:, COL_QKV:COL_QKV + GDN_CONV], sm_ref, proj_ref.at[:, COL_Z + SSD_W:COL_Z + SSD_W + GDN_W],
                     cst_ref, st_ref, cw_ref, gb_ref, galog_ref, mixw_ref.at[:, SSD_W:SSD_W + GDN_W], en_ref,
                     y_ref, cst_out_ref, st_out_ref)


def _decode_rider(kernel, per_row_in, consts, out_shape, steps, step_index):
    groups = per_row_in[0].shape[0] // DEC_ROWS
    assert steps % groups == 0
    per_group = steps // groups
    group_spec = lambda a: pl.BlockSpec((DEC_ROWS,) + a.shape[1:],
                                        lambda *g, nd=len(a.shape): (step_index(*g) // per_group,) + (0,) * (nd - 1))
    whole = lambda a: pl.BlockSpec(a.shape, lambda *g, nd=a.ndim: (0,) * nd)
    return dict(kernel=kernel, arrays=tuple(per_row_in) + tuple(consts), steps_per_group=per_group,
                in_specs=[group_spec(a) for a in per_row_in] + [whole(a) for a in consts],
                out_shape=list(out_shape), out_specs=[group_spec(a) for a in out_shape])


def _scan_with_rider_kernel(*refs, scan_kernel, n_scan_in, n_scan_out, rider_kernel, n_rider_in, n_rider_out, steps_per_group):
    scan_in, refs = refs[:n_scan_in], refs[n_scan_in:]
    rider_in, refs = refs[:n_rider_in], refs[n_rider_in:]
    scan_out, refs = refs[:n_scan_out], refs[n_scan_out:]
    rider_out, scratch = refs[:n_rider_out], refs[n_rider_out:]
    scan_kernel(*scan_in, *scan_out, *scratch)
    step = pl.program_id(0) * pl.num_programs(1) + pl.program_id(1)

    @pl.when(step % steps_per_group == 0)
    def _():
        rider_kernel(*rider_in, *rider_out)


def _scan_with_rider(name, grid, scan_kernel, arrays, in_specs, out_shape, out_specs, scratch_shapes, rider):
    body = functools.partial(
        _scan_with_rider_kernel, scan_kernel=scan_kernel, n_scan_in=len(arrays), n_scan_out=len(out_shape),
        rider_kernel=rider["kernel"], n_rider_in=len(rider["arrays"]), n_rider_out=len(rider["out_shape"]),
        steps_per_group=rider["steps_per_group"])
    outs = pl.pallas_call(
        body,
        grid=grid,
        in_specs=list(in_specs) + rider["in_specs"],
        out_specs=list(out_specs) + rider["out_specs"],
        out_shape=list(out_shape) + rider["out_shape"],
        scratch_shapes=scratch_shapes,
        compiler_params=_params("arbitrary", "arbitrary"),
        name=name,
    )(*arrays, *rider["arrays"])
    return outs[:len(out_shape)], outs[len(out_shape):]


IN_DT = SSD_CONV
IN_QKV = IN_DT + SSD_HEADS
IN_B = IN_QKV + GDN_CONV
IN_QMEM = IN_B + 2 * GDN_HEADS
IN_COLS = IN_QMEM + MEM_W + MIX_W
PREP_COLS = 256


def _prep_w_in_kernel(w_ref, main_ref, small_ref):
    main_ref[COL_QKV:COL_QKV + GDN_CONV, :] = w_ref[IN_QKV:IN_B, :].astype(BF16)
    main_ref[COL_XBC:COL_XBC + SSD_CONV, :] = w_ref[:SSD_CONV, :].astype(BF16)
    main_ref[COL_QMEM:N_MAIN, :] = w_ref[IN_QMEM:IN_COLS, :].astype(BF16)
    small_ref[SM_DT:SM_B, :] = w_ref[IN_DT:IN_QKV, :].astype(BF16)
    small_ref[SM_B:SM_A + GDN_HEADS, :] = w_ref[IN_B:IN_QMEM, :].astype(BF16)
    small_ref[SM_A + GDN_HEADS:, :] = jnp.zeros((LANES - SM_A - GDN_HEADS, w_ref.shape[1]), BF16)


def _prep_w_in(w_t):
    n, k = w_t.shape
    assert n == IN_COLS
    return pl.pallas_call(
        _prep_w_in_kernel,
        grid=(k // PREP_COLS,),
        in_specs=[pl.BlockSpec((IN_COLS, PREP_COLS), lambda i: (0, i))],
        out_specs=[pl.BlockSpec((N_MAIN, PREP_COLS), lambda i: (0, i)), pl.BlockSpec((LANES, PREP_COLS), lambda i: (0, i))],
        out_shape=[jax.ShapeDtypeStruct((N_MAIN, k), BF16), jax.ShapeDtypeStruct((LANES, k), BF16)],
        compiler_params=_params("parallel"),
        name="prep_w_in",
    )(w_t)


def _head_expander(heads, first_lane, width):
    m = np.zeros((LANES, heads * width), np.float32)
    for h in range(heads):
        m[first_lane + h, h * width:(h + 1) * width] = 1.0
    return jnp.asarray(m, BF16)


def _lane_row(vec, first_lane):
    return jnp.zeros((1, LANES), F32).at[0, first_lane:first_lane + vec.shape[0]].set(vec.astype(F32))


def kernel(x_prompt, x_sample, mem_prompt, state_ssd_conv, state_ssd, state_gdn_conv, state_gdn, cache_mem_k, cache_mem_v, norm_w, w_in, ssd_conv_w, ssd_conv_b, ssd_dt_bias, ssd_A_log, ssd_D, gdn_conv_w, gdn_dt_bias, gdn_A_log, mem_norm_w, w_mem_kv, mix_norm_w, w_out, final_norm_w):
    bp, seq, d = x_prompt.shape
    bs = x_sample.shape[0]
    assert (d, seq % CHUNK, norm_w.shape[0]) == (D_MODEL, 0, 1)

    w_main, w_small = _prep_w_in(w_in[0].T)
    wo = w_out[0].astype(BF16)
    wo1, wo2, wo3 = wo[:SSD_W], wo[SSD_W:SSD_W + GDN_W], wo[SSD_W + GDN_W:]
    nw = norm_w[0][None, :]
    mixw = mix_norm_w[0][None, :]
    mixw1, mixw2, mixw3 = mixw[:, :SSD_W], mixw[:, SSD_W:SSD_W + GDN_W], mixw[:, SSD_W + GDN_W:]
    fw = final_norm_w[None, :]
    ssd_dtb = _lane_row(ssd_dt_bias[0], SM_DT)
    ssd_alog = _lane_row(ssd_A_log[0], SM_DT)
    ssd_dexp = jnp.repeat(ssd_D[0].astype(F32), SSD_P)[None, :]
    gdn_b = _lane_row(gdn_dt_bias[0], SM_A)
    gdn_alog = _lane_row(gdn_A_log[0], SM_A)
    e_ssd = _head_expander(SSD_HEADS, SM_DT, SSD_P)
    e_ssd_n = _head_expander(SSD_HEADS, SM_DT, SSD_N)
    e_gdn_n = _head_expander(GDN_HEADS, SM_A, GDN_D)
    ssd_cw, ssd_cb, gdn_cw = ssd_conv_w[0], ssd_conv_b[0][None, :], gdn_conv_w[0]

    xp = x_prompt.reshape(bp * seq, d)
    proj_p, small_p = _norm_matmul(xp, nw, w_main, w_small, _row_tile(bp * seq, PROJ_ROWS), SSD_CONV)
    kv, _ = _norm_matmul(mem_prompt.reshape(bp * MEM_TOKENS, d), mem_norm_w[0][None, :], w_mem_kv[0].T.astype(BF16),
                         jnp.zeros((LANES, d), BF16), _row_tile(bp * MEM_TOKENS, PROJ_ROWS), 2 * MEM_W)
    scan_rows = SCAN_ROWS if seq % SCAN_ROWS == 0 else CHUNK
    scan_steps = bp * seq // scan_rows
    step_index = lambda b, c: b * (seq // scan_rows) + c

    xs = x_sample.reshape(bs, d)
    proj_s, small_s = _norm_matmul(xs, nw, w_main, w_small, bs, SSD_CONV)
    ssd_cst, gdn_cst = state_ssd_conv[0].reshape(bs, -1), state_gdn_conv[0].reshape(bs, -1)
    mem_k = cache_mem_k.reshape(bs, MEM_TOKENS * MEM_HEADS, MEM_D)
    mem_v = cache_mem_v.reshape(bs, MEM_TOKENS * MEM_HEADS, MEM_D)
    row_out = lambda width: jax.ShapeDtypeStruct((bs, width), F32)
    like = lambda a: jax.ShapeDtypeStruct(a.shape, F32)
    ssd_rider = _decode_rider(
        _ssd_mem_decode_kernel, (proj_s, small_s, ssd_cst, state_ssd[0], mem_k, mem_v),
        (ssd_cw, ssd_cb, ssd_dtb, ssd_alog, ssd_dexp, mixw, e_ssd, e_ssd_n),
        (row_out(SSD_W), row_out(MEM_W), like(ssd_cst), like(state_ssd[0])), scan_steps, step_index)
    gdn_rider = _decode_rider(
        _gdn_decode_kernel, (proj_s, small_s, gdn_cst, state_gdn[0]), (gdn_cw, gdn_b, gdn_alog, mixw, e_gdn_n),
        (row_out(GDN_W), like(gdn_cst), like(state_gdn[0])), scan_steps, step_index)

    (y_ssd, tail_ssd, p_ssd), (ys_ssd, ys_mem, s_ssd_conv, s_ssd) = _ssd_prompt(
        proj_p, small_p, bp, ssd_cw, ssd_cb, ssd_dtb, ssd_alog, ssd_dexp, mixw1, e_ssd, scan_rows, ssd_rider)
    (y_gdn, tail_gdn, p_gdn), (ys_gdn, s_gdn_conv, s_gdn) = _gdn_prompt(
        proj_p, small_p, bp, gdn_cw, gdn_b, gdn_alog, mixw2, scan_rows, gdn_rider)
    y_mem = _mem_prompt(proj_p, kv, bp, mixw3, _row_tile(seq, MEM_Q_ROWS))
    y_prompt = _out_proj(y_ssd, y_gdn, y_mem, wo1, wo2, wo3, xp, fw, _row_tile(bp * seq, OUT_ROWS)).reshape(bp, seq, d)
    y_sample = _out_proj(ys_ssd, ys_gdn, ys_mem, wo1, wo2, wo3, xs, fw, bs).reshape(bs, 1, d)

    keep = CONV_K - 1
    mem_shape = (1, bp, MEM_TOKENS, MEM_HEADS, MEM_D)
    return (
        y_prompt, y_sample,
        tail_ssd[None, :, SUBLANES - keep:, :], p_ssd[None],
        tail_gdn[None, :, SUBLANES - keep:, :], p_gdn[None],
        kv[:, :MEM_W].reshape(mem_shape), kv[:, MEM_W:].reshape(mem_shape),
        s_ssd_conv.reshape(1, bs, keep, SSD_CONV), s_ssd[None],
        s_gdn_conv.reshape(1, bs, keep, GDN_CONV), s_gdn[None],
    )
```

```python
import functools

import numpy as np
import jax
import jax.numpy as jnp
from jax import lax
from jax.experimental import pallas as pl
from jax.experimental.pallas import tpu as pltpu

F32, BF16 = jnp.float32, jnp.bfloat16

D_MODEL = 2048
SSD_HEADS, SSD_P, SSD_GROUPS, SSD_N = 16, 64, 2, 128
SSD_W = SSD_HEADS * SSD_P
SSD_GW = SSD_W // SSD_GROUPS
SSD_CONV = SSD_W + 2 * SSD_GROUPS * SSD_N
GDN_HEADS, GDN_D = 8, 128
GDN_W = GDN_HEADS * GDN_D
GDN_CONV = 3 * GDN_W
MEM_TOKENS, MEM_HEADS, MEM_D = 256, 4, 128
MEM_W = MEM_HEADS * MEM_D
MIX_W = SSD_W + GDN_W + MEM_W
CONV_K = 4
CHUNK = 64
EPS = 1e-6

LANES = 128
SUBLANES = 8
VMEM_LIMIT = 56 * 1024 * 1024
PROJ_ROWS = 1024
OUT_ROWS = 512
MEM_Q_ROWS = 256
SCAN_ROWS = 256
CONV_PHASES = 4

COL_QKV = 0
COL_XBC = COL_QKV + GDN_CONV
COL_QMEM = COL_XBC + SSD_CONV
COL_Z = COL_QMEM + MEM_W
N_MAIN = COL_Z + MIX_W
SM_DT, SM_B, SM_A = 0, SSD_HEADS, SSD_HEADS + GDN_HEADS


def _dot(a, b):
    return jnp.dot(a, b, preferred_element_type=F32)


def _dot_nt(a, b):
    return lax.dot_general(a, b, (((1,), (1,)), ((), ())), preferred_element_type=F32)


def _dot_tn(a, b):
    return lax.dot_general(a, b, (((0,), (0,)), ((), ())), preferred_element_type=F32)


def _split(x, n):
    parts, r = [], x
    for i in range(n):
        p = r.astype(BF16)
        parts.append(p)
        if i + 1 < n:
            r = r - p.astype(F32)
    return parts


def _sel_left(sel, x, n=3):
    return functools.reduce(lambda a, b: a + b, [_dot(sel, p) for p in _split(x, n)])


def _sel_right(x, sel, n=3):
    return functools.reduce(lambda a, b: a + b, [_dot(p, sel) for p in _split(x, n)])


def _sel_right_nt(x, sel, n=3):
    return functools.reduce(lambda a, b: a + b, [_dot_nt(p, sel) for p in _split(x, n)])


def _transpose_sel(x, n=3):
    eye = _eye(LANES).astype(BF16)
    return functools.reduce(lambda a, b: a + b, [_dot_nt(eye, p) for p in _split(x, n)])


def _eye(n):
    return (lax.broadcasted_iota(jnp.int32, (n, n), 0) == lax.broadcasted_iota(jnp.int32, (n, n), 1)).astype(F32)


def _sigmoid(x):
    return 1.0 / (1.0 + jnp.exp(-x))


def _silu(x):
    return x * _sigmoid(x)


def _softplus(x):
    return jnp.maximum(x, 0.0) + jnp.log1p(jnp.exp(-jnp.abs(x)))


def _params(*sem):
    return pltpu.CompilerParams(dimension_semantics=sem, vmem_limit_bytes=VMEM_LIMIT)


def _row_tile(rows, preferred):
    return preferred if rows % preferred == 0 else rows


def _norm_matmul_kernel(x_ref, nw_ref, w_ref, ws_ref, o_ref, os_ref, h_ref):
    @pl.when(pl.program_id(1) == 0)
    def _():
        x = x_ref[...]
        ms = jnp.mean(x * x, axis=-1, keepdims=True)
        h = (x * lax.rsqrt(ms + EPS) * nw_ref[...]).astype(BF16)
        h_ref[...] = h
        os_ref[...] = _dot_nt(h, ws_ref[...])

    o_ref[...] = _dot_nt(h_ref[...], w_ref[...])


def _norm_matmul(x, nw, w_t, ws_t, tm, tn):
    m, k = x.shape
    n = w_t.shape[0]
    ns = ws_t.shape[0]
    return pl.pallas_call(
        _norm_matmul_kernel,
        grid=(m // tm, n // tn),
        in_specs=[
            pl.BlockSpec((tm, k), lambda i, j: (i, 0)),
            pl.BlockSpec((1, k), lambda i, j: (0, 0)),
            pl.BlockSpec((tn, k), lambda i, j: (j, 0)),
            pl.BlockSpec((ns, k), lambda i, j: (0, 0)),
        ],
        out_specs=[
            pl.BlockSpec((tm, tn), lambda i, j: (i, j)),
            pl.BlockSpec((tm, ns), lambda i, j: (i, 0)),
        ],
        out_shape=[jax.ShapeDtypeStruct((m, n), F32), jax.ShapeDtypeStruct((m, ns), F32)],
        scratch_shapes=[pltpu.VMEM((tm, k), BF16)],
        compiler_params=_params("parallel", "arbitrary"),
        name="norm_matmul",
    )(x, nw, w_t, ws_t)


def _out_proj_kernel(y1_ref, y2_ref, y3_ref, w1_ref, w2_ref, w3_ref, x_ref, fw_ref, o_ref):
    acc = (_dot(y1_ref[...].astype(BF16), w1_ref[...]) + _dot(y2_ref[...].astype(BF16), w2_ref[...])
           + _dot(y3_ref[...].astype(BF16), w3_ref[...]))
    r = x_ref[...] + acc
    ms = jnp.mean(r * r, axis=-1, keepdims=True)
    o_ref[...] = r * lax.rsqrt(ms + EPS) * fw_ref[...]


def _out_proj(y1, y2, y3, w1, w2, w3, x, fw, tm):
    m, d = x.shape
    row = lambda i: (i, 0)
    whole = lambda i: (0, 0)
    return pl.pallas_call(
        _out_proj_kernel,
        grid=(m // tm,),
        in_specs=[
            pl.BlockSpec((tm, y1.shape[1]), row), pl.BlockSpec((tm, y2.shape[1]), row), pl.BlockSpec((tm, y3.shape[1]), row),
            pl.BlockSpec(w1.shape, whole), pl.BlockSpec(w2.shape, whole), pl.BlockSpec(w3.shape, whole),
            pl.BlockSpec((tm, d), row), pl.BlockSpec((1, d), whole),
        ],
        out_specs=pl.BlockSpec((tm, d), row),
        out_shape=jax.ShapeDtypeStruct((m, d), F32),
        compiler_params=_params("parallel"),
        name="out_proj",
    )(y1, y2, y3, w1, w2, w3, x, fw)


def _causal_conv_tile(u_ref, ubuf_ref, cw_ref, cb_ref, out_ref, tail_ref):
    t, width = u_ref.shape
    n = t // CONV_PHASES
    for s in range(width // LANES):
        cs = slice(s * LANES, (s + 1) * LANES)
        ubuf_ref[s, SUBLANES:SUBLANES + t, :] = u_ref[:, cs]
        taps = {d: ubuf_ref[s, pl.ds(SUBLANES + d, n, stride=CONV_PHASES), :] for d in range(1 - CONV_K, CONV_PHASES)}
        w = [cw_ref[j:j + 1, cs] for j in range(CONV_K)]
        for r in range(CONV_PHASES):
            acc = w[CONV_K - 1] * taps[r]
            if cb_ref is not None:
                acc = acc + cb_ref[:, cs]
            for j in range(CONV_K - 1):
                acc = acc + w[j] * taps[r - (CONV_K - 1) + j]
            out_ref[s, pl.ds(r, n, stride=CONV_PHASES), :] = _silu(acc)
        tail = ubuf_ref[s, t:t + SUBLANES, :]
        ubuf_ref[s, 0:SUBLANES, :] = tail
        tail_ref[0, :, cs] = tail


def _head_norm_gate(y, msq, width, z, mixw):
    return y * lax.rsqrt(msq * (1.0 / width) + EPS) * mixw * _silu(z)


def _ssd_prompt_kernel(xbc_ref, sm_ref, z_ref, cw_ref, cb_ref, dtb_ref, alog_ref, dexp_ref, mixw_ref, e_ref,
                       y_ref, tail_ref, state_ref, ubuf_ref, conv_ref, h_ref):
    c = pl.program_id(1)
    t = xbc_ref.shape[0]
    subs = range(t // CHUNK)
    groups = range(SSD_GROUPS)
    blocks = range(SSD_GW // LANES)

    @pl.when(c == 0)
    def _():
        ubuf_ref[:, 0:SUBLANES, :] = jnp.zeros((SSD_CONV // LANES, SUBLANES, LANES), F32)
        h_ref[...] = jnp.zeros_like(h_ref)

    _causal_conv_tile(xbc_ref, ubuf_ref, cw_ref, cb_ref, conv_ref, tail_ref)
    xs = jnp.concatenate([conv_ref[s] for s in range(SSD_W // LANES)], axis=1)
    e = e_ref[...]
    rows = [slice(j * CHUNK, (j + 1) * CHUNK) for j in subs]
    gs = [slice(g * SSD_GW, (g + 1) * SSD_GW) for g in groups]

    dt = _softplus(sm_ref[...] + dtb_ref[...])
    a = dt * (-jnp.exp(alog_ref[...]))
    rt = lax.broadcasted_iota(jnp.int32, (t, t), 0)
    ct = lax.broadcasted_iota(jnp.int32, (t, t), 1)
    chunk_causal = (rt >= ct) & (rt // CHUNK == ct // CHUNK)
    cum = _sel_left(chunk_causal.astype(BF16), a)
    cum_t = _transpose_sel(cum)
    ecum = jnp.exp(cum)
    wend = jnp.concatenate([jnp.exp(cum[(j + 1) * CHUNK - 1:(j + 1) * CHUNK, :] - cum[rows[j]]) for j in subs], axis=0)
    dt_x = _sel_right(dt, e, 2)
    ecum_x = _sel_right(ecum, e, 2)
    wend_x = _sel_right(wend, e, 2)

    xdt = xs * dt_x
    xdt_b = xdt.astype(BF16)
    xw_b = (xdt * wend_x).astype(BF16)
    causal = lax.broadcasted_iota(jnp.int32, (CHUNK, CHUNK), 0) >= lax.broadcasted_iota(jnp.int32, (CHUNK, CHUNK), 1)
    left = lax.broadcasted_iota(jnp.int32, (CHUNK, LANES), 1) < SSD_P

    jg = [(j, g) for j in subs for g in groups]
    b_slab, c_slab = SSD_W // LANES, SSD_W // LANES + SSD_GROUPS
    bmat = {(j, g): conv_ref[b_slab + g, rows[j], :].astype(BF16) for j, g in jg}
    cmat = {(j, g): conv_ref[c_slab + g, rows[j], :].astype(BF16) for j, g in jg}
    cb = {p: _dot_nt(cmat[p], bmat[p]) for p in jg}
    inc = {(j, g): _dot_tn(bmat[j, g], xw_b[rows[j], gs[g]]) for j, g in jg}
    scores = {}
    for j, g in jg:
        for blk in blocks:
            for half in range(2):
                h = (g * len(blocks) + blk) * 2 + half
                diff = cum[rows[j], h:h + 1] - cum_t[h:h + 1, rows[j]]
                lmat = jnp.where(causal, jnp.exp(jnp.minimum(diff, 0.0)), 0.0)
                scores[j, g, blk, half] = (cb[j, g] * lmat).astype(BF16)
    intra = {}
    for j, g in jg:
        for blk in blocks:
            lanes = slice(g * SSD_GW + blk * LANES, g * SSD_GW + (blk + 1) * LANES)
            xb = xdt_b[rows[j], lanes]
            zero = jnp.zeros_like(xb)
            intra[j, g, blk] = (_dot(scores[j, g, blk, 0], jnp.where(left, xb, zero))
                                + _dot(scores[j, g, blk, 1], jnp.where(left, zero, xb)))

    state = {(0, g): h_ref[:, gs[g]] for g in groups}
    for j in subs:
        last = (j + 1) * CHUNK - 1
        for g in groups:
            state[j + 1, g] = state[j, g] * ecum_x[last:last + 1, gs[g]] + inc[j, g]
    for g in groups:
        h_ref[:, gs[g]] = state[len(subs), g]
    inter = {p: _dot(cmat[p], state[p].astype(BF16)) for p in jg}
    inter_x = jnp.concatenate([jnp.concatenate([inter[j, g] for g in groups], axis=1) for j in subs], axis=0) * ecum_x
    intra_x = jnp.concatenate([jnp.concatenate([intra[j, g, blk] for g in groups for blk in blocks], axis=1)
                               for j in subs], axis=0)
    y = intra_x + inter_x + dexp_ref[...] * xs
    msq = _sel_right(_sel_right_nt(y * y, e, 2), e, 2)
    y_ref[...] = _head_norm_gate(y, msq, SSD_P, z_ref[...], mixw_ref[...]).astype(BF16)

    @pl.when(c == pl.num_programs(1) - 1)
    def _():
        state_ref[0] = h_ref[...].T.reshape(SSD_HEADS, SSD_P, SSD_N)


def _ssd_prompt(proj, small, batch, cw, cb, dtb, alog, dexp, mixw, e, tile, riders):
    rows = proj.shape[0]
    nc = rows // batch // tile
    row = lambda b, c: (b * nc + c, 0)
    whole = lambda b, c: (0, 0)
    return _scan_with_riders(
        "ssd_prompt", (batch, nc), _ssd_prompt_kernel,
        arrays=(proj, small, proj, cw, cb, dtb, alog, dexp, mixw, e),
        in_specs=[
            pl.BlockSpec((tile, SSD_CONV), lambda b, c: (b * nc + c, COL_XBC // SSD_CONV)),
            pl.BlockSpec((tile, LANES), row),
            pl.BlockSpec((tile, SSD_W), lambda b, c: (b * nc + c, COL_Z // SSD_W)),
            pl.BlockSpec(cw.shape, whole), pl.BlockSpec(cb.shape, whole), pl.BlockSpec(dtb.shape, whole),
            pl.BlockSpec(alog.shape, whole), pl.BlockSpec(dexp.shape, whole), pl.BlockSpec(mixw.shape, whole),
            pl.BlockSpec(e.shape, whole),
        ],
        out_shape=[
            jax.ShapeDtypeStruct((rows, SSD_W), BF16),
            jax.ShapeDtypeStruct((batch, SUBLANES, SSD_CONV), F32),
            jax.ShapeDtypeStruct((batch, SSD_HEADS, SSD_P, SSD_N), F32),
        ],
        out_specs=[
            pl.BlockSpec((tile, SSD_W), row),
            pl.BlockSpec((1, SUBLANES, SSD_CONV), lambda b, c: (b, 0, 0)),
            pl.BlockSpec((1, SSD_HEADS, SSD_P, SSD_N), lambda b, c: (b, 0, 0, 0)),
        ],
        scratch_shapes=[pltpu.VMEM((SSD_CONV // LANES, tile + SUBLANES, LANES), F32),
                        pltpu.VMEM((SSD_CONV // LANES, tile, LANES), F32), pltpu.VMEM((SSD_N, SSD_W), F32)],
        riders=riders)


def _unit_lower_inverses(a_stricts, ri, ci):
    t = a_stricts[0].shape[0]
    eye = _eye(t)
    first = (ri == ci + 1) & (ci % 2 == 0)
    invs = [eye - jnp.where(first, a, 0.0) for a in a_stricts]
    s = 2
    while s < t:
        sel = (ri // (2 * s) == ci // (2 * s)) & ((ri // s) % 2 == 1) & ((ci // s) % 2 == 0)
        inv_bs = [inv.astype(BF16) for inv in invs]
        lefts = [_dot(inv_b, jnp.where(sel, a, 0.0).astype(BF16)).astype(BF16) for inv_b, a in zip(inv_bs, a_stricts)]
        invs = [inv - _dot(left, inv_b) for inv, left, inv_b in zip(invs, lefts, inv_bs)]
        s *= 2
    return invs


def _gdn_prompt_kernel(qkv_ref, sm_ref, z_ref, cw_ref, gb_ref, galog_ref, mixw_ref,
                       y_ref, tail_ref, state_ref, ubuf_ref, conv_ref, s_ref):
    c = pl.program_id(1)
    t = qkv_ref.shape[0]
    subs = range(t // CHUNK)
    heads = range(GDN_HEADS)

    @pl.when(c == 0)
    def _():
        ubuf_ref[:, 0:SUBLANES, :] = jnp.zeros((GDN_CONV // LANES, SUBLANES, LANES), F32)
        s_ref[...] = jnp.zeros_like(s_ref)

    _causal_conv_tile(qkv_ref, ubuf_ref, cw_ref, None, conv_ref, tail_ref)

    sm = sm_ref[...]
    beta = _sigmoid(sm)
    g = -jnp.exp(galog_ref[...]) * _softplus(sm + gb_ref[...])
    rt = lax.broadcasted_iota(jnp.int32, (t, t), 0)
    ct = lax.broadcasted_iota(jnp.int32, (t, t), 1)
    chunk_causal = (rt >= ct) & (rt // CHUNK == ct // CHUNK)
    gc = _sel_left(chunk_causal.astype(BF16), g)
    gc_t = _transpose_sel(gc)
    eg = jnp.exp(gc)
    ri = lax.broadcasted_iota(jnp.int32, (CHUNK, CHUNK), 0)
    ci = lax.broadcasted_iota(jnp.int32, (CHUNK, CHUNK), 1)
    causal = ri >= ci
    strict = ri > ci

    rows = [slice(j * CHUNK, (j + 1) * CHUNK) for j in subs]
    hs = [slice(h * GDN_D, (h + 1) * GDN_D) for h in heads]
    la = [SM_A + h for h in heads]
    pairs = [(j, h) for j in subs for h in heads]
    q, k, kb, vb, kbg, qg, decay = {}, {}, {}, {}, {}, {}, {}
    for h in heads:
        qf, kf, vf = conv_ref[h], conv_ref[GDN_HEADS + h], conv_ref[2 * GDN_HEADS + h]
        qf = qf * lax.rsqrt(jnp.sum(qf * qf, axis=-1, keepdims=True) + EPS) * (GDN_D ** -0.5)
        kf = kf * lax.rsqrt(jnp.sum(kf * kf, axis=-1, keepdims=True) + EPS)
        b_col = beta[:, SM_B + h:SM_B + h + 1]
        eg_col = eg[:, la[h]:la[h] + 1]
        kbf = kf * b_col
        vbf, kbgf, qgf = (vf * b_col).astype(BF16), (kbf * eg_col).astype(BF16), (qf * eg_col).astype(BF16)
        for j in subs:
            q[j, h], k[j, h], kb[j, h] = qf[rows[j]].astype(BF16), kf[rows[j]], kbf[rows[j]].astype(BF16)
            vb[j, h], kbg[j, h], qg[j, h] = vbf[rows[j]], kbgf[rows[j]], qgf[rows[j]]
            diff = gc[rows[j], la[h]:la[h] + 1] - gc_t[la[h]:la[h] + 1, rows[j]]
            decay[j, h] = jnp.where(causal, jnp.exp(jnp.minimum(diff, 0.0)), 0.0)
    k_b = {p: k[p].astype(BF16) for p in pairs}
    kk = {p: _dot_nt(kb[p], k_b[p]) for p in pairs}
    qk = {p: _dot_nt(q[p], k_b[p]) for p in pairs}
    a_strict = [jnp.where(strict, kk[p] * decay[p], 0.0) for p in pairs]
    attn = {p: (qk[p] * decay[p]).astype(BF16) for p in pairs}
    t_inv = dict(zip(pairs, [x.astype(BF16) for x in _unit_lower_inverses(a_strict, ri, ci)]))
    u = {p: _dot(t_inv[p], vb[p]) for p in pairs}
    wk = {p: _dot(t_inv[p], kbg[p]).astype(BF16) for p in pairs}

    state = [s_ref[h] for h in heads]
    for j in subs:
        g_last = gc[(j + 1) * CHUNK - 1:(j + 1) * CHUNK, :]
        eend = jnp.exp(g_last - gc[rows[j]])
        elast = jnp.exp(g_last)
        s_b = [x.astype(BF16) for x in state]
        v_new = [(u[j, h] - _dot(wk[j, h], s_b[h])).astype(BF16) for h in heads]
        k_end = [(k[j, h] * eend[:, la[h]:la[h] + 1]).astype(BF16) for h in heads]
        s_inc = [_dot_tn(k_end[h], v_new[h]) for h in heads]
        state = [state[h] * elast[:, la[h]:la[h] + 1] + s_inc[h] for h in heads]
        o = [_dot(qg[j, h], s_b[h]) + _dot(attn[j, h], v_new[h]) for h in heads]
        msq = [jnp.sum(x * x, axis=-1, keepdims=True) for x in o]
        for h in heads:
            y_ref[rows[j], hs[h]] = _head_norm_gate(o[h], msq[h], GDN_D, z_ref[rows[j], hs[h]], mixw_ref[:, hs[h]]).astype(BF16)
    for h in heads:
        s_ref[h] = state[h]

    @pl.when(c == pl.num_programs(1) - 1)
    def _():
        state_ref[0] = s_ref[...]


def _gdn_prompt(proj, small, batch, cw, gb, galog, mixw, tile, riders):
    rows = proj.shape[0]
    nc = rows // batch // tile
    row = lambda b, c: (b * nc + c, 0)
    whole = lambda b, c: (0, 0)
    return _scan_with_riders(
        "gdn_prompt", (batch, nc), _gdn_prompt_kernel,
        arrays=(proj, small, proj, cw, gb, galog, mixw),
        in_specs=[
            pl.BlockSpec((tile, GDN_CONV), lambda b, c: (b * nc + c, COL_QKV // GDN_CONV)),
            pl.BlockSpec((tile, LANES), row),
            pl.BlockSpec((tile, GDN_W), lambda b, c: (b * nc + c, (COL_Z + SSD_W) // GDN_W)),
            pl.BlockSpec(cw.shape, whole), pl.BlockSpec(gb.shape, whole), pl.BlockSpec(galog.shape, whole),
            pl.BlockSpec(mixw.shape, whole),
        ],
        out_shape=[
            jax.ShapeDtypeStruct((rows, GDN_W), BF16),
            jax.ShapeDtypeStruct((batch, SUBLANES, GDN_CONV), F32),
            jax.ShapeDtypeStruct((batch, GDN_HEADS, GDN_D, GDN_D), F32),
        ],
        out_specs=[
            pl.BlockSpec((tile, GDN_W), row),
            pl.BlockSpec((1, SUBLANES, GDN_CONV), lambda b, c: (b, 0, 0)),
            pl.BlockSpec((1, GDN_HEADS, GDN_D, GDN_D), lambda b, c: (b, 0, 0, 0)),
        ],
        scratch_shapes=[pltpu.VMEM((GDN_CONV // LANES, tile + SUBLANES, LANES), F32),
                        pltpu.VMEM((GDN_CONV // LANES, tile, LANES), F32), pltpu.VMEM((GDN_HEADS, GDN_D, GDN_D), F32)],
        riders=riders)


def _mem_prompt_kernel(q_ref, k_ref, v_ref, z_ref, mixw_ref, y_ref):
    heads = range(MEM_HEADS)
    hs = [slice(h * MEM_D, (h + 1) * MEM_D) for h in heads]
    s = [_dot_nt(q_ref[:, hs[h]].astype(BF16), k_ref[:, hs[h]].astype(BF16)) * (MEM_D ** -0.5) for h in heads]
    e = [jnp.exp(x - jnp.max(x, axis=-1, keepdims=True)) for x in s]
    p = [(x / jnp.sum(x, axis=-1, keepdims=True)).astype(BF16) for x in e]
    o = [_dot(p[h], v_ref[:, hs[h]].astype(BF16)) for h in heads]
    msq = [jnp.sum(x * x, axis=-1, keepdims=True) for x in o]
    for h in heads:
        y_ref[:, hs[h]] = _head_norm_gate(o[h], msq[h], MEM_D, z_ref[:, hs[h]], mixw_ref[:, hs[h]]).astype(BF16)


def _mem_prompt(proj, kv, batch, mixw, tq):
    rows = proj.shape[0]
    nq = rows // batch // tq
    return pl.pallas_call(
        _mem_prompt_kernel,
        grid=(batch, nq),
        in_specs=[
            pl.BlockSpec((tq, MEM_W), lambda b, i: (b * nq + i, COL_QMEM // MEM_W)),
            pl.BlockSpec((MEM_TOKENS, MEM_W), lambda b, i: (b, 0)),
            pl.BlockSpec((MEM_TOKENS, MEM_W), lambda b, i: (b, 1)),
            pl.BlockSpec((tq, MEM_W), lambda b, i: (b * nq + i, (COL_Z + SSD_W + GDN_W) // MEM_W)),
            pl.BlockSpec(mixw.shape, lambda b, i: (0, 0)),
        ],
        out_specs=pl.BlockSpec((tq, MEM_W), lambda b, i: (b * nq + i, 0)),
        out_shape=jax.ShapeDtypeStruct((rows, MEM_W), BF16),
        compiler_params=_params("parallel", "parallel"),
        name="mem_prompt",
    )(proj, kv, kv, proj, mixw)


def _conv_step(u, st, cw_ref, bias, width):
    acc = cw_ref[CONV_K - 1:CONV_K, :] * u
    if bias is not None:
        acc = acc + bias
    for j in range(CONV_K - 1):
        acc = acc + cw_ref[j:j + 1, :] * st[:, j * width:(j + 1) * width]
    return _silu(acc), jnp.concatenate([st[:, width:], u], axis=1)


def _rows_to_columns(x):
    pad = jnp.zeros((LANES - x.shape[0], x.shape[1]), F32)
    return jnp.concatenate([x, pad], axis=0).T


def _pick_rows(parts):
    rid = lax.broadcasted_iota(jnp.int32, parts[0].shape, 0)
    out = parts[0]
    for i in range(1, len(parts)):
        out = jnp.where(rid == i, parts[i], out)
    return out


def _ssd_decode_step(xbc_ref, sm_ref, z_ref, cst_ref, st_ref, cw_ref, cb_ref, dtb_ref, alog_ref, dexp_ref,
                     mixw_ref, e_ref, en_ref, y_ref, cst_out_ref, st_out_ref):
    xbc, cst_new = _conv_step(xbc_ref[...], cst_ref[...], cw_ref, cb_ref[...], SSD_CONV)
    cst_out_ref[...] = cst_new
    xs = xbc[:, :SSD_W]
    e = e_ref[...]
    dt = _softplus(sm_ref[...] + dtb_ref[...])
    dec = jnp.exp(dt * (-jnp.exp(alog_ref[...])))
    xd_t = _rows_to_columns(xs * _sel_right(dt, e))
    dec_n = _sel_right(dec, en_ref[...])

    y_groups = []
    for g in range(SSD_GROUPS):
        b_g = xbc[:, SSD_W + g * SSD_N:SSD_W + (g + 1) * SSD_N]
        c_g = xbc[:, SSD_W + (SSD_GROUPS + g) * SSD_N:SSD_W + (SSD_GROUPS + g + 1) * SSD_N].astype(BF16)
        per_row = []
        for i in range(xs.shape[0]):
            new = []
            for r in range(SSD_HEADS // SSD_GROUPS):
                h = g * (SSD_HEADS // SSD_GROUPS) + r
                col = xd_t[h * SSD_P:(h + 1) * SSD_P, i:i + 1]
                hn = st_ref[i, h] * dec_n[i:i + 1, h * SSD_N:(h + 1) * SSD_N] + col * b_g[i:i + 1, :]
                st_out_ref[i, h] = hn
                new.append(hn)
            hg = jnp.concatenate(new, axis=0).astype(BF16)
            per_row.append(_dot_nt(c_g, hg))
        y_groups.append(_pick_rows(per_row))
    y = jnp.concatenate(y_groups, axis=1) + dexp_ref[...] * xs
    msq = _sel_right(_sel_right_nt(y * y, e, 2), e, 2)
    y_ref[...] = _head_norm_gate(y, msq, SSD_P, z_ref[...], mixw_ref[...])


def _gdn_decode_step(qkv_ref, sm_ref, z_ref, cst_ref, st_ref, cw_ref, gb_ref, galog_ref, mixw_ref, en_ref,
                     y_ref, cst_out_ref, st_out_ref):
    qkv, cst_new = _conv_step(qkv_ref[...], cst_ref[...], cw_ref, None, GDN_CONV)
    cst_out_ref[...] = cst_new
    sm = sm_ref[...]
    beta = _sigmoid(sm)
    eg = jnp.exp(-jnp.exp(galog_ref[...]) * _softplus(sm + gb_ref[...]))
    eg_n = _sel_right(eg, en_ref[...])

    qs, ks = [], []
    for h in range(GDN_HEADS):
        q = qkv[:, h * GDN_D:(h + 1) * GDN_D]
        k = qkv[:, GDN_W + h * GDN_D:GDN_W + (h + 1) * GDN_D]
        qs.append(q * lax.rsqrt(jnp.sum(q * q, axis=-1, keepdims=True) + EPS) * (GDN_D ** -0.5))
        ks.append(k * lax.rsqrt(jnp.sum(k * k, axis=-1, keepdims=True) + EPS))
    k_t = _rows_to_columns(jnp.concatenate(ks, axis=1))

    for h in range(GDN_HEADS):
        hs = slice(h * GDN_D, (h + 1) * GDN_D)
        q, k = qs[h], ks[h]
        v = qkv[:, 2 * GDN_W + h * GDN_D:2 * GDN_W + (h + 1) * GDN_D]
        q_b, k_b = q.astype(BF16), k.astype(BF16)
        ks_rows, qs_rows = [], []
        for i in range(q.shape[0]):
            s_b = st_ref[i, h].astype(BF16)
            ks_rows.append(_dot(k_b, s_b))
            qs_rows.append(_dot(q_b, s_b))
        k_s, q_s = _pick_rows(ks_rows), _pick_rows(qs_rows)
        eg_h = eg_n[:, hs]
        v_new = beta[:, SM_B + h:SM_B + h + 1] * (v - eg_h * k_s)
        o = eg_h * q_s + jnp.sum(q * k, axis=-1, keepdims=True) * v_new
        for i in range(q.shape[0]):
            col = k_t[hs, i:i + 1]
            st_out_ref[i, h] = st_ref[i, h] * eg_h[i:i + 1, :] + col * v_new[i:i + 1, :]
        msq = jnp.sum(o * o, axis=-1, keepdims=True)
        y_ref[:, hs] = _head_norm_gate(o, msq, GDN_D, z_ref[:, hs], mixw_ref[:, hs])


def _mem_decode_step(q_ref, k_ref, v_ref, z_ref, mixw_ref, y_ref):
    heads, rows = range(MEM_HEADS), range(q_ref.shape[0])
    hs = [slice(h * MEM_D, (h + 1) * MEM_D) for h in heads]
    win = [pl.ds(h, MEM_TOKENS, stride=MEM_HEADS) for h in heads]
    q = [q_ref[:, hs[h]].astype(BF16) for h in heads]
    s = [_pick_rows([_dot_nt(q[h], k_ref[i, win[h], :].astype(BF16)) for i in rows]) * (MEM_D ** -0.5) for h in heads]
    e = [jnp.exp(x - jnp.max(x, axis=-1, keepdims=True)) for x in s]
    p = [(x / jnp.sum(x, axis=-1, keepdims=True)).astype(BF16) for x in e]
    o = [_pick_rows([_dot(p[h], v_ref[i, win[h], :].astype(BF16)) for i in rows]) for h in heads]
    msq = [jnp.sum(x * x, axis=-1, keepdims=True) for x in o]
    for h in heads:
        y_ref[:, hs[h]] = _head_norm_gate(o[h], msq[h], MEM_D, z_ref[:, hs[h]], mixw_ref[:, hs[h]])


DEC_ROWS = SUBLANES


def _ssd_decode_kernel(proj_ref, sm_ref, cst_ref, st_ref, cw_ref, cb_ref, dtb_ref, alog_ref, dexp_ref, mixw_ref, e_ref, en_ref,
                       y_ref, cst_out_ref, st_out_ref):
    _ssd_decode_step(proj_ref.at[:, COL_XBC:COL_XBC + SSD_CONV], sm_ref, proj_ref.at[:, COL_Z:COL_Z + SSD_W], cst_ref, st_ref,
                     cw_ref, cb_ref, dtb_ref, alog_ref, dexp_ref, mixw_ref.at[:, :SSD_W], e_ref, en_ref,
                     y_ref, cst_out_ref, st_out_ref)


def _gdn_decode_kernel(proj_ref, sm_ref, cst_ref, st_ref, cw_ref, gb_ref, galog_ref, mixw_ref, en_ref,
                       y_ref, cst_out_ref, st_out_ref):
    _gdn_decode_step(proj_ref.at[:, COL_QKV:COL_QKV + GDN_CONV], sm_ref, proj_ref.at[:, COL_Z + SSD_W:COL_Z + SSD_W + GDN_W],
                     cst_ref, st_ref, cw_ref, gb_ref, galog_ref, mixw_ref.at[:, SSD_W:SSD_W + GDN_W], en_ref,
                     y_ref, cst_out_ref, st_out_ref)


def _mem_decode_kernel(proj_ref, k_ref, v_ref, mixw_ref, y_ref):
    _mem_decode_step(proj_ref.at[:, COL_QMEM:COL_QMEM + MEM_W], k_ref, v_ref, proj_ref.at[:, N_MAIN - MEM_W:N_MAIN],
                     mixw_ref.at[:, SSD_W + GDN_W:], y_ref)


def _decode_rider(kernel, per_row_in, consts, out_shape, steps, step_index, phase):
    groups = per_row_in[0].shape[0] // DEC_ROWS
    assert steps % groups == 0
    per_group = steps // groups
    phase = phase % per_group
    group_of = lambda *g: jnp.maximum(step_index(*g) - phase, 0) // per_group
    group_spec = lambda a: pl.BlockSpec((DEC_ROWS,) + a.shape[1:], lambda *g, nd=len(a.shape): (group_of(*g),) + (0,) * (nd - 1))
    whole = lambda a: pl.BlockSpec(a.shape, lambda *g, nd=a.ndim: (0,) * nd)
    return dict(kernel=kernel, arrays=tuple(per_row_in) + tuple(consts), steps_per_group=per_group, phase=phase,
                in_specs=[group_spec(a) for a in per_row_in] + [whole(a) for a in consts],
                out_shape=list(out_shape), out_specs=[group_spec(a) for a in out_shape])


def _scan_with_riders_kernel(*refs, scan_kernel, n_scan_in, n_scan_out, riders):
    scan_in, refs = refs[:n_scan_in], refs[n_scan_in:]
    rider_in = []
    for _, n_in, _, _, _ in riders:
        rider_in.append(refs[:n_in])
        refs = refs[n_in:]
    scan_out, refs = refs[:n_scan_out], refs[n_scan_out:]
    rider_out = []
    for _, _, n_out, _, _ in riders:
        rider_out.append(refs[:n_out])
        refs = refs[n_out:]
    scan_kernel(*scan_in, *scan_out, *refs)
    step = pl.program_id(0) * pl.num_programs(1) + pl.program_id(1)
    for (kernel, _, _, per_group, phase), ins, outs in zip(riders, rider_in, rider_out):
        pl.when(step % per_group == phase)(functools.partial(kernel, *ins, *outs))


def _scan_with_riders(name, grid, scan_kernel, arrays, in_specs, out_shape, out_specs, scratch_shapes, riders):
    body = functools.partial(
        _scan_with_riders_kernel, scan_kernel=scan_kernel, n_scan_in=len(arrays), n_scan_out=len(out_shape),
        riders=tuple((r["kernel"], len(r["arrays"]), len(r["out_shape"]), r["steps_per_group"], r["phase"]) for r in riders))
    outs = pl.pallas_call(
        body,
        grid=grid,
        in_specs=list(in_specs) + [s for r in riders for s in r["in_specs"]],
        out_specs=list(out_specs) + [s for r in riders for s in r["out_specs"]],
        out_shape=list(out_shape) + [s for r in riders for s in r["out_shape"]],
        scratch_shapes=scratch_shapes,
        compiler_params=_params("arbitrary", "arbitrary"),
        name=name,
    )(*arrays, *[a for r in riders for a in r["arrays"]])
    scan_res, outs = outs[:len(out_shape)], outs[len(out_shape):]
    rider_res = []
    for r in riders:
        rider_res.append(outs[:len(r["out_shape"])])
        outs = outs[len(r["out_shape"]):]
    return scan_res, rider_res


IN_DT = SSD_CONV
IN_QKV = IN_DT + SSD_HEADS
IN_B = IN_QKV + GDN_CONV
IN_QMEM = IN_B + 2 * GDN_HEADS
IN_COLS = IN_QMEM + MEM_W + MIX_W
PREP_COLS = 256


def _prep_w_in_kernel(w_ref, main_ref, small_ref):
    main_ref[COL_QKV:COL_QKV + GDN_CONV, :] = w_ref[IN_QKV:IN_B, :].astype(BF16)
    main_ref[COL_XBC:COL_XBC + SSD_CONV, :] = w_ref[:SSD_CONV, :].astype(BF16)
    main_ref[COL_QMEM:N_MAIN, :] = w_ref[IN_QMEM:IN_COLS, :].astype(BF16)
    small_ref[SM_DT:SM_B, :] = w_ref[IN_DT:IN_QKV, :].astype(BF16)
    small_ref[SM_B:SM_A + GDN_HEADS, :] = w_ref[IN_B:IN_QMEM, :].astype(BF16)
    small_ref[SM_A + GDN_HEADS:, :] = jnp.zeros((LANES - SM_A - GDN_HEADS, w_ref.shape[1]), BF16)


def _prep_w_in(w_t):
    n, k = w_t.shape
    assert n == IN_COLS
    return pl.pallas_call(
        _prep_w_in_kernel,
        grid=(k // PREP_COLS,),
        in_specs=[pl.BlockSpec((IN_COLS, PREP_COLS), lambda i: (0, i))],
        out_specs=[pl.BlockSpec((N_MAIN, PREP_COLS), lambda i: (0, i)), pl.BlockSpec((LANES, PREP_COLS), lambda i: (0, i))],
        out_shape=[jax.ShapeDtypeStruct((N_MAIN, k), BF16), jax.ShapeDtypeStruct((LANES, k), BF16)],
        compiler_params=_params("parallel"),
        name="prep_w_in",
    )(w_t)


def _head_expander(heads, first_lane, width):
    m = np.zeros((LANES, heads * width), np.float32)
    for h in range(heads):
        m[first_lane + h, h * width:(h + 1) * width] = 1.0
    return jnp.asarray(m, BF16)


def _lane_row(vec, first_lane):
    return jnp.zeros((1, LANES), F32).at[0, first_lane:first_lane + vec.shape[0]].set(vec.astype(F32))


def kernel(x_prompt, x_sample, mem_prompt, state_ssd_conv, state_ssd, state_gdn_conv, state_gdn, cache_mem_k, cache_mem_v, norm_w, w_in, ssd_conv_w, ssd_conv_b, ssd_dt_bias, ssd_A_log, ssd_D, gdn_conv_w, gdn_dt_bias, gdn_A_log, mem_norm_w, w_mem_kv, mix_norm_w, w_out, final_norm_w):
    bp, seq, d = x_prompt.shape
    bs = x_sample.shape[0]
    assert (d, seq % CHUNK, norm_w.shape[0]) == (D_MODEL, 0, 1)

    w_main, w_small = _prep_w_in(w_in[0].T)
    wo = w_out[0].astype(BF16)
    wo1, wo2, wo3 = wo[:SSD_W], wo[SSD_W:SSD_W + GDN_W], wo[SSD_W + GDN_W:]
    nw = norm_w[0][None, :]
    mixw = mix_norm_w[0][None, :]
    mixw1, mixw2, mixw3 = mixw[:, :SSD_W], mixw[:, SSD_W:SSD_W + GDN_W], mixw[:, SSD_W + GDN_W:]
    fw = final_norm_w[None, :]
    ssd_dtb = _lane_row(ssd_dt_bias[0], SM_DT)
    ssd_alog = _lane_row(ssd_A_log[0], SM_DT)
    ssd_dexp = jnp.repeat(ssd_D[0].astype(F32), SSD_P)[None, :]
    gdn_b = _lane_row(gdn_dt_bias[0], SM_A)
    gdn_alog = _lane_row(gdn_A_log[0], SM_A)
    e_ssd = _head_expander(SSD_HEADS, SM_DT, SSD_P)
    e_ssd_n = _head_expander(SSD_HEADS, SM_DT, SSD_N)
    e_gdn_n = _head_expander(GDN_HEADS, SM_A, GDN_D)
    ssd_cw, ssd_cb, gdn_cw = ssd_conv_w[0], ssd_conv_b[0][None, :], gdn_conv_w[0]

    xp = x_prompt.reshape(bp * seq, d)
    proj_p, small_p = _norm_matmul(xp, nw, w_main, w_small, _row_tile(bp * seq, PROJ_ROWS), SSD_CONV)
    kv, _ = _norm_matmul(mem_prompt.reshape(bp * MEM_TOKENS, d), mem_norm_w[0][None, :], w_mem_kv[0].T.astype(BF16),
                         jnp.zeros((LANES, d), BF16), _row_tile(bp * MEM_TOKENS, PROJ_ROWS), 2 * MEM_W)
    scan_rows = SCAN_ROWS if seq % SCAN_ROWS == 0 else CHUNK
    scan_steps = bp * seq // scan_rows
    step_index = lambda b, c: b * (seq // scan_rows) + c

    xs = x_sample.reshape(bs, d)
    proj_s, small_s = _norm_matmul(xs, nw, w_main, w_small, bs, SSD_CONV)
    ssd_cst, gdn_cst = state_ssd_conv[0].reshape(bs, -1), state_gdn_conv[0].reshape(bs, -1)
    mem_k = cache_mem_k.reshape(bs, MEM_TOKENS * MEM_HEADS, MEM_D)
    mem_v = cache_mem_v.reshape(bs, MEM_TOKENS * MEM_HEADS, MEM_D)
    row_out = lambda width: jax.ShapeDtypeStruct((bs, width), F32)
    like = lambda a: jax.ShapeDtypeStruct(a.shape, F32)
    ssd_rider = _decode_rider(
        _ssd_decode_kernel, (proj_s, small_s, ssd_cst, state_ssd[0]),
        (ssd_cw, ssd_cb, ssd_dtb, ssd_alog, ssd_dexp, mixw, e_ssd, e_ssd_n),
        (row_out(SSD_W), like(ssd_cst), like(state_ssd[0])), scan_steps, step_index, phase=0)
    mem_rider = _decode_rider(
        _mem_decode_kernel, (proj_s, mem_k, mem_v), (mixw,), (row_out(MEM_W),), scan_steps, step_index, phase=1)
    gdn_rider = _decode_rider(
        _gdn_decode_kernel, (proj_s, small_s, gdn_cst, state_gdn[0]), (gdn_cw, gdn_b, gdn_alog, mixw, e_gdn_n),
        (row_out(GDN_W), like(gdn_cst), like(state_gdn[0])), scan_steps, step_index, phase=0)

    (y_ssd, tail_ssd, p_ssd), ((ys_ssd, s_ssd_conv, s_ssd), (ys_mem,)) = _ssd_prompt(
        proj_p, small_p, bp, ssd_cw, ssd_cb, ssd_dtb, ssd_alog, ssd_dexp, mixw1, e_ssd, scan_rows, (ssd_rider, mem_rider))
    (y_gdn, tail_gdn, p_gdn), ((ys_gdn, s_gdn_conv, s_gdn),) = _gdn_prompt(
        proj_p, small_p, bp, gdn_cw, gdn_b, gdn_alog, mixw2, scan_rows, (gdn_rider,))
    y_mem = _mem_prompt(proj_p, kv, bp, mixw3, _row_tile(seq, MEM_Q_ROWS))
    y_prompt = _out_proj(y_ssd, y_gdn, y_mem, wo1, wo2, wo3, xp, fw, _row_tile(bp * seq, OUT_ROWS)).reshape(bp, seq, d)
    y_sample = _out_proj(ys_ssd, ys_gdn, ys_mem, wo1, wo2, wo3, xs, fw, bs).reshape(bs, 1, d)

    keep = CONV_K - 1
    mem_shape = (1, bp, MEM_TOKENS, MEM_HEADS, MEM_D)
    return (
        y_prompt, y_sample,
        tail_ssd[None, :, SUBLANES - keep:, :], p_ssd[None],
        tail_gdn[None, :, SUBLANES - keep:, :], p_gdn[None],
        kv[:, :MEM_W].reshape(mem_shape), kv[:, MEM_W:].reshape(mem_shape),
        s_ssd_conv.reshape(1, bs, keep, SSD_CONV), s_ssd[None],
        s_gdn_conv.reshape(1, bs, keep, GDN_CONV), s_gdn[None],
    )
```

```python
import functools

import numpy as np
import jax
import jax.numpy as jnp
from jax import lax
from jax.experimental import pallas as pl
from jax.experimental.pallas import tpu as pltpu

F32, BF16 = jnp.float32, jnp.bfloat16

D_MODEL = 2048
SSD_HEADS, SSD_P, SSD_GROUPS, SSD_N = 16, 64, 2, 128
SSD_W = SSD_HEADS * SSD_P
SSD_GW = SSD_W // SSD_GROUPS
SSD_CONV = SSD_W + 2 * SSD_GROUPS * SSD_N
GDN_HEADS, GDN_D = 8, 128
GDN_W = GDN_HEADS * GDN_D
GDN_CONV = 3 * GDN_W
MEM_TOKENS, MEM_HEADS, MEM_D = 256, 4, 128
MEM_W = MEM_HEADS * MEM_D
MIX_W = SSD_W + GDN_W + MEM_W
CONV_K = 4
CHUNK = 64
EPS = 1e-6

LANES = 128
SUBLANES = 8
VMEM_LIMIT = 56 * 1024 * 1024
PROJ_ROWS = 1024
OUT_ROWS = 512
MEM_Q_ROWS = 256
SCAN_ROWS = 256
CONV_PHASES = 4

COL_QKV = 0
COL_XBC = COL_QKV + GDN_CONV
COL_QMEM = COL_XBC + SSD_CONV
COL_Z = COL_QMEM + MEM_W
N_MAIN = COL_Z + MIX_W
SM_DT, SM_B, SM_A = 0, SSD_HEADS, SSD_HEADS + GDN_HEADS


def _dot(a, b):
    return jnp.dot(a, b, preferred_element_type=F32)


def _dot_nt(a, b):
    return lax.dot_general(a, b, (((1,), (1,)), ((), ())), preferred_element_type=F32)


def _dot_tn(a, b):
    return lax.dot_general(a, b, (((0,), (0,)), ((), ())), preferred_element_type=F32)


def _split(x, n):
    parts, r = [], x
    for i in range(n):
        p = r.astype(BF16)
        parts.append(p)
        if i + 1 < n:
            r = r - p.astype(F32)
    return parts


def _sel_left(sel, x, n=3):
    return functools.reduce(lambda a, b: a + b, [_dot(sel, p) for p in _split(x, n)])


def _sel_right(x, sel, n=3):
    return functools.reduce(lambda a, b: a + b, [_dot(p, sel) for p in _split(x, n)])


def _sel_right_nt(x, sel, n=3):
    return functools.reduce(lambda a, b: a + b, [_dot_nt(p, sel) for p in _split(x, n)])


def _transpose_sel(x, n=3):
    eye = _eye(LANES).astype(BF16)
    return functools.reduce(lambda a, b: a + b, [_dot_nt(eye, p) for p in _split(x, n)])


def _eye(n):
    return (lax.broadcasted_iota(jnp.int32, (n, n), 0) == lax.broadcasted_iota(jnp.int32, (n, n), 1)).astype(F32)


def _sigmoid(x):
    return 1.0 / (1.0 + jnp.exp(-x))


def _silu(x):
    return x * _sigmoid(x)


def _softplus(x):
    return jnp.maximum(x, 0.0) + jnp.log1p(jnp.exp(-jnp.abs(x)))


def _params(*sem):
    return pltpu.CompilerParams(dimension_semantics=sem, vmem_limit_bytes=VMEM_LIMIT)


def _row_tile(rows, preferred):
    return preferred if rows % preferred == 0 else rows


def _rmsnorm_bf16(x, nw):
    ms = jnp.mean(x * x, axis=-1, keepdims=True)
    return (x * lax.rsqrt(ms + EPS) * nw).astype(BF16)


def _norm_matmul_kernel(x_ref, nw_ref, w_ref, ws_ref, o_ref, os_ref, h_ref):
    @pl.when(pl.program_id(1) == 0)
    def _():
        h = _rmsnorm_bf16(x_ref[...], nw_ref[...])
        h_ref[...] = h
        os_ref[...] = _dot_nt(h, ws_ref[...])

    o_ref[...] = _dot_nt(h_ref[...], w_ref[...])


def _norm_matmul(x, nw, w_t, ws_t, tm, tn):
    m, k = x.shape
    n = w_t.shape[0]
    ns = ws_t.shape[0]
    return pl.pallas_call(
        _norm_matmul_kernel,
        grid=(m // tm, n // tn),
        in_specs=[
            pl.BlockSpec((tm, k), lambda i, j: (i, 0)),
            pl.BlockSpec((1, k), lambda i, j: (0, 0)),
            pl.BlockSpec((tn, k), lambda i, j: (j, 0)),
            pl.BlockSpec((ns, k), lambda i, j: (0, 0)),
        ],
        out_specs=[
            pl.BlockSpec((tm, tn), lambda i, j: (i, j)),
            pl.BlockSpec((tm, ns), lambda i, j: (i, 0)),
        ],
        out_shape=[jax.ShapeDtypeStruct((m, n), F32), jax.ShapeDtypeStruct((m, ns), F32)],
        scratch_shapes=[pltpu.VMEM((tm, k), BF16)],
        compiler_params=_params("parallel", "arbitrary"),
        name="norm_matmul",
    )(x, nw, w_t, ws_t)


def _mem_kv_kernel(x_ref, nw_ref, w_ref, o_ref):
    o_ref[...] = _dot(_rmsnorm_bf16(x_ref[...], nw_ref[...]), w_ref[...])


def _mem_kv(x, nw, w, tm):
    m, k = x.shape
    n = w.shape[1]
    return pl.pallas_call(
        _mem_kv_kernel,
        grid=(m // tm,),
        in_specs=[pl.BlockSpec((tm, k), lambda i: (i, 0)), pl.BlockSpec((1, k), lambda i: (0, 0)),
                  pl.BlockSpec((k, n), lambda i: (0, 0))],
        out_specs=pl.BlockSpec((tm, n), lambda i: (i, 0)),
        out_shape=jax.ShapeDtypeStruct((m, n), F32),
        compiler_params=_params("parallel"),
        name="mem_kv",
    )(x, nw, w)


def _out_proj_kernel(y1_ref, y2_ref, y3_ref, w_ref, x_ref, fw_ref, o_ref):
    n1, n2 = y1_ref.shape[1], y1_ref.shape[1] + y2_ref.shape[1]
    acc = (_dot(y1_ref[...].astype(BF16), w_ref[:n1, :]) + _dot(y2_ref[...].astype(BF16), w_ref[n1:n2, :])
           + _dot(y3_ref[...].astype(BF16), w_ref[n2:, :]))
    r = x_ref[...] + acc
    ms = jnp.mean(r * r, axis=-1, keepdims=True)
    o_ref[...] = r * lax.rsqrt(ms + EPS) * fw_ref[...]


def _out_proj(y1, y2, y3, w, x, fw, tm):
    m, d = x.shape
    row = lambda i: (i, 0)
    whole = lambda i: (0, 0)
    return pl.pallas_call(
        _out_proj_kernel,
        grid=(m // tm,),
        in_specs=[
            pl.BlockSpec((tm, y1.shape[1]), row), pl.BlockSpec((tm, y2.shape[1]), row), pl.BlockSpec((tm, y3.shape[1]), row),
            pl.BlockSpec(w.shape, whole), pl.BlockSpec((tm, d), row), pl.BlockSpec((1, d), whole),
        ],
        out_specs=pl.BlockSpec((tm, d), row),
        out_shape=jax.ShapeDtypeStruct((m, d), F32),
        compiler_params=_params("parallel"),
        name="out_proj",
    )(y1, y2, y3, w, x, fw)


def _causal_conv_tile(u_ref, ubuf_ref, cw_ref, cb_ref, out_ref, tail_ref):
    t, width = u_ref.shape
    n = t // CONV_PHASES
    for s in range(width // LANES):
        cs = slice(s * LANES, (s + 1) * LANES)
        ubuf_ref[s, SUBLANES:SUBLANES + t, :] = u_ref[:, cs]
        taps = {d: ubuf_ref[s, pl.ds(SUBLANES + d, n, stride=CONV_PHASES), :] for d in range(1 - CONV_K, CONV_PHASES)}
        w = [cw_ref[j:j + 1, cs] for j in range(CONV_K)]
        for r in range(CONV_PHASES):
            acc = w[CONV_K - 1] * taps[r]
            if cb_ref is not None:
                acc = acc + cb_ref[:, cs]
            for j in range(CONV_K - 1):
                acc = acc + w[j] * taps[r - (CONV_K - 1) + j]
            out_ref[s, pl.ds(r, n, stride=CONV_PHASES), :] = _silu(acc)
        tail = ubuf_ref[s, t:t + SUBLANES, :]
        ubuf_ref[s, 0:SUBLANES, :] = tail
        tail_ref[0, :, cs] = tail


def _head_norm_gate(y, msq, width, z, mixw):
    return y * lax.rsqrt(msq * (1.0 / width) + EPS) * mixw * _silu(z)


def _ssd_prompt_kernel(xbc_ref, sm_ref, z_ref, cw_ref, cb_ref, dtb_ref, alog_ref, dexp_ref, mixw_ref, e_ref,
                       y_ref, tail_ref, state_ref, ubuf_ref, conv_ref, h_ref):
    c = pl.program_id(1)
    t = xbc_ref.shape[0]
    subs = range(t // CHUNK)
    groups = range(SSD_GROUPS)
    blocks = range(SSD_GW // LANES)

    @pl.when(c == 0)
    def _():
        ubuf_ref[:, 0:SUBLANES, :] = jnp.zeros((SSD_CONV // LANES, SUBLANES, LANES), F32)
        h_ref[...] = jnp.zeros_like(h_ref)

    _causal_conv_tile(xbc_ref, ubuf_ref, cw_ref, cb_ref, conv_ref, tail_ref)
    xs = jnp.concatenate([conv_ref[s] for s in range(SSD_W // LANES)], axis=1)
    e = e_ref[...]
    rows = [slice(j * CHUNK, (j + 1) * CHUNK) for j in subs]
    gs = [slice(g * SSD_GW, (g + 1) * SSD_GW) for g in groups]

    dt = _softplus(sm_ref[...] + dtb_ref[...])
    a = dt * (-jnp.exp(alog_ref[...]))
    rt = lax.broadcasted_iota(jnp.int32, (t, t), 0)
    ct = lax.broadcasted_iota(jnp.int32, (t, t), 1)
    chunk_causal = (rt >= ct) & (rt // CHUNK == ct // CHUNK)
    cum = _sel_left(chunk_causal.astype(BF16), a)
    cum_t = _transpose_sel(cum)
    ecum = jnp.exp(cum)
    wend = jnp.concatenate([jnp.exp(cum[(j + 1) * CHUNK - 1:(j + 1) * CHUNK, :] - cum[rows[j]]) for j in subs], axis=0)
    dt_x = _sel_right(dt, e, 2)
    ecum_x = _sel_right(ecum, e, 2)
    wend_x = _sel_right(wend, e, 2)

    xdt = xs * dt_x
    xdt_b = xdt.astype(BF16)
    xw_b = (xdt * wend_x).astype(BF16)
    causal = lax.broadcasted_iota(jnp.int32, (CHUNK, CHUNK), 0) >= lax.broadcasted_iota(jnp.int32, (CHUNK, CHUNK), 1)
    left = lax.broadcasted_iota(jnp.int32, (CHUNK, LANES), 1) < SSD_P

    jg = [(j, g) for j in subs for g in groups]
    b_slab, c_slab = SSD_W // LANES, SSD_W // LANES + SSD_GROUPS
    bmat = {(j, g): conv_ref[b_slab + g, rows[j], :].astype(BF16) for j, g in jg}
    cmat = {(j, g): conv_ref[c_slab + g, rows[j], :].astype(BF16) for j, g in jg}
    cb = {p: _dot_nt(cmat[p], bmat[p]) for p in jg}
    inc = {(j, g): _dot_tn(bmat[j, g], xw_b[rows[j], gs[g]]) for j, g in jg}
    scores = {}
    for j, g in jg:
        for blk in blocks:
            for half in range(2):
                h = (g * len(blocks) + blk) * 2 + half
                diff = cum[rows[j], h:h + 1] - cum_t[h:h + 1, rows[j]]
                lmat = jnp.where(causal, jnp.exp(jnp.minimum(diff, 0.0)), 0.0)
                scores[j, g, blk, half] = (cb[j, g] * lmat).astype(BF16)
    intra = {}
    for j, g in jg:
        for blk in blocks:
            lanes = slice(g * SSD_GW + blk * LANES, g * SSD_GW + (blk + 1) * LANES)
            xb = xdt_b[rows[j], lanes]
            zero = jnp.zeros_like(xb)
            intra[j, g, blk] = (_dot(scores[j, g, blk, 0], jnp.where(left, xb, zero))
                                + _dot(scores[j, g, blk, 1], jnp.where(left, zero, xb)))

    state = {(0, g): h_ref[:, gs[g]] for g in groups}
    for j in subs:
        last = (j + 1) * CHUNK - 1
        for g in groups:
            state[j + 1, g] = state[j, g] * ecum_x[last:last + 1, gs[g]] + inc[j, g]
    for g in groups:
        h_ref[:, gs[g]] = state[len(subs), g]
    inter = {p: _dot(cmat[p], state[p].astype(BF16)) for p in jg}
    inter_x = jnp.concatenate([jnp.concatenate([inter[j, g] for g in groups], axis=1) for j in subs], axis=0) * ecum_x
    intra_x = jnp.concatenate([jnp.concatenate([intra[j, g, blk] for g in groups for blk in blocks], axis=1)
                               for j in subs], axis=0)
    y = intra_x + inter_x + dexp_ref[...] * xs
    msq = _sel_right(_sel_right_nt(y * y, e, 2), e, 2)
    y_ref[...] = _head_norm_gate(y, msq, SSD_P, z_ref[...], mixw_ref[...]).astype(BF16)

    @pl.when(c == pl.num_programs(1) - 1)
    def _():
        state_ref[0] = h_ref[...].T.reshape(SSD_HEADS, SSD_P, SSD_N)


def _ssd_prompt(proj, small, batch, cw, cb, dtb, alog, dexp, mixw, e, tile, riders):
    rows = proj.shape[0]
    nc = rows // batch // tile
    row = lambda b, c: (b * nc + c, 0)
    whole = lambda b, c: (0, 0)
    return _scan_with_riders(
        "ssd_prompt", (batch, nc), _ssd_prompt_kernel,
        arrays=(proj, small, proj, cw, cb, dtb, alog, dexp, mixw, e),
        in_specs=[
            pl.BlockSpec((tile, SSD_CONV), lambda b, c: (b * nc + c, COL_XBC // SSD_CONV)),
            pl.BlockSpec((tile, LANES), row),
            pl.BlockSpec((tile, SSD_W), lambda b, c: (b * nc + c, COL_Z // SSD_W)),
            pl.BlockSpec(cw.shape, whole), pl.BlockSpec(cb.shape, whole), pl.BlockSpec(dtb.shape, whole),
            pl.BlockSpec(alog.shape, whole), pl.BlockSpec(dexp.shape, whole), pl.BlockSpec(mixw.shape, whole),
            pl.BlockSpec(e.shape, whole),
        ],
        out_shape=[
            jax.ShapeDtypeStruct((rows, SSD_W), BF16),
            jax.ShapeDtypeStruct((batch, SUBLANES, SSD_CONV), F32),
            jax.ShapeDtypeStruct((batch, SSD_HEADS, SSD_P, SSD_N), F32),
        ],
        out_specs=[
            pl.BlockSpec((tile, SSD_W), row),
            pl.BlockSpec((1, SUBLANES, SSD_CONV), lambda b, c: (b, 0, 0)),
            pl.BlockSpec((1, SSD_HEADS, SSD_P, SSD_N), lambda b, c: (b, 0, 0, 0)),
        ],
        scratch_shapes=[pltpu.VMEM((SSD_CONV // LANES, tile + SUBLANES, LANES), F32),
                        pltpu.VMEM((SSD_CONV // LANES, tile, LANES), F32), pltpu.VMEM((SSD_N, SSD_W), F32)],
        riders=riders)


def _unit_lower_inverses(a_stricts, ri, ci):
    t = a_stricts[0].shape[0]
    eye = _eye(t)
    first = (ri == ci + 1) & (ci % 2 == 0)
    invs = [eye - jnp.where(first, a, 0.0) for a in a_stricts]
    a_bs = [a.astype(BF16) for a in a_stricts]
    zero = jnp.zeros((t, t), BF16)
    s = 2
    while s < t:
        sel = (ri // (2 * s) == ci // (2 * s)) & ((ri // s) % 2 == 1) & ((ci // s) % 2 == 0)
        inv_bs = [inv.astype(BF16) for inv in invs]
        lefts = [_dot(inv_b, jnp.where(sel, a_b, zero)).astype(BF16) for inv_b, a_b in zip(inv_bs, a_bs)]
        invs = [inv - _dot(left, inv_b) for inv, left, inv_b in zip(invs, lefts, inv_bs)]
        s *= 2
    return invs


def _gdn_prompt_kernel(qkv_ref, sm_ref, z_ref, cw_ref, gb_ref, galog_ref, mixw_ref,
                       y_ref, tail_ref, state_ref, ubuf_ref, conv_ref, s_ref):
    c = pl.program_id(1)
    t = qkv_ref.shape[0]
    subs = range(t // CHUNK)
    heads = range(GDN_HEADS)

    @pl.when(c == 0)
    def _():
        ubuf_ref[:, 0:SUBLANES, :] = jnp.zeros((GDN_CONV // LANES, SUBLANES, LANES), F32)
        s_ref[...] = jnp.zeros_like(s_ref)

    _causal_conv_tile(qkv_ref, ubuf_ref, cw_ref, None, conv_ref, tail_ref)

    sm = sm_ref[...]
    beta = _sigmoid(sm)
    g = -jnp.exp(galog_ref[...]) * _softplus(sm + gb_ref[...])
    rt = lax.broadcasted_iota(jnp.int32, (t, t), 0)
    ct = lax.broadcasted_iota(jnp.int32, (t, t), 1)
    chunk_causal = (rt >= ct) & (rt // CHUNK == ct // CHUNK)
    gc = _sel_left(chunk_causal.astype(BF16), g)
    gc_t = _transpose_sel(gc)
    eg = jnp.exp(gc)
    ri = lax.broadcasted_iota(jnp.int32, (CHUNK, CHUNK), 0)
    ci = lax.broadcasted_iota(jnp.int32, (CHUNK, CHUNK), 1)
    causal = ri >= ci
    strict = ri > ci

    rows = [slice(j * CHUNK, (j + 1) * CHUNK) for j in subs]
    hs = [slice(h * GDN_D, (h + 1) * GDN_D) for h in heads]
    la = [SM_A + h for h in heads]
    pairs = [(j, h) for j in subs for h in heads]
    q, k, kb, vb, kbg, qg, decay = {}, {}, {}, {}, {}, {}, {}
    for h in heads:
        qf, kf, vf = conv_ref[h], conv_ref[GDN_HEADS + h], conv_ref[2 * GDN_HEADS + h]
        qf = qf * lax.rsqrt(jnp.sum(qf * qf, axis=-1, keepdims=True) + EPS) * (GDN_D ** -0.5)
        kf = kf * lax.rsqrt(jnp.sum(kf * kf, axis=-1, keepdims=True) + EPS)
        b_col = beta[:, SM_B + h:SM_B + h + 1]
        eg_col = eg[:, la[h]:la[h] + 1]
        kbf = kf * b_col
        vbf, kbgf, qgf = (vf * b_col).astype(BF16), (kbf * eg_col).astype(BF16), (qf * eg_col).astype(BF16)
        for j in subs:
            q[j, h], k[j, h], kb[j, h] = qf[rows[j]].astype(BF16), kf[rows[j]], kbf[rows[j]].astype(BF16)
            vb[j, h], kbg[j, h], qg[j, h] = vbf[rows[j]], kbgf[rows[j]], qgf[rows[j]]
            diff = gc[rows[j], la[h]:la[h] + 1] - gc_t[la[h]:la[h] + 1, rows[j]]
            decay[j, h] = jnp.where(causal, jnp.exp(jnp.minimum(diff, 0.0)), 0.0)
    k_b = {p: k[p].astype(BF16) for p in pairs}
    kq = {p: _dot_nt(jnp.concatenate([kb[p], q[p]], axis=0), k_b[p]) for p in pairs}
    a_strict = [jnp.where(strict, kq[p][:CHUNK] * decay[p], 0.0) for p in pairs]
    attn = {p: (kq[p][CHUNK:] * decay[p]).astype(BF16) for p in pairs}
    t_inv = dict(zip(pairs, [x.astype(BF16) for x in _unit_lower_inverses(a_strict, ri, ci)]))
    uw = {p: _dot(t_inv[p], jnp.concatenate([vb[p], kbg[p]], axis=1)) for p in pairs}
    u = {p: uw[p][:, :GDN_D] for p in pairs}
    wk = {p: uw[p][:, GDN_D:].astype(BF16) for p in pairs}

    state = [s_ref[h] for h in heads]
    for j in subs:
        g_last = gc[(j + 1) * CHUNK - 1:(j + 1) * CHUNK, :]
        eend = jnp.exp(g_last - gc[rows[j]])
        elast = jnp.exp(g_last)
        s_b = [x.astype(BF16) for x in state]
        v_new = [(u[j, h] - _dot(wk[j, h], s_b[h])).astype(BF16) for h in heads]
        k_end = [(k[j, h] * eend[:, la[h]:la[h] + 1]).astype(BF16) for h in heads]
        s_inc = [_dot_tn(k_end[h], v_new[h]) for h in heads]
        state = [state[h] * elast[:, la[h]:la[h] + 1] + s_inc[h] for h in heads]
        o = [_dot(qg[j, h], s_b[h]) + _dot(attn[j, h], v_new[h]) for h in heads]
        msq = [jnp.sum(x * x, axis=-1, keepdims=True) for x in o]
        for h in heads:
            y_ref[rows[j], hs[h]] = _head_norm_gate(o[h], msq[h], GDN_D, z_ref[rows[j], hs[h]], mixw_ref[:, hs[h]]).astype(BF16)
    for h in heads:
        s_ref[h] = state[h]

    @pl.when(c == pl.num_programs(1) - 1)
    def _():
        state_ref[0] = s_ref[...]


def _gdn_prompt(proj, small, batch, cw, gb, galog, mixw, tile, riders):
    rows = proj.shape[0]
    nc = rows // batch // tile
    row = lambda b, c: (b * nc + c, 0)
    whole = lambda b, c: (0, 0)
    return _scan_with_riders(
        "gdn_prompt", (batch, nc), _gdn_prompt_kernel,
        arrays=(proj, small, proj, cw, gb, galog, mixw),
        in_specs=[
            pl.BlockSpec((tile, GDN_CONV), lambda b, c: (b * nc + c, COL_QKV // GDN_CONV)),
            pl.BlockSpec((tile, LANES), row),
            pl.BlockSpec((tile, GDN_W), lambda b, c: (b * nc + c, (COL_Z + SSD_W) // GDN_W)),
            pl.BlockSpec(cw.shape, whole), pl.BlockSpec(gb.shape, whole), pl.BlockSpec(galog.shape, whole),
            pl.BlockSpec(mixw.shape, whole),
        ],
        out_shape=[
            jax.ShapeDtypeStruct((rows, GDN_W), BF16),
            jax.ShapeDtypeStruct((batch, SUBLANES, GDN_CONV), F32),
            jax.ShapeDtypeStruct((batch, GDN_HEADS, GDN_D, GDN_D), F32),
        ],
        out_specs=[
            pl.BlockSpec((tile, GDN_W), row),
            pl.BlockSpec((1, SUBLANES, GDN_CONV), lambda b, c: (b, 0, 0)),
            pl.BlockSpec((1, GDN_HEADS, GDN_D, GDN_D), lambda b, c: (b, 0, 0, 0)),
        ],
        scratch_shapes=[pltpu.VMEM((GDN_CONV // LANES, tile + SUBLANES, LANES), F32),
                        pltpu.VMEM((GDN_CONV // LANES, tile, LANES), F32), pltpu.VMEM((GDN_HEADS, GDN_D, GDN_D), F32)],
        riders=riders)


def _mem_prompt_kernel(q_ref, k_ref, v_ref, z_ref, mixw_ref, y_ref):
    heads = range(MEM_HEADS)
    hs = [slice(h * MEM_D, (h + 1) * MEM_D) for h in heads]
    s = [_dot_nt(q_ref[:, hs[h]].astype(BF16), k_ref[:, hs[h]].astype(BF16)) * (MEM_D ** -0.5) for h in heads]
    e = [jnp.exp(x - jnp.max(x, axis=-1, keepdims=True)) for x in s]
    p = [(x / jnp.sum(x, axis=-1, keepdims=True)).astype(BF16) for x in e]
    o = [_dot(p[h], v_ref[:, hs[h]].astype(BF16)) for h in heads]
    msq = [jnp.sum(x * x, axis=-1, keepdims=True) for x in o]
    for h in heads:
        y_ref[:, hs[h]] = _head_norm_gate(o[h], msq[h], MEM_D, z_ref[:, hs[h]], mixw_ref[:, hs[h]]).astype(BF16)


def _mem_prompt(proj, kv, batch, mixw, tq):
    rows = proj.shape[0]
    nq = rows // batch // tq
    return pl.pallas_call(
        _mem_prompt_kernel,
        grid=(batch, nq),
        in_specs=[
            pl.BlockSpec((tq, MEM_W), lambda b, i: (b * nq + i, COL_QMEM // MEM_W)),
            pl.BlockSpec((MEM_TOKENS, MEM_W), lambda b, i: (b, 0)),
            pl.BlockSpec((MEM_TOKENS, MEM_W), lambda b, i: (b, 1)),
            pl.BlockSpec((tq, MEM_W), lambda b, i: (b * nq + i, (COL_Z + SSD_W + GDN_W) // MEM_W)),
            pl.BlockSpec(mixw.shape, lambda b, i: (0, 0)),
        ],
        out_specs=pl.BlockSpec((tq, MEM_W), lambda b, i: (b * nq + i, 0)),
        out_shape=jax.ShapeDtypeStruct((rows, MEM_W), BF16),
        compiler_params=_params("parallel", "parallel"),
        name="mem_prompt",
    )(proj, kv, kv, proj, mixw)


def _conv_step(u, cst_ref, cst_out_ref, cw_ref, bias):
    acc = cw_ref[CONV_K - 1:CONV_K, :] * u
    if bias is not None:
        acc = acc + bias
    for j in range(CONV_K - 1):
        prev = cst_ref[j]
        acc = acc + cw_ref[j:j + 1, :] * prev
        if j > 0:
            cst_out_ref[j - 1] = prev
    cst_out_ref[CONV_K - 2] = u
    return _silu(acc)


def _rows_to_columns(x):
    pad = jnp.zeros((LANES - x.shape[0], x.shape[1]), F32)
    return jnp.concatenate([x, pad], axis=0).T


def _pick_rows(parts):
    rid = lax.broadcasted_iota(jnp.int32, parts[0].shape, 0)
    out = parts[0]
    for i in range(1, len(parts)):
        out = jnp.where(rid == i, parts[i], out)
    return out


def _ssd_decode_step(xbc_ref, sm_ref, z_ref, cst_ref, st_ref, cw_ref, cb_ref, dtb_ref, alog_ref, dexp_ref,
                     mixw_ref, e_ref, en_ref, y_ref, cst_out_ref, st_out_ref):
    xbc = _conv_step(xbc_ref[...], cst_ref, cst_out_ref, cw_ref, cb_ref[...])
    xs = xbc[:, :SSD_W]
    e = e_ref[...]
    dt = _softplus(sm_ref[...] + dtb_ref[...])
    dec = jnp.exp(dt * (-jnp.exp(alog_ref[...])))
    xd_t = _rows_to_columns(xs * _sel_right(dt, e))
    dec_n = _sel_right(dec, en_ref[...])

    y_groups = []
    for g in range(SSD_GROUPS):
        b_g = xbc[:, SSD_W + g * SSD_N:SSD_W + (g + 1) * SSD_N]
        c_g = xbc[:, SSD_W + (SSD_GROUPS + g) * SSD_N:SSD_W + (SSD_GROUPS + g + 1) * SSD_N].astype(BF16)
        per_row = []
        for i in range(xs.shape[0]):
            new = []
            for r in range(SSD_HEADS // SSD_GROUPS):
                h = g * (SSD_HEADS // SSD_GROUPS) + r
                col = xd_t[h * SSD_P:(h + 1) * SSD_P, i:i + 1]
                hn = st_ref[i, h] * dec_n[i:i + 1, h * SSD_N:(h + 1) * SSD_N] + col * b_g[i:i + 1, :]
                st_out_ref[i, h] = hn
                new.append(hn)
            hg = jnp.concatenate(new, axis=0).astype(BF16)
            per_row.append(_dot_nt(c_g, hg))
        y_groups.append(_pick_rows(per_row))
    y = jnp.concatenate(y_groups, axis=1) + dexp_ref[...] * xs
    msq = _sel_right(_sel_right_nt(y * y, e, 2), e, 2)
    y_ref[...] = _head_norm_gate(y, msq, SSD_P, z_ref[...], mixw_ref[...])


def _gdn_decode_step(qkv_ref, sm_ref, z_ref, cst_ref, st_ref, cw_ref, gb_ref, galog_ref, mixw_ref, en_ref,
                     y_ref, cst_out_ref, st_out_ref):
    qkv = _conv_step(qkv_ref[...], cst_ref, cst_out_ref, cw_ref, None)
    sm = sm_ref[...]
    beta = _sigmoid(sm)
    eg = jnp.exp(-jnp.exp(galog_ref[...]) * _softplus(sm + gb_ref[...]))
    eg_n = _sel_right(eg, en_ref[...])

    qs, ks = [], []
    for h in range(GDN_HEADS):
        q = qkv[:, h * GDN_D:(h + 1) * GDN_D]
        k = qkv[:, GDN_W + h * GDN_D:GDN_W + (h + 1) * GDN_D]
        qs.append(q * lax.rsqrt(jnp.sum(q * q, axis=-1, keepdims=True) + EPS) * (GDN_D ** -0.5))
        ks.append(k * lax.rsqrt(jnp.sum(k * k, axis=-1, keepdims=True) + EPS))
    k_t = _rows_to_columns(jnp.concatenate(ks, axis=1))

    for h in range(GDN_HEADS):
        hs = slice(h * GDN_D, (h + 1) * GDN_D)
        q, k = qs[h], ks[h]
        v = qkv[:, 2 * GDN_W + h * GDN_D:2 * GDN_W + (h + 1) * GDN_D]
        q_b, k_b = q.astype(BF16), k.astype(BF16)
        ks_rows, qs_rows = [], []
        for i in range(q.shape[0]):
            s_b = st_ref[i, h].astype(BF16)
            ks_rows.append(_dot(k_b, s_b))
            qs_rows.append(_dot(q_b, s_b))
        k_s, q_s = _pick_rows(ks_rows), _pick_rows(qs_rows)
        eg_h = eg_n[:, hs]
        v_new = beta[:, SM_B + h:SM_B + h + 1] * (v - eg_h * k_s)
        o = eg_h * q_s + jnp.sum(q * k, axis=-1, keepdims=True) * v_new
        for i in range(q.shape[0]):
            col = k_t[hs, i:i + 1]
            st_out_ref[i, h] = st_ref[i, h] * eg_h[i:i + 1, :] + col * v_new[i:i + 1, :]
        msq = jnp.sum(o * o, axis=-1, keepdims=True)
        y_ref[:, hs] = _head_norm_gate(o, msq, GDN_D, z_ref[:, hs], mixw_ref[:, hs])


def _mem_decode_step(q_ref, k_ref, v_ref, z_ref, mixw_ref, y_ref):
    heads, rows = range(MEM_HEADS), range(q_ref.shape[0])
    hs = [slice(h * MEM_D, (h + 1) * MEM_D) for h in heads]
    win = [pl.ds(h, MEM_TOKENS, stride=MEM_HEADS) for h in heads]
    q = [q_ref[:, hs[h]].astype(BF16) for h in heads]
    s = [_pick_rows([_dot_nt(q[h], k_ref[i, win[h], :].astype(BF16)) for i in rows]) * (MEM_D ** -0.5) for h in heads]
    e = [jnp.exp(x - jnp.max(x, axis=-1, keepdims=True)) for x in s]
    p = [(x / jnp.sum(x, axis=-1, keepdims=True)).astype(BF16) for x in e]
    o = [_pick_rows([_dot(p[h], v_ref[i, win[h], :].astype(BF16)) for i in rows]) for h in heads]
    msq = [jnp.sum(x * x, axis=-1, keepdims=True) for x in o]
    for h in heads:
        y_ref[:, hs[h]] = _head_norm_gate(o[h], msq[h], MEM_D, z_ref[:, hs[h]], mixw_ref[:, hs[h]])


DEC_ROWS = SUBLANES


def _ssd_decode_kernel(proj_ref, sm_ref, cst_ref, st_ref, cw_ref, cb_ref, dtb_ref, alog_ref, dexp_ref, mixw_ref, e_ref, en_ref,
                       y_ref, cst_out_ref, st_out_ref):
    _ssd_decode_step(proj_ref.at[:, COL_XBC:COL_XBC + SSD_CONV], sm_ref, proj_ref.at[:, COL_Z:COL_Z + SSD_W], cst_ref, st_ref,
                     cw_ref, cb_ref, dtb_ref, alog_ref, dexp_ref, mixw_ref.at[:, :SSD_W], e_ref, en_ref,
                     y_ref, cst_out_ref, st_out_ref)


def _gdn_decode_kernel(proj_ref, sm_ref, cst_ref, st_ref, cw_ref, gb_ref, galog_ref, mixw_ref, en_ref,
                       y_ref, cst_out_ref, st_out_ref):
    _gdn_decode_step(proj_ref.at[:, COL_QKV:COL_QKV + GDN_CONV], sm_ref, proj_ref.at[:, COL_Z + SSD_W:COL_Z + SSD_W + GDN_W],
                     cst_ref, st_ref, cw_ref, gb_ref, galog_ref, mixw_ref.at[:, SSD_W:SSD_W + GDN_W], en_ref,
                     y_ref, cst_out_ref, st_out_ref)


def _mem_decode_kernel(proj_ref, k_ref, v_ref, mixw_ref, y_ref):
    _mem_decode_step(proj_ref.at[:, COL_QMEM:COL_QMEM + MEM_W], k_ref, v_ref, proj_ref.at[:, N_MAIN - MEM_W:N_MAIN],
                     mixw_ref.at[:, SSD_W + GDN_W:], y_ref)


def _decode_rider(kernel, per_row_in, consts, out_shape, steps, step_index, phase):
    groups = per_row_in[0].shape[0] // DEC_ROWS
    assert steps % groups == 0
    per_group = steps // groups
    phase = phase % per_group
    group_of = lambda *g: jnp.maximum(step_index(*g) - phase, 0) // per_group

    def group_spec(a):
        axis = 1 if len(a.shape) == 3 and a.shape[0] == CONV_K - 1 else 0
        block = a.shape[:axis] + (DEC_ROWS,) + a.shape[axis + 1:]
        return pl.BlockSpec(block, lambda *g: (0,) * axis + (group_of(*g),) + (0,) * (len(block) - axis - 1))

    whole = lambda a: pl.BlockSpec(a.shape, lambda *g, nd=a.ndim: (0,) * nd)
    return dict(kernel=kernel, arrays=tuple(per_row_in) + tuple(consts), steps_per_group=per_group, phase=phase,
                in_specs=[group_spec(a) for a in per_row_in] + [whole(a) for a in consts],
                out_shape=list(out_shape), out_specs=[group_spec(a) for a in out_shape])


def _scan_with_riders_kernel(*refs, scan_kernel, n_scan_in, n_scan_out, riders):
    scan_in, refs = refs[:n_scan_in], refs[n_scan_in:]
    rider_in = []
    for _, n_in, _, _, _ in riders:
        rider_in.append(refs[:n_in])
        refs = refs[n_in:]
    scan_out, refs = refs[:n_scan_out], refs[n_scan_out:]
    rider_out = []
    for _, _, n_out, _, _ in riders:
        rider_out.append(refs[:n_out])
        refs = refs[n_out:]
    scan_kernel(*scan_in, *scan_out, *refs)
    step = pl.program_id(0) * pl.num_programs(1) + pl.program_id(1)
    for (kernel, _, _, per_group, phase), ins, outs in zip(riders, rider_in, rider_out):
        pl.when(step % per_group == phase)(functools.partial(kernel, *ins, *outs))


def _scan_with_riders(name, grid, scan_kernel, arrays, in_specs, out_shape, out_specs, scratch_shapes, riders):
    body = functools.partial(
        _scan_with_riders_kernel, scan_kernel=scan_kernel, n_scan_in=len(arrays), n_scan_out=len(out_shape),
        riders=tuple((r["kernel"], len(r["arrays"]), len(r["out_shape"]), r["steps_per_group"], r["phase"]) for r in riders))
    outs = pl.pallas_call(
        body,
        grid=grid,
        in_specs=list(in_specs) + [s for r in riders for s in r["in_specs"]],
        out_specs=list(out_specs) + [s for r in riders for s in r["out_specs"]],
        out_shape=list(out_shape) + [s for r in riders for s in r["out_shape"]],
        scratch_shapes=scratch_shapes,
        compiler_params=_params("arbitrary", "arbitrary"),
        name=name,
    )(*arrays, *[a for r in riders for a in r["arrays"]])
    scan_res, outs = outs[:len(out_shape)], outs[len(out_shape):]
    rider_res = []
    for r in riders:
        rider_res.append(outs[:len(r["out_shape"])])
        outs = outs[len(r["out_shape"]):]
    return scan_res, rider_res


IN_DT = SSD_CONV
IN_QKV = IN_DT + SSD_HEADS
IN_B = IN_QKV + GDN_CONV
IN_QMEM = IN_B + 2 * GDN_HEADS
IN_COLS = IN_QMEM + MEM_W + MIX_W
PREP_COLS = 256


def _prep_w_in_kernel(w_ref, main_ref, small_ref):
    main_ref[COL_QKV:COL_QKV + GDN_CONV, :] = w_ref[IN_QKV:IN_B, :].astype(BF16)
    main_ref[COL_XBC:COL_XBC + SSD_CONV, :] = w_ref[:SSD_CONV, :].astype(BF16)
    main_ref[COL_QMEM:N_MAIN, :] = w_ref[IN_QMEM:IN_COLS, :].astype(BF16)
    small_ref[SM_DT:SM_B, :] = w_ref[IN_DT:IN_QKV, :].astype(BF16)
    small_ref[SM_B:SM_A + GDN_HEADS, :] = w_ref[IN_B:IN_QMEM, :].astype(BF16)
    small_ref[SM_A + GDN_HEADS:, :] = jnp.zeros((LANES - SM_A - GDN_HEADS, w_ref.shape[1]), BF16)


def _prep_w_in(w_t):
    n, k = w_t.shape
    assert n == IN_COLS
    return pl.pallas_call(
        _prep_w_in_kernel,
        grid=(k // PREP_COLS,),
        in_specs=[pl.BlockSpec((IN_COLS, PREP_COLS), lambda i: (0, i))],
        out_specs=[pl.BlockSpec((N_MAIN, PREP_COLS), lambda i: (0, i)), pl.BlockSpec((LANES, PREP_COLS), lambda i: (0, i))],
        out_shape=[jax.ShapeDtypeStruct((N_MAIN, k), BF16), jax.ShapeDtypeStruct((LANES, k), BF16)],
        compiler_params=_params("parallel"),
        name="prep_w_in",
    )(w_t)


def _head_expander(heads, first_lane, width):
    m = np.zeros((LANES, heads * width), np.float32)
    for h in range(heads):
        m[first_lane + h, h * width:(h + 1) * width] = 1.0
    return jnp.asarray(m, BF16)


def _lane_row(vec, first_lane):
    return jnp.zeros((1, LANES), F32).at[0, first_lane:first_lane + vec.shape[0]].set(vec.astype(F32))


def kernel(x_prompt, x_sample, mem_prompt, state_ssd_conv, state_ssd, state_gdn_conv, state_gdn, cache_mem_k, cache_mem_v, norm_w, w_in, ssd_conv_w, ssd_conv_b, ssd_dt_bias, ssd_A_log, ssd_D, gdn_conv_w, gdn_dt_bias, gdn_A_log, mem_norm_w, w_mem_kv, mix_norm_w, w_out, final_norm_w):
    bp, seq, d = x_prompt.shape
    bs = x_sample.shape[0]
    assert (d, seq % CHUNK, norm_w.shape[0]) == (D_MODEL, 0, 1)

    w_main, w_small = _prep_w_in(w_in[0].T)
    wo = w_out[0].astype(BF16)
    nw = norm_w[0][None, :]
    mixw = mix_norm_w[0][None, :]
    mixw1, mixw2, mixw3 = mixw[:, :SSD_W], mixw[:, SSD_W:SSD_W + GDN_W], mixw[:, SSD_W + GDN_W:]
    fw = final_norm_w[None, :]
    ssd_dtb = _lane_row(ssd_dt_bias[0], SM_DT)
    ssd_alog = _lane_row(ssd_A_log[0], SM_DT)
    ssd_dexp = jnp.repeat(ssd_D[0].astype(F32), SSD_P)[None, :]
    gdn_b = _lane_row(gdn_dt_bias[0], SM_A)
    gdn_alog = _lane_row(gdn_A_log[0], SM_A)
    e_ssd = _head_expander(SSD_HEADS, SM_DT, SSD_P)
    e_ssd_n = _head_expander(SSD_HEADS, SM_DT, SSD_N)
    e_gdn_n = _head_expander(GDN_HEADS, SM_A, GDN_D)
    ssd_cw, ssd_cb, gdn_cw = ssd_conv_w[0], ssd_conv_b[0][None, :], gdn_conv_w[0]

    xp = x_prompt.reshape(bp * seq, d)
    proj_p, small_p = _norm_matmul(xp, nw, w_main, w_small, _row_tile(bp * seq, PROJ_ROWS), SSD_CONV)
    kv = _mem_kv(mem_prompt.reshape(bp * MEM_TOKENS, d), mem_norm_w[0][None, :], w_mem_kv[0].astype(BF16), MEM_TOKENS)
    scan_rows = SCAN_ROWS if seq % SCAN_ROWS == 0 else CHUNK
    scan_steps = bp * seq // scan_rows
    step_index = lambda b, c: b * (seq // scan_rows) + c

    xs = x_sample.reshape(bs, d)
    proj_s, small_s = _norm_matmul(xs, nw, w_main, w_small, bs, SSD_CONV)
    ssd_cst, gdn_cst = jnp.swapaxes(state_ssd_conv[0], 0, 1), jnp.swapaxes(state_gdn_conv[0], 0, 1)
    mem_k = cache_mem_k.reshape(bs, MEM_TOKENS * MEM_HEADS, MEM_D)
    mem_v = cache_mem_v.reshape(bs, MEM_TOKENS * MEM_HEADS, MEM_D)
    row_out = lambda width: jax.ShapeDtypeStruct((bs, width), F32)
    like = lambda a: jax.ShapeDtypeStruct(a.shape, F32)
    ssd_rider = _decode_rider(
        _ssd_decode_kernel, (proj_s, small_s, ssd_cst, state_ssd[0]),
        (ssd_cw, ssd_cb, ssd_dtb, ssd_alog, ssd_dexp, mixw, e_ssd, e_ssd_n),
        (row_out(SSD_W), like(ssd_cst), like(state_ssd[0])), scan_steps, step_index, phase=0)
    mem_rider = _decode_rider(
        _mem_decode_kernel, (proj_s, mem_k, mem_v), (mixw,), (row_out(MEM_W),), scan_steps, step_index, phase=1)
    gdn_rider = _decode_rider(
        _gdn_decode_kernel, (proj_s, small_s, gdn_cst, state_gdn[0]), (gdn_cw, gdn_b, gdn_alog, mixw, e_gdn_n),
        (row_out(GDN_W), like(gdn_cst), like(state_gdn[0])), scan_steps, step_index, phase=0)

    (y_ssd, tail_ssd, p_ssd), ((ys_ssd, s_ssd_conv, s_ssd), (ys_mem,)) = _ssd_prompt(
        proj_p, small_p, bp, ssd_cw, ssd_cb, ssd_dtb, ssd_alog, ssd_dexp, mixw1, e_ssd, scan_rows, (ssd_rider, mem_rider))
    (y_gdn, tail_gdn, p_gdn), ((ys_gdn, s_gdn_conv, s_gdn),) = _gdn_prompt(
        proj_p, small_p, bp, gdn_cw, gdn_b, gdn_alog, mixw2, scan_rows, (gdn_rider,))
    y_mem = _mem_prompt(proj_p, kv, bp, mixw3, _row_tile(seq, MEM_Q_ROWS))
    y_prompt = _out_proj(y_ssd, y_gdn, y_mem, wo, xp, fw, _row_tile(bp * seq, OUT_ROWS)).reshape(bp, seq, d)
    y_sample = _out_proj(ys_ssd, ys_gdn, ys_mem, wo, xs, fw, bs).reshape(bs, 1, d)

    keep = CONV_K - 1
    mem_shape = (1, bp, MEM_TOKENS, MEM_HEADS, MEM_D)
    return (
        y_prompt, y_sample,
        tail_ssd[None, :, SUBLANES - keep:, :], p_ssd[None],
        tail_gdn[None, :, SUBLANES - keep:, :], p_gdn[None],
        kv[:, :MEM_W].reshape(mem_shape), kv[:, MEM_W:].reshape(mem_shape),
        jnp.swapaxes(s_ssd_conv, 0, 1)[None], s_ssd[None],
        jnp.swapaxes(s_gdn_conv, 0, 1)[None], s_gdn[None],
    )
```

```python
import functools

import numpy as np
import jax
import jax.numpy as jnp
from jax import lax
from jax.experimental import pallas as pl
from jax.experimental.pallas import tpu as pltpu

F32, BF16 = jnp.float32, jnp.bfloat16

D_MODEL = 2048
SSD_HEADS, SSD_P, SSD_GROUPS, SSD_N = 16, 64, 2, 128
SSD_W = SSD_HEADS * SSD_P
SSD_GW = SSD_W // SSD_GROUPS
SSD_CONV = SSD_W + 2 * SSD_GROUPS * SSD_N
GDN_HEADS, GDN_D = 8, 128
GDN_W = GDN_HEADS * GDN_D
GDN_CONV = 3 * GDN_W
MEM_TOKENS, MEM_HEADS, MEM_D = 256, 4, 128
MEM_W = MEM_HEADS * MEM_D
MIX_W = SSD_W + GDN_W + MEM_W
CONV_K = 4
CHUNK = 64
EPS = 1e-6

LANES = 128
SUBLANES = 8
VMEM_LIMIT = 56 * 1024 * 1024
PROJ_ROWS = 1024
OUT_ROWS = 512
MEM_Q_ROWS = 512
SCAN_ROWS = 256
CONV_PHASES = 4

COL_QKV = 0
COL_XBC = COL_QKV + GDN_CONV
COL_QMEM = COL_XBC + SSD_CONV
COL_Z = COL_QMEM + MEM_W
N_MAIN = COL_Z + MIX_W
SM_DT, SM_B, SM_A = 0, SSD_HEADS, SSD_HEADS + GDN_HEADS


def _dot(a, b):
    return jnp.dot(a, b, preferred_element_type=F32)


def _dot_nt(a, b):
    return lax.dot_general(a, b, (((1,), (1,)), ((), ())), preferred_element_type=F32)


def _dot_tn(a, b):
    return lax.dot_general(a, b, (((0,), (0,)), ((), ())), preferred_element_type=F32)


def _split(x, n):
    parts, r = [], x
    for i in range(n):
        p = r.astype(BF16)
        parts.append(p)
        if i + 1 < n:
            r = r - p.astype(F32)
    return parts


def _sel_left(sel, x, n=3):
    return functools.reduce(lambda a, b: a + b, [_dot(sel, p) for p in _split(x, n)])


def _sel_right(x, sel, n=3):
    return functools.reduce(lambda a, b: a + b, [_dot(p, sel) for p in _split(x, n)])


def _sel_right_nt(x, sel, n=3):
    return functools.reduce(lambda a, b: a + b, [_dot_nt(p, sel) for p in _split(x, n)])


def _transpose_sel(x, n=3):
    eye = _eye(LANES).astype(BF16)
    return functools.reduce(lambda a, b: a + b, [_dot_nt(eye, p) for p in _split(x, n)])


def _eye(n):
    return (lax.broadcasted_iota(jnp.int32, (n, n), 0) == lax.broadcasted_iota(jnp.int32, (n, n), 1)).astype(F32)


def _sigmoid(x):
    return 1.0 / (1.0 + jnp.exp(-x))


def _silu(x):
    return x * _sigmoid(x)


def _softplus(x):
    return jnp.maximum(x, 0.0) + jnp.log1p(jnp.exp(-jnp.abs(x)))


def _params(*sem):
    return pltpu.CompilerParams(dimension_semantics=sem, vmem_limit_bytes=VMEM_LIMIT)


def _row_tile(rows, preferred):
    return preferred if rows % preferred == 0 else rows


def _rmsnorm_bf16(x, nw):
    ms = jnp.mean(x * x, axis=-1, keepdims=True)
    return (x * lax.rsqrt(ms + EPS) * nw).astype(BF16)


def _norm_matmul_kernel(x_ref, nw_ref, w_ref, ws_ref, o_ref, os_ref, h_ref):
    @pl.when(pl.program_id(1) == 0)
    def _():
        h = _rmsnorm_bf16(x_ref[...], nw_ref[...])
        h_ref[...] = h
        os_ref[...] = _dot_nt(h, ws_ref[...])

    o_ref[...] = _dot_nt(h_ref[...], w_ref[...])


def _norm_matmul(x, nw, w_t, ws_t, tm, tn):
    m, k = x.shape
    n = w_t.shape[0]
    ns = ws_t.shape[0]
    return pl.pallas_call(
        _norm_matmul_kernel,
        grid=(m // tm, n // tn),
        in_specs=[
            pl.BlockSpec((tm, k), lambda i, j: (i, 0)),
            pl.BlockSpec((1, k), lambda i, j: (0, 0)),
            pl.BlockSpec((tn, k), lambda i, j: (j, 0)),
            pl.BlockSpec((ns, k), lambda i, j: (0, 0)),
        ],
        out_specs=[
            pl.BlockSpec((tm, tn), lambda i, j: (i, j)),
            pl.BlockSpec((tm, ns), lambda i, j: (i, 0)),
        ],
        out_shape=[jax.ShapeDtypeStruct((m, n), F32), jax.ShapeDtypeStruct((m, ns), F32)],
        scratch_shapes=[pltpu.VMEM((tm, k), BF16)],
        compiler_params=_params("parallel", "arbitrary"),
        name="norm_matmul",
    )(x, nw, w_t, ws_t)


def _mem_kv_kernel(x_ref, nw_ref, w_ref, o_ref):
    o_ref[...] = _dot(_rmsnorm_bf16(x_ref[...], nw_ref[...]), w_ref[...])


def _mem_kv(x, nw, w, tm):
    m, k = x.shape
    n = w.shape[1]
    return pl.pallas_call(
        _mem_kv_kernel,
        grid=(m // tm,),
        in_specs=[pl.BlockSpec((tm, k), lambda i: (i, 0)), pl.BlockSpec((1, k), lambda i: (0, 0)),
                  pl.BlockSpec((k, n), lambda i: (0, 0))],
        out_specs=pl.BlockSpec((tm, n), lambda i: (i, 0)),
        out_shape=jax.ShapeDtypeStruct((m, n), F32),
        compiler_params=_params("parallel"),
        name="mem_kv",
    )(x, nw, w)


def _out_proj_kernel(y1_ref, y2_ref, y3_ref, w_ref, x_ref, fw_ref, o_ref):
    n1, n2 = y1_ref.shape[1], y1_ref.shape[1] + y2_ref.shape[1]
    acc = (_dot(y1_ref[...].astype(BF16), w_ref[:n1, :]) + _dot(y2_ref[...].astype(BF16), w_ref[n1:n2, :])
           + _dot(y3_ref[...].astype(BF16), w_ref[n2:, :]))
    r = x_ref[...] + acc
    ms = jnp.mean(r * r, axis=-1, keepdims=True)
    o_ref[...] = r * lax.rsqrt(ms + EPS) * fw_ref[...]


def _out_proj(y1, y2, y3, w, x, fw, tm):
    m, d = x.shape
    row = lambda i: (i, 0)
    whole = lambda i: (0, 0)
    return pl.pallas_call(
        _out_proj_kernel,
        grid=(m // tm,),
        in_specs=[
            pl.BlockSpec((tm, y1.shape[1]), row), pl.BlockSpec((tm, y2.shape[1]), row), pl.BlockSpec((tm, y3.shape[1]), row),
            pl.BlockSpec(w.shape, whole), pl.BlockSpec((tm, d), row), pl.BlockSpec((1, d), whole),
        ],
        out_specs=pl.BlockSpec((tm, d), row),
        out_shape=jax.ShapeDtypeStruct((m, d), F32),
        compiler_params=_params("parallel"),
        name="out_proj",
    )(y1, y2, y3, w, x, fw)


def _causal_conv_tile(u_ref, ubuf_ref, cw_ref, cb_ref, out_ref, tail_ref):
    t, width = u_ref.shape
    n = t // CONV_PHASES
    for s in range(width // LANES):
        cs = slice(s * LANES, (s + 1) * LANES)
        ubuf_ref[s, SUBLANES:SUBLANES + t, :] = u_ref[:, cs]
        taps = {d: ubuf_ref[s, pl.ds(SUBLANES + d, n, stride=CONV_PHASES), :] for d in range(1 - CONV_K, CONV_PHASES)}
        w = [cw_ref[j:j + 1, cs] for j in range(CONV_K)]
        for r in range(CONV_PHASES):
            acc = w[CONV_K - 1] * taps[r]
            if cb_ref is not None:
                acc = acc + cb_ref[:, cs]
            for j in range(CONV_K - 1):
                acc = acc + w[j] * taps[r - (CONV_K - 1) + j]
            out_ref[s, pl.ds(r, n, stride=CONV_PHASES), :] = _silu(acc)
        tail = ubuf_ref[s, t:t + SUBLANES, :]
        ubuf_ref[s, 0:SUBLANES, :] = tail
        tail_ref[0, :, cs] = tail


def _head_norm_gate(y, msq, width, z, mixw):
    return y * lax.rsqrt(msq * (1.0 / width) + EPS) * mixw * _silu(z)


def _ssd_prompt_kernel(xbc_ref, sm_ref, z_ref, cw_ref, cb_ref, dtb_ref, alog_ref, dexp_ref, mixw_ref, e_ref,
                       y_ref, tail_ref, state_ref, ubuf_ref, conv_ref, h_ref):
    c = pl.program_id(1)
    t = xbc_ref.shape[0]
    subs = range(t // CHUNK)
    groups = range(SSD_GROUPS)
    blocks = range(SSD_GW // LANES)

    @pl.when(c == 0)
    def _():
        ubuf_ref[:, 0:SUBLANES, :] = jnp.zeros((SSD_CONV // LANES, SUBLANES, LANES), F32)
        h_ref[...] = jnp.zeros_like(h_ref)

    _causal_conv_tile(xbc_ref, ubuf_ref, cw_ref, cb_ref, conv_ref, tail_ref)
    xs = jnp.concatenate([conv_ref[s] for s in range(SSD_W // LANES)], axis=1)
    e = e_ref[...]
    rows = [slice(j * CHUNK, (j + 1) * CHUNK) for j in subs]
    gs = [slice(g * SSD_GW, (g + 1) * SSD_GW) for g in groups]

    dt = _softplus(sm_ref[...] + dtb_ref[...])
    a = dt * (-jnp.exp(alog_ref[...]))
    rt = lax.broadcasted_iota(jnp.int32, (t, t), 0)
    ct = lax.broadcasted_iota(jnp.int32, (t, t), 1)
    chunk_causal = (rt >= ct) & (rt // CHUNK == ct // CHUNK)
    cum = _sel_left(chunk_causal.astype(BF16), a)
    cum_t = _transpose_sel(cum)
    ecum = jnp.exp(cum)
    wend = jnp.concatenate([jnp.exp(cum[(j + 1) * CHUNK - 1:(j + 1) * CHUNK, :] - cum[rows[j]]) for j in subs], axis=0)
    dt_x = _sel_right(dt, e, 2)
    ecum_x = _sel_right(ecum, e, 2)
    wend_x = _sel_right(wend, e, 2)

    xdt = xs * dt_x
    xdt_b = xdt.astype(BF16)
    xw_b = (xdt * wend_x).astype(BF16)
    causal = lax.broadcasted_iota(jnp.int32, (CHUNK, CHUNK), 0) >= lax.broadcasted_iota(jnp.int32, (CHUNK, CHUNK), 1)
    left = lax.broadcasted_iota(jnp.int32, (CHUNK, LANES), 1) < SSD_P

    jg = [(j, g) for j in subs for g in groups]
    b_slab, c_slab = SSD_W // LANES, SSD_W // LANES + SSD_GROUPS
    bmat = {(j, g): conv_ref[b_slab + g, rows[j], :].astype(BF16) for j, g in jg}
    cmat = {(j, g): conv_ref[c_slab + g, rows[j], :].astype(BF16) for j, g in jg}
    cb = {p: _dot_nt(cmat[p], bmat[p]) for p in jg}
    inc = {(j, g): _dot_tn(bmat[j, g], xw_b[rows[j], gs[g]]) for j, g in jg}
    scores = {}
    for j, g in jg:
        for blk in blocks:
            for half in range(2):
                h = (g * len(blocks) + blk) * 2 + half
                diff = cum[rows[j], h:h + 1] - cum_t[h:h + 1, rows[j]]
                lmat = jnp.where(causal, jnp.exp(jnp.minimum(diff, 0.0)), 0.0)
                scores[j, g, blk, half] = (cb[j, g] * lmat).astype(BF16)
    intra = {}
    for j, g in jg:
        for blk in blocks:
            lanes = slice(g * SSD_GW + blk * LANES, g * SSD_GW + (blk + 1) * LANES)
            xb = xdt_b[rows[j], lanes]
            zero = jnp.zeros_like(xb)
            intra[j, g, blk] = (_dot(scores[j, g, blk, 0], jnp.where(left, xb, zero))
                                + _dot(scores[j, g, blk, 1], jnp.where(left, zero, xb)))

    state = {(0, g): h_ref[:, gs[g]] for g in groups}
    for j in subs:
        last = (j + 1) * CHUNK - 1
        for g in groups:
            state[j + 1, g] = state[j, g] * ecum_x[last:last + 1, gs[g]] + inc[j, g]
    for g in groups:
        h_ref[:, gs[g]] = state[len(subs), g]
    inter = {p: _dot(cmat[p], state[p].astype(BF16)) for p in jg}
    inter_x = jnp.concatenate([jnp.concatenate([inter[j, g] for g in groups], axis=1) for j in subs], axis=0) * ecum_x
    intra_x = jnp.concatenate([jnp.concatenate([intra[j, g, blk] for g in groups for blk in blocks], axis=1)
                               for j in subs], axis=0)
    y = intra_x + inter_x + dexp_ref[...] * xs
    msq = _sel_right(_sel_right_nt(y * y, e, 2), e, 2)
    y_ref[...] = _head_norm_gate(y, msq, SSD_P, z_ref[...], mixw_ref[...]).astype(BF16)

    @pl.when(c == pl.num_programs(1) - 1)
    def _():
        state_ref[0] = h_ref[...].T.reshape(SSD_HEADS, SSD_P, SSD_N)


def _ssd_prompt(proj, small, batch, cw, cb, dtb, alog, dexp, mixw, e, tile, riders):
    rows = proj.shape[0]
    nc = rows // batch // tile
    row = lambda b, c: (b * nc + c, 0)
    whole = lambda b, c: (0, 0)
    return _scan_with_riders(
        "ssd_prompt", (batch, nc), _ssd_prompt_kernel,
        arrays=(proj, small, proj, cw, cb, dtb, alog, dexp, mixw, e),
        in_specs=[
            pl.BlockSpec((tile, SSD_CONV), lambda b, c: (b * nc + c, COL_XBC // SSD_CONV)),
            pl.BlockSpec((tile, LANES), row),
            pl.BlockSpec((tile, SSD_W), lambda b, c: (b * nc + c, COL_Z // SSD_W)),
            pl.BlockSpec(cw.shape, whole), pl.BlockSpec(cb.shape, whole), pl.BlockSpec(dtb.shape, whole),
            pl.BlockSpec(alog.shape, whole), pl.BlockSpec(dexp.shape, whole), pl.BlockSpec(mixw.shape, whole),
            pl.BlockSpec(e.shape, whole),
        ],
        out_shape=[
            jax.ShapeDtypeStruct((rows, SSD_W), BF16),
            jax.ShapeDtypeStruct((batch, SUBLANES, SSD_CONV), F32),
            jax.ShapeDtypeStruct((batch, SSD_HEADS, SSD_P, SSD_N), F32),
        ],
        out_specs=[
            pl.BlockSpec((tile, SSD_W), row),
            pl.BlockSpec((1, SUBLANES, SSD_CONV), lambda b, c: (b, 0, 0)),
            pl.BlockSpec((1, SSD_HEADS, SSD_P, SSD_N), lambda b, c: (b, 0, 0, 0)),
        ],
        scratch_shapes=[pltpu.VMEM((SSD_CONV // LANES, tile + SUBLANES, LANES), F32),
                        pltpu.VMEM((SSD_CONV // LANES, tile, LANES), F32), pltpu.VMEM((SSD_N, SSD_W), F32)],
        riders=riders)


def _unit_lower_inverses(a_stricts, ri, ci):
    t = a_stricts[0].shape[0]
    eye = _eye(t)
    first = (ri == ci + 1) & (ci % 2 == 0)
    invs = [eye - jnp.where(first, a, 0.0) for a in a_stricts]
    a_bs = [a.astype(BF16) for a in a_stricts]
    zero = jnp.zeros((t, t), BF16)
    s = 2
    while s < t:
        sel = (ri // (2 * s) == ci // (2 * s)) & ((ri // s) % 2 == 1) & ((ci // s) % 2 == 0)
        inv_bs = [inv.astype(BF16) for inv in invs]
        lefts = [_dot(inv_b, jnp.where(sel, a_b, zero)).astype(BF16) for inv_b, a_b in zip(inv_bs, a_bs)]
        invs = [inv - _dot(left, inv_b) for inv, left, inv_b in zip(invs, lefts, inv_bs)]
        s *= 2
    return invs


def _gdn_prompt_kernel(qkv_ref, sm_ref, z_ref, cw_ref, gb_ref, galog_ref, mixw_ref,
                       y_ref, tail_ref, state_ref, ubuf_ref, conv_ref, s_ref):
    c = pl.program_id(1)
    t = qkv_ref.shape[0]
    subs = range(t // CHUNK)
    heads = range(GDN_HEADS)

    @pl.when(c == 0)
    def _():
        ubuf_ref[:, 0:SUBLANES, :] = jnp.zeros((GDN_CONV // LANES, SUBLANES, LANES), F32)
        s_ref[...] = jnp.zeros_like(s_ref)

    _causal_conv_tile(qkv_ref, ubuf_ref, cw_ref, None, conv_ref, tail_ref)

    sm = sm_ref[...]
    beta = _sigmoid(sm)
    g = -jnp.exp(galog_ref[...]) * _softplus(sm + gb_ref[...])
    rt = lax.broadcasted_iota(jnp.int32, (t, t), 0)
    ct = lax.broadcasted_iota(jnp.int32, (t, t), 1)
    chunk_causal = (rt >= ct) & (rt // CHUNK == ct // CHUNK)
    gc = _sel_left(chunk_causal.astype(BF16), g)
    gc_t = _transpose_sel(gc)
    eg = jnp.exp(gc)
    ri = lax.broadcasted_iota(jnp.int32, (CHUNK, CHUNK), 0)
    ci = lax.broadcasted_iota(jnp.int32, (CHUNK, CHUNK), 1)
    causal = ri >= ci
    strict = ri > ci

    rows = [slice(j * CHUNK, (j + 1) * CHUNK) for j in subs]
    hs = [slice(h * GDN_D, (h + 1) * GDN_D) for h in heads]
    la = [SM_A + h for h in heads]
    pairs = [(j, h) for j in subs for h in heads]
    q, k, kb, vb, kbg, qg, decay = {}, {}, {}, {}, {}, {}, {}
    for h in heads:
        qf, kf, vf = conv_ref[h], conv_ref[GDN_HEADS + h], conv_ref[2 * GDN_HEADS + h]
        qf = qf * lax.rsqrt(jnp.sum(qf * qf, axis=-1, keepdims=True) + EPS) * (GDN_D ** -0.5)
        kf = kf * lax.rsqrt(jnp.sum(kf * kf, axis=-1, keepdims=True) + EPS)
        b_col = beta[:, SM_B + h:SM_B + h + 1]
        eg_col = eg[:, la[h]:la[h] + 1]
        kbf = kf * b_col
        vbf, kbgf, qgf = (vf * b_col).astype(BF16), (kbf * eg_col).astype(BF16), (qf * eg_col).astype(BF16)
        for j in subs:
            q[j, h], k[j, h], kb[j, h] = qf[rows[j]].astype(BF16), kf[rows[j]], kbf[rows[j]].astype(BF16)
            vb[j, h], kbg[j, h], qg[j, h] = vbf[rows[j]], kbgf[rows[j]], qgf[rows[j]]
            diff = gc[rows[j], la[h]:la[h] + 1] - gc_t[la[h]:la[h] + 1, rows[j]]
            decay[j, h] = jnp.where(causal, jnp.exp(jnp.minimum(diff, 0.0)), 0.0)
    k_b = {p: k[p].astype(BF16) for p in pairs}
    kq = {p: _dot_nt(jnp.concatenate([kb[p], q[p]], axis=0), k_b[p]) for p in pairs}
    a_strict = [jnp.where(strict, kq[p][:CHUNK] * decay[p], 0.0) for p in pairs]
    attn = {p: (kq[p][CHUNK:] * decay[p]).astype(BF16) for p in pairs}
    t_inv = dict(zip(pairs, [x.astype(BF16) for x in _unit_lower_inverses(a_strict, ri, ci)]))
    uw = {p: _dot(t_inv[p], jnp.concatenate([vb[p], kbg[p]], axis=1)) for p in pairs}
    u = {p: uw[p][:, :GDN_D] for p in pairs}
    wk = {p: uw[p][:, GDN_D:].astype(BF16) for p in pairs}

    state = [s_ref[h] for h in heads]
    for j in subs:
        g_last = gc[(j + 1) * CHUNK - 1:(j + 1) * CHUNK, :]
        eend = jnp.exp(g_last - gc[rows[j]])
        elast = jnp.exp(g_last)
        s_b = [x.astype(BF16) for x in state]
        v_new = [(u[j, h] - _dot(wk[j, h], s_b[h])).astype(BF16) for h in heads]
        k_end = [(k[j, h] * eend[:, la[h]:la[h] + 1]).astype(BF16) for h in heads]
        s_inc = [_dot_tn(k_end[h], v_new[h]) for h in heads]
        state = [state[h] * elast[:, la[h]:la[h] + 1] + s_inc[h] for h in heads]
        o = [_dot(qg[j, h], s_b[h]) + _dot(attn[j, h], v_new[h]) for h in heads]
        msq = [jnp.sum(x * x, axis=-1, keepdims=True) for x in o]
        for h in heads:
            y_ref[rows[j], hs[h]] = _head_norm_gate(o[h], msq[h], GDN_D, z_ref[rows[j], hs[h]], mixw_ref[:, hs[h]]).astype(BF16)
    for h in heads:
        s_ref[h] = state[h]

    @pl.when(c == pl.num_programs(1) - 1)
    def _():
        state_ref[0] = s_ref[...]


def _gdn_prompt(proj, small, batch, cw, gb, galog, mixw, tile, riders):
    rows = proj.shape[0]
    nc = rows // batch // tile
    row = lambda b, c: (b * nc + c, 0)
    whole = lambda b, c: (0, 0)
    return _scan_with_riders(
        "gdn_prompt", (batch, nc), _gdn_prompt_kernel,
        arrays=(proj, small, proj, cw, gb, galog, mixw),
        in_specs=[
            pl.BlockSpec((tile, GDN_CONV), lambda b, c: (b * nc + c, COL_QKV // GDN_CONV)),
            pl.BlockSpec((tile, LANES), row),
            pl.BlockSpec((tile, GDN_W), lambda b, c: (b * nc + c, (COL_Z + SSD_W) // GDN_W)),
            pl.BlockSpec(cw.shape, whole), pl.BlockSpec(gb.shape, whole), pl.BlockSpec(galog.shape, whole),
            pl.BlockSpec(mixw.shape, whole),
        ],
        out_shape=[
            jax.ShapeDtypeStruct((rows, GDN_W), BF16),
            jax.ShapeDtypeStruct((batch, SUBLANES, GDN_CONV), F32),
            jax.ShapeDtypeStruct((batch, GDN_HEADS, GDN_D, GDN_D), F32),
        ],
        out_specs=[
            pl.BlockSpec((tile, GDN_W), row),
            pl.BlockSpec((1, SUBLANES, GDN_CONV), lambda b, c: (b, 0, 0)),
            pl.BlockSpec((1, GDN_HEADS, GDN_D, GDN_D), lambda b, c: (b, 0, 0, 0)),
        ],
        scratch_shapes=[pltpu.VMEM((GDN_CONV // LANES, tile + SUBLANES, LANES), F32),
                        pltpu.VMEM((GDN_CONV // LANES, tile, LANES), F32), pltpu.VMEM((GDN_HEADS, GDN_D, GDN_D), F32)],
        riders=riders)


def _mem_prompt_kernel(q_ref, k_ref, v_ref, z_ref, mixw_ref, y_ref):
    heads = range(MEM_HEADS)
    hs = [slice(h * MEM_D, (h + 1) * MEM_D) for h in heads]
    s = [_dot_nt(q_ref[:, hs[h]].astype(BF16), k_ref[:, hs[h]].astype(BF16)) * (MEM_D ** -0.5) for h in heads]
    e = [jnp.exp(x - jnp.max(x, axis=-1, keepdims=True)) for x in s]
    p = [(x / jnp.sum(x, axis=-1, keepdims=True)).astype(BF16) for x in e]
    o = [_dot(p[h], v_ref[:, hs[h]].astype(BF16)) for h in heads]
    msq = [jnp.sum(x * x, axis=-1, keepdims=True) for x in o]
    for h in heads:
        y_ref[:, hs[h]] = _head_norm_gate(o[h], msq[h], MEM_D, z_ref[:, hs[h]], mixw_ref[:, hs[h]]).astype(BF16)


def _mem_prompt(proj, kv, batch, mixw, tq):
    rows = proj.shape[0]
    nq = rows // batch // tq
    return pl.pallas_call(
        _mem_prompt_kernel,
        grid=(batch, nq),
        in_specs=[
            pl.BlockSpec((tq, MEM_W), lambda b, i: (b * nq + i, COL_QMEM // MEM_W)),
            pl.BlockSpec((MEM_TOKENS, MEM_W), lambda b, i: (b, 0)),
            pl.BlockSpec((MEM_TOKENS, MEM_W), lambda b, i: (b, 1)),
            pl.BlockSpec((tq, MEM_W), lambda b, i: (b * nq + i, (COL_Z + SSD_W + GDN_W) // MEM_W)),
            pl.BlockSpec(mixw.shape, lambda b, i: (0, 0)),
        ],
        out_specs=pl.BlockSpec((tq, MEM_W), lambda b, i: (b * nq + i, 0)),
        out_shape=jax.ShapeDtypeStruct((rows, MEM_W), BF16),
        compiler_params=_params("parallel", "parallel"),
        name="mem_prompt",
    )(proj, kv, kv, proj, mixw)


def _conv_step(u, cst_ref, cst_out_ref, cw_ref, bias):
    acc = cw_ref[CONV_K - 1:CONV_K, :] * u
    if bias is not None:
        acc = acc + bias
    for j in range(CONV_K - 1):
        prev = cst_ref[j]
        acc = acc + cw_ref[j:j + 1, :] * prev
        if j > 0:
            cst_out_ref[j - 1] = prev
    cst_out_ref[CONV_K - 2] = u
    return _silu(acc)


def _rows_to_columns(x):
    pad = jnp.zeros((LANES - x.shape[0], x.shape[1]), F32)
    return jnp.concatenate([x, pad], axis=0).T


def _pick_rows(parts):
    rid = lax.broadcasted_iota(jnp.int32, parts[0].shape, 0)
    out = parts[0]
    for i in range(1, len(parts)):
        out = jnp.where(rid == i, parts[i], out)
    return out


def _ssd_decode_step(xbc_ref, sm_ref, z_ref, cst_ref, st_ref, cw_ref, cb_ref, dtb_ref, alog_ref, dexp_ref,
                     mixw_ref, e_ref, en_ref, y_ref, cst_out_ref, st_out_ref):
    xbc = _conv_step(xbc_ref[...], cst_ref, cst_out_ref, cw_ref, cb_ref[...])
    xs = xbc[:, :SSD_W]
    e = e_ref[...]
    dt = _softplus(sm_ref[...] + dtb_ref[...])
    dec = jnp.exp(dt * (-jnp.exp(alog_ref[...])))
    xd_t = _rows_to_columns(xs * _sel_right(dt, e))
    dec_n = _sel_right(dec, en_ref[...])

    y_groups = []
    for g in range(SSD_GROUPS):
        b_g = xbc[:, SSD_W + g * SSD_N:SSD_W + (g + 1) * SSD_N]
        c_g = xbc[:, SSD_W + (SSD_GROUPS + g) * SSD_N:SSD_W + (SSD_GROUPS + g + 1) * SSD_N].astype(BF16)
        per_row = []
        for i in range(xs.shape[0]):
            new = []
            for r in range(SSD_HEADS // SSD_GROUPS):
                h = g * (SSD_HEADS // SSD_GROUPS) + r
                col = xd_t[h * SSD_P:(h + 1) * SSD_P, i:i + 1]
                hn = st_ref[i, h] * dec_n[i:i + 1, h * SSD_N:(h + 1) * SSD_N] + col * b_g[i:i + 1, :]
                st_out_ref[i, h] = hn
                new.append(hn)
            hg = jnp.concatenate(new, axis=0).astype(BF16)
            per_row.append(_dot_nt(c_g, hg))
        y_groups.append(_pick_rows(per_row))
    y = jnp.concatenate(y_groups, axis=1) + dexp_ref[...] * xs
    msq = _sel_right(_sel_right_nt(y * y, e, 2), e, 2)
    y_ref[...] = _head_norm_gate(y, msq, SSD_P, z_ref[...], mixw_ref[...])


def _gdn_decode_step(qkv_ref, sm_ref, z_ref, cst_ref, st_ref, cw_ref, gb_ref, galog_ref, mixw_ref, en_ref,
                     y_ref, cst_out_ref, st_out_ref):
    qkv = _conv_step(qkv_ref[...], cst_ref, cst_out_ref, cw_ref, None)
    sm = sm_ref[...]
    beta = _sigmoid(sm)
    eg = jnp.exp(-jnp.exp(galog_ref[...]) * _softplus(sm + gb_ref[...]))
    eg_n = _sel_right(eg, en_ref[...])

    qs, ks = [], []
    for h in range(GDN_HEADS):
        q = qkv[:, h * GDN_D:(h + 1) * GDN_D]
        k = qkv[:, GDN_W + h * GDN_D:GDN_W + (h + 1) * GDN_D]
        qs.append(q * lax.rsqrt(jnp.sum(q * q, axis=-1, keepdims=True) + EPS) * (GDN_D ** -0.5))
        ks.append(k * lax.rsqrt(jnp.sum(k * k, axis=-1, keepdims=True) + EPS))
    k_t = _rows_to_columns(jnp.concatenate(ks, axis=1))

    for h in range(GDN_HEADS):
        hs = slice(h * GDN_D, (h + 1) * GDN_D)
        q, k = qs[h], ks[h]
        v = qkv[:, 2 * GDN_W + h * GDN_D:2 * GDN_W + (h + 1) * GDN_D]
        q_b, k_b = q.astype(BF16), k.astype(BF16)
        ks_rows, qs_rows = [], []
        for i in range(q.shape[0]):
            s_b = st_ref[i, h].astype(BF16)
            ks_rows.append(_dot(k_b, s_b))
            qs_rows.append(_dot(q_b, s_b))
        k_s, q_s = _pick_rows(ks_rows), _pick_rows(qs_rows)
        eg_h = eg_n[:, hs]
        v_new = beta[:, SM_B + h:SM_B + h + 1] * (v - eg_h * k_s)
        o = eg_h * q_s + jnp.sum(q * k, axis=-1, keepdims=True) * v_new
        for i in range(q.shape[0]):
            col = k_t[hs, i:i + 1]
            st_out_ref[i, h] = st_ref[i, h] * eg_h[i:i + 1, :] + col * v_new[i:i + 1, :]
        msq = jnp.sum(o * o, axis=-1, keepdims=True)
        y_ref[:, hs] = _head_norm_gate(o, msq, GDN_D, z_ref[:, hs], mixw_ref[:, hs])


def _mem_decode_step(q_ref, k_ref, v_ref, z_ref, mixw_ref, y_ref):
    heads, rows = range(MEM_HEADS), range(q_ref.shape[0])
    hs = [slice(h * MEM_D, (h + 1) * MEM_D) for h in heads]
    win = [pl.ds(h, MEM_TOKENS, stride=MEM_HEADS) for h in heads]
    q = [q_ref[:, hs[h]].astype(BF16) for h in heads]
    s = [_pick_rows([_dot_nt(q[h], k_ref[i, win[h], :].astype(BF16)) for i in rows]) * (MEM_D ** -0.5) for h in heads]
    e = [jnp.exp(x - jnp.max(x, axis=-1, keepdims=True)) for x in s]
    p = [(x / jnp.sum(x, axis=-1, keepdims=True)).astype(BF16) for x in e]
    o = [_pick_rows([_dot(p[h], v_ref[i, win[h], :].astype(BF16)) for i in rows]) for h in heads]
    msq = [jnp.sum(x * x, axis=-1, keepdims=True) for x in o]
    for h in heads:
        y_ref[:, hs[h]] = _head_norm_gate(o[h], msq[h], MEM_D, z_ref[:, hs[h]], mixw_ref[:, hs[h]])


DEC_ROWS = SUBLANES


def _ssd_decode_kernel(proj_ref, sm_ref, cst_ref, st_ref, cw_ref, cb_ref, dtb_ref, alog_ref, dexp_ref, mixw_ref, e_ref, en_ref,
                       y_ref, cst_out_ref, st_out_ref):
    _ssd_decode_step(proj_ref.at[:, COL_XBC:COL_XBC + SSD_CONV], sm_ref, proj_ref.at[:, COL_Z:COL_Z + SSD_W], cst_ref, st_ref,
                     cw_ref, cb_ref, dtb_ref, alog_ref, dexp_ref, mixw_ref.at[:, :SSD_W], e_ref, en_ref,
                     y_ref, cst_out_ref, st_out_ref)


def _gdn_decode_kernel(proj_ref, sm_ref, cst_ref, st_ref, cw_ref, gb_ref, galog_ref, mixw_ref, en_ref,
                       y_ref, cst_out_ref, st_out_ref):
    _gdn_decode_step(proj_ref.at[:, COL_QKV:COL_QKV + GDN_CONV], sm_ref, proj_ref.at[:, COL_Z + SSD_W:COL_Z + SSD_W + GDN_W],
                     cst_ref, st_ref, cw_ref, gb_ref, galog_ref, mixw_ref.at[:, SSD_W:SSD_W + GDN_W], en_ref,
                     y_ref, cst_out_ref, st_out_ref)


def _mem_decode_kernel(proj_ref, k_ref, v_ref, mixw_ref, y_ref):
    _mem_decode_step(proj_ref.at[:, COL_QMEM:COL_QMEM + MEM_W], k_ref, v_ref, proj_ref.at[:, N_MAIN - MEM_W:N_MAIN],
                     mixw_ref.at[:, SSD_W + GDN_W:], y_ref)


def _decode_rider(kernel, per_row_in, consts, out_shape, steps, step_index, phase):
    groups = per_row_in[0].shape[0] // DEC_ROWS
    assert steps % groups == 0
    per_group = steps // groups
    phase = phase % per_group
    group_of = lambda *g: jnp.maximum(step_index(*g) - phase, 0) // per_group

    def group_spec(a):
        axis = 1 if len(a.shape) == 3 and a.shape[0] == CONV_K - 1 else 0
        block = a.shape[:axis] + (DEC_ROWS,) + a.shape[axis + 1:]
        return pl.BlockSpec(block, lambda *g: (0,) * axis + (group_of(*g),) + (0,) * (len(block) - axis - 1))

    whole = lambda a: pl.BlockSpec(a.shape, lambda *g, nd=a.ndim: (0,) * nd)
    return dict(kernel=kernel, arrays=tuple(per_row_in) + tuple(consts), steps_per_group=per_group, phase=phase,
                in_specs=[group_spec(a) for a in per_row_in] + [whole(a) for a in consts],
                out_shape=list(out_shape), out_specs=[group_spec(a) for a in out_shape])


def _scan_with_riders_kernel(*refs, scan_kernel, n_scan_in, n_scan_out, riders):
    scan_in, refs = refs[:n_scan_in], refs[n_scan_in:]
    rider_in = []
    for _, n_in, _, _, _ in riders:
        rider_in.append(refs[:n_in])
        refs = refs[n_in:]
    scan_out, refs = refs[:n_scan_out], refs[n_scan_out:]
    rider_out = []
    for _, _, n_out, _, _ in riders:
        rider_out.append(refs[:n_out])
        refs = refs[n_out:]
    scan_kernel(*scan_in, *scan_out, *refs)
    step = pl.program_id(0) * pl.num_programs(1) + pl.program_id(1)
    for (kernel, _, _, per_group, phase), ins, outs in zip(riders, rider_in, rider_out):
        pl.when(step % per_group == phase)(functools.partial(kernel, *ins, *outs))


def _scan_with_riders(name, grid, scan_kernel, arrays, in_specs, out_shape, out_specs, scratch_shapes, riders):
    body = functools.partial(
        _scan_with_riders_kernel, scan_kernel=scan_kernel, n_scan_in=len(arrays), n_scan_out=len(out_shape),
        riders=tuple((r["kernel"], len(r["arrays"]), len(r["out_shape"]), r["steps_per_group"], r["phase"]) for r in riders))
    outs = pl.pallas_call(
        body,
        grid=grid,
        in_specs=list(in_specs) + [s for r in riders for s in r["in_specs"]],
        out_specs=list(out_specs) + [s for r in riders for s in r["out_specs"]],
        out_shape=list(out_shape) + [s for r in riders for s in r["out_shape"]],
        scratch_shapes=scratch_shapes,
        compiler_params=_params("arbitrary", "arbitrary"),
        name=name,
    )(*arrays, *[a for r in riders for a in r["arrays"]])
    scan_res, outs = outs[:len(out_shape)], outs[len(out_shape):]
    rider_res = []
    for r in riders:
        rider_res.append(outs[:len(r["out_shape"])])
        outs = outs[len(r["out_shape"]):]
    return scan_res, rider_res


IN_DT = SSD_CONV
IN_QKV = IN_DT + SSD_HEADS
IN_B = IN_QKV + GDN_CONV
IN_QMEM = IN_B + 2 * GDN_HEADS
IN_COLS = IN_QMEM + MEM_W + MIX_W
PREP_COLS = 256


def _prep_w_in_kernel(w_ref, x_ref, nw_ref, main_ref, small_ref, o_ref, os_ref, h_ref):
    i = pl.program_id(0)
    nblk = h_ref.shape[0]

    @pl.when(i == 0)
    def _():
        h = _rmsnorm_bf16(x_ref[...], nw_ref[...])
        for kb in range(nblk):
            h_ref[kb] = h[:, kb * PREP_COLS:(kb + 1) * PREP_COLS]
        o_ref[...] = jnp.zeros_like(o_ref)
        os_ref[...] = jnp.zeros_like(os_ref)

    main_ref[COL_QKV:COL_QKV + GDN_CONV, :] = w_ref[IN_QKV:IN_B, :].astype(BF16)
    main_ref[COL_XBC:COL_XBC + SSD_CONV, :] = w_ref[:SSD_CONV, :].astype(BF16)
    main_ref[COL_QMEM:N_MAIN, :] = w_ref[IN_QMEM:IN_COLS, :].astype(BF16)
    small_ref[SM_DT:SM_B, :] = w_ref[IN_DT:IN_QKV, :].astype(BF16)
    small_ref[SM_B:SM_A + GDN_HEADS, :] = w_ref[IN_B:IN_QMEM, :].astype(BF16)
    small_ref[SM_A + GDN_HEADS:, :] = jnp.zeros((LANES - SM_A - GDN_HEADS, w_ref.shape[1]), BF16)

    hb = h_ref[i]
    o_ref[...] += _dot_nt(hb, main_ref[...])
    os_ref[...] += _dot_nt(hb, small_ref[...])


def _prep_w_in(w_t, x, nw):
    n, k = w_t.shape
    rows = x.shape[0]
    assert n == IN_COLS
    whole = lambda i: (0, 0)
    return pl.pallas_call(
        _prep_w_in_kernel,
        grid=(k // PREP_COLS,),
        in_specs=[pl.BlockSpec((IN_COLS, PREP_COLS), lambda i: (0, i)), pl.BlockSpec((rows, k), whole),
                  pl.BlockSpec((1, k), whole)],
        out_specs=[pl.BlockSpec((N_MAIN, PREP_COLS), lambda i: (0, i)), pl.BlockSpec((LANES, PREP_COLS), lambda i: (0, i)),
                   pl.BlockSpec((rows, N_MAIN), whole), pl.BlockSpec((rows, LANES), whole)],
        out_shape=[jax.ShapeDtypeStruct((N_MAIN, k), BF16), jax.ShapeDtypeStruct((LANES, k), BF16),
                   jax.ShapeDtypeStruct((rows, N_MAIN), F32), jax.ShapeDtypeStruct((rows, LANES), F32)],
        scratch_shapes=[pltpu.VMEM((k // PREP_COLS, rows, PREP_COLS), BF16)],
        compiler_params=_params("arbitrary"),
        name="prep_w_in",
    )(w_t, x, nw)


def _head_expander(heads, first_lane, width):
    m = np.zeros((LANES, heads * width), np.float32)
    for h in range(heads):
        m[first_lane + h, h * width:(h + 1) * width] = 1.0
    return jnp.asarray(m, BF16)


def _lane_row(vec, first_lane):
    return jnp.zeros((1, LANES), F32).at[0, first_lane:first_lane + vec.shape[0]].set(vec.astype(F32))


def kernel(x_prompt, x_sample, mem_prompt, state_ssd_conv, state_ssd, state_gdn_conv, state_gdn, cache_mem_k, cache_mem_v, norm_w, w_in, ssd_conv_w, ssd_conv_b, ssd_dt_bias, ssd_A_log, ssd_D, gdn_conv_w, gdn_dt_bias, gdn_A_log, mem_norm_w, w_mem_kv, mix_norm_w, w_out, final_norm_w):
    bp, seq, d = x_prompt.shape
    bs = x_sample.shape[0]
    assert (d, seq % CHUNK, norm_w.shape[0]) == (D_MODEL, 0, 1)

    nw = norm_w[0][None, :]
    xs = x_sample.reshape(bs, d)
    w_main, w_small, proj_s, small_s = _prep_w_in(w_in[0].T, xs, nw)
    wo = w_out[0].astype(BF16)
    mixw = mix_norm_w[0][None, :]
    mixw1, mixw2, mixw3 = mixw[:, :SSD_W], mixw[:, SSD_W:SSD_W + GDN_W], mixw[:, SSD_W + GDN_W:]
    fw = final_norm_w[None, :]
    ssd_dtb = _lane_row(ssd_dt_bias[0], SM_DT)
    ssd_alog = _lane_row(ssd_A_log[0], SM_DT)
    ssd_dexp = jnp.repeat(ssd_D[0].astype(F32), SSD_P)[None, :]
    gdn_b = _lane_row(gdn_dt_bias[0], SM_A)
    gdn_alog = _lane_row(gdn_A_log[0], SM_A)
    e_ssd = _head_expander(SSD_HEADS, SM_DT, SSD_P)
    e_ssd_n = _head_expander(SSD_HEADS, SM_DT, SSD_N)
    e_gdn_n = _head_expander(GDN_HEADS, SM_A, GDN_D)
    ssd_cw, ssd_cb, gdn_cw = ssd_conv_w[0], ssd_conv_b[0][None, :], gdn_conv_w[0]

    xp = x_prompt.reshape(bp * seq, d)
    proj_p, small_p = _norm_matmul(xp, nw, w_main, w_small, _row_tile(bp * seq, PROJ_ROWS), SSD_CONV)
    kv = _mem_kv(mem_prompt.reshape(bp * MEM_TOKENS, d), mem_norm_w[0][None, :], w_mem_kv[0].astype(BF16), MEM_TOKENS)
    scan_rows = SCAN_ROWS if seq % SCAN_ROWS == 0 else CHUNK
    scan_steps = bp * seq // scan_rows
    step_index = lambda b, c: b * (seq // scan_rows) + c

    ssd_cst, gdn_cst = jnp.swapaxes(state_ssd_conv[0], 0, 1), jnp.swapaxes(state_gdn_conv[0], 0, 1)
    mem_k = cache_mem_k.reshape(bs, MEM_TOKENS * MEM_HEADS, MEM_D)
    mem_v = cache_mem_v.reshape(bs, MEM_TOKENS * MEM_HEADS, MEM_D)
    row_out = lambda width: jax.ShapeDtypeStruct((bs, width), F32)
    like = lambda a: jax.ShapeDtypeStruct(a.shape, F32)
    ssd_rider = _decode_rider(
        _ssd_decode_kernel, (proj_s, small_s, ssd_cst, state_ssd[0]),
        (ssd_cw, ssd_cb, ssd_dtb, ssd_alog, ssd_dexp, mixw, e_ssd, e_ssd_n),
        (row_out(SSD_W), like(ssd_cst), like(state_ssd[0])), scan_steps, step_index, phase=0)
    mem_rider = _decode_rider(
        _mem_decode_kernel, (proj_s, mem_k, mem_v), (mixw,), (row_out(MEM_W),), scan_steps, step_index, phase=1)
    gdn_rider = _decode_rider(
        _gdn_decode_kernel, (proj_s, small_s, gdn_cst, state_gdn[0]), (gdn_cw, gdn_b, gdn_alog, mixw, e_gdn_n),
        (row_out(GDN_W), like(gdn_cst), like(state_gdn[0])), scan_steps, step_index, phase=0)

    (y_ssd, tail_ssd, p_ssd), ((ys_ssd, s_ssd_conv, s_ssd), (ys_mem,)) = _ssd_prompt(
        proj_p, small_p, bp, ssd_cw, ssd_cb, ssd_dtb, ssd_alog, ssd_dexp, mixw1, e_ssd, scan_rows, (ssd_rider, mem_rider))
    (y_gdn, tail_gdn, p_gdn), ((ys_gdn, s_gdn_conv, s_gdn),) = _gdn_prompt(
        proj_p, small_p, bp, gdn_cw, gdn_b, gdn_alog, mixw2, scan_rows, (gdn_rider,))
    y_mem = _mem_prompt(proj_p, kv, bp, mixw3, _row_tile(seq, MEM_Q_ROWS))
    y_prompt = _out_proj(y_ssd, y_gdn, y_mem, wo, xp, fw, _row_tile(bp * seq, OUT_ROWS)).reshape(bp, seq, d)
    y_sample = _out_proj(ys_ssd, ys_gdn, ys_mem, wo, xs, fw, bs).reshape(bs, 1, d)

    keep = CONV_K - 1
    mem_shape = (1, bp, MEM_TOKENS, MEM_HEADS, MEM_D)
    return (
        y_prompt, y_sample,
        tail_ssd[None, :, SUBLANES - keep:, :], p_ssd[None],
        tail_gdn[None, :, SUBLANES - keep:, :], p_gdn[None],
        kv[:, :MEM_W].reshape(mem_shape), kv[:, MEM_W:].reshape(mem_shape),
        jnp.swapaxes(s_ssd_conv, 0, 1)[None], s_ssd[None],
        jnp.swapaxes(s_gdn_conv, 0, 1)[None], s_gdn[None],
    )
```

```python
import functools

import numpy as np
import jax
import jax.numpy as jnp
from jax import lax
from jax.experimental import pallas as pl
from jax.experimental.pallas import tpu as pltpu

F32, BF16 = jnp.float32, jnp.bfloat16

D_MODEL = 2048
SSD_HEADS, SSD_P, SSD_GROUPS, SSD_N = 16, 64, 2, 128
SSD_W = SSD_HEADS * SSD_P
SSD_GW = SSD_W // SSD_GROUPS
SSD_CONV = SSD_W + 2 * SSD_GROUPS * SSD_N
GDN_HEADS, GDN_D = 8, 128
GDN_W = GDN_HEADS * GDN_D
GDN_CONV = 3 * GDN_W
MEM_TOKENS, MEM_HEADS, MEM_D = 256, 4, 128
MEM_W = MEM_HEADS * MEM_D
MIX_W = SSD_W + GDN_W + MEM_W
CONV_K = 4
CHUNK = 64
EPS = 1e-6

LANES = 128
SUBLANES = 8
VMEM_LIMIT = 56 * 1024 * 1024
PROJ_ROWS = 1024
OUT_ROWS = 512
MEM_Q_ROWS = 512
SCAN_ROWS = 256
CONV_PHASES = 4

COL_QKV = 0
COL_XBC = COL_QKV + GDN_CONV
COL_QMEM = COL_XBC + SSD_CONV
COL_Z = COL_QMEM + MEM_W
N_MAIN = COL_Z + MIX_W
SM_DT, SM_B, SM_A = 0, SSD_HEADS, SSD_HEADS + GDN_HEADS


def _dot(a, b):
    return jnp.dot(a, b, preferred_element_type=F32)


def _dot_nt(a, b):
    return lax.dot_general(a, b, (((1,), (1,)), ((), ())), preferred_element_type=F32)


def _dot_tn(a, b):
    return lax.dot_general(a, b, (((0,), (0,)), ((), ())), preferred_element_type=F32)


def _split(x, n):
    parts, r = [], x
    for i in range(n):
        p = r.astype(BF16)
        parts.append(p)
        if i + 1 < n:
            r = r - p.astype(F32)
    return parts


def _sel_left(sel, x, n=3):
    return functools.reduce(lambda a, b: a + b, [_dot(sel, p) for p in _split(x, n)])


def _sel_right(x, sel, n=3):
    return functools.reduce(lambda a, b: a + b, [_dot(p, sel) for p in _split(x, n)])


def _sel_right_nt(x, sel, n=3):
    return functools.reduce(lambda a, b: a + b, [_dot_nt(p, sel) for p in _split(x, n)])


def _transpose_sel(x, n=3):
    eye = _eye(LANES).astype(BF16)
    return functools.reduce(lambda a, b: a + b, [_dot_nt(eye, p) for p in _split(x, n)])


def _eye(n):
    return (lax.broadcasted_iota(jnp.int32, (n, n), 0) == lax.broadcasted_iota(jnp.int32, (n, n), 1)).astype(F32)


def _sigmoid(x):
    return 1.0 / (1.0 + jnp.exp(-x))


def _silu(x):
    return x * _sigmoid(x)


def _softplus(x):
    return jnp.maximum(x, 0.0) + jnp.log1p(jnp.exp(-jnp.abs(x)))


def _params(*sem):
    return pltpu.CompilerParams(dimension_semantics=sem, vmem_limit_bytes=VMEM_LIMIT)


def _row_tile(rows, preferred):
    return preferred if rows % preferred == 0 else rows


def _rmsnorm_bf16(x, nw):
    ms = jnp.mean(x * x, axis=-1, keepdims=True)
    return (x * lax.rsqrt(ms + EPS) * nw).astype(BF16)


def _norm_matmul_kernel(x_ref, nw_ref, w_ref, ws_ref, o_ref, os_ref, h_ref):
    @pl.when(pl.program_id(1) == 0)
    def _():
        h = _rmsnorm_bf16(x_ref[...], nw_ref[...])
        h_ref[...] = h
        os_ref[...] = _dot_nt(h, ws_ref[...])

    o_ref[...] = _dot_nt(h_ref[...], w_ref[...])


def _norm_matmul(x, nw, w_t, ws_t, tm, tn):
    m, k = x.shape
    n = w_t.shape[0]
    ns = ws_t.shape[0]
    return pl.pallas_call(
        _norm_matmul_kernel,
        grid=(m // tm, n // tn),
        in_specs=[
            pl.BlockSpec((tm, k), lambda i, j: (i, 0)),
            pl.BlockSpec((1, k), lambda i, j: (0, 0)),
            pl.BlockSpec((tn, k), lambda i, j: (j, 0)),
            pl.BlockSpec((ns, k), lambda i, j: (0, 0)),
        ],
        out_specs=[
            pl.BlockSpec((tm, tn), lambda i, j: (i, j)),
            pl.BlockSpec((tm, ns), lambda i, j: (i, 0)),
        ],
        out_shape=[jax.ShapeDtypeStruct((m, n), F32), jax.ShapeDtypeStruct((m, ns), F32)],
        scratch_shapes=[pltpu.VMEM((tm, k), BF16)],
        compiler_params=_params("parallel", "arbitrary"),
        name="norm_matmul",
    )(x, nw, w_t, ws_t)


def _mem_kv_kernel(x_ref, nw_ref, w_ref, o_ref):
    o_ref[...] = _dot(_rmsnorm_bf16(x_ref[...], nw_ref[...]), w_ref[...])


def _mem_kv(x, nw, w, tm):
    m, k = x.shape
    n = w.shape[1]
    return pl.pallas_call(
        _mem_kv_kernel,
        grid=(m // tm,),
        in_specs=[pl.BlockSpec((tm, k), lambda i: (i, 0)), pl.BlockSpec((1, k), lambda i: (0, 0)),
                  pl.BlockSpec((k, n), lambda i: (0, 0))],
        out_specs=pl.BlockSpec((tm, n), lambda i: (i, 0)),
        out_shape=jax.ShapeDtypeStruct((m, n), F32),
        compiler_params=_params("parallel"),
        name="mem_kv",
    )(x, nw, w)


def _out_proj_kernel(y1_ref, y2_ref, y3_ref, w_ref, x_ref, fw_ref, o_ref):
    n1, n2 = y1_ref.shape[1], y1_ref.shape[1] + y2_ref.shape[1]
    acc = (_dot(y1_ref[...].astype(BF16), w_ref[:n1, :]) + _dot(y2_ref[...].astype(BF16), w_ref[n1:n2, :])
           + _dot(y3_ref[...].astype(BF16), w_ref[n2:, :]))
    r = x_ref[...] + acc
    ms = jnp.mean(r * r, axis=-1, keepdims=True)
    o_ref[...] = r * lax.rsqrt(ms + EPS) * fw_ref[...]


def _out_proj(y1, y2, y3, w, x, fw, tm):
    m, d = x.shape
    row = lambda i: (i, 0)
    whole = lambda i: (0, 0)
    return pl.pallas_call(
        _out_proj_kernel,
        grid=(m // tm,),
        in_specs=[
            pl.BlockSpec((tm, y1.shape[1]), row), pl.BlockSpec((tm, y2.shape[1]), row), pl.BlockSpec((tm, y3.shape[1]), row),
            pl.BlockSpec(w.shape, whole), pl.BlockSpec((tm, d), row), pl.BlockSpec((1, d), whole),
        ],
        out_specs=pl.BlockSpec((tm, d), row),
        out_shape=jax.ShapeDtypeStruct((m, d), F32),
        compiler_params=_params("parallel"),
        name="out_proj",
    )(y1, y2, y3, w, x, fw)


def _causal_conv_tile(u_ref, ubuf_ref, cw_ref, cb_ref, out_ref, tail_ref):
    t, width = u_ref.shape
    n = t // CONV_PHASES
    for s in range(width // LANES):
        cs = slice(s * LANES, (s + 1) * LANES)
        ubuf_ref[s, SUBLANES:SUBLANES + t, :] = u_ref[:, cs]
        taps = {d: ubuf_ref[s, pl.ds(SUBLANES + d, n, stride=CONV_PHASES), :] for d in range(1 - CONV_K, CONV_PHASES)}
        w = [cw_ref[j:j + 1, cs] for j in range(CONV_K)]
        for r in range(CONV_PHASES):
            acc = w[CONV_K - 1] * taps[r]
            if cb_ref is not None:
                acc = acc + cb_ref[:, cs]
            for j in range(CONV_K - 1):
                acc = acc + w[j] * taps[r - (CONV_K - 1) + j]
            out_ref[s, pl.ds(r, n, stride=CONV_PHASES), :] = _silu(acc)
        tail = ubuf_ref[s, t:t + SUBLANES, :]
        ubuf_ref[s, 0:SUBLANES, :] = tail
        tail_ref[0, :, cs] = tail


def _head_norm_gate(y, msq, width, z, mixw):
    return y * lax.rsqrt(msq * (1.0 / width) + EPS) * mixw * _silu(z)


def _ssd_prompt_kernel(xbc_ref, sm_ref, z_ref, cw_ref, cb_ref, dtb_ref, alog_ref, dexp_ref, mixw_ref, e_ref,
                       y_ref, tail_ref, state_ref, ubuf_ref, conv_ref, h_ref):
    c = pl.program_id(1)
    t = xbc_ref.shape[0]
    subs = range(t // CHUNK)
    groups = range(SSD_GROUPS)
    blocks = range(SSD_GW // LANES)

    @pl.when(c == 0)
    def _():
        ubuf_ref[:, 0:SUBLANES, :] = jnp.zeros((SSD_CONV // LANES, SUBLANES, LANES), F32)
        h_ref[...] = jnp.zeros_like(h_ref)

    _causal_conv_tile(xbc_ref, ubuf_ref, cw_ref, cb_ref, conv_ref, tail_ref)
    xs = jnp.concatenate([conv_ref[s] for s in range(SSD_W // LANES)], axis=1)
    e = e_ref[...]
    rows = [slice(j * CHUNK, (j + 1) * CHUNK) for j in subs]
    gs = [slice(g * SSD_GW, (g + 1) * SSD_GW) for g in groups]

    dt = _softplus(sm_ref[...] + dtb_ref[...])
    a = dt * (-jnp.exp(alog_ref[...]))
    rt = lax.broadcasted_iota(jnp.int32, (t, t), 0)
    ct = lax.broadcasted_iota(jnp.int32, (t, t), 1)
    chunk_causal = (rt >= ct) & (rt // CHUNK == ct // CHUNK)
    cum = _sel_left(chunk_causal.astype(BF16), a)
    cum_t = _transpose_sel(cum)
    ecum = jnp.exp(cum)
    wend = jnp.concatenate([jnp.exp(cum[(j + 1) * CHUNK - 1:(j + 1) * CHUNK, :] - cum[rows[j]]) for j in subs], axis=0)
    dt_x = _sel_right(dt, e, 2)
    ecum_x = _sel_right(ecum, e, 2)
    wend_x = _sel_right(wend, e, 2)

    xdt = xs * dt_x
    xdt_b = xdt.astype(BF16)
    xw_b = (xdt * wend_x).astype(BF16)
    causal = lax.broadcasted_iota(jnp.int32, (CHUNK, CHUNK), 0) >= lax.broadcasted_iota(jnp.int32, (CHUNK, CHUNK), 1)
    left = lax.broadcasted_iota(jnp.int32, (CHUNK, LANES), 1) < SSD_P

    jg = [(j, g) for j in subs for g in groups]
    b_slab, c_slab = SSD_W // LANES, SSD_W // LANES + SSD_GROUPS
    bmat = {(j, g): conv_ref[b_slab + g, rows[j], :].astype(BF16) for j, g in jg}
    cmat = {(j, g): conv_ref[c_slab + g, rows[j], :].astype(BF16) for j, g in jg}
    cb = {p: _dot_nt(cmat[p], bmat[p]) for p in jg}
    inc = {(j, g): _dot_tn(bmat[j, g], xw_b[rows[j], gs[g]]) for j, g in jg}
    scores = {}
    for j, g in jg:
        for blk in blocks:
            for half in range(2):
                h = (g * len(blocks) + blk) * 2 + half
                diff = cum[rows[j], h:h + 1] - cum_t[h:h + 1, rows[j]]
                lmat = jnp.where(causal, jnp.exp(jnp.minimum(diff, 0.0)), 0.0)
                scores[j, g, blk, half] = (cb[j, g] * lmat).astype(BF16)
    intra = {}
    for j, g in jg:
        for blk in blocks:
            lanes = slice(g * SSD_GW + blk * LANES, g * SSD_GW + (blk + 1) * LANES)
            xb = xdt_b[rows[j], lanes]
            zero = jnp.zeros_like(xb)
            intra[j, g, blk] = (_dot(scores[j, g, blk, 0], jnp.where(left, xb, zero))
                                + _dot(scores[j, g, blk, 1], jnp.where(left, zero, xb)))

    state = {(0, g): h_ref[:, gs[g]] for g in groups}
    for j in subs:
        last = (j + 1) * CHUNK - 1
        for g in groups:
            state[j + 1, g] = state[j, g] * ecum_x[last:last + 1, gs[g]] + inc[j, g]
    for g in groups:
        h_ref[:, gs[g]] = state[len(subs), g]
    inter = {p: _dot(cmat[p], state[p].astype(BF16)) for p in jg}
    inter_x = jnp.concatenate([jnp.concatenate([inter[j, g] for g in groups], axis=1) for j in subs], axis=0) * ecum_x
    intra_x = jnp.concatenate([jnp.concatenate([intra[j, g, blk] for g in groups for blk in blocks], axis=1)
                               for j in subs], axis=0)
    y = intra_x + inter_x + dexp_ref[...] * xs
    msq = _sel_right(_sel_right_nt(y * y, e, 2), e, 2)
    y_ref[...] = _head_norm_gate(y, msq, SSD_P, z_ref[...], mixw_ref[...]).astype(BF16)

    @pl.when(c == pl.num_programs(1) - 1)
    def _():
        state_ref[0] = h_ref[...].T.reshape(SSD_HEADS, SSD_P, SSD_N)


def _ssd_prompt(proj, small, batch, cw, cb, dtb, alog, dexp, mixw, e, tile, riders):
    rows = proj.shape[0]
    nc = rows // batch // tile
    row = lambda b, c: (b * nc + c, 0)
    whole = lambda b, c: (0, 0)
    return _scan_with_riders(
        "ssd_prompt", (batch, nc), _ssd_prompt_kernel,
        arrays=(proj, small, proj, cw, cb, dtb, alog, dexp, mixw, e),
        in_specs=[
            pl.BlockSpec((tile, SSD_CONV), lambda b, c: (b * nc + c, COL_XBC // SSD_CONV)),
            pl.BlockSpec((tile, LANES), row),
            pl.BlockSpec((tile, SSD_W), lambda b, c: (b * nc + c, COL_Z // SSD_W)),
            pl.BlockSpec(cw.shape, whole), pl.BlockSpec(cb.shape, whole), pl.BlockSpec(dtb.shape, whole),
            pl.BlockSpec(alog.shape, whole), pl.BlockSpec(dexp.shape, whole), pl.BlockSpec(mixw.shape, whole),
            pl.BlockSpec(e.shape, whole),
        ],
        out_shape=[
            jax.ShapeDtypeStruct((rows, SSD_W), BF16),
            jax.ShapeDtypeStruct((batch, SUBLANES, SSD_CONV), F32),
            jax.ShapeDtypeStruct((batch, SSD_HEADS, SSD_P, SSD_N), F32),
        ],
        out_specs=[
            pl.BlockSpec((tile, SSD_W), row),
            pl.BlockSpec((1, SUBLANES, SSD_CONV), lambda b, c: (b, 0, 0)),
            pl.BlockSpec((1, SSD_HEADS, SSD_P, SSD_N), lambda b, c: (b, 0, 0, 0)),
        ],
        scratch_shapes=[pltpu.VMEM((SSD_CONV // LANES, tile + SUBLANES, LANES), F32),
                        pltpu.VMEM((SSD_CONV // LANES, tile, LANES), F32), pltpu.VMEM((SSD_N, SSD_W), F32)],
        riders=riders)


def _unit_lower_inverses(a_stricts, ri, ci):
    t = a_stricts[0].shape[0]
    eye = (ri == ci).astype(F32)
    first = (ri == ci + 1) & (ci % 2 == 0)
    invs = [eye - jnp.where(first, a, 0.0) for a in a_stricts]
    a_bs = [a.astype(BF16) for a in a_stricts]
    zero = jnp.zeros(a_bs[0].shape, BF16)
    s = 2
    while s < t:
        sel = (ri // (2 * s) == ci // (2 * s)) & ((ri // s) % 2 == 1) & ((ci // s) % 2 == 0)
        inv_bs = [inv.astype(BF16) for inv in invs]
        lefts = [_dot(inv_b, _pair_diag(jnp.where(sel, a_b, zero))).astype(BF16) for inv_b, a_b in zip(inv_bs, a_bs)]
        invs = [inv - _dot(left, _pair_diag(inv_b)) for inv, left, inv_b in zip(invs, lefts, inv_bs)]
        s *= 2
    return invs


def _pair_halves(x):
    left = lax.broadcasted_iota(jnp.int32, x.shape, 1) < x.shape[1] // 2
    zero = jnp.zeros_like(x)
    return jnp.where(left, x, zero), jnp.where(left, zero, x)


def _pair_diag(x):
    return jnp.concatenate(_pair_halves(x), axis=0)


def _gdn_prompt_kernel(qkv_ref, sm_ref, z_ref, cw_ref, gb_ref, galog_ref, mixw_ref,
                       y_ref, tail_ref, state_ref, ubuf_ref, conv_ref, s_ref):
    c = pl.program_id(1)
    t = qkv_ref.shape[0]
    subs = range(t // CHUNK)
    heads = range(GDN_HEADS)

    @pl.when(c == 0)
    def _():
        ubuf_ref[:, 0:SUBLANES, :] = jnp.zeros((GDN_CONV // LANES, SUBLANES, LANES), F32)
        s_ref[...] = jnp.zeros_like(s_ref)

    _causal_conv_tile(qkv_ref, ubuf_ref, cw_ref, None, conv_ref, tail_ref)

    sm = sm_ref[...]
    beta = _sigmoid(sm)
    g = -jnp.exp(galog_ref[...]) * _softplus(sm + gb_ref[...])
    rt = lax.broadcasted_iota(jnp.int32, (t, t), 0)
    ct = lax.broadcasted_iota(jnp.int32, (t, t), 1)
    chunk_causal = (rt >= ct) & (rt // CHUNK == ct // CHUNK)
    gc = _sel_left(chunk_causal.astype(BF16), g)
    gc_t = _transpose_sel(gc)
    eg = jnp.exp(gc)
    ri = lax.broadcasted_iota(jnp.int32, (CHUNK, 2 * CHUNK), 0)
    lane = lax.broadcasted_iota(jnp.int32, (CHUNK, 2 * CHUNK), 1)
    ci = lane % CHUNK
    causal = ri >= ci
    strict = ri > ci
    first_head = lane < CHUNK

    rows = [slice(j * CHUNK, (j + 1) * CHUNK) for j in subs]
    hs = [slice(h * GDN_D, (h + 1) * GDN_D) for h in heads]
    la = [SM_A + h for h in heads]
    packs = [(j, a) for j in subs for a in range(0, GDN_HEADS, 2)]
    q, k, kb, vb, kbg, qg = {}, {}, {}, {}, {}, {}
    for h in heads:
        qf, kf, vf = conv_ref[h], conv_ref[GDN_HEADS + h], conv_ref[2 * GDN_HEADS + h]
        qf = qf * lax.rsqrt(jnp.sum(qf * qf, axis=-1, keepdims=True) + EPS) * (GDN_D ** -0.5)
        kf = kf * lax.rsqrt(jnp.sum(kf * kf, axis=-1, keepdims=True) + EPS)
        b_col = beta[:, SM_B + h:SM_B + h + 1]
        eg_col = eg[:, la[h]:la[h] + 1]
        kbf = kf * b_col
        vbf, kbgf, qgf = (vf * b_col).astype(BF16), (kbf * eg_col).astype(BF16), (qf * eg_col).astype(BF16)
        for j in subs:
            q[j, h], k[j, h], kb[j, h] = qf[rows[j]].astype(BF16), kf[rows[j]], kbf[rows[j]].astype(BF16)
            vb[j, h], kbg[j, h], qg[j, h] = vbf[rows[j]], kbgf[rows[j]], qgf[rows[j]]
    decay, kq = {}, {}
    no_keys = jnp.zeros((CHUNK, GDN_D), BF16)
    for j, a in packs:
        b = a + 1
        col = jnp.where(first_head, gc[rows[j], la[a]:la[a] + 1], gc[rows[j], la[b]:la[b] + 1])
        row = jnp.concatenate([gc_t[la[a]:la[a] + 1, rows[j]], gc_t[la[b]:la[b] + 1, rows[j]]], axis=1)
        decay[j, a] = jnp.where(causal, jnp.exp(jnp.minimum(col - row, 0.0)), 0.0)
        kq[j, a] = (_dot_nt(jnp.concatenate([kb[j, a], q[j, a]], axis=0), jnp.concatenate([k[j, a].astype(BF16), no_keys], axis=0))
                    + _dot_nt(jnp.concatenate([kb[j, b], q[j, b]], axis=0), jnp.concatenate([no_keys, k[j, b].astype(BF16)], axis=0)))
    a_strict = [jnp.where(strict, kq[p][:CHUNK] * decay[p], 0.0) for p in packs]
    attn = {p: _pair_halves((kq[p][CHUNK:] * decay[p]).astype(BF16)) for p in packs}
    t_inv = dict(zip(packs, [_pair_halves(x.astype(BF16)) for x in _unit_lower_inverses(a_strict, ri, ci)]))
    u, wk = {}, {}
    for j, a in packs:
        rhs = jnp.concatenate([jnp.concatenate([vb[j, h], kbg[j, h]], axis=1) for h in (a, a + 1)], axis=0)
        for half, h in enumerate((a, a + 1)):
            uw = _dot(t_inv[j, a][half], rhs)
            u[j, h], wk[j, h] = uw[:, :GDN_D], uw[:, GDN_D:].astype(BF16)

    state = [s_ref[h] for h in heads]
    for j in subs:
        g_last = gc[(j + 1) * CHUNK - 1:(j + 1) * CHUNK, :]
        eend = jnp.exp(g_last - gc[rows[j]])
        elast = jnp.exp(g_last)
        s_b = [x.astype(BF16) for x in state]
        v_new = [(u[j, h] - _dot(wk[j, h], s_b[h])).astype(BF16) for h in heads]
        k_end = [(k[j, h] * eend[:, la[h]:la[h] + 1]).astype(BF16) for h in heads]
        s_inc = [_dot_tn(k_end[h], v_new[h]) for h in heads]
        state = [state[h] * elast[:, la[h]:la[h] + 1] + s_inc[h] for h in heads]
        v_pair = {a: jnp.concatenate([v_new[a], v_new[a + 1]], axis=0) for a in range(0, GDN_HEADS, 2)}
        o = [_dot(qg[j, h], s_b[h]) + _dot(attn[j, h - h % 2][h % 2], v_pair[h - h % 2]) for h in heads]
        msq = [jnp.sum(x * x, axis=-1, keepdims=True) for x in o]
        for h in heads:
            y_ref[rows[j], hs[h]] = _head_norm_gate(o[h], msq[h], GDN_D, z_ref[rows[j], hs[h]], mixw_ref[:, hs[h]]).astype(BF16)
    for h in heads:
        s_ref[h] = state[h]

    @pl.when(c == pl.num_programs(1) - 1)
    def _():
        state_ref[0] = s_ref[...]


def _gdn_prompt(proj, small, batch, cw, gb, galog, mixw, tile, riders):
    rows = proj.shape[0]
    nc = rows // batch // tile
    row = lambda b, c: (b * nc + c, 0)
    whole = lambda b, c: (0, 0)
    return _scan_with_riders(
        "gdn_prompt", (batch, nc), _gdn_prompt_kernel,
        arrays=(proj, small, proj, cw, gb, galog, mixw),
        in_specs=[
            pl.BlockSpec((tile, GDN_CONV), lambda b, c: (b * nc + c, COL_QKV // GDN_CONV)),
            pl.BlockSpec((tile, LANES), row),
            pl.BlockSpec((tile, GDN_W), lambda b, c: (b * nc + c, (COL_Z + SSD_W) // GDN_W)),
            pl.BlockSpec(cw.shape, whole), pl.BlockSpec(gb.shape, whole), pl.BlockSpec(galog.shape, whole),
            pl.BlockSpec(mixw.shape, whole),
        ],
        out_shape=[
            jax.ShapeDtypeStruct((rows, GDN_W), BF16),
            jax.ShapeDtypeStruct((batch, SUBLANES, GDN_CONV), F32),
            jax.ShapeDtypeStruct((batch, GDN_HEADS, GDN_D, GDN_D), F32),
        ],
        out_specs=[
            pl.BlockSpec((tile, GDN_W), row),
            pl.BlockSpec((1, SUBLANES, GDN_CONV), lambda b, c: (b, 0, 0)),
            pl.BlockSpec((1, GDN_HEADS, GDN_D, GDN_D), lambda b, c: (b, 0, 0, 0)),
        ],
        scratch_shapes=[pltpu.VMEM((GDN_CONV // LANES, tile + SUBLANES, LANES), F32),
                        pltpu.VMEM((GDN_CONV // LANES, tile, LANES), F32), pltpu.VMEM((GDN_HEADS, GDN_D, GDN_D), F32)],
        riders=riders)


def _mem_prompt_kernel(q_ref, k_ref, v_ref, z_ref, mixw_ref, y_ref):
    heads = range(MEM_HEADS)
    hs = [slice(h * MEM_D, (h + 1) * MEM_D) for h in heads]
    s = [_dot_nt(q_ref[:, hs[h]].astype(BF16), k_ref[:, hs[h]].astype(BF16)) * (MEM_D ** -0.5) for h in heads]
    e = [jnp.exp(x - jnp.max(x, axis=-1, keepdims=True)) for x in s]
    p = [(x / jnp.sum(x, axis=-1, keepdims=True)).astype(BF16) for x in e]
    o = [_dot(p[h], v_ref[:, hs[h]].astype(BF16)) for h in heads]
    msq = [jnp.sum(x * x, axis=-1, keepdims=True) for x in o]
    for h in heads:
        y_ref[:, hs[h]] = _head_norm_gate(o[h], msq[h], MEM_D, z_ref[:, hs[h]], mixw_ref[:, hs[h]]).astype(BF16)


def _mem_prompt(proj, kv, batch, mixw, tq):
    rows = proj.shape[0]
    nq = rows // batch // tq
    return pl.pallas_call(
        _mem_prompt_kernel,
        grid=(batch, nq),
        in_specs=[
            pl.BlockSpec((tq, MEM_W), lambda b, i: (b * nq + i, COL_QMEM // MEM_W)),
            pl.BlockSpec((MEM_TOKENS, MEM_W), lambda b, i: (b, 0)),
            pl.BlockSpec((MEM_TOKENS, MEM_W), lambda b, i: (b, 1)),
            pl.BlockSpec((tq, MEM_W), lambda b, i: (b * nq + i, (COL_Z + SSD_W + GDN_W) // MEM_W)),
            pl.BlockSpec(mixw.shape, lambda b, i: (0, 0)),
        ],
        out_specs=pl.BlockSpec((tq, MEM_W), lambda b, i: (b * nq + i, 0)),
        out_shape=jax.ShapeDtypeStruct((rows, MEM_W), BF16),
        compiler_params=_params("parallel", "parallel"),
        name="mem_prompt",
    )(proj, kv, kv, proj, mixw)


def _conv_step(u, cst_ref, cst_out_ref, cw_ref, bias):
    acc = cw_ref[CONV_K - 1:CONV_K, :] * u
    if bias is not None:
        acc = acc + bias
    for j in range(CONV_K - 1):
        prev = cst_ref[j]
        acc = acc + cw_ref[j:j + 1, :] * prev
        if j > 0:
            cst_out_ref[j - 1] = prev
    cst_out_ref[CONV_K - 2] = u
    return _silu(acc)


def _rows_to_columns(x):
    pad = jnp.zeros((LANES - x.shape[0], x.shape[1]), F32)
    return jnp.concatenate([x, pad], axis=0).T


def _pick_rows(parts):
    rid = lax.broadcasted_iota(jnp.int32, parts[0].shape, 0)
    out = parts[0]
    for i in range(1, len(parts)):
        out = jnp.where(rid == i, parts[i], out)
    return out


def _ssd_decode_step(xbc_ref, sm_ref, z_ref, cst_ref, st_ref, cw_ref, cb_ref, dtb_ref, alog_ref, dexp_ref,
                     mixw_ref, e_ref, en_ref, y_ref, cst_out_ref, st_out_ref):
    xbc = _conv_step(xbc_ref[...], cst_ref, cst_out_ref, cw_ref, cb_ref[...])
    xs = xbc[:, :SSD_W]
    e = e_ref[...]
    dt = _softplus(sm_ref[...] + dtb_ref[...])
    dec = jnp.exp(dt * (-jnp.exp(alog_ref[...])))
    xd_t = _rows_to_columns(xs * _sel_right(dt, e))
    dec_n = _sel_right(dec, en_ref[...])

    groups, rows, per_group = range(SSD_GROUPS), range(xs.shape[0]), SSD_HEADS // SSD_GROUPS
    b_g = [xbc[:, SSD_W + g * SSD_N:SSD_W + (g + 1) * SSD_N] for g in groups]
    c_g = [xbc[:, SSD_W + (SSD_GROUPS + g) * SSD_N:SSD_W + (SSD_GROUPS + g + 1) * SSD_N].astype(BF16) for g in groups]
    hn = {}
    for g in groups:
        for i in rows:
            for h in range(g * per_group, (g + 1) * per_group):
                col = xd_t[h * SSD_P:(h + 1) * SSD_P, i:i + 1]
                hn[i, h] = st_ref[i, h] * dec_n[i:i + 1, h * SSD_N:(h + 1) * SSD_N] + col * b_g[g][i:i + 1, :]
                st_out_ref[i, h] = hn[i, h]
    hg = {(g, i): jnp.concatenate([hn[i, h] for h in range(g * per_group, (g + 1) * per_group)], axis=0).astype(BF16)
          for g in groups for i in rows}
    y_rows = {p: _dot_nt(c_g[p[0]], hg[p]) for p in hg}
    y = jnp.concatenate([_pick_rows([y_rows[g, i] for i in rows]) for g in groups], axis=1) + dexp_ref[...] * xs
    msq = _sel_right(_sel_right_nt(y * y, e, 2), e, 2)
    y_ref[...] = _head_norm_gate(y, msq, SSD_P, z_ref[...], mixw_ref[...])


def _gdn_decode_step(qkv_ref, sm_ref, z_ref, cst_ref, st_ref, cw_ref, gb_ref, galog_ref, mixw_ref, en_ref,
                     y_ref, cst_out_ref, st_out_ref):
    qkv = _conv_step(qkv_ref[...], cst_ref, cst_out_ref, cw_ref, None)
    sm = sm_ref[...]
    beta = _sigmoid(sm)
    eg = jnp.exp(-jnp.exp(galog_ref[...]) * _softplus(sm + gb_ref[...]))
    eg_n = _sel_right(eg, en_ref[...])

    qs, ks = [], []
    for h in range(GDN_HEADS):
        q = qkv[:, h * GDN_D:(h + 1) * GDN_D]
        k = qkv[:, GDN_W + h * GDN_D:GDN_W + (h + 1) * GDN_D]
        qs.append(q * lax.rsqrt(jnp.sum(q * q, axis=-1, keepdims=True) + EPS) * (GDN_D ** -0.5))
        ks.append(k * lax.rsqrt(jnp.sum(k * k, axis=-1, keepdims=True) + EPS))
    k_t = _rows_to_columns(jnp.concatenate(ks, axis=1))

    heads, rows = range(GDN_HEADS), range(sm.shape[0])
    n = len(rows)
    hs = [slice(h * GDN_D, (h + 1) * GDN_D) for h in heads]
    kq_b = [jnp.concatenate([ks[h], qs[h]], axis=0).astype(BF16) for h in heads]
    prod = {(h, i): _dot(kq_b[h], st_ref[i, h].astype(BF16)) for h in heads for i in rows}
    k_s = [_pick_rows([prod[h, i][:n] for i in rows]) for h in heads]
    q_s = [_pick_rows([prod[h, i][n:] for i in rows]) for h in heads]
    eg_h = [eg_n[:, hs[h]] for h in heads]
    v = [qkv[:, 2 * GDN_W + h * GDN_D:2 * GDN_W + (h + 1) * GDN_D] for h in heads]
    v_new = [beta[:, SM_B + h:SM_B + h + 1] * (v[h] - eg_h[h] * k_s[h]) for h in heads]
    o = [eg_h[h] * q_s[h] + jnp.sum(qs[h] * ks[h], axis=-1, keepdims=True) * v_new[h] for h in heads]
    for h in heads:
        for i in rows:
            st_out_ref[i, h] = st_ref[i, h] * eg_h[h][i:i + 1, :] + k_t[hs[h], i:i + 1] * v_new[h][i:i + 1, :]
    msq = [jnp.sum(x * x, axis=-1, keepdims=True) for x in o]
    for h in heads:
        y_ref[:, hs[h]] = _head_norm_gate(o[h], msq[h], GDN_D, z_ref[:, hs[h]], mixw_ref[:, hs[h]])


def _mem_decode_step(q_ref, k_ref, v_ref, z_ref, mixw_ref, y_ref):
    heads, rows = range(MEM_HEADS), range(q_ref.shape[0])
    hs = [slice(h * MEM_D, (h + 1) * MEM_D) for h in heads]
    win = [pl.ds(h, MEM_TOKENS, stride=MEM_HEADS) for h in heads]
    q = [q_ref[:, hs[h]].astype(BF16) for h in heads]
    s = [_pick_rows([_dot_nt(q[h], k_ref[i, win[h], :].astype(BF16)) for i in rows]) * (MEM_D ** -0.5) for h in heads]
    e = [jnp.exp(x - jnp.max(x, axis=-1, keepdims=True)) for x in s]
    p = [(x / jnp.sum(x, axis=-1, keepdims=True)).astype(BF16) for x in e]
    o = [_pick_rows([_dot(p[h], v_ref[i, win[h], :].astype(BF16)) for i in rows]) for h in heads]
    msq = [jnp.sum(x * x, axis=-1, keepdims=True) for x in o]
    for h in heads:
        y_ref[:, hs[h]] = _head_norm_gate(o[h], msq[h], MEM_D, z_ref[:, hs[h]], mixw_ref[:, hs[h]])


DEC_ROWS = SUBLANES


def _ssd_decode_kernel(proj_ref, sm_ref, cst_ref, st_ref, cw_ref, cb_ref, dtb_ref, alog_ref, dexp_ref, mixw_ref, e_ref, en_ref,
                       y_ref, cst_out_ref, st_out_ref):
    _ssd_decode_step(proj_ref.at[:, COL_XBC:COL_XBC + SSD_CONV], sm_ref, proj_ref.at[:, COL_Z:COL_Z + SSD_W], cst_ref, st_ref,
                     cw_ref, cb_ref, dtb_ref, alog_ref, dexp_ref, mixw_ref.at[:, :SSD_W], e_ref, en_ref,
                     y_ref, cst_out_ref, st_out_ref)


def _gdn_decode_kernel(proj_ref, sm_ref, cst_ref, st_ref, cw_ref, gb_ref, galog_ref, mixw_ref, en_ref,
                       y_ref, cst_out_ref, st_out_ref):
    _gdn_decode_step(proj_ref.at[:, COL_QKV:COL_QKV + GDN_CONV], sm_ref, proj_ref.at[:, COL_Z + SSD_W:COL_Z + SSD_W + GDN_W],
                     cst_ref, st_ref, cw_ref, gb_ref, galog_ref, mixw_ref.at[:, SSD_W:SSD_W + GDN_W], en_ref,
                     y_ref, cst_out_ref, st_out_ref)


def _mem_decode_kernel(proj_ref, k_ref, v_ref, mixw_ref, y_ref):
    _mem_decode_step(proj_ref.at[:, COL_QMEM:COL_QMEM + MEM_W], k_ref, v_ref, proj_ref.at[:, N_MAIN - MEM_W:N_MAIN],
                     mixw_ref.at[:, SSD_W + GDN_W:], y_ref)


def _decode_rider(kernel, per_row_in, consts, out_shape, steps, step_index, phase):
    groups = per_row_in[0].shape[0] // DEC_ROWS
    assert steps % groups == 0
    per_group = steps // groups
    phase = phase % per_group
    group_of = lambda *g: jnp.maximum(step_index(*g) - phase, 0) // per_group

    def group_spec(a):
        axis = 1 if len(a.shape) == 3 and a.shape[0] == CONV_K - 1 else 0
        block = a.shape[:axis] + (DEC_ROWS,) + a.shape[axis + 1:]
        return pl.BlockSpec(block, lambda *g: (0,) * axis + (group_of(*g),) + (0,) * (len(block) - axis - 1))

    whole = lambda a: pl.BlockSpec(a.shape, lambda *g, nd=a.ndim: (0,) * nd)
    return dict(kernel=kernel, arrays=tuple(per_row_in) + tuple(consts), steps_per_group=per_group, phase=phase,
                in_specs=[group_spec(a) for a in per_row_in] + [whole(a) for a in consts],
                out_shape=list(out_shape), out_specs=[group_spec(a) for a in out_shape])


def _scan_with_riders_kernel(*refs, scan_kernel, n_scan_in, n_scan_out, riders):
    scan_in, refs = refs[:n_scan_in], refs[n_scan_in:]
    rider_in = []
    for _, n_in, _, _, _ in riders:
        rider_in.append(refs[:n_in])
        refs = refs[n_in:]
    scan_out, refs = refs[:n_scan_out], refs[n_scan_out:]
    rider_out = []
    for _, _, n_out, _, _ in riders:
        rider_out.append(refs[:n_out])
        refs = refs[n_out:]
    scan_kernel(*scan_in, *scan_out, *refs)
    step = pl.program_id(0) * pl.num_programs(1) + pl.program_id(1)
    for (kernel, _, _, per_group, phase), ins, outs in zip(riders, rider_in, rider_out):
        pl.when(step % per_group == phase)(functools.partial(kernel, *ins, *outs))


def _scan_with_riders(name, grid, scan_kernel, arrays, in_specs, out_shape, out_specs, scratch_shapes, riders):
    body = functools.partial(
        _scan_with_riders_kernel, scan_kernel=scan_kernel, n_scan_in=len(arrays), n_scan_out=len(out_shape),
        riders=tuple((r["kernel"], len(r["arrays"]), len(r["out_shape"]), r["steps_per_group"], r["phase"]) for r in riders))
    outs = pl.pallas_call(
        body,
        grid=grid,
        in_specs=list(in_specs) + [s for r in riders for s in r["in_specs"]],
        out_specs=list(out_specs) + [s for r in riders for s in r["out_specs"]],
        out_shape=list(out_shape) + [s for r in riders for s in r["out_shape"]],
        scratch_shapes=scratch_shapes,
        compiler_params=_params("arbitrary", "arbitrary"),
        name=name,
    )(*arrays, *[a for r in riders for a in r["arrays"]])
    scan_res, outs = outs[:len(out_shape)], outs[len(out_shape):]
    rider_res = []
    for r in riders:
        rider_res.append(outs[:len(r["out_shape"])])
        outs = outs[len(r["out_shape"]):]
    return scan_res, rider_res


IN_DT = SSD_CONV
IN_QKV = IN_DT + SSD_HEADS
IN_B = IN_QKV + GDN_CONV
IN_QMEM = IN_B + 2 * GDN_HEADS
IN_COLS = IN_QMEM + MEM_W + MIX_W
PREP_COLS = 256


def _prep_w_in_kernel(w_ref, x_ref, nw_ref, main_ref, small_ref, o_ref, os_ref, h_ref):
    i = pl.program_id(0)
    nblk = h_ref.shape[0]

    @pl.when(i == 0)
    def _():
        h = _rmsnorm_bf16(x_ref[...], nw_ref[...])
        for kb in range(nblk):
            h_ref[kb] = h[:, kb * PREP_COLS:(kb + 1) * PREP_COLS]
        o_ref[...] = jnp.zeros_like(o_ref)
        os_ref[...] = jnp.zeros_like(os_ref)

    main_ref[COL_QKV:COL_QKV + GDN_CONV, :] = w_ref[IN_QKV:IN_B, :].astype(BF16)
    main_ref[COL_XBC:COL_XBC + SSD_CONV, :] = w_ref[:SSD_CONV, :].astype(BF16)
    main_ref[COL_QMEM:N_MAIN, :] = w_ref[IN_QMEM:IN_COLS, :].astype(BF16)
    small_ref[SM_DT:SM_B, :] = w_ref[IN_DT:IN_QKV, :].astype(BF16)
    small_ref[SM_B:SM_A + GDN_HEADS, :] = w_ref[IN_B:IN_QMEM, :].astype(BF16)
    small_ref[SM_A + GDN_HEADS:, :] = jnp.zeros((LANES - SM_A - GDN_HEADS, w_ref.shape[1]), BF16)

    hb = h_ref[i]
    o_ref[...] += _dot_nt(hb, main_ref[...])
    os_ref[...] += _dot_nt(hb, small_ref[...])


def _prep_w_in(w_t, x, nw):
    n, k = w_t.shape
    rows = x.shape[0]
    assert n == IN_COLS
    whole = lambda i: (0, 0)
    return pl.pallas_call(
        _prep_w_in_kernel,
        grid=(k // PREP_COLS,),
        in_specs=[pl.BlockSpec((IN_COLS, PREP_COLS), lambda i: (0, i)), pl.BlockSpec((rows, k), whole),
                  pl.BlockSpec((1, k), whole)],
        out_specs=[pl.BlockSpec((N_MAIN, PREP_COLS), lambda i: (0, i)), pl.BlockSpec((LANES, PREP_COLS), lambda i: (0, i)),
                   pl.BlockSpec((rows, N_MAIN), whole), pl.BlockSpec((rows, LANES), whole)],
        out_shape=[jax.ShapeDtypeStruct((N_MAIN, k), BF16), jax.ShapeDtypeStruct((LANES, k), BF16),
                   jax.ShapeDtypeStruct((rows, N_MAIN), F32), jax.ShapeDtypeStruct((rows, LANES), F32)],
        scratch_shapes=[pltpu.VMEM((k // PREP_COLS, rows, PREP_COLS), BF16)],
        compiler_params=_params("arbitrary"),
        name="prep_w_in",
    )(w_t, x, nw)


def _head_expander(heads, first_lane, width):
    m = np.zeros((LANES, heads * width), np.float32)
    for h in range(heads):
        m[first_lane + h, h * width:(h + 1) * width] = 1.0
    return jnp.asarray(m, BF16)


def _lane_row(vec, first_lane):
    return jnp.zeros((1, LANES), F32).at[0, first_lane:first_lane + vec.shape[0]].set(vec.astype(F32))


def kernel(x_prompt, x_sample, mem_prompt, state_ssd_conv, state_ssd, state_gdn_conv, state_gdn, cache_mem_k, cache_mem_v, norm_w, w_in, ssd_conv_w, ssd_conv_b, ssd_dt_bias, ssd_A_log, ssd_D, gdn_conv_w, gdn_dt_bias, gdn_A_log, mem_norm_w, w_mem_kv, mix_norm_w, w_out, final_norm_w):
    bp, seq, d = x_prompt.shape
    bs = x_sample.shape[0]
    assert (d, seq % CHUNK, norm_w.shape[0]) == (D_MODEL, 0, 1)

    nw = norm_w[0][None, :]
    xs = x_sample.reshape(bs, d)
    w_main, w_small, proj_s, small_s = _prep_w_in(w_in[0].T, xs, nw)
    wo = w_out[0].astype(BF16)
    mixw = mix_norm_w[0][None, :]
    mixw1, mixw2, mixw3 = mixw[:, :SSD_W], mixw[:, SSD_W:SSD_W + GDN_W], mixw[:, SSD_W + GDN_W:]
    fw = final_norm_w[None, :]
    ssd_dtb = _lane_row(ssd_dt_bias[0], SM_DT)
    ssd_alog = _lane_row(ssd_A_log[0], SM_DT)
    ssd_dexp = jnp.repeat(ssd_D[0].astype(F32), SSD_P)[None, :]
    gdn_b = _lane_row(gdn_dt_bias[0], SM_A)
    gdn_alog = _lane_row(gdn_A_log[0], SM_A)
    e_ssd = _head_expander(SSD_HEADS, SM_DT, SSD_P)
    e_ssd_n = _head_expander(SSD_HEADS, SM_DT, SSD_N)
    e_gdn_n = _head_expander(GDN_HEADS, SM_A, GDN_D)
    ssd_cw, ssd_cb, gdn_cw = ssd_conv_w[0], ssd_conv_b[0][None, :], gdn_conv_w[0]

    xp = x_prompt.reshape(bp * seq, d)
    proj_p, small_p = _norm_matmul(xp, nw, w_main, w_small, _row_tile(bp * seq, PROJ_ROWS), SSD_CONV)
    kv = _mem_kv(mem_prompt.reshape(bp * MEM_TOKENS, d), mem_norm_w[0][None, :], w_mem_kv[0].astype(BF16), MEM_TOKENS)
    scan_rows = SCAN_ROWS if seq % SCAN_ROWS == 0 else CHUNK
    scan_steps = bp * seq // scan_rows
    step_index = lambda b, c: b * (seq // scan_rows) + c

    ssd_cst, gdn_cst = jnp.swapaxes(state_ssd_conv[0], 0, 1), jnp.swapaxes(state_gdn_conv[0], 0, 1)
    mem_k = cache_mem_k.reshape(bs, MEM_TOKENS * MEM_HEADS, MEM_D)
    mem_v = cache_mem_v.reshape(bs, MEM_TOKENS * MEM_HEADS, MEM_D)
    row_out = lambda width: jax.ShapeDtypeStruct((bs, width), F32)
    like = lambda a: jax.ShapeDtypeStruct(a.shape, F32)
    ssd_rider = _decode_rider(
        _ssd_decode_kernel, (proj_s, small_s, ssd_cst, state_ssd[0]),
        (ssd_cw, ssd_cb, ssd_dtb, ssd_alog, ssd_dexp, mixw, e_ssd, e_ssd_n),
        (row_out(SSD_W), like(ssd_cst), like(state_ssd[0])), scan_steps, step_index, phase=0)
    mem_rider = _decode_rider(
        _mem_decode_kernel, (proj_s, mem_k, mem_v), (mixw,), (row_out(MEM_W),), scan_steps, step_index, phase=1)
    gdn_rider = _decode_rider(
        _gdn_decode_kernel, (proj_s, small_s, gdn_cst, state_gdn[0]), (gdn_cw, gdn_b, gdn_alog, mixw, e_gdn_n),
        (row_out(GDN_W), like(gdn_cst), like(state_gdn[0])), scan_steps, step_index, phase=0)

    (y_ssd, tail_ssd, p_ssd), ((ys_ssd, s_ssd_conv, s_ssd), (ys_mem,)) = _ssd_prompt(
        proj_p, small_p, bp, ssd_cw, ssd_cb, ssd_dtb, ssd_alog, ssd_dexp, mixw1, e_ssd, scan_rows, (ssd_rider, mem_rider))
    (y_gdn, tail_gdn, p_gdn), ((ys_gdn, s_gdn_conv, s_gdn),) = _gdn_prompt(
        proj_p, small_p, bp, gdn_cw, gdn_b, gdn_alog, mixw2, scan_rows, (gdn_rider,))
    y_mem = _mem_prompt(proj_p, kv, bp, mixw3, _row_tile(seq, MEM_Q_ROWS))
    y_prompt = _out_proj(y_ssd, y_gdn, y_mem, wo, xp, fw, _row_tile(bp * seq, OUT_ROWS)).reshape(bp, seq, d)
    y_sample = _out_proj(ys_ssd, ys_gdn, ys_mem, wo, xs, fw, bs).reshape(bs, 1, d)

    keep = CONV_K - 1
    mem_shape = (1, bp, MEM_TOKENS, MEM_HEADS, MEM_D)
    return (
        y_prompt, y_sample,
        tail_ssd[None, :, SUBLANES - keep:, :], p_ssd[None],
        tail_gdn[None, :, SUBLANES - keep:, :], p_gdn[None],
        kv[:, :MEM_W].reshape(mem_shape), kv[:, MEM_W:].reshape(mem_shape),
        jnp.swapaxes(s_ssd_conv, 0, 1)[None], s_ssd[None],
        jnp.swapaxes(s_gdn_conv, 0, 1)[None], s_gdn[None],
    )
```

```python
import functools

import numpy as np
import jax
import jax.numpy as jnp
from jax import lax
from jax.experimental import pallas as pl
from jax.experimental.pallas import tpu as pltpu

F32, BF16 = jnp.float32, jnp.bfloat16

D_MODEL = 2048
SSD_HEADS, SSD_P, SSD_GROUPS, SSD_N = 16, 64, 2, 128
SSD_W = SSD_HEADS * SSD_P
SSD_GW = SSD_W // SSD_GROUPS
SSD_CONV = SSD_W + 2 * SSD_GROUPS * SSD_N
GDN_HEADS, GDN_D = 8, 128
GDN_W = GDN_HEADS * GDN_D
GDN_CONV = 3 * GDN_W
MEM_TOKENS, MEM_HEADS, MEM_D = 256, 4, 128
MEM_W = MEM_HEADS * MEM_D
MIX_W = SSD_W + GDN_W + MEM_W
CONV_K = 4
CHUNK = 64
EPS = 1e-6

LANES = 128
SUBLANES = 8
VMEM_LIMIT = 56 * 1024 * 1024
PROJ_ROWS = 1024
OUT_ROWS = 512
MEM_Q_ROWS = 512
SCAN_ROWS = 256
CONV_PHASES = 4

COL_QKV = 0
COL_XBC = COL_QKV + GDN_CONV
COL_QMEM = COL_XBC + SSD_CONV
COL_Z = COL_QMEM + MEM_W
N_MAIN = COL_Z + MIX_W
SM_DT, SM_B, SM_A = 0, SSD_HEADS, SSD_HEADS + GDN_HEADS


def _dot(a, b):
    return jnp.dot(a, b, preferred_element_type=F32)


def _dot_nt(a, b):
    return lax.dot_general(a, b, (((1,), (1,)), ((), ())), preferred_element_type=F32)


def _dot_tn(a, b):
    return lax.dot_general(a, b, (((0,), (0,)), ((), ())), preferred_element_type=F32)


def _split(x, n):
    parts, r = [], x
    for i in range(n):
        p = r.astype(BF16)
        parts.append(p)
        if i + 1 < n:
            r = r - p.astype(F32)
    return parts


def _sel_left(sel, x, n=3):
    return functools.reduce(lambda a, b: a + b, [_dot(sel, p) for p in _split(x, n)])


def _sel_right(x, sel, n=3):
    return functools.reduce(lambda a, b: a + b, [_dot(p, sel) for p in _split(x, n)])


def _sel_right_nt(x, sel, n=3):
    return functools.reduce(lambda a, b: a + b, [_dot_nt(p, sel) for p in _split(x, n)])


def _transpose_sel(x, n=3):
    eye = _eye(LANES).astype(BF16)
    return functools.reduce(lambda a, b: a + b, [_dot_nt(eye, p) for p in _split(x, n)])


def _eye(n):
    return (lax.broadcasted_iota(jnp.int32, (n, n), 0) == lax.broadcasted_iota(jnp.int32, (n, n), 1)).astype(F32)


def _sigmoid(x):
    return 1.0 / (1.0 + jnp.exp(-x))


def _silu(x):
    return x * _sigmoid(x)


def _softplus(x):
    return jnp.maximum(x, 0.0) + jnp.log1p(jnp.exp(-jnp.abs(x)))


def _params(*sem):
    return pltpu.CompilerParams(dimension_semantics=sem, vmem_limit_bytes=VMEM_LIMIT)


def _row_tile(rows, preferred):
    return preferred if rows % preferred == 0 else rows


def _rmsnorm_bf16(x, nw):
    ms = jnp.mean(x * x, axis=-1, keepdims=True)
    return (x * lax.rsqrt(ms + EPS) * nw).astype(BF16)


def _norm_matmul_kernel(x_ref, nw_ref, w_ref, ws_ref, o_ref, os_ref, h_ref):
    @pl.when(pl.program_id(1) == 0)
    def _():
        h = _rmsnorm_bf16(x_ref[...], nw_ref[...])
        h_ref[...] = h
        os_ref[...] = _dot_nt(h, ws_ref[...])

    o_ref[...] = _dot_nt(h_ref[...], w_ref[...])


def _norm_matmul(x, nw, w_t, ws_t, tm, tn):
    m, k = x.shape
    n = w_t.shape[0]
    ns = ws_t.shape[0]
    return pl.pallas_call(
        _norm_matmul_kernel,
        grid=(m // tm, n // tn),
        in_specs=[
            pl.BlockSpec((tm, k), lambda i, j: (i, 0)),
            pl.BlockSpec((1, k), lambda i, j: (0, 0)),
            pl.BlockSpec((tn, k), lambda i, j: (j, 0)),
            pl.BlockSpec((ns, k), lambda i, j: (0, 0)),
        ],
        out_specs=[
            pl.BlockSpec((tm, tn), lambda i, j: (i, j)),
            pl.BlockSpec((tm, ns), lambda i, j: (i, 0)),
        ],
        out_shape=[jax.ShapeDtypeStruct((m, n), F32), jax.ShapeDtypeStruct((m, ns), F32)],
        scratch_shapes=[pltpu.VMEM((tm, k), BF16)],
        compiler_params=_params("parallel", "arbitrary"),
        name="norm_matmul",
    )(x, nw, w_t, ws_t)


def _mem_kv_kernel(x_ref, nw_ref, w_ref, o_ref):
    o_ref[...] = _dot(_rmsnorm_bf16(x_ref[...], nw_ref[...]), w_ref[...])


def _mem_kv(x, nw, w, tm):
    m, k = x.shape
    n = w.shape[1]
    return pl.pallas_call(
        _mem_kv_kernel,
        grid=(m // tm,),
        in_specs=[pl.BlockSpec((tm, k), lambda i: (i, 0)), pl.BlockSpec((1, k), lambda i: (0, 0)),
                  pl.BlockSpec((k, n), lambda i: (0, 0))],
        out_specs=pl.BlockSpec((tm, n), lambda i: (i, 0)),
        out_shape=jax.ShapeDtypeStruct((m, n), F32),
        compiler_params=_params("parallel"),
        name="mem_kv",
    )(x, nw, w)


def _out_proj_kernel(y1_ref, y2_ref, y3_ref, w_ref, x_ref, fw_ref, o_ref):
    n1, n2 = y1_ref.shape[1], y1_ref.shape[1] + y2_ref.shape[1]
    acc = (_dot(y1_ref[...].astype(BF16), w_ref[:n1, :]) + _dot(y2_ref[...].astype(BF16), w_ref[n1:n2, :])
           + _dot(y3_ref[...].astype(BF16), w_ref[n2:, :]))
    r = x_ref[...] + acc
    ms = jnp.mean(r * r, axis=-1, keepdims=True)
    o_ref[...] = r * lax.rsqrt(ms + EPS) * fw_ref[...]


def _out_proj(y1, y2, y3, w, x, fw, tm):
    m, d = x.shape
    row = lambda i: (i, 0)
    whole = lambda i: (0, 0)
    return pl.pallas_call(
        _out_proj_kernel,
        grid=(m // tm,),
        in_specs=[
            pl.BlockSpec((tm, y1.shape[1]), row), pl.BlockSpec((tm, y2.shape[1]), row), pl.BlockSpec((tm, y3.shape[1]), row),
            pl.BlockSpec(w.shape, whole), pl.BlockSpec((tm, d), row), pl.BlockSpec((1, d), whole),
        ],
        out_specs=pl.BlockSpec((tm, d), row),
        out_shape=jax.ShapeDtypeStruct((m, d), F32),
        compiler_params=_params("parallel"),
        name="out_proj",
    )(y1, y2, y3, w, x, fw)


def _causal_conv_tile(u_ref, ubuf_ref, cw_ref, cb_ref, out_ref, tail_ref):
    t, width = u_ref.shape
    n = t // CONV_PHASES
    for s in range(width // LANES):
        cs = slice(s * LANES, (s + 1) * LANES)
        ubuf_ref[s, SUBLANES:SUBLANES + t, :] = u_ref[:, cs]
        taps = {d: ubuf_ref[s, pl.ds(SUBLANES + d, n, stride=CONV_PHASES), :] for d in range(1 - CONV_K, CONV_PHASES)}
        w = [cw_ref[j:j + 1, cs] for j in range(CONV_K)]
        for r in range(CONV_PHASES):
            acc = w[CONV_K - 1] * taps[r]
            if cb_ref is not None:
                acc = acc + cb_ref[:, cs]
            for j in range(CONV_K - 1):
                acc = acc + w[j] * taps[r - (CONV_K - 1) + j]
            out_ref[s, pl.ds(r, n, stride=CONV_PHASES), :] = _silu(acc)
        tail = ubuf_ref[s, t:t + SUBLANES, :]
        ubuf_ref[s, 0:SUBLANES, :] = tail
        tail_ref[0, :, cs] = tail


def _head_norm_gate(y, msq, width, z, mixw):
    return y * lax.rsqrt(msq * (1.0 / width) + EPS) * mixw * _silu(z)


def _ssd_prompt_kernel(xbc_ref, sm_ref, z_ref, cw_ref, cb_ref, dtb_ref, alog_ref, dexp_ref, mixw_ref, e_ref,
                       y_ref, tail_ref, state_ref, ubuf_ref, conv_ref, h_ref):
    c = pl.program_id(1)
    t = xbc_ref.shape[0]
    subs = range(t // CHUNK)
    groups = range(SSD_GROUPS)
    blocks = range(SSD_GW // LANES)

    @pl.when(c == 0)
    def _():
        ubuf_ref[:, 0:SUBLANES, :] = jnp.zeros((SSD_CONV // LANES, SUBLANES, LANES), F32)
        h_ref[...] = jnp.zeros_like(h_ref)

    _causal_conv_tile(xbc_ref, ubuf_ref, cw_ref, cb_ref, conv_ref, tail_ref)
    xs = jnp.concatenate([conv_ref[s] for s in range(SSD_W // LANES)], axis=1)
    e = e_ref[...]
    rows = [slice(j * CHUNK, (j + 1) * CHUNK) for j in subs]
    gs = [slice(g * SSD_GW, (g + 1) * SSD_GW) for g in groups]

    dt = _softplus(sm_ref[...] + dtb_ref[...])
    a = dt * (-jnp.exp(alog_ref[...]))
    rt = lax.broadcasted_iota(jnp.int32, (t, t), 0)
    ct = lax.broadcasted_iota(jnp.int32, (t, t), 1)
    chunk_causal = (rt >= ct) & (rt // CHUNK == ct // CHUNK)
    cum = _sel_left(chunk_causal.astype(BF16), a)
    cum_t = _transpose_sel(cum)
    ecum = jnp.exp(cum)
    wend = jnp.concatenate([jnp.exp(cum[(j + 1) * CHUNK - 1:(j + 1) * CHUNK, :] - cum[rows[j]]) for j in subs], axis=0)
    dt_x = _sel_right(dt, e, 2)
    ecum_x = _sel_right(ecum, e, 2)
    wend_x = _sel_right(wend, e, 2)

    xdt = xs * dt_x
    xdt_b = xdt.astype(BF16)
    xw_b = (xdt * wend_x).astype(BF16)
    lane = lax.broadcasted_iota(jnp.int32, (CHUNK, LANES), 1)
    causal = lax.broadcasted_iota(jnp.int32, (CHUNK, LANES), 0) >= lane % CHUNK
    first_head = lane < SSD_P

    jg = [(j, g) for j in subs for g in groups]
    b_slab, c_slab = SSD_W // LANES, SSD_W // LANES + SSD_GROUPS
    bmat = {(j, g): conv_ref[b_slab + g, rows[j], :].astype(BF16) for j, g in jg}
    cmat = {(j, g): conv_ref[c_slab + g, rows[j], :].astype(BF16) for j, g in jg}
    cb = {p: _dot_nt(cmat[p], jnp.concatenate([bmat[p], bmat[p]], axis=0)) for p in jg}
    inc = {(j, g): _dot_tn(bmat[j, g], xw_b[rows[j], gs[g]]) for j, g in jg}
    intra = {}
    for j, g in jg:
        for blk in blocks:
            a = (g * len(blocks) + blk) * 2
            col = jnp.where(first_head, cum[rows[j], a:a + 1], cum[rows[j], a + 1:a + 2])
            row = jnp.concatenate([cum_t[a:a + 1, rows[j]], cum_t[a + 1:a + 2, rows[j]]], axis=1)
            lmat = jnp.where(causal, jnp.exp(jnp.minimum(col - row, 0.0)), 0.0)
            lanes = slice(g * SSD_GW + blk * LANES, g * SSD_GW + (blk + 1) * LANES)
            intra[j, g, blk] = _dot((cb[j, g] * lmat).astype(BF16), _pair_diag(xdt_b[rows[j], lanes]))

    state = {(0, g): h_ref[:, gs[g]] for g in groups}
    for j in subs:
        last = (j + 1) * CHUNK - 1
        for g in groups:
            state[j + 1, g] = state[j, g] * ecum_x[last:last + 1, gs[g]] + inc[j, g]
    for g in groups:
        h_ref[:, gs[g]] = state[len(subs), g]
    inter = {p: _dot(cmat[p], state[p].astype(BF16)) for p in jg}
    inter_x = jnp.concatenate([jnp.concatenate([inter[j, g] for g in groups], axis=1) for j in subs], axis=0) * ecum_x
    intra_x = jnp.concatenate([jnp.concatenate([intra[j, g, blk] for g in groups for blk in blocks], axis=1)
                               for j in subs], axis=0)
    y = intra_x + inter_x + dexp_ref[...] * xs
    msq = _sel_right(_sel_right_nt(y * y, e, 2), e, 2)
    y_ref[...] = _head_norm_gate(y, msq, SSD_P, z_ref[...], mixw_ref[...]).astype(BF16)

    @pl.when(c == pl.num_programs(1) - 1)
    def _():
        state_ref[0] = h_ref[...].T.reshape(SSD_HEADS, SSD_P, SSD_N)


def _ssd_prompt(proj, small, batch, cw, cb, dtb, alog, dexp, mixw, e, tile, riders):
    rows = proj.shape[0]
    nc = rows // batch // tile
    row = lambda b, c: (b * nc + c, 0)
    whole = lambda b, c: (0, 0)
    return _scan_with_riders(
        "ssd_prompt", (batch, nc), _ssd_prompt_kernel,
        arrays=(proj, small, proj, cw, cb, dtb, alog, dexp, mixw, e),
        in_specs=[
            pl.BlockSpec((tile, SSD_CONV), lambda b, c: (b * nc + c, COL_XBC // SSD_CONV)),
            pl.BlockSpec((tile, LANES), row),
            pl.BlockSpec((tile, SSD_W), lambda b, c: (b * nc + c, COL_Z // SSD_W)),
            pl.BlockSpec(cw.shape, whole), pl.BlockSpec(cb.shape, whole), pl.BlockSpec(dtb.shape, whole),
            pl.BlockSpec(alog.shape, whole), pl.BlockSpec(dexp.shape, whole), pl.BlockSpec(mixw.shape, whole),
            pl.BlockSpec(e.shape, whole),
        ],
        out_shape=[
            jax.ShapeDtypeStruct((rows, SSD_W), BF16),
            jax.ShapeDtypeStruct((batch, SUBLANES, SSD_CONV), F32),
            jax.ShapeDtypeStruct((batch, SSD_HEADS, SSD_P, SSD_N), F32),
        ],
        out_specs=[
            pl.BlockSpec((tile, SSD_W), row),
            pl.BlockSpec((1, SUBLANES, SSD_CONV), lambda b, c: (b, 0, 0)),
            pl.BlockSpec((1, SSD_HEADS, SSD_P, SSD_N), lambda b, c: (b, 0, 0, 0)),
        ],
        scratch_shapes=[pltpu.VMEM((SSD_CONV // LANES, tile + SUBLANES, LANES), F32),
                        pltpu.VMEM((SSD_CONV // LANES, tile, LANES), F32), pltpu.VMEM((SSD_N, SSD_W), F32)],
        riders=riders)


def _unit_lower_inverses(a_stricts, ri, ci):
    t = a_stricts[0].shape[0]
    eye = (ri == ci).astype(F32)
    first = (ri == ci + 1) & (ci % 2 == 0)
    invs = [eye - jnp.where(first, a, 0.0) for a in a_stricts]
    a_bs = [a.astype(BF16) for a in a_stricts]
    zero = jnp.zeros(a_bs[0].shape, BF16)
    s = 2
    while s < t:
        sel = (ri // (2 * s) == ci // (2 * s)) & ((ri // s) % 2 == 1) & ((ci // s) % 2 == 0)
        inv_bs = [inv.astype(BF16) for inv in invs]
        lefts = [_dot(inv_b, _pair_diag(jnp.where(sel, a_b, zero))).astype(BF16) for inv_b, a_b in zip(inv_bs, a_bs)]
        invs = [inv - _dot(left, _pair_diag(inv_b)) for inv, left, inv_b in zip(invs, lefts, inv_bs)]
        s *= 2
    return invs


def _pair_halves(x):
    left = lax.broadcasted_iota(jnp.int32, x.shape, 1) < x.shape[1] // 2
    zero = jnp.zeros_like(x)
    return jnp.where(left, x, zero), jnp.where(left, zero, x)


def _pair_diag(x):
    return jnp.concatenate(_pair_halves(x), axis=0)


def _gdn_prompt_kernel(qkv_ref, sm_ref, z_ref, cw_ref, gb_ref, galog_ref, mixw_ref,
                       y_ref, tail_ref, state_ref, ubuf_ref, conv_ref, s_ref):
    c = pl.program_id(1)
    t = qkv_ref.shape[0]
    subs = range(t // CHUNK)
    heads = range(GDN_HEADS)

    @pl.when(c == 0)
    def _():
        ubuf_ref[:, 0:SUBLANES, :] = jnp.zeros((GDN_CONV // LANES, SUBLANES, LANES), F32)
        s_ref[...] = jnp.zeros_like(s_ref)

    _causal_conv_tile(qkv_ref, ubuf_ref, cw_ref, None, conv_ref, tail_ref)

    sm = sm_ref[...]
    beta = _sigmoid(sm)
    g = -jnp.exp(galog_ref[...]) * _softplus(sm + gb_ref[...])
    rt = lax.broadcasted_iota(jnp.int32, (t, t), 0)
    ct = lax.broadcasted_iota(jnp.int32, (t, t), 1)
    chunk_causal = (rt >= ct) & (rt // CHUNK == ct // CHUNK)
    gc = _sel_left(chunk_causal.astype(BF16), g)
    gc_t = _transpose_sel(gc)
    eg = jnp.exp(gc)
    ri = lax.broadcasted_iota(jnp.int32, (CHUNK, 2 * CHUNK), 0)
    lane = lax.broadcasted_iota(jnp.int32, (CHUNK, 2 * CHUNK), 1)
    ci = lane % CHUNK
    causal = ri >= ci
    strict = ri > ci
    first_head = lane < CHUNK

    rows = [slice(j * CHUNK, (j + 1) * CHUNK) for j in subs]
    hs = [slice(h * GDN_D, (h + 1) * GDN_D) for h in heads]
    la = [SM_A + h for h in heads]
    packs = [(j, a) for j in subs for a in range(0, GDN_HEADS, 2)]
    q, k, kb, vb, kbg, qg = {}, {}, {}, {}, {}, {}
    for h in heads:
        qf, kf, vf = conv_ref[h], conv_ref[GDN_HEADS + h], conv_ref[2 * GDN_HEADS + h]
        qf = qf * lax.rsqrt(jnp.sum(qf * qf, axis=-1, keepdims=True) + EPS) * (GDN_D ** -0.5)
        kf = kf * lax.rsqrt(jnp.sum(kf * kf, axis=-1, keepdims=True) + EPS)
        b_col = beta[:, SM_B + h:SM_B + h + 1]
        eg_col = eg[:, la[h]:la[h] + 1]
        kbf = kf * b_col
        vbf, kbgf, qgf = (vf * b_col).astype(BF16), (kbf * eg_col).astype(BF16), (qf * eg_col).astype(BF16)
        for j in subs:
            q[j, h], k[j, h], kb[j, h] = qf[rows[j]].astype(BF16), kf[rows[j]], kbf[rows[j]].astype(BF16)
            vb[j, h], kbg[j, h], qg[j, h] = vbf[rows[j]], kbgf[rows[j]], qgf[rows[j]]
    decay, kq = {}, {}
    no_keys = jnp.zeros((CHUNK, GDN_D), BF16)
    for j, a in packs:
        b = a + 1
        col = jnp.where(first_head, gc[rows[j], la[a]:la[a] + 1], gc[rows[j], la[b]:la[b] + 1])
        row = jnp.concatenate([gc_t[la[a]:la[a] + 1, rows[j]], gc_t[la[b]:la[b] + 1, rows[j]]], axis=1)
        decay[j, a] = jnp.where(causal, jnp.exp(jnp.minimum(col - row, 0.0)), 0.0)
        kq[j, a] = (_dot_nt(jnp.concatenate([kb[j, a], q[j, a]], axis=0), jnp.concatenate([k[j, a].astype(BF16), no_keys], axis=0))
                    + _dot_nt(jnp.concatenate([kb[j, b], q[j, b]], axis=0), jnp.concatenate([no_keys, k[j, b].astype(BF16)], axis=0)))
    a_strict = [jnp.where(strict, kq[p][:CHUNK] * decay[p], 0.0) for p in packs]
    attn = {p: _pair_halves((kq[p][CHUNK:] * decay[p]).astype(BF16)) for p in packs}
    t_inv = dict(zip(packs, [_pair_halves(x.astype(BF16)) for x in _unit_lower_inverses(a_strict, ri, ci)]))
    u, wk = {}, {}
    for j, a in packs:
        rhs = jnp.concatenate([jnp.concatenate([vb[j, h], kbg[j, h]], axis=1) for h in (a, a + 1)], axis=0)
        for half, h in enumerate((a, a + 1)):
            uw = _dot(t_inv[j, a][half], rhs)
            u[j, h], wk[j, h] = uw[:, :GDN_D], uw[:, GDN_D:].astype(BF16)

    state = [s_ref[h] for h in heads]
    for j in subs:
        g_last = gc[(j + 1) * CHUNK - 1:(j + 1) * CHUNK, :]
        eend = jnp.exp(g_last - gc[rows[j]])
        elast = jnp.exp(g_last)
        s_b = [x.astype(BF16) for x in state]
        v_new = [(u[j, h] - _dot(wk[j, h], s_b[h])).astype(BF16) for h in heads]
        k_end = [(k[j, h] * eend[:, la[h]:la[h] + 1]).astype(BF16) for h in heads]
        s_inc = [_dot_tn(k_end[h], v_new[h]) for h in heads]
        state = [state[h] * elast[:, la[h]:la[h] + 1] + s_inc[h] for h in heads]
        v_pair = {a: jnp.concatenate([v_new[a], v_new[a + 1]], axis=0) for a in range(0, GDN_HEADS, 2)}
        o = [_dot(qg[j, h], s_b[h]) + _dot(attn[j, h - h % 2][h % 2], v_pair[h - h % 2]) for h in heads]
        msq = [jnp.sum(x * x, axis=-1, keepdims=True) for x in o]
        for h in heads:
            y_ref[rows[j], hs[h]] = _head_norm_gate(o[h], msq[h], GDN_D, z_ref[rows[j], hs[h]], mixw_ref[:, hs[h]]).astype(BF16)
    for h in heads:
        s_ref[h] = state[h]

    @pl.when(c == pl.num_programs(1) - 1)
    def _():
        state_ref[0] = s_ref[...]


def _gdn_prompt(proj, small, batch, cw, gb, galog, mixw, tile, riders):
    rows = proj.shape[0]
    nc = rows // batch // tile
    row = lambda b, c: (b * nc + c, 0)
    whole = lambda b, c: (0, 0)
    return _scan_with_riders(
        "gdn_prompt", (batch, nc), _gdn_prompt_kernel,
        arrays=(proj, small, proj, cw, gb, galog, mixw),
        in_specs=[
            pl.BlockSpec((tile, GDN_CONV), lambda b, c: (b * nc + c, COL_QKV // GDN_CONV)),
            pl.BlockSpec((tile, LANES), row),
            pl.BlockSpec((tile, GDN_W), lambda b, c: (b * nc + c, (COL_Z + SSD_W) // GDN_W)),
            pl.BlockSpec(cw.shape, whole), pl.BlockSpec(gb.shape, whole), pl.BlockSpec(galog.shape, whole),
            pl.BlockSpec(mixw.shape, whole),
        ],
        out_shape=[
            jax.ShapeDtypeStruct((rows, GDN_W), BF16),
            jax.ShapeDtypeStruct((batch, SUBLANES, GDN_CONV), F32),
            jax.ShapeDtypeStruct((batch, GDN_HEADS, GDN_D, GDN_D), F32),
        ],
        out_specs=[
            pl.BlockSpec((tile, GDN_W), row),
            pl.BlockSpec((1, SUBLANES, GDN_CONV), lambda b, c: (b, 0, 0)),
            pl.BlockSpec((1, GDN_HEADS, GDN_D, GDN_D), lambda b, c: (b, 0, 0, 0)),
        ],
        scratch_shapes=[pltpu.VMEM((GDN_CONV // LANES, tile + SUBLANES, LANES), F32),
                        pltpu.VMEM((GDN_CONV // LANES, tile, LANES), F32), pltpu.VMEM((GDN_HEADS, GDN_D, GDN_D), F32)],
        riders=riders)


def _mem_prompt_kernel(q_ref, k_ref, v_ref, z_ref, mixw_ref, y_ref):
    heads = range(MEM_HEADS)
    hs = [slice(h * MEM_D, (h + 1) * MEM_D) for h in heads]
    s = [_dot_nt(q_ref[:, hs[h]].astype(BF16), k_ref[:, hs[h]].astype(BF16)) * (MEM_D ** -0.5) for h in heads]
    e = [jnp.exp(x - jnp.max(x, axis=-1, keepdims=True)) for x in s]
    p = [(x / jnp.sum(x, axis=-1, keepdims=True)).astype(BF16) for x in e]
    o = [_dot(p[h], v_ref[:, hs[h]].astype(BF16)) for h in heads]
    msq = [jnp.sum(x * x, axis=-1, keepdims=True) for x in o]
    for h in heads:
        y_ref[:, hs[h]] = _head_norm_gate(o[h], msq[h], MEM_D, z_ref[:, hs[h]], mixw_ref[:, hs[h]]).astype(BF16)


def _mem_prompt(proj, kv, batch, mixw, tq):
    rows = proj.shape[0]
    nq = rows // batch // tq
    return pl.pallas_call(
        _mem_prompt_kernel,
        grid=(batch, nq),
        in_specs=[
            pl.BlockSpec((tq, MEM_W), lambda b, i: (b * nq + i, COL_QMEM // MEM_W)),
            pl.BlockSpec((MEM_TOKENS, MEM_W), lambda b, i: (b, 0)),
            pl.BlockSpec((MEM_TOKENS, MEM_W), lambda b, i: (b, 1)),
            pl.BlockSpec((tq, MEM_W), lambda b, i: (b * nq + i, (COL_Z + SSD_W + GDN_W) // MEM_W)),
            pl.BlockSpec(mixw.shape, lambda b, i: (0, 0)),
        ],
        out_specs=pl.BlockSpec((tq, MEM_W), lambda b, i: (b * nq + i, 0)),
        out_shape=jax.ShapeDtypeStruct((rows, MEM_W), BF16),
        compiler_params=_params("parallel", "parallel"),
        name="mem_prompt",
    )(proj, kv, kv, proj, mixw)


def _conv_step(u, cst_ref, cst_out_ref, cw_ref, bias):
    acc = cw_ref[CONV_K - 1:CONV_K, :] * u
    if bias is not None:
        acc = acc + bias
    for j in range(CONV_K - 1):
        prev = cst_ref[j]
        acc = acc + cw_ref[j:j + 1, :] * prev
        if j > 0:
            cst_out_ref[j - 1] = prev
    cst_out_ref[CONV_K - 2] = u
    return _silu(acc)


def _rows_to_columns(x):
    pad = jnp.zeros((LANES - x.shape[0], x.shape[1]), F32)
    return jnp.concatenate([x, pad], axis=0).T


def _pick_rows(parts):
    rid = lax.broadcasted_iota(jnp.int32, parts[0].shape, 0)
    out = parts[0]
    for i in range(1, len(parts)):
        out = jnp.where(rid == i, parts[i], out)
    return out


def _ssd_decode_step(xbc_ref, sm_ref, z_ref, cst_ref, st_ref, cw_ref, cb_ref, dtb_ref, alog_ref, dexp_ref,
                     mixw_ref, e_ref, en_ref, y_ref, cst_out_ref, st_out_ref):
    xbc = _conv_step(xbc_ref[...], cst_ref, cst_out_ref, cw_ref, cb_ref[...])
    xs = xbc[:, :SSD_W]
    e = e_ref[...]
    dt = _softplus(sm_ref[...] + dtb_ref[...])
    dec = jnp.exp(dt * (-jnp.exp(alog_ref[...])))
    xd_t = _rows_to_columns(xs * _sel_right(dt, e))
    dec_n = _sel_right(dec, en_ref[...])

    groups, rows, per_group = range(SSD_GROUPS), range(xs.shape[0]), SSD_HEADS // SSD_GROUPS
    b_g = [xbc[:, SSD_W + g * SSD_N:SSD_W + (g + 1) * SSD_N] for g in groups]
    c_g = [xbc[:, SSD_W + (SSD_GROUPS + g) * SSD_N:SSD_W + (SSD_GROUPS + g + 1) * SSD_N].astype(BF16) for g in groups]
    hn = {}
    for g in groups:
        for i in rows:
            for h in range(g * per_group, (g + 1) * per_group):
                col = xd_t[h * SSD_P:(h + 1) * SSD_P, i:i + 1]
                hn[i, h] = st_ref[i, h] * dec_n[i:i + 1, h * SSD_N:(h + 1) * SSD_N] + col * b_g[g][i:i + 1, :]
                st_out_ref[i, h] = hn[i, h]
    hg = {(g, i): jnp.concatenate([hn[i, h] for h in range(g * per_group, (g + 1) * per_group)], axis=0).astype(BF16)
          for g in groups for i in rows}
    y_rows = {p: _dot_nt(c_g[p[0]], hg[p]) for p in hg}
    y = jnp.concatenate([_pick_rows([y_rows[g, i] for i in rows]) for g in groups], axis=1) + dexp_ref[...] * xs
    msq = _sel_right(_sel_right_nt(y * y, e, 2), e, 2)
    y_ref[...] = _head_norm_gate(y, msq, SSD_P, z_ref[...], mixw_ref[...])


def _gdn_decode_step(qkv_ref, sm_ref, z_ref, cst_ref, st_ref, cw_ref, gb_ref, galog_ref, mixw_ref, en_ref,
                     y_ref, cst_out_ref, st_out_ref):
    qkv = _conv_step(qkv_ref[...], cst_ref, cst_out_ref, cw_ref, None)
    sm = sm_ref[...]
    beta = _sigmoid(sm)
    eg = jnp.exp(-jnp.exp(galog_ref[...]) * _softplus(sm + gb_ref[...]))
    eg_n = _sel_right(eg, en_ref[...])

    qs, ks = [], []
    for h in range(GDN_HEADS):
        q = qkv[:, h * GDN_D:(h + 1) * GDN_D]
        k = qkv[:, GDN_W + h * GDN_D:GDN_W + (h + 1) * GDN_D]
        qs.append(q * lax.rsqrt(jnp.sum(q * q, axis=-1, keepdims=True) + EPS) * (GDN_D ** -0.5))
        ks.append(k * lax.rsqrt(jnp.sum(k * k, axis=-1, keepdims=True) + EPS))
    k_t = _rows_to_columns(jnp.concatenate(ks, axis=1))

    heads, rows = range(GDN_HEADS), range(sm.shape[0])
    n = len(rows)
    hs = [slice(h * GDN_D, (h + 1) * GDN_D) for h in heads]
    kq_b = [jnp.concatenate([ks[h], qs[h]], axis=0).astype(BF16) for h in heads]
    prod = {(h, i): _dot(kq_b[h], st_ref[i, h].astype(BF16)) for h in heads for i in rows}
    k_s = [_pick_rows([prod[h, i][:n] for i in rows]) for h in heads]
    q_s = [_pick_rows([prod[h, i][n:] for i in rows]) for h in heads]
    eg_h = [eg_n[:, hs[h]] for h in heads]
    v = [qkv[:, 2 * GDN_W + h * GDN_D:2 * GDN_W + (h + 1) * GDN_D] for h in heads]
    v_new = [beta[:, SM_B + h:SM_B + h + 1] * (v[h] - eg_h[h] * k_s[h]) for h in heads]
    o = [eg_h[h] * q_s[h] + jnp.sum(qs[h] * ks[h], axis=-1, keepdims=True) * v_new[h] for h in heads]
    for h in heads:
        for i in rows:
            st_out_ref[i, h] = st_ref[i, h] * eg_h[h][i:i + 1, :] + k_t[hs[h], i:i + 1] * v_new[h][i:i + 1, :]
    msq = [jnp.sum(x * x, axis=-1, keepdims=True) for x in o]
    for h in heads:
        y_ref[:, hs[h]] = _head_norm_gate(o[h], msq[h], GDN_D, z_ref[:, hs[h]], mixw_ref[:, hs[h]])


def _mem_decode_step(q_ref, k_ref, v_ref, z_ref, mixw_ref, y_ref):
    heads, rows = range(MEM_HEADS), range(q_ref.shape[0])
    hs = [slice(h * MEM_D, (h + 1) * MEM_D) for h in heads]
    win = [pl.ds(h, MEM_TOKENS, stride=MEM_HEADS) for h in heads]
    q = [q_ref[:, hs[h]].astype(BF16) for h in heads]
    s = [_pick_rows([_dot_nt(q[h], k_ref[i, win[h], :].astype(BF16)) for i in rows]) * (MEM_D ** -0.5) for h in heads]
    e = [jnp.exp(x - jnp.max(x, axis=-1, keepdims=True)) for x in s]
    p = [(x / jnp.sum(x, axis=-1, keepdims=True)).astype(BF16) for x in e]
    o = [_pick_rows([_dot(p[h], v_ref[i, win[h], :].astype(BF16)) for i in rows]) for h in heads]
    msq = [jnp.sum(x * x, axis=-1, keepdims=True) for x in o]
    for h in heads:
        y_ref[:, hs[h]] = _head_norm_gate(o[h], msq[h], MEM_D, z_ref[:, hs[h]], mixw_ref[:, hs[h]])


DEC_ROWS = SUBLANES


def _ssd_decode_kernel(proj_ref, sm_ref, cst_ref, st_ref, cw_ref, cb_ref, dtb_ref, alog_ref, dexp_ref, mixw_ref, e_ref, en_ref,
                       y_ref, cst_out_ref, st_out_ref):
    _ssd_decode_step(proj_ref.at[:, COL_XBC:COL_XBC + SSD_CONV], sm_ref, proj_ref.at[:, COL_Z:COL_Z + SSD_W], cst_ref, st_ref,
                     cw_ref, cb_ref, dtb_ref, alog_ref, dexp_ref, mixw_ref.at[:, :SSD_W], e_ref, en_ref,
                     y_ref, cst_out_ref, st_out_ref)


def _gdn_decode_kernel(proj_ref, sm_ref, cst_ref, st_ref, cw_ref, gb_ref, galog_ref, mixw_ref, en_ref,
                       y_ref, cst_out_ref, st_out_ref):
    _gdn_decode_step(proj_ref.at[:, COL_QKV:COL_QKV + GDN_CONV], sm_ref, proj_ref.at[:, COL_Z + SSD_W:COL_Z + SSD_W + GDN_W],
                     cst_ref, st_ref, cw_ref, gb_ref, galog_ref, mixw_ref.at[:, SSD_W:SSD_W + GDN_W], en_ref,
                     y_ref, cst_out_ref, st_out_ref)


def _ssd_mem_decode_kernel(proj_ref, sm_ref, cst_ref, st_ref, k_ref, v_ref, cw_ref, cb_ref, dtb_ref, alog_ref, dexp_ref,
                           mixw_ref, e_ref, en_ref, y_ssd_ref, y_mem_ref, cst_out_ref, st_out_ref):
    _mem_decode_step(proj_ref.at[:, COL_QMEM:COL_QMEM + MEM_W], k_ref, v_ref, proj_ref.at[:, N_MAIN - MEM_W:N_MAIN],
                     mixw_ref.at[:, SSD_W + GDN_W:], y_mem_ref)
    _ssd_decode_kernel(proj_ref, sm_ref, cst_ref, st_ref, cw_ref, cb_ref, dtb_ref, alog_ref, dexp_ref, mixw_ref, e_ref, en_ref,
                       y_ssd_ref, cst_out_ref, st_out_ref)


def _decode_rider(kernel, per_row_in, consts, out_shape, steps, step_index, early=()):
    groups = per_row_in[0].shape[0] // DEC_ROWS
    assert steps % groups == 0
    per_group = steps // groups

    def group_spec(a, lead=0):
        axis = 1 if len(a.shape) == 3 and a.shape[0] == CONV_K - 1 else 0
        block = a.shape[:axis] + (DEC_ROWS,) + a.shape[axis + 1:]
        group_of = lambda *g: jnp.minimum((step_index(*g) + lead) // per_group, groups - 1)
        return pl.BlockSpec(block, lambda *g: (0,) * axis + (group_of(*g),) + (0,) * (len(block) - axis - 1))

    whole = lambda a: pl.BlockSpec(a.shape, lambda *g, nd=a.ndim: (0,) * nd)
    lead = lambda n: min(1, per_group - 1) if n in early else 0
    return dict(kernel=kernel, arrays=tuple(per_row_in) + tuple(consts), steps_per_group=per_group,
                in_specs=[group_spec(a, lead(n)) for n, a in enumerate(per_row_in)] + [whole(a) for a in consts],
                out_shape=list(out_shape), out_specs=[group_spec(a) for a in out_shape])


def _scan_with_riders_kernel(*refs, scan_kernel, n_scan_in, n_scan_out, riders):
    scan_in, refs = refs[:n_scan_in], refs[n_scan_in:]
    rider_in = []
    for _, n_in, _, _ in riders:
        rider_in.append(refs[:n_in])
        refs = refs[n_in:]
    scan_out, refs = refs[:n_scan_out], refs[n_scan_out:]
    rider_out = []
    for _, _, n_out, _ in riders:
        rider_out.append(refs[:n_out])
        refs = refs[n_out:]
    scan_kernel(*scan_in, *scan_out, *refs)
    step = pl.program_id(0) * pl.num_programs(1) + pl.program_id(1)
    for (kernel, _, _, per_group), ins, outs in zip(riders, rider_in, rider_out):
        pl.when(step % per_group == 0)(functools.partial(kernel, *ins, *outs))


def _scan_with_riders(name, grid, scan_kernel, arrays, in_specs, out_shape, out_specs, scratch_shapes, riders):
    body = functools.partial(
        _scan_with_riders_kernel, scan_kernel=scan_kernel, n_scan_in=len(arrays), n_scan_out=len(out_shape),
        riders=tuple((r["kernel"], len(r["arrays"]), len(r["out_shape"]), r["steps_per_group"]) for r in riders))
    outs = pl.pallas_call(
        body,
        grid=grid,
        in_specs=list(in_specs) + [s for r in riders for s in r["in_specs"]],
        out_specs=list(out_specs) + [s for r in riders for s in r["out_specs"]],
        out_shape=list(out_shape) + [s for r in riders for s in r["out_shape"]],
        scratch_shapes=scratch_shapes,
        compiler_params=_params("arbitrary", "arbitrary"),
        name=name,
    )(*arrays, *[a for r in riders for a in r["arrays"]])
    scan_res, outs = outs[:len(out_shape)], outs[len(out_shape):]
    rider_res = []
    for r in riders:
        rider_res.append(outs[:len(r["out_shape"])])
        outs = outs[len(r["out_shape"]):]
    return scan_res, rider_res


IN_DT = SSD_CONV
IN_QKV = IN_DT + SSD_HEADS
IN_B = IN_QKV + GDN_CONV
IN_QMEM = IN_B + 2 * GDN_HEADS
IN_COLS = IN_QMEM + MEM_W + MIX_W
PREP_COLS = 256


def _prep_w_in_kernel(w_ref, x_ref, nw_ref, main_ref, small_ref, o_ref, os_ref, h_ref):
    i = pl.program_id(0)
    nblk = h_ref.shape[0]

    @pl.when(i == 0)
    def _():
        h = _rmsnorm_bf16(x_ref[...], nw_ref[...])
        for kb in range(nblk):
            h_ref[kb] = h[:, kb * PREP_COLS:(kb + 1) * PREP_COLS]
        o_ref[...] = jnp.zeros_like(o_ref)
        os_ref[...] = jnp.zeros_like(os_ref)

    main_ref[COL_QKV:COL_QKV + GDN_CONV, :] = w_ref[IN_QKV:IN_B, :].astype(BF16)
    main_ref[COL_XBC:COL_XBC + SSD_CONV, :] = w_ref[:SSD_CONV, :].astype(BF16)
    main_ref[COL_QMEM:N_MAIN, :] = w_ref[IN_QMEM:IN_COLS, :].astype(BF16)
    small_ref[SM_DT:SM_B, :] = w_ref[IN_DT:IN_QKV, :].astype(BF16)
    small_ref[SM_B:SM_A + GDN_HEADS, :] = w_ref[IN_B:IN_QMEM, :].astype(BF16)
    small_ref[SM_A + GDN_HEADS:, :] = jnp.zeros((LANES - SM_A - GDN_HEADS, w_ref.shape[1]), BF16)

    hb = h_ref[i]
    o_ref[...] += _dot_nt(hb, main_ref[...])
    os_ref[...] += _dot_nt(hb, small_ref[...])


def _prep_w_in(w_t, x, nw):
    n, k = w_t.shape
    rows = x.shape[0]
    assert n == IN_COLS
    whole = lambda i: (0, 0)
    return pl.pallas_call(
        _prep_w_in_kernel,
        grid=(k // PREP_COLS,),
        in_specs=[pl.BlockSpec((IN_COLS, PREP_COLS), lambda i: (0, i)), pl.BlockSpec((rows, k), whole),
                  pl.BlockSpec((1, k), whole)],
        out_specs=[pl.BlockSpec((N_MAIN, PREP_COLS), lambda i: (0, i)), pl.BlockSpec((LANES, PREP_COLS), lambda i: (0, i)),
                   pl.BlockSpec((rows, N_MAIN), whole), pl.BlockSpec((rows, LANES), whole)],
        out_shape=[jax.ShapeDtypeStruct((N_MAIN, k), BF16), jax.ShapeDtypeStruct((LANES, k), BF16),
                   jax.ShapeDtypeStruct((rows, N_MAIN), F32), jax.ShapeDtypeStruct((rows, LANES), F32)],
        scratch_shapes=[pltpu.VMEM((k // PREP_COLS, rows, PREP_COLS), BF16)],
        compiler_params=_params("arbitrary"),
        name="prep_w_in",
    )(w_t, x, nw)


def _head_expander(heads, first_lane, width):
    m = np.zeros((LANES, heads * width), np.float32)
    for h in range(heads):
        m[first_lane + h, h * width:(h + 1) * width] = 1.0
    return jnp.asarray(m, BF16)


def _lane_row(vec, first_lane):
    return jnp.zeros((1, LANES), F32).at[0, first_lane:first_lane + vec.shape[0]].set(vec.astype(F32))


def kernel(x_prompt, x_sample, mem_prompt, state_ssd_conv, state_ssd, state_gdn_conv, state_gdn, cache_mem_k, cache_mem_v, norm_w, w_in, ssd_conv_w, ssd_conv_b, ssd_dt_bias, ssd_A_log, ssd_D, gdn_conv_w, gdn_dt_bias, gdn_A_log, mem_norm_w, w_mem_kv, mix_norm_w, w_out, final_norm_w):
    bp, seq, d = x_prompt.shape
    bs = x_sample.shape[0]
    assert (d, seq % CHUNK, norm_w.shape[0]) == (D_MODEL, 0, 1)

    nw = norm_w[0][None, :]
    xs = x_sample.reshape(bs, d)
    w_main, w_small, proj_s, small_s = _prep_w_in(w_in[0].T, xs, nw)
    wo = w_out[0].astype(BF16)
    mixw = mix_norm_w[0][None, :]
    mixw1, mixw2, mixw3 = mixw[:, :SSD_W], mixw[:, SSD_W:SSD_W + GDN_W], mixw[:, SSD_W + GDN_W:]
    fw = final_norm_w[None, :]
    ssd_dtb = _lane_row(ssd_dt_bias[0], SM_DT)
    ssd_alog = _lane_row(ssd_A_log[0], SM_DT)
    ssd_dexp = jnp.repeat(ssd_D[0].astype(F32), SSD_P)[None, :]
    gdn_b = _lane_row(gdn_dt_bias[0], SM_A)
    gdn_alog = _lane_row(gdn_A_log[0], SM_A)
    e_ssd = _head_expander(SSD_HEADS, SM_DT, SSD_P)
    e_ssd_n = _head_expander(SSD_HEADS, SM_DT, SSD_N)
    e_gdn_n = _head_expander(GDN_HEADS, SM_A, GDN_D)
    ssd_cw, ssd_cb, gdn_cw = ssd_conv_w[0], ssd_conv_b[0][None, :], gdn_conv_w[0]

    xp = x_prompt.reshape(bp * seq, d)
    proj_p, small_p = _norm_matmul(xp, nw, w_main, w_small, _row_tile(bp * seq, PROJ_ROWS), SSD_CONV)
    kv = _mem_kv(mem_prompt.reshape(bp * MEM_TOKENS, d), mem_norm_w[0][None, :], w_mem_kv[0].astype(BF16), MEM_TOKENS)
    scan_rows = SCAN_ROWS if seq % SCAN_ROWS == 0 else CHUNK
    scan_steps = bp * seq // scan_rows
    step_index = lambda b, c: b * (seq // scan_rows) + c

    ssd_cst, gdn_cst = jnp.swapaxes(state_ssd_conv[0], 0, 1), jnp.swapaxes(state_gdn_conv[0], 0, 1)
    mem_k = cache_mem_k.reshape(bs, MEM_TOKENS * MEM_HEADS, MEM_D)
    mem_v = cache_mem_v.reshape(bs, MEM_TOKENS * MEM_HEADS, MEM_D)
    row_out = lambda width: jax.ShapeDtypeStruct((bs, width), F32)
    like = lambda a: jax.ShapeDtypeStruct(a.shape, F32)
    ssd_rider = _decode_rider(
        _ssd_mem_decode_kernel, (proj_s, small_s, ssd_cst, state_ssd[0], mem_k, mem_v),
        (ssd_cw, ssd_cb, ssd_dtb, ssd_alog, ssd_dexp, mixw, e_ssd, e_ssd_n),
        (row_out(SSD_W), row_out(MEM_W), like(ssd_cst), like(state_ssd[0])), scan_steps, step_index, early=(4, 5))
    gdn_rider = _decode_rider(
        _gdn_decode_kernel, (proj_s, small_s, gdn_cst, state_gdn[0]), (gdn_cw, gdn_b, gdn_alog, mixw, e_gdn_n),
        (row_out(GDN_W), like(gdn_cst), like(state_gdn[0])), scan_steps, step_index)

    (y_ssd, tail_ssd, p_ssd), ((ys_ssd, ys_mem, s_ssd_conv, s_ssd),) = _ssd_prompt(
        proj_p, small_p, bp, ssd_cw, ssd_cb, ssd_dtb, ssd_alog, ssd_dexp, mixw1, e_ssd, scan_rows, (ssd_rider,))
    (y_gdn, tail_gdn, p_gdn), ((ys_gdn, s_gdn_conv, s_gdn),) = _gdn_prompt(
        proj_p, small_p, bp, gdn_cw, gdn_b, gdn_alog, mixw2, scan_rows, (gdn_rider,))
    y_mem = _mem_prompt(proj_p, kv, bp, mixw3, _row_tile(seq, MEM_Q_ROWS))
    y_prompt = _out_proj(y_ssd, y_gdn, y_mem, wo, xp, fw, _row_tile(bp * seq, OUT_ROWS)).reshape(bp, seq, d)
    y_sample = _out_proj(ys_ssd, ys_gdn, ys_mem, wo, xs, fw, bs).reshape(bs, 1, d)

    keep = CONV_K - 1
    mem_shape = (1, bp, MEM_TOKENS, MEM_HEADS, MEM_D)
    return (
        y_prompt, y_sample,
        tail_ssd[None, :, SUBLANES - keep:, :], p_ssd[None],
        tail_gdn[None, :, SUBLANES - keep:, :], p_gdn[None],
        kv[:, :MEM_W].reshape(mem_shape), kv[:, MEM_W:].reshape(mem_shape),
        jnp.swapaxes(s_ssd_conv, 0, 1)[None], s_ssd[None],
        jnp.swapaxes(s_gdn_conv, 0, 1)[None], s_gdn[None],
    )
```

```python
import functools

import numpy as np
import jax
import jax.numpy as jnp
from jax import lax
from jax.experimental import pallas as pl
from jax.experimental.pallas import tpu as pltpu

F32, BF16 = jnp.float32, jnp.bfloat16

D_MODEL = 2048
SSD_HEADS, SSD_P, SSD_GROUPS, SSD_N = 16, 64, 2, 128
SSD_W = SSD_HEADS * SSD_P
SSD_GW = SSD_W // SSD_GROUPS
SSD_CONV = SSD_W + 2 * SSD_GROUPS * SSD_N
GDN_HEADS, GDN_D = 8, 128
GDN_W = GDN_HEADS * GDN_D
GDN_CONV = 3 * GDN_W
MEM_TOKENS, MEM_HEADS, MEM_D = 256, 4, 128
MEM_W = MEM_HEADS * MEM_D
MIX_W = SSD_W + GDN_W + MEM_W
CONV_K = 4
CHUNK = 64
EPS = 1e-6

LANES = 128
SUBLANES = 8
VMEM_LIMIT = 60 * 1024 * 1024
PROJ_ROWS = 1024
OUT_ROWS = 512
MEM_Q_ROWS = 512
SCAN_ROWS = 256
CONV_PHASES = 4

COL_QKV = 0
COL_XBC = COL_QKV + GDN_CONV
COL_QMEM = COL_XBC + SSD_CONV
COL_Z = COL_QMEM + MEM_W
N_MAIN = COL_Z + MIX_W
SM_DT, SM_B, SM_A = 0, SSD_HEADS, SSD_HEADS + GDN_HEADS


def _dot(a, b):
    return jnp.dot(a, b, preferred_element_type=F32)


def _dot_nt(a, b):
    return lax.dot_general(a, b, (((1,), (1,)), ((), ())), preferred_element_type=F32)


def _dot_tn(a, b):
    return lax.dot_general(a, b, (((0,), (0,)), ((), ())), preferred_element_type=F32)


def _split(x, n):
    parts, r = [], x
    for i in range(n):
        p = r.astype(BF16)
        parts.append(p)
        if i + 1 < n:
            r = r - p.astype(F32)
    return parts


def _sel_left(sel, x, n=3):
    return functools.reduce(lambda a, b: a + b, [_dot(sel, p) for p in _split(x, n)])


def _sel_right(x, sel, n=3):
    return functools.reduce(lambda a, b: a + b, [_dot(p, sel) for p in _split(x, n)])


def _sel_right_nt(x, sel, n=3):
    return functools.reduce(lambda a, b: a + b, [_dot_nt(p, sel) for p in _split(x, n)])


def _transpose_sel(x, n=3):
    eye = _eye(LANES).astype(BF16)
    return functools.reduce(lambda a, b: a + b, [_dot_nt(eye, p) for p in _split(x, n)])


def _eye(n):
    return (lax.broadcasted_iota(jnp.int32, (n, n), 0) == lax.broadcasted_iota(jnp.int32, (n, n), 1)).astype(F32)


def _sigmoid(x):
    return 1.0 / (1.0 + jnp.exp(-x))


def _silu(x):
    return x * _sigmoid(x)


def _softplus(x):
    return jnp.maximum(x, 0.0) + jnp.log1p(jnp.exp(-jnp.abs(x)))


def _params(*sem):
    return pltpu.CompilerParams(dimension_semantics=sem, vmem_limit_bytes=VMEM_LIMIT)


def _row_tile(rows, preferred):
    return preferred if rows % preferred == 0 else rows


def _rmsnorm_bf16(x, nw):
    ms = jnp.mean(x * x, axis=-1, keepdims=True)
    return (x * lax.rsqrt(ms + EPS) * nw).astype(BF16)


def _norm_matmul_kernel(x_ref, nw_ref, w_ref, ws_ref, o_ref, os_ref, h_ref):
    @pl.when(pl.program_id(1) == 0)
    def _():
        h = _rmsnorm_bf16(x_ref[...], nw_ref[...])
        h_ref[...] = h
        os_ref[...] = _dot_nt(h, ws_ref[...])

    o_ref[...] = _dot_nt(h_ref[...], w_ref[...])


def _norm_matmul(x, nw, w_t, ws_t, tm, tn):
    m, k = x.shape
    n = w_t.shape[0]
    ns = ws_t.shape[0]
    return pl.pallas_call(
        _norm_matmul_kernel,
        grid=(m // tm, n // tn),
        in_specs=[
            pl.BlockSpec((tm, k), lambda i, j: (i, 0)),
            pl.BlockSpec((1, k), lambda i, j: (0, 0)),
            pl.BlockSpec((tn, k), lambda i, j: (j, 0)),
            pl.BlockSpec((ns, k), lambda i, j: (0, 0)),
        ],
        out_specs=[
            pl.BlockSpec((tm, tn), lambda i, j: (i, j)),
            pl.BlockSpec((tm, ns), lambda i, j: (i, 0)),
        ],
        out_shape=[jax.ShapeDtypeStruct((m, n), F32), jax.ShapeDtypeStruct((m, ns), F32)],
        scratch_shapes=[pltpu.VMEM((tm, k), BF16)],
        compiler_params=_params("parallel", "arbitrary"),
        name="norm_matmul",
    )(x, nw, w_t, ws_t)


def _mem_kv_kernel(x_ref, nw_ref, w_ref, o_ref):
    o_ref[...] = _dot(_rmsnorm_bf16(x_ref[...], nw_ref[...]), w_ref[...])


def _mem_kv(x, nw, w, tm):
    m, k = x.shape
    n = w.shape[1]
    return pl.pallas_call(
        _mem_kv_kernel,
        grid=(m // tm,),
        in_specs=[pl.BlockSpec((tm, k), lambda i: (i, 0)), pl.BlockSpec((1, k), lambda i: (0, 0)),
                  pl.BlockSpec((k, n), lambda i: (0, 0))],
        out_specs=pl.BlockSpec((tm, n), lambda i: (i, 0)),
        out_shape=jax.ShapeDtypeStruct((m, n), F32),
        compiler_params=_params("parallel"),
        name="mem_kv",
    )(x, nw, w)


def _out_proj_kernel(y1_ref, y2_ref, y3_ref, w_ref, x_ref, fw_ref, o_ref):
    n1, n2 = y1_ref.shape[1], y1_ref.shape[1] + y2_ref.shape[1]
    acc = (_dot(y1_ref[...].astype(BF16), w_ref[:n1, :]) + _dot(y2_ref[...].astype(BF16), w_ref[n1:n2, :])
           + _dot(y3_ref[...].astype(BF16), w_ref[n2:, :]))
    r = x_ref[...] + acc
    ms = jnp.mean(r * r, axis=-1, keepdims=True)
    o_ref[...] = r * lax.rsqrt(ms + EPS) * fw_ref[...]


def _out_proj(y1, y2, y3, w, x, fw, tm, riders=()):
    m, d = x.shape
    row = lambda i, _: (i, 0)
    whole = lambda i, _: (0, 0)
    (out,), rider_res = _scan_with_riders(
        "out_proj", (m // tm, 1), _out_proj_kernel,
        arrays=(y1, y2, y3, w, x, fw),
        in_specs=[
            pl.BlockSpec((tm, y1.shape[1]), row), pl.BlockSpec((tm, y2.shape[1]), row), pl.BlockSpec((tm, y3.shape[1]), row),
            pl.BlockSpec(w.shape, whole, pipeline_mode=pl.Buffered(1)), pl.BlockSpec((tm, d), row), pl.BlockSpec((1, d), whole),
        ],
        out_shape=[jax.ShapeDtypeStruct((m, d), F32)],
        out_specs=[pl.BlockSpec((tm, d), row)],
        scratch_shapes=[],
        riders=riders)
    return out, rider_res


def _causal_conv_tile(u_ref, ubuf_ref, cw_ref, cb_ref, out_ref, tail_ref):
    t, width = u_ref.shape
    n = t // CONV_PHASES
    for s in range(width // LANES):
        cs = slice(s * LANES, (s + 1) * LANES)
        ubuf_ref[s, SUBLANES:SUBLANES + t, :] = u_ref[:, cs]
        taps = {d: ubuf_ref[s, pl.ds(SUBLANES + d, n, stride=CONV_PHASES), :] for d in range(1 - CONV_K, CONV_PHASES)}
        w = [cw_ref[j:j + 1, cs] for j in range(CONV_K)]
        for r in range(CONV_PHASES):
            acc = w[CONV_K - 1] * taps[r]
            if cb_ref is not None:
                acc = acc + cb_ref[:, cs]
            for j in range(CONV_K - 1):
                acc = acc + w[j] * taps[r - (CONV_K - 1) + j]
            out_ref[s, pl.ds(r, n, stride=CONV_PHASES), :] = _silu(acc)
        tail = ubuf_ref[s, t:t + SUBLANES, :]
        ubuf_ref[s, 0:SUBLANES, :] = tail
        tail_ref[0, :, cs] = tail


def _head_norm_gate(y, msq, width, z, mixw):
    return y * lax.rsqrt(msq * (1.0 / width) + EPS) * mixw * _silu(z)


def _ssd_prompt_kernel(xbc_ref, sm_ref, z_ref, cw_ref, cb_ref, dtb_ref, alog_ref, dexp_ref, mixw_ref, e_ref,
                       y_ref, tail_ref, state_ref, ubuf_ref, conv_ref, h_ref):
    c = pl.program_id(1)
    t = xbc_ref.shape[0]
    subs = range(t // CHUNK)
    groups = range(SSD_GROUPS)
    blocks = range(SSD_GW // LANES)

    @pl.when(c == 0)
    def _():
        ubuf_ref[:, 0:SUBLANES, :] = jnp.zeros((SSD_CONV // LANES, SUBLANES, LANES), F32)
        h_ref[...] = jnp.zeros_like(h_ref)

    _causal_conv_tile(xbc_ref, ubuf_ref, cw_ref, cb_ref, conv_ref, tail_ref)
    xs = jnp.concatenate([conv_ref[s] for s in range(SSD_W // LANES)], axis=1)
    e = e_ref[...]
    rows = [slice(j * CHUNK, (j + 1) * CHUNK) for j in subs]
    gs = [slice(g * SSD_GW, (g + 1) * SSD_GW) for g in groups]

    dt = _softplus(sm_ref[...] + dtb_ref[...])
    a = dt * (-jnp.exp(alog_ref[...]))
    rt = lax.broadcasted_iota(jnp.int32, (t, t), 0)
    ct = lax.broadcasted_iota(jnp.int32, (t, t), 1)
    chunk_causal = (rt >= ct) & (rt // CHUNK == ct // CHUNK)
    cum = _sel_left(chunk_causal.astype(BF16), a)
    cum_t = _transpose_sel(cum)
    ecum = jnp.exp(cum)
    wend = jnp.concatenate([jnp.exp(cum[(j + 1) * CHUNK - 1:(j + 1) * CHUNK, :] - cum[rows[j]]) for j in subs], axis=0)
    dt_x = _sel_right(dt, e, 2)
    ecum_x = _sel_right(ecum, e, 2)
    wend_x = _sel_right(wend, e, 2)

    xdt = xs * dt_x
    xdt_b = xdt.astype(BF16)
    xw_b = (xdt * wend_x).astype(BF16)
    lane = lax.broadcasted_iota(jnp.int32, (CHUNK, LANES), 1)
    causal = lax.broadcasted_iota(jnp.int32, (CHUNK, LANES), 0) >= lane % CHUNK
    first_head = lane < SSD_P

    jg = [(j, g) for j in subs for g in groups]
    b_slab, c_slab = SSD_W // LANES, SSD_W // LANES + SSD_GROUPS
    bmat = {(j, g): conv_ref[b_slab + g, rows[j], :].astype(BF16) for j, g in jg}
    cmat = {(j, g): conv_ref[c_slab + g, rows[j], :].astype(BF16) for j, g in jg}
    cb = {p: _dot_nt(cmat[p], jnp.concatenate([bmat[p], bmat[p]], axis=0)) for p in jg}
    inc = {(j, g): _dot_tn(bmat[j, g], xw_b[rows[j], gs[g]]) for j, g in jg}
    intra = {}
    for j, g in jg:
        for blk in blocks:
            a = (g * len(blocks) + blk) * 2
            col = jnp.where(first_head, cum[rows[j], a:a + 1], cum[rows[j], a + 1:a + 2])
            row = jnp.concatenate([cum_t[a:a + 1, rows[j]], cum_t[a + 1:a + 2, rows[j]]], axis=1)
            lmat = jnp.where(causal, jnp.exp(jnp.minimum(col - row, 0.0)), 0.0)
            lanes = slice(g * SSD_GW + blk * LANES, g * SSD_GW + (blk + 1) * LANES)
            intra[j, g, blk] = _dot((cb[j, g] * lmat).astype(BF16), _pair_diag(xdt_b[rows[j], lanes]))

    state = {(0, g): h_ref[:, gs[g]] for g in groups}
    for j in subs:
        last = (j + 1) * CHUNK - 1
        for g in groups:
            state[j + 1, g] = state[j, g] * ecum_x[last:last + 1, gs[g]] + inc[j, g]
    for g in groups:
        h_ref[:, gs[g]] = state[len(subs), g]
    inter = {p: _dot(cmat[p], state[p].astype(BF16)) for p in jg}
    inter_x = jnp.concatenate([jnp.concatenate([inter[j, g] for g in groups], axis=1) for j in subs], axis=0) * ecum_x
    intra_x = jnp.concatenate([jnp.concatenate([intra[j, g, blk] for g in groups for blk in blocks], axis=1)
                               for j in subs], axis=0)
    y = intra_x + inter_x + dexp_ref[...] * xs
    msq = _sel_right(_sel_right_nt(y * y, e, 2), e, 2)
    y_ref[...] = _head_norm_gate(y, msq, SSD_P, z_ref[...], mixw_ref[...]).astype(BF16)

    @pl.when(c == pl.num_programs(1) - 1)
    def _():
        state_ref[0] = h_ref[...].T.reshape(SSD_HEADS, SSD_P, SSD_N)


def _ssd_prompt(proj, small, batch, cw, cb, dtb, alog, dexp, mixw, e, tile, riders):
    rows = proj.shape[0]
    nc = rows // batch // tile
    row = lambda b, c: (b * nc + c, 0)
    whole = lambda b, c: (0, 0)
    return _scan_with_riders(
        "ssd_prompt", (batch, nc), _ssd_prompt_kernel,
        arrays=(proj, small, proj, cw, cb, dtb, alog, dexp, mixw, e),
        in_specs=[
            pl.BlockSpec((tile, SSD_CONV), lambda b, c: (b * nc + c, COL_XBC // SSD_CONV)),
            pl.BlockSpec((tile, LANES), row),
            pl.BlockSpec((tile, SSD_W), lambda b, c: (b * nc + c, COL_Z // SSD_W)),
            pl.BlockSpec(cw.shape, whole), pl.BlockSpec(cb.shape, whole), pl.BlockSpec(dtb.shape, whole),
            pl.BlockSpec(alog.shape, whole), pl.BlockSpec(dexp.shape, whole), pl.BlockSpec(mixw.shape, whole),
            pl.BlockSpec(e.shape, whole),
        ],
        out_shape=[
            jax.ShapeDtypeStruct((rows, SSD_W), BF16),
            jax.ShapeDtypeStruct((batch, SUBLANES, SSD_CONV), F32),
            jax.ShapeDtypeStruct((batch, SSD_HEADS, SSD_P, SSD_N), F32),
        ],
        out_specs=[
            pl.BlockSpec((tile, SSD_W), row),
            pl.BlockSpec((1, SUBLANES, SSD_CONV), lambda b, c: (b, 0, 0)),
            pl.BlockSpec((1, SSD_HEADS, SSD_P, SSD_N), lambda b, c: (b, 0, 0, 0)),
        ],
        scratch_shapes=[pltpu.VMEM((SSD_CONV // LANES, tile + SUBLANES, LANES), F32),
                        pltpu.VMEM((SSD_CONV // LANES, tile, LANES), F32), pltpu.VMEM((SSD_N, SSD_W), F32)],
        riders=riders)


def _unit_lower_inverses(a_stricts, ri, ci):
    t = a_stricts[0].shape[0]
    eye = (ri == ci).astype(F32)
    first = (ri == ci + 1) & (ci % 2 == 0)
    invs = [eye - jnp.where(first, a, 0.0) for a in a_stricts]
    a_bs = [a.astype(BF16) for a in a_stricts]
    zero = jnp.zeros(a_bs[0].shape, BF16)
    s = 2
    while s < t:
        sel = (ri // (2 * s) == ci // (2 * s)) & ((ri // s) % 2 == 1) & ((ci // s) % 2 == 0)
        inv_bs = [inv.astype(BF16) for inv in invs]
        lefts = [_dot(inv_b, _pair_diag(jnp.where(sel, a_b, zero))).astype(BF16) for inv_b, a_b in zip(inv_bs, a_bs)]
        invs = [inv - _dot(left, _pair_diag(inv_b)) for inv, left, inv_b in zip(invs, lefts, inv_bs)]
        s *= 2
    return invs


def _pair_halves(x):
    left = lax.broadcasted_iota(jnp.int32, x.shape, 1) < x.shape[1] // 2
    zero = jnp.zeros_like(x)
    return jnp.where(left, x, zero), jnp.where(left, zero, x)


def _pair_diag(x):
    return jnp.concatenate(_pair_halves(x), axis=0)


def _gdn_prompt_kernel(qkv_ref, sm_ref, z_ref, cw_ref, gb_ref, galog_ref, mixw_ref,
                       y_ref, tail_ref, state_ref, ubuf_ref, conv_ref, s_ref):
    c = pl.program_id(1)
    t = qkv_ref.shape[0]
    subs = range(t // CHUNK)
    heads = range(GDN_HEADS)

    @pl.when(c == 0)
    def _():
        ubuf_ref[:, 0:SUBLANES, :] = jnp.zeros((GDN_CONV // LANES, SUBLANES, LANES), F32)
        s_ref[...] = jnp.zeros_like(s_ref)

    _causal_conv_tile(qkv_ref, ubuf_ref, cw_ref, None, conv_ref, tail_ref)

    sm = sm_ref[...]
    beta = _sigmoid(sm)
    g = -jnp.exp(galog_ref[...]) * _softplus(sm + gb_ref[...])
    rt = lax.broadcasted_iota(jnp.int32, (t, t), 0)
    ct = lax.broadcasted_iota(jnp.int32, (t, t), 1)
    chunk_causal = (rt >= ct) & (rt // CHUNK == ct // CHUNK)
    gc = _sel_left(chunk_causal.astype(BF16), g)
    gc_t = _transpose_sel(gc)
    eg = jnp.exp(gc)
    ri = lax.broadcasted_iota(jnp.int32, (CHUNK, 2 * CHUNK), 0)
    lane = lax.broadcasted_iota(jnp.int32, (CHUNK, 2 * CHUNK), 1)
    ci = lane % CHUNK
    causal = ri >= ci
    strict = ri > ci
    first_head = lane < CHUNK

    rows = [slice(j * CHUNK, (j + 1) * CHUNK) for j in subs]
    hs = [slice(h * GDN_D, (h + 1) * GDN_D) for h in heads]
    la = [SM_A + h for h in heads]
    packs = [(j, a) for j in subs for a in range(0, GDN_HEADS, 2)]
    q, k, kb, vb, kbg, qg = {}, {}, {}, {}, {}, {}
    for h in heads:
        qf, kf, vf = conv_ref[h], conv_ref[GDN_HEADS + h], conv_ref[2 * GDN_HEADS + h]
        qf = qf * lax.rsqrt(jnp.sum(qf * qf, axis=-1, keepdims=True) + EPS) * (GDN_D ** -0.5)
        kf = kf * lax.rsqrt(jnp.sum(kf * kf, axis=-1, keepdims=True) + EPS)
        b_col = beta[:, SM_B + h:SM_B + h + 1]
        eg_col = eg[:, la[h]:la[h] + 1]
        kbf = kf * b_col
        vbf, kbgf, qgf = (vf * b_col).astype(BF16), (kbf * eg_col).astype(BF16), (qf * eg_col).astype(BF16)
        for j in subs:
            q[j, h], k[j, h], kb[j, h] = qf[rows[j]].astype(BF16), kf[rows[j]], kbf[rows[j]].astype(BF16)
            vb[j, h], kbg[j, h], qg[j, h] = vbf[rows[j]], kbgf[rows[j]], qgf[rows[j]]
    decay, kq = {}, {}
    no_keys = jnp.zeros((CHUNK, GDN_D), BF16)
    for j, a in packs:
        b = a + 1
        col = jnp.where(first_head, gc[rows[j], la[a]:la[a] + 1], gc[rows[j], la[b]:la[b] + 1])
        row = jnp.concatenate([gc_t[la[a]:la[a] + 1, rows[j]], gc_t[la[b]:la[b] + 1, rows[j]]], axis=1)
        decay[j, a] = jnp.where(causal, jnp.exp(jnp.minimum(col - row, 0.0)), 0.0)
        kq[j, a] = (_dot_nt(jnp.concatenate([kb[j, a], q[j, a]], axis=0), jnp.concatenate([k[j, a].astype(BF16), no_keys], axis=0))
                    + _dot_nt(jnp.concatenate([kb[j, b], q[j, b]], axis=0), jnp.concatenate([no_keys, k[j, b].astype(BF16)], axis=0)))
    a_strict = [jnp.where(strict, kq[p][:CHUNK] * decay[p], 0.0) for p in packs]
    attn = {p: _pair_halves((kq[p][CHUNK:] * decay[p]).astype(BF16)) for p in packs}
    t_inv = dict(zip(packs, [_pair_halves(x.astype(BF16)) for x in _unit_lower_inverses(a_strict, ri, ci)]))
    u, wk = {}, {}
    for j, a in packs:
        rhs = jnp.concatenate([jnp.concatenate([vb[j, h], kbg[j, h]], axis=1) for h in (a, a + 1)], axis=0)
        for half, h in enumerate((a, a + 1)):
            uw = _dot(t_inv[j, a][half], rhs)
            u[j, h], wk[j, h] = uw[:, :GDN_D], uw[:, GDN_D:].astype(BF16)

    state = [s_ref[h] for h in heads]
    for j in subs:
        g_last = gc[(j + 1) * CHUNK - 1:(j + 1) * CHUNK, :]
        eend = jnp.exp(g_last - gc[rows[j]])
        elast = jnp.exp(g_last)
        s_b = [x.astype(BF16) for x in state]
        v_new = [(u[j, h] - _dot(wk[j, h], s_b[h])).astype(BF16) for h in heads]
        k_end = [(k[j, h] * eend[:, la[h]:la[h] + 1]).astype(BF16) for h in heads]
        s_inc = [_dot_tn(k_end[h], v_new[h]) for h in heads]
        state = [state[h] * elast[:, la[h]:la[h] + 1] + s_inc[h] for h in heads]
        v_pair = {a: jnp.concatenate([v_new[a], v_new[a + 1]], axis=0) for a in range(0, GDN_HEADS, 2)}
        o = [_dot(qg[j, h], s_b[h]) + _dot(attn[j, h - h % 2][h % 2], v_pair[h - h % 2]) for h in heads]
        msq = [jnp.sum(x * x, axis=-1, keepdims=True) for x in o]
        for h in heads:
            y_ref[rows[j], hs[h]] = _head_norm_gate(o[h], msq[h], GDN_D, z_ref[rows[j], hs[h]], mixw_ref[:, hs[h]]).astype(BF16)
    for h in heads:
        s_ref[h] = state[h]

    @pl.when(c == pl.num_programs(1) - 1)
    def _():
        state_ref[0] = s_ref[...]


def _gdn_prompt(proj, small, batch, cw, gb, galog, mixw, tile, riders):
    rows = proj.shape[0]
    nc = rows // batch // tile
    row = lambda b, c: (b * nc + c, 0)
    whole = lambda b, c: (0, 0)
    return _scan_with_riders(
        "gdn_prompt", (batch, nc), _gdn_prompt_kernel,
        arrays=(proj, small, proj, cw, gb, galog, mixw),
        in_specs=[
            pl.BlockSpec((tile, GDN_CONV), lambda b, c: (b * nc + c, COL_QKV // GDN_CONV)),
            pl.BlockSpec((tile, LANES), row),
            pl.BlockSpec((tile, GDN_W), lambda b, c: (b * nc + c, (COL_Z + SSD_W) // GDN_W)),
            pl.BlockSpec(cw.shape, whole), pl.BlockSpec(gb.shape, whole), pl.BlockSpec(galog.shape, whole),
            pl.BlockSpec(mixw.shape, whole),
        ],
        out_shape=[
            jax.ShapeDtypeStruct((rows, GDN_W), BF16),
            jax.ShapeDtypeStruct((batch, SUBLANES, GDN_CONV), F32),
            jax.ShapeDtypeStruct((batch, GDN_HEADS, GDN_D, GDN_D), F32),
        ],
        out_specs=[
            pl.BlockSpec((tile, GDN_W), row),
            pl.BlockSpec((1, SUBLANES, GDN_CONV), lambda b, c: (b, 0, 0)),
            pl.BlockSpec((1, GDN_HEADS, GDN_D, GDN_D), lambda b, c: (b, 0, 0, 0)),
        ],
        scratch_shapes=[pltpu.VMEM((GDN_CONV // LANES, tile + SUBLANES, LANES), F32),
                        pltpu.VMEM((GDN_CONV // LANES, tile, LANES), F32), pltpu.VMEM((GDN_HEADS, GDN_D, GDN_D), F32)],
        riders=riders)


def _mem_prompt_kernel(q_ref, k_ref, v_ref, z_ref, mixw_ref, y_ref):
    heads = range(MEM_HEADS)
    hs = [slice(h * MEM_D, (h + 1) * MEM_D) for h in heads]
    s = [_dot_nt(q_ref[:, hs[h]].astype(BF16), k_ref[:, hs[h]].astype(BF16)) * (MEM_D ** -0.5) for h in heads]
    e = [jnp.exp(x - jnp.max(x, axis=-1, keepdims=True)) for x in s]
    p = [(x / jnp.sum(x, axis=-1, keepdims=True)).astype(BF16) for x in e]
    o = [_dot(p[h], v_ref[:, hs[h]].astype(BF16)) for h in heads]
    msq = [jnp.sum(x * x, axis=-1, keepdims=True) for x in o]
    for h in heads:
        y_ref[:, hs[h]] = _head_norm_gate(o[h], msq[h], MEM_D, z_ref[:, hs[h]], mixw_ref[:, hs[h]]).astype(BF16)


def _mem_prompt(proj, kv, batch, mixw, tq):
    rows = proj.shape[0]
    nq = rows // batch // tq
    return pl.pallas_call(
        _mem_prompt_kernel,
        grid=(batch, nq),
        in_specs=[
            pl.BlockSpec((tq, MEM_W), lambda b, i: (b * nq + i, COL_QMEM // MEM_W)),
            pl.BlockSpec((MEM_TOKENS, MEM_W), lambda b, i: (b, 0)),
            pl.BlockSpec((MEM_TOKENS, MEM_W), lambda b, i: (b, 1)),
            pl.BlockSpec((tq, MEM_W), lambda b, i: (b * nq + i, (COL_Z + SSD_W + GDN_W) // MEM_W)),
            pl.BlockSpec(mixw.shape, lambda b, i: (0, 0)),
        ],
        out_specs=pl.BlockSpec((tq, MEM_W), lambda b, i: (b * nq + i, 0)),
        out_shape=jax.ShapeDtypeStruct((rows, MEM_W), BF16),
        compiler_params=_params("parallel", "parallel"),
        name="mem_prompt",
    )(proj, kv, kv, proj, mixw)


def _conv_step(u, cst_ref, cst_out_ref, cw_ref, bias):
    acc = cw_ref[CONV_K - 1:CONV_K, :] * u
    if bias is not None:
        acc = acc + bias
    for j in range(CONV_K - 1):
        prev = cst_ref[j]
        acc = acc + cw_ref[j:j + 1, :] * prev
        if j > 0:
            cst_out_ref[j - 1] = prev
    cst_out_ref[CONV_K - 2] = u
    return _silu(acc)


def _rows_to_columns(x):
    pad = jnp.zeros((LANES - x.shape[0], x.shape[1]), F32)
    return jnp.concatenate([x, pad], axis=0).T


def _pick_rows(parts):
    rid = lax.broadcasted_iota(jnp.int32, parts[0].shape, 0)
    out = parts[0]
    for i in range(1, len(parts)):
        out = jnp.where(rid == i, parts[i], out)
    return out


def _ssd_decode_step(xbc_ref, sm_ref, z_ref, cst_ref, st_ref, cw_ref, cb_ref, dtb_ref, alog_ref, dexp_ref,
                     mixw_ref, e_ref, en_ref, y_ref, cst_out_ref, st_out_ref):
    xbc = _conv_step(xbc_ref[...], cst_ref, cst_out_ref, cw_ref, cb_ref[...])
    xs = xbc[:, :SSD_W]
    e = e_ref[...]
    dt = _softplus(sm_ref[...] + dtb_ref[...])
    dec = jnp.exp(dt * (-jnp.exp(alog_ref[...])))
    xd_t = _rows_to_columns(xs * _sel_right(dt, e))
    dec_n = _sel_right(dec, en_ref[...])

    groups, rows, per_group = range(SSD_GROUPS), range(xs.shape[0]), SSD_HEADS // SSD_GROUPS
    b_g = [xbc[:, SSD_W + g * SSD_N:SSD_W + (g + 1) * SSD_N] for g in groups]
    c_g = [xbc[:, SSD_W + (SSD_GROUPS + g) * SSD_N:SSD_W + (SSD_GROUPS + g + 1) * SSD_N].astype(BF16) for g in groups]
    hn = {}
    for g in groups:
        for i in rows:
            for h in range(g * per_group, (g + 1) * per_group):
                col = xd_t[h * SSD_P:(h + 1) * SSD_P, i:i + 1]
                hn[i, h] = st_ref[i, h] * dec_n[i:i + 1, h * SSD_N:(h + 1) * SSD_N] + col * b_g[g][i:i + 1, :]
                st_out_ref[i, h] = hn[i, h]
    hg = {(g, i): jnp.concatenate([hn[i, h] for h in range(g * per_group, (g + 1) * per_group)], axis=0).astype(BF16)
          for g in groups for i in rows}
    y_rows = {p: _dot_nt(c_g[p[0]], hg[p]) for p in hg}
    y = jnp.concatenate([_pick_rows([y_rows[g, i] for i in rows]) for g in groups], axis=1) + dexp_ref[...] * xs
    msq = _sel_right(_sel_right_nt(y * y, e, 2), e, 2)
    y_ref[...] = _head_norm_gate(y, msq, SSD_P, z_ref[...], mixw_ref[...])


def _gdn_decode_step(qkv_ref, sm_ref, z_ref, cst_ref, st_ref, cw_ref, gb_ref, galog_ref, mixw_ref, en_ref,
                     y_ref, cst_out_ref, st_out_ref):
    qkv = _conv_step(qkv_ref[...], cst_ref, cst_out_ref, cw_ref, None)
    sm = sm_ref[...]
    beta = _sigmoid(sm)
    eg = jnp.exp(-jnp.exp(galog_ref[...]) * _softplus(sm + gb_ref[...]))
    eg_n = _sel_right(eg, en_ref[...])

    qs, ks = [], []
    for h in range(GDN_HEADS):
        q = qkv[:, h * GDN_D:(h + 1) * GDN_D]
        k = qkv[:, GDN_W + h * GDN_D:GDN_W + (h + 1) * GDN_D]
        qs.append(q * lax.rsqrt(jnp.sum(q * q, axis=-1, keepdims=True) + EPS) * (GDN_D ** -0.5))
        ks.append(k * lax.rsqrt(jnp.sum(k * k, axis=-1, keepdims=True) + EPS))
    k_t = _rows_to_columns(jnp.concatenate(ks, axis=1))

    heads, rows = range(GDN_HEADS), range(sm.shape[0])
    n = len(rows)
    hs = [slice(h * GDN_D, (h + 1) * GDN_D) for h in heads]
    kq_b = [jnp.concatenate([ks[h], qs[h]], axis=0).astype(BF16) for h in heads]
    prod = {(h, i): _dot(kq_b[h], st_ref[i, h].astype(BF16)) for h in heads for i in rows}
    k_s = [_pick_rows([prod[h, i][:n] for i in rows]) for h in heads]
    q_s = [_pick_rows([prod[h, i][n:] for i in rows]) for h in heads]
    eg_h = [eg_n[:, hs[h]] for h in heads]
    v = [qkv[:, 2 * GDN_W + h * GDN_D:2 * GDN_W + (h + 1) * GDN_D] for h in heads]
    v_new = [beta[:, SM_B + h:SM_B + h + 1] * (v[h] - eg_h[h] * k_s[h]) for h in heads]
    o = [eg_h[h] * q_s[h] + jnp.sum(qs[h] * ks[h], axis=-1, keepdims=True) * v_new[h] for h in heads]
    for h in heads:
        for i in rows:
            st_out_ref[i, h] = st_ref[i, h] * eg_h[h][i:i + 1, :] + k_t[hs[h], i:i + 1] * v_new[h][i:i + 1, :]
    msq = [jnp.sum(x * x, axis=-1, keepdims=True) for x in o]
    for h in heads:
        y_ref[:, hs[h]] = _head_norm_gate(o[h], msq[h], GDN_D, z_ref[:, hs[h]], mixw_ref[:, hs[h]])


def _mem_decode_step(q_ref, k_ref, v_ref, z_ref, mixw_ref, y_ref):
    heads, rows = range(MEM_HEADS), range(q_ref.shape[0])
    hs = [slice(h * MEM_D, (h + 1) * MEM_D) for h in heads]
    win = [pl.ds(h, MEM_TOKENS, stride=MEM_HEADS) for h in heads]
    q = [q_ref[:, hs[h]].astype(BF16) for h in heads]
    s = [_pick_rows([_dot_nt(q[h], k_ref[i, win[h], :].astype(BF16)) for i in rows]) * (MEM_D ** -0.5) for h in heads]
    e = [jnp.exp(x - jnp.max(x, axis=-1, keepdims=True)) for x in s]
    p = [(x / jnp.sum(x, axis=-1, keepdims=True)).astype(BF16) for x in e]
    o = [_pick_rows([_dot(p[h], v_ref[i, win[h], :].astype(BF16)) for i in rows]) for h in heads]
    msq = [jnp.sum(x * x, axis=-1, keepdims=True) for x in o]
    for h in heads:
        y_ref[:, hs[h]] = _head_norm_gate(o[h], msq[h], MEM_D, z_ref[:, hs[h]], mixw_ref[:, hs[h]])


DEC_ROWS = SUBLANES


def _ssd_decode_kernel(proj_ref, sm_ref, cst_ref, st_ref, cw_ref, cb_ref, dtb_ref, alog_ref, dexp_ref, mixw_ref, e_ref, en_ref,
                       y_ref, cst_out_ref, st_out_ref):
    _ssd_decode_step(proj_ref.at[:, COL_XBC:COL_XBC + SSD_CONV], sm_ref, proj_ref.at[:, COL_Z:COL_Z + SSD_W], cst_ref, st_ref,
                     cw_ref, cb_ref, dtb_ref, alog_ref, dexp_ref, mixw_ref.at[:, :SSD_W], e_ref, en_ref,
                     y_ref, cst_out_ref, st_out_ref)


def _gdn_decode_kernel(proj_ref, sm_ref, cst_ref, st_ref, cw_ref, gb_ref, galog_ref, mixw_ref, en_ref,
                       y_ref, cst_out_ref, st_out_ref):
    _gdn_decode_step(proj_ref.at[:, COL_QKV:COL_QKV + GDN_CONV], sm_ref, proj_ref.at[:, COL_Z + SSD_W:COL_Z + SSD_W + GDN_W],
                     cst_ref, st_ref, cw_ref, gb_ref, galog_ref, mixw_ref.at[:, SSD_W:SSD_W + GDN_W], en_ref,
                     y_ref, cst_out_ref, st_out_ref)


def _ssd_mem_decode_kernel(proj_ref, sm_ref, cst_ref, st_ref, k_ref, v_ref, cw_ref, cb_ref, dtb_ref, alog_ref, dexp_ref,
                           mixw_ref, e_ref, en_ref, y_ssd_ref, y_mem_ref, cst_out_ref, st_out_ref):
    _mem_decode_step(proj_ref.at[:, COL_QMEM:COL_QMEM + MEM_W], k_ref, v_ref, proj_ref.at[:, N_MAIN - MEM_W:N_MAIN],
                     mixw_ref.at[:, SSD_W + GDN_W:], y_mem_ref)
    _ssd_decode_kernel(proj_ref, sm_ref, cst_ref, st_ref, cw_ref, cb_ref, dtb_ref, alog_ref, dexp_ref, mixw_ref, e_ref, en_ref,
                       y_ssd_ref, cst_out_ref, st_out_ref)


def _decode_rider(kernel, per_row_in, consts, out_shape, steps, step_index, early=()):
    groups = per_row_in[0].shape[0] // DEC_ROWS
    assert steps % groups == 0
    per_group = steps // groups

    def group_spec(a, lead=0):
        axis = 1 if len(a.shape) == 3 and a.shape[0] == CONV_K - 1 else 0
        block = a.shape[:axis] + (DEC_ROWS,) + a.shape[axis + 1:]
        group_of = lambda *g: jnp.minimum((step_index(*g) + lead) // per_group, groups - 1)
        return pl.BlockSpec(block, lambda *g: (0,) * axis + (group_of(*g),) + (0,) * (len(block) - axis - 1))

    whole = lambda a: pl.BlockSpec(a.shape, lambda *g, nd=a.ndim: (0,) * nd)
    lead = lambda n: min(1, per_group - 1) if n in early else 0
    return dict(kernel=kernel, arrays=tuple(per_row_in) + tuple(consts), steps_per_group=per_group,
                in_specs=[group_spec(a, lead(n)) for n, a in enumerate(per_row_in)] + [whole(a) for a in consts],
                out_shape=list(out_shape), out_specs=[group_spec(a) for a in out_shape])


def _scan_with_riders_kernel(*refs, scan_kernel, n_scan_in, n_scan_out, riders):
    scan_in, refs = refs[:n_scan_in], refs[n_scan_in:]
    rider_in = []
    for _, n_in, _, _ in riders:
        rider_in.append(refs[:n_in])
        refs = refs[n_in:]
    scan_out, refs = refs[:n_scan_out], refs[n_scan_out:]
    rider_out = []
    for _, _, n_out, _ in riders:
        rider_out.append(refs[:n_out])
        refs = refs[n_out:]
    scan_kernel(*scan_in, *scan_out, *refs)
    step = pl.program_id(0) * pl.num_programs(1) + pl.program_id(1)
    for (kernel, _, _, per_group), ins, outs in zip(riders, rider_in, rider_out):
        if per_group == 1:
            kernel(*ins, *outs)
        else:
            pl.when(step % per_group == 0)(functools.partial(kernel, *ins, *outs))


def _scan_with_riders(name, grid, scan_kernel, arrays, in_specs, out_shape, out_specs, scratch_shapes, riders):
    body = functools.partial(
        _scan_with_riders_kernel, scan_kernel=scan_kernel, n_scan_in=len(arrays), n_scan_out=len(out_shape),
        riders=tuple((r["kernel"], len(r["arrays"]), len(r["out_shape"]), r["steps_per_group"]) for r in riders))
    outs = pl.pallas_call(
        body,
        grid=grid,
        in_specs=list(in_specs) + [s for r in riders for s in r["in_specs"]],
        out_specs=list(out_specs) + [s for r in riders for s in r["out_specs"]],
        out_shape=list(out_shape) + [s for r in riders for s in r["out_shape"]],
        scratch_shapes=scratch_shapes,
        compiler_params=_params("arbitrary", "arbitrary"),
        name=name,
    )(*arrays, *[a for r in riders for a in r["arrays"]])
    scan_res, outs = outs[:len(out_shape)], outs[len(out_shape):]
    rider_res = []
    for r in riders:
        rider_res.append(outs[:len(r["out_shape"])])
        outs = outs[len(r["out_shape"]):]
    return scan_res, rider_res


IN_DT = SSD_CONV
IN_QKV = IN_DT + SSD_HEADS
IN_B = IN_QKV + GDN_CONV
IN_QMEM = IN_B + 2 * GDN_HEADS
IN_COLS = IN_QMEM + MEM_W + MIX_W
PREP_COLS = 256


def _prep_w_in_kernel(w_ref, x_ref, nw_ref, main_ref, small_ref, o_ref, os_ref, h_ref):
    i = pl.program_id(0)
    nblk = h_ref.shape[0]

    @pl.when(i == 0)
    def _():
        h = _rmsnorm_bf16(x_ref[...], nw_ref[...])
        for kb in range(nblk):
            h_ref[kb] = h[:, kb * PREP_COLS:(kb + 1) * PREP_COLS]
        o_ref[...] = jnp.zeros_like(o_ref)
        os_ref[...] = jnp.zeros_like(os_ref)

    main_ref[COL_QKV:COL_QKV + GDN_CONV, :] = w_ref[IN_QKV:IN_B, :].astype(BF16)
    main_ref[COL_XBC:COL_XBC + SSD_CONV, :] = w_ref[:SSD_CONV, :].astype(BF16)
    main_ref[COL_QMEM:N_MAIN, :] = w_ref[IN_QMEM:IN_COLS, :].astype(BF16)
    small_ref[SM_DT:SM_B, :] = w_ref[IN_DT:IN_QKV, :].astype(BF16)
    small_ref[SM_B:SM_A + GDN_HEADS, :] = w_ref[IN_B:IN_QMEM, :].astype(BF16)
    small_ref[SM_A + GDN_HEADS:, :] = jnp.zeros((LANES - SM_A - GDN_HEADS, w_ref.shape[1]), BF16)

    hb = h_ref[i]
    o_ref[...] += _dot_nt(hb, main_ref[...])
    os_ref[...] += _dot_nt(hb, small_ref[...])


def _prep_w_in(w_t, x, nw):
    n, k = w_t.shape
    rows = x.shape[0]
    assert n == IN_COLS
    whole = lambda i: (0, 0)
    return pl.pallas_call(
        _prep_w_in_kernel,
        grid=(k // PREP_COLS,),
        in_specs=[pl.BlockSpec((IN_COLS, PREP_COLS), lambda i: (0, i)), pl.BlockSpec((rows, k), whole),
                  pl.BlockSpec((1, k), whole)],
        out_specs=[pl.BlockSpec((N_MAIN, PREP_COLS), lambda i: (0, i)), pl.BlockSpec((LANES, PREP_COLS), lambda i: (0, i)),
                   pl.BlockSpec((rows, N_MAIN), whole), pl.BlockSpec((rows, LANES), whole)],
        out_shape=[jax.ShapeDtypeStruct((N_MAIN, k), BF16), jax.ShapeDtypeStruct((LANES, k), BF16),
                   jax.ShapeDtypeStruct((rows, N_MAIN), F32), jax.ShapeDtypeStruct((rows, LANES), F32)],
        scratch_shapes=[pltpu.VMEM((k // PREP_COLS, rows, PREP_COLS), BF16)],
        compiler_params=_params("arbitrary"),
        name="prep_w_in",
    )(w_t, x, nw)


def _head_expander(heads, first_lane, width):
    m = np.zeros((LANES, heads * width), np.float32)
    for h in range(heads):
        m[first_lane + h, h * width:(h + 1) * width] = 1.0
    return jnp.asarray(m, BF16)


def _lane_row(vec, first_lane):
    return jnp.zeros((1, LANES), F32).at[0, first_lane:first_lane + vec.shape[0]].set(vec.astype(F32))


def kernel(x_prompt, x_sample, mem_prompt, state_ssd_conv, state_ssd, state_gdn_conv, state_gdn, cache_mem_k, cache_mem_v, norm_w, w_in, ssd_conv_w, ssd_conv_b, ssd_dt_bias, ssd_A_log, ssd_D, gdn_conv_w, gdn_dt_bias, gdn_A_log, mem_norm_w, w_mem_kv, mix_norm_w, w_out, final_norm_w):
    bp, seq, d = x_prompt.shape
    bs = x_sample.shape[0]
    assert (d, seq % CHUNK, norm_w.shape[0]) == (D_MODEL, 0, 1)

    nw = norm_w[0][None, :]
    xs = x_sample.reshape(bs, d)
    w_main, w_small, proj_s, small_s = _prep_w_in(w_in[0].T, xs, nw)
    wo = w_out[0].astype(BF16)
    mixw = mix_norm_w[0][None, :]
    mixw1, mixw2, mixw3 = mixw[:, :SSD_W], mixw[:, SSD_W:SSD_W + GDN_W], mixw[:, SSD_W + GDN_W:]
    fw = final_norm_w[None, :]
    ssd_dtb = _lane_row(ssd_dt_bias[0], SM_DT)
    ssd_alog = _lane_row(ssd_A_log[0], SM_DT)
    ssd_dexp = jnp.repeat(ssd_D[0].astype(F32), SSD_P)[None, :]
    gdn_b = _lane_row(gdn_dt_bias[0], SM_A)
    gdn_alog = _lane_row(gdn_A_log[0], SM_A)
    e_ssd = _head_expander(SSD_HEADS, SM_DT, SSD_P)
    e_ssd_n = _head_expander(SSD_HEADS, SM_DT, SSD_N)
    e_gdn_n = _head_expander(GDN_HEADS, SM_A, GDN_D)
    ssd_cw, ssd_cb, gdn_cw = ssd_conv_w[0], ssd_conv_b[0][None, :], gdn_conv_w[0]

    xp = x_prompt.reshape(bp * seq, d)
    proj_p, small_p = _norm_matmul(xp, nw, w_main, w_small, _row_tile(bp * seq, PROJ_ROWS), SSD_CONV)
    kv = _mem_kv(mem_prompt.reshape(bp * MEM_TOKENS, d), mem_norm_w[0][None, :], w_mem_kv[0].astype(BF16), MEM_TOKENS)
    scan_rows = SCAN_ROWS if seq % SCAN_ROWS == 0 else CHUNK
    scan_steps = bp * seq // scan_rows
    step_index = lambda b, c: b * (seq // scan_rows) + c

    ssd_cst, gdn_cst = jnp.swapaxes(state_ssd_conv[0], 0, 1), jnp.swapaxes(state_gdn_conv[0], 0, 1)
    mem_k = cache_mem_k.reshape(bs, MEM_TOKENS * MEM_HEADS, MEM_D)
    mem_v = cache_mem_v.reshape(bs, MEM_TOKENS * MEM_HEADS, MEM_D)
    row_out = lambda width: jax.ShapeDtypeStruct((bs, width), F32)
    like = lambda a: jax.ShapeDtypeStruct(a.shape, F32)
    ssd_rider = _decode_rider(
        _ssd_mem_decode_kernel, (proj_s, small_s, ssd_cst, state_ssd[0], mem_k, mem_v),
        (ssd_cw, ssd_cb, ssd_dtb, ssd_alog, ssd_dexp, mixw, e_ssd, e_ssd_n),
        (row_out(SSD_W), row_out(MEM_W), like(ssd_cst), like(state_ssd[0])), scan_steps, step_index, early=(4, 5))
    out_rows = _row_tile(bp * seq, OUT_ROWS)
    gdn_rider = _decode_rider(
        _gdn_decode_kernel, (proj_s, small_s, gdn_cst, state_gdn[0]), (gdn_cw, gdn_b, gdn_alog, mixw, e_gdn_n),
        (row_out(GDN_W), like(gdn_cst), like(state_gdn[0])), bp * seq // out_rows, lambda i, _: i)

    (y_ssd, tail_ssd, p_ssd), ((ys_ssd, ys_mem, s_ssd_conv, s_ssd),) = _ssd_prompt(
        proj_p, small_p, bp, ssd_cw, ssd_cb, ssd_dtb, ssd_alog, ssd_dexp, mixw1, e_ssd, scan_rows, (ssd_rider,))
    (y_gdn, tail_gdn, p_gdn), _ = _gdn_prompt(proj_p, small_p, bp, gdn_cw, gdn_b, gdn_alog, mixw2, scan_rows, ())
    y_mem = _mem_prompt(proj_p, kv, bp, mixw3, _row_tile(seq, MEM_Q_ROWS))
    y_prompt, ((ys_gdn, s_gdn_conv, s_gdn),) = _out_proj(y_ssd, y_gdn, y_mem, wo, xp, fw, out_rows, (gdn_rider,))
    y_prompt = y_prompt.reshape(bp, seq, d)
    y_sample = _out_proj(ys_ssd, ys_gdn, ys_mem, wo, xs, fw, bs)[0].reshape(bs, 1, d)

    keep = CONV_K - 1
    mem_shape = (1, bp, MEM_TOKENS, MEM_HEADS, MEM_D)
    return (
        y_prompt, y_sample,
        tail_ssd[None, :, SUBLANES - keep:, :], p_ssd[None],
        tail_gdn[None, :, SUBLANES - keep:, :], p_gdn[None],
        kv[:, :MEM_W].reshape(mem_shape), kv[:, MEM_W:].reshape(mem_shape),
        jnp.swapaxes(s_ssd_conv, 0, 1)[None], s_ssd[None],
        jnp.swapaxes(s_gdn_conv, 0, 1)[None], s_gdn[None],
    )
```

```python
import functools

import numpy as np
import jax
import jax.numpy as jnp
from jax import lax
from jax.experimental import pallas as pl
from jax.experimental.pallas import tpu as pltpu

F32, BF16 = jnp.float32, jnp.bfloat16

D_MODEL = 2048
SSD_HEADS, SSD_P, SSD_GROUPS, SSD_N = 16, 64, 2, 128
SSD_W = SSD_HEADS * SSD_P
SSD_GW = SSD_W // SSD_GROUPS
SSD_CONV = SSD_W + 2 * SSD_GROUPS * SSD_N
GDN_HEADS, GDN_D = 8, 128
GDN_W = GDN_HEADS * GDN_D
GDN_CONV = 3 * GDN_W
MEM_TOKENS, MEM_HEADS, MEM_D = 256, 4, 128
MEM_W = MEM_HEADS * MEM_D
MIX_W = SSD_W + GDN_W + MEM_W
CONV_K = 4
CHUNK = 64
EPS = 1e-6

LANES = 128
SUBLANES = 8
VMEM_LIMIT = 60 * 1024 * 1024
PROJ_ROWS = 1024
OUT_ROWS = 512
MEM_Q_ROWS = 512
SCAN_ROWS = 256
CONV_PHASES = 4

COL_QKV = 0
COL_XBC = COL_QKV + GDN_CONV
COL_QMEM = COL_XBC + SSD_CONV
COL_Z = COL_QMEM + MEM_W
N_MAIN = COL_Z + MIX_W
SM_DT, SM_B, SM_A = 0, SSD_HEADS, SSD_HEADS + GDN_HEADS


def _dot(a, b):
    return jnp.dot(a, b, preferred_element_type=F32)


def _dot_nt(a, b):
    return lax.dot_general(a, b, (((1,), (1,)), ((), ())), preferred_element_type=F32)


def _dot_tn(a, b):
    return lax.dot_general(a, b, (((0,), (0,)), ((), ())), preferred_element_type=F32)


def _split(x, n):
    parts, r = [], x
    for i in range(n):
        p = r.astype(BF16)
        parts.append(p)
        if i + 1 < n:
            r = r - p.astype(F32)
    return parts


def _sel_left(sel, x, n=3):
    return functools.reduce(lambda a, b: a + b, [_dot(sel, p) for p in _split(x, n)])


def _sel_right(x, sel, n=3):
    return functools.reduce(lambda a, b: a + b, [_dot(p, sel) for p in _split(x, n)])


def _sel_right_nt(x, sel, n=3):
    return functools.reduce(lambda a, b: a + b, [_dot_nt(p, sel) for p in _split(x, n)])


def _transpose_sel(x, n=3):
    eye = _eye(LANES).astype(BF16)
    return functools.reduce(lambda a, b: a + b, [_dot_nt(eye, p) for p in _split(x, n)])


def _eye(n):
    return (lax.broadcasted_iota(jnp.int32, (n, n), 0) == lax.broadcasted_iota(jnp.int32, (n, n), 1)).astype(F32)


def _sigmoid(x):
    return 0.5 * jnp.tanh(0.5 * x) + 0.5


def _silu(x):
    h = 0.5 * x
    return h + h * jnp.tanh(h)


def _softplus(x):
    return jnp.maximum(x, 0.0) + jnp.log1p(jnp.exp(-jnp.abs(x)))


def _params(*sem):
    return pltpu.CompilerParams(dimension_semantics=sem, vmem_limit_bytes=VMEM_LIMIT)


def _row_tile(rows, preferred):
    return preferred if rows % preferred == 0 else rows


def _rmsnorm_bf16(x, nw):
    ms = jnp.mean(x * x, axis=-1, keepdims=True)
    return (x * lax.rsqrt(ms + EPS) * nw).astype(BF16)


def _norm_matmul_kernel(x_ref, nw_ref, w_ref, ws_ref, o_ref, os_ref, h_ref):
    @pl.when(pl.program_id(1) == 0)
    def _():
        h = _rmsnorm_bf16(x_ref[...], nw_ref[...])
        h_ref[...] = h
        os_ref[...] = _dot_nt(h, ws_ref[...])

    o_ref[...] = _dot_nt(h_ref[...], w_ref[...])


def _norm_matmul(x, nw, w_t, ws_t, tm, tn):
    m, k = x.shape
    n = w_t.shape[0]
    ns = ws_t.shape[0]
    return pl.pallas_call(
        _norm_matmul_kernel,
        grid=(m // tm, n // tn),
        in_specs=[
            pl.BlockSpec((tm, k), lambda i, j: (i, 0)),
            pl.BlockSpec((1, k), lambda i, j: (0, 0)),
            pl.BlockSpec((tn, k), lambda i, j: (j, 0)),
            pl.BlockSpec((ns, k), lambda i, j: (0, 0)),
        ],
        out_specs=[
            pl.BlockSpec((tm, tn), lambda i, j: (i, j)),
            pl.BlockSpec((tm, ns), lambda i, j: (i, 0)),
        ],
        out_shape=[jax.ShapeDtypeStruct((m, n), F32), jax.ShapeDtypeStruct((m, ns), F32)],
        scratch_shapes=[pltpu.VMEM((tm, k), BF16)],
        compiler_params=_params("parallel", "arbitrary"),
        name="norm_matmul",
    )(x, nw, w_t, ws_t)


def _mem_kv_kernel(x_ref, nw_ref, w_ref, o_ref):
    o_ref[...] = _dot(_rmsnorm_bf16(x_ref[...], nw_ref[...]), w_ref[...])


def _mem_kv(x, nw, w, tm):
    m, k = x.shape
    n = w.shape[1]
    return pl.pallas_call(
        _mem_kv_kernel,
        grid=(m // tm,),
        in_specs=[pl.BlockSpec((tm, k), lambda i: (i, 0)), pl.BlockSpec((1, k), lambda i: (0, 0)),
                  pl.BlockSpec((k, n), lambda i: (0, 0))],
        out_specs=pl.BlockSpec((tm, n), lambda i: (i, 0)),
        out_shape=jax.ShapeDtypeStruct((m, n), F32),
        compiler_params=_params("parallel"),
        name="mem_kv",
    )(x, nw, w)


def _out_proj_kernel(y1_ref, y2_ref, y3_ref, w_ref, x_ref, fw_ref, o_ref):
    n1, n2 = y1_ref.shape[1], y1_ref.shape[1] + y2_ref.shape[1]
    acc = (_dot(y1_ref[...].astype(BF16), w_ref[:n1, :]) + _dot(y2_ref[...].astype(BF16), w_ref[n1:n2, :])
           + _dot(y3_ref[...].astype(BF16), w_ref[n2:, :]))
    r = x_ref[...] + acc
    ms = jnp.mean(r * r, axis=-1, keepdims=True)
    o_ref[...] = r * lax.rsqrt(ms + EPS) * fw_ref[...]


def _out_proj(y1, y2, y3, w, x, fw, tm, riders=()):
    m, d = x.shape
    row = lambda i, _: (i, 0)
    whole = lambda i, _: (0, 0)
    (out,), rider_res = _scan_with_riders(
        "out_proj", (m // tm, 1), _out_proj_kernel,
        arrays=(y1, y2, y3, w, x, fw),
        in_specs=[
            pl.BlockSpec((tm, y1.shape[1]), row), pl.BlockSpec((tm, y2.shape[1]), row), pl.BlockSpec((tm, y3.shape[1]), row),
            pl.BlockSpec(w.shape, whole, pipeline_mode=pl.Buffered(1)), pl.BlockSpec((tm, d), row), pl.BlockSpec((1, d), whole),
        ],
        out_shape=[jax.ShapeDtypeStruct((m, d), F32)],
        out_specs=[pl.BlockSpec((tm, d), row)],
        scratch_shapes=[],
        riders=riders)
    return out, rider_res


def _causal_conv_tile(u_ref, ubuf_ref, cw_ref, cb_ref, out_ref, tail_ref):
    t, width = u_ref.shape
    n = t // CONV_PHASES
    for s in range(width // LANES):
        cs = slice(s * LANES, (s + 1) * LANES)
        ubuf_ref[s, SUBLANES:SUBLANES + t, :] = u_ref[:, cs]
        taps = {d: ubuf_ref[s, pl.ds(SUBLANES + d, n, stride=CONV_PHASES), :] for d in range(1 - CONV_K, CONV_PHASES)}
        w = [cw_ref[j:j + 1, cs] for j in range(CONV_K)]
        for r in range(CONV_PHASES):
            acc = w[CONV_K - 1] * taps[r]
            if cb_ref is not None:
                acc = acc + cb_ref[:, cs]
            for j in range(CONV_K - 1):
                acc = acc + w[j] * taps[r - (CONV_K - 1) + j]
            out_ref[s, pl.ds(r, n, stride=CONV_PHASES), :] = _silu(acc)
        tail = ubuf_ref[s, t:t + SUBLANES, :]
        ubuf_ref[s, 0:SUBLANES, :] = tail
        tail_ref[0, :, cs] = tail


def _head_norm_gate(y, msq, width, z, mixw):
    return y * lax.rsqrt(msq * (1.0 / width) + EPS) * mixw * _silu(z)


def _ssd_prompt_kernel(xbc_ref, sm_ref, z_ref, cw_ref, cb_ref, dtb_ref, alog_ref, dexp_ref, mixw_ref, e_ref,
                       y_ref, tail_ref, state_ref, ubuf_ref, conv_ref, h_ref):
    c = pl.program_id(1)
    t = xbc_ref.shape[0]
    subs = range(t // CHUNK)
    groups = range(SSD_GROUPS)
    blocks = range(SSD_GW // LANES)

    @pl.when(c == 0)
    def _():
        ubuf_ref[:, 0:SUBLANES, :] = jnp.zeros((SSD_CONV // LANES, SUBLANES, LANES), F32)
        h_ref[...] = jnp.zeros_like(h_ref)

    _causal_conv_tile(xbc_ref, ubuf_ref, cw_ref, cb_ref, conv_ref, tail_ref)
    xs = jnp.concatenate([conv_ref[s] for s in range(SSD_W // LANES)], axis=1)
    e = e_ref[...]
    rows = [slice(j * CHUNK, (j + 1) * CHUNK) for j in subs]
    gs = [slice(g * SSD_GW, (g + 1) * SSD_GW) for g in groups]

    dt = _softplus(sm_ref[...] + dtb_ref[...])
    a = dt * (-jnp.exp(alog_ref[...]))
    rt = lax.broadcasted_iota(jnp.int32, (t, t), 0)
    ct = lax.broadcasted_iota(jnp.int32, (t, t), 1)
    chunk_causal = (rt >= ct) & (rt // CHUNK == ct // CHUNK)
    cum = _sel_left(chunk_causal.astype(BF16), a)
    cum_t = _transpose_sel(cum)
    ecum = jnp.exp(cum)
    wend = jnp.concatenate([jnp.exp(cum[(j + 1) * CHUNK - 1:(j + 1) * CHUNK, :] - cum[rows[j]]) for j in subs], axis=0)
    dt_x = _sel_right(dt, e, 2)
    ecum_x = _sel_right(ecum, e, 2)
    wend_x = _sel_right(wend, e, 2)

    xdt = xs * dt_x
    xdt_b = xdt.astype(BF16)
    xw_b = (xdt * wend_x).astype(BF16)
    lane = lax.broadcasted_iota(jnp.int32, (CHUNK, LANES), 1)
    causal = lax.broadcasted_iota(jnp.int32, (CHUNK, LANES), 0) >= lane % CHUNK
    first_head = lane < SSD_P

    jg = [(j, g) for j in subs for g in groups]
    b_slab, c_slab = SSD_W // LANES, SSD_W // LANES + SSD_GROUPS
    bmat = {(j, g): conv_ref[b_slab + g, rows[j], :].astype(BF16) for j, g in jg}
    cmat = {(j, g): conv_ref[c_slab + g, rows[j], :].astype(BF16) for j, g in jg}
    cb = {p: _dot_nt(cmat[p], jnp.concatenate([bmat[p], bmat[p]], axis=0)) for p in jg}
    inc = {(j, g): _dot_tn(bmat[j, g], xw_b[rows[j], gs[g]]) for j, g in jg}
    intra = {}
    for j, g in jg:
        for blk in blocks:
            a = (g * len(blocks) + blk) * 2
            col = jnp.where(first_head, cum[rows[j], a:a + 1], cum[rows[j], a + 1:a + 2])
            row = jnp.concatenate([cum_t[a:a + 1, rows[j]], cum_t[a + 1:a + 2, rows[j]]], axis=1)
            lmat = jnp.where(causal, jnp.exp(jnp.minimum(col - row, 0.0)), 0.0)
            lanes = slice(g * SSD_GW + blk * LANES, g * SSD_GW + (blk + 1) * LANES)
            intra[j, g, blk] = _dot((cb[j, g] * lmat).astype(BF16), _pair_diag(xdt_b[rows[j], lanes]))

    state = {(0, g): h_ref[:, gs[g]] for g in groups}
    for j in subs:
        last = (j + 1) * CHUNK - 1
        for g in groups:
            state[j + 1, g] = state[j, g] * ecum_x[last:last + 1, gs[g]] + inc[j, g]
    for g in groups:
        h_ref[:, gs[g]] = state[len(subs), g]
    inter = {p: _dot(cmat[p], state[p].astype(BF16)) for p in jg}
    inter_x = jnp.concatenate([jnp.concatenate([inter[j, g] for g in groups], axis=1) for j in subs], axis=0) * ecum_x
    intra_x = jnp.concatenate([jnp.concatenate([intra[j, g, blk] for g in groups for blk in blocks], axis=1)
                               for j in subs], axis=0)
    y = intra_x + inter_x + dexp_ref[...] * xs
    msq = _sel_right(_sel_right_nt(y * y, e, 2), e, 2)
    y_ref[...] = _head_norm_gate(y, msq, SSD_P, z_ref[...], mixw_ref[...]).astype(BF16)

    @pl.when(c == pl.num_programs(1) - 1)
    def _():
        state_ref[0] = h_ref[...].T.reshape(SSD_HEADS, SSD_P, SSD_N)


def _ssd_prompt(proj, small, batch, cw, cb, dtb, alog, dexp, mixw, e, tile, riders):
    rows = proj.shape[0]
    nc = rows // batch // tile
    row = lambda b, c: (b * nc + c, 0)
    whole = lambda b, c: (0, 0)
    return _scan_with_riders(
        "ssd_prompt", (batch, nc), _ssd_prompt_kernel,
        arrays=(proj, small, proj, cw, cb, dtb, alog, dexp, mixw, e),
        in_specs=[
            pl.BlockSpec((tile, SSD_CONV), lambda b, c: (b * nc + c, COL_XBC // SSD_CONV)),
            pl.BlockSpec((tile, LANES), row),
            pl.BlockSpec((tile, SSD_W), lambda b, c: (b * nc + c, COL_Z // SSD_W)),
            pl.BlockSpec(cw.shape, whole), pl.BlockSpec(cb.shape, whole), pl.BlockSpec(dtb.shape, whole),
            pl.BlockSpec(alog.shape, whole), pl.BlockSpec(dexp.shape, whole), pl.BlockSpec(mixw.shape, whole),
            pl.BlockSpec(e.shape, whole),
        ],
        out_shape=[
            jax.ShapeDtypeStruct((rows, SSD_W), BF16),
            jax.ShapeDtypeStruct((batch, SUBLANES, SSD_CONV), F32),
            jax.ShapeDtypeStruct((batch, SSD_HEADS, SSD_P, SSD_N), F32),
        ],
        out_specs=[
            pl.BlockSpec((tile, SSD_W), row),
            pl.BlockSpec((1, SUBLANES, SSD_CONV), lambda b, c: (b, 0, 0)),
            pl.BlockSpec((1, SSD_HEADS, SSD_P, SSD_N), lambda b, c: (b, 0, 0, 0)),
        ],
        scratch_shapes=[pltpu.VMEM((SSD_CONV // LANES, tile + SUBLANES, LANES), F32),
                        pltpu.VMEM((SSD_CONV // LANES, tile, LANES), F32), pltpu.VMEM((SSD_N, SSD_W), F32)],
        riders=riders)


def _unit_lower_inverses(a_stricts, ri, ci):
    t = a_stricts[0].shape[0]
    eye = (ri == ci).astype(F32)
    first = (ri == ci + 1) & (ci % 2 == 0)
    invs = [eye - jnp.where(first, a, 0.0) for a in a_stricts]
    a_bs = [a.astype(BF16) for a in a_stricts]
    zero = jnp.zeros(a_bs[0].shape, BF16)
    s = 2
    while s < t:
        sel = (ri // (2 * s) == ci // (2 * s)) & ((ri // s) % 2 == 1) & ((ci // s) % 2 == 0)
        inv_bs = [inv.astype(BF16) for inv in invs]
        lefts = [_dot(inv_b, _pair_diag(jnp.where(sel, a_b, zero))).astype(BF16) for inv_b, a_b in zip(inv_bs, a_bs)]
        invs = [inv - _dot(left, _pair_diag(inv_b)) for inv, left, inv_b in zip(invs, lefts, inv_bs)]
        s *= 2
    return invs


def _pair_halves(x):
    left = lax.broadcasted_iota(jnp.int32, x.shape, 1) < x.shape[1] // 2
    zero = jnp.zeros_like(x)
    return jnp.where(left, x, zero), jnp.where(left, zero, x)


def _pair_diag(x):
    return jnp.concatenate(_pair_halves(x), axis=0)


def _gdn_prompt_kernel(qkv_ref, sm_ref, z_ref, cw_ref, gb_ref, galog_ref, mixw_ref,
                       y_ref, tail_ref, state_ref, ubuf_ref, conv_ref, s_ref):
    c = pl.program_id(1)
    t = qkv_ref.shape[0]
    subs = range(t // CHUNK)
    heads = range(GDN_HEADS)

    @pl.when(c == 0)
    def _():
        ubuf_ref[:, 0:SUBLANES, :] = jnp.zeros((GDN_CONV // LANES, SUBLANES, LANES), F32)
        s_ref[...] = jnp.zeros_like(s_ref)

    _causal_conv_tile(qkv_ref, ubuf_ref, cw_ref, None, conv_ref, tail_ref)

    sm = sm_ref[...]
    beta = _sigmoid(sm)
    g = -jnp.exp(galog_ref[...]) * _softplus(sm + gb_ref[...])
    rt = lax.broadcasted_iota(jnp.int32, (t, t), 0)
    ct = lax.broadcasted_iota(jnp.int32, (t, t), 1)
    chunk_causal = (rt >= ct) & (rt // CHUNK == ct // CHUNK)
    gc = _sel_left(chunk_causal.astype(BF16), g)
    gc_t = _transpose_sel(gc)
    eg = jnp.exp(gc)
    ri = lax.broadcasted_iota(jnp.int32, (CHUNK, 2 * CHUNK), 0)
    lane = lax.broadcasted_iota(jnp.int32, (CHUNK, 2 * CHUNK), 1)
    ci = lane % CHUNK
    causal = ri >= ci
    strict = ri > ci
    first_head = lane < CHUNK

    rows = [slice(j * CHUNK, (j + 1) * CHUNK) for j in subs]
    hs = [slice(h * GDN_D, (h + 1) * GDN_D) for h in heads]
    la = [SM_A + h for h in heads]
    packs = [(j, a) for j in subs for a in range(0, GDN_HEADS, 2)]
    q, k, kb, vb, kbg, qg = {}, {}, {}, {}, {}, {}
    for h in heads:
        qf, kf, vf = conv_ref[h], conv_ref[GDN_HEADS + h], conv_ref[2 * GDN_HEADS + h]
        qf = qf * lax.rsqrt(jnp.sum(qf * qf, axis=-1, keepdims=True) + EPS) * (GDN_D ** -0.5)
        kf = kf * lax.rsqrt(jnp.sum(kf * kf, axis=-1, keepdims=True) + EPS)
        b_col = beta[:, SM_B + h:SM_B + h + 1]
        eg_col = eg[:, la[h]:la[h] + 1]
        kbf = kf * b_col
        vbf, kbgf, qgf = (vf * b_col).astype(BF16), (kbf * eg_col).astype(BF16), (qf * eg_col).astype(BF16)
        for j in subs:
            q[j, h], k[j, h], kb[j, h] = qf[rows[j]].astype(BF16), kf[rows[j]], kbf[rows[j]].astype(BF16)
            vb[j, h], kbg[j, h], qg[j, h] = vbf[rows[j]], kbgf[rows[j]], qgf[rows[j]]
    decay, kq = {}, {}
    no_keys = jnp.zeros((CHUNK, GDN_D), BF16)
    for j, a in packs:
        b = a + 1
        col = jnp.where(first_head, gc[rows[j], la[a]:la[a] + 1], gc[rows[j], la[b]:la[b] + 1])
        row = jnp.concatenate([gc_t[la[a]:la[a] + 1, rows[j]], gc_t[la[b]:la[b] + 1, rows[j]]], axis=1)
        decay[j, a] = jnp.where(causal, jnp.exp(jnp.minimum(col - row, 0.0)), 0.0)
        kq[j, a] = (_dot_nt(jnp.concatenate([kb[j, a], q[j, a]], axis=0), jnp.concatenate([k[j, a].astype(BF16), no_keys], axis=0))
                    + _dot_nt(jnp.concatenate([kb[j, b], q[j, b]], axis=0), jnp.concatenate([no_keys, k[j, b].astype(BF16)], axis=0)))
    a_strict = [jnp.where(strict, kq[p][:CHUNK] * decay[p], 0.0) for p in packs]
    attn = {p: _pair_halves((kq[p][CHUNK:] * decay[p]).astype(BF16)) for p in packs}
    t_inv = dict(zip(packs, [_pair_halves(x.astype(BF16)) for x in _unit_lower_inverses(a_strict, ri, ci)]))
    u, wk = {}, {}
    for j, a in packs:
        rhs = jnp.concatenate([jnp.concatenate([vb[j, h], kbg[j, h]], axis=1) for h in (a, a + 1)], axis=0)
        for half, h in enumerate((a, a + 1)):
            uw = _dot(t_inv[j, a][half], rhs)
            u[j, h], wk[j, h] = uw[:, :GDN_D], uw[:, GDN_D:].astype(BF16)

    state = [s_ref[h] for h in heads]
    for j in subs:
        g_last = gc[(j + 1) * CHUNK - 1:(j + 1) * CHUNK, :]
        eend = jnp.exp(g_last - gc[rows[j]])
        elast = jnp.exp(g_last)
        s_b = [x.astype(BF16) for x in state]
        v_new = [(u[j, h] - _dot(wk[j, h], s_b[h])).astype(BF16) for h in heads]
        k_end = [(k[j, h] * eend[:, la[h]:la[h] + 1]).astype(BF16) for h in heads]
        s_inc = [_dot_tn(k_end[h], v_new[h]) for h in heads]
        state = [state[h] * elast[:, la[h]:la[h] + 1] + s_inc[h] for h in heads]
        v_pair = {a: jnp.concatenate([v_new[a], v_new[a + 1]], axis=0) for a in range(0, GDN_HEADS, 2)}
        o = [_dot(qg[j, h], s_b[h]) + _dot(attn[j, h - h % 2][h % 2], v_pair[h - h % 2]) for h in heads]
        msq = [jnp.sum(x * x, axis=-1, keepdims=True) for x in o]
        for h in heads:
            y_ref[rows[j], hs[h]] = _head_norm_gate(o[h], msq[h], GDN_D, z_ref[rows[j], hs[h]], mixw_ref[:, hs[h]]).astype(BF16)
    for h in heads:
        s_ref[h] = state[h]

    @pl.when(c == pl.num_programs(1) - 1)
    def _():
        state_ref[0] = s_ref[...]


def _gdn_prompt(proj, small, batch, cw, gb, galog, mixw, tile, riders):
    rows = proj.shape[0]
    nc = rows // batch // tile
    row = lambda b, c: (b * nc + c, 0)
    whole = lambda b, c: (0, 0)
    return _scan_with_riders(
        "gdn_prompt", (batch, nc), _gdn_prompt_kernel,
        arrays=(proj, small, proj, cw, gb, galog, mixw),
        in_specs=[
            pl.BlockSpec((tile, GDN_CONV), lambda b, c: (b * nc + c, COL_QKV // GDN_CONV)),
            pl.BlockSpec((tile, LANES), row),
            pl.BlockSpec((tile, GDN_W), lambda b, c: (b * nc + c, (COL_Z + SSD_W) // GDN_W)),
            pl.BlockSpec(cw.shape, whole), pl.BlockSpec(gb.shape, whole), pl.BlockSpec(galog.shape, whole),
            pl.BlockSpec(mixw.shape, whole),
        ],
        out_shape=[
            jax.ShapeDtypeStruct((rows, GDN_W), BF16),
            jax.ShapeDtypeStruct((batch, SUBLANES, GDN_CONV), F32),
            jax.ShapeDtypeStruct((batch, GDN_HEADS, GDN_D, GDN_D), F32),
        ],
        out_specs=[
            pl.BlockSpec((tile, GDN_W), row),
            pl.BlockSpec((1, SUBLANES, GDN_CONV), lambda b, c: (b, 0, 0)),
            pl.BlockSpec((1, GDN_HEADS, GDN_D, GDN_D), lambda b, c: (b, 0, 0, 0)),
        ],
        scratch_shapes=[pltpu.VMEM((GDN_CONV // LANES, tile + SUBLANES, LANES), F32),
                        pltpu.VMEM((GDN_CONV // LANES, tile, LANES), F32), pltpu.VMEM((GDN_HEADS, GDN_D, GDN_D), F32)],
        riders=riders)


def _mem_prompt_kernel(q_ref, k_ref, v_ref, z_ref, mixw_ref, y_ref):
    heads = range(MEM_HEADS)
    hs = [slice(h * MEM_D, (h + 1) * MEM_D) for h in heads]
    s = [_dot_nt(q_ref[:, hs[h]].astype(BF16), k_ref[:, hs[h]].astype(BF16)) * (MEM_D ** -0.5) for h in heads]
    e = [jnp.exp(x - jnp.max(x, axis=-1, keepdims=True)) for x in s]
    p = [(x / jnp.sum(x, axis=-1, keepdims=True)).astype(BF16) for x in e]
    o = [_dot(p[h], v_ref[:, hs[h]].astype(BF16)) for h in heads]
    msq = [jnp.sum(x * x, axis=-1, keepdims=True) for x in o]
    for h in heads:
        y_ref[:, hs[h]] = _head_norm_gate(o[h], msq[h], MEM_D, z_ref[:, hs[h]], mixw_ref[:, hs[h]]).astype(BF16)


def _mem_prompt(proj, kv, batch, mixw, tq):
    rows = proj.shape[0]
    nq = rows // batch // tq
    return pl.pallas_call(
        _mem_prompt_kernel,
        grid=(batch, nq),
        in_specs=[
            pl.BlockSpec((tq, MEM_W), lambda b, i: (b * nq + i, COL_QMEM // MEM_W)),
            pl.BlockSpec((MEM_TOKENS, MEM_W), lambda b, i: (b, 0)),
            pl.BlockSpec((MEM_TOKENS, MEM_W), lambda b, i: (b, 1)),
            pl.BlockSpec((tq, MEM_W), lambda b, i: (b * nq + i, (COL_Z + SSD_W + GDN_W) // MEM_W)),
            pl.BlockSpec(mixw.shape, lambda b, i: (0, 0)),
        ],
        out_specs=pl.BlockSpec((tq, MEM_W), lambda b, i: (b * nq + i, 0)),
        out_shape=jax.ShapeDtypeStruct((rows, MEM_W), BF16),
        compiler_params=_params("parallel", "parallel"),
        name="mem_prompt",
    )(proj, kv, kv, proj, mixw)


def _conv_step(u, cst_ref, cst_out_ref, cw_ref, bias):
    acc = cw_ref[CONV_K - 1:CONV_K, :] * u
    if bias is not None:
        acc = acc + bias
    for j in range(CONV_K - 1):
        prev = cst_ref[j]
        acc = acc + cw_ref[j:j + 1, :] * prev
        if j > 0:
            cst_out_ref[j - 1] = prev
    cst_out_ref[CONV_K - 2] = u
    return _silu(acc)


def _rows_to_columns(x):
    pad = jnp.zeros((LANES - x.shape[0], x.shape[1]), F32)
    return jnp.concatenate([x, pad], axis=0).T


def _pick_rows(parts):
    rid = lax.broadcasted_iota(jnp.int32, parts[0].shape, 0)
    out = parts[0]
    for i in range(1, len(parts)):
        out = jnp.where(rid == i, parts[i], out)
    return out


def _ssd_decode_step(xbc_ref, sm_ref, z_ref, cst_ref, st_ref, cw_ref, cb_ref, dtb_ref, alog_ref, dexp_ref,
                     mixw_ref, e_ref, en_ref, y_ref, cst_out_ref, st_out_ref):
    xbc = _conv_step(xbc_ref[...], cst_ref, cst_out_ref, cw_ref, cb_ref[...])
    xs = xbc[:, :SSD_W]
    e = e_ref[...]
    dt = _softplus(sm_ref[...] + dtb_ref[...])
    dec = jnp.exp(dt * (-jnp.exp(alog_ref[...])))
    xd_t = _rows_to_columns(xs * _sel_right(dt, e))
    dec_n = _sel_right(dec, en_ref[...])

    groups, rows, per_group = range(SSD_GROUPS), range(xs.shape[0]), SSD_HEADS // SSD_GROUPS
    b_g = [xbc[:, SSD_W + g * SSD_N:SSD_W + (g + 1) * SSD_N] for g in groups]
    c_g = [xbc[:, SSD_W + (SSD_GROUPS + g) * SSD_N:SSD_W + (SSD_GROUPS + g + 1) * SSD_N].astype(BF16) for g in groups]
    hn = {}
    for g in groups:
        for i in rows:
            for h in range(g * per_group, (g + 1) * per_group):
                col = xd_t[h * SSD_P:(h + 1) * SSD_P, i:i + 1]
                hn[i, h] = st_ref[i, h] * dec_n[i:i + 1, h * SSD_N:(h + 1) * SSD_N] + col * b_g[g][i:i + 1, :]
                st_out_ref[i, h] = hn[i, h]
    hg = {(g, i): jnp.concatenate([hn[i, h] for h in range(g * per_group, (g + 1) * per_group)], axis=0).astype(BF16)
          for g in groups for i in rows}
    y_rows = {p: _dot_nt(c_g[p[0]], hg[p]) for p in hg}
    y = jnp.concatenate([_pick_rows([y_rows[g, i] for i in rows]) for g in groups], axis=1) + dexp_ref[...] * xs
    msq = _sel_right(_sel_right_nt(y * y, e, 2), e, 2)
    y_ref[...] = _head_norm_gate(y, msq, SSD_P, z_ref[...], mixw_ref[...])


def _gdn_decode_step(qkv_ref, sm_ref, z_ref, cst_ref, st_ref, cw_ref, gb_ref, galog_ref, mixw_ref, en_ref,
                     y_ref, cst_out_ref, st_out_ref):
    qkv = _conv_step(qkv_ref[...], cst_ref, cst_out_ref, cw_ref, None)
    sm = sm_ref[...]
    beta = _sigmoid(sm)
    eg = jnp.exp(-jnp.exp(galog_ref[...]) * _softplus(sm + gb_ref[...]))
    eg_n = _sel_right(eg, en_ref[...])

    qs, ks = [], []
    for h in range(GDN_HEADS):
        q = qkv[:, h * GDN_D:(h + 1) * GDN_D]
        k = qkv[:, GDN_W + h * GDN_D:GDN_W + (h + 1) * GDN_D]
        qs.append(q * lax.rsqrt(jnp.sum(q * q, axis=-1, keepdims=True) + EPS) * (GDN_D ** -0.5))
        ks.append(k * lax.rsqrt(jnp.sum(k * k, axis=-1, keepdims=True) + EPS))
    k_t = _rows_to_columns(jnp.concatenate(ks, axis=1))

    heads, rows = range(GDN_HEADS), range(sm.shape[0])
    n = len(rows)
    hs = [slice(h * GDN_D, (h + 1) * GDN_D) for h in heads]
    kq_b = [jnp.concatenate([ks[h], qs[h]], axis=0).astype(BF16) for h in heads]
    prod = {(h, i): _dot(kq_b[h], st_ref[i, h].astype(BF16)) for h in heads for i in rows}
    k_s = [_pick_rows([prod[h, i][:n] for i in rows]) for h in heads]
    q_s = [_pick_rows([prod[h, i][n:] for i in rows]) for h in heads]
    eg_h = [eg_n[:, hs[h]] for h in heads]
    v = [qkv[:, 2 * GDN_W + h * GDN_D:2 * GDN_W + (h + 1) * GDN_D] for h in heads]
    v_new = [beta[:, SM_B + h:SM_B + h + 1] * (v[h] - eg_h[h] * k_s[h]) for h in heads]
    o = [eg_h[h] * q_s[h] + jnp.sum(qs[h] * ks[h], axis=-1, keepdims=True) * v_new[h] for h in heads]
    for h in heads:
        for i in rows:
            st_out_ref[i, h] = st_ref[i, h] * eg_h[h][i:i + 1, :] + k_t[hs[h], i:i + 1] * v_new[h][i:i + 1, :]
    msq = [jnp.sum(x * x, axis=-1, keepdims=True) for x in o]
    for h in heads:
        y_ref[:, hs[h]] = _head_norm_gate(o[h], msq[h], GDN_D, z_ref[:, hs[h]], mixw_ref[:, hs[h]])


def _mem_decode_step(q_ref, k_ref, v_ref, z_ref, mixw_ref, y_ref):
    heads, rows = range(MEM_HEADS), range(q_ref.shape[0])
    hs = [slice(h * MEM_D, (h + 1) * MEM_D) for h in heads]
    win = [pl.ds(h, MEM_TOKENS, stride=MEM_HEADS) for h in heads]
    q = [q_ref[:, hs[h]].astype(BF16) for h in heads]
    s = [_pick_rows([_dot_nt(q[h], k_ref[i, win[h], :].astype(BF16)) for i in rows]) * (MEM_D ** -0.5) for h in heads]
    e = [jnp.exp(x - jnp.max(x, axis=-1, keepdims=True)) for x in s]
    p = [(x / jnp.sum(x, axis=-1, keepdims=True)).astype(BF16) for x in e]
    o = [_pick_rows([_dot(p[h], v_ref[i, win[h], :].astype(BF16)) for i in rows]) for h in heads]
    msq = [jnp.sum(x * x, axis=-1, keepdims=True) for x in o]
    for h in heads:
        y_ref[:, hs[h]] = _head_norm_gate(o[h], msq[h], MEM_D, z_ref[:, hs[h]], mixw_ref[:, hs[h]])


DEC_ROWS = SUBLANES


def _ssd_decode_kernel(proj_ref, sm_ref, cst_ref, st_ref, cw_ref, cb_ref, dtb_ref, alog_ref, dexp_ref, mixw_ref, e_ref, en_ref,
                       y_ref, cst_out_ref, st_out_ref):
    _ssd_decode_step(proj_ref.at[:, COL_XBC:COL_XBC + SSD_CONV], sm_ref, proj_ref.at[:, COL_Z:COL_Z + SSD_W], cst_ref, st_ref,
                     cw_ref, cb_ref, dtb_ref, alog_ref, dexp_ref, mixw_ref.at[:, :SSD_W], e_ref, en_ref,
                     y_ref, cst_out_ref, st_out_ref)


def _gdn_decode_kernel(proj_ref, sm_ref, cst_ref, st_ref, cw_ref, gb_ref, galog_ref, mixw_ref, en_ref,
                       y_ref, cst_out_ref, st_out_ref):
    _gdn_decode_step(proj_ref.at[:, COL_QKV:COL_QKV + GDN_CONV], sm_ref, proj_ref.at[:, COL_Z + SSD_W:COL_Z + SSD_W + GDN_W],
                     cst_ref, st_ref, cw_ref, gb_ref, galog_ref, mixw_ref.at[:, SSD_W:SSD_W + GDN_W], en_ref,
                     y_ref, cst_out_ref, st_out_ref)


def _ssd_mem_decode_kernel(proj_ref, sm_ref, cst_ref, st_ref, k_ref, v_ref, cw_ref, cb_ref, dtb_ref, alog_ref, dexp_ref,
                           mixw_ref, e_ref, en_ref, y_ssd_ref, y_mem_ref, cst_out_ref, st_out_ref):
    _mem_decode_step(proj_ref.at[:, COL_QMEM:COL_QMEM + MEM_W], k_ref, v_ref, proj_ref.at[:, N_MAIN - MEM_W:N_MAIN],
                     mixw_ref.at[:, SSD_W + GDN_W:], y_mem_ref)
    _ssd_decode_kernel(proj_ref, sm_ref, cst_ref, st_ref, cw_ref, cb_ref, dtb_ref, alog_ref, dexp_ref, mixw_ref, e_ref, en_ref,
                       y_ssd_ref, cst_out_ref, st_out_ref)


def _decode_rider(kernel, per_row_in, consts, out_shape, steps, step_index, early=()):
    groups = per_row_in[0].shape[0] // DEC_ROWS
    assert steps % groups == 0
    per_group = steps // groups

    def group_spec(a, lead=0):
        axis = 1 if len(a.shape) == 3 and a.shape[0] == CONV_K - 1 else 0
        block = a.shape[:axis] + (DEC_ROWS,) + a.shape[axis + 1:]
        group_of = lambda *g: jnp.minimum((step_index(*g) + lead) // per_group, groups - 1)
        return pl.BlockSpec(block, lambda *g: (0,) * axis + (group_of(*g),) + (0,) * (len(block) - axis - 1))

    whole = lambda a: pl.BlockSpec(a.shape, lambda *g, nd=a.ndim: (0,) * nd)
    lead = lambda n: min(1, per_group - 1) if n in early else 0
    return dict(kernel=kernel, arrays=tuple(per_row_in) + tuple(consts), steps_per_group=per_group,
                in_specs=[group_spec(a, lead(n)) for n, a in enumerate(per_row_in)] + [whole(a) for a in consts],
                out_shape=list(out_shape), out_specs=[group_spec(a) for a in out_shape])


def _scan_with_riders_kernel(*refs, scan_kernel, n_scan_in, n_scan_out, riders):
    scan_in, refs = refs[:n_scan_in], refs[n_scan_in:]
    rider_in = []
    for _, n_in, _, _ in riders:
        rider_in.append(refs[:n_in])
        refs = refs[n_in:]
    scan_out, refs = refs[:n_scan_out], refs[n_scan_out:]
    rider_out = []
    for _, _, n_out, _ in riders:
        rider_out.append(refs[:n_out])
        refs = refs[n_out:]
    scan_kernel(*scan_in, *scan_out, *refs)
    step = pl.program_id(0) * pl.num_programs(1) + pl.program_id(1)
    for (kernel, _, _, per_group), ins, outs in zip(riders, rider_in, rider_out):
        if per_group == 1:
            kernel(*ins, *outs)
        else:
            pl.when(step % per_group == 0)(functools.partial(kernel, *ins, *outs))


def _scan_with_riders(name, grid, scan_kernel, arrays, in_specs, out_shape, out_specs, scratch_shapes, riders):
    body = functools.partial(
        _scan_with_riders_kernel, scan_kernel=scan_kernel, n_scan_in=len(arrays), n_scan_out=len(out_shape),
        riders=tuple((r["kernel"], len(r["arrays"]), len(r["out_shape"]), r["steps_per_group"]) for r in riders))
    outs = pl.pallas_call(
        body,
        grid=grid,
        in_specs=list(in_specs) + [s for r in riders for s in r["in_specs"]],
        out_specs=list(out_specs) + [s for r in riders for s in r["out_specs"]],
        out_shape=list(out_shape) + [s for r in riders for s in r["out_shape"]],
        scratch_shapes=scratch_shapes,
        compiler_params=_params("arbitrary", "arbitrary"),
        name=name,
    )(*arrays, *[a for r in riders for a in r["arrays"]])
    scan_res, outs = outs[:len(out_shape)], outs[len(out_shape):]
    rider_res = []
    for r in riders:
        rider_res.append(outs[:len(r["out_shape"])])
        outs = outs[len(r["out_shape"]):]
    return scan_res, rider_res


IN_DT = SSD_CONV
IN_QKV = IN_DT + SSD_HEADS
IN_B = IN_QKV + GDN_CONV
IN_QMEM = IN_B + 2 * GDN_HEADS
IN_COLS = IN_QMEM + MEM_W + MIX_W
PREP_COLS = 256


def _prep_w_in_kernel(w_ref, x_ref, nw_ref, main_ref, small_ref, o_ref, os_ref, h_ref):
    i = pl.program_id(0)
    nblk = h_ref.shape[0]

    @pl.when(i == 0)
    def _():
        h = _rmsnorm_bf16(x_ref[...], nw_ref[...])
        for kb in range(nblk):
            h_ref[kb] = h[:, kb * PREP_COLS:(kb + 1) * PREP_COLS]
        o_ref[...] = jnp.zeros_like(o_ref)
        os_ref[...] = jnp.zeros_like(os_ref)

    main_ref[COL_QKV:COL_QKV + GDN_CONV, :] = w_ref[IN_QKV:IN_B, :].astype(BF16)
    main_ref[COL_XBC:COL_XBC + SSD_CONV, :] = w_ref[:SSD_CONV, :].astype(BF16)
    main_ref[COL_QMEM:N_MAIN, :] = w_ref[IN_QMEM:IN_COLS, :].astype(BF16)
    small_ref[SM_DT:SM_B, :] = w_ref[IN_DT:IN_QKV, :].astype(BF16)
    small_ref[SM_B:SM_A + GDN_HEADS, :] = w_ref[IN_B:IN_QMEM, :].astype(BF16)
    small_ref[SM_A + GDN_HEADS:, :] = jnp.zeros((LANES - SM_A - GDN_HEADS, w_ref.shape[1]), BF16)

    hb = h_ref[i]
    o_ref[...] += _dot_nt(hb, main_ref[...])
    os_ref[...] += _dot_nt(hb, small_ref[...])


def _prep_w_in(w_t, x, nw):
    n, k = w_t.shape
    rows = x.shape[0]
    assert n == IN_COLS
    whole = lambda i: (0, 0)
    return pl.pallas_call(
        _prep_w_in_kernel,
        grid=(k // PREP_COLS,),
        in_specs=[pl.BlockSpec((IN_COLS, PREP_COLS), lambda i: (0, i)), pl.BlockSpec((rows, k), whole),
                  pl.BlockSpec((1, k), whole)],
        out_specs=[pl.BlockSpec((N_MAIN, PREP_COLS), lambda i: (0, i)), pl.BlockSpec((LANES, PREP_COLS), lambda i: (0, i)),
                   pl.BlockSpec((rows, N_MAIN), whole), pl.BlockSpec((rows, LANES), whole)],
        out_shape=[jax.ShapeDtypeStruct((N_MAIN, k), BF16), jax.ShapeDtypeStruct((LANES, k), BF16),
                   jax.ShapeDtypeStruct((rows, N_MAIN), F32), jax.ShapeDtypeStruct((rows, LANES), F32)],
        scratch_shapes=[pltpu.VMEM((k // PREP_COLS, rows, PREP_COLS), BF16)],
        compiler_params=_params("arbitrary"),
        name="prep_w_in",
    )(w_t, x, nw)


def _head_expander(heads, first_lane, width):
    m = np.zeros((LANES, heads * width), np.float32)
    for h in range(heads):
        m[first_lane + h, h * width:(h + 1) * width] = 1.0
    return jnp.asarray(m, BF16)


def _lane_row(vec, first_lane):
    return jnp.zeros((1, LANES), F32).at[0, first_lane:first_lane + vec.shape[0]].set(vec.astype(F32))


def kernel(x_prompt, x_sample, mem_prompt, state_ssd_conv, state_ssd, state_gdn_conv, state_gdn, cache_mem_k, cache_mem_v, norm_w, w_in, ssd_conv_w, ssd_conv_b, ssd_dt_bias, ssd_A_log, ssd_D, gdn_conv_w, gdn_dt_bias, gdn_A_log, mem_norm_w, w_mem_kv, mix_norm_w, w_out, final_norm_w):
    bp, seq, d = x_prompt.shape
    bs = x_sample.shape[0]
    assert (d, seq % CHUNK, norm_w.shape[0]) == (D_MODEL, 0, 1)

    nw = norm_w[0][None, :]
    xs = x_sample.reshape(bs, d)
    w_main, w_small, proj_s, small_s = _prep_w_in(w_in[0].T, xs, nw)
    wo = w_out[0].astype(BF16)
    mixw = mix_norm_w[0][None, :]
    mixw1, mixw2, mixw3 = mixw[:, :SSD_W], mixw[:, SSD_W:SSD_W + GDN_W], mixw[:, SSD_W + GDN_W:]
    fw = final_norm_w[None, :]
    ssd_dtb = _lane_row(ssd_dt_bias[0], SM_DT)
    ssd_alog = _lane_row(ssd_A_log[0], SM_DT)
    ssd_dexp = jnp.repeat(ssd_D[0].astype(F32), SSD_P)[None, :]
    gdn_b = _lane_row(gdn_dt_bias[0], SM_A)
    gdn_alog = _lane_row(gdn_A_log[0], SM_A)
    e_ssd = _head_expander(SSD_HEADS, SM_DT, SSD_P)
    e_ssd_n = _head_expander(SSD_HEADS, SM_DT, SSD_N)
    e_gdn_n = _head_expander(GDN_HEADS, SM_A, GDN_D)
    ssd_cw, ssd_cb, gdn_cw = ssd_conv_w[0], ssd_conv_b[0][None, :], gdn_conv_w[0]

    xp = x_prompt.reshape(bp * seq, d)
    proj_p, small_p = _norm_matmul(xp, nw, w_main, w_small, _row_tile(bp * seq, PROJ_ROWS), SSD_CONV)
    kv = _mem_kv(mem_prompt.reshape(bp * MEM_TOKENS, d), mem_norm_w[0][None, :], w_mem_kv[0].astype(BF16), MEM_TOKENS)
    scan_rows = SCAN_ROWS if seq % SCAN_ROWS == 0 else CHUNK
    scan_steps = bp * seq // scan_rows
    step_index = lambda b, c: b * (seq // scan_rows) + c

    ssd_cst, gdn_cst = jnp.swapaxes(state_ssd_conv[0], 0, 1), jnp.swapaxes(state_gdn_conv[0], 0, 1)
    mem_k = cache_mem_k.reshape(bs, MEM_TOKENS * MEM_HEADS, MEM_D)
    mem_v = cache_mem_v.reshape(bs, MEM_TOKENS * MEM_HEADS, MEM_D)
    row_out = lambda width: jax.ShapeDtypeStruct((bs, width), F32)
    like = lambda a: jax.ShapeDtypeStruct(a.shape, F32)
    ssd_rider = _decode_rider(
        _ssd_mem_decode_kernel, (proj_s, small_s, ssd_cst, state_ssd[0], mem_k, mem_v),
        (ssd_cw, ssd_cb, ssd_dtb, ssd_alog, ssd_dexp, mixw, e_ssd, e_ssd_n),
        (row_out(SSD_W), row_out(MEM_W), like(ssd_cst), like(state_ssd[0])), scan_steps, step_index, early=(4, 5))
    out_rows = _row_tile(bp * seq, OUT_ROWS)
    gdn_rider = _decode_rider(
        _gdn_decode_kernel, (proj_s, small_s, gdn_cst, state_gdn[0]), (gdn_cw, gdn_b, gdn_alog, mixw, e_gdn_n),
        (row_out(GDN_W), like(gdn_cst), like(state_gdn[0])), bp * seq // out_rows, lambda i, _: i)

    (y_ssd, tail_ssd, p_ssd), ((ys_ssd, ys_mem, s_ssd_conv, s_ssd),) = _ssd_prompt(
        proj_p, small_p, bp, ssd_cw, ssd_cb, ssd_dtb, ssd_alog, ssd_dexp, mixw1, e_ssd, scan_rows, (ssd_rider,))
    (y_gdn, tail_gdn, p_gdn), _ = _gdn_prompt(proj_p, small_p, bp, gdn_cw, gdn_b, gdn_alog, mixw2, scan_rows, ())
    y_mem = _mem_prompt(proj_p, kv, bp, mixw3, _row_tile(seq, MEM_Q_ROWS))
    y_prompt, ((ys_gdn, s_gdn_conv, s_gdn),) = _out_proj(y_ssd, y_gdn, y_mem, wo, xp, fw, out_rows, (gdn_rider,))
    y_prompt = y_prompt.reshape(bp, seq, d)
    y_sample = _out_proj(ys_ssd, ys_gdn, ys_mem, wo, xs, fw, bs)[0].reshape(bs, 1, d)

    keep = CONV_K - 1
    mem_shape = (1, bp, MEM_TOKENS, MEM_HEADS, MEM_D)
    return (
        y_prompt, y_sample,
        tail_ssd[None, :, SUBLANES - keep:, :], p_ssd[None],
        tail_gdn[None, :, SUBLANES - keep:, :], p_gdn[None],
        kv[:, :MEM_W].reshape(mem_shape), kv[:, MEM_W:].reshape(mem_shape),
        jnp.swapaxes(s_ssd_conv, 0, 1)[None], s_ssd[None],
        jnp.swapaxes(s_gdn_conv, 0, 1)[None], s_gdn[None],
    )
```

```python
import functools

import numpy as np
import jax
import jax.numpy as jnp
from jax import lax
from jax.experimental import pallas as pl
from jax.experimental.pallas import tpu as pltpu

F32, BF16 = jnp.float32, jnp.bfloat16

D_MODEL = 2048
SSD_HEADS, SSD_P, SSD_GROUPS, SSD_N = 16, 64, 2, 128
SSD_W = SSD_HEADS * SSD_P
SSD_GW = SSD_W // SSD_GROUPS
SSD_CONV = SSD_W + 2 * SSD_GROUPS * SSD_N
GDN_HEADS, GDN_D = 8, 128
GDN_W = GDN_HEADS * GDN_D
GDN_CONV = 3 * GDN_W
MEM_TOKENS, MEM_HEADS, MEM_D = 256, 4, 128
MEM_W = MEM_HEADS * MEM_D
MIX_W = SSD_W + GDN_W + MEM_W
CONV_K = 4
CHUNK = 64
EPS = 1e-6

LANES = 128
SUBLANES = 8
VMEM_LIMIT = 60 * 1024 * 1024
PROJ_ROWS = 1024
OUT_ROWS = 512
MEM_Q_ROWS = 512
SCAN_ROWS = 256
CONV_PHASES = 4

COL_QKV = 0
COL_XBC = COL_QKV + GDN_CONV
COL_QMEM = COL_XBC + SSD_CONV
COL_Z = COL_QMEM + MEM_W
N_MAIN = COL_Z + MIX_W
SM_DT, SM_B, SM_A = 0, SSD_HEADS, SSD_HEADS + GDN_HEADS


def _dot(a, b):
    return jnp.dot(a, b, preferred_element_type=F32)


def _dot_nt(a, b):
    return lax.dot_general(a, b, (((1,), (1,)), ((), ())), preferred_element_type=F32)


def _dot_tn(a, b):
    return lax.dot_general(a, b, (((0,), (0,)), ((), ())), preferred_element_type=F32)


def _split(x, n):
    parts, r = [], x
    for i in range(n):
        p = r.astype(BF16)
        parts.append(p)
        if i + 1 < n:
            r = r - p.astype(F32)
    return parts


def _sel_left(sel, x, n=3):
    return functools.reduce(lambda a, b: a + b, [_dot(sel, p) for p in _split(x, n)])


def _sel_right(x, sel, n=3):
    return functools.reduce(lambda a, b: a + b, [_dot(p, sel) for p in _split(x, n)])


def _sel_right_nt(x, sel, n=3):
    return functools.reduce(lambda a, b: a + b, [_dot_nt(p, sel) for p in _split(x, n)])


def _transpose_sel(x, n=3):
    eye = _eye(LANES).astype(BF16)
    return functools.reduce(lambda a, b: a + b, [_dot_nt(eye, p) for p in _split(x, n)])


def _eye(n):
    return (lax.broadcasted_iota(jnp.int32, (n, n), 0) == lax.broadcasted_iota(jnp.int32, (n, n), 1)).astype(F32)


def _sigmoid(x):
    return 0.5 * jnp.tanh(0.5 * x) + 0.5


def _silu(x):
    h = 0.5 * x
    return h + h * jnp.tanh(h)


def _softplus(x):
    return jnp.maximum(x, 0.0) + jnp.log1p(jnp.exp(-jnp.abs(x)))


def _params(*sem):
    return pltpu.CompilerParams(dimension_semantics=sem, vmem_limit_bytes=VMEM_LIMIT)


def _row_tile(rows, preferred):
    return preferred if rows % preferred == 0 else rows


def _rmsnorm_bf16(x, nw):
    ms = jnp.mean(x * x, axis=-1, keepdims=True)
    return (x * lax.rsqrt(ms + EPS) * nw).astype(BF16)


def _norm_matmul_kernel(x_ref, nw_ref, w_ref, ws_ref, *refs, n_casts):
    cast_in, (o_ref, os_ref), cast_out, (h_ref,) = (
        refs[:n_casts], refs[n_casts:n_casts + 2], refs[n_casts + 2:2 * n_casts + 2], refs[2 * n_casts + 2:])

    @pl.when(pl.program_id(1) == 0)
    def _():
        h = _rmsnorm_bf16(x_ref[...], nw_ref[...])
        h_ref[...] = h
        os_ref[...] = _dot_nt(h, ws_ref[...])

    o_ref[...] = _dot_nt(h_ref[...], w_ref[...])
    for src, dst in zip(cast_in, cast_out):
        dst[...] = src[...].astype(BF16)


def _cast_block_rows(rows, steps):
    aligned = [r for r in range(2 * SUBLANES, rows + 1, 2 * SUBLANES) if rows % r == 0 and rows // r <= steps]
    return aligned[0] if aligned else rows


def _norm_matmul(x, nw, w_t, ws_t, tm, tn, casts=()):
    m, k = x.shape
    n = w_t.shape[0]
    ns = ws_t.shape[0]
    nj = n // tn
    steps = (m // tm) * nj

    def cast_spec(a):
        r = _cast_block_rows(a.shape[0], steps)
        return pl.BlockSpec((r, a.shape[1]), lambda i, j: (jnp.minimum(i * nj + j, a.shape[0] // r - 1), 0))

    return pl.pallas_call(
        functools.partial(_norm_matmul_kernel, n_casts=len(casts)),
        grid=(m // tm, nj),
        in_specs=[
            pl.BlockSpec((tm, k), lambda i, j: (i, 0)),
            pl.BlockSpec((1, k), lambda i, j: (0, 0)),
            pl.BlockSpec((tn, k), lambda i, j: (j, 0)),
            pl.BlockSpec((ns, k), lambda i, j: (0, 0)),
        ] + [cast_spec(a) for a in casts],
        out_specs=[
            pl.BlockSpec((tm, tn), lambda i, j: (i, j)),
            pl.BlockSpec((tm, ns), lambda i, j: (i, 0)),
        ] + [cast_spec(a) for a in casts],
        out_shape=[jax.ShapeDtypeStruct((m, n), F32), jax.ShapeDtypeStruct((m, ns), F32)]
        + [jax.ShapeDtypeStruct(a.shape, BF16) for a in casts],
        scratch_shapes=[pltpu.VMEM((tm, k), BF16)],
        compiler_params=_params("arbitrary", "arbitrary"),
        name="norm_matmul",
    )(x, nw, w_t, ws_t, *casts)


def _mem_kv_kernel(x_ref, nw_ref, w_ref, o_ref):
    o_ref[...] = _dot(_rmsnorm_bf16(x_ref[...], nw_ref[...]), w_ref[...])


def _mem_kv(x, nw, w, tm):
    m, k = x.shape
    n = w.shape[1]
    return pl.pallas_call(
        _mem_kv_kernel,
        grid=(m // tm,),
        in_specs=[pl.BlockSpec((tm, k), lambda i: (i, 0)), pl.BlockSpec((1, k), lambda i: (0, 0)),
                  pl.BlockSpec((k, n), lambda i: (0, 0))],
        out_specs=pl.BlockSpec((tm, n), lambda i: (i, 0)),
        out_shape=jax.ShapeDtypeStruct((m, n), F32),
        compiler_params=_params("parallel"),
        name="mem_kv",
    )(x, nw, w)


def _out_proj_kernel(y1_ref, y2_ref, y3_ref, w_ref, x_ref, fw_ref, o_ref):
    n1, n2 = y1_ref.shape[1], y1_ref.shape[1] + y2_ref.shape[1]
    acc = (_dot(y1_ref[...].astype(BF16), w_ref[:n1, :]) + _dot(y2_ref[...].astype(BF16), w_ref[n1:n2, :])
           + _dot(y3_ref[...].astype(BF16), w_ref[n2:, :]))
    r = x_ref[...] + acc
    ms = jnp.mean(r * r, axis=-1, keepdims=True)
    o_ref[...] = r * lax.rsqrt(ms + EPS) * fw_ref[...]


def _out_proj(y1, y2, y3, w, x, fw, tm, riders=()):
    m, d = x.shape
    row = lambda i, _: (i, 0)
    whole = lambda i, _: (0, 0)
    (out,), rider_res = _scan_with_riders(
        "out_proj", (m // tm, 1), _out_proj_kernel,
        arrays=(y1, y2, y3, w, x, fw),
        in_specs=[
            pl.BlockSpec((tm, y1.shape[1]), row), pl.BlockSpec((tm, y2.shape[1]), row), pl.BlockSpec((tm, y3.shape[1]), row),
            pl.BlockSpec(w.shape, whole, pipeline_mode=pl.Buffered(1)), pl.BlockSpec((tm, d), row), pl.BlockSpec((1, d), whole),
        ],
        out_shape=[jax.ShapeDtypeStruct((m, d), F32)],
        out_specs=[pl.BlockSpec((tm, d), row)],
        scratch_shapes=[],
        riders=riders)
    return out, rider_res


def _causal_conv_tile(u_ref, ubuf_ref, cw_ref, cb_ref, out_ref, tail_ref):
    t, width = u_ref.shape
    n = t // CONV_PHASES
    for s in range(width // LANES):
        cs = slice(s * LANES, (s + 1) * LANES)
        ubuf_ref[s, SUBLANES:SUBLANES + t, :] = u_ref[:, cs]
        taps = {d: ubuf_ref[s, pl.ds(SUBLANES + d, n, stride=CONV_PHASES), :] for d in range(1 - CONV_K, CONV_PHASES)}
        w = [cw_ref[j:j + 1, cs] for j in range(CONV_K)]
        for r in range(CONV_PHASES):
            acc = w[CONV_K - 1] * taps[r]
            if cb_ref is not None:
                acc = acc + cb_ref[:, cs]
            for j in range(CONV_K - 1):
                acc = acc + w[j] * taps[r - (CONV_K - 1) + j]
            out_ref[s, pl.ds(r, n, stride=CONV_PHASES), :] = _silu(acc)
        tail = ubuf_ref[s, t:t + SUBLANES, :]
        ubuf_ref[s, 0:SUBLANES, :] = tail
        tail_ref[0, :, cs] = tail


def _head_norm_gate(y, msq, width, z, mixw):
    return y * lax.rsqrt(msq * (1.0 / width) + EPS) * mixw * _silu(z)


def _ssd_prompt_kernel(xbc_ref, sm_ref, z_ref, cw_ref, cb_ref, dtb_ref, alog_ref, dexp_ref, mixw_ref, e_ref,
                       y_ref, tail_ref, state_ref, ubuf_ref, conv_ref, h_ref):
    c = pl.program_id(1)
    t = xbc_ref.shape[0]
    subs = range(t // CHUNK)
    groups = range(SSD_GROUPS)
    blocks = range(SSD_GW // LANES)

    @pl.when(c == 0)
    def _():
        ubuf_ref[:, 0:SUBLANES, :] = jnp.zeros((SSD_CONV // LANES, SUBLANES, LANES), F32)
        h_ref[...] = jnp.zeros_like(h_ref)

    _causal_conv_tile(xbc_ref, ubuf_ref, cw_ref, cb_ref, conv_ref, tail_ref)
    xs = jnp.concatenate([conv_ref[s] for s in range(SSD_W // LANES)], axis=1)
    e = e_ref[...]
    rows = [slice(j * CHUNK, (j + 1) * CHUNK) for j in subs]
    gs = [slice(g * SSD_GW, (g + 1) * SSD_GW) for g in groups]

    dt = _softplus(sm_ref[...] + dtb_ref[...])
    a = dt * (-jnp.exp(alog_ref[...]))
    rt = lax.broadcasted_iota(jnp.int32, (t, t), 0)
    ct = lax.broadcasted_iota(jnp.int32, (t, t), 1)
    chunk_causal = (rt >= ct) & (rt // CHUNK == ct // CHUNK)
    cum = _sel_left(chunk_causal.astype(BF16), a)
    cum_t = _transpose_sel(cum)
    ecum = jnp.exp(cum)
    wend = jnp.concatenate([jnp.exp(cum[(j + 1) * CHUNK - 1:(j + 1) * CHUNK, :] - cum[rows[j]]) for j in subs], axis=0)
    dt_x = _sel_right(dt, e, 2)
    ecum_x = _sel_right(ecum, e, 2)
    wend_x = _sel_right(wend, e, 2)

    xdt = xs * dt_x
    xdt_b = xdt.astype(BF16)
    xw_b = (xdt * wend_x).astype(BF16)
    lane = lax.broadcasted_iota(jnp.int32, (CHUNK, LANES), 1)
    causal = lax.broadcasted_iota(jnp.int32, (CHUNK, LANES), 0) >= lane % CHUNK
    first_head = lane < SSD_P

    jg = [(j, g) for j in subs for g in groups]
    b_slab, c_slab = SSD_W // LANES, SSD_W // LANES + SSD_GROUPS
    bmat = {(j, g): conv_ref[b_slab + g, rows[j], :].astype(BF16) for j, g in jg}
    cmat = {(j, g): conv_ref[c_slab + g, rows[j], :].astype(BF16) for j, g in jg}
    cb = {p: _dot_nt(cmat[p], jnp.concatenate([bmat[p], bmat[p]], axis=0)) for p in jg}
    inc = {(j, g): _dot_tn(bmat[j, g], xw_b[rows[j], gs[g]]) for j, g in jg}
    intra = {}
    for j, g in jg:
        for blk in blocks:
            a = (g * len(blocks) + blk) * 2
            col = jnp.where(first_head, cum[rows[j], a:a + 1], cum[rows[j], a + 1:a + 2])
            row = jnp.concatenate([cum_t[a:a + 1, rows[j]], cum_t[a + 1:a + 2, rows[j]]], axis=1)
            lmat = jnp.where(causal, jnp.exp(jnp.minimum(col - row, 0.0)), 0.0)
            lanes = slice(g * SSD_GW + blk * LANES, g * SSD_GW + (blk + 1) * LANES)
            intra[j, g, blk] = _dot((cb[j, g] * lmat).astype(BF16), _pair_diag(xdt_b[rows[j], lanes]))

    state = {(0, g): h_ref[:, gs[g]] for g in groups}
    for j in subs:
        last = (j + 1) * CHUNK - 1
        for g in groups:
            state[j + 1, g] = state[j, g] * ecum_x[last:last + 1, gs[g]] + inc[j, g]
    for g in groups:
        h_ref[:, gs[g]] = state[len(subs), g]
    inter = {p: _dot(cmat[p], state[p].astype(BF16)) for p in jg}
    inter_x = jnp.concatenate([jnp.concatenate([inter[j, g] for g in groups], axis=1) for j in subs], axis=0) * ecum_x
    intra_x = jnp.concatenate([jnp.concatenate([intra[j, g, blk] for g in groups for blk in blocks], axis=1)
                               for j in subs], axis=0)
    y = intra_x + inter_x + dexp_ref[...] * xs
    msq = _sel_right(_sel_right_nt(y * y, e, 2), e, 2)
    y_ref[...] = _head_norm_gate(y, msq, SSD_P, z_ref[...], mixw_ref[...]).astype(BF16)

    @pl.when(c == pl.num_programs(1) - 1)
    def _():
        state_ref[0] = h_ref[...].T.reshape(SSD_HEADS, SSD_P, SSD_N)


def _ssd_prompt(proj, small, batch, cw, cb, dtb, alog, dexp, mixw, e, tile, riders):
    rows = proj.shape[0]
    nc = rows // batch // tile
    row = lambda b, c: (b * nc + c, 0)
    whole = lambda b, c: (0, 0)
    return _scan_with_riders(
        "ssd_prompt", (batch, nc), _ssd_prompt_kernel,
        arrays=(proj, small, proj, cw, cb, dtb, alog, dexp, mixw, e),
        in_specs=[
            pl.BlockSpec((tile, SSD_CONV), lambda b, c: (b * nc + c, COL_XBC // SSD_CONV)),
            pl.BlockSpec((tile, LANES), row),
            pl.BlockSpec((tile, SSD_W), lambda b, c: (b * nc + c, COL_Z // SSD_W)),
            pl.BlockSpec(cw.shape, whole), pl.BlockSpec(cb.shape, whole), pl.BlockSpec(dtb.shape, whole),
            pl.BlockSpec(alog.shape, whole), pl.BlockSpec(dexp.shape, whole), pl.BlockSpec(mixw.shape, whole),
            pl.BlockSpec(e.shape, whole),
        ],
        out_shape=[
            jax.ShapeDtypeStruct((rows, SSD_W), BF16),
            jax.ShapeDtypeStruct((batch, SUBLANES, SSD_CONV), F32),
            jax.ShapeDtypeStruct((batch, SSD_HEADS, SSD_P, SSD_N), F32),
        ],
        out_specs=[
            pl.BlockSpec((tile, SSD_W), row),
            pl.BlockSpec((1, SUBLANES, SSD_CONV), lambda b, c: (b, 0, 0)),
            pl.BlockSpec((1, SSD_HEADS, SSD_P, SSD_N), lambda b, c: (b, 0, 0, 0)),
        ],
        scratch_shapes=[pltpu.VMEM((SSD_CONV // LANES, tile + SUBLANES, LANES), F32),
                        pltpu.VMEM((SSD_CONV // LANES, tile, LANES), F32), pltpu.VMEM((SSD_N, SSD_W), F32)],
        riders=riders)


def _unit_lower_inverses(a_stricts, ri, ci):
    t = a_stricts[0].shape[0]
    eye = (ri == ci).astype(F32)
    first = (ri == ci + 1) & (ci % 2 == 0)
    invs = [eye - jnp.where(first, a, 0.0) for a in a_stricts]
    a_bs = [a.astype(BF16) for a in a_stricts]
    zero = jnp.zeros(a_bs[0].shape, BF16)
    s = 2
    while s < t:
        sel = (ri // (2 * s) == ci // (2 * s)) & ((ri // s) % 2 == 1) & ((ci // s) % 2 == 0)
        inv_bs = [inv.astype(BF16) for inv in invs]
        lefts = [_dot(inv_b, _pair_diag(jnp.where(sel, a_b, zero))).astype(BF16) for inv_b, a_b in zip(inv_bs, a_bs)]
        invs = [inv - _dot(left, _pair_diag(inv_b)) for inv, left, inv_b in zip(invs, lefts, inv_bs)]
        s *= 2
    return invs


def _pair_halves(x):
    left = lax.broadcasted_iota(jnp.int32, x.shape, 1) < x.shape[1] // 2
    zero = jnp.zeros_like(x)
    return jnp.where(left, x, zero), jnp.where(left, zero, x)


def _pair_diag(x):
    return jnp.concatenate(_pair_halves(x), axis=0)


def _gdn_prompt_kernel(qkv_ref, sm_ref, z_ref, cw_ref, gb_ref, galog_ref, mixw_ref,
                       y_ref, tail_ref, state_ref, ubuf_ref, conv_ref, s_ref):
    c = pl.program_id(1)
    t = qkv_ref.shape[0]
    subs = range(t // CHUNK)
    heads = range(GDN_HEADS)

    @pl.when(c == 0)
    def _():
        ubuf_ref[:, 0:SUBLANES, :] = jnp.zeros((GDN_CONV // LANES, SUBLANES, LANES), F32)
        s_ref[...] = jnp.zeros_like(s_ref)

    _causal_conv_tile(qkv_ref, ubuf_ref, cw_ref, None, conv_ref, tail_ref)

    sm = sm_ref[...]
    beta = _sigmoid(sm)
    g = -jnp.exp(galog_ref[...]) * _softplus(sm + gb_ref[...])
    rt = lax.broadcasted_iota(jnp.int32, (t, t), 0)
    ct = lax.broadcasted_iota(jnp.int32, (t, t), 1)
    chunk_causal = (rt >= ct) & (rt // CHUNK == ct // CHUNK)
    gc = _sel_left(chunk_causal.astype(BF16), g)
    gc_t = _transpose_sel(gc)
    eg = jnp.exp(gc)
    ri = lax.broadcasted_iota(jnp.int32, (CHUNK, 2 * CHUNK), 0)
    lane = lax.broadcasted_iota(jnp.int32, (CHUNK, 2 * CHUNK), 1)
    ci = lane % CHUNK
    causal = ri >= ci
    strict = ri > ci
    first_head = lane < CHUNK

    rows = [slice(j * CHUNK, (j + 1) * CHUNK) for j in subs]
    hs = [slice(h * GDN_D, (h + 1) * GDN_D) for h in heads]
    la = [SM_A + h for h in heads]
    packs = [(j, a) for j in subs for a in range(0, GDN_HEADS, 2)]
    q, k, kb, vb, kbg, qg = {}, {}, {}, {}, {}, {}
    for h in heads:
        qf, kf, vf = conv_ref[h], conv_ref[GDN_HEADS + h], conv_ref[2 * GDN_HEADS + h]
        qf = qf * lax.rsqrt(jnp.sum(qf * qf, axis=-1, keepdims=True) + EPS) * (GDN_D ** -0.5)
        kf = kf * lax.rsqrt(jnp.sum(kf * kf, axis=-1, keepdims=True) + EPS)
        b_col = beta[:, SM_B + h:SM_B + h + 1]
        eg_col = eg[:, la[h]:la[h] + 1]
        kbf = kf * b_col
        vbf, kbgf, qgf = (vf * b_col).astype(BF16), (kbf * eg_col).astype(BF16), (qf * eg_col).astype(BF16)
        for j in subs:
            q[j, h], k[j, h], kb[j, h] = qf[rows[j]].astype(BF16), kf[rows[j]], kbf[rows[j]].astype(BF16)
            vb[j, h], kbg[j, h], qg[j, h] = vbf[rows[j]], kbgf[rows[j]], qgf[rows[j]]
    decay, kq = {}, {}
    no_keys = jnp.zeros((CHUNK, GDN_D), BF16)
    for j, a in packs:
        b = a + 1
        col = jnp.where(first_head, gc[rows[j], la[a]:la[a] + 1], gc[rows[j], la[b]:la[b] + 1])
        row = jnp.concatenate([gc_t[la[a]:la[a] + 1, rows[j]], gc_t[la[b]:la[b] + 1, rows[j]]], axis=1)
        decay[j, a] = jnp.where(causal, jnp.exp(jnp.minimum(col - row, 0.0)), 0.0)
        kq[j, a] = (_dot_nt(jnp.concatenate([kb[j, a], q[j, a]], axis=0), jnp.concatenate([k[j, a].astype(BF16), no_keys], axis=0))
                    + _dot_nt(jnp.concatenate([kb[j, b], q[j, b]], axis=0), jnp.concatenate([no_keys, k[j, b].astype(BF16)], axis=0)))
    a_strict = [jnp.where(strict, kq[p][:CHUNK] * decay[p], 0.0) for p in packs]
    attn = {p: _pair_halves((kq[p][CHUNK:] * decay[p]).astype(BF16)) for p in packs}
    t_inv = dict(zip(packs, [_pair_halves(x.astype(BF16)) for x in _unit_lower_inverses(a_strict, ri, ci)]))
    u, wk = {}, {}
    for j, a in packs:
        rhs = jnp.concatenate([jnp.concatenate([vb[j, h], kbg[j, h]], axis=1) for h in (a, a + 1)], axis=0)
        for half, h in enumerate((a, a + 1)):
            uw = _dot(t_inv[j, a][half], rhs)
            u[j, h], wk[j, h] = uw[:, :GDN_D], uw[:, GDN_D:].astype(BF16)

    state = [s_ref[h] for h in heads]
    for j in subs:
        g_last = gc[(j + 1) * CHUNK - 1:(j + 1) * CHUNK, :]
        eend = jnp.exp(g_last - gc[rows[j]])
        elast = jnp.exp(g_last)
        s_b = [x.astype(BF16) for x in state]
        v_new = [(u[j, h] - _dot(wk[j, h], s_b[h])).astype(BF16) for h in heads]
        k_end = [(k[j, h] * eend[:, la[h]:la[h] + 1]).astype(BF16) for h in heads]
        s_inc = [_dot_tn(k_end[h], v_new[h]) for h in heads]
        state = [state[h] * elast[:, la[h]:la[h] + 1] + s_inc[h] for h in heads]
        v_pair = {a: jnp.concatenate([v_new[a], v_new[a + 1]], axis=0) for a in range(0, GDN_HEADS, 2)}
        o = [_dot(qg[j, h], s_b[h]) + _dot(attn[j, h - h % 2][h % 2], v_pair[h - h % 2]) for h in heads]
        msq = [jnp.sum(x * x, axis=-1, keepdims=True) for x in o]
        for h in heads:
            y_ref[rows[j], hs[h]] = _head_norm_gate(o[h], msq[h], GDN_D, z_ref[rows[j], hs[h]], mixw_ref[:, hs[h]]).astype(BF16)
    for h in heads:
        s_ref[h] = state[h]

    @pl.when(c == pl.num_programs(1) - 1)
    def _():
        state_ref[0] = s_ref[...]


def _gdn_prompt(proj, small, batch, cw, gb, galog, mixw, tile, riders):
    rows = proj.shape[0]
    nc = rows // batch // tile
    row = lambda b, c: (b * nc + c, 0)
    whole = lambda b, c: (0, 0)
    return _scan_with_riders(
        "gdn_prompt", (batch, nc), _gdn_prompt_kernel,
        arrays=(proj, small, proj, cw, gb, galog, mixw),
        in_specs=[
            pl.BlockSpec((tile, GDN_CONV), lambda b, c: (b * nc + c, COL_QKV // GDN_CONV)),
            pl.BlockSpec((tile, LANES), row),
            pl.BlockSpec((tile, GDN_W), lambda b, c: (b * nc + c, (COL_Z + SSD_W) // GDN_W)),
            pl.BlockSpec(cw.shape, whole), pl.BlockSpec(gb.shape, whole), pl.BlockSpec(galog.shape, whole),
            pl.BlockSpec(mixw.shape, whole),
        ],
        out_shape=[
            jax.ShapeDtypeStruct((rows, GDN_W), BF16),
            jax.ShapeDtypeStruct((batch, SUBLANES, GDN_CONV), F32),
            jax.ShapeDtypeStruct((batch, GDN_HEADS, GDN_D, GDN_D), F32),
        ],
        out_specs=[
            pl.BlockSpec((tile, GDN_W), row),
            pl.BlockSpec((1, SUBLANES, GDN_CONV), lambda b, c: (b, 0, 0)),
            pl.BlockSpec((1, GDN_HEADS, GDN_D, GDN_D), lambda b, c: (b, 0, 0, 0)),
        ],
        scratch_shapes=[pltpu.VMEM((GDN_CONV // LANES, tile + SUBLANES, LANES), F32),
                        pltpu.VMEM((GDN_CONV // LANES, tile, LANES), F32), pltpu.VMEM((GDN_HEADS, GDN_D, GDN_D), F32)],
        riders=riders)


def _mem_prompt_kernel(q_ref, k_ref, v_ref, z_ref, mixw_ref, y_ref):
    heads = range(MEM_HEADS)
    hs = [slice(h * MEM_D, (h + 1) * MEM_D) for h in heads]
    s = [_dot_nt(q_ref[:, hs[h]].astype(BF16), k_ref[:, hs[h]].astype(BF16)) * (MEM_D ** -0.5) for h in heads]
    e = [jnp.exp(x - jnp.max(x, axis=-1, keepdims=True)) for x in s]
    p = [(x / jnp.sum(x, axis=-1, keepdims=True)).astype(BF16) for x in e]
    o = [_dot(p[h], v_ref[:, hs[h]].astype(BF16)) for h in heads]
    msq = [jnp.sum(x * x, axis=-1, keepdims=True) for x in o]
    for h in heads:
        y_ref[:, hs[h]] = _head_norm_gate(o[h], msq[h], MEM_D, z_ref[:, hs[h]], mixw_ref[:, hs[h]]).astype(BF16)


def _mem_prompt(proj, kv, batch, mixw, tq):
    rows = proj.shape[0]
    nq = rows // batch // tq
    return pl.pallas_call(
        _mem_prompt_kernel,
        grid=(batch, nq),
        in_specs=[
            pl.BlockSpec((tq, MEM_W), lambda b, i: (b * nq + i, COL_QMEM // MEM_W)),
            pl.BlockSpec((MEM_TOKENS, MEM_W), lambda b, i: (b, 0)),
            pl.BlockSpec((MEM_TOKENS, MEM_W), lambda b, i: (b, 1)),
            pl.BlockSpec((tq, MEM_W), lambda b, i: (b * nq + i, (COL_Z + SSD_W + GDN_W) // MEM_W)),
            pl.BlockSpec(mixw.shape, lambda b, i: (0, 0)),
        ],
        out_specs=pl.BlockSpec((tq, MEM_W), lambda b, i: (b * nq + i, 0)),
        out_shape=jax.ShapeDtypeStruct((rows, MEM_W), BF16),
        compiler_params=_params("parallel", "parallel"),
        name="mem_prompt",
    )(proj, kv, kv, proj, mixw)


def _conv_step(u, cst_ref, cst_out_ref, cw_ref, bias):
    acc = cw_ref[CONV_K - 1:CONV_K, :] * u
    if bias is not None:
        acc = acc + bias
    for j in range(CONV_K - 1):
        prev = cst_ref[j]
        acc = acc + cw_ref[j:j + 1, :] * prev
        if j > 0:
            cst_out_ref[j - 1] = prev
    cst_out_ref[CONV_K - 2] = u
    return _silu(acc)


def _rows_to_columns(x):
    pad = jnp.zeros((LANES - x.shape[0], x.shape[1]), F32)
    return jnp.concatenate([x, pad], axis=0).T


def _pick_rows(parts):
    rid = lax.broadcasted_iota(jnp.int32, parts[0].shape, 0)
    out = parts[0]
    for i in range(1, len(parts)):
        out = jnp.where(rid == i, parts[i], out)
    return out


def _ssd_decode_step(xbc_ref, sm_ref, z_ref, cst_ref, st_ref, cw_ref, cb_ref, dtb_ref, alog_ref, dexp_ref,
                     mixw_ref, e_ref, en_ref, y_ref, cst_out_ref, st_out_ref):
    xbc = _conv_step(xbc_ref[...], cst_ref, cst_out_ref, cw_ref, cb_ref[...])
    xs = xbc[:, :SSD_W]
    e = e_ref[...]
    dt = _softplus(sm_ref[...] + dtb_ref[...])
    dec = jnp.exp(dt * (-jnp.exp(alog_ref[...])))
    xd_t = _rows_to_columns(xs * _sel_right(dt, e))
    dec_n = _sel_right(dec, en_ref[...])

    groups, rows, per_group = range(SSD_GROUPS), range(xs.shape[0]), SSD_HEADS // SSD_GROUPS
    b_g = [xbc[:, SSD_W + g * SSD_N:SSD_W + (g + 1) * SSD_N] for g in groups]
    c_g = [xbc[:, SSD_W + (SSD_GROUPS + g) * SSD_N:SSD_W + (SSD_GROUPS + g + 1) * SSD_N].astype(BF16) for g in groups]
    hn = {}
    for g in groups:
        for i in rows:
            for h in range(g * per_group, (g + 1) * per_group):
                col = xd_t[h * SSD_P:(h + 1) * SSD_P, i:i + 1]
                hn[i, h] = st_ref[i, h] * dec_n[i:i + 1, h * SSD_N:(h + 1) * SSD_N] + col * b_g[g][i:i + 1, :]
                st_out_ref[i, h] = hn[i, h]
    hg = {(g, i): jnp.concatenate([hn[i, h] for h in range(g * per_group, (g + 1) * per_group)], axis=0).astype(BF16)
          for g in groups for i in rows}
    y_rows = {p: _dot_nt(c_g[p[0]], hg[p]) for p in hg}
    y = jnp.concatenate([_pick_rows([y_rows[g, i] for i in rows]) for g in groups], axis=1) + dexp_ref[...] * xs
    msq = _sel_right(_sel_right_nt(y * y, e, 2), e, 2)
    y_ref[...] = _head_norm_gate(y, msq, SSD_P, z_ref[...], mixw_ref[...])


def _gdn_decode_step(qkv_ref, sm_ref, z_ref, cst_ref, st_ref, cw_ref, gb_ref, galog_ref, mixw_ref, en_ref,
                     y_ref, cst_out_ref, st_out_ref):
    qkv = _conv_step(qkv_ref[...], cst_ref, cst_out_ref, cw_ref, None)
    sm = sm_ref[...]
    beta = _sigmoid(sm)
    eg = jnp.exp(-jnp.exp(galog_ref[...]) * _softplus(sm + gb_ref[...]))
    eg_n = _sel_right(eg, en_ref[...])

    qs, ks = [], []
    for h in range(GDN_HEADS):
        q = qkv[:, h * GDN_D:(h + 1) * GDN_D]
        k = qkv[:, GDN_W + h * GDN_D:GDN_W + (h + 1) * GDN_D]
        qs.append(q * lax.rsqrt(jnp.sum(q * q, axis=-1, keepdims=True) + EPS) * (GDN_D ** -0.5))
        ks.append(k * lax.rsqrt(jnp.sum(k * k, axis=-1, keepdims=True) + EPS))
    k_t = _rows_to_columns(jnp.concatenate(ks, axis=1))

    heads, rows = range(GDN_HEADS), range(sm.shape[0])
    n = len(rows)
    hs = [slice(h * GDN_D, (h + 1) * GDN_D) for h in heads]
    kq_b = [jnp.concatenate([ks[h], qs[h]], axis=0).astype(BF16) for h in heads]
    prod = {(h, i): _dot(kq_b[h], st_ref[i, h].astype(BF16)) for h in heads for i in rows}
    k_s = [_pick_rows([prod[h, i][:n] for i in rows]) for h in heads]
    q_s = [_pick_rows([prod[h, i][n:] for i in rows]) for h in heads]
    eg_h = [eg_n[:, hs[h]] for h in heads]
    v = [qkv[:, 2 * GDN_W + h * GDN_D:2 * GDN_W + (h + 1) * GDN_D] for h in heads]
    v_new = [beta[:, SM_B + h:SM_B + h + 1] * (v[h] - eg_h[h] * k_s[h]) for h in heads]
    o = [eg_h[h] * q_s[h] + jnp.sum(qs[h] * ks[h], axis=-1, keepdims=True) * v_new[h] for h in heads]
    for h in heads:
        for i in rows:
            st_out_ref[i, h] = st_ref[i, h] * eg_h[h][i:i + 1, :] + k_t[hs[h], i:i + 1] * v_new[h][i:i + 1, :]
    msq = [jnp.sum(x * x, axis=-1, keepdims=True) for x in o]
    for h in heads:
        y_ref[:, hs[h]] = _head_norm_gate(o[h], msq[h], GDN_D, z_ref[:, hs[h]], mixw_ref[:, hs[h]])


def _mem_decode_step(q_ref, k_ref, v_ref, z_ref, mixw_ref, y_ref):
    heads, rows = range(MEM_HEADS), range(q_ref.shape[0])
    hs = [slice(h * MEM_D, (h + 1) * MEM_D) for h in heads]
    win = [pl.ds(h, MEM_TOKENS, stride=MEM_HEADS) for h in heads]
    q = [q_ref[:, hs[h]].astype(BF16) for h in heads]
    s = [_pick_rows([_dot_nt(q[h], k_ref[i, win[h], :].astype(BF16)) for i in rows]) * (MEM_D ** -0.5) for h in heads]
    e = [jnp.exp(x - jnp.max(x, axis=-1, keepdims=True)) for x in s]
    p = [(x / jnp.sum(x, axis=-1, keepdims=True)).astype(BF16) for x in e]
    o = [_pick_rows([_dot(p[h], v_ref[i, win[h], :].astype(BF16)) for i in rows]) for h in heads]
    msq = [jnp.sum(x * x, axis=-1, keepdims=True) for x in o]
    for h in heads:
        y_ref[:, hs[h]] = _head_norm_gate(o[h], msq[h], MEM_D, z_ref[:, hs[h]], mixw_ref[:, hs[h]])


DEC_ROWS = SUBLANES


def _ssd_decode_kernel(proj_ref, sm_ref, cst_ref, st_ref, cw_ref, cb_ref, dtb_ref, alog_ref, dexp_ref, mixw_ref, e_ref, en_ref,
                       y_ref, cst_out_ref, st_out_ref):
    _ssd_decode_step(proj_ref.at[:, COL_XBC:COL_XBC + SSD_CONV], sm_ref, proj_ref.at[:, COL_Z:COL_Z + SSD_W], cst_ref, st_ref,
                     cw_ref, cb_ref, dtb_ref, alog_ref, dexp_ref, mixw_ref.at[:, :SSD_W], e_ref, en_ref,
                     y_ref, cst_out_ref, st_out_ref)


def _gdn_decode_kernel(proj_ref, sm_ref, cst_ref, st_ref, cw_ref, gb_ref, galog_ref, mixw_ref, en_ref,
                       y_ref, cst_out_ref, st_out_ref):
    _gdn_decode_step(proj_ref.at[:, COL_QKV:COL_QKV + GDN_CONV], sm_ref, proj_ref.at[:, COL_Z + SSD_W:COL_Z + SSD_W + GDN_W],
                     cst_ref, st_ref, cw_ref, gb_ref, galog_ref, mixw_ref.at[:, SSD_W:SSD_W + GDN_W], en_ref,
                     y_ref, cst_out_ref, st_out_ref)


def _ssd_mem_decode_kernel(proj_ref, sm_ref, cst_ref, st_ref, k_ref, v_ref, cw_ref, cb_ref, dtb_ref, alog_ref, dexp_ref,
                           mixw_ref, e_ref, en_ref, y_ssd_ref, y_mem_ref, cst_out_ref, st_out_ref):
    _mem_decode_step(proj_ref.at[:, COL_QMEM:COL_QMEM + MEM_W], k_ref, v_ref, proj_ref.at[:, N_MAIN - MEM_W:N_MAIN],
                     mixw_ref.at[:, SSD_W + GDN_W:], y_mem_ref)
    _ssd_decode_kernel(proj_ref, sm_ref, cst_ref, st_ref, cw_ref, cb_ref, dtb_ref, alog_ref, dexp_ref, mixw_ref, e_ref, en_ref,
                       y_ssd_ref, cst_out_ref, st_out_ref)


def _decode_rider(kernel, per_row_in, consts, out_shape, steps, step_index, early=()):
    groups = per_row_in[0].shape[0] // DEC_ROWS
    assert steps % groups == 0
    per_group = steps // groups

    def group_spec(a, lead=0):
        axis = 1 if len(a.shape) == 3 and a.shape[0] == CONV_K - 1 else 0
        block = a.shape[:axis] + (DEC_ROWS,) + a.shape[axis + 1:]
        group_of = lambda *g: jnp.minimum((step_index(*g) + lead) // per_group, groups - 1)
        return pl.BlockSpec(block, lambda *g: (0,) * axis + (group_of(*g),) + (0,) * (len(block) - axis - 1))

    whole = lambda a: pl.BlockSpec(a.shape, lambda *g, nd=a.ndim: (0,) * nd)
    lead = lambda n: min(1, per_group - 1) if n in early else 0
    return dict(kernel=kernel, arrays=tuple(per_row_in) + tuple(consts), steps_per_group=per_group,
                in_specs=[group_spec(a, lead(n)) for n, a in enumerate(per_row_in)] + [whole(a) for a in consts],
                out_shape=list(out_shape), out_specs=[group_spec(a) for a in out_shape])


def _scan_with_riders_kernel(*refs, scan_kernel, n_scan_in, n_scan_out, riders):
    scan_in, refs = refs[:n_scan_in], refs[n_scan_in:]
    rider_in = []
    for _, n_in, _, _ in riders:
        rider_in.append(refs[:n_in])
        refs = refs[n_in:]
    scan_out, refs = refs[:n_scan_out], refs[n_scan_out:]
    rider_out = []
    for _, _, n_out, _ in riders:
        rider_out.append(refs[:n_out])
        refs = refs[n_out:]
    scan_kernel(*scan_in, *scan_out, *refs)
    step = pl.program_id(0) * pl.num_programs(1) + pl.program_id(1)
    for (kernel, _, _, per_group), ins, outs in zip(riders, rider_in, rider_out):
        if per_group == 1:
            kernel(*ins, *outs)
        else:
            pl.when(step % per_group == 0)(functools.partial(kernel, *ins, *outs))


def _scan_with_riders(name, grid, scan_kernel, arrays, in_specs, out_shape, out_specs, scratch_shapes, riders):
    body = functools.partial(
        _scan_with_riders_kernel, scan_kernel=scan_kernel, n_scan_in=len(arrays), n_scan_out=len(out_shape),
        riders=tuple((r["kernel"], len(r["arrays"]), len(r["out_shape"]), r["steps_per_group"]) for r in riders))
    outs = pl.pallas_call(
        body,
        grid=grid,
        in_specs=list(in_specs) + [s for r in riders for s in r["in_specs"]],
        out_specs=list(out_specs) + [s for r in riders for s in r["out_specs"]],
        out_shape=list(out_shape) + [s for r in riders for s in r["out_shape"]],
        scratch_shapes=scratch_shapes,
        compiler_params=_params("arbitrary", "arbitrary"),
        name=name,
    )(*arrays, *[a for r in riders for a in r["arrays"]])
    scan_res, outs = outs[:len(out_shape)], outs[len(out_shape):]
    rider_res = []
    for r in riders:
        rider_res.append(outs[:len(r["out_shape"])])
        outs = outs[len(r["out_shape"]):]
    return scan_res, rider_res


IN_DT = SSD_CONV
IN_QKV = IN_DT + SSD_HEADS
IN_B = IN_QKV + GDN_CONV
IN_QMEM = IN_B + 2 * GDN_HEADS
IN_COLS = IN_QMEM + MEM_W + MIX_W
PREP_COLS = 256


def _prep_w_in_kernel(w_ref, x_ref, nw_ref, main_ref, small_ref, o_ref, os_ref, h_ref):
    i = pl.program_id(0)
    nblk = h_ref.shape[0]

    @pl.when(i == 0)
    def _():
        h = _rmsnorm_bf16(x_ref[...], nw_ref[...])
        for kb in range(nblk):
            h_ref[kb] = h[:, kb * PREP_COLS:(kb + 1) * PREP_COLS]
        o_ref[...] = jnp.zeros_like(o_ref)
        os_ref[...] = jnp.zeros_like(os_ref)

    main_ref[COL_QKV:COL_QKV + GDN_CONV, :] = w_ref[IN_QKV:IN_B, :].astype(BF16)
    main_ref[COL_XBC:COL_XBC + SSD_CONV, :] = w_ref[:SSD_CONV, :].astype(BF16)
    main_ref[COL_QMEM:N_MAIN, :] = w_ref[IN_QMEM:IN_COLS, :].astype(BF16)
    small_ref[SM_DT:SM_B, :] = w_ref[IN_DT:IN_QKV, :].astype(BF16)
    small_ref[SM_B:SM_A + GDN_HEADS, :] = w_ref[IN_B:IN_QMEM, :].astype(BF16)
    small_ref[SM_A + GDN_HEADS:, :] = jnp.zeros((LANES - SM_A - GDN_HEADS, w_ref.shape[1]), BF16)

    hb = h_ref[i]
    o_ref[...] += _dot_nt(hb, main_ref[...])
    os_ref[...] += _dot_nt(hb, small_ref[...])


def _prep_w_in(w_t, x, nw):
    n, k = w_t.shape
    rows = x.shape[0]
    assert n == IN_COLS
    whole = lambda i: (0, 0)
    return pl.pallas_call(
        _prep_w_in_kernel,
        grid=(k // PREP_COLS,),
        in_specs=[pl.BlockSpec((IN_COLS, PREP_COLS), lambda i: (0, i)), pl.BlockSpec((rows, k), whole),
                  pl.BlockSpec((1, k), whole)],
        out_specs=[pl.BlockSpec((N_MAIN, PREP_COLS), lambda i: (0, i)), pl.BlockSpec((LANES, PREP_COLS), lambda i: (0, i)),
                   pl.BlockSpec((rows, N_MAIN), whole), pl.BlockSpec((rows, LANES), whole)],
        out_shape=[jax.ShapeDtypeStruct((N_MAIN, k), BF16), jax.ShapeDtypeStruct((LANES, k), BF16),
                   jax.ShapeDtypeStruct((rows, N_MAIN), F32), jax.ShapeDtypeStruct((rows, LANES), F32)],
        scratch_shapes=[pltpu.VMEM((k // PREP_COLS, rows, PREP_COLS), BF16)],
        compiler_params=_params("arbitrary"),
        name="prep_w_in",
    )(w_t, x, nw)


def _head_expander(heads, first_lane, width):
    m = np.zeros((LANES, heads * width), np.float32)
    for h in range(heads):
        m[first_lane + h, h * width:(h + 1) * width] = 1.0
    return jnp.asarray(m, BF16)


def _lane_row(vec, first_lane):
    return jnp.zeros((1, LANES), F32).at[0, first_lane:first_lane + vec.shape[0]].set(vec.astype(F32))


def kernel(x_prompt, x_sample, mem_prompt, state_ssd_conv, state_ssd, state_gdn_conv, state_gdn, cache_mem_k, cache_mem_v, norm_w, w_in, ssd_conv_w, ssd_conv_b, ssd_dt_bias, ssd_A_log, ssd_D, gdn_conv_w, gdn_dt_bias, gdn_A_log, mem_norm_w, w_mem_kv, mix_norm_w, w_out, final_norm_w):
    bp, seq, d = x_prompt.shape
    bs = x_sample.shape[0]
    assert (d, seq % CHUNK, norm_w.shape[0]) == (D_MODEL, 0, 1)

    nw = norm_w[0][None, :]
    xs = x_sample.reshape(bs, d)
    w_main, w_small, proj_s, small_s = _prep_w_in(w_in[0].T, xs, nw)
    mixw =mix_norm_w[0][None, :]
    mixw1, mixw2, mixw3 = mixw[:, :SSD_W], mixw[:, SSD_W:SSD_W + GDN_W], mixw[:, SSD_W + GDN_W:]
    fw = final_norm_w[None, :]
    ssd_dtb = _lane_row(ssd_dt_bias[0], SM_DT)
    ssd_alog = _lane_row(ssd_A_log[0], SM_DT)
    ssd_dexp = jnp.repeat(ssd_D[0].astype(F32), SSD_P)[None, :]
    gdn_b = _lane_row(gdn_dt_bias[0], SM_A)
    gdn_alog = _lane_row(gdn_A_log[0], SM_A)
    e_ssd = _head_expander(SSD_HEADS, SM_DT, SSD_P)
    e_ssd_n = _head_expander(SSD_HEADS, SM_DT, SSD_N)
    e_gdn_n = _head_expander(GDN_HEADS, SM_A, GDN_D)
    ssd_cw, ssd_cb, gdn_cw = ssd_conv_w[0], ssd_conv_b[0][None, :], gdn_conv_w[0]

    xp = x_prompt.reshape(bp * seq, d)
    proj_p, small_p, wo, w_kv = _norm_matmul(
        xp, nw, w_main, w_small, _row_tile(bp * seq, PROJ_ROWS), SSD_CONV, casts=(w_out[0], w_mem_kv[0]))
    kv = _mem_kv(mem_prompt.reshape(bp * MEM_TOKENS, d), mem_norm_w[0][None, :], w_kv, MEM_TOKENS)
    scan_rows = SCAN_ROWS if seq % SCAN_ROWS == 0 else CHUNK
    scan_steps = bp * seq // scan_rows
    step_index = lambda b, c: b * (seq // scan_rows) + c

    ssd_cst, gdn_cst = jnp.swapaxes(state_ssd_conv[0], 0, 1), jnp.swapaxes(state_gdn_conv[0], 0, 1)
    mem_k = cache_mem_k.reshape(bs, MEM_TOKENS * MEM_HEADS, MEM_D)
    mem_v = cache_mem_v.reshape(bs, MEM_TOKENS * MEM_HEADS, MEM_D)
    row_out = lambda width: jax.ShapeDtypeStruct((bs, width), F32)
    like = lambda a: jax.ShapeDtypeStruct(a.shape, F32)
    ssd_rider = _decode_rider(
        _ssd_mem_decode_kernel, (proj_s, small_s, ssd_cst, state_ssd[0], mem_k, mem_v),
        (ssd_cw, ssd_cb, ssd_dtb, ssd_alog, ssd_dexp, mixw, e_ssd, e_ssd_n),
        (row_out(SSD_W), row_out(MEM_W), like(ssd_cst), like(state_ssd[0])), scan_steps, step_index, early=(4, 5))
    out_rows = _row_tile(bp * seq, OUT_ROWS)
    gdn_rider = _decode_rider(
        _gdn_decode_kernel, (proj_s, small_s, gdn_cst, state_gdn[0]), (gdn_cw, gdn_b, gdn_alog, mixw, e_gdn_n),
        (row_out(GDN_W), like(gdn_cst), like(state_gdn[0])), bp * seq // out_rows, lambda i, _: i)

    (y_ssd, tail_ssd, p_ssd), ((ys_ssd, ys_mem, s_ssd_conv, s_ssd),) = _ssd_prompt(
        proj_p, small_p, bp, ssd_cw, ssd_cb, ssd_dtb, ssd_alog, ssd_dexp, mixw1, e_ssd, scan_rows, (ssd_rider,))
    (y_gdn, tail_gdn, p_gdn), _ = _gdn_prompt(proj_p, small_p, bp, gdn_cw, gdn_b, gdn_alog, mixw2, scan_rows, ())
    y_mem = _mem_prompt(proj_p, kv, bp, mixw3, _row_tile(seq, MEM_Q_ROWS))
    y_prompt, ((ys_gdn, s_gdn_conv, s_gdn),) = _out_proj(y_ssd, y_gdn, y_mem, wo, xp, fw, out_rows, (gdn_rider,))
    y_prompt = y_prompt.reshape(bp, seq, d)
    y_sample = _out_proj(ys_ssd, ys_gdn, ys_mem, wo, xs, fw, bs)[0].reshape(bs, 1, d)

    keep = CONV_K - 1
    mem_shape = (1, bp, MEM_TOKENS, MEM_HEADS, MEM_D)
    return (
        y_prompt, y_sample,
        tail_ssd[None, :, SUBLANES - keep:, :], p_ssd[None],
        tail_gdn[None, :, SUBLANES - keep:, :], p_gdn[None],
        kv[:, :MEM_W].reshape(mem_shape), kv[:, MEM_W:].reshape(mem_shape),
        jnp.swapaxes(s_ssd_conv, 0, 1)[None], s_ssd[None],
        jnp.swapaxes(s_gdn_conv, 0, 1)[None], s_gdn[None],
    )
```

```python
import functools

import numpy as np
import jax
import jax.numpy as jnp
from jax import lax
from jax.experimental import pallas as pl
from jax.experimental.pallas import tpu as pltpu

F32, BF16 = jnp.float32, jnp.bfloat16

D_MODEL = 2048
SSD_HEADS, SSD_P, SSD_GROUPS, SSD_N = 16, 64, 2, 128
SSD_W = SSD_HEADS * SSD_P
SSD_GW = SSD_W // SSD_GROUPS
SSD_CONV = SSD_W + 2 * SSD_GROUPS * SSD_N
GDN_HEADS, GDN_D = 8, 128
GDN_W = GDN_HEADS * GDN_D
GDN_CONV = 3 * GDN_W
MEM_TOKENS, MEM_HEADS, MEM_D = 256, 4, 128
MEM_W = MEM_HEADS * MEM_D
MIX_W = SSD_W + GDN_W + MEM_W
CONV_K = 4
CHUNK = 64
EPS = 1e-6

LANES = 128
SUBLANES = 8
VMEM_LIMIT = 60 * 1024 * 1024
PROJ_ROWS = 1024
OUT_ROWS = 512
MEM_Q_ROWS = 512
SCAN_ROWS = 256
CONV_PHASES = 4

COL_QKV = 0
COL_XBC = COL_QKV + GDN_CONV
COL_QMEM = COL_XBC + SSD_CONV
COL_Z = COL_QMEM + MEM_W
N_MAIN = COL_Z + MIX_W
SM_DT, SM_B, SM_A = 0, SSD_HEADS, SSD_HEADS + GDN_HEADS


def _dot(a, b):
    return jnp.dot(a, b, preferred_element_type=F32)


def _dot_nt(a, b):
    return lax.dot_general(a, b, (((1,), (1,)), ((), ())), preferred_element_type=F32)


def _dot_tn(a, b):
    return lax.dot_general(a, b, (((0,), (0,)), ((), ())), preferred_element_type=F32)


def _split(x, n):
    parts, r = [], x
    for i in range(n):
        p = r.astype(BF16)
        parts.append(p)
        if i + 1 < n:
            r = r - p.astype(F32)
    return parts


def _sel_left(sel, x, n=3):
    return functools.reduce(lambda a, b: a + b, [_dot(sel, p) for p in _split(x, n)])


def _sel_right(x, sel, n=3):
    return functools.reduce(lambda a, b: a + b, [_dot(p, sel) for p in _split(x, n)])


def _sel_right_nt(x, sel, n=3):
    return functools.reduce(lambda a, b: a + b, [_dot_nt(p, sel) for p in _split(x, n)])


def _transpose_sel(x, n=3):
    eye = _eye(LANES).astype(BF16)
    return functools.reduce(lambda a, b: a + b, [_dot_nt(eye, p) for p in _split(x, n)])


def _eye(n):
    return (lax.broadcasted_iota(jnp.int32, (n, n), 0) == lax.broadcasted_iota(jnp.int32, (n, n), 1)).astype(F32)


def _sigmoid(x):
    return 0.5 * jnp.tanh(0.5 * x) + 0.5


def _silu(x):
    h = 0.5 * x
    return h + h * jnp.tanh(h)


def _softplus(x):
    return jnp.maximum(x, 0.0) + jnp.log1p(jnp.exp(-jnp.abs(x)))


def _params(*sem):
    return pltpu.CompilerParams(dimension_semantics=sem, vmem_limit_bytes=VMEM_LIMIT)


def _row_tile(rows, preferred):
    return preferred if rows % preferred == 0 else rows


def _rmsnorm_bf16(x, nw):
    ms = jnp.mean(x * x, axis=-1, keepdims=True)
    return (x * lax.rsqrt(ms + EPS) * nw).astype(BF16)


def _norm_matmul_kernel(x_ref, nw_ref, w_ref, ws_ref, *refs, n_casts):
    cast_in, (o_ref, os_ref), cast_out, (h_ref,) = (
        refs[:n_casts], refs[n_casts:n_casts + 2], refs[n_casts + 2:2 * n_casts + 2], refs[2 * n_casts + 2:])

    @pl.when(pl.program_id(1) == 0)
    def _():
        h = _rmsnorm_bf16(x_ref[...], nw_ref[...])
        h_ref[...] = h
        os_ref[...] = _dot_nt(h, ws_ref[...])

    o_ref[...] = _dot_nt(h_ref[...], w_ref[...])
    for src, dst in zip(cast_in, cast_out):
        dst[...] = src[...].astype(BF16)


def _cast_block_rows(rows, steps):
    aligned = [r for r in range(2 * SUBLANES, rows + 1, 2 * SUBLANES) if rows % r == 0 and rows // r <= steps]
    return aligned[0] if aligned else rows


def _norm_matmul(x, nw, w_t, ws_t, tm, tn, casts=()):
    m, k = x.shape
    n = w_t.shape[0]
    ns = ws_t.shape[0]
    nj = n // tn
    steps = (m // tm) * nj

    def cast_spec(a):
        r = _cast_block_rows(a.shape[0], steps)
        return pl.BlockSpec((r, a.shape[1]), lambda i, j: (jnp.minimum(i * nj + j, a.shape[0] // r - 1), 0))

    return pl.pallas_call(
        functools.partial(_norm_matmul_kernel, n_casts=len(casts)),
        grid=(m // tm, nj),
        in_specs=[
            pl.BlockSpec((tm, k), lambda i, j: (i, 0)),
            pl.BlockSpec((1, k), lambda i, j: (0, 0)),
            pl.BlockSpec((tn, k), lambda i, j: (j, 0)),
            pl.BlockSpec((ns, k), lambda i, j: (0, 0)),
        ] + [cast_spec(a) for a in casts],
        out_specs=[
            pl.BlockSpec((tm, tn), lambda i, j: (i, j)),
            pl.BlockSpec((tm, ns), lambda i, j: (i, 0)),
        ] + [cast_spec(a) for a in casts],
        out_shape=[jax.ShapeDtypeStruct((m, n), F32), jax.ShapeDtypeStruct((m, ns), F32)]
        + [jax.ShapeDtypeStruct(a.shape, BF16) for a in casts],
        scratch_shapes=[pltpu.VMEM((tm, k), BF16)],
        compiler_params=_params("arbitrary", "arbitrary"),
        name="norm_matmul",
    )(x, nw, w_t, ws_t, *casts)


def _mem_kv_kernel(x_ref, nw_ref, w_ref, o_ref, k_ref, v_ref):
    kv = _dot(_rmsnorm_bf16(x_ref[...], nw_ref[...]), w_ref[...])
    o_ref[...] = kv
    for half, ref in enumerate((k_ref, v_ref)):
        for h in range(MEM_HEADS):
            ref[pl.ds(h, x_ref.shape[0], stride=MEM_HEADS), :] = kv[:, half * MEM_W + h * MEM_D:half * MEM_W + (h + 1) * MEM_D]


def _mem_kv(x, nw, w, tm):
    m, k = x.shape
    n = w.shape[1]
    assert n == 2 * MEM_W
    per_head = pl.BlockSpec((tm * MEM_HEADS, MEM_D), lambda i: (i, 0))
    return pl.pallas_call(
        _mem_kv_kernel,
        grid=(m // tm,),
        in_specs=[pl.BlockSpec((tm, k), lambda i: (i, 0)), pl.BlockSpec((1, k), lambda i: (0, 0)),
                  pl.BlockSpec((k, n), lambda i: (0, 0))],
        out_specs=[pl.BlockSpec((tm, n), lambda i: (i, 0)), per_head, per_head],
        out_shape=[jax.ShapeDtypeStruct((m, n), F32)] + [jax.ShapeDtypeStruct((m * MEM_HEADS, MEM_D), F32)] * 2,
        compiler_params=_params("parallel"),
        name="mem_kv",
    )(x, nw, w)


def _out_proj_kernel(y1_ref, y2_ref, y3_ref, w_ref, x_ref, fw_ref, o_ref):
    n1, n2 = y1_ref.shape[1], y1_ref.shape[1] + y2_ref.shape[1]
    acc = (_dot(y1_ref[...].astype(BF16), w_ref[:n1, :]) + _dot(y2_ref[...].astype(BF16), w_ref[n1:n2, :])
           + _dot(y3_ref[...].astype(BF16), w_ref[n2:, :]))
    r = x_ref[...] + acc
    ms = jnp.mean(r * r, axis=-1, keepdims=True)
    o_ref[...] = r * lax.rsqrt(ms + EPS) * fw_ref[...]


def _out_proj(y1, y2, y3, w, x, fw, tm, riders=()):
    m, d = x.shape
    row = lambda i, _: (i, 0)
    whole = lambda i, _: (0, 0)
    (out,), rider_res = _scan_with_riders(
        "out_proj", (m // tm, 1), _out_proj_kernel,
        arrays=(y1, y2, y3, w, x, fw),
        in_specs=[
            pl.BlockSpec((tm, y1.shape[1]), row), pl.BlockSpec((tm, y2.shape[1]), row), pl.BlockSpec((tm, y3.shape[1]), row),
            pl.BlockSpec(w.shape, whole, pipeline_mode=pl.Buffered(1)), pl.BlockSpec((tm, d), row), pl.BlockSpec((1, d), whole),
        ],
        out_shape=[jax.ShapeDtypeStruct((m, d), F32)],
        out_specs=[pl.BlockSpec((tm, d), row)],
        scratch_shapes=[],
        riders=riders)
    return out, rider_res


def _causal_conv_tile(u_ref, ubuf_ref, cw_ref, cb_ref, out_ref, tail_ref):
    t, width = u_ref.shape
    n = t // CONV_PHASES
    for s in range(width // LANES):
        cs = slice(s * LANES, (s + 1) * LANES)
        ubuf_ref[s, SUBLANES:SUBLANES + t, :] = u_ref[:, cs]
        taps = {d: ubuf_ref[s, pl.ds(SUBLANES + d, n, stride=CONV_PHASES), :] for d in range(1 - CONV_K, CONV_PHASES)}
        w = [cw_ref[j:j + 1, cs] for j in range(CONV_K)]
        for r in range(CONV_PHASES):
            acc = w[CONV_K - 1] * taps[r]
            if cb_ref is not None:
                acc = acc + cb_ref[:, cs]
            for j in range(CONV_K - 1):
                acc = acc + w[j] * taps[r - (CONV_K - 1) + j]
            out_ref[s, pl.ds(r, n, stride=CONV_PHASES), :] = _silu(acc)
        tail = ubuf_ref[s, t:t + SUBLANES, :]
        ubuf_ref[s, 0:SUBLANES, :] = tail
        tail_ref[0, :, cs] = tail


def _head_norm_gate(y, msq, width, z, mixw):
    return y * lax.rsqrt(msq * (1.0 / width) + EPS) * mixw * _silu(z)


def _ssd_prompt_kernel(xbc_ref, sm_ref, z_ref, cw_ref, cb_ref, dtb_ref, alog_ref, dexp_ref, mixw_ref, e_ref,
                       y_ref, tail_ref, state_ref, ubuf_ref, conv_ref, h_ref):
    c = pl.program_id(1)
    t = xbc_ref.shape[0]
    subs = range(t // CHUNK)
    groups = range(SSD_GROUPS)
    blocks = range(SSD_GW // LANES)

    @pl.when(c == 0)
    def _():
        ubuf_ref[:, 0:SUBLANES, :] = jnp.zeros((SSD_CONV // LANES, SUBLANES, LANES), F32)
        h_ref[...] = jnp.zeros_like(h_ref)

    _causal_conv_tile(xbc_ref, ubuf_ref, cw_ref, cb_ref, conv_ref, tail_ref)
    xs = jnp.concatenate([conv_ref[s] for s in range(SSD_W // LANES)], axis=1)
    e = e_ref[...]
    rows = [slice(j * CHUNK, (j + 1) * CHUNK) for j in subs]
    gs = [slice(g * SSD_GW, (g + 1) * SSD_GW) for g in groups]

    dt = _softplus(sm_ref[...] + dtb_ref[...])
    a = dt * (-jnp.exp(alog_ref[...]))
    rt = lax.broadcasted_iota(jnp.int32, (t, t), 0)
    ct = lax.broadcasted_iota(jnp.int32, (t, t), 1)
    chunk_causal = (rt >= ct) & (rt // CHUNK == ct // CHUNK)
    cum = _sel_left(chunk_causal.astype(BF16), a)
    cum_t = _transpose_sel(cum)
    ecum = jnp.exp(cum)
    wend = jnp.concatenate([jnp.exp(cum[(j + 1) * CHUNK - 1:(j + 1) * CHUNK, :] - cum[rows[j]]) for j in subs], axis=0)
    dt_x = _sel_right(dt, e, 2)
    ecum_x = _sel_right(ecum, e, 2)
    wend_x = _sel_right(wend, e, 2)

    xdt = xs * dt_x
    xdt_b = xdt.astype(BF16)
    xw_b = (xdt * wend_x).astype(BF16)
    lane = lax.broadcasted_iota(jnp.int32, (CHUNK, LANES), 1)
    causal = lax.broadcasted_iota(jnp.int32, (CHUNK, LANES), 0) >= lane % CHUNK
    first_head = lane < SSD_P

    jg = [(j, g) for j in subs for g in groups]
    b_slab, c_slab = SSD_W // LANES, SSD_W // LANES + SSD_GROUPS
    bmat = {(j, g): conv_ref[b_slab + g, rows[j], :].astype(BF16) for j, g in jg}
    cmat = {(j, g): conv_ref[c_slab + g, rows[j], :].astype(BF16) for j, g in jg}
    cb = {p: _dot_nt(cmat[p], jnp.concatenate([bmat[p], bmat[p]], axis=0)) for p in jg}
    inc = {(j, g): _dot_tn(bmat[j, g], xw_b[rows[j], gs[g]]) for j, g in jg}
    intra = {}
    for j, g in jg:
        for blk in blocks:
            a = (g * len(blocks) + blk) * 2
            col = jnp.where(first_head, cum[rows[j], a:a + 1], cum[rows[j], a + 1:a + 2])
            row = jnp.concatenate([cum_t[a:a + 1, rows[j]], cum_t[a + 1:a + 2, rows[j]]], axis=1)
            lmat = jnp.where(causal, jnp.exp(jnp.minimum(col - row, 0.0)), 0.0)
            lanes = slice(g * SSD_GW + blk * LANES, g * SSD_GW + (blk + 1) * LANES)
            intra[j, g, blk] = _dot((cb[j, g] * lmat).astype(BF16), _pair_diag(xdt_b[rows[j], lanes]))

    state = {(0, g): h_ref[:, gs[g]] for g in groups}
    for j in subs:
        last = (j + 1) * CHUNK - 1
        for g in groups:
            state[j + 1, g] = state[j, g] * ecum_x[last:last + 1, gs[g]] + inc[j, g]
    for g in groups:
        h_ref[:, gs[g]] = state[len(subs), g]
    inter = {p: _dot(cmat[p], state[p].astype(BF16)) for p in jg}
    inter_x = jnp.concatenate([jnp.concatenate([inter[j, g] for g in groups], axis=1) for j in subs], axis=0) * ecum_x
    intra_x = jnp.concatenate([jnp.concatenate([intra[j, g, blk] for g in groups for blk in blocks], axis=1)
                               for j in subs], axis=0)
    y = intra_x + inter_x + dexp_ref[...] * xs
    msq = _sel_right(_sel_right_nt(y * y, e, 2), e, 2)
    y_ref[...] = _head_norm_gate(y, msq, SSD_P, z_ref[...], mixw_ref[...]).astype(BF16)

    @pl.when(c == pl.num_programs(1) - 1)
    def _():
        state_ref[0] = h_ref[...].T.reshape(SSD_HEADS, SSD_P, SSD_N)


def _ssd_prompt(proj, small, batch, cw, cb, dtb, alog, dexp, mixw, e, tile, riders):
    rows = proj.shape[0]
    nc = rows // batch // tile
    row = lambda b, c: (b * nc + c, 0)
    whole = lambda b, c: (0, 0)
    return _scan_with_riders(
        "ssd_prompt", (batch, nc), _ssd_prompt_kernel,
        arrays=(proj, small, proj, cw, cb, dtb, alog, dexp, mixw, e),
        in_specs=[
            pl.BlockSpec((tile, SSD_CONV), lambda b, c: (b * nc + c, COL_XBC // SSD_CONV)),
            pl.BlockSpec((tile, LANES), row),
            pl.BlockSpec((tile, SSD_W), lambda b, c: (b * nc + c, COL_Z // SSD_W)),
            pl.BlockSpec(cw.shape, whole), pl.BlockSpec(cb.shape, whole), pl.BlockSpec(dtb.shape, whole),
            pl.BlockSpec(alog.shape, whole), pl.BlockSpec(dexp.shape, whole),
            pl.BlockSpec((1, SSD_W), whole),
            pl.BlockSpec(e.shape, whole),
        ],
        out_shape=[
            jax.ShapeDtypeStruct((rows, SSD_W), BF16),
            jax.ShapeDtypeStruct((batch, SUBLANES, SSD_CONV), F32),
            jax.ShapeDtypeStruct((batch, SSD_HEADS, SSD_P, SSD_N), F32),
        ],
        out_specs=[
            pl.BlockSpec((tile, SSD_W), row),
            pl.BlockSpec((1, SUBLANES, SSD_CONV), lambda b, c: (b, 0, 0)),
            pl.BlockSpec((1, SSD_HEADS, SSD_P, SSD_N), lambda b, c: (b, 0, 0, 0)),
        ],
        scratch_shapes=[pltpu.VMEM((SSD_CONV // LANES, tile + SUBLANES, LANES), F32),
                        pltpu.VMEM((SSD_CONV // LANES, tile, LANES), F32), pltpu.VMEM((SSD_N, SSD_W), F32)],
        riders=riders)


def _unit_lower_inverses(a_stricts, ri, ci):
    t = a_stricts[0].shape[0]
    eye = (ri == ci).astype(F32)
    first = (ri == ci + 1) & (ci % 2 == 0)
    invs = [eye - jnp.where(first, a, 0.0) for a in a_stricts]
    a_bs = [a.astype(BF16) for a in a_stricts]
    zero = jnp.zeros(a_bs[0].shape, BF16)
    s = 2
    while s < t:
        sel = (ri // (2 * s) == ci // (2 * s)) & ((ri // s) % 2 == 1) & ((ci // s) % 2 == 0)
        inv_bs = [inv.astype(BF16) for inv in invs]
        lefts = [_dot(inv_b, _pair_diag(jnp.where(sel, a_b, zero))).astype(BF16) for inv_b, a_b in zip(inv_bs, a_bs)]
        invs = [inv - _dot(left, _pair_diag(inv_b)) for inv, left, inv_b in zip(invs, lefts, inv_bs)]
        s *= 2
    return invs


def _pair_halves(x):
    left = lax.broadcasted_iota(jnp.int32, x.shape, 1) < x.shape[1] // 2
    zero = jnp.zeros_like(x)
    return jnp.where(left, x, zero), jnp.where(left, zero, x)


def _pair_diag(x):
    return jnp.concatenate(_pair_halves(x), axis=0)


def _gdn_prompt_kernel(qkv_ref, sm_ref, z_ref, cw_ref, gb_ref, galog_ref, mixw_ref,
                       y_ref, tail_ref, state_ref, ubuf_ref, conv_ref, s_ref):
    c = pl.program_id(1)
    t = qkv_ref.shape[0]
    subs = range(t // CHUNK)
    heads = range(GDN_HEADS)

    @pl.when(c == 0)
    def _():
        ubuf_ref[:, 0:SUBLANES, :] = jnp.zeros((GDN_CONV // LANES, SUBLANES, LANES), F32)
        s_ref[...] = jnp.zeros_like(s_ref)

    _causal_conv_tile(qkv_ref, ubuf_ref, cw_ref, None, conv_ref, tail_ref)

    sm = sm_ref[...]
    beta = _sigmoid(sm)
    g = -jnp.exp(galog_ref[...]) * _softplus(sm + gb_ref[...])
    rt = lax.broadcasted_iota(jnp.int32, (t, t), 0)
    ct = lax.broadcasted_iota(jnp.int32, (t, t), 1)
    chunk_causal = (rt >= ct) & (rt // CHUNK == ct // CHUNK)
    gc = _sel_left(chunk_causal.astype(BF16), g)
    gc_t = _transpose_sel(gc)
    eg = jnp.exp(gc)
    ri = lax.broadcasted_iota(jnp.int32, (CHUNK, 2 * CHUNK), 0)
    lane = lax.broadcasted_iota(jnp.int32, (CHUNK, 2 * CHUNK), 1)
    ci = lane % CHUNK
    causal = ri >= ci
    strict = ri > ci
    first_head = lane < CHUNK

    rows = [slice(j * CHUNK, (j + 1) * CHUNK) for j in subs]
    hs = [slice(h * GDN_D, (h + 1) * GDN_D) for h in heads]
    la = [SM_A + h for h in heads]
    packs = [(j, a) for j in subs for a in range(0, GDN_HEADS, 2)]
    q, k, kb, vb, kbg, qg = {}, {}, {}, {}, {}, {}
    for h in heads:
        qf, kf, vf = conv_ref[h], conv_ref[GDN_HEADS + h], conv_ref[2 * GDN_HEADS + h]
        qf = qf * lax.rsqrt(jnp.sum(qf * qf, axis=-1, keepdims=True) + EPS) * (GDN_D ** -0.5)
        kf = kf * lax.rsqrt(jnp.sum(kf * kf, axis=-1, keepdims=True) + EPS)
        b_col = beta[:, SM_B + h:SM_B + h + 1]
        eg_col = eg[:, la[h]:la[h] + 1]
        kbf = kf * b_col
        vbf, kbgf, qgf = (vf * b_col).astype(BF16), (kbf * eg_col).astype(BF16), (qf * eg_col).astype(BF16)
        for j in subs:
            q[j, h], k[j, h], kb[j, h] = qf[rows[j]].astype(BF16), kf[rows[j]], kbf[rows[j]].astype(BF16)
            vb[j, h], kbg[j, h], qg[j, h] = vbf[rows[j]], kbgf[rows[j]], qgf[rows[j]]
    decay, kq = {}, {}
    no_keys = jnp.zeros((CHUNK, GDN_D), BF16)
    for j, a in packs:
        b = a + 1
        col = jnp.where(first_head, gc[rows[j], la[a]:la[a] + 1], gc[rows[j], la[b]:la[b] + 1])
        row = jnp.concatenate([gc_t[la[a]:la[a] + 1, rows[j]], gc_t[la[b]:la[b] + 1, rows[j]]], axis=1)
        decay[j, a] = jnp.where(causal, jnp.exp(jnp.minimum(col - row, 0.0)), 0.0)
        kq[j, a] = (_dot_nt(jnp.concatenate([kb[j, a], q[j, a]], axis=0), jnp.concatenate([k[j, a].astype(BF16), no_keys], axis=0))
                    + _dot_nt(jnp.concatenate([kb[j, b], q[j, b]], axis=0), jnp.concatenate([no_keys, k[j, b].astype(BF16)], axis=0)))
    a_strict = [jnp.where(strict, kq[p][:CHUNK] * decay[p], 0.0) for p in packs]
    attn = {p: _pair_halves((kq[p][CHUNK:] * decay[p]).astype(BF16)) for p in packs}
    t_inv = dict(zip(packs, [_pair_halves(x.astype(BF16)) for x in _unit_lower_inverses(a_strict, ri, ci)]))
    u, wk = {}, {}
    for j, a in packs:
        rhs = jnp.concatenate([jnp.concatenate([vb[j, h], kbg[j, h]], axis=1) for h in (a, a + 1)], axis=0)
        for half, h in enumerate((a, a + 1)):
            uw = _dot(t_inv[j, a][half], rhs)
            u[j, h], wk[j, h] = uw[:, :GDN_D], uw[:, GDN_D:].astype(BF16)

    state = [s_ref[h] for h in heads]
    for j in subs:
        g_last = gc[(j + 1) * CHUNK - 1:(j + 1) * CHUNK, :]
        eend = jnp.exp(g_last - gc[rows[j]])
        elast = jnp.exp(g_last)
        s_b = [x.astype(BF16) for x in state]
        v_new = [(u[j, h] - _dot(wk[j, h], s_b[h])).astype(BF16) for h in heads]
        k_end = [(k[j, h] * eend[:, la[h]:la[h] + 1]).astype(BF16) for h in heads]
        s_inc = [_dot_tn(k_end[h], v_new[h]) for h in heads]
        state = [state[h] * elast[:, la[h]:la[h] + 1] + s_inc[h] for h in heads]
        v_pair = {a: jnp.concatenate([v_new[a], v_new[a + 1]], axis=0) for a in range(0, GDN_HEADS, 2)}
        o = [_dot(qg[j, h], s_b[h]) + _dot(attn[j, h - h % 2][h % 2], v_pair[h - h % 2]) for h in heads]
        msq = [jnp.sum(x * x, axis=-1, keepdims=True) for x in o]
        for h in heads:
            y_ref[rows[j], hs[h]] = _head_norm_gate(o[h], msq[h], GDN_D, z_ref[rows[j], hs[h]], mixw_ref[:, hs[h]]).astype(BF16)
    for h in heads:
        s_ref[h] = state[h]

    @pl.when(c == pl.num_programs(1) - 1)
    def _():
        state_ref[0] = s_ref[...]


def _gdn_prompt(proj, small, batch, cw, gb, galog, mixw, tile, riders):
    rows = proj.shape[0]
    nc = rows // batch // tile
    row = lambda b, c: (b * nc + c, 0)
    whole = lambda b, c: (0, 0)
    return _scan_with_riders(
        "gdn_prompt", (batch, nc), _gdn_prompt_kernel,
        arrays=(proj, small, proj, cw, gb, galog, mixw),
        in_specs=[
            pl.BlockSpec((tile, GDN_CONV), lambda b, c: (b * nc + c, COL_QKV // GDN_CONV)),
            pl.BlockSpec((tile, LANES), row),
            pl.BlockSpec((tile, GDN_W), lambda b, c: (b * nc + c, (COL_Z + SSD_W) // GDN_W)),
            pl.BlockSpec(cw.shape, whole), pl.BlockSpec(gb.shape, whole), pl.BlockSpec(galog.shape, whole),
            pl.BlockSpec((1, GDN_W), lambda b, c: (0, SSD_W // GDN_W)),
        ],
        out_shape=[
            jax.ShapeDtypeStruct((rows, GDN_W), BF16),
            jax.ShapeDtypeStruct((batch, SUBLANES, GDN_CONV), F32),
            jax.ShapeDtypeStruct((batch, GDN_HEADS, GDN_D, GDN_D), F32),
        ],
        out_specs=[
            pl.BlockSpec((tile, GDN_W), row),
            pl.BlockSpec((1, SUBLANES, GDN_CONV), lambda b, c: (b, 0, 0)),
            pl.BlockSpec((1, GDN_HEADS, GDN_D, GDN_D), lambda b, c: (b, 0, 0, 0)),
        ],
        scratch_shapes=[pltpu.VMEM((GDN_CONV // LANES, tile + SUBLANES, LANES), F32),
                        pltpu.VMEM((GDN_CONV // LANES, tile, LANES), F32), pltpu.VMEM((GDN_HEADS, GDN_D, GDN_D), F32)],
        riders=riders)


def _mem_prompt_kernel(q_ref, k_ref, v_ref, z_ref, mixw_ref, y_ref):
    heads = range(MEM_HEADS)
    hs = [slice(h * MEM_D, (h + 1) * MEM_D) for h in heads]
    s = [_dot_nt(q_ref[:, hs[h]].astype(BF16), k_ref[:, hs[h]].astype(BF16)) * (MEM_D ** -0.5) for h in heads]
    e = [jnp.exp(x - jnp.max(x, axis=-1, keepdims=True)) for x in s]
    p = [(x / jnp.sum(x, axis=-1, keepdims=True)).astype(BF16) for x in e]
    o = [_dot(p[h], v_ref[:, hs[h]].astype(BF16)) for h in heads]
    msq = [jnp.sum(x * x, axis=-1, keepdims=True) for x in o]
    for h in heads:
        y_ref[:, hs[h]] = _head_norm_gate(o[h], msq[h], MEM_D, z_ref[:, hs[h]], mixw_ref[:, hs[h]]).astype(BF16)


def _mem_prompt(proj, kv, batch, mixw, tq):
    rows = proj.shape[0]
    nq = rows // batch // tq
    return pl.pallas_call(
        _mem_prompt_kernel,
        grid=(batch, nq),
        in_specs=[
            pl.BlockSpec((tq, MEM_W), lambda b, i: (b * nq + i, COL_QMEM // MEM_W)),
            pl.BlockSpec((MEM_TOKENS, MEM_W), lambda b, i: (b, 0)),
            pl.BlockSpec((MEM_TOKENS, MEM_W), lambda b, i: (b, 1)),
            pl.BlockSpec((tq, MEM_W), lambda b, i: (b * nq + i, (COL_Z + SSD_W + GDN_W) // MEM_W)),
            pl.BlockSpec((1, MEM_W), lambda b, i: (0, (SSD_W + GDN_W) // MEM_W)),
        ],
        out_specs=pl.BlockSpec((tq, MEM_W), lambda b, i: (b * nq + i, 0)),
        out_shape=jax.ShapeDtypeStruct((rows, MEM_W), BF16),
        compiler_params=_params("parallel", "parallel"),
        name="mem_prompt",
    )(proj, kv, kv, proj, mixw)


def _conv_step(u, cst_ref, cst_out_ref, cw_ref, bias):
    acc = cw_ref[CONV_K - 1:CONV_K, :] * u
    if bias is not None:
        acc = acc + bias
    for j in range(CONV_K - 1):
        prev = cst_ref[j]
        acc = acc + cw_ref[j:j + 1, :] * prev
        if j > 0:
            cst_out_ref[j - 1] = prev
    cst_out_ref[CONV_K - 2] = u
    return _silu(acc)


def _rows_to_columns(x):
    pad = jnp.zeros((LANES - x.shape[0], x.shape[1]), F32)
    return jnp.concatenate([x, pad], axis=0).T


def _pick_rows(parts):
    rid = lax.broadcasted_iota(jnp.int32, parts[0].shape, 0)
    out = parts[0]
    for i in range(1, len(parts)):
        out = jnp.where(rid == i, parts[i], out)
    return out


def _ssd_decode_step(xbc_ref, sm_ref, z_ref, cst_ref, st_ref, cw_ref, cb_ref, dtb_ref, alog_ref, dexp_ref,
                     mixw_ref, e_ref, en_ref, y_ref, cst_out_ref, st_out_ref):
    xbc = _conv_step(xbc_ref[...], cst_ref, cst_out_ref, cw_ref, cb_ref[...])
    xs = xbc[:, :SSD_W]
    e = e_ref[...]
    dt = _softplus(sm_ref[...] + dtb_ref[...])
    dec = jnp.exp(dt * (-jnp.exp(alog_ref[...])))
    xd_t = _rows_to_columns(xs * _sel_right(dt, e))
    dec_n = _sel_right(dec, en_ref[...])

    groups, rows, per_group = range(SSD_GROUPS), range(xs.shape[0]), SSD_HEADS // SSD_GROUPS
    b_g = [xbc[:, SSD_W + g * SSD_N:SSD_W + (g + 1) * SSD_N] for g in groups]
    c_g = [xbc[:, SSD_W + (SSD_GROUPS + g) * SSD_N:SSD_W + (SSD_GROUPS + g + 1) * SSD_N].astype(BF16) for g in groups]
    hn = {}
    for g in groups:
        for i in rows:
            for h in range(g * per_group, (g + 1) * per_group):
                col = xd_t[h * SSD_P:(h + 1) * SSD_P, i:i + 1]
                hn[i, h] = st_ref[i, h] * dec_n[i:i + 1, h * SSD_N:(h + 1) * SSD_N] + col * b_g[g][i:i + 1, :]
                st_out_ref[i, h] = hn[i, h]
    hg = {(g, i): jnp.concatenate([hn[i, h] for h in range(g * per_group, (g + 1) * per_group)], axis=0).astype(BF16)
          for g in groups for i in rows}
    y_rows = {p: _dot_nt(c_g[p[0]], hg[p]) for p in hg}
    y = jnp.concatenate([_pick_rows([y_rows[g, i] for i in rows]) for g in groups], axis=1) + dexp_ref[...] * xs
    msq = _sel_right(_sel_right_nt(y * y, e, 2), e, 2)
    y_ref[...] = _head_norm_gate(y, msq, SSD_P, z_ref[...], mixw_ref[...])


def _gdn_decode_step(qkv_ref, sm_ref, z_ref, cst_ref, st_ref, cw_ref, gb_ref, galog_ref, mixw_ref, en_ref,
                     y_ref, cst_out_ref, st_out_ref):
    qkv = _conv_step(qkv_ref[...], cst_ref, cst_out_ref, cw_ref, None)
    sm = sm_ref[...]
    beta = _sigmoid(sm)
    eg = jnp.exp(-jnp.exp(galog_ref[...]) * _softplus(sm + gb_ref[...]))
    eg_n = _sel_right(eg, en_ref[...])

    qs, ks = [], []
    for h in range(GDN_HEADS):
        q = qkv[:, h * GDN_D:(h + 1) * GDN_D]
        k = qkv[:, GDN_W + h * GDN_D:GDN_W + (h + 1) * GDN_D]
        qs.append(q * lax.rsqrt(jnp.sum(q * q, axis=-1, keepdims=True) + EPS) * (GDN_D ** -0.5))
        ks.append(k * lax.rsqrt(jnp.sum(k * k, axis=-1, keepdims=True) + EPS))
    k_t = _rows_to_columns(jnp.concatenate(ks, axis=1))

    heads, rows = range(GDN_HEADS), range(sm.shape[0])
    n = len(rows)
    hs = [slice(h * GDN_D, (h + 1) * GDN_D) for h in heads]
    kq_b = [jnp.concatenate([ks[h], qs[h]], axis=0).astype(BF16) for h in heads]
    prod = {(h, i): _dot(kq_b[h], st_ref[i, h].astype(BF16)) for h in heads for i in rows}
    k_s = [_pick_rows([prod[h, i][:n] for i in rows]) for h in heads]
    q_s = [_pick_rows([prod[h, i][n:] for i in rows]) for h in heads]
    eg_h = [eg_n[:, hs[h]] for h in heads]
    v = [qkv[:, 2 * GDN_W + h * GDN_D:2 * GDN_W + (h + 1) * GDN_D] for h in heads]
    v_new = [beta[:, SM_B + h:SM_B + h + 1] * (v[h] - eg_h[h] * k_s[h]) for h in heads]
    o = [eg_h[h] * q_s[h] + jnp.sum(qs[h] * ks[h], axis=-1, keepdims=True) * v_new[h] for h in heads]
    for h in heads:
        for i in rows:
            st_out_ref[i, h] = st_ref[i, h] * eg_h[h][i:i + 1, :] + k_t[hs[h], i:i + 1] * v_new[h][i:i + 1, :]
    msq = [jnp.sum(x * x, axis=-1, keepdims=True) for x in o]
    for h in heads:
        y_ref[:, hs[h]] = _head_norm_gate(o[h], msq[h], GDN_D, z_ref[:, hs[h]], mixw_ref[:, hs[h]])


def _mem_decode_step(q_ref, k_ref, v_ref, z_ref, mixw_ref, y_ref):
    heads, rows = range(MEM_HEADS), range(q_ref.shape[0])
    hs = [slice(h * MEM_D, (h + 1) * MEM_D) for h in heads]
    win = [pl.ds(h, MEM_TOKENS, stride=MEM_HEADS) for h in heads]
    q = [q_ref[:, hs[h]].astype(BF16) for h in heads]
    s = [_pick_rows([_dot_nt(q[h], k_ref[i, win[h], :].astype(BF16)) for i in rows]) * (MEM_D ** -0.5) for h in heads]
    e = [jnp.exp(x - jnp.max(x, axis=-1, keepdims=True)) for x in s]
    p = [(x / jnp.sum(x, axis=-1, keepdims=True)).astype(BF16) for x in e]
    o = [_pick_rows([_dot(p[h], v_ref[i, win[h], :].astype(BF16)) for i in rows]) for h in heads]
    msq = [jnp.sum(x * x, axis=-1, keepdims=True) for x in o]
    for h in heads:
        y_ref[:, hs[h]] = _head_norm_gate(o[h], msq[h], MEM_D, z_ref[:, hs[h]], mixw_ref[:, hs[h]])


DEC_ROWS = SUBLANES


def _ssd_decode_kernel(proj_ref, sm_ref, cst_ref, st_ref, cw_ref, cb_ref, dtb_ref, alog_ref, dexp_ref, mixw_ref, e_ref, en_ref,
                       y_ref, cst_out_ref, st_out_ref):
    _ssd_decode_step(proj_ref.at[:, COL_XBC:COL_XBC + SSD_CONV], sm_ref, proj_ref.at[:, COL_Z:COL_Z + SSD_W], cst_ref, st_ref,
                     cw_ref, cb_ref, dtb_ref, alog_ref, dexp_ref, mixw_ref.at[:, :SSD_W], e_ref, en_ref,
                     y_ref, cst_out_ref, st_out_ref)


def _gdn_decode_kernel(proj_ref, sm_ref, cst_ref, st_ref, cw_ref, gb_ref, galog_ref, mixw_ref, en_ref,
                       y_ref, cst_out_ref, st_out_ref):
    _gdn_decode_step(proj_ref.at[:, COL_QKV:COL_QKV + GDN_CONV], sm_ref, proj_ref.at[:, COL_Z + SSD_W:COL_Z + SSD_W + GDN_W],
                     cst_ref, st_ref, cw_ref, gb_ref, galog_ref, mixw_ref.at[:, SSD_W:SSD_W + GDN_W], en_ref,
                     y_ref, cst_out_ref, st_out_ref)


def _ssd_mem_decode_kernel(proj_ref, sm_ref, cst_ref, st_ref, k_ref, v_ref, cw_ref, cb_ref, dtb_ref, alog_ref, dexp_ref,
                           mixw_ref, e_ref, en_ref, y_ssd_ref, y_mem_ref, cst_out_ref, st_out_ref):
    _mem_decode_step(proj_ref.at[:, COL_QMEM:COL_QMEM + MEM_W], k_ref, v_ref, proj_ref.at[:, N_MAIN - MEM_W:N_MAIN],
                     mixw_ref.at[:, SSD_W + GDN_W:], y_mem_ref)
    _ssd_decode_kernel(proj_ref, sm_ref, cst_ref, st_ref, cw_ref, cb_ref, dtb_ref, alog_ref, dexp_ref, mixw_ref, e_ref, en_ref,
                       y_ssd_ref, cst_out_ref, st_out_ref)


def _decode_rider(kernel, per_row_in, consts, out_shape, steps, step_index, early=()):
    groups = per_row_in[0].shape[0] // DEC_ROWS
    assert steps % groups == 0
    per_group = steps // groups

    def group_spec(a, lead=0):
        axis = 1 if len(a.shape) == 3 and a.shape[0] == CONV_K - 1 else 0
        block = a.shape[:axis] + (DEC_ROWS,) + a.shape[axis + 1:]
        group_of = lambda *g: jnp.minimum((step_index(*g) + lead) // per_group, groups - 1)
        return pl.BlockSpec(block, lambda *g: (0,) * axis + (group_of(*g),) + (0,) * (len(block) - axis - 1))

    whole = lambda a: pl.BlockSpec(a.shape, lambda *g, nd=a.ndim: (0,) * nd)
    lead = lambda n: min(1, per_group - 1) if n in early else 0
    return dict(kernel=kernel, arrays=tuple(per_row_in) + tuple(consts), steps_per_group=per_group,
                in_specs=[group_spec(a, lead(n)) for n, a in enumerate(per_row_in)] + [whole(a) for a in consts],
                out_shape=list(out_shape), out_specs=[group_spec(a) for a in out_shape])


def _scan_with_riders_kernel(*refs, scan_kernel, n_scan_in, n_scan_out, riders):
    scan_in, refs = refs[:n_scan_in], refs[n_scan_in:]
    rider_in = []
    for _, n_in, _, _ in riders:
        rider_in.append(refs[:n_in])
        refs = refs[n_in:]
    scan_out, refs = refs[:n_scan_out], refs[n_scan_out:]
    rider_out = []
    for _, _, n_out, _ in riders:
        rider_out.append(refs[:n_out])
        refs = refs[n_out:]
    scan_kernel(*scan_in, *scan_out, *refs)
    step = pl.program_id(0) * pl.num_programs(1) + pl.program_id(1)
    for (kernel, _, _, per_group), ins, outs in zip(riders, rider_in, rider_out):
        if per_group == 1:
            kernel(*ins, *outs)
        else:
            pl.when(step % per_group == 0)(functools.partial(kernel, *ins, *outs))


def _scan_with_riders(name, grid, scan_kernel, arrays, in_specs, out_shape, out_specs, scratch_shapes, riders):
    body = functools.partial(
        _scan_with_riders_kernel, scan_kernel=scan_kernel, n_scan_in=len(arrays), n_scan_out=len(out_shape),
        riders=tuple((r["kernel"], len(r["arrays"]), len(r["out_shape"]), r["steps_per_group"]) for r in riders))
    outs = pl.pallas_call(
        body,
        grid=grid,
        in_specs=list(in_specs) + [s for r in riders for s in r["in_specs"]],
        out_specs=list(out_specs) + [s for r in riders for s in r["out_specs"]],
        out_shape=list(out_shape) + [s for r in riders for s in r["out_shape"]],
        scratch_shapes=scratch_shapes,
        compiler_params=_params("arbitrary", "arbitrary"),
        name=name,
    )(*arrays, *[a for r in riders for a in r["arrays"]])
    scan_res, outs = outs[:len(out_shape)], outs[len(out_shape):]
    rider_res = []
    for r in riders:
        rider_res.append(outs[:len(r["out_shape"])])
        outs = outs[len(r["out_shape"]):]
    return scan_res, rider_res


IN_DT = SSD_CONV
IN_QKV = IN_DT + SSD_HEADS
IN_B = IN_QKV + GDN_CONV
IN_QMEM = IN_B + 2 * GDN_HEADS
IN_COLS = IN_QMEM + MEM_W + MIX_W
PREP_COLS = 256


def _prep_w_in_kernel(w_ref, x_ref, nw_ref, main_ref, small_ref, o_ref, os_ref, h_ref):
    i = pl.program_id(0)
    nblk = h_ref.shape[0]

    @pl.when(i == 0)
    def _():
        h = _rmsnorm_bf16(x_ref[...], nw_ref[...])
        for kb in range(nblk):
            h_ref[kb] = h[:, kb * PREP_COLS:(kb + 1) * PREP_COLS]
        o_ref[...] = jnp.zeros_like(o_ref)
        os_ref[...] = jnp.zeros_like(os_ref)

    main_ref[COL_QKV:COL_QKV + GDN_CONV, :] = w_ref[IN_QKV:IN_B, :].astype(BF16)
    main_ref[COL_XBC:COL_XBC + SSD_CONV, :] = w_ref[:SSD_CONV, :].astype(BF16)
    main_ref[COL_QMEM:N_MAIN, :] = w_ref[IN_QMEM:IN_COLS, :].astype(BF16)
    small_ref[SM_DT:SM_B, :] = w_ref[IN_DT:IN_QKV, :].astype(BF16)
    small_ref[SM_B:SM_A + GDN_HEADS, :] = w_ref[IN_B:IN_QMEM, :].astype(BF16)
    small_ref[SM_A + GDN_HEADS:, :] = jnp.zeros((LANES - SM_A - GDN_HEADS, w_ref.shape[1]), BF16)

    hb = h_ref[i]
    o_ref[...] += _dot_nt(hb, main_ref[...])
    os_ref[...] += _dot_nt(hb, small_ref[...])


def _prep_w_in(w_t, x, nw):
    n, k = w_t.shape
    rows = x.shape[0]
    assert n == IN_COLS
    whole = lambda i: (0, 0)
    return pl.pallas_call(
        _prep_w_in_kernel,
        grid=(k // PREP_COLS,),
        in_specs=[pl.BlockSpec((IN_COLS, PREP_COLS), lambda i: (0, i)), pl.BlockSpec((rows, k), whole),
                  pl.BlockSpec((1, k), whole)],
        out_specs=[pl.BlockSpec((N_MAIN, PREP_COLS), lambda i: (0, i)), pl.BlockSpec((LANES, PREP_COLS), lambda i: (0, i)),
                   pl.BlockSpec((rows, N_MAIN), whole), pl.BlockSpec((rows, LANES), whole)],
        out_shape=[jax.ShapeDtypeStruct((N_MAIN, k), BF16), jax.ShapeDtypeStruct((LANES, k), BF16),
                   jax.ShapeDtypeStruct((rows, N_MAIN), F32), jax.ShapeDtypeStruct((rows, LANES), F32)],
        scratch_shapes=[pltpu.VMEM((k // PREP_COLS, rows, PREP_COLS), BF16)],
        compiler_params=_params("arbitrary"),
        name="prep_w_in",
    )(w_t, x, nw)


def _head_expander(heads, first_lane, width):
    m = np.zeros((LANES, heads * width), np.float32)
    for h in range(heads):
        m[first_lane + h, h * width:(h + 1) * width] = 1.0
    return jnp.asarray(m, BF16)


def _lane_row(vec, first_lane):
    return jnp.zeros((1, LANES), F32).at[0, first_lane:first_lane + vec.shape[0]].set(vec.astype(F32))


def kernel(x_prompt, x_sample, mem_prompt, state_ssd_conv, state_ssd, state_gdn_conv, state_gdn, cache_mem_k, cache_mem_v, norm_w, w_in, ssd_conv_w, ssd_conv_b, ssd_dt_bias, ssd_A_log, ssd_D, gdn_conv_w, gdn_dt_bias, gdn_A_log, mem_norm_w, w_mem_kv, mix_norm_w, w_out, final_norm_w):
    bp, seq, d = x_prompt.shape
    bs = x_sample.shape[0]
    assert (d, seq % CHUNK, norm_w.shape[0]) == (D_MODEL, 0, 1)

    nw = norm_w[0][None, :]
    xs = x_sample.reshape(bs, d)
    w_main, w_small, proj_s, small_s = _prep_w_in(w_in[0].T, xs, nw)
    mixw = mix_norm_w[0][None, :]
    fw = final_norm_w[None, :]
    ssd_dtb = _lane_row(ssd_dt_bias[0], SM_DT)
    ssd_alog = _lane_row(ssd_A_log[0], SM_DT)
    ssd_dexp = jnp.repeat(ssd_D[0].astype(F32), SSD_P)[None, :]
    gdn_b = _lane_row(gdn_dt_bias[0], SM_A)
    gdn_alog = _lane_row(gdn_A_log[0], SM_A)
    e_ssd = _head_expander(SSD_HEADS, SM_DT, SSD_P)
    e_ssd_n = _head_expander(SSD_HEADS, SM_DT, SSD_N)
    e_gdn_n = _head_expander(GDN_HEADS, SM_A, GDN_D)
    ssd_cw, ssd_cb, gdn_cw = ssd_conv_w[0], ssd_conv_b[0][None, :], gdn_conv_w[0]

    xp = x_prompt.reshape(bp * seq, d)
    proj_p, small_p, wo, w_kv = _norm_matmul(
        xp, nw, w_main, w_small, _row_tile(bp * seq, PROJ_ROWS), SSD_CONV, casts=(w_out[0], w_mem_kv[0]))
    kv, k_prompt, v_prompt = _mem_kv(mem_prompt.reshape(bp * MEM_TOKENS, d), mem_norm_w[0][None, :], w_kv, MEM_TOKENS)
    scan_rows = SCAN_ROWS if seq % SCAN_ROWS == 0 else CHUNK
    scan_steps = bp * seq // scan_rows
    step_index = lambda b, c: b * (seq // scan_rows) + c

    ssd_cst, gdn_cst = jnp.swapaxes(state_ssd_conv[0], 0, 1), jnp.swapaxes(state_gdn_conv[0], 0, 1)
    mem_k = cache_mem_k.reshape(bs, MEM_TOKENS * MEM_HEADS, MEM_D)
    mem_v = cache_mem_v.reshape(bs, MEM_TOKENS * MEM_HEADS, MEM_D)
    row_out = lambda width: jax.ShapeDtypeStruct((bs, width), F32)
    like = lambda a: jax.ShapeDtypeStruct(a.shape, F32)
    ssd_rider = _decode_rider(
        _ssd_mem_decode_kernel, (proj_s, small_s, ssd_cst, state_ssd[0], mem_k, mem_v),
        (ssd_cw, ssd_cb, ssd_dtb, ssd_alog, ssd_dexp, mixw, e_ssd, e_ssd_n),
        (row_out(SSD_W), row_out(MEM_W), like(ssd_cst), like(state_ssd[0])), scan_steps, step_index, early=(4, 5))
    out_rows = _row_tile(bp * seq, OUT_ROWS)
    gdn_rider = _decode_rider(
        _gdn_decode_kernel, (proj_s, small_s, gdn_cst, state_gdn[0]), (gdn_cw, gdn_b, gdn_alog, mixw, e_gdn_n),
        (row_out(GDN_W), like(gdn_cst), like(state_gdn[0])), bp * seq // out_rows, lambda i, _: i)

    (y_ssd, tail_ssd, p_ssd), ((ys_ssd, ys_mem, s_ssd_conv, s_ssd),) = _ssd_prompt(
        proj_p, small_p, bp, ssd_cw, ssd_cb, ssd_dtb, ssd_alog, ssd_dexp, mixw, e_ssd, scan_rows, (ssd_rider,))
    (y_gdn, tail_gdn, p_gdn), _ = _gdn_prompt(proj_p, small_p, bp, gdn_cw, gdn_b, gdn_alog, mixw, scan_rows, ())
    y_mem = _mem_prompt(proj_p, kv, bp, mixw, _row_tile(seq, MEM_Q_ROWS))
    y_prompt, ((ys_gdn, s_gdn_conv, s_gdn),) = _out_proj(y_ssd, y_gdn, y_mem, wo, xp, fw, out_rows, (gdn_rider,))
    y_prompt = y_prompt.reshape(bp, seq, d)
    y_sample = _out_proj(ys_ssd, ys_gdn, ys_mem, wo, xs, fw, bs)[0].reshape(bs, 1, d)

    keep = CONV_K - 1
    mem_shape = (1, bp, MEM_TOKENS, MEM_HEADS, MEM_D)
    return (
        y_prompt, y_sample,
        tail_ssd[None, :, SUBLANES - keep:, :], p_ssd[None],
        tail_gdn[None, :, SUBLANES - keep:, :], p_gdn[None],
        k_prompt.reshape(mem_shape), v_prompt.reshape(mem_shape),
        jnp.swapaxes(s_ssd_conv, 0, 1)[None], s_ssd[None],
        jnp.swapaxes(s_gdn_conv, 0, 1)[None], s_gdn[None],
    )
```

```python
import functools

import numpy as np
import jax
import jax.numpy as jnp
from jax import lax
from jax.experimental import pallas as pl
from jax.experimental.pallas import tpu as pltpu

F32, BF16 = jnp.float32, jnp.bfloat16

D_MODEL = 2048
SSD_HEADS, SSD_P, SSD_GROUPS, SSD_N = 16, 64, 2, 128
SSD_W = SSD_HEADS * SSD_P
SSD_GW = SSD_W // SSD_GROUPS
SSD_CONV = SSD_W + 2 * SSD_GROUPS * SSD_N
GDN_HEADS, GDN_D = 8, 128
GDN_W = GDN_HEADS * GDN_D
GDN_CONV = 3 * GDN_W
MEM_TOKENS, MEM_HEADS, MEM_D = 256, 4, 128
MEM_W = MEM_HEADS * MEM_D
MIX_W = SSD_W + GDN_W + MEM_W
CONV_K = 4
CHUNK = 64
EPS = 1e-6

LANES = 128
SUBLANES = 8
VMEM_LIMIT = 60 * 1024 * 1024
PROJ_ROWS = 1024
OUT_ROWS = 512
MEM_Q_ROWS = 512
SCAN_ROWS = 256
CONV_PHASES = 4

COL_QKV = 0
COL_XBC = COL_QKV + GDN_CONV
COL_QMEM = COL_XBC + SSD_CONV
COL_Z = COL_QMEM + MEM_W
N_MAIN = COL_Z + MIX_W
SM_DT, SM_B, SM_A = 0, SSD_HEADS, SSD_HEADS + GDN_HEADS


def _dot(a, b):
    return jnp.dot(a, b, preferred_element_type=F32)


def _dot_nt(a, b):
    return lax.dot_general(a, b, (((1,), (1,)), ((), ())), preferred_element_type=F32)


def _dot_tn(a, b):
    return lax.dot_general(a, b, (((0,), (0,)), ((), ())), preferred_element_type=F32)


def _split(x, n):
    parts, r = [], x
    for i in range(n):
        p = r.astype(BF16)
        parts.append(p)
        if i + 1 < n:
            r = r - p.astype(F32)
    return parts


def _sel_left(sel, x, n=3):
    return functools.reduce(lambda a, b: a + b, [_dot(sel, p) for p in _split(x, n)])


def _sel_right(x, sel, n=3):
    return functools.reduce(lambda a, b: a + b, [_dot(p, sel) for p in _split(x, n)])


def _sel_right_nt(x, sel, n=3):
    return functools.reduce(lambda a, b: a + b, [_dot_nt(p, sel) for p in _split(x, n)])


def _transpose_sel(x, n=3):
    eye = _eye(LANES).astype(BF16)
    return functools.reduce(lambda a, b: a + b, [_dot_nt(eye, p) for p in _split(x, n)])


def _eye(n):
    return (lax.broadcasted_iota(jnp.int32, (n, n), 0) == lax.broadcasted_iota(jnp.int32, (n, n), 1)).astype(F32)


def _sigmoid(x):
    return 0.5 * jnp.tanh(0.5 * x) + 0.5


def _silu(x):
    h = 0.5 * x
    return h + h * jnp.tanh(h)


def _softplus(x):
    return jnp.maximum(x, 0.0) + jnp.log1p(jnp.exp(-jnp.abs(x)))


def _params(*sem):
    return pltpu.CompilerParams(dimension_semantics=sem, vmem_limit_bytes=VMEM_LIMIT)


def _row_tile(rows, preferred):
    return preferred if rows % preferred == 0 else rows


def _rmsnorm_bf16(x, nw):
    ms = jnp.mean(x * x, axis=-1, keepdims=True)
    return (x * lax.rsqrt(ms + EPS) * nw).astype(BF16)


def _norm_matmul_kernel(x_ref, nw_ref, w_ref, ws_ref, *refs, n_casts):
    cast_in, (o_ref, os_ref), cast_out, (h_ref,) = (
        refs[:n_casts], refs[n_casts:n_casts + 2], refs[n_casts + 2:2 * n_casts + 2], refs[2 * n_casts + 2:])

    @pl.when(pl.program_id(1) == 0)
    def _():
        h = _rmsnorm_bf16(x_ref[...], nw_ref[...])
        h_ref[...] = h
        os_ref[...] = _dot_nt(h, ws_ref[...])

    o_ref[...] = _dot_nt(h_ref[...], w_ref[...])
    for src, dst in zip(cast_in, cast_out):
        dst[...] = src[...].astype(BF16)


def _cast_block_rows(rows, steps):
    aligned = [r for r in range(2 * SUBLANES, rows + 1, 2 * SUBLANES) if rows % r == 0 and rows // r <= steps]
    return aligned[0] if aligned else rows


def _norm_matmul(x, nw, w_t, ws_t, tm, tn, casts=()):
    m, k = x.shape
    n = w_t.shape[0]
    ns = ws_t.shape[0]
    nj = n // tn
    steps = (m // tm) * nj

    def cast_spec(a):
        r = _cast_block_rows(a.shape[0], steps)
        return pl.BlockSpec((r, a.shape[1]), lambda i, j: (jnp.minimum(i * nj + j, a.shape[0] // r - 1), 0))

    return pl.pallas_call(
        functools.partial(_norm_matmul_kernel, n_casts=len(casts)),
        grid=(m // tm, nj),
        in_specs=[
            pl.BlockSpec((tm, k), lambda i, j: (i, 0)),
            pl.BlockSpec((1, k), lambda i, j: (0, 0)),
            pl.BlockSpec((tn, k), lambda i, j: (j, 0)),
            pl.BlockSpec((ns, k), lambda i, j: (0, 0)),
        ] + [cast_spec(a) for a in casts],
        out_specs=[
            pl.BlockSpec((tm, tn), lambda i, j: (i, j)),
            pl.BlockSpec((tm, ns), lambda i, j: (i, 0)),
        ] + [cast_spec(a) for a in casts],
        out_shape=[jax.ShapeDtypeStruct((m, n), F32), jax.ShapeDtypeStruct((m, ns), F32)]
        + [jax.ShapeDtypeStruct(a.shape, BF16) for a in casts],
        scratch_shapes=[pltpu.VMEM((tm, k), BF16)],
        compiler_params=_params("arbitrary", "arbitrary"),
        name="norm_matmul",
    )(x, nw, w_t, ws_t, *casts)


def _mem_kv_kernel(x_ref, nw_ref, w_ref, o_ref, k_ref, v_ref):
    kv = _dot(_rmsnorm_bf16(x_ref[...], nw_ref[...]), w_ref[...])
    o_ref[...] = kv
    for half, ref in enumerate((k_ref, v_ref)):
        for h in range(MEM_HEADS):
            ref[pl.ds(h, x_ref.shape[0], stride=MEM_HEADS), :] = kv[:, half * MEM_W + h * MEM_D:half * MEM_W + (h + 1) * MEM_D]


def _mem_kv(x, nw, w, tm):
    m, k = x.shape
    n = w.shape[1]
    assert n == 2 * MEM_W
    per_head = pl.BlockSpec((tm * MEM_HEADS, MEM_D), lambda i: (i, 0))
    return pl.pallas_call(
        _mem_kv_kernel,
        grid=(m // tm,),
        in_specs=[pl.BlockSpec((tm, k), lambda i: (i, 0)), pl.BlockSpec((1, k), lambda i: (0, 0)),
                  pl.BlockSpec((k, n), lambda i: (0, 0))],
        out_specs=[pl.BlockSpec((tm, n), lambda i: (i, 0)), per_head, per_head],
        out_shape=[jax.ShapeDtypeStruct((m, n), F32)] + [jax.ShapeDtypeStruct((m * MEM_HEADS, MEM_D), F32)] * 2,
        compiler_params=_params("parallel"),
        name="mem_kv",
    )(x, nw, w)


def _out_proj_kernel(y1_ref, y2_ref, y3_ref, w_ref, x_ref, fw_ref, o_ref):
    n1, n2 = y1_ref.shape[1], y1_ref.shape[1] + y2_ref.shape[1]
    acc = (_dot(y1_ref[...].astype(BF16), w_ref[:n1, :]) + _dot(y2_ref[...].astype(BF16), w_ref[n1:n2, :])
           + _dot(y3_ref[...].astype(BF16), w_ref[n2:, :]))
    r = x_ref[...] + acc
    ms = jnp.mean(r * r, axis=-1, keepdims=True)
    o_ref[...] = r * lax.rsqrt(ms + EPS) * fw_ref[...]


def _out_proj(y1, y2, y3, w, x, fw, tm, riders=()):
    m, d = x.shape
    row = lambda i, _: (i, 0)
    whole = lambda i, _: (0, 0)
    (out,), rider_res = _scan_with_riders(
        "out_proj", (m // tm, 1), _out_proj_kernel,
        arrays=(y1, y2, y3, w, x, fw),
        in_specs=[
            pl.BlockSpec((tm, y1.shape[1]), row), pl.BlockSpec((tm, y2.shape[1]), row), pl.BlockSpec((tm, y3.shape[1]), row),
            pl.BlockSpec(w.shape, whole, pipeline_mode=pl.Buffered(1)), pl.BlockSpec((tm, d), row), pl.BlockSpec((1, d), whole),
        ],
        out_shape=[jax.ShapeDtypeStruct((m, d), F32)],
        out_specs=[pl.BlockSpec((tm, d), row)],
        scratch_shapes=[],
        riders=riders)
    return out, rider_res


def _causal_conv_tile(u_ref, ubuf_ref, cw_ref, cb_ref, out_ref, tail_ref):
    t, width = u_ref.shape
    n = t // CONV_PHASES
    for s in range(width // LANES):
        cs = slice(s * LANES, (s + 1) * LANES)
        ubuf_ref[s, SUBLANES:SUBLANES + t, :] = u_ref[:, cs]
        taps = {d: ubuf_ref[s, pl.ds(SUBLANES + d, n, stride=CONV_PHASES), :] for d in range(1 - CONV_K, CONV_PHASES)}
        w = [cw_ref[j:j + 1, cs] for j in range(CONV_K)]
        for r in range(CONV_PHASES):
            acc = w[CONV_K - 1] * taps[r]
            if cb_ref is not None:
                acc = acc + cb_ref[:, cs]
            for j in range(CONV_K - 1):
                acc = acc + w[j] * taps[r - (CONV_K - 1) + j]
            out_ref[s, pl.ds(r, n, stride=CONV_PHASES), :] = _silu(acc)
        tail = ubuf_ref[s, t:t + SUBLANES, :]
        ubuf_ref[s, 0:SUBLANES, :] = tail
        tail_ref[0, :, cs] = tail


def _head_norm_gate(y, msq, width, z, mixw):
    return y * lax.rsqrt(msq * (1.0 / width) + EPS) * mixw * _silu(z)


def _ssd_prompt_kernel(xbc_ref, sm_ref, z_ref, cw_ref, cb_ref, dtb_ref, alog_ref, dexp_ref, mixw_ref, e_ref,
                       y_ref, tail_ref, state_ref, ubuf_ref, conv_ref, h_ref):
    c = pl.program_id(1)
    t = xbc_ref.shape[0]
    subs = range(t // CHUNK)
    groups = range(SSD_GROUPS)
    blocks = range(SSD_GW // LANES)

    @pl.when(c == 0)
    def _():
        ubuf_ref[:, 0:SUBLANES, :] = jnp.zeros((SSD_CONV // LANES, SUBLANES, LANES), F32)
        h_ref[...] = jnp.zeros_like(h_ref)

    _causal_conv_tile(xbc_ref, ubuf_ref, cw_ref, cb_ref, conv_ref, tail_ref)
    xs = jnp.concatenate([conv_ref[s] for s in range(SSD_W // LANES)], axis=1)
    e = e_ref[...]
    rows = [slice(j * CHUNK, (j + 1) * CHUNK) for j in subs]
    gs = [slice(g * SSD_GW, (g + 1) * SSD_GW) for g in groups]

    dt = _softplus(sm_ref[...] + dtb_ref[...])
    a = dt * (-jnp.exp(alog_ref[...]))
    rt = lax.broadcasted_iota(jnp.int32, (t, t), 0)
    ct = lax.broadcasted_iota(jnp.int32, (t, t), 1)
    chunk_causal = (rt >= ct) & (rt // CHUNK == ct // CHUNK)
    cum = _sel_left(chunk_causal.astype(BF16), a)
    cum_t = _transpose_sel(cum)
    ecum = jnp.exp(cum)
    wend = jnp.concatenate([jnp.exp(cum[(j + 1) * CHUNK - 1:(j + 1) * CHUNK, :] - cum[rows[j]]) for j in subs], axis=0)
    dt_x = _sel_right(dt, e, 2)
    ecum_x = _sel_right(ecum, e, 2)
    wend_x = _sel_right(wend, e, 2)

    xdt = xs * dt_x
    xdt_b = xdt.astype(BF16)
    xw_b = (xdt * wend_x).astype(BF16)
    lane = lax.broadcasted_iota(jnp.int32, (CHUNK, LANES), 1)
    causal = lax.broadcasted_iota(jnp.int32, (CHUNK, LANES), 0) >= lane % CHUNK
    first_head = lane < SSD_P

    jg = [(j, g) for j in subs for g in groups]
    b_slab, c_slab = SSD_W // LANES, SSD_W // LANES + SSD_GROUPS
    bmat = {(j, g): conv_ref[b_slab + g, rows[j], :].astype(BF16) for j, g in jg}
    cmat = {(j, g): conv_ref[c_slab + g, rows[j], :].astype(BF16) for j, g in jg}
    cb = {p: _dot_nt(cmat[p], jnp.concatenate([bmat[p], bmat[p]], axis=0)) for p in jg}
    inc = {(j, g): _dot_tn(bmat[j, g], xw_b[rows[j], gs[g]]) for j, g in jg}
    intra = {}
    for j, g in jg:
        for blk in blocks:
            a = (g * len(blocks) + blk) * 2
            col = jnp.where(first_head, cum[rows[j], a:a + 1], cum[rows[j], a + 1:a + 2])
            row = jnp.concatenate([cum_t[a:a + 1, rows[j]], cum_t[a + 1:a + 2, rows[j]]], axis=1)
            lmat = jnp.where(causal, jnp.exp(jnp.minimum(col - row, 0.0)), 0.0)
            lanes = slice(g * SSD_GW + blk * LANES, g * SSD_GW + (blk + 1) * LANES)
            intra[j, g, blk] = _dot((cb[j, g] * lmat).astype(BF16), _pair_diag(xdt_b[rows[j], lanes]))

    state = {(0, g): h_ref[:, gs[g]] for g in groups}
    for j in subs:
        last = (j + 1) * CHUNK - 1
        for g in groups:
            state[j + 1, g] = state[j, g] * ecum_x[last:last + 1, gs[g]] + inc[j, g]
    for g in groups:
        h_ref[:, gs[g]] = state[len(subs), g]
    inter = {p: _dot(cmat[p], state[p].astype(BF16)) for p in jg}
    inter_x = jnp.concatenate([jnp.concatenate([inter[j, g] for g in groups], axis=1) for j in subs], axis=0) * ecum_x
    intra_x = jnp.concatenate([jnp.concatenate([intra[j, g, blk] for g in groups for blk in blocks], axis=1)
                               for j in subs], axis=0)
    y = intra_x + inter_x + dexp_ref[...] * xs
    msq = _sel_right(_sel_right_nt(y * y, e, 2), e, 2)
    y_ref[...] = _head_norm_gate(y, msq, SSD_P, z_ref[...], mixw_ref[...]).astype(BF16)

    @pl.when(c == pl.num_programs(1) - 1)
    def _():
        state_ref[0] = h_ref[...].T.reshape(SSD_HEADS, SSD_P, SSD_N)


def _ssd_prompt(proj, small, batch, cw, cb, dtb, alog, dexp, mixw, e, tile, riders):
    rows = proj.shape[0]
    nc = rows // batch // tile
    row = lambda b, c: (b * nc + c, 0)
    whole = lambda b, c: (0, 0)
    return _scan_with_riders(
        "ssd_prompt", (batch, nc), _ssd_prompt_kernel,
        arrays=(proj, small, proj, cw, cb, dtb, alog, dexp, mixw, e),
        in_specs=[
            pl.BlockSpec((tile, SSD_CONV), lambda b, c: (b * nc + c, COL_XBC // SSD_CONV)),
            pl.BlockSpec((tile, LANES), row),
            pl.BlockSpec((tile, SSD_W), lambda b, c: (b * nc + c, COL_Z // SSD_W)),
            pl.BlockSpec(cw.shape, whole), pl.BlockSpec(cb.shape, whole), pl.BlockSpec(dtb.shape, whole),
            pl.BlockSpec(alog.shape, whole), pl.BlockSpec(dexp.shape, whole),
            pl.BlockSpec((1, SSD_W), whole),
            pl.BlockSpec(e.shape, whole),
        ],
        out_shape=[
            jax.ShapeDtypeStruct((rows, SSD_W), BF16),
            jax.ShapeDtypeStruct((batch, SUBLANES, SSD_CONV), F32),
            jax.ShapeDtypeStruct((batch, SSD_HEADS, SSD_P, SSD_N), F32),
        ],
        out_specs=[
            pl.BlockSpec((tile, SSD_W), row),
            pl.BlockSpec((1, SUBLANES, SSD_CONV), lambda b, c: (b, 0, 0)),
            pl.BlockSpec((1, SSD_HEADS, SSD_P, SSD_N), lambda b, c: (b, 0, 0, 0)),
        ],
        scratch_shapes=[pltpu.VMEM((SSD_CONV // LANES, tile + SUBLANES, LANES), F32),
                        pltpu.VMEM((SSD_CONV // LANES, tile, LANES), F32), pltpu.VMEM((SSD_N, SSD_W), F32)],
        riders=riders)


def _unit_lower_inverses(a_stricts, ri, ci):
    t = a_stricts[0].shape[0]
    eye = (ri == ci).astype(F32)
    first = (ri == ci + 1) & (ci % 2 == 0)
    invs = [eye - jnp.where(first, a, 0.0) for a in a_stricts]
    a_bs = [a.astype(BF16) for a in a_stricts]
    zero = jnp.zeros(a_bs[0].shape, BF16)
    s = 2
    while s < t:
        sel = (ri // (2 * s) == ci // (2 * s)) & ((ri // s) % 2 == 1) & ((ci // s) % 2 == 0)
        inv_bs = [inv.astype(BF16) for inv in invs]
        lefts = [_dot(inv_b, _pair_diag(jnp.where(sel, a_b, zero))).astype(BF16) for inv_b, a_b in zip(inv_bs, a_bs)]
        invs = [inv - _dot(left, _pair_diag(inv_b)) for inv, left, inv_b in zip(invs, lefts, inv_bs)]
        s *= 2
    return invs


def _pair_halves(x):
    left = lax.broadcasted_iota(jnp.int32, x.shape, 1) < x.shape[1] // 2
    zero = jnp.zeros_like(x)
    return jnp.where(left, x, zero), jnp.where(left, zero, x)


def _pair_diag(x):
    return jnp.concatenate(_pair_halves(x), axis=0)


def _gdn_prompt_kernel(qkv_ref, sm_ref, z_ref, cw_ref, gb_ref, galog_ref, mixw_ref,
                       y_ref, tail_ref, state_ref, ubuf_ref, conv_ref, s_ref):
    c = pl.program_id(1)
    t = qkv_ref.shape[0]
    subs = range(t // CHUNK)
    heads = range(GDN_HEADS)

    @pl.when(c == 0)
    def _():
        ubuf_ref[:, 0:SUBLANES, :] = jnp.zeros((GDN_CONV // LANES, SUBLANES, LANES), F32)
        s_ref[...] = jnp.zeros_like(s_ref)

    _causal_conv_tile(qkv_ref, ubuf_ref, cw_ref, None, conv_ref, tail_ref)

    sm = sm_ref[...]
    beta = _sigmoid(sm)
    g = -jnp.exp(galog_ref[...]) * _softplus(sm + gb_ref[...])
    rt = lax.broadcasted_iota(jnp.int32, (t, t), 0)
    ct = lax.broadcasted_iota(jnp.int32, (t, t), 1)
    chunk_causal = (rt >= ct) & (rt // CHUNK == ct // CHUNK)
    gc = _sel_left(chunk_causal.astype(BF16), g)
    gc_t = _transpose_sel(gc)
    eg = jnp.exp(gc)
    ri = lax.broadcasted_iota(jnp.int32, (CHUNK, 2 * CHUNK), 0)
    lane = lax.broadcasted_iota(jnp.int32, (CHUNK, 2 * CHUNK), 1)
    ci = lane % CHUNK
    causal = ri >= ci
    strict = ri > ci
    first_head = lane < CHUNK

    rows = [slice(j * CHUNK, (j + 1) * CHUNK) for j in subs]
    hs = [slice(h * GDN_D, (h + 1) * GDN_D) for h in heads]
    la = [SM_A + h for h in heads]
    packs = [(j, a) for j in subs for a in range(0, GDN_HEADS, 2)]
    q, k, kb, vb, kbg, qg = {}, {}, {}, {}, {}, {}
    for h in heads:
        qf, kf, vf = conv_ref[h], conv_ref[GDN_HEADS + h], conv_ref[2 * GDN_HEADS + h]
        qf = qf * lax.rsqrt(jnp.sum(qf * qf, axis=-1, keepdims=True) + EPS) * (GDN_D ** -0.5)
        kf = kf * lax.rsqrt(jnp.sum(kf * kf, axis=-1, keepdims=True) + EPS)
        b_col = beta[:, SM_B + h:SM_B + h + 1]
        eg_col = eg[:, la[h]:la[h] + 1]
        kbf = kf * b_col
        vbf, kbgf, qgf = (vf * b_col).astype(BF16), (kbf * eg_col).astype(BF16), (qf * eg_col).astype(BF16)
        for j in subs:
            q[j, h], k[j, h], kb[j, h] = qf[rows[j]].astype(BF16), kf[rows[j]], kbf[rows[j]].astype(BF16)
            vb[j, h], kbg[j, h], qg[j, h] = vbf[rows[j]], kbgf[rows[j]], qgf[rows[j]]
    decay, kq = {}, {}
    no_keys = jnp.zeros((CHUNK, GDN_D), BF16)
    for j, a in packs:
        b = a + 1
        col = jnp.where(first_head, gc[rows[j], la[a]:la[a] + 1], gc[rows[j], la[b]:la[b] + 1])
        row = jnp.concatenate([gc_t[la[a]:la[a] + 1, rows[j]], gc_t[la[b]:la[b] + 1, rows[j]]], axis=1)
        decay[j, a] = jnp.where(causal, jnp.exp(jnp.minimum(col - row, 0.0)), 0.0)
        kq[j, a] = (_dot_nt(jnp.concatenate([kb[j, a], q[j, a]], axis=0), jnp.concatenate([k[j, a].astype(BF16), no_keys], axis=0))
                    + _dot_nt(jnp.concatenate([kb[j, b], q[j, b]], axis=0), jnp.concatenate([no_keys, k[j, b].astype(BF16)], axis=0)))
    a_strict = [jnp.where(strict, kq[p][:CHUNK] * decay[p], 0.0) for p in packs]
    attn = {p: _pair_halves((kq[p][CHUNK:] * decay[p]).astype(BF16)) for p in packs}
    t_inv = dict(zip(packs, [_pair_halves(x.astype(BF16)) for x in _unit_lower_inverses(a_strict, ri, ci)]))
    u, wk = {}, {}
    for j, a in packs:
        rhs = jnp.concatenate([jnp.concatenate([vb[j, h], kbg[j, h]], axis=1) for h in (a, a + 1)], axis=0)
        for half, h in enumerate((a, a + 1)):
            uw = _dot(t_inv[j, a][half], rhs)
            u[j, h], wk[j, h] = uw[:, :GDN_D], uw[:, GDN_D:].astype(BF16)

    state = [s_ref[h] for h in heads]
    for j in subs:
        g_last = gc[(j + 1) * CHUNK - 1:(j + 1) * CHUNK, :]
        eend = jnp.exp(g_last - gc[rows[j]])
        elast = jnp.exp(g_last)
        s_b = [x.astype(BF16) for x in state]
        v_new = [(u[j, h] - _dot(wk[j, h], s_b[h])).astype(BF16) for h in heads]
        k_end = [(k[j, h] * eend[:, la[h]:la[h] + 1]).astype(BF16) for h in heads]
        s_inc = [_dot_tn(k_end[h], v_new[h]) for h in heads]
        state = [state[h] * elast[:, la[h]:la[h] + 1] + s_inc[h] for h in heads]
        v_pair = {a: jnp.concatenate([v_new[a], v_new[a + 1]], axis=0) for a in range(0, GDN_HEADS, 2)}
        o = [_dot(qg[j, h], s_b[h]) + _dot(attn[j, h - h % 2][h % 2], v_pair[h - h % 2]) for h in heads]
        msq = [jnp.sum(x * x, axis=-1, keepdims=True) for x in o]
        for h in heads:
            y_ref[rows[j], hs[h]] = _head_norm_gate(o[h], msq[h], GDN_D, z_ref[rows[j], hs[h]], mixw_ref[:, hs[h]]).astype(BF16)
    for h in heads:
        s_ref[h] = state[h]

    @pl.when(c == pl.num_programs(1) - 1)
    def _():
        state_ref[0] = s_ref[...]


def _gdn_prompt(proj, small, batch, cw, gb, galog, mixw, tile, riders):
    rows = proj.shape[0]
    nc = rows // batch // tile
    row = lambda b, c: (b * nc + c, 0)
    whole = lambda b, c: (0, 0)
    return _scan_with_riders(
        "gdn_prompt", (batch, nc), _gdn_prompt_kernel,
        arrays=(proj, small, proj, cw, gb, galog, mixw),
        in_specs=[
            pl.BlockSpec((tile, GDN_CONV), lambda b, c: (b * nc + c, COL_QKV // GDN_CONV)),
            pl.BlockSpec((tile, LANES), row),
            pl.BlockSpec((tile, GDN_W), lambda b, c: (b * nc + c, (COL_Z + SSD_W) // GDN_W)),
            pl.BlockSpec(cw.shape, whole), pl.BlockSpec(gb.shape, whole), pl.BlockSpec(galog.shape, whole),
            pl.BlockSpec((1, GDN_W), lambda b, c: (0, SSD_W // GDN_W)),
        ],
        out_shape=[
            jax.ShapeDtypeStruct((rows, GDN_W), BF16),
            jax.ShapeDtypeStruct((batch, SUBLANES, GDN_CONV), F32),
            jax.ShapeDtypeStruct((batch, GDN_HEADS, GDN_D, GDN_D), F32),
        ],
        out_specs=[
            pl.BlockSpec((tile, GDN_W), row),
            pl.BlockSpec((1, SUBLANES, GDN_CONV), lambda b, c: (b, 0, 0)),
            pl.BlockSpec((1, GDN_HEADS, GDN_D, GDN_D), lambda b, c: (b, 0, 0, 0)),
        ],
        scratch_shapes=[pltpu.VMEM((GDN_CONV // LANES, tile + SUBLANES, LANES), F32),
                        pltpu.VMEM((GDN_CONV // LANES, tile, LANES), F32), pltpu.VMEM((GDN_HEADS, GDN_D, GDN_D), F32)],
        riders=riders)


def _mem_prompt_kernel(q_ref, k_ref, v_ref, z_ref, mixw_ref, y_ref):
    heads = range(MEM_HEADS)
    hs = [slice(h * MEM_D, (h + 1) * MEM_D) for h in heads]
    s = [_dot_nt(q_ref[:, hs[h]].astype(BF16), k_ref[:, hs[h]].astype(BF16)) * (MEM_D ** -0.5) for h in heads]
    e = [jnp.exp(x - jnp.max(x, axis=-1, keepdims=True)) for x in s]
    p = [(x / jnp.sum(x, axis=-1, keepdims=True)).astype(BF16) for x in e]
    o = [_dot(p[h], v_ref[:, hs[h]].astype(BF16)) for h in heads]
    msq = [jnp.sum(x * x, axis=-1, keepdims=True) for x in o]
    for h in heads:
        y_ref[:, hs[h]] = _head_norm_gate(o[h], msq[h], MEM_D, z_ref[:, hs[h]], mixw_ref[:, hs[h]]).astype(BF16)


def _mem_prompt(proj, kv, batch, mixw, tq):
    rows = proj.shape[0]
    nq = rows // batch // tq
    return pl.pallas_call(
        _mem_prompt_kernel,
        grid=(batch, nq),
        in_specs=[
            pl.BlockSpec((tq, MEM_W), lambda b, i: (b * nq + i, COL_QMEM // MEM_W)),
            pl.BlockSpec((MEM_TOKENS, MEM_W), lambda b, i: (b, 0)),
            pl.BlockSpec((MEM_TOKENS, MEM_W), lambda b, i: (b, 1)),
            pl.BlockSpec((tq, MEM_W), lambda b, i: (b * nq + i, (COL_Z + SSD_W + GDN_W) // MEM_W)),
            pl.BlockSpec((1, MEM_W), lambda b, i: (0, (SSD_W + GDN_W) // MEM_W)),
        ],
        out_specs=pl.BlockSpec((tq, MEM_W), lambda b, i: (b * nq + i, 0)),
        out_shape=jax.ShapeDtypeStruct((rows, MEM_W), BF16),
        compiler_params=_params("parallel", "parallel"),
        name="mem_prompt",
    )(proj, kv, kv, proj, mixw)


def _conv_step(u, cst_ref, cst_out_ref, cw_ref, bias):
    acc = cw_ref[CONV_K - 1:CONV_K, :] * u
    if bias is not None:
        acc = acc + bias
    for j in range(CONV_K - 1):
        prev = cst_ref[j]
        acc = acc + cw_ref[j:j + 1, :] * prev
        if j > 0:
            cst_out_ref[j - 1] = prev
    cst_out_ref[CONV_K - 2] = u
    return _silu(acc)


def _rows_to_columns(x):
    pad = jnp.zeros((LANES - x.shape[0], x.shape[1]), F32)
    return jnp.concatenate([x, pad], axis=0).T


def _pick_rows(parts):
    rid = lax.broadcasted_iota(jnp.int32, parts[0].shape, 0)
    out = parts[0]
    for i in range(1, len(parts)):
        out = jnp.where(rid == i, parts[i], out)
    return out


def _ssd_decode_step(xbc_ref, sm_ref, z_ref, cst_ref, st_ref, cw_ref, cb_ref, dtb_ref, alog_ref, dexp_ref,
                     mixw_ref, e_ref, en_ref, y_ref, cst_out_ref, st_out_ref):
    xbc = _conv_step(xbc_ref[...], cst_ref, cst_out_ref, cw_ref, cb_ref[...])
    xs = xbc[:, :SSD_W]
    e = e_ref[...]
    dt = _softplus(sm_ref[...] + dtb_ref[...])
    dec = jnp.exp(dt * (-jnp.exp(alog_ref[...])))
    xd_t = _rows_to_columns(xs * _sel_right(dt, e))
    dec_n = _sel_right(dec, en_ref[...])

    groups, rows, per_group = range(SSD_GROUPS), range(xs.shape[0]), SSD_HEADS // SSD_GROUPS
    b_g = [xbc[:, SSD_W + g * SSD_N:SSD_W + (g + 1) * SSD_N] for g in groups]
    c_g = [xbc[:, SSD_W + (SSD_GROUPS + g) * SSD_N:SSD_W + (SSD_GROUPS + g + 1) * SSD_N].astype(BF16) for g in groups]
    hn = {}
    for g in groups:
        for i in rows:
            for h in range(g * per_group, (g + 1) * per_group):
                col = xd_t[h * SSD_P:(h + 1) * SSD_P, i:i + 1]
                hn[i, h] = st_ref[i, h] * dec_n[i:i + 1, h * SSD_N:(h + 1) * SSD_N] + col * b_g[g][i:i + 1, :]
                st_out_ref[i, h] = hn[i, h]
    hg = {(g, i): jnp.concatenate([hn[i, h] for h in range(g * per_group, (g + 1) * per_group)], axis=0).astype(BF16)
          for g in groups for i in rows}
    y_rows = {p: _dot_nt(c_g[p[0]], hg[p]) for p in hg}
    y = jnp.concatenate([_pick_rows([y_rows[g, i] for i in rows]) for g in groups], axis=1) + dexp_ref[...] * xs
    msq = _sel_right(_sel_right_nt(y * y, e, 2), e, 2)
    y_ref[...] = _head_norm_gate(y, msq, SSD_P, z_ref[...], mixw_ref[...])


def _gdn_decode_step(qkv_ref, sm_ref, z_ref, cst_ref, st_ref, cw_ref, gb_ref, galog_ref, mixw_ref, en_ref,
                     y_ref, cst_out_ref, st_out_ref):
    qkv = _conv_step(qkv_ref[...], cst_ref, cst_out_ref, cw_ref, None)
    sm = sm_ref[...]
    beta = _sigmoid(sm)
    eg = jnp.exp(-jnp.exp(galog_ref[...]) * _softplus(sm + gb_ref[...]))
    eg_n = _sel_right(eg, en_ref[...])

    qs, ks = [], []
    for h in range(GDN_HEADS):
        q = qkv[:, h * GDN_D:(h + 1) * GDN_D]
        k = qkv[:, GDN_W + h * GDN_D:GDN_W + (h + 1) * GDN_D]
        qs.append(q * lax.rsqrt(jnp.sum(q * q, axis=-1, keepdims=True) + EPS) * (GDN_D ** -0.5))
        ks.append(k * lax.rsqrt(jnp.sum(k * k, axis=-1, keepdims=True) + EPS))
    k_t = _rows_to_columns(jnp.concatenate(ks, axis=1))

    heads, rows = range(GDN_HEADS), range(sm.shape[0])
    n = len(rows)
    hs = [slice(h * GDN_D, (h + 1) * GDN_D) for h in heads]
    kq_b = [jnp.concatenate([ks[h], qs[h]], axis=0).astype(BF16) for h in heads]
    prod = {(h, i): _dot(kq_b[h], st_ref[i, h].astype(BF16)) for h in heads for i in rows}
    k_s = [_pick_rows([prod[h, i][:n] for i in rows]) for h in heads]
    q_s = [_pick_rows([prod[h, i][n:] for i in rows]) for h in heads]
    eg_h = [eg_n[:, hs[h]] for h in heads]
    v = [qkv[:, 2 * GDN_W + h * GDN_D:2 * GDN_W + (h + 1) * GDN_D] for h in heads]
    v_new = [beta[:, SM_B + h:SM_B + h + 1] * (v[h] - eg_h[h] * k_s[h]) for h in heads]
    o = [eg_h[h] * q_s[h] + jnp.sum(qs[h] * ks[h], axis=-1, keepdims=True) * v_new[h] for h in heads]
    for h in heads:
        for i in rows:
            st_out_ref[i, h] = st_ref[i, h] * eg_h[h][i:i + 1, :] + k_t[hs[h], i:i + 1] * v_new[h][i:i + 1, :]
    msq = [jnp.sum(x * x, axis=-1, keepdims=True) for x in o]
    for h in heads:
        y_ref[:, hs[h]] = _head_norm_gate(o[h], msq[h], GDN_D, z_ref[:, hs[h]], mixw_ref[:, hs[h]])


def _mem_decode_step(q_ref, k_ref, v_ref, z_ref, mixw_ref, y_ref):
    heads, rows = range(MEM_HEADS), range(q_ref.shape[0])
    hs = [slice(h * MEM_D, (h + 1) * MEM_D) for h in heads]
    win = [pl.ds(h, MEM_TOKENS, stride=MEM_HEADS) for h in heads]
    q = [q_ref[:, hs[h]].astype(BF16) for h in heads]
    s = [_pick_rows([_dot_nt(q[h], k_ref[i, win[h], :].astype(BF16)) for i in rows]) * (MEM_D ** -0.5) for h in heads]
    e = [jnp.exp(x - jnp.max(x, axis=-1, keepdims=True)) for x in s]
    p = [(x / jnp.sum(x, axis=-1, keepdims=True)).astype(BF16) for x in e]
    o = [_pick_rows([_dot(p[h], v_ref[i, win[h], :].astype(BF16)) for i in rows]) for h in heads]
    msq = [jnp.sum(x * x, axis=-1, keepdims=True) for x in o]
    for h in heads:
        y_ref[:, hs[h]] = _head_norm_gate(o[h], msq[h], MEM_D, z_ref[:, hs[h]], mixw_ref[:, hs[h]])


DEC_ROWS = SUBLANES


def _ssd_decode_kernel(proj_ref, sm_ref, cst_ref, st_ref, cw_ref, cb_ref, dtb_ref, alog_ref, dexp_ref, mixw_ref, e_ref, en_ref,
                       y_ref, cst_out_ref, st_out_ref):
    _ssd_decode_step(proj_ref.at[:, COL_XBC:COL_XBC + SSD_CONV], sm_ref, proj_ref.at[:, COL_Z:COL_Z + SSD_W], cst_ref, st_ref,
                     cw_ref, cb_ref, dtb_ref, alog_ref, dexp_ref, mixw_ref.at[:, :SSD_W], e_ref, en_ref,
                     y_ref, cst_out_ref, st_out_ref)


def _gdn_decode_kernel(proj_ref, sm_ref, cst_ref, st_ref, cw_ref, gb_ref, galog_ref, mixw_ref, en_ref,
                       y_ref, cst_out_ref, st_out_ref):
    _gdn_decode_step(proj_ref.at[:, COL_QKV:COL_QKV + GDN_CONV], sm_ref, proj_ref.at[:, COL_Z + SSD_W:COL_Z + SSD_W + GDN_W],
                     cst_ref, st_ref, cw_ref, gb_ref, galog_ref, mixw_ref.at[:, SSD_W:SSD_W + GDN_W], en_ref,
                     y_ref, cst_out_ref, st_out_ref)


def _ssd_mem_decode_kernel(proj_ref, sm_ref, cst_ref, st_ref, k_ref, v_ref, cw_ref, cb_ref, dtb_ref, alog_ref, dexp_ref,
                           mixw_ref, e_ref, en_ref, y_ssd_ref, y_mem_ref, cst_out_ref, st_out_ref):
    _mem_decode_step(proj_ref.at[:, COL_QMEM:COL_QMEM + MEM_W], k_ref, v_ref, proj_ref.at[:, N_MAIN - MEM_W:N_MAIN],
                     mixw_ref.at[:, SSD_W + GDN_W:], y_mem_ref)
    _ssd_decode_kernel(proj_ref, sm_ref, cst_ref, st_ref, cw_ref, cb_ref, dtb_ref, alog_ref, dexp_ref, mixw_ref, e_ref, en_ref,
                       y_ssd_ref, cst_out_ref, st_out_ref)


def _decode_rider(kernel, per_row_in, consts, out_shape, steps, step_index, early=()):
    groups = per_row_in[0].shape[0] // DEC_ROWS
    assert steps % groups == 0
    per_group = steps // groups

    def group_spec(a, lead=0):
        axis = 1 if len(a.shape) == 3 and a.shape[0] == CONV_K - 1 else 0
        block = a.shape[:axis] + (DEC_ROWS,) + a.shape[axis + 1:]
        group_of = lambda *g: jnp.minimum((step_index(*g) + lead) // per_group, groups - 1)
        return pl.BlockSpec(block, lambda *g: (0,) * axis + (group_of(*g),) + (0,) * (len(block) - axis - 1))

    whole = lambda a: pl.BlockSpec(a.shape, lambda *g, nd=a.ndim: (0,) * nd)
    lead = lambda n: min(1, per_group - 1) if n in early else 0
    return dict(kernel=kernel, arrays=tuple(per_row_in) + tuple(consts), steps_per_group=per_group,
                in_specs=[group_spec(a, lead(n)) for n, a in enumerate(per_row_in)] + [whole(a) for a in consts],
                out_shape=list(out_shape), out_specs=[group_spec(a) for a in out_shape])


def _scan_with_riders_kernel(*refs, scan_kernel, n_scan_in, n_scan_out, riders):
    scan_in, refs = refs[:n_scan_in], refs[n_scan_in:]
    rider_in = []
    for _, n_in, _, _ in riders:
        rider_in.append(refs[:n_in])
        refs = refs[n_in:]
    scan_out, refs = refs[:n_scan_out], refs[n_scan_out:]
    rider_out = []
    for _, _, n_out, _ in riders:
        rider_out.append(refs[:n_out])
        refs = refs[n_out:]
    step = pl.program_id(0) * pl.num_programs(1) + pl.program_id(1)
    for (kernel, _, _, per_group), ins, outs in zip(riders, rider_in, rider_out):
        if per_group == 1:
            kernel(*ins, *outs)
    scan_kernel(*scan_in, *scan_out, *refs)
    for (kernel, _, _, per_group), ins, outs in zip(riders, rider_in, rider_out):
        if per_group > 1:
            pl.when(step % per_group == 0)(functools.partial(kernel, *ins, *outs))


def _scan_with_riders(name, grid, scan_kernel, arrays, in_specs, out_shape, out_specs, scratch_shapes, riders):
    body = functools.partial(
        _scan_with_riders_kernel, scan_kernel=scan_kernel, n_scan_in=len(arrays), n_scan_out=len(out_shape),
        riders=tuple((r["kernel"], len(r["arrays"]), len(r["out_shape"]), r["steps_per_group"]) for r in riders))
    outs = pl.pallas_call(
        body,
        grid=grid,
        in_specs=list(in_specs) + [s for r in riders for s in r["in_specs"]],
        out_specs=list(out_specs) + [s for r in riders for s in r["out_specs"]],
        out_shape=list(out_shape) + [s for r in riders for s in r["out_shape"]],
        scratch_shapes=scratch_shapes,
        compiler_params=_params("arbitrary", "arbitrary"),
        name=name,
    )(*arrays, *[a for r in riders for a in r["arrays"]])
    scan_res, outs = outs[:len(out_shape)], outs[len(out_shape):]
    rider_res = []
    for r in riders:
        rider_res.append(outs[:len(r["out_shape"])])
        outs = outs[len(r["out_shape"]):]
    return scan_res, rider_res


IN_DT = SSD_CONV
IN_QKV = IN_DT + SSD_HEADS
IN_B = IN_QKV + GDN_CONV
IN_QMEM = IN_B + 2 * GDN_HEADS
IN_COLS = IN_QMEM + MEM_W + MIX_W
PREP_COLS = 256


def _prep_w_in_kernel(w_ref, x_ref, nw_ref, main_ref, small_ref, o_ref, os_ref, h_ref):
    i = pl.program_id(0)
    nblk = h_ref.shape[0]

    @pl.when(i == 0)
    def _():
        h = _rmsnorm_bf16(x_ref[...], nw_ref[...])
        for kb in range(nblk):
            h_ref[kb] = h[:, kb * PREP_COLS:(kb + 1) * PREP_COLS]
        o_ref[...] = jnp.zeros_like(o_ref)
        os_ref[...] = jnp.zeros_like(os_ref)

    main_ref[COL_QKV:COL_QKV + GDN_CONV, :] = w_ref[IN_QKV:IN_B, :].astype(BF16)
    main_ref[COL_XBC:COL_XBC + SSD_CONV, :] = w_ref[:SSD_CONV, :].astype(BF16)
    main_ref[COL_QMEM:N_MAIN, :] = w_ref[IN_QMEM:IN_COLS, :].astype(BF16)
    small_ref[SM_DT:SM_B, :] = w_ref[IN_DT:IN_QKV, :].astype(BF16)
    small_ref[SM_B:SM_A + GDN_HEADS, :] = w_ref[IN_B:IN_QMEM, :].astype(BF16)
    small_ref[SM_A + GDN_HEADS:, :] = jnp.zeros((LANES - SM_A - GDN_HEADS, w_ref.shape[1]), BF16)

    hb = h_ref[i]
    o_ref[...] += _dot_nt(hb, main_ref[...])
    os_ref[...] += _dot_nt(hb, small_ref[...])


def _prep_w_in(w_t, x, nw):
    n, k = w_t.shape
    rows = x.shape[0]
    assert n == IN_COLS
    whole = lambda i: (0, 0)
    return pl.pallas_call(
        _prep_w_in_kernel,
        grid=(k // PREP_COLS,),
        in_specs=[pl.BlockSpec((IN_COLS, PREP_COLS), lambda i: (0, i)), pl.BlockSpec((rows, k), whole),
                  pl.BlockSpec((1, k), whole)],
        out_specs=[pl.BlockSpec((N_MAIN, PREP_COLS), lambda i: (0, i)), pl.BlockSpec((LANES, PREP_COLS), lambda i: (0, i)),
                   pl.BlockSpec((rows, N_MAIN), whole), pl.BlockSpec((rows, LANES), whole)],
        out_shape=[jax.ShapeDtypeStruct((N_MAIN, k), BF16), jax.ShapeDtypeStruct((LANES, k), BF16),
                   jax.ShapeDtypeStruct((rows, N_MAIN), F32), jax.ShapeDtypeStruct((rows, LANES), F32)],
        scratch_shapes=[pltpu.VMEM((k // PREP_COLS, rows, PREP_COLS), BF16)],
        compiler_params=_params("arbitrary"),
        name="prep_w_in",
    )(w_t, x, nw)


def _head_expander(heads, first_lane, width):
    m = np.zeros((LANES, heads * width), np.float32)
    for h in range(heads):
        m[first_lane + h, h * width:(h + 1) * width] = 1.0
    return jnp.asarray(m, BF16)


def _lane_row(vec, first_lane):
    return jnp.zeros((1, LANES), F32).at[0, first_lane:first_lane + vec.shape[0]].set(vec.astype(F32))


def kernel(x_prompt, x_sample, mem_prompt, state_ssd_conv, state_ssd, state_gdn_conv, state_gdn, cache_mem_k, cache_mem_v, norm_w, w_in, ssd_conv_w, ssd_conv_b, ssd_dt_bias, ssd_A_log, ssd_D, gdn_conv_w, gdn_dt_bias, gdn_A_log, mem_norm_w, w_mem_kv, mix_norm_w, w_out, final_norm_w):
    bp, seq, d = x_prompt.shape
    bs = x_sample.shape[0]
    assert (d, seq % CHUNK, norm_w.shape[0]) == (D_MODEL, 0, 1)

    nw = norm_w[0][None, :]
    xs = x_sample.reshape(bs, d)
    w_main, w_small, proj_s, small_s = _prep_w_in(w_in[0].T, xs, nw)
    mixw = mix_norm_w[0][None, :]
    fw = final_norm_w[None, :]
    ssd_dtb = _lane_row(ssd_dt_bias[0], SM_DT)
    ssd_alog = _lane_row(ssd_A_log[0], SM_DT)
    ssd_dexp = jnp.repeat(ssd_D[0].astype(F32), SSD_P)[None, :]
    gdn_b = _lane_row(gdn_dt_bias[0], SM_A)
    gdn_alog = _lane_row(gdn_A_log[0], SM_A)
    e_ssd = _head_expander(SSD_HEADS, SM_DT, SSD_P)
    e_ssd_n = _head_expander(SSD_HEADS, SM_DT, SSD_N)
    e_gdn_n = _head_expander(GDN_HEADS, SM_A, GDN_D)
    ssd_cw, ssd_cb, gdn_cw = ssd_conv_w[0], ssd_conv_b[0][None, :], gdn_conv_w[0]

    xp = x_prompt.reshape(bp * seq, d)
    proj_p, small_p, wo, w_kv = _norm_matmul(
        xp, nw, w_main, w_small, _row_tile(bp * seq, PROJ_ROWS), SSD_CONV, casts=(w_out[0], w_mem_kv[0]))
    kv, k_prompt, v_prompt = _mem_kv(mem_prompt.reshape(bp * MEM_TOKENS, d), mem_norm_w[0][None, :], w_kv, MEM_TOKENS)
    scan_rows = SCAN_ROWS if seq % SCAN_ROWS == 0 else CHUNK
    scan_steps = bp * seq // scan_rows
    step_index = lambda b, c: b * (seq // scan_rows) + c

    ssd_cst, gdn_cst = jnp.swapaxes(state_ssd_conv[0], 0, 1), jnp.swapaxes(state_gdn_conv[0], 0, 1)
    mem_k = cache_mem_k.reshape(bs, MEM_TOKENS * MEM_HEADS, MEM_D)
    mem_v = cache_mem_v.reshape(bs, MEM_TOKENS * MEM_HEADS, MEM_D)
    row_out = lambda width: jax.ShapeDtypeStruct((bs, width), F32)
    like = lambda a: jax.ShapeDtypeStruct(a.shape, F32)
    ssd_rider = _decode_rider(
        _ssd_mem_decode_kernel, (proj_s, small_s, ssd_cst, state_ssd[0], mem_k, mem_v),
        (ssd_cw, ssd_cb, ssd_dtb, ssd_alog, ssd_dexp, mixw, e_ssd, e_ssd_n),
        (row_out(SSD_W), row_out(MEM_W), like(ssd_cst), like(state_ssd[0])), scan_steps, step_index, early=(4, 5))
    out_rows = _row_tile(bp * seq, OUT_ROWS)
    gdn_rider = _decode_rider(
        _gdn_decode_kernel, (proj_s, small_s, gdn_cst, state_gdn[0]), (gdn_cw, gdn_b, gdn_alog, mixw, e_gdn_n),
        (row_out(GDN_W), like(gdn_cst), like(state_gdn[0])), bp * seq // out_rows, lambda i, _: i)

    (y_ssd, tail_ssd, p_ssd), ((ys_ssd, ys_mem, s_ssd_conv, s_ssd),) = _ssd_prompt(
        proj_p, small_p, bp, ssd_cw, ssd_cb, ssd_dtb, ssd_alog, ssd_dexp, mixw, e_ssd, scan_rows, (ssd_rider,))
    (y_gdn, tail_gdn, p_gdn), _ = _gdn_prompt(proj_p, small_p, bp, gdn_cw, gdn_b, gdn_alog, mixw, scan_rows, ())
    y_mem = _mem_prompt(proj_p, kv, bp, mixw, _row_tile(seq, MEM_Q_ROWS))
    y_prompt, ((ys_gdn, s_gdn_conv, s_gdn),) = _out_proj(y_ssd, y_gdn, y_mem, wo, xp, fw, out_rows, (gdn_rider,))
    y_prompt = y_prompt.reshape(bp, seq, d)
    y_sample = _out_proj(ys_ssd, ys_gdn, ys_mem, wo, xs, fw, bs)[0].reshape(bs, 1, d)

    keep = CONV_K - 1
    mem_shape = (1, bp, MEM_TOKENS, MEM_HEADS, MEM_D)
    return (
        y_prompt, y_sample,
        tail_ssd[None, :, SUBLANES - keep:, :], p_ssd[None],
        tail_gdn[None, :, SUBLANES - keep:, :], p_gdn[None],
        k_prompt.reshape(mem_shape), v_prompt.reshape(mem_shape),
        jnp.swapaxes(s_ssd_conv, 0, 1)[None], s_ssd[None],
        jnp.swapaxes(s_gdn_conv, 0, 1)[None], s_gdn[None],
    )
```

```python
import functools

import numpy as np
import jax
import jax.numpy as jnp
from jax import lax
from jax.experimental import pallas as pl
from jax.experimental.pallas import tpu as pltpu

F32, BF16 = jnp.float32, jnp.bfloat16

D_MODEL = 2048
SSD_HEADS, SSD_P, SSD_GROUPS, SSD_N = 16, 64, 2, 128
SSD_W = SSD_HEADS * SSD_P
SSD_GW = SSD_W // SSD_GROUPS
SSD_CONV = SSD_W + 2 * SSD_GROUPS * SSD_N
GDN_HEADS, GDN_D = 8, 128
GDN_W = GDN_HEADS * GDN_D
GDN_CONV = 3 * GDN_W
MEM_TOKENS, MEM_HEADS, MEM_D = 256, 4, 128
MEM_W = MEM_HEADS * MEM_D
MIX_W = SSD_W + GDN_W + MEM_W
CONV_K = 4
CHUNK = 64
EPS = 1e-6

LANES = 128
SUBLANES = 8
VMEM_LIMIT = 60 * 1024 * 1024
PROJ_ROWS = 1024
OUT_ROWS = 512
MEM_Q_ROWS = 512
SCAN_ROWS = 256
CONV_PHASES = 4

COL_QKV = 0
COL_XBC = COL_QKV + GDN_CONV
COL_QMEM = COL_XBC + SSD_CONV
COL_Z = COL_QMEM + MEM_W
N_MAIN = COL_Z + MIX_W
SM_DT, SM_B, SM_A = 0, SSD_HEADS, SSD_HEADS + GDN_HEADS


def _dot(a, b):
    return jnp.dot(a, b, preferred_element_type=F32)


def _dot_nt(a, b):
    return lax.dot_general(a, b, (((1,), (1,)), ((), ())), preferred_element_type=F32)


def _dot_tn(a, b):
    return lax.dot_general(a, b, (((0,), (0,)), ((), ())), preferred_element_type=F32)


def _split(x, n):
    parts, r = [], x
    for i in range(n):
        p = r.astype(BF16)
        parts.append(p)
        if i + 1 < n:
            r = r - p.astype(F32)
    return parts


def _sel_left(sel, x, n=3):
    return functools.reduce(lambda a, b: a + b, [_dot(sel, p) for p in _split(x, n)])


def _sel_right(x, sel, n=3):
    return functools.reduce(lambda a, b: a + b, [_dot(p, sel) for p in _split(x, n)])


def _sel_right_nt(x, sel, n=3):
    return functools.reduce(lambda a, b: a + b, [_dot_nt(p, sel) for p in _split(x, n)])


def _transpose_sel(x, n=3):
    eye = _eye(LANES).astype(BF16)
    return functools.reduce(lambda a, b: a + b, [_dot_nt(eye, p) for p in _split(x, n)])


def _eye(n):
    return (lax.broadcasted_iota(jnp.int32, (n, n), 0) == lax.broadcasted_iota(jnp.int32, (n, n), 1)).astype(F32)


def _sigmoid(x):
    return 0.5 * jnp.tanh(0.5 * x) + 0.5


def _silu(x):
    h = 0.5 * x
    return h + h * jnp.tanh(h)


def _softplus(x):
    return jnp.maximum(x, 0.0) + jnp.log1p(jnp.exp(-jnp.abs(x)))


def _params(*sem):
    return pltpu.CompilerParams(dimension_semantics=sem, vmem_limit_bytes=VMEM_LIMIT)


def _row_tile(rows, preferred):
    return preferred if rows % preferred == 0 else rows


def _rmsnorm_bf16(x, nw):
    ms = jnp.mean(x * x, axis=-1, keepdims=True)
    return (x * lax.rsqrt(ms + EPS) * nw).astype(BF16)


def _norm_matmul_kernel(x_ref, nw_ref, w_ref, ws_ref, *refs, n_casts):
    cast_in, (o_ref, os_ref), cast_out, (h_ref,) = (
        refs[:n_casts], refs[n_casts:n_casts + 2], refs[n_casts + 2:2 * n_casts + 2], refs[2 * n_casts + 2:])

    @pl.when(pl.program_id(1) == 0)
    def _():
        h = _rmsnorm_bf16(x_ref[...], nw_ref[...])
        h_ref[...] = h
        os_ref[...] = _dot_nt(h, ws_ref[...])

    o_ref[...] = _dot_nt(h_ref[...], w_ref[...])
    for src, dst in zip(cast_in, cast_out):
        dst[...] = src[...].astype(BF16)


def _cast_block_rows(rows, steps):
    aligned = [r for r in range(2 * SUBLANES, rows + 1, 2 * SUBLANES) if rows % r == 0 and rows // r <= steps]
    return aligned[0] if aligned else rows


def _norm_matmul(x, nw, w_t, ws_t, tm, tn, casts=()):
    m, k = x.shape
    n = w_t.shape[0]
    ns = ws_t.shape[0]
    nj = n // tn
    steps = (m // tm) * nj

    def cast_spec(a):
        r = _cast_block_rows(a.shape[0], steps)
        return pl.BlockSpec((r, a.shape[1]), lambda i, j: (jnp.minimum(i * nj + j, a.shape[0] // r - 1), 0))

    return pl.pallas_call(
        functools.partial(_norm_matmul_kernel, n_casts=len(casts)),
        grid=(m // tm, nj),
        in_specs=[
            pl.BlockSpec((tm, k), lambda i, j: (i, 0)),
            pl.BlockSpec((1, k), lambda i, j: (0, 0)),
            pl.BlockSpec((tn, k), lambda i, j: (j, 0)),
            pl.BlockSpec((ns, k), lambda i, j: (0, 0)),
        ] + [cast_spec(a) for a in casts],
        out_specs=[
            pl.BlockSpec((tm, tn), lambda i, j: (i, j)),
            pl.BlockSpec((tm, ns), lambda i, j: (i, 0)),
        ] + [cast_spec(a) for a in casts],
        out_shape=[jax.ShapeDtypeStruct((m, n), F32), jax.ShapeDtypeStruct((m, ns), F32)]
        + [jax.ShapeDtypeStruct(a.shape, BF16) for a in casts],
        scratch_shapes=[pltpu.VMEM((tm, k), BF16)],
        compiler_params=_params("arbitrary", "arbitrary"),
        name="norm_matmul",
    )(x, nw, w_t, ws_t, *casts)


def _mem_kv_kernel(x_ref, nw_ref, w_ref, o_ref, k_ref, v_ref):
    kv = _dot(_rmsnorm_bf16(x_ref[...], nw_ref[...]), w_ref[...])
    o_ref[...] = kv
    for half, ref in enumerate((k_ref, v_ref)):
        for h in range(MEM_HEADS):
            ref[pl.ds(h, x_ref.shape[0], stride=MEM_HEADS), :] = kv[:, half * MEM_W + h * MEM_D:half * MEM_W + (h + 1) * MEM_D]


def _mem_kv(x, nw, w, tm):
    m, k = x.shape
    n = w.shape[1]
    assert n == 2 * MEM_W
    per_head = pl.BlockSpec((tm * MEM_HEADS, MEM_D), lambda i: (i, 0))
    return pl.pallas_call(
        _mem_kv_kernel,
        grid=(m // tm,),
        in_specs=[pl.BlockSpec((tm, k), lambda i: (i, 0)), pl.BlockSpec((1, k), lambda i: (0, 0)),
                  pl.BlockSpec((k, n), lambda i: (0, 0))],
        out_specs=[pl.BlockSpec((tm, n), lambda i: (i, 0)), per_head, per_head],
        out_shape=[jax.ShapeDtypeStruct((m, n), F32)] + [jax.ShapeDtypeStruct((m * MEM_HEADS, MEM_D), F32)] * 2,
        compiler_params=_params("parallel"),
        name="mem_kv",
    )(x, nw, w)


def _out_proj_kernel(y1_ref, y2_ref, y3_ref, w_ref, x_ref, fw_ref, o_ref):
    n1, n2 = y1_ref.shape[1], y1_ref.shape[1] + y2_ref.shape[1]
    acc = (_dot(y1_ref[...].astype(BF16), w_ref[:n1, :]) + _dot(y2_ref[...].astype(BF16), w_ref[n1:n2, :])
           + _dot(y3_ref[...].astype(BF16), w_ref[n2:, :]))
    r = x_ref[...] + acc
    ms = jnp.mean(r * r, axis=-1, keepdims=True)
    o_ref[...] = r * lax.rsqrt(ms + EPS) * fw_ref[...]


def _out_proj(y1, y2, y3, w, x, fw, tm, riders=()):
    m, d = x.shape[0], x.shape[-1]
    row = lambda i, _: (i, 0)
    whole = lambda i, _: (0, 0)
    x_spec = pl.BlockSpec((tm, None, d), lambda i, _: (i, 0, 0)) if x.ndim == 3 else pl.BlockSpec((tm, d), row)
    (out,), rider_res = _scan_with_riders(
        "out_proj", (m // tm, 1), _out_proj_kernel,
        arrays=(y1, y2, y3, w, x, fw),
        in_specs=[
            pl.BlockSpec((tm, y1.shape[1]), row), pl.BlockSpec((tm, y2.shape[1]), row), pl.BlockSpec((tm, y3.shape[1]), row),
            pl.BlockSpec(w.shape, whole, pipeline_mode=pl.Buffered(1)), x_spec, pl.BlockSpec((1, d), whole),
        ],
        out_shape=[jax.ShapeDtypeStruct(x.shape, F32)],
        out_specs=[x_spec],
        scratch_shapes=[],
        riders=riders)
    return out, rider_res


def _causal_conv_tile(u_ref, ubuf_ref, cw_ref, cb_ref, out_ref, tail_ref):
    t, width = u_ref.shape
    n = t // CONV_PHASES
    for s in range(width // LANES):
        cs = slice(s * LANES, (s + 1) * LANES)
        ubuf_ref[s, SUBLANES:SUBLANES + t, :] = u_ref[:, cs]
        taps = {d: ubuf_ref[s, pl.ds(SUBLANES + d, n, stride=CONV_PHASES), :] for d in range(1 - CONV_K, CONV_PHASES)}
        w = [cw_ref[j:j + 1, cs] for j in range(CONV_K)]
        for r in range(CONV_PHASES):
            acc = w[CONV_K - 1] * taps[r]
            if cb_ref is not None:
                acc = acc + cb_ref[:, cs]
            for j in range(CONV_K - 1):
                acc = acc + w[j] * taps[r - (CONV_K - 1) + j]
            out_ref[s, pl.ds(r, n, stride=CONV_PHASES), :] = _silu(acc)
        tail = ubuf_ref[s, t:t + SUBLANES, :]
        ubuf_ref[s, 0:SUBLANES, :] = tail
        tail_ref[0, :, cs] = tail


def _head_norm_gate(y, msq, width, z, mixw):
    return y * lax.rsqrt(msq * (1.0 / width) + EPS) * mixw * _silu(z)


def _ssd_prompt_kernel(xbc_ref, sm_ref, z_ref, cw_ref, cb_ref, dtb_ref, alog_ref, dexp_ref, mixw_ref, e_ref,
                       y_ref, tail_ref, state_ref, ubuf_ref, conv_ref, h_ref):
    c = pl.program_id(1)
    t = xbc_ref.shape[0]
    subs = range(t // CHUNK)
    groups = range(SSD_GROUPS)
    blocks = range(SSD_GW // LANES)

    @pl.when(c == 0)
    def _():
        ubuf_ref[:, 0:SUBLANES, :] = jnp.zeros((SSD_CONV // LANES, SUBLANES, LANES), F32)
        h_ref[...] = jnp.zeros_like(h_ref)

    _causal_conv_tile(xbc_ref, ubuf_ref, cw_ref, cb_ref, conv_ref, tail_ref)
    xs = jnp.concatenate([conv_ref[s] for s in range(SSD_W // LANES)], axis=1)
    e = e_ref[...]
    rows = [slice(j * CHUNK, (j + 1) * CHUNK) for j in subs]
    gs = [slice(g * SSD_GW, (g + 1) * SSD_GW) for g in groups]

    dt = _softplus(sm_ref[...] + dtb_ref[...])
    a = dt * (-jnp.exp(alog_ref[...]))
    rt = lax.broadcasted_iota(jnp.int32, (t, t), 0)
    ct = lax.broadcasted_iota(jnp.int32, (t, t), 1)
    chunk_causal = (rt >= ct) & (rt // CHUNK == ct // CHUNK)
    cum = _sel_left(chunk_causal.astype(BF16), a)
    cum_t = _transpose_sel(cum)
    ecum = jnp.exp(cum)
    wend = jnp.concatenate([jnp.exp(cum[(j + 1) * CHUNK - 1:(j + 1) * CHUNK, :] - cum[rows[j]]) for j in subs], axis=0)
    dt_x = _sel_right(dt, e, 2)
    ecum_x = _sel_right(ecum, e, 2)
    wend_x = _sel_right(wend, e, 2)

    xdt = xs * dt_x
    xdt_b = xdt.astype(BF16)
    xw_b = (xdt * wend_x).astype(BF16)
    lane = lax.broadcasted_iota(jnp.int32, (CHUNK, LANES), 1)
    causal = lax.broadcasted_iota(jnp.int32, (CHUNK, LANES), 0) >= lane % CHUNK
    first_head = lane < SSD_P

    jg = [(j, g) for j in subs for g in groups]
    b_slab, c_slab = SSD_W // LANES, SSD_W // LANES + SSD_GROUPS
    bmat = {(j, g): conv_ref[b_slab + g, rows[j], :].astype(BF16) for j, g in jg}
    cmat = {(j, g): conv_ref[c_slab + g, rows[j], :].astype(BF16) for j, g in jg}
    cb = {p: _dot_nt(cmat[p], jnp.concatenate([bmat[p], bmat[p]], axis=0)) for p in jg}
    inc = {(j, g): _dot_tn(bmat[j, g], xw_b[rows[j], gs[g]]) for j, g in jg}
    intra = {}
    for j, g in jg:
        for blk in blocks:
            a = (g * len(blocks) + blk) * 2
            col = jnp.where(first_head, cum[rows[j], a:a + 1], cum[rows[j], a + 1:a + 2])
            row = jnp.concatenate([cum_t[a:a + 1, rows[j]], cum_t[a + 1:a + 2, rows[j]]], axis=1)
            lmat = jnp.where(causal, jnp.exp(jnp.minimum(col - row, 0.0)), 0.0)
            lanes = slice(g * SSD_GW + blk * LANES, g * SSD_GW + (blk + 1) * LANES)
            intra[j, g, blk] = _dot((cb[j, g] * lmat).astype(BF16), _pair_diag(xdt_b[rows[j], lanes]))

    state = {(0, g): h_ref[:, gs[g]] for g in groups}
    for j in subs:
        last = (j + 1) * CHUNK - 1
        for g in groups:
            state[j + 1, g] = state[j, g] * ecum_x[last:last + 1, gs[g]] + inc[j, g]
    for g in groups:
        h_ref[:, gs[g]] = state[len(subs), g]
    inter = {p: _dot(cmat[p], state[p].astype(BF16)) for p in jg}
    inter_x = jnp.concatenate([jnp.concatenate([inter[j, g] for g in groups], axis=1) for j in subs], axis=0) * ecum_x
    intra_x = jnp.concatenate([jnp.concatenate([intra[j, g, blk] for g in groups for blk in blocks], axis=1)
                               for j in subs], axis=0)
    y = intra_x + inter_x + dexp_ref[...] * xs
    msq = _sel_right(_sel_right_nt(y * y, e, 2), e, 2)
    y_ref[...] = _head_norm_gate(y, msq, SSD_P, z_ref[...], mixw_ref[...]).astype(BF16)

    @pl.when(c == pl.num_programs(1) - 1)
    def _():
        state_ref[0] = h_ref[...].T.reshape(SSD_HEADS, SSD_P, SSD_N)


def _ssd_prompt(proj, small, batch, cw, cb, dtb, alog, dexp, mixw, e, tile, riders):
    rows = proj.shape[0]
    nc = rows // batch // tile
    row = lambda b, c: (b * nc + c, 0)
    whole = lambda b, c: (0, 0)
    return _scan_with_riders(
        "ssd_prompt", (batch, nc), _ssd_prompt_kernel,
        arrays=(proj, small, proj, cw, cb, dtb, alog, dexp, mixw, e),
        in_specs=[
            pl.BlockSpec((tile, SSD_CONV), lambda b, c: (b * nc + c, COL_XBC // SSD_CONV)),
            pl.BlockSpec((tile, LANES), row),
            pl.BlockSpec((tile, SSD_W), lambda b, c: (b * nc + c, COL_Z // SSD_W)),
            pl.BlockSpec(cw.shape, whole), pl.BlockSpec(cb.shape, whole), pl.BlockSpec(dtb.shape, whole),
            pl.BlockSpec(alog.shape, whole), pl.BlockSpec(dexp.shape, whole),
            pl.BlockSpec((1, SSD_W), whole),
            pl.BlockSpec(e.shape, whole),
        ],
        out_shape=[
            jax.ShapeDtypeStruct((rows, SSD_W), BF16),
            jax.ShapeDtypeStruct((batch, SUBLANES, SSD_CONV), F32),
            jax.ShapeDtypeStruct((batch, SSD_HEADS, SSD_P, SSD_N), F32),
        ],
        out_specs=[
            pl.BlockSpec((tile, SSD_W), row),
            pl.BlockSpec((1, SUBLANES, SSD_CONV), lambda b, c: (b, 0, 0)),
            pl.BlockSpec((1, SSD_HEADS, SSD_P, SSD_N), lambda b, c: (b, 0, 0, 0)),
        ],
        scratch_shapes=[pltpu.VMEM((SSD_CONV // LANES, tile + SUBLANES, LANES), F32),
                        pltpu.VMEM((SSD_CONV // LANES, tile, LANES), F32), pltpu.VMEM((SSD_N, SSD_W), F32)],
        riders=riders)


def _unit_lower_inverses(a_stricts, ri, ci):
    t = a_stricts[0].shape[0]
    eye = (ri == ci).astype(F32)
    first = (ri == ci + 1) & (ci % 2 == 0)
    invs = [eye - jnp.where(first, a, 0.0) for a in a_stricts]
    a_bs = [a.astype(BF16) for a in a_stricts]
    zero = jnp.zeros(a_bs[0].shape, BF16)
    s = 2
    while s < t:
        sel = (ri // (2 * s) == ci // (2 * s)) & ((ri // s) % 2 == 1) & ((ci // s) % 2 == 0)
        inv_bs = [inv.astype(BF16) for inv in invs]
        lefts = [_dot(inv_b, _pair_diag(jnp.where(sel, a_b, zero))).astype(BF16) for inv_b, a_b in zip(inv_bs, a_bs)]
        invs = [inv - _dot(left, _pair_diag(inv_b)) for inv, left, inv_b in zip(invs, lefts, inv_bs)]
        s *= 2
    return invs


def _pair_halves(x):
    left = lax.broadcasted_iota(jnp.int32, x.shape, 1) < x.shape[1] // 2
    zero = jnp.zeros_like(x)
    return jnp.where(left, x, zero), jnp.where(left, zero, x)


def _pair_diag(x):
    return jnp.concatenate(_pair_halves(x), axis=0)


def _gdn_prompt_kernel(qkv_ref, sm_ref, z_ref, cw_ref, gb_ref, galog_ref, mixw_ref,
                       y_ref, tail_ref, state_ref, ubuf_ref, conv_ref, s_ref):
    c = pl.program_id(1)
    t = qkv_ref.shape[0]
    subs = range(t // CHUNK)
    heads = range(GDN_HEADS)

    @pl.when(c == 0)
    def _():
        ubuf_ref[:, 0:SUBLANES, :] = jnp.zeros((GDN_CONV // LANES, SUBLANES, LANES), F32)
        s_ref[...] = jnp.zeros_like(s_ref)

    _causal_conv_tile(qkv_ref, ubuf_ref, cw_ref, None, conv_ref, tail_ref)

    sm = sm_ref[...]
    beta = _sigmoid(sm)
    g = -jnp.exp(galog_ref[...]) * _softplus(sm + gb_ref[...])
    rt = lax.broadcasted_iota(jnp.int32, (t, t), 0)
    ct = lax.broadcasted_iota(jnp.int32, (t, t), 1)
    chunk_causal = (rt >= ct) & (rt // CHUNK == ct // CHUNK)
    gc = _sel_left(chunk_causal.astype(BF16), g)
    gc_t = _transpose_sel(gc)
    eg = jnp.exp(gc)
    ri = lax.broadcasted_iota(jnp.int32, (CHUNK, 2 * CHUNK), 0)
    lane = lax.broadcasted_iota(jnp.int32, (CHUNK, 2 * CHUNK), 1)
    ci = lane % CHUNK
    causal = ri >= ci
    strict = ri > ci
    first_head = lane < CHUNK

    rows = [slice(j * CHUNK, (j + 1) * CHUNK) for j in subs]
    hs = [slice(h * GDN_D, (h + 1) * GDN_D) for h in heads]
    la = [SM_A + h for h in heads]
    packs = [(j, a) for j in subs for a in range(0, GDN_HEADS, 2)]
    q, k, kb, vb, kbg, qg = {}, {}, {}, {}, {}, {}
    for h in heads:
        qf, kf, vf = conv_ref[h], conv_ref[GDN_HEADS + h], conv_ref[2 * GDN_HEADS + h]
        qf = qf * lax.rsqrt(jnp.sum(qf * qf, axis=-1, keepdims=True) + EPS) * (GDN_D ** -0.5)
        kf = kf * lax.rsqrt(jnp.sum(kf * kf, axis=-1, keepdims=True) + EPS)
        b_col = beta[:, SM_B + h:SM_B + h + 1]
        eg_col = eg[:, la[h]:la[h] + 1]
        kbf = kf * b_col
        vbf, kbgf, qgf = (vf * b_col).astype(BF16), (kbf * eg_col).astype(BF16), (qf * eg_col).astype(BF16)
        for j in subs:
            q[j, h], k[j, h], kb[j, h] = qf[rows[j]].astype(BF16), kf[rows[j]], kbf[rows[j]].astype(BF16)
            vb[j, h], kbg[j, h], qg[j, h] = vbf[rows[j]], kbgf[rows[j]], qgf[rows[j]]
    decay, kq = {}, {}
    no_keys = jnp.zeros((CHUNK, GDN_D), BF16)
    for j, a in packs:
        b = a + 1
        col = jnp.where(first_head, gc[rows[j], la[a]:la[a] + 1], gc[rows[j], la[b]:la[b] + 1])
        row = jnp.concatenate([gc_t[la[a]:la[a] + 1, rows[j]], gc_t[la[b]:la[b] + 1, rows[j]]], axis=1)
        decay[j, a] = jnp.where(causal, jnp.exp(jnp.minimum(col - row, 0.0)), 0.0)
        kq[j, a] = (_dot_nt(jnp.concatenate([kb[j, a], q[j, a]], axis=0), jnp.concatenate([k[j, a].astype(BF16), no_keys], axis=0))
                    + _dot_nt(jnp.concatenate([kb[j, b], q[j, b]], axis=0), jnp.concatenate([no_keys, k[j, b].astype(BF16)], axis=0)))
    a_strict = [jnp.where(strict, kq[p][:CHUNK] * decay[p], 0.0) for p in packs]
    attn = {p: _pair_halves((kq[p][CHUNK:] * decay[p]).astype(BF16)) for p in packs}
    t_inv = dict(zip(packs, [_pair_halves(x.astype(BF16)) for x in _unit_lower_inverses(a_strict, ri, ci)]))
    u, wk = {}, {}
    for j, a in packs:
        rhs = jnp.concatenate([jnp.concatenate([vb[j, h], kbg[j, h]], axis=1) for h in (a, a + 1)], axis=0)
        for half, h in enumerate((a, a + 1)):
            uw = _dot(t_inv[j, a][half], rhs)
            u[j, h], wk[j, h] = uw[:, :GDN_D], uw[:, GDN_D:].astype(BF16)

    state = [s_ref[h] for h in heads]
    for j in subs:
        g_last = gc[(j + 1) * CHUNK - 1:(j + 1) * CHUNK, :]
        eend = jnp.exp(g_last - gc[rows[j]])
        elast = jnp.exp(g_last)
        s_b = [x.astype(BF16) for x in state]
        v_new = [(u[j, h] - _dot(wk[j, h], s_b[h])).astype(BF16) for h in heads]
        k_end = [(k[j, h] * eend[:, la[h]:la[h] + 1]).astype(BF16) for h in heads]
        s_inc = [_dot_tn(k_end[h], v_new[h]) for h in heads]
        state = [state[h] * elast[:, la[h]:la[h] + 1] + s_inc[h] for h in heads]
        v_pair = {a: jnp.concatenate([v_new[a], v_new[a + 1]], axis=0) for a in range(0, GDN_HEADS, 2)}
        o = [_dot(qg[j, h], s_b[h]) + _dot(attn[j, h - h % 2][h % 2], v_pair[h - h % 2]) for h in heads]
        msq = [jnp.sum(x * x, axis=-1, keepdims=True) for x in o]
        for h in heads:
            y_ref[rows[j], hs[h]] = _head_norm_gate(o[h], msq[h], GDN_D, z_ref[rows[j], hs[h]], mixw_ref[:, hs[h]]).astype(BF16)
    for h in heads:
        s_ref[h] = state[h]

    @pl.when(c == pl.num_programs(1) - 1)
    def _():
        state_ref[0] = s_ref[...]


def _gdn_prompt(proj, small, batch, cw, gb, galog, mixw, tile, riders):
    rows = proj.shape[0]
    nc = rows // batch // tile
    row = lambda b, c: (b * nc + c, 0)
    whole = lambda b, c: (0, 0)
    return _scan_with_riders(
        "gdn_prompt", (batch, nc), _gdn_prompt_kernel,
        arrays=(proj, small, proj, cw, gb, galog, mixw),
        in_specs=[
            pl.BlockSpec((tile, GDN_CONV), lambda b, c: (b * nc + c, COL_QKV // GDN_CONV)),
            pl.BlockSpec((tile, LANES), row),
            pl.BlockSpec((tile, GDN_W), lambda b, c: (b * nc + c, (COL_Z + SSD_W) // GDN_W)),
            pl.BlockSpec(cw.shape, whole), pl.BlockSpec(gb.shape, whole), pl.BlockSpec(galog.shape, whole),
            pl.BlockSpec((1, GDN_W), lambda b, c: (0, SSD_W // GDN_W)),
        ],
        out_shape=[
            jax.ShapeDtypeStruct((rows, GDN_W), BF16),
            jax.ShapeDtypeStruct((batch, SUBLANES, GDN_CONV), F32),
            jax.ShapeDtypeStruct((batch, GDN_HEADS, GDN_D, GDN_D), F32),
        ],
        out_specs=[
            pl.BlockSpec((tile, GDN_W), row),
            pl.BlockSpec((1, SUBLANES, GDN_CONV), lambda b, c: (b, 0, 0)),
            pl.BlockSpec((1, GDN_HEADS, GDN_D, GDN_D), lambda b, c: (b, 0, 0, 0)),
        ],
        scratch_shapes=[pltpu.VMEM((GDN_CONV // LANES, tile + SUBLANES, LANES), F32),
                        pltpu.VMEM((GDN_CONV // LANES, tile, LANES), F32), pltpu.VMEM((GDN_HEADS, GDN_D, GDN_D), F32)],
        riders=riders)


def _mem_prompt_kernel(q_ref, k_ref, v_ref, z_ref, mixw_ref, y_ref):
    heads = range(MEM_HEADS)
    hs = [slice(h * MEM_D, (h + 1) * MEM_D) for h in heads]
    s = [_dot_nt(q_ref[:, hs[h]].astype(BF16), k_ref[:, hs[h]].astype(BF16)) * (MEM_D ** -0.5) for h in heads]
    e = [jnp.exp(x - jnp.max(x, axis=-1, keepdims=True)) for x in s]
    p = [(x / jnp.sum(x, axis=-1, keepdims=True)).astype(BF16) for x in e]
    o = [_dot(p[h], v_ref[:, hs[h]].astype(BF16)) for h in heads]
    msq = [jnp.sum(x * x, axis=-1, keepdims=True) for x in o]
    for h in heads:
        y_ref[:, hs[h]] = _head_norm_gate(o[h], msq[h], MEM_D, z_ref[:, hs[h]], mixw_ref[:, hs[h]]).astype(BF16)


def _mem_prompt(proj, kv, batch, mixw, tq):
    rows = proj.shape[0]
    nq = rows // batch // tq
    return pl.pallas_call(
        _mem_prompt_kernel,
        grid=(batch, nq),
        in_specs=[
            pl.BlockSpec((tq, MEM_W), lambda b, i: (b * nq + i, COL_QMEM // MEM_W)),
            pl.BlockSpec((MEM_TOKENS, MEM_W), lambda b, i: (b, 0)),
            pl.BlockSpec((MEM_TOKENS, MEM_W), lambda b, i: (b, 1)),
            pl.BlockSpec((tq, MEM_W), lambda b, i: (b * nq + i, (COL_Z + SSD_W + GDN_W) // MEM_W)),
            pl.BlockSpec((1, MEM_W), lambda b, i: (0, (SSD_W + GDN_W) // MEM_W)),
        ],
        out_specs=pl.BlockSpec((tq, MEM_W), lambda b, i: (b * nq + i, 0)),
        out_shape=jax.ShapeDtypeStruct((rows, MEM_W), BF16),
        compiler_params=_params("parallel", "parallel"),
        name="mem_prompt",
    )(proj, kv, kv, proj, mixw)


def _conv_step(u, cst_ref, cst_out_ref, cw_ref, bias):
    acc = cw_ref[CONV_K - 1:CONV_K, :] * u
    if bias is not None:
        acc = acc + bias
    for j in range(CONV_K - 1):
        prev = cst_ref[j]
        acc = acc + cw_ref[j:j + 1, :] * prev
        if j > 0:
            cst_out_ref[j - 1] = prev
    cst_out_ref[CONV_K - 2] = u
    return _silu(acc)


def _rows_to_columns(x):
    pad = jnp.zeros((LANES - x.shape[0], x.shape[1]), F32)
    return jnp.concatenate([x, pad], axis=0).T


def _pick_rows(parts):
    rid = lax.broadcasted_iota(jnp.int32, parts[0].shape, 0)
    out = parts[0]
    for i in range(1, len(parts)):
        out = jnp.where(rid == i, parts[i], out)
    return out


def _ssd_decode_step(xbc_ref, sm_ref, z_ref, cst_ref, st_ref, cw_ref, cb_ref, dtb_ref, alog_ref, dexp_ref,
                     mixw_ref, e_ref, en_ref, y_ref, cst_out_ref, st_out_ref):
    xbc = _conv_step(xbc_ref[...], cst_ref, cst_out_ref, cw_ref, cb_ref[...])
    xs = xbc[:, :SSD_W]
    e = e_ref[...]
    dt = _softplus(sm_ref[...] + dtb_ref[...])
    dec = jnp.exp(dt * (-jnp.exp(alog_ref[...])))
    xd_t = _rows_to_columns(xs * _sel_right(dt, e))
    dec_n = _sel_right(dec, en_ref[...])

    groups, rows, per_group = range(SSD_GROUPS), range(xs.shape[0]), SSD_HEADS // SSD_GROUPS
    b_g = [xbc[:, SSD_W + g * SSD_N:SSD_W + (g + 1) * SSD_N] for g in groups]
    c_g = [xbc[:, SSD_W + (SSD_GROUPS + g) * SSD_N:SSD_W + (SSD_GROUPS + g + 1) * SSD_N].astype(BF16) for g in groups]
    hn = {}
    for g in groups:
        for i in rows:
            for h in range(g * per_group, (g + 1) * per_group):
                col = xd_t[h * SSD_P:(h + 1) * SSD_P, i:i + 1]
                hn[i, h] = st_ref[i, h] * dec_n[i:i + 1, h * SSD_N:(h + 1) * SSD_N] + col * b_g[g][i:i + 1, :]
                st_out_ref[i, h] = hn[i, h]
    hg = {(g, i): jnp.concatenate([hn[i, h] for h in range(g * per_group, (g + 1) * per_group)], axis=0).astype(BF16)
          for g in groups for i in rows}
    y_rows = {p: _dot_nt(c_g[p[0]], hg[p]) for p in hg}
    y = jnp.concatenate([_pick_rows([y_rows[g, i] for i in rows]) for g in groups], axis=1) + dexp_ref[...] * xs
    msq = _sel_right(_sel_right_nt(y * y, e, 2), e, 2)
    y_ref[...] = _head_norm_gate(y, msq, SSD_P, z_ref[...], mixw_ref[...])


def _gdn_decode_step(qkv_ref, sm_ref, z_ref, cst_ref, st_ref, cw_ref, gb_ref, galog_ref, mixw_ref, en_ref,
                     y_ref, cst_out_ref, st_out_ref):
    qkv = _conv_step(qkv_ref[...], cst_ref, cst_out_ref, cw_ref, None)
    sm = sm_ref[...]
    beta = _sigmoid(sm)
    eg = jnp.exp(-jnp.exp(galog_ref[...]) * _softplus(sm + gb_ref[...]))
    eg_n = _sel_right(eg, en_ref[...])

    qs, ks = [], []
    for h in range(GDN_HEADS):
        q = qkv[:, h * GDN_D:(h + 1) * GDN_D]
        k = qkv[:, GDN_W + h * GDN_D:GDN_W + (h + 1) * GDN_D]
        qs.append(q * lax.rsqrt(jnp.sum(q * q, axis=-1, keepdims=True) + EPS) * (GDN_D ** -0.5))
        ks.append(k * lax.rsqrt(jnp.sum(k * k, axis=-1, keepdims=True) + EPS))
    k_t = _rows_to_columns(jnp.concatenate(ks, axis=1))

    heads, rows = range(GDN_HEADS), range(sm.shape[0])
    n = len(rows)
    hs = [slice(h * GDN_D, (h + 1) * GDN_D) for h in heads]
    kq_b = [jnp.concatenate([ks[h], qs[h]], axis=0).astype(BF16) for h in heads]
    prod = {(h, i): _dot(kq_b[h], st_ref[i, h].astype(BF16)) for h in heads for i in rows}
    k_s = [_pick_rows([prod[h, i][:n] for i in rows]) for h in heads]
    q_s = [_pick_rows([prod[h, i][n:] for i in rows]) for h in heads]
    eg_h = [eg_n[:, hs[h]] for h in heads]
    v = [qkv[:, 2 * GDN_W + h * GDN_D:2 * GDN_W + (h + 1) * GDN_D] for h in heads]
    v_new = [beta[:, SM_B + h:SM_B + h + 1] * (v[h] - eg_h[h] * k_s[h]) for h in heads]
    o = [eg_h[h] * q_s[h] + jnp.sum(qs[h] * ks[h], axis=-1, keepdims=True) * v_new[h] for h in heads]
    for h in heads:
        for i in rows:
            st_out_ref[i, h] = st_ref[i, h] * eg_h[h][i:i + 1, :] + k_t[hs[h], i:i + 1] * v_new[h][i:i + 1, :]
    msq = [jnp.sum(x * x, axis=-1, keepdims=True) for x in o]
    for h in heads:
        y_ref[:, hs[h]] = _head_norm_gate(o[h], msq[h], GDN_D, z_ref[:, hs[h]], mixw_ref[:, hs[h]])


def _mem_decode_step(q_ref, k_ref, v_ref, z_ref, mixw_ref, y_ref):
    heads, rows = range(MEM_HEADS), range(q_ref.shape[0])
    hs = [slice(h * MEM_D, (h + 1) * MEM_D) for h in heads]
    win = [pl.ds(h, MEM_TOKENS, stride=MEM_HEADS) for h in heads]
    q = [q_ref[:, hs[h]].astype(BF16) for h in heads]
    s = [_pick_rows([_dot_nt(q[h], k_ref[i, win[h], :].astype(BF16)) for i in rows]) * (MEM_D ** -0.5) for h in heads]
    e = [jnp.exp(x - jnp.max(x, axis=-1, keepdims=True)) for x in s]
    p = [(x / jnp.sum(x, axis=-1, keepdims=True)).astype(BF16) for x in e]
    o = [_pick_rows([_dot(p[h], v_ref[i, win[h], :].astype(BF16)) for i in rows]) for h in heads]
    msq = [jnp.sum(x * x, axis=-1, keepdims=True) for x in o]
    for h in heads:
        y_ref[:, hs[h]] = _head_norm_gate(o[h], msq[h], MEM_D, z_ref[:, hs[h]], mixw_ref[:, hs[h]])


DEC_ROWS = SUBLANES


def _ssd_decode_kernel(proj_ref, sm_ref, cst_ref, st_ref, cw_ref, cb_ref, dtb_ref, alog_ref, dexp_ref, mixw_ref, e_ref, en_ref,
                       y_ref, cst_out_ref, st_out_ref):
    _ssd_decode_step(proj_ref.at[:, COL_XBC:COL_XBC + SSD_CONV], sm_ref, proj_ref.at[:, COL_Z:COL_Z + SSD_W], cst_ref, st_ref,
                     cw_ref, cb_ref, dtb_ref, alog_ref, dexp_ref, mixw_ref.at[:, :SSD_W], e_ref, en_ref,
                     y_ref, cst_out_ref, st_out_ref)


def _gdn_decode_kernel(proj_ref, sm_ref, cst_ref, st_ref, cw_ref, gb_ref, galog_ref, mixw_ref, en_ref,
                       y_ref, cst_out_ref, st_out_ref):
    _gdn_decode_step(proj_ref.at[:, COL_QKV:COL_QKV + GDN_CONV], sm_ref, proj_ref.at[:, COL_Z + SSD_W:COL_Z + SSD_W + GDN_W],
                     cst_ref, st_ref, cw_ref, gb_ref, galog_ref, mixw_ref.at[:, SSD_W:SSD_W + GDN_W], en_ref,
                     y_ref, cst_out_ref, st_out_ref)


def _ssd_mem_decode_kernel(proj_ref, sm_ref, cst_ref, st_ref, k_ref, v_ref, cw_ref, cb_ref, dtb_ref, alog_ref, dexp_ref,
                           mixw_ref, e_ref, en_ref, y_ssd_ref, y_mem_ref, cst_out_ref, st_out_ref):
    _mem_decode_step(proj_ref.at[:, COL_QMEM:COL_QMEM + MEM_W], k_ref, v_ref, proj_ref.at[:, N_MAIN - MEM_W:N_MAIN],
                     mixw_ref.at[:, SSD_W + GDN_W:], y_mem_ref)
    _ssd_decode_kernel(proj_ref, sm_ref, cst_ref, st_ref, cw_ref, cb_ref, dtb_ref, alog_ref, dexp_ref, mixw_ref, e_ref, en_ref,
                       y_ssd_ref, cst_out_ref, st_out_ref)


def _decode_rider(kernel, per_row_in, consts, out_shape, steps, step_index, early=()):
    groups = per_row_in[0].shape[0] // DEC_ROWS
    assert steps % groups == 0
    per_group = steps // groups

    def group_spec(a, lead=0):
        axis = 1 if len(a.shape) == 3 and a.shape[0] == CONV_K - 1 else 0
        block = a.shape[:axis] + (DEC_ROWS,) + a.shape[axis + 1:]
        group_of = lambda *g: jnp.minimum((step_index(*g) + lead) // per_group, groups - 1)
        return pl.BlockSpec(block, lambda *g: (0,) * axis + (group_of(*g),) + (0,) * (len(block) - axis - 1))

    whole = lambda a: pl.BlockSpec(a.shape, lambda *g, nd=a.ndim: (0,) * nd)
    lead = lambda n: min(1, per_group - 1) if n in early else 0
    return dict(kernel=kernel, arrays=tuple(per_row_in) + tuple(consts), steps_per_group=per_group,
                in_specs=[group_spec(a, lead(n)) for n, a in enumerate(per_row_in)] + [whole(a) for a in consts],
                out_shape=list(out_shape), out_specs=[group_spec(a) for a in out_shape])


def _scan_with_riders_kernel(*refs, scan_kernel, n_scan_in, n_scan_out, riders):
    scan_in, refs = refs[:n_scan_in], refs[n_scan_in:]
    rider_in = []
    for _, n_in, _, _ in riders:
        rider_in.append(refs[:n_in])
        refs = refs[n_in:]
    scan_out, refs = refs[:n_scan_out], refs[n_scan_out:]
    rider_out = []
    for _, _, n_out, _ in riders:
        rider_out.append(refs[:n_out])
        refs = refs[n_out:]
    step = pl.program_id(0) * pl.num_programs(1) + pl.program_id(1)
    for (kernel, _, _, per_group), ins, outs in zip(riders, rider_in, rider_out):
        if per_group == 1:
            kernel(*ins, *outs)
    scan_kernel(*scan_in, *scan_out, *refs)
    for (kernel, _, _, per_group), ins, outs in zip(riders, rider_in, rider_out):
        if per_group > 1:
            pl.when(step % per_group == 0)(functools.partial(kernel, *ins, *outs))


def _scan_with_riders(name, grid, scan_kernel, arrays, in_specs, out_shape, out_specs, scratch_shapes, riders):
    body = functools.partial(
        _scan_with_riders_kernel, scan_kernel=scan_kernel, n_scan_in=len(arrays), n_scan_out=len(out_shape),
        riders=tuple((r["kernel"], len(r["arrays"]), len(r["out_shape"]), r["steps_per_group"]) for r in riders))
    outs = pl.pallas_call(
        body,
        grid=grid,
        in_specs=list(in_specs) + [s for r in riders for s in r["in_specs"]],
        out_specs=list(out_specs) + [s for r in riders for s in r["out_specs"]],
        out_shape=list(out_shape) + [s for r in riders for s in r["out_shape"]],
        scratch_shapes=scratch_shapes,
        compiler_params=_params("arbitrary", "arbitrary"),
        name=name,
    )(*arrays, *[a for r in riders for a in r["arrays"]])
    scan_res, outs = outs[:len(out_shape)], outs[len(out_shape):]
    rider_res = []
    for r in riders:
        rider_res.append(outs[:len(r["out_shape"])])
        outs = outs[len(r["out_shape"]):]
    return scan_res, rider_res


IN_DT = SSD_CONV
IN_QKV = IN_DT + SSD_HEADS
IN_B = IN_QKV + GDN_CONV
IN_QMEM = IN_B + 2 * GDN_HEADS
IN_COLS = IN_QMEM + MEM_W + MIX_W
PREP_COLS = 256


def _prep_w_in_kernel(w_ref, x_ref, nw_ref, main_ref, small_ref, o_ref, os_ref, h_ref):
    i = pl.program_id(0)
    nblk = h_ref.shape[0]

    @pl.when(i == 0)
    def _():
        h = _rmsnorm_bf16(x_ref[...], nw_ref[...])
        for kb in range(nblk):
            h_ref[kb] = h[:, kb * PREP_COLS:(kb + 1) * PREP_COLS]
        o_ref[...] = jnp.zeros_like(o_ref)
        os_ref[...] = jnp.zeros_like(os_ref)

    main_ref[COL_QKV:COL_QKV + GDN_CONV, :] = w_ref[IN_QKV:IN_B, :].astype(BF16)
    main_ref[COL_XBC:COL_XBC + SSD_CONV, :] = w_ref[:SSD_CONV, :].astype(BF16)
    main_ref[COL_QMEM:N_MAIN, :] = w_ref[IN_QMEM:IN_COLS, :].astype(BF16)
    small_ref[SM_DT:SM_B, :] = w_ref[IN_DT:IN_QKV, :].astype(BF16)
    small_ref[SM_B:SM_A + GDN_HEADS, :] = w_ref[IN_B:IN_QMEM, :].astype(BF16)
    small_ref[SM_A + GDN_HEADS:, :] = jnp.zeros((LANES - SM_A - GDN_HEADS, w_ref.shape[1]), BF16)

    hb = h_ref[i]
    o_ref[...] += _dot_nt(hb, main_ref[...])
    os_ref[...] += _dot_nt(hb, small_ref[...])


def _prep_w_in(w_t, x, nw):
    n, k = w_t.shape
    rows = x.shape[0]
    assert n == IN_COLS
    whole = lambda i: (0, 0)
    x_spec = pl.BlockSpec((rows, None, k), lambda i: (0, 0, 0)) if x.ndim == 3 else pl.BlockSpec((rows, k), whole)
    return pl.pallas_call(
        _prep_w_in_kernel,
        grid=(k // PREP_COLS,),
        in_specs=[pl.BlockSpec((IN_COLS, PREP_COLS), lambda i: (0, i)), x_spec, pl.BlockSpec((1, k), whole)],
        out_specs=[pl.BlockSpec((N_MAIN, PREP_COLS), lambda i: (0, i)), pl.BlockSpec((LANES, PREP_COLS), lambda i: (0, i)),
                   pl.BlockSpec((rows, N_MAIN), whole), pl.BlockSpec((rows, LANES), whole)],
        out_shape=[jax.ShapeDtypeStruct((N_MAIN, k), BF16), jax.ShapeDtypeStruct((LANES, k), BF16),
                   jax.ShapeDtypeStruct((rows, N_MAIN), F32), jax.ShapeDtypeStruct((rows, LANES), F32)],
        scratch_shapes=[pltpu.VMEM((k // PREP_COLS, rows, PREP_COLS), BF16)],
        compiler_params=_params("arbitrary"),
        name="prep_w_in",
    )(w_t, x, nw)


def _head_expander(heads, first_lane, width):
    m = np.zeros((LANES, heads * width), np.float32)
    for h in range(heads):
        m[first_lane + h, h * width:(h + 1) * width] = 1.0
    return jnp.asarray(m, BF16)


def _lane_row(vec, first_lane):
    return jnp.zeros((1, LANES), F32).at[0, first_lane:first_lane + vec.shape[0]].set(vec.astype(F32))


def kernel(x_prompt, x_sample, mem_prompt, state_ssd_conv, state_ssd, state_gdn_conv, state_gdn, cache_mem_k, cache_mem_v, norm_w, w_in, ssd_conv_w, ssd_conv_b, ssd_dt_bias, ssd_A_log, ssd_D, gdn_conv_w, gdn_dt_bias, gdn_A_log, mem_norm_w, w_mem_kv, mix_norm_w, w_out, final_norm_w):
    bp, seq, d = x_prompt.shape
    bs = x_sample.shape[0]
    assert (d, seq % CHUNK, norm_w.shape[0]) == (D_MODEL, 0, 1)

    nw = norm_w[0][None, :]
    w_main, w_small, proj_s, small_s = _prep_w_in(w_in[0].T, x_sample, nw)
    mixw = mix_norm_w[0][None, :]
    fw = final_norm_w[None, :]
    ssd_dtb = _lane_row(ssd_dt_bias[0], SM_DT)
    ssd_alog = _lane_row(ssd_A_log[0], SM_DT)
    ssd_dexp = jnp.repeat(ssd_D[0].astype(F32), SSD_P)[None, :]
    gdn_b = _lane_row(gdn_dt_bias[0], SM_A)
    gdn_alog = _lane_row(gdn_A_log[0], SM_A)
    e_ssd = _head_expander(SSD_HEADS, SM_DT, SSD_P)
    e_ssd_n = _head_expander(SSD_HEADS, SM_DT, SSD_N)
    e_gdn_n = _head_expander(GDN_HEADS, SM_A, GDN_D)
    ssd_cw, ssd_cb, gdn_cw = ssd_conv_w[0], ssd_conv_b[0][None, :], gdn_conv_w[0]

    xp = x_prompt.reshape(bp * seq, d)
    proj_p, small_p, wo, w_kv = _norm_matmul(
        xp, nw, w_main, w_small, _row_tile(bp * seq, PROJ_ROWS), SSD_CONV, casts=(w_out[0], w_mem_kv[0]))
    kv, k_prompt, v_prompt = _mem_kv(mem_prompt.reshape(bp * MEM_TOKENS, d), mem_norm_w[0][None, :], w_kv, MEM_TOKENS)
    scan_rows = SCAN_ROWS if seq % SCAN_ROWS == 0 else CHUNK
    scan_steps = bp * seq // scan_rows
    step_index = lambda b, c: b * (seq // scan_rows) + c

    ssd_cst, gdn_cst = jnp.swapaxes(state_ssd_conv[0], 0, 1), jnp.swapaxes(state_gdn_conv[0], 0, 1)
    mem_k = cache_mem_k.reshape(bs, MEM_TOKENS * MEM_HEADS, MEM_D)
    mem_v = cache_mem_v.reshape(bs, MEM_TOKENS * MEM_HEADS, MEM_D)
    row_out = lambda width: jax.ShapeDtypeStruct((bs, width), F32)
    like = lambda a: jax.ShapeDtypeStruct(a.shape, F32)
    ssd_rider = _decode_rider(
        _ssd_mem_decode_kernel, (proj_s, small_s, ssd_cst, state_ssd[0], mem_k, mem_v),
        (ssd_cw, ssd_cb, ssd_dtb, ssd_alog, ssd_dexp, mixw, e_ssd, e_ssd_n),
        (row_out(SSD_W), row_out(MEM_W), like(ssd_cst), like(state_ssd[0])), scan_steps, step_index, early=(4, 5))
    out_rows = _row_tile(bp * seq, OUT_ROWS)
    gdn_rider = _decode_rider(
        _gdn_decode_kernel, (proj_s, small_s, gdn_cst, state_gdn[0]), (gdn_cw, gdn_b, gdn_alog, mixw, e_gdn_n),
        (row_out(GDN_W), like(gdn_cst), like(state_gdn[0])), bp * seq // out_rows, lambda i, _: i)

    (y_ssd, tail_ssd, p_ssd), ((ys_ssd, ys_mem, s_ssd_conv, s_ssd),) = _ssd_prompt(
        proj_p, small_p, bp, ssd_cw, ssd_cb, ssd_dtb, ssd_alog, ssd_dexp, mixw, e_ssd, scan_rows, (ssd_rider,))
    (y_gdn, tail_gdn, p_gdn), _ = _gdn_prompt(proj_p, small_p, bp, gdn_cw, gdn_b, gdn_alog, mixw, scan_rows, ())
    y_mem = _mem_prompt(proj_p, kv, bp, mixw, _row_tile(seq, MEM_Q_ROWS))
    y_prompt, ((ys_gdn, s_gdn_conv, s_gdn),) = _out_proj(y_ssd, y_gdn, y_mem, wo, xp, fw, out_rows, (gdn_rider,))
    y_prompt = y_prompt.reshape(bp, seq, d)
    y_sample = _out_proj(ys_ssd, ys_gdn, ys_mem, wo, x_sample, fw, bs)[0]

    keep = CONV_K - 1
    mem_shape = (1, bp, MEM_TOKENS, MEM_HEADS, MEM_D)
    return (
        y_prompt, y_sample,
        tail_ssd[None, :, SUBLANES - keep:, :], p_ssd[None],
        tail_gdn[None, :, SUBLANES - keep:, :], p_gdn[None],
        k_prompt.reshape(mem_shape), v_prompt.reshape(mem_shape),
        jnp.swapaxes(s_ssd_conv, 0, 1)[None], s_ssd[None],
        jnp.swapaxes(s_gdn_conv, 0, 1)[None], s_gdn[None],
    )
```

```python
import functools

import numpy as np
import jax
import jax.numpy as jnp
from jax import lax
from jax.experimental import pallas as pl
from jax.experimental.pallas import tpu as pltpu

F32, BF16 = jnp.float32, jnp.bfloat16

D_MODEL = 2048
SSD_HEADS, SSD_P, SSD_GROUPS, SSD_N = 16, 64, 2, 128
SSD_W = SSD_HEADS * SSD_P
SSD_GW = SSD_W // SSD_GROUPS
SSD_CONV = SSD_W + 2 * SSD_GROUPS * SSD_N
GDN_HEADS, GDN_D = 8, 128
GDN_W = GDN_HEADS * GDN_D
GDN_CONV = 3 * GDN_W
MEM_TOKENS, MEM_HEADS, MEM_D = 256, 4, 128
MEM_W = MEM_HEADS * MEM_D
MIX_W = SSD_W + GDN_W + MEM_W
CONV_K = 4
CHUNK = 64
EPS = 1e-6

LANES = 128
SUBLANES = 8
VMEM_LIMIT = 60 * 1024 * 1024
PROJ_ROWS = 1024
OUT_ROWS = 512
MEM_Q_ROWS = 512
SCAN_ROWS = 256
CONV_PHASES = 4

COL_QKV = 0
COL_XBC = COL_QKV + GDN_CONV
COL_QMEM = COL_XBC + SSD_CONV
COL_Z = COL_QMEM + MEM_W
N_MAIN = COL_Z + MIX_W
SM_DT, SM_B, SM_A = 0, SSD_HEADS, SSD_HEADS + GDN_HEADS


def _dot(a, b):
    return jnp.dot(a, b, preferred_element_type=F32)


def _dot_nt(a, b):
    return lax.dot_general(a, b, (((1,), (1,)), ((), ())), preferred_element_type=F32)


def _dot_tn(a, b):
    return lax.dot_general(a, b, (((0,), (0,)), ((), ())), preferred_element_type=F32)


def _split(x, n):
    parts, r = [], x
    for i in range(n):
        p = r.astype(BF16)
        parts.append(p)
        if i + 1 < n:
            r = r - p.astype(F32)
    return parts


def _sel_left(sel, x, n=3):
    return functools.reduce(lambda a, b: a + b, [_dot(sel, p) for p in _split(x, n)])


def _sel_right(x, sel, n=3):
    return functools.reduce(lambda a, b: a + b, [_dot(p, sel) for p in _split(x, n)])


def _sel_right_nt(x, sel, n=3):
    return functools.reduce(lambda a, b: a + b, [_dot_nt(p, sel) for p in _split(x, n)])


def _transpose_sel(x, n=3):
    eye = _eye(LANES).astype(BF16)
    return functools.reduce(lambda a, b: a + b, [_dot_nt(eye, p) for p in _split(x, n)])


def _eye(n):
    return (lax.broadcasted_iota(jnp.int32, (n, n), 0) == lax.broadcasted_iota(jnp.int32, (n, n), 1)).astype(F32)


def _sigmoid(x):
    return 0.5 * jnp.tanh(0.5 * x) + 0.5


def _silu(x):
    h = 0.5 * x
    return h + h * jnp.tanh(h)


def _softplus(x):
    return jnp.maximum(x, 0.0) + jnp.log1p(jnp.exp(-jnp.abs(x)))


def _params(*sem):
    return pltpu.CompilerParams(dimension_semantics=sem, vmem_limit_bytes=VMEM_LIMIT)


def _row_tile(rows, preferred):
    return preferred if rows % preferred == 0 else rows


def _rmsnorm_bf16(x, nw):
    ms = jnp.mean(x * x, axis=-1, keepdims=True)
    return (x * lax.rsqrt(ms + EPS) * nw).astype(BF16)


def _norm_matmul_kernel(x_ref, nw_ref, w_ref, ws_ref, *refs, n_casts):
    cast_in, (o_ref, os_ref), cast_out, (h_ref,) = (
        refs[:n_casts], refs[n_casts:n_casts + 2], refs[n_casts + 2:2 * n_casts + 2], refs[2 * n_casts + 2:])

    @pl.when(pl.program_id(1) == 0)
    def _():
        h = _rmsnorm_bf16(x_ref[...], nw_ref[...])
        h_ref[...] = h
        os_ref[...] = _dot_nt(h, ws_ref[...])

    o_ref[...] = _dot_nt(h_ref[...], w_ref[...])
    for src, dst in zip(cast_in, cast_out):
        dst[...] = src[...].astype(BF16)


def _cast_block_rows(rows, steps):
    aligned = [r for r in range(2 * SUBLANES, rows + 1, 2 * SUBLANES) if rows % r == 0 and rows // r <= steps]
    return aligned[0] if aligned else rows


def _norm_matmul(x, nw, w_t, ws_t, tm, tn, casts=()):
    m, k = x.shape
    n = w_t.shape[0]
    ns = ws_t.shape[0]
    nj = n // tn
    steps = (m // tm) * nj

    def cast_spec(a):
        r = _cast_block_rows(a.shape[0], steps)
        return pl.BlockSpec((r, a.shape[1]), lambda i, j: (jnp.minimum(i * nj + j, a.shape[0] // r - 1), 0))

    return pl.pallas_call(
        functools.partial(_norm_matmul_kernel, n_casts=len(casts)),
        grid=(m // tm, nj),
        in_specs=[
            pl.BlockSpec((tm, k), lambda i, j: (i, 0)),
            pl.BlockSpec((1, k), lambda i, j: (0, 0)),
            pl.BlockSpec((tn, k), lambda i, j: (j, 0)),
            pl.BlockSpec((ns, k), lambda i, j: (0, 0)),
        ] + [cast_spec(a) for a in casts],
        out_specs=[
            pl.BlockSpec((tm, tn), lambda i, j: (i, j)),
            pl.BlockSpec((tm, ns), lambda i, j: (i, 0)),
        ] + [cast_spec(a) for a in casts],
        out_shape=[jax.ShapeDtypeStruct((m, n), F32), jax.ShapeDtypeStruct((m, ns), F32)]
        + [jax.ShapeDtypeStruct(a.shape, BF16) for a in casts],
        scratch_shapes=[pltpu.VMEM((tm, k), BF16)],
        compiler_params=_params("arbitrary", "arbitrary"),
        name="norm_matmul",
    )(x, nw, w_t, ws_t, *casts)


def _mem_kv_kernel(x_ref, nw_ref, w_ref, o_ref, k_ref, v_ref):
    kv = _dot(_rmsnorm_bf16(x_ref[...], nw_ref[...]), w_ref[...])
    o_ref[...] = kv
    for half, ref in enumerate((k_ref, v_ref)):
        for h in range(MEM_HEADS):
            ref[pl.ds(h, x_ref.shape[0], stride=MEM_HEADS), :] = kv[:, half * MEM_W + h * MEM_D:half * MEM_W + (h + 1) * MEM_D]


def _mem_kv(x, nw, w, tm):
    m, k = x.shape
    n = w.shape[1]
    assert n == 2 * MEM_W
    per_head = pl.BlockSpec((tm * MEM_HEADS, MEM_D), lambda i: (i, 0))
    return pl.pallas_call(
        _mem_kv_kernel,
        grid=(m // tm,),
        in_specs=[pl.BlockSpec((tm, k), lambda i: (i, 0)), pl.BlockSpec((1, k), lambda i: (0, 0)),
                  pl.BlockSpec((k, n), lambda i: (0, 0))],
        out_specs=[pl.BlockSpec((tm, n), lambda i: (i, 0)), per_head, per_head],
        out_shape=[jax.ShapeDtypeStruct((m, n), F32)] + [jax.ShapeDtypeStruct((m * MEM_HEADS, MEM_D), F32)] * 2,
        compiler_params=_params("parallel"),
        name="mem_kv",
    )(x, nw, w)


def _out_proj_kernel(y1_ref, y2_ref, y3_ref, w_ref, x_ref, fw_ref, o_ref):
    y = jnp.concatenate([y1_ref[...].astype(BF16), y2_ref[...].astype(BF16), y3_ref[...].astype(BF16)], axis=1)
    r = x_ref[...] + _dot(y, w_ref[...])
    ms = jnp.mean(r * r, axis=-1, keepdims=True)
    o_ref[...] = r * lax.rsqrt(ms + EPS) * fw_ref[...]


def _out_proj(y1, y2, y3, w, x, fw, tm, riders=()):
    m, d = x.shape[0], x.shape[-1]
    row = lambda i, _: (i, 0)
    whole = lambda i, _: (0, 0)
    x_spec = pl.BlockSpec((tm, None, d), lambda i, _: (i, 0, 0)) if x.ndim == 3 else pl.BlockSpec((tm, d), row)
    (out,), rider_res = _scan_with_riders(
        "out_proj", (m // tm, 1), _out_proj_kernel,
        arrays=(y1, y2, y3, w, x, fw),
        in_specs=[
            pl.BlockSpec((tm, y1.shape[1]), row), pl.BlockSpec((tm, y2.shape[1]), row), pl.BlockSpec((tm, y3.shape[1]), row),
            pl.BlockSpec(w.shape, whole, pipeline_mode=pl.Buffered(1)), x_spec, pl.BlockSpec((1, d), whole),
        ],
        out_shape=[jax.ShapeDtypeStruct(x.shape, F32)],
        out_specs=[x_spec],
        scratch_shapes=[],
        riders=riders)
    return out, rider_res


def _causal_conv_tile(u_ref, ubuf_ref, cw_ref, cb_ref, out_ref, tail_ref):
    t, width = u_ref.shape
    n = t // CONV_PHASES
    for s in range(width // LANES):
        cs = slice(s * LANES, (s + 1) * LANES)
        ubuf_ref[s, SUBLANES:SUBLANES + t, :] = u_ref[:, cs]
        taps = {d: ubuf_ref[s, pl.ds(SUBLANES + d, n, stride=CONV_PHASES), :] for d in range(1 - CONV_K, CONV_PHASES)}
        w = [cw_ref[j:j + 1, cs] for j in range(CONV_K)]
        for r in range(CONV_PHASES):
            acc = w[CONV_K - 1] * taps[r]
            if cb_ref is not None:
                acc = acc + cb_ref[:, cs]
            for j in range(CONV_K - 1):
                acc = acc + w[j] * taps[r - (CONV_K - 1) + j]
            out_ref[s, pl.ds(r, n, stride=CONV_PHASES), :] = _silu(acc)
        tail = ubuf_ref[s, t:t + SUBLANES, :]
        ubuf_ref[s, 0:SUBLANES, :] = tail
        tail_ref[0, :, cs] = tail


def _head_norm_gate(y, msq, width, z, mixw):
    return y * lax.rsqrt(msq * (1.0 / width) + EPS) * mixw * _silu(z)


def _ssd_prompt_kernel(xbc_ref, sm_ref, z_ref, cw_ref, cb_ref, dtb_ref, alog_ref, dexp_ref, mixw_ref, e_ref,
                       y_ref, tail_ref, state_ref, ubuf_ref, conv_ref, h_ref):
    c = pl.program_id(1)
    t = xbc_ref.shape[0]
    subs = range(t // CHUNK)
    groups = range(SSD_GROUPS)
    blocks = range(SSD_GW // LANES)

    @pl.when(c == 0)
    def _():
        ubuf_ref[:, 0:SUBLANES, :] = jnp.zeros((SSD_CONV // LANES, SUBLANES, LANES), F32)
        h_ref[...] = jnp.zeros_like(h_ref)

    _causal_conv_tile(xbc_ref, ubuf_ref, cw_ref, cb_ref, conv_ref, tail_ref)
    xs = jnp.concatenate([conv_ref[s] for s in range(SSD_W // LANES)], axis=1)
    e = e_ref[...]
    rows = [slice(j * CHUNK, (j + 1) * CHUNK) for j in subs]
    gs = [slice(g * SSD_GW, (g + 1) * SSD_GW) for g in groups]

    dt = _softplus(sm_ref[...] + dtb_ref[...])
    a = dt * (-jnp.exp(alog_ref[...]))
    rt = lax.broadcasted_iota(jnp.int32, (t, t), 0)
    ct = lax.broadcasted_iota(jnp.int32, (t, t), 1)
    chunk_causal = (rt >= ct) & (rt // CHUNK == ct // CHUNK)
    cum = _sel_left(chunk_causal.astype(BF16), a)
    cum_t = _transpose_sel(cum)
    ecum = jnp.exp(cum)
    wend = jnp.concatenate([jnp.exp(cum[(j + 1) * CHUNK - 1:(j + 1) * CHUNK, :] - cum[rows[j]]) for j in subs], axis=0)
    dt_x = _sel_right(dt, e, 2)
    ecum_x = _sel_right(ecum, e, 2)
    wend_x = _sel_right(wend, e, 2)

    xdt = xs * dt_x
    xdt_b = xdt.astype(BF16)
    xw_b = (xdt * wend_x).astype(BF16)
    lane = lax.broadcasted_iota(jnp.int32, (CHUNK, LANES), 1)
    causal = lax.broadcasted_iota(jnp.int32, (CHUNK, LANES), 0) >= lane % CHUNK
    first_head = lane < SSD_P

    jg = [(j, g) for j in subs for g in groups]
    b_slab, c_slab = SSD_W // LANES, SSD_W // LANES + SSD_GROUPS
    bmat = {(j, g): conv_ref[b_slab + g, rows[j], :].astype(BF16) for j, g in jg}
    cmat = {(j, g): conv_ref[c_slab + g, rows[j], :].astype(BF16) for j, g in jg}
    cb = {p: _dot_nt(cmat[p], jnp.concatenate([bmat[p], bmat[p]], axis=0)) for p in jg}
    inc = {(j, g): _dot_tn(bmat[j, g], xw_b[rows[j], gs[g]]) for j, g in jg}
    intra = {}
    for j, g in jg:
        for blk in blocks:
            a = (g * len(blocks) + blk) * 2
            col = jnp.where(first_head, cum[rows[j], a:a + 1], cum[rows[j], a + 1:a + 2])
            row = jnp.concatenate([cum_t[a:a + 1, rows[j]], cum_t[a + 1:a + 2, rows[j]]], axis=1)
            lmat = jnp.where(causal, jnp.exp(jnp.minimum(col - row, 0.0)), 0.0)
            lanes = slice(g * SSD_GW + blk * LANES, g * SSD_GW + (blk + 1) * LANES)
            intra[j, g, blk] = _dot((cb[j, g] * lmat).astype(BF16), _pair_diag(xdt_b[rows[j], lanes]))

    state = {(0, g): h_ref[:, gs[g]] for g in groups}
    for j in subs:
        last = (j + 1) * CHUNK - 1
        for g in groups:
            state[j + 1, g] = state[j, g] * ecum_x[last:last + 1, gs[g]] + inc[j, g]
    for g in groups:
        h_ref[:, gs[g]] = state[len(subs), g]
    inter = {p: _dot(cmat[p], state[p].astype(BF16)) for p in jg}
    inter_x = jnp.concatenate([jnp.concatenate([inter[j, g] for g in groups], axis=1) for j in subs], axis=0) * ecum_x
    intra_x = jnp.concatenate([jnp.concatenate([intra[j, g, blk] for g in groups for blk in blocks], axis=1)
                               for j in subs], axis=0)
    y = intra_x + inter_x + dexp_ref[...] * xs
    msq = _sel_right(_sel_right_nt(y * y, e, 2), e, 2)
    y_ref[...] = _head_norm_gate(y, msq, SSD_P, z_ref[...], mixw_ref[...]).astype(BF16)

    @pl.when(c == pl.num_programs(1) - 1)
    def _():
        state_ref[0] = h_ref[...].T.reshape(SSD_HEADS, SSD_P, SSD_N)


def _ssd_prompt(proj, small, batch, cw, cb, dtb, alog, dexp, mixw, e, tile, riders):
    rows = proj.shape[0]
    nc = rows // batch // tile
    row = lambda b, c: (b * nc + c, 0)
    whole = lambda b, c: (0, 0)
    return _scan_with_riders(
        "ssd_prompt", (batch, nc), _ssd_prompt_kernel,
        arrays=(proj, small, proj, cw, cb, dtb, alog, dexp, mixw, e),
        in_specs=[
            pl.BlockSpec((tile, SSD_CONV), lambda b, c: (b * nc + c, COL_XBC // SSD_CONV)),
            pl.BlockSpec((tile, LANES), row),
            pl.BlockSpec((tile, SSD_W), lambda b, c: (b * nc + c, COL_Z // SSD_W)),
            pl.BlockSpec(cw.shape, whole), pl.BlockSpec(cb.shape, whole), pl.BlockSpec(dtb.shape, whole),
            pl.BlockSpec(alog.shape, whole), pl.BlockSpec(dexp.shape, whole),
            pl.BlockSpec((1, SSD_W), whole),
            pl.BlockSpec(e.shape, whole),
        ],
        out_shape=[
            jax.ShapeDtypeStruct((rows, SSD_W), BF16),
            jax.ShapeDtypeStruct((batch, SUBLANES, SSD_CONV), F32),
            jax.ShapeDtypeStruct((batch, SSD_HEADS, SSD_P, SSD_N), F32),
        ],
        out_specs=[
            pl.BlockSpec((tile, SSD_W), row),
            pl.BlockSpec((1, SUBLANES, SSD_CONV), lambda b, c: (b, 0, 0)),
            pl.BlockSpec((1, SSD_HEADS, SSD_P, SSD_N), lambda b, c: (b, 0, 0, 0)),
        ],
        scratch_shapes=[pltpu.VMEM((SSD_CONV // LANES, tile + SUBLANES, LANES), F32),
                        pltpu.VMEM((SSD_CONV // LANES, tile, LANES), F32), pltpu.VMEM((SSD_N, SSD_W), F32)],
        riders=riders)


def _unit_lower_inverses(a_stricts, ri, ci):
    t = a_stricts[0].shape[0]
    eye = (ri == ci).astype(F32)
    first = (ri == ci + 1) & (ci % 2 == 0)
    invs = [eye - jnp.where(first, a, 0.0) for a in a_stricts]
    a_bs = [a.astype(BF16) for a in a_stricts]
    zero = jnp.zeros(a_bs[0].shape, BF16)
    s = 2
    while s < t:
        sel = (ri // (2 * s) == ci // (2 * s)) & ((ri // s) % 2 == 1) & ((ci // s) % 2 == 0)
        inv_bs = [inv.astype(BF16) for inv in invs]
        lefts = [_dot(inv_b, _pair_diag(jnp.where(sel, a_b, zero))).astype(BF16) for inv_b, a_b in zip(inv_bs, a_bs)]
        invs = [inv - _dot(left, _pair_diag(inv_b)) for inv, left, inv_b in zip(invs, lefts, inv_bs)]
        s *= 2
    return invs


def _pair_halves(x):
    left = lax.broadcasted_iota(jnp.int32, x.shape, 1) < x.shape[1] // 2
    zero = jnp.zeros_like(x)
    return jnp.where(left, x, zero), jnp.where(left, zero, x)


def _pair_diag(x):
    return jnp.concatenate(_pair_halves(x), axis=0)


def _gdn_prompt_kernel(qkv_ref, sm_ref, z_ref, cw_ref, gb_ref, galog_ref, mixw_ref,
                       y_ref, tail_ref, state_ref, ubuf_ref, conv_ref, s_ref):
    c = pl.program_id(1)
    t = qkv_ref.shape[0]
    subs = range(t // CHUNK)
    heads = range(GDN_HEADS)

    @pl.when(c == 0)
    def _():
        ubuf_ref[:, 0:SUBLANES, :] = jnp.zeros((GDN_CONV // LANES, SUBLANES, LANES), F32)
        s_ref[...] = jnp.zeros_like(s_ref)

    _causal_conv_tile(qkv_ref, ubuf_ref, cw_ref, None, conv_ref, tail_ref)

    sm = sm_ref[...]
    beta = _sigmoid(sm)
    g = -jnp.exp(galog_ref[...]) * _softplus(sm + gb_ref[...])
    rt = lax.broadcasted_iota(jnp.int32, (t, t), 0)
    ct = lax.broadcasted_iota(jnp.int32, (t, t), 1)
    chunk_causal = (rt >= ct) & (rt // CHUNK == ct // CHUNK)
    gc = _sel_left(chunk_causal.astype(BF16), g)
    gc_t = _transpose_sel(gc)
    eg = jnp.exp(gc)
    ri = lax.broadcasted_iota(jnp.int32, (CHUNK, 2 * CHUNK), 0)
    lane = lax.broadcasted_iota(jnp.int32, (CHUNK, 2 * CHUNK), 1)
    ci = lane % CHUNK
    causal = ri >= ci
    strict = ri > ci
    first_head = lane < CHUNK

    rows = [slice(j * CHUNK, (j + 1) * CHUNK) for j in subs]
    hs = [slice(h * GDN_D, (h + 1) * GDN_D) for h in heads]
    la = [SM_A + h for h in heads]
    packs = [(j, a) for j in subs for a in range(0, GDN_HEADS, 2)]
    q, k, kb, vb, kbg, qg = {}, {}, {}, {}, {}, {}
    for h in heads:
        qf, kf, vf = conv_ref[h], conv_ref[GDN_HEADS + h], conv_ref[2 * GDN_HEADS + h]
        qf = qf * lax.rsqrt(jnp.sum(qf * qf, axis=-1, keepdims=True) + EPS) * (GDN_D ** -0.5)
        kf = kf * lax.rsqrt(jnp.sum(kf * kf, axis=-1, keepdims=True) + EPS)
        b_col = beta[:, SM_B + h:SM_B + h + 1]
        eg_col = eg[:, la[h]:la[h] + 1]
        kbf = kf * b_col
        vbf, kbgf, qgf = (vf * b_col).astype(BF16), (kbf * eg_col).astype(BF16), (qf * eg_col).astype(BF16)
        for j in subs:
            q[j, h], k[j, h], kb[j, h] = qf[rows[j]].astype(BF16), kf[rows[j]], kbf[rows[j]].astype(BF16)
            vb[j, h], kbg[j, h], qg[j, h] = vbf[rows[j]], kbgf[rows[j]], qgf[rows[j]]
    decay, kq = {}, {}
    no_keys = jnp.zeros((CHUNK, GDN_D), BF16)
    for j, a in packs:
        b = a + 1
        col = jnp.where(first_head, gc[rows[j], la[a]:la[a] + 1], gc[rows[j], la[b]:la[b] + 1])
        row = jnp.concatenate([gc_t[la[a]:la[a] + 1, rows[j]], gc_t[la[b]:la[b] + 1, rows[j]]], axis=1)
        decay[j, a] = jnp.where(causal, jnp.exp(jnp.minimum(col - row, 0.0)), 0.0)
        kq[j, a] = (_dot_nt(jnp.concatenate([kb[j, a], q[j, a]], axis=0), jnp.concatenate([k[j, a].astype(BF16), no_keys], axis=0))
                    + _dot_nt(jnp.concatenate([kb[j, b], q[j, b]], axis=0), jnp.concatenate([no_keys, k[j, b].astype(BF16)], axis=0)))
    a_strict = [jnp.where(strict, kq[p][:CHUNK] * decay[p], 0.0) for p in packs]
    attn = {p: _pair_halves((kq[p][CHUNK:] * decay[p]).astype(BF16)) for p in packs}
    t_inv = dict(zip(packs, [_pair_halves(x.astype(BF16)) for x in _unit_lower_inverses(a_strict, ri, ci)]))
    u, wk = {}, {}
    for j, a in packs:
        rhs = jnp.concatenate([jnp.concatenate([vb[j, h], kbg[j, h]], axis=1) for h in (a, a + 1)], axis=0)
        for half, h in enumerate((a, a + 1)):
            uw = _dot(t_inv[j, a][half], rhs)
            u[j, h], wk[j, h] = uw[:, :GDN_D], uw[:, GDN_D:].astype(BF16)

    state = [s_ref[h] for h in heads]
    for j in subs:
        g_last = gc[(j + 1) * CHUNK - 1:(j + 1) * CHUNK, :]
        eend = jnp.exp(g_last - gc[rows[j]])
        elast = jnp.exp(g_last)
        s_b = [x.astype(BF16) for x in state]
        v_new = [(u[j, h] - _dot(wk[j, h], s_b[h])).astype(BF16) for h in heads]
        k_end = [(k[j, h] * eend[:, la[h]:la[h] + 1]).astype(BF16) for h in heads]
        s_inc = [_dot_tn(k_end[h], v_new[h]) for h in heads]
        state = [state[h] * elast[:, la[h]:la[h] + 1] + s_inc[h] for h in heads]
        v_pair = {a: jnp.concatenate([v_new[a], v_new[a + 1]], axis=0) for a in range(0, GDN_HEADS, 2)}
        o = [_dot(qg[j, h], s_b[h]) + _dot(attn[j, h - h % 2][h % 2], v_pair[h - h % 2]) for h in heads]
        msq = [jnp.sum(x * x, axis=-1, keepdims=True) for x in o]
        for h in heads:
            y_ref[rows[j], hs[h]] = _head_norm_gate(o[h], msq[h], GDN_D, z_ref[rows[j], hs[h]], mixw_ref[:, hs[h]]).astype(BF16)
    for h in heads:
        s_ref[h] = state[h]

    @pl.when(c == pl.num_programs(1) - 1)
    def _():
        state_ref[0] = s_ref[...]


def _gdn_prompt(proj, small, batch, cw, gb, galog, mixw, tile, riders):
    rows = proj.shape[0]
    nc = rows // batch // tile
    row = lambda b, c: (b * nc + c, 0)
    whole = lambda b, c: (0, 0)
    return _scan_with_riders(
        "gdn_prompt", (batch, nc), _gdn_prompt_kernel,
        arrays=(proj, small, proj, cw, gb, galog, mixw),
        in_specs=[
            pl.BlockSpec((tile, GDN_CONV), lambda b, c: (b * nc + c, COL_QKV // GDN_CONV)),
            pl.BlockSpec((tile, LANES), row),
            pl.BlockSpec((tile, GDN_W), lambda b, c: (b * nc + c, (COL_Z + SSD_W) // GDN_W)),
            pl.BlockSpec(cw.shape, whole), pl.BlockSpec(gb.shape, whole), pl.BlockSpec(galog.shape, whole),
            pl.BlockSpec((1, GDN_W), lambda b, c: (0, SSD_W // GDN_W)),
        ],
        out_shape=[
            jax.ShapeDtypeStruct((rows, GDN_W), BF16),
            jax.ShapeDtypeStruct((batch, SUBLANES, GDN_CONV), F32),
            jax.ShapeDtypeStruct((batch, GDN_HEADS, GDN_D, GDN_D), F32),
        ],
        out_specs=[
            pl.BlockSpec((tile, GDN_W), row),
            pl.BlockSpec((1, SUBLANES, GDN_CONV), lambda b, c: (b, 0, 0)),
            pl.BlockSpec((1, GDN_HEADS, GDN_D, GDN_D), lambda b, c: (b, 0, 0, 0)),
        ],
        scratch_shapes=[pltpu.VMEM((GDN_CONV // LANES, tile + SUBLANES, LANES), F32),
                        pltpu.VMEM((GDN_CONV // LANES, tile, LANES), F32), pltpu.VMEM((GDN_HEADS, GDN_D, GDN_D), F32)],
        riders=riders)


def _mem_prompt_kernel(q_ref, k_ref, v_ref, z_ref, mixw_ref, y_ref):
    heads = range(MEM_HEADS)
    hs = [slice(h * MEM_D, (h + 1) * MEM_D) for h in heads]
    s = [_dot_nt(q_ref[:, hs[h]].astype(BF16), k_ref[:, hs[h]].astype(BF16)) * (MEM_D ** -0.5) for h in heads]
    e = [jnp.exp(x - jnp.max(x, axis=-1, keepdims=True)) for x in s]
    p = [(x / jnp.sum(x, axis=-1, keepdims=True)).astype(BF16) for x in e]
    o = [_dot(p[h], v_ref[:, hs[h]].astype(BF16)) for h in heads]
    msq = [jnp.sum(x * x, axis=-1, keepdims=True) for x in o]
    for h in heads:
        y_ref[:, hs[h]] = _head_norm_gate(o[h], msq[h], MEM_D, z_ref[:, hs[h]], mixw_ref[:, hs[h]]).astype(BF16)


def _mem_prompt(proj, kv, batch, mixw, tq):
    rows = proj.shape[0]
    nq = rows // batch // tq
    return pl.pallas_call(
        _mem_prompt_kernel,
        grid=(batch, nq),
        in_specs=[
            pl.BlockSpec((tq, MEM_W), lambda b, i: (b * nq + i, COL_QMEM // MEM_W)),
            pl.BlockSpec((MEM_TOKENS, MEM_W), lambda b, i: (b, 0)),
            pl.BlockSpec((MEM_TOKENS, MEM_W), lambda b, i: (b, 1)),
            pl.BlockSpec((tq, MEM_W), lambda b, i: (b * nq + i, (COL_Z + SSD_W + GDN_W) // MEM_W)),
            pl.BlockSpec((1, MEM_W), lambda b, i: (0, (SSD_W + GDN_W) // MEM_W)),
        ],
        out_specs=pl.BlockSpec((tq, MEM_W), lambda b, i: (b * nq + i, 0)),
        out_shape=jax.ShapeDtypeStruct((rows, MEM_W), BF16),
        compiler_params=_params("parallel", "parallel"),
        name="mem_prompt",
    )(proj, kv, kv, proj, mixw)


def _conv_step(u, cst_ref, cst_out_ref, cw_ref, bias):
    acc = cw_ref[CONV_K - 1:CONV_K, :] * u
    if bias is not None:
        acc = acc + bias
    for j in range(CONV_K - 1):
        prev = cst_ref[j]
        acc = acc + cw_ref[j:j + 1, :] * prev
        if j > 0:
            cst_out_ref[j - 1] = prev
    cst_out_ref[CONV_K - 2] = u
    return _silu(acc)


def _rows_to_columns(x):
    pad = jnp.zeros((LANES - x.shape[0], x.shape[1]), F32)
    return jnp.concatenate([x, pad], axis=0).T


def _pick_rows(parts):
    rid = lax.broadcasted_iota(jnp.int32, parts[0].shape, 0)
    out = parts[0]
    for i in range(1, len(parts)):
        out = jnp.where(rid == i, parts[i], out)
    return out


def _ssd_decode_step(xbc_ref, sm_ref, z_ref, cst_ref, st_ref, cw_ref, cb_ref, dtb_ref, alog_ref, dexp_ref,
                     mixw_ref, e_ref, en_ref, y_ref, cst_out_ref, st_out_ref):
    xbc = _conv_step(xbc_ref[...], cst_ref, cst_out_ref, cw_ref, cb_ref[...])
    xs = xbc[:, :SSD_W]
    e = e_ref[...]
    dt = _softplus(sm_ref[...] + dtb_ref[...])
    dec = jnp.exp(dt * (-jnp.exp(alog_ref[...])))
    xd_t = _rows_to_columns(xs * _sel_right(dt, e))
    dec_n = _sel_right(dec, en_ref[...])

    groups, rows, per_group = range(SSD_GROUPS), range(xs.shape[0]), SSD_HEADS // SSD_GROUPS
    b_g = [xbc[:, SSD_W + g * SSD_N:SSD_W + (g + 1) * SSD_N] for g in groups]
    c_g = [xbc[:, SSD_W + (SSD_GROUPS + g) * SSD_N:SSD_W + (SSD_GROUPS + g + 1) * SSD_N].astype(BF16) for g in groups]
    hn = {}
    for g in groups:
        for i in rows:
            for h in range(g * per_group, (g + 1) * per_group):
                col = xd_t[h * SSD_P:(h + 1) * SSD_P, i:i + 1]
                hn[i, h] = st_ref[i, h] * dec_n[i:i + 1, h * SSD_N:(h + 1) * SSD_N] + col * b_g[g][i:i + 1, :]
                st_out_ref[i, h] = hn[i, h]
    hg = {(g, i): jnp.concatenate([hn[i, h] for h in range(g * per_group, (g + 1) * per_group)], axis=0).astype(BF16)
          for g in groups for i in rows}
    y_rows = {p: _dot_nt(c_g[p[0]], hg[p]) for p in hg}
    y = jnp.concatenate([_pick_rows([y_rows[g, i] for i in rows]) for g in groups], axis=1) + dexp_ref[...] * xs
    msq = _sel_right(_sel_right_nt(y * y, e, 2), e, 2)
    y_ref[...] = _head_norm_gate(y, msq, SSD_P, z_ref[...], mixw_ref[...])


def _gdn_decode_step(qkv_ref, sm_ref, z_ref, cst_ref, st_ref, cw_ref, gb_ref, galog_ref, mixw_ref, en_ref,
                     y_ref, cst_out_ref, st_out_ref):
    qkv = _conv_step(qkv_ref[...], cst_ref, cst_out_ref, cw_ref, None)
    sm = sm_ref[...]
    beta = _sigmoid(sm)
    eg = jnp.exp(-jnp.exp(galog_ref[...]) * _softplus(sm + gb_ref[...]))
    eg_n = _sel_right(eg, en_ref[...])

    qs, ks = [], []
    for h in range(GDN_HEADS):
        q = qkv[:, h * GDN_D:(h + 1) * GDN_D]
        k = qkv[:, GDN_W + h * GDN_D:GDN_W + (h + 1) * GDN_D]
        qs.append(q * lax.rsqrt(jnp.sum(q * q, axis=-1, keepdims=True) + EPS) * (GDN_D ** -0.5))
        ks.append(k * lax.rsqrt(jnp.sum(k * k, axis=-1, keepdims=True) + EPS))
    k_t = _rows_to_columns(jnp.concatenate(ks, axis=1))

    heads, rows = range(GDN_HEADS), range(sm.shape[0])
    n = len(rows)
    hs = [slice(h * GDN_D, (h + 1) * GDN_D) for h in heads]
    kq_b = [jnp.concatenate([ks[h], qs[h]], axis=0).astype(BF16) for h in heads]
    prod = {(h, i): _dot(kq_b[h], st_ref[i, h].astype(BF16)) for h in heads for i in rows}
    k_s = [_pick_rows([prod[h, i][:n] for i in rows]) for h in heads]
    q_s = [_pick_rows([prod[h, i][n:] for i in rows]) for h in heads]
    eg_h = [eg_n[:, hs[h]] for h in heads]
    v = [qkv[:, 2 * GDN_W + h * GDN_D:2 * GDN_W + (h + 1) * GDN_D] for h in heads]
    v_new = [beta[:, SM_B + h:SM_B + h + 1] * (v[h] - eg_h[h] * k_s[h]) for h in heads]
    o = [eg_h[h] * q_s[h] + jnp.sum(qs[h] * ks[h], axis=-1, keepdims=True) * v_new[h] for h in heads]
    for h in heads:
        for i in rows:
            st_out_ref[i, h] = st_ref[i, h] * eg_h[h][i:i + 1, :] + k_t[hs[h], i:i + 1] * v_new[h][i:i + 1, :]
    msq = [jnp.sum(x * x, axis=-1, keepdims=True) for x in o]
    for h in heads:
        y_ref[:, hs[h]] = _head_norm_gate(o[h], msq[h], GDN_D, z_ref[:, hs[h]], mixw_ref[:, hs[h]])


def _mem_decode_step(q_ref, k_ref, v_ref, z_ref, mixw_ref, y_ref):
    heads, rows = range(MEM_HEADS), range(q_ref.shape[0])
    hs = [slice(h * MEM_D, (h + 1) * MEM_D) for h in heads]
    win = [pl.ds(h, MEM_TOKENS, stride=MEM_HEADS) for h in heads]
    q = [q_ref[:, hs[h]].astype(BF16) for h in heads]
    s = [_pick_rows([_dot_nt(q[h], k_ref[i, win[h], :].astype(BF16)) for i in rows]) * (MEM_D ** -0.5) for h in heads]
    e = [jnp.exp(x - jnp.max(x, axis=-1, keepdims=True)) for x in s]
    p = [(x / jnp.sum(x, axis=-1, keepdims=True)).astype(BF16) for x in e]
    o = [_pick_rows([_dot(p[h], v_ref[i, win[h], :].astype(BF16)) for i in rows]) for h in heads]
    msq = [jnp.sum(x * x, axis=-1, keepdims=True) for x in o]
    for h in heads:
        y_ref[:, hs[h]] = _head_norm_gate(o[h], msq[h], MEM_D, z_ref[:, hs[h]], mixw_ref[:, hs[h]])


DEC_ROWS = SUBLANES


def _ssd_decode_kernel(proj_ref, sm_ref, cst_ref, st_ref, cw_ref, cb_ref, dtb_ref, alog_ref, dexp_ref, mixw_ref, e_ref, en_ref,
                       y_ref, cst_out_ref, st_out_ref):
    _ssd_decode_step(proj_ref.at[:, COL_XBC:COL_XBC + SSD_CONV], sm_ref, proj_ref.at[:, COL_Z:COL_Z + SSD_W], cst_ref, st_ref,
                     cw_ref, cb_ref, dtb_ref, alog_ref, dexp_ref, mixw_ref.at[:, :SSD_W], e_ref, en_ref,
                     y_ref, cst_out_ref, st_out_ref)


def _gdn_decode_kernel(proj_ref, sm_ref, cst_ref, st_ref, cw_ref, gb_ref, galog_ref, mixw_ref, en_ref,
                       y_ref, cst_out_ref, st_out_ref):
    _gdn_decode_step(proj_ref.at[:, COL_QKV:COL_QKV + GDN_CONV], sm_ref, proj_ref.at[:, COL_Z + SSD_W:COL_Z + SSD_W + GDN_W],
                     cst_ref, st_ref, cw_ref, gb_ref, galog_ref, mixw_ref.at[:, SSD_W:SSD_W + GDN_W], en_ref,
                     y_ref, cst_out_ref, st_out_ref)


def _ssd_mem_decode_kernel(proj_ref, sm_ref, cst_ref, st_ref, k_ref, v_ref, cw_ref, cb_ref, dtb_ref, alog_ref, dexp_ref,
                           mixw_ref, e_ref, en_ref, y_ssd_ref, y_mem_ref, cst_out_ref, st_out_ref):
    _mem_decode_step(proj_ref.at[:, COL_QMEM:COL_QMEM + MEM_W], k_ref, v_ref, proj_ref.at[:, N_MAIN - MEM_W:N_MAIN],
                     mixw_ref.at[:, SSD_W + GDN_W:], y_mem_ref)
    _ssd_decode_kernel(proj_ref, sm_ref, cst_ref, st_ref, cw_ref, cb_ref, dtb_ref, alog_ref, dexp_ref, mixw_ref, e_ref, en_ref,
                       y_ssd_ref, cst_out_ref, st_out_ref)


def _decode_rider(kernel, per_row_in, consts, out_shape, steps, step_index, early=()):
    groups = per_row_in[0].shape[0] // DEC_ROWS
    assert steps % groups == 0
    per_group = steps // groups

    def group_spec(a, lead=0):
        axis = 1 if len(a.shape) == 3 and a.shape[0] == CONV_K - 1 else 0
        block = a.shape[:axis] + (DEC_ROWS,) + a.shape[axis + 1:]
        group_of = lambda *g: jnp.minimum((step_index(*g) + lead) // per_group, groups - 1)
        return pl.BlockSpec(block, lambda *g: (0,) * axis + (group_of(*g),) + (0,) * (len(block) - axis - 1))

    whole = lambda a: pl.BlockSpec(a.shape, lambda *g, nd=a.ndim: (0,) * nd)
    lead = lambda n: min(1, per_group - 1) if n in early else 0
    return dict(kernel=kernel, arrays=tuple(per_row_in) + tuple(consts), steps_per_group=per_group,
                in_specs=[group_spec(a, lead(n)) for n, a in enumerate(per_row_in)] + [whole(a) for a in consts],
                out_shape=list(out_shape), out_specs=[group_spec(a) for a in out_shape])


def _scan_with_riders_kernel(*refs, scan_kernel, n_scan_in, n_scan_out, riders):
    scan_in, refs = refs[:n_scan_in], refs[n_scan_in:]
    rider_in = []
    for _, n_in, _, _ in riders:
        rider_in.append(refs[:n_in])
        refs = refs[n_in:]
    scan_out, refs = refs[:n_scan_out], refs[n_scan_out:]
    rider_out = []
    for _, _, n_out, _ in riders:
        rider_out.append(refs[:n_out])
        refs = refs[n_out:]
    step = pl.program_id(0) * pl.num_programs(1) + pl.program_id(1)
    for (kernel, _, _, per_group), ins, outs in zip(riders, rider_in, rider_out):
        if per_group == 1:
            kernel(*ins, *outs)
    scan_kernel(*scan_in, *scan_out, *refs)
    for (kernel, _, _, per_group), ins, outs in zip(riders, rider_in, rider_out):
        if per_group > 1:
            pl.when(step % per_group == 0)(functools.partial(kernel, *ins, *outs))


def _scan_with_riders(name, grid, scan_kernel, arrays, in_specs, out_shape, out_specs, scratch_shapes, riders):
    body = functools.partial(
        _scan_with_riders_kernel, scan_kernel=scan_kernel, n_scan_in=len(arrays), n_scan_out=len(out_shape),
        riders=tuple((r["kernel"], len(r["arrays"]), len(r["out_shape"]), r["steps_per_group"]) for r in riders))
    outs = pl.pallas_call(
        body,
        grid=grid,
        in_specs=list(in_specs) + [s for r in riders for s in r["in_specs"]],
        out_specs=list(out_specs) + [s for r in riders for s in r["out_specs"]],
        out_shape=list(out_shape) + [s for r in riders for s in r["out_shape"]],
        scratch_shapes=scratch_shapes,
        compiler_params=_params("arbitrary", "arbitrary"),
        name=name,
    )(*arrays, *[a for r in riders for a in r["arrays"]])
    scan_res, outs = outs[:len(out_shape)], outs[len(out_shape):]
    rider_res = []
    for r in riders:
        rider_res.append(outs[:len(r["out_shape"])])
        outs = outs[len(r["out_shape"]):]
    return scan_res, rider_res


IN_DT = SSD_CONV
IN_QKV = IN_DT + SSD_HEADS
IN_B = IN_QKV + GDN_CONV
IN_QMEM = IN_B + 2 * GDN_HEADS
IN_COLS = IN_QMEM + MEM_W + MIX_W
PREP_COLS = 256


def _prep_w_in_kernel(w_ref, x_ref, nw_ref, main_ref, small_ref, o_ref, os_ref, h_ref):
    i = pl.program_id(0)
    nblk = h_ref.shape[0]

    @pl.when(i == 0)
    def _():
        h = _rmsnorm_bf16(x_ref[...], nw_ref[...])
        for kb in range(nblk):
            h_ref[kb] = h[:, kb * PREP_COLS:(kb + 1) * PREP_COLS]
        o_ref[...] = jnp.zeros_like(o_ref)
        os_ref[...] = jnp.zeros_like(os_ref)

    main_ref[COL_QKV:COL_QKV + GDN_CONV, :] = w_ref[IN_QKV:IN_B, :].astype(BF16)
    main_ref[COL_XBC:COL_XBC + SSD_CONV, :] = w_ref[:SSD_CONV, :].astype(BF16)
    main_ref[COL_QMEM:N_MAIN, :] = w_ref[IN_QMEM:IN_COLS, :].astype(BF16)
    small_ref[SM_DT:SM_B, :] = w_ref[IN_DT:IN_QKV, :].astype(BF16)
    small_ref[SM_B:SM_A + GDN_HEADS, :] = w_ref[IN_B:IN_QMEM, :].astype(BF16)
    small_ref[SM_A + GDN_HEADS:, :] = jnp.zeros((LANES - SM_A - GDN_HEADS, w_ref.shape[1]), BF16)

    hb = h_ref[i]
    o_ref[...] += _dot_nt(hb, main_ref[...])
    os_ref[...] += _dot_nt(hb, small_ref[...])


def _prep_w_in(w_t, x, nw):
    n, k = w_t.shape
    rows = x.shape[0]
    assert n == IN_COLS
    whole = lambda i: (0, 0)
    x_spec = pl.BlockSpec((rows, None, k), lambda i: (0, 0, 0)) if x.ndim == 3 else pl.BlockSpec((rows, k), whole)
    return pl.pallas_call(
        _prep_w_in_kernel,
        grid=(k // PREP_COLS,),
        in_specs=[pl.BlockSpec((IN_COLS, PREP_COLS), lambda i: (0, i)), x_spec, pl.BlockSpec((1, k), whole)],
        out_specs=[pl.BlockSpec((N_MAIN, PREP_COLS), lambda i: (0, i)), pl.BlockSpec((LANES, PREP_COLS), lambda i: (0, i)),
                   pl.BlockSpec((rows, N_MAIN), whole), pl.BlockSpec((rows, LANES), whole)],
        out_shape=[jax.ShapeDtypeStruct((N_MAIN, k), BF16), jax.ShapeDtypeStruct((LANES, k), BF16),
                   jax.ShapeDtypeStruct((rows, N_MAIN), F32), jax.ShapeDtypeStruct((rows, LANES), F32)],
        scratch_shapes=[pltpu.VMEM((k // PREP_COLS, rows, PREP_COLS), BF16)],
        compiler_params=_params("arbitrary"),
        name="prep_w_in",
    )(w_t, x, nw)


def _head_expander(heads, first_lane, width):
    m = np.zeros((LANES, heads * width), np.float32)
    for h in range(heads):
        m[first_lane + h, h * width:(h + 1) * width] = 1.0
    return jnp.asarray(m, BF16)


def _lane_row(vec, first_lane):
    return jnp.zeros((1, LANES), F32).at[0, first_lane:first_lane + vec.shape[0]].set(vec.astype(F32))


def kernel(x_prompt, x_sample, mem_prompt, state_ssd_conv, state_ssd, state_gdn_conv, state_gdn, cache_mem_k, cache_mem_v, norm_w, w_in, ssd_conv_w, ssd_conv_b, ssd_dt_bias, ssd_A_log, ssd_D, gdn_conv_w, gdn_dt_bias, gdn_A_log, mem_norm_w, w_mem_kv, mix_norm_w, w_out, final_norm_w):
    bp, seq, d = x_prompt.shape
    bs = x_sample.shape[0]
    assert (d, seq % CHUNK, norm_w.shape[0]) == (D_MODEL, 0, 1)

    nw = norm_w[0][None, :]
    w_main, w_small, proj_s, small_s = _prep_w_in(w_in[0].T, x_sample, nw)
    mixw = mix_norm_w[0][None, :]
    fw = final_norm_w[None, :]
    ssd_dtb = _lane_row(ssd_dt_bias[0], SM_DT)
    ssd_alog = _lane_row(ssd_A_log[0], SM_DT)
    ssd_dexp = jnp.repeat(ssd_D[0].astype(F32), SSD_P)[None, :]
    gdn_b = _lane_row(gdn_dt_bias[0], SM_A)
    gdn_alog = _lane_row(gdn_A_log[0], SM_A)
    e_ssd = _head_expander(SSD_HEADS, SM_DT, SSD_P)
    e_ssd_n = _head_expander(SSD_HEADS, SM_DT, SSD_N)
    e_gdn_n = _head_expander(GDN_HEADS, SM_A, GDN_D)
    ssd_cw, ssd_cb, gdn_cw = ssd_conv_w[0], ssd_conv_b[0][None, :], gdn_conv_w[0]

    xp = x_prompt.reshape(bp * seq, d)
    proj_p, small_p, wo, w_kv = _norm_matmul(
        xp, nw, w_main, w_small, _row_tile(bp * seq, PROJ_ROWS), SSD_CONV, casts=(w_out[0], w_mem_kv[0]))
    kv, k_prompt, v_prompt = _mem_kv(mem_prompt.reshape(bp * MEM_TOKENS, d), mem_norm_w[0][None, :], w_kv, MEM_TOKENS)
    scan_rows = SCAN_ROWS if seq % SCAN_ROWS == 0 else CHUNK
    scan_steps = bp * seq // scan_rows
    step_index = lambda b, c: b * (seq // scan_rows) + c

    ssd_cst, gdn_cst = jnp.swapaxes(state_ssd_conv[0], 0, 1), jnp.swapaxes(state_gdn_conv[0], 0, 1)
    mem_k = cache_mem_k.reshape(bs, MEM_TOKENS * MEM_HEADS, MEM_D)
    mem_v = cache_mem_v.reshape(bs, MEM_TOKENS * MEM_HEADS, MEM_D)
    row_out = lambda width: jax.ShapeDtypeStruct((bs, width), F32)
    like = lambda a: jax.ShapeDtypeStruct(a.shape, F32)
    ssd_rider = _decode_rider(
        _ssd_mem_decode_kernel, (proj_s, small_s, ssd_cst, state_ssd[0], mem_k, mem_v),
        (ssd_cw, ssd_cb, ssd_dtb, ssd_alog, ssd_dexp, mixw, e_ssd, e_ssd_n),
        (row_out(SSD_W), row_out(MEM_W), like(ssd_cst), like(state_ssd[0])), scan_steps, step_index, early=(4, 5))
    out_rows = _row_tile(bp * seq, OUT_ROWS)
    gdn_rider = _decode_rider(
        _gdn_decode_kernel, (proj_s, small_s, gdn_cst, state_gdn[0]), (gdn_cw, gdn_b, gdn_alog, mixw, e_gdn_n),
        (row_out(GDN_W), like(gdn_cst), like(state_gdn[0])), bp * seq // out_rows, lambda i, _: i)

    (y_ssd, tail_ssd, p_ssd), ((ys_ssd, ys_mem, s_ssd_conv, s_ssd),) = _ssd_prompt(
        proj_p, small_p, bp, ssd_cw, ssd_cb, ssd_dtb, ssd_alog, ssd_dexp, mixw, e_ssd, scan_rows, (ssd_rider,))
    (y_gdn, tail_gdn, p_gdn), _ = _gdn_prompt(proj_p, small_p, bp, gdn_cw, gdn_b, gdn_alog, mixw, scan_rows, ())
    y_mem = _mem_prompt(proj_p, kv, bp, mixw, _row_tile(seq, MEM_Q_ROWS))
    y_prompt, ((ys_gdn, s_gdn_conv, s_gdn),) = _out_proj(y_ssd, y_gdn, y_mem, wo, xp, fw, out_rows, (gdn_rider,))
    y_prompt = y_prompt.reshape(bp, seq, d)
    y_sample = _out_proj(ys_ssd, ys_gdn, ys_mem, wo, x_sample, fw, bs)[0]

    keep = CONV_K - 1
    mem_shape = (1, bp, MEM_TOKENS, MEM_HEADS, MEM_D)
    return (
        y_prompt, y_sample,
        tail_ssd[None, :, SUBLANES - keep:, :], p_ssd[None],
        tail_gdn[None, :, SUBLANES - keep:, :], p_gdn[None],
        k_prompt.reshape(mem_shape), v_prompt.reshape(mem_shape),
        jnp.swapaxes(s_ssd_conv, 0, 1)[None], s_ssd[None],
        jnp.swapaxes(s_gdn_conv, 0, 1)[None], s_gdn[None],
    )
```
